```python
import math
import jax
import jax.numpy as jnp
from jax import lax
import numpy as np

D_MODEL = 1024
BATCH = 16
SEQ = 2048
DEPTH = 2

CTX_LEN = 256
GRID_W = 64
HEAD_DIM = 64
N_GROUPS = 4
GROUP_W = D_MODEL // N_GROUPS
MIX_W = N_GROUPS * GROUP_W
HEADS_PER_GROUP = GROUP_W // HEAD_DIM
CHUNK = 128
DIFF_D = HEAD_DIM // 2
NA_ROWS = 8
NA_COLS = 16
NA_QCOLS = 16
NA_KCOLS = 32
N_EXPERTS = 16
EXPERT_FF = D_MODEL
CAPACITY_FACTOR = 2
ROPE_BASE = 10000.0
EPS = 1e-6
DIFF_Q_BLOCK = 128
A_END = 2 * GROUP_W
B_END = A_END + GROUP_W
C_END = B_END + 3 * GROUP_W
IN_W = C_END + 3 * GROUP_W

kernel_name = "hybrid_diffusion_parallel_mixers_ec_moe"


def rms_norm(x, g=None, eps=EPS):
    xf = x.astype(jnp.float32)
    y = xf * lax.rsqrt(jnp.mean(xf * xf, axis=-1, keepdims=True) + eps)
    if g is not None:
        y = y * g.astype(jnp.float32)
    return y.astype(x.dtype)


def axial_rope_tables(n_tokens, dim):
    half = dim // 2
    inv = 1.0 / (ROPE_BASE ** (jnp.arange(0, half, 2, dtype=jnp.float32) / half))
    t = jnp.arange(n_tokens)
    row = (t // GRID_W).astype(jnp.float32)[:, None] * inv
    col = (t % GRID_W).astype(jnp.float32)[:, None] * inv
    return (jnp.cos(row), jnp.sin(row), jnp.cos(col), jnp.sin(col))


def _rotate(x, cos, sin):
    x1, x2 = jnp.split(x, 2, axis=-1)
    return jnp.concatenate([x1 * cos - x2 * sin, x2 * cos + x1 * sin], axis=-1)


def apply_axial_rope(x, rope):
    n = x.shape[1]
    shp = (1, n) + (1,) * (x.ndim - 3) + (-1,)
    cr, sr, cc, sc = [t.reshape(shp).astype(x.dtype) for t in rope]
    xr, xc = jnp.split(x, 2, axis=-1)
    return jnp.concatenate([_rotate(xr, cr, sr), _rotate(xc, cc, sc)], axis=-1)


def chunk_spatial_gating(p, w_s, b_s):
    bn, n, _ = p.shape
    z = jax.nn.gelu(p)
    u, v = jnp.split(z, 2, axis=-1)
    v = rms_norm(v.reshape(bn, n // CHUNK, CHUNK, HEADS_PER_GROUP, HEAD_DIM))
    sv = jnp.einsum('hpq,bnqhc->bnphc', w_s, v) + b_s.T[None, None, :, :, None]
    return u * sv.reshape(bn, n, GROUP_W)


def fourier_mix(p):
    bn, n, _ = p.shape
    z = p.astype(jnp.float32).reshape(bn, n, HEADS_PER_GROUP, HEAD_DIM)
    f = jnp.fft.fft2(z, axes=(1, 3), norm="ortho").real
    return f.reshape(bn, n, GROUP_W).astype(p.dtype)


def lambda_value(lam, lam_init):
    lf = lam.astype(jnp.float32)
    return jnp.exp(jnp.sum(lf[0] * lf[1])) - jnp.exp(jnp.sum(lf[2] * lf[3])) + lam_init


def diff_attn_core(q, k, v, lam):
    s = jnp.einsum('bqhmd,bkhmd->bhmqk', q, k).astype(jnp.float32) * (DIFF_D ** -0.5)
    a = jax.nn.softmax(s, axis=-1)
    w = a[:, :, 0] - lam * a[:, :, 1]
    return jnp.einsum('bhqk,bkhd->bqhd', w.astype(v.dtype), v)


def ctx_attention(q, k, v):
    s = jnp.einsum('bqhd,bkhd->bhqk', q, k).astype(jnp.float32) * (HEAD_DIM ** -0.5)
    p = jax.nn.softmax(s, axis=-1).astype(v.dtype)
    return jnp.einsum('bhqk,bkhd->bqhd', p, v)


def neighbourhood_attn_latent(q, k, v, kc, vc, rpb):
    bn, n, h, dh = q.shape
    rows = n // GRID_W
    kr = min(NA_ROWS, rows)
    n_cb = GRID_W // NA_QCOLS
    scale = HEAD_DIM ** -0.5
    qg = q.reshape(bn, rows, n_cb, NA_QCOLS, h, dh).transpose(1, 0, 2, 3, 4, 5)
    kg = k.reshape(bn, rows, GRID_W, h, dh)
    vg = v.reshape(bn, rows, GRID_W, h, dh)
    qcol = jnp.arange(GRID_W).reshape(n_cb, NA_QCOLS)
    kstart = jnp.clip(qcol[:, 0] - NA_COLS // 2, 0, GRID_W - NA_KCOLS)
    kcol = kstart[:, None] + jnp.arange(NA_KCOLS)
    wstart = jnp.clip(qcol - NA_COLS // 2, 0, GRID_W - NA_COLS)
    kc3 = kcol[:, None, :]
    valid = (kc3 >= wstart[..., None]) & (kc3 < wstart[..., None] + NA_COLS)
    coff = jnp.clip(kc3 - qcol[..., None], -(NA_COLS - 1), NA_COLS - 1) + NA_COLS - 1
    rpb_c = rpb[:, :, coff]
    n_loc = kr * NA_KCOLS

    def one_row(args):
        r, q_row = args
        rstart = jnp.clip(r - kr // 2, 0, rows - kr)
        k_rows = lax.dynamic_slice_in_dim(kg, rstart, kr, axis=1)
        v_rows = lax.dynamic_slice_in_dim(vg, rstart, kr, axis=1)
        k_blk = k_rows[:, :, kcol]
        v_blk = v_rows[:, :, kcol]
        roff = rstart + jnp.arange(kr) - r + NA_ROWS - 1
        bias = rpb_c[:, roff].transpose(0, 2, 3, 1, 4).astype(jnp.float32)
        s_loc = jnp.einsum('bjqhd,brjkhd->bhjqrk', q_row, k_blk).astype(jnp.float32) * scale + bias
        s_loc = jnp.where(valid[:, :, None, :], s_loc, -jnp.inf)
        s_loc = s_loc.reshape(bn, h, n_cb, NA_QCOLS, n_loc)
        s_ctx = jnp.einsum('bjqhd,bkhd->bhjqk', q_row, kc).astype(jnp.float32) * scale
        p = jax.nn.softmax(jnp.concatenate([s_loc, s_ctx], axis=-1), axis=-1).astype(v.dtype)
        p_loc = p[..., :n_loc].reshape(bn, h, n_cb, NA_QCOLS, kr, NA_KCOLS)
        p_ctx = p[..., n_loc:]
        o = (jnp.einsum('bhjqrk,brjkhd->bjqhd', p_loc, v_blk)
             + jnp.einsum('bhjqk,bkhd->bjqhd', p_ctx, vc))
        return o.reshape(bn, GRID_W, h * dh)

    out = lax.map(one_row, (jnp.arange(rows), qg))
    return out.transpose(1, 0, 2, 3).reshape(bn, n, h * dh)


def token_mixers(hx, hc, w_in, w_out, head_g, sgu_w, sgu_b, dqn, dkn, dlam,
                 nqn, nkn, rpb, rope, lam_init, ctx_out):
    bn, n, _ = hx.shape
    h = HEADS_PER_GROUP
    px = hx @ w_in
    pc = hc @ w_in
    lam = lambda_value(dlam, lam_init)
    head_scale = jnp.asarray(np.array([1.0] * (2 * h) + [1.0 - lam_init] * h + [1.0] * h,
                                      dtype=np.float32), dtype=hx.dtype)

    def groups(p):
        return p[..., :A_END], p[..., A_END:B_END], p[..., B_END:C_END], p[..., C_END:]

    def diff_qkv(p, use_rope):
        m = p.shape[1]
        q, k, v = jnp.split(p, 3, axis=-1)
        q = rms_norm(q.reshape(bn, m, h, 2, DIFF_D), dqn)
        k = rms_norm(k.reshape(bn, m, h, 2, DIFF_D), dkn)
        if use_rope:
            q = apply_axial_rope(q, rope)
            k = apply_axial_rope(k, rope)
        return q, k, v.reshape(bn, m, h, HEAD_DIM)

    def na_qkv(p):
        m = p.shape[1]
        q, k, v = jnp.split(p, 3, axis=-1)
        q = rms_norm(q.reshape(bn, m, h, HEAD_DIM), nqn)
        k = rms_norm(k.reshape(bn, m, h, HEAD_DIM), nkn)
        return q, k, v.reshape(bn, m, h, HEAD_DIM)

    def merge(ya, yb, yc, yd):
        y = jnp.concatenate([ya, yb, yc, yd], axis=-1)
        m = y.shape[1]
        y = rms_norm(y.reshape(bn, m, N_GROUPS * h, HEAD_DIM), head_g.reshape(N_GROUPS * h, HEAD_DIM))
        y = y * head_scale[:, None]
        return y.reshape(bn, m, MIX_W) @ w_out

    aX, bX, cX, dX = groups(px)
    aC, bC, cC, dC = groups(pc)

    cq_c, ck_c, cv_c = diff_qkv(cC, False)
    nq_c, nk_c, nv_c = na_qkv(dC)

    ya = chunk_spatial_gating(aX, sgu_w, sgu_b)
    yb = fourier_mix(bX)
    cq, ck, cv = diff_qkv(cX, True)
    k_all = jnp.concatenate([ck, ck_c], axis=1)
    v_all = jnp.concatenate([cv, cv_c], axis=1)
    qb = cq.reshape(bn, n // DIFF_Q_BLOCK, DIFF_Q_BLOCK, h, 2, DIFF_D).transpose(1, 0, 2, 3, 4, 5)
    yc = lax.map(lambda qblk: diff_attn_core(qblk, k_all, v_all, lam), qb)
    yc = yc.transpose(1, 0, 2, 3, 4).reshape(bn, n, GROUP_W)
    nq, nk, nv = na_qkv(dX)
    yd = neighbourhood_attn_latent(nq, nk, nv, nk_c, nv_c, rpb)
    y_lat = merge(ya, yb, yc, yd)

    if not ctx_out:
        return y_lat, None
    m = hc.shape[1]
    ya_c = chunk_spatial_gating(aC, sgu_w, sgu_b)
    yb_c = fourier_mix(bC)
    yc_c = diff_attn_core(cq_c, ck_c, cv_c, lam).reshape(bn, m, GROUP_W)
    yd_c = ctx_attention(nq_c, nk_c, nv_c).reshape(bn, m, GROUP_W)
    return y_lat, merge(ya_c, yb_c, yc_c, yd_c)


def expert_choice_ffn(x, router_w, wg, wu, wd):
    bn, n, _ = x.shape
    cap = CAPACITY_FACTOR * n // N_EXPERTS
    aff = jax.nn.softmax(jnp.einsum('bnd,de->bne', x, router_w).astype(jnp.float32), axis=-1)
    gate, idx = lax.top_k(aff.transpose(0, 2, 1), cap)
    bidx = jnp.arange(bn)[:, None, None]
    xg = x[bidx, idx]
    hid = jax.nn.silu(jnp.einsum('becd,edf->becf', xg, wg)) * jnp.einsum('becd,edf->becf', xg, wu)
    o = jnp.einsum('becf,efd->becd', hid, wd) * gate[..., None].astype(x.dtype)
    return jnp.zeros_like(x).at[bidx, idx].add(o)


def setup_inputs(seed: int = 0) -> dict:
    key = jax.random.key(seed)
    ks = jax.random.split(key, 24)
    f32 = jnp.float32
    L, D, H = DEPTH, D_MODEL, HEADS_PER_GROUP

    def nrm(k, shape, s):
        return jax.random.normal(k, shape, f32) * s

    return {
        "x": nrm(ks[0], (BATCH, SEQ, D), 1.0),
        "c": nrm(ks[1], (BATCH, D), 1.0),
        "ctx": nrm(ks[2], (BATCH, CTX_LEN, D), 1.0),
        "c_ctx": nrm(ks[3], (D,), 1.0),
        "ada_w": nrm(ks[4], (L, D, 6 * D), 0.5 * D ** -0.5),
        "ada_b": nrm(ks[5], (L, 6 * D), 0.02),
        "norm1_g": 1.0 + nrm(ks[6], (L, D), 0.02),
        "norm2_g": 1.0 + nrm(ks[7], (L, D), 0.02),
        "w_in": nrm(ks[8], (L, D, IN_W), D ** -0.5),
        "w_out": nrm(ks[9], (L, MIX_W, D), MIX_W ** -0.5),
        "head_out_g": 1.0 + nrm(ks[10], (L, MIX_W), 0.02),
        "sgu_w": nrm(ks[11], (L, H, CHUNK, CHUNK), CHUNK ** -0.5),
        "sgu_b": 1.0 + nrm(ks[12], (L, H, CHUNK), 0.02),
        "diff_qn_g": 1.0 + nrm(ks[13], (L, DIFF_D), 0.02),
        "diff_kn_g": 1.0 + nrm(ks[14], (L, DIFF_D), 0.02),
        "diff_lambda": nrm(ks[15], (L, 4, DIFF_D), 0.1),
        "na_qn_g": 1.0 + nrm(ks[16], (L, HEAD_DIM), 0.02),
        "na_kn_g": 1.0 + nrm(ks[17], (L, HEAD_DIM), 0.02),
        "na_rpb": nrm(ks[18], (L, H, 2 * NA_ROWS - 1, 2 * NA_COLS - 1), 0.1),
        "router_w": nrm(ks[19], (L, D, N_EXPERTS), D ** -0.5),
        "exp_w_gate": nrm(ks[20], (L, N_EXPERTS, D, EXPERT_FF), D ** -0.5),
        "exp_w_up": nrm(ks[21], (L, N_EXPERTS, D, EXPERT_FF), D ** -0.5),
        "exp_w_down": nrm(ks[22], (L, N_EXPERTS, EXPERT_FF, D), EXPERT_FF ** -0.5),
    }


def reference(x, c, ctx, c_ctx, ada_w, ada_b, norm1_g, norm2_g, w_in, w_out, head_out_g,
              sgu_w, sgu_b, diff_qn_g, diff_kn_g, diff_lambda, na_qn_g, na_kn_g, na_rpb,
              router_w, exp_w_gate, exp_w_up, exp_w_down):
    n = x.shape[1]
    rope = axial_rope_tables(n, DIFF_D)
    s_c = jax.nn.silu(c)
    s_cc = jax.nn.silu(c_ctx)
    xc = ctx
    for l in range(DEPTH):
        last = l == DEPTH - 1
        lam_init = 0.8 - 0.6 * math.exp(-0.3 * l)
        mod = (s_c @ ada_w[l] + ada_b[l])[:, None, :]
        mod_c = (s_cc @ ada_w[l] + ada_b[l])[None, None, :]
        sh1, sc1, g1, sh2, sc2, g2 = jnp.split(mod, 6, axis=-1)
        csh1, csc1, cg1, csh2, csc2, cg2 = jnp.split(mod_c, 6, axis=-1)
        hx = rms_norm(x, norm1_g[l]) * (1.0 + sc1) + sh1
        hc = rms_norm(xc, norm1_g[l]) * (1.0 + csc1) + csh1
        y_lat, y_ctx = token_mixers(hx, hc, w_in[l], w_out[l], head_out_g[l], sgu_w[l], sgu_b[l],
                                    diff_qn_g[l], diff_kn_g[l], diff_lambda[l],
                                    na_qn_g[l], na_kn_g[l], na_rpb[l], rope, lam_init,
                                    not last)
        x = x + g1 * y_lat
        x = x + g2 * expert_choice_ffn(rms_norm(x, norm2_g[l]) * (1.0 + sc2) + sh2,
                                       router_w[l], exp_w_gate[l], exp_w_up[l], exp_w_down[l])
        if not last:
            xc = xc + cg1 * y_ctx
            xc = xc + cg2 * expert_choice_ffn(rms_norm(xc, norm2_g[l]) * (1.0 + csc2) + csh2,
                                              router_w[l], exp_w_gate[l], exp_w_up[l], exp_w_down[l])
    return x
```

```python
import functools
import math

import numpy as np
import jax
import jax.numpy as jnp
from jax import lax
from jax.experimental import pallas as pl
from jax.experimental.pallas import tpu as pltpu

F32 = jnp.float32
BF16 = jnp.bfloat16
I32 = jnp.int32

D_MODEL = 1024
DEPTH = 2
GRID_W = 64
HEAD_DIM = 64
LOG2_HEAD_DIM = 6
GROUP_W = 256
HEADS = GROUP_W // HEAD_DIM
CHUNK = 128
DIFF_D = HEAD_DIM // 2
NA_ROWS = 8
NA_COLS = 16
N_EXPERTS = 16
CAPACITY_FACTOR = 2
ROPE_BASE = 10000.0
EPS = 1e-6
IN_W = 9 * GROUP_W
LOG2E = 1.4426950408889634

VMEM_LIMIT_BYTES = 56 * 1024 * 1024
NA_QROWS = 2
NA_WIN_ROWS = NA_ROWS + 2
NA_WIN = NA_WIN_ROWS * GRID_W
NEG_INF = float("-inf")


def _dot(a, b):
    return jnp.dot(a, b, preferred_element_type=F32)


def _params(*sem):
    return pltpu.CompilerParams(dimension_semantics=sem, vmem_limit_bytes=VMEM_LIMIT_BYTES)


def _full(shape):
    nd = len(shape)
    return pl.BlockSpec(shape, lambda *_: (0,) * nd)


def _seg_rms(x, seg, width):
    x2 = x * x
    hi = x2.astype(BF16)
    lo = (x2 - hi.astype(F32)).astype(BF16)
    ss = _dot(hi, seg) + _dot(lo, seg)
    return x * lax.rsqrt(ss * (1.0 / width) + EPS)


def _mod_kernel(c_ref, w_ref, b_ref, o_ref):
    s = jax.nn.silu(c_ref[...]).astype(BF16)
    o_ref[0] = _dot(s, w_ref[0].astype(BF16)) + b_ref[0]


def _modulation(c_rows, ada_w, ada_b):
    depth, d, w6 = ada_w.shape
    r = c_rows.shape[0]
    tn = 1024
    return pl.pallas_call(
        _mod_kernel,
        out_shape=jax.ShapeDtypeStruct((depth, r, w6), F32),
        grid=(depth, w6 // tn),
        in_specs=[
            pl.BlockSpec((r, d), lambda l, j: (0, 0)),
            pl.BlockSpec((1, d, tn), lambda l, j: (l, 0, j)),
            pl.BlockSpec((1, 1, tn), lambda l, j: (l, 0, j)),
        ],
        out_specs=pl.BlockSpec((1, r, tn), lambda l, j: (l, 0, j)),
        compiler_params=_params("arbitrary", "arbitrary"),
        name="modulation",
    )(c_rows, ada_w, ada_b.reshape(depth, 1, w6))


def _rope(x, c, s, lane):
    fwd = pltpu.roll(x, GROUP_W - 8, 1)
    bwd = pltpu.roll(x, 8, 1)
    partner = jnp.where((lane & 8) == 0, fwd, bwd)
    return x * c + partner * s


def _in_kernel(*refs, tm, use_rope, c_scale, d_scale):
    (x_ref, mod_ref, g1_ref, w_ref, seg32_ref, seg64_ref, cc_ref, ss_ref, sguw_ref, sgub_ref, vec_ref) = refs[:11]
    rest = refs[11:]
    if use_rope:
        ropec_ref, ropes_ref = rest[:2]
        rest = rest[2:]
    ya_ref, zc_ref, zs_ref, qc_ref, kct_ref, vc_ref, qd_ref, kdt_ref, vd_ref = rest

    x = x_ref[0]
    mod = mod_ref[0]
    sh = mod[:, 0:D_MODEL]
    sc = mod[:, D_MODEL:2 * D_MODEL]
    ms = jnp.mean(x * x, axis=-1, keepdims=True)
    h = x * lax.rsqrt(ms + EPS) * g1_ref[...]
    h = (h * (1.0 + sc) + sh).astype(BF16)
    lane = lax.broadcasted_iota(I32, (1, GROUP_W), 1)
    head = lane >> LOG2_HEAD_DIM
    seg32 = seg32_ref[...]
    seg64 = seg64_ref[...]
    vec = vec_ref[...]

    z = jax.nn.gelu(_dot(h, w_ref[:, 0:2 * GROUP_W]))
    u = z[:, 0:GROUP_W]
    vn = _seg_rms(z[:, GROUP_W:2 * GROUP_W], seg64, HEAD_DIM).astype(BF16)
    rows = []
    for c in range(tm // CHUNK):
        vch = vn[c * CHUNK:(c + 1) * CHUNK]
        sv = jnp.zeros((CHUNK, GROUP_W), F32)
        for hh in range(HEADS):
            sv = jnp.where(head == hh, _dot(sguw_ref[hh], vch), sv)
        rows.append(sv + sgub_ref[...])
    ya = u * jnp.concatenate(rows, axis=0)
    ya_ref[0] = (_seg_rms(ya, seg64, HEAD_DIM) * vec[4:5]).astype(BF16)

    zb = _dot(h, w_ref[:, 2 * GROUP_W:3 * GROUP_W]).astype(BF16)
    zc_ref[0] = _dot(zb, cc_ref[...]).astype(BF16)
    zs_ref[0] = _dot(zb, ss_ref[...]).astype(BF16)

    pc = _dot(h, w_ref[:, 3 * GROUP_W:6 * GROUP_W])
    q = _seg_rms(pc[:, 0:GROUP_W], seg32, DIFF_D) * vec[0:1]
    k = _seg_rms(pc[:, GROUP_W:2 * GROUP_W], seg32, DIFF_D) * vec[1:2]
    if use_rope:
        rc = ropec_ref[...]
        rs = ropes_ref[...]
        q = _rope(q, rc, rs, lane)
        k = _rope(k, rc, rs, lane)
    qc_ref[0] = (q * c_scale).astype(BF16)
    kct_ref[0] = k.T.astype(BF16)
    vc_ref[0] = pc[:, 2 * GROUP_W:3 * GROUP_W].astype(BF16)

    pd = _dot(h, w_ref[:, 6 * GROUP_W:9 * GROUP_W])
    qd = _seg_rms(pd[:, 0:GROUP_W], seg64, HEAD_DIM) * vec[2:3]
    kd = _seg_rms(pd[:, GROUP_W:2 * GROUP_W], seg64, HEAD_DIM) * vec[3:4]
    qd_ref[0] = (qd * d_scale).astype(BF16)
    kdt_ref[0] = kd.T.astype(BF16)
    vd_ref[0] = pd[:, 2 * GROUP_W:3 * GROUP_W].astype(BF16)


def _in_proj(x, mod, g1, w_in, consts, sguw, sgub, vec, rope, tm):
    b, n, d = x.shape
    use_rope = rope is not None
    tok = pl.BlockSpec((1, tm, GROUP_W), lambda i, t: (i, t, 0))
    tok_t = pl.BlockSpec((1, GROUP_W, tm), lambda i, t: (i, 0, t))
    in_specs = [
        pl.BlockSpec((1, tm, d), lambda i, t: (i, t, 0)),
        pl.BlockSpec((1, 1, 6 * d), lambda i, t: (i, 0, 0)),
        _full((1, d)),
        _full((d, IN_W)),
        _full((GROUP_W, GROUP_W)), _full((GROUP_W, GROUP_W)), _full((GROUP_W, GROUP_W)), _full((GROUP_W, GROUP_W)),
        _full((HEADS, CHUNK, CHUNK)),
        _full((CHUNK, GROUP_W)),
        _full((8, GROUP_W)),
    ]
    args = [x, mod, g1, w_in, consts["seg32"], consts["seg64"], consts["cc"], consts["ss"], sguw, sgub, vec]
    if use_rope:
        in_specs += [pl.BlockSpec((tm, GROUP_W), lambda i, t: (t, 0))] * 2
        args += list(rope)
    sd = jax.ShapeDtypeStruct((b, n, GROUP_W), BF16)
    sdt = jax.ShapeDtypeStruct((b, GROUP_W, n), BF16)
    kern = functools.partial(
        _in_kernel, tm=tm, use_rope=use_rope,
        c_scale=(DIFF_D ** -0.5) * LOG2E, d_scale=(HEAD_DIM ** -0.5) * LOG2E)
    return pl.pallas_call(
        kern,
        out_shape=(sd, sd, sd, sd, sdt, sd, sd, sdt, sd),
        grid=(b, n // tm),
        in_specs=in_specs,
        out_specs=(tok, tok, tok, tok, tok_t, tok, tok, tok_t, tok),
        compiler_params=_params("arbitrary", "arbitrary"),
        name="in_proj",
    )(*args)


def _fourier_kernel(cn_ref, sn_ref, zc_ref, zs_ref, seg64_ref, vec_ref, o_ref, *, norm):
    y = (_dot(cn_ref[...], zc_ref[0]) - _dot(sn_ref[...], zs_ref[0])) * norm
    o_ref[0] = (_seg_rms(y, seg64_ref[...], HEAD_DIM) * vec_ref[5:6]).astype(BF16)


def _fourier(zc, zs, cn, sn, seg64, vec, tn):
    b, n, _ = zc.shape
    kern = functools.partial(_fourier_kernel, norm=1.0 / math.sqrt(n * HEAD_DIM))
    return pl.pallas_call(
        kern,
        out_shape=jax.ShapeDtypeStruct((b, n, GROUP_W), BF16),
        grid=(n // tn, b),
        in_specs=[
            pl.BlockSpec((tn, n), lambda t, i: (t, 0)),
            pl.BlockSpec((tn, n), lambda t, i: (t, 0)),
            pl.BlockSpec((1, n, GROUP_W), lambda t, i: (i, 0, 0)),
            pl.BlockSpec((1, n, GROUP_W), lambda t, i: (i, 0, 0)),
            _full((GROUP_W, GROUP_W)),
            _full((8, GROUP_W)),
        ],
        out_specs=pl.BlockSpec((1, tn, GROUP_W), lambda t, i: (i, t, 0)),
        compiler_params=_params("arbitrary", "arbitrary"),
        name="fourier",
    )(cn, sn, zc, zs, seg64, vec)


def _attn_kernel(*refs, n_src, diff, lam_init, chunk, tq, vec_row, out_scale):
    q_ref = refs[0]
    srcs = [(refs[1 + 2 * i], refs[2 + 2 * i]) for i in range(n_src)]
    rest = refs[1 + 2 * n_src:]
    if diff:
        lam_ref = rest[0]
        rest = rest[1:]
    seg64_ref, vec_ref, o_ref = rest

    q = q_ref[0]
    lane = lax.broadcasted_iota(I32, (1, GROUP_W), 1)
    shift = LOG2_HEAD_DIM - 1 if diff else LOG2_HEAD_DIM
    n_maps = GROUP_W >> shift
    if diff:
        lf = lam_ref[...]
        lam = (jnp.exp(jnp.sum(lf[0:1] * lf[1:2], axis=-1, keepdims=True))
               - jnp.exp(jnp.sum(lf[2:3] * lf[3:4], axis=-1, keepdims=True)) + lam_init)

    def body(i, out):
        qm = jnp.where((lane >> shift) == i, q, jnp.zeros_like(q))
        m = jnp.full((tq, 1), NEG_INF, F32)
        l = jnp.zeros((tq, 1), F32)
        acc = jnp.zeros((tq, GROUP_W), F32)
        for kt_ref, v_ref in srcs:
            nk = kt_ref.shape[2]
            for c0 in range(0, nk, chunk):
                ck = min(chunk, nk - c0)
                s = _dot(qm, kt_ref[0, :, c0:c0 + ck])
                m_new = jnp.maximum(m, jnp.max(s, axis=-1, keepdims=True))
                alpha = jnp.exp2(m - m_new)
                p = jnp.exp2(s - m_new)
                l = alpha * l + jnp.sum(p, axis=-1, keepdims=True)
                acc = alpha * acc + _dot(p.astype(BF16), v_ref[0, c0:c0 + ck, :])
                m = m_new
        o = acc * (1.0 / l)
        if diff:
            hd = i >> 1
            o = o * jnp.where((i & 1) == 0, 1.0, -lam)
        else:
            hd = i
        return jnp.where((lane >> LOG2_HEAD_DIM) == hd, out + o, out)

    out = lax.fori_loop(0, n_maps, body, jnp.zeros((tq, GROUP_W), F32))
    y = _seg_rms(out, seg64_ref[...], HEAD_DIM) * vec_ref[vec_row:vec_row + 1]
    o_ref[0] = (y * out_scale).astype(BF16)


def _attention(q, srcs, lam, lam_init, seg64, vec, vec_row, out_scale, tq, chunk=512):
    b, nq, _ = q.shape
    diff = lam is not None
    in_specs = [pl.BlockSpec((1, tq, GROUP_W), lambda i, t: (i, t, 0))]
    args = [q]
    for kt, v in srcs:
        nk = v.shape[1]
        in_specs += [pl.BlockSpec((1, GROUP_W, nk), lambda i, t: (i, 0, 0)),
                     pl.BlockSpec((1, nk, GROUP_W), lambda i, t: (i, 0, 0))]
        args += [kt, v]
    if diff:
        in_specs.append(_full((4, DIFF_D)))
        args.append(lam)
    in_specs += [_full((GROUP_W, GROUP_W)), _full((8, GROUP_W))]
    args += [seg64, vec]
    kern = functools.partial(_attn_kernel, n_src=len(srcs), diff=diff, lam_init=lam_init, chunk=chunk, tq=tq,
                             vec_row=vec_row, out_scale=out_scale)
    return pl.pallas_call(
        kern,
        out_shape=jax.ShapeDtypeStruct((b, nq, GROUP_W), BF16),
        grid=(b, nq // tq),
        in_specs=in_specs,
        out_specs=pl.BlockSpec((1, tq, GROUP_W), lambda i, t: (i, t, 0)),
        compiler_params=_params("arbitrary", "arbitrary"),
        name="diff_attention" if diff else "ctx_attention",
    )(*args)


def _na_kernel(q_ref, kt_ref, v_ref, ktc_ref, vc_ref, tab_ref, seg64_ref, vec_ref, o_ref, *, n_rows):
    t = pl.program_id(1)
    n_steps = n_rows // NA_QROWS
    ws = jnp.clip(NA_QROWS * t - NA_ROWS // 2, 0, n_rows - NA_WIN_ROWS)
    k0 = pl.multiple_of(ws * GRID_W, 128)
    tid = jnp.where(t < 2, t, jnp.where(t < n_steps - 2, 2, t - (n_steps - 5)))
    tq = NA_QROWS * GRID_W

    q = q_ref[0]
    lane = lax.broadcasted_iota(I32, (1, GROUP_W), 1)
    head = lane >> LOG2_HEAD_DIM
    qs = jnp.concatenate([jnp.where(head == hh, q, jnp.zeros_like(q)) for hh in range(HEADS)], axis=0)
    s_loc = _dot(qs, kt_ref[0, :, pl.ds(k0, NA_WIN)]) + tab_ref[tid]
    s_ctx = _dot(qs, ktc_ref[0])
    m = jnp.maximum(jnp.max(s_loc, axis=-1, keepdims=True), jnp.max(s_ctx, axis=-1, keepdims=True))
    p_loc = jnp.exp2(s_loc - m)
    p_ctx = jnp.exp2(s_ctx - m)
    l = jnp.sum(p_loc, axis=-1, keepdims=True) + jnp.sum(p_ctx, axis=-1, keepdims=True)
    o = _dot(p_loc.astype(BF16), v_ref[0, pl.ds(k0, NA_WIN), :]) + _dot(p_ctx.astype(BF16), vc_ref[0])
    o = o * (1.0 / l)
    out = jnp.zeros((tq, GROUP_W), F32)
    for hh in range(HEADS):
        out = jnp.where(head == hh, o[hh * tq:(hh + 1) * tq], out)
    o_ref[0] = (_seg_rms(out, seg64_ref[...], HEAD_DIM) * vec_ref[7:8]).astype(BF16)


def _na_attention(q, kt, v, ktc, vc, table, seg64, vec):
    b, n, _ = q.shape
    nc = vc.shape[1]
    tq = NA_QROWS * GRID_W
    kern = functools.partial(_na_kernel, n_rows=n // GRID_W)
    return pl.pallas_call(
        kern,
        out_shape=jax.ShapeDtypeStruct((b, n, GROUP_W), BF16),
        grid=(b, n // tq),
        in_specs=[
            pl.BlockSpec((1, tq, GROUP_W), lambda i, t: (i, t, 0)),
            pl.BlockSpec((1, GROUP_W, n), lambda i, t: (i, 0, 0)),
            pl.BlockSpec((1, n, GROUP_W), lambda i, t: (i, 0, 0)),
            pl.BlockSpec((1, GROUP_W, nc), lambda i, t: (i, 0, 0)),
            pl.BlockSpec((1, nc, GROUP_W), lambda i, t: (i, 0, 0)),
            _full(table.shape),
            _full((GROUP_W, GROUP_W)),
            _full((8, GROUP_W)),
        ],
        out_specs=pl.BlockSpec((1, tq, GROUP_W), lambda i, t: (i, t, 0)),
        compiler_params=_params("arbitrary", "arbitrary"),
        name="neighbourhood_attention",
    )(q, kt, v, ktc, vc, table, seg64, vec)


def _out_kernel(ya_ref, yb_ref, yc_ref, yd_ref, w_ref, x_ref, mod_ref, o_ref):
    y = jnp.concatenate([ya_ref[0], yb_ref[0], yc_ref[0], yd_ref[0]], axis=-1)
    g = mod_ref[0][:, 2 * D_MODEL:3 * D_MODEL]
    o_ref[0] = x_ref[0] + g * _dot(y, w_ref[...])


def _out_proj(ya, yb, yc, yd, w_out, x, mod, tm):
    b, n, d = x.shape
    tok = pl.BlockSpec((1, tm, GROUP_W), lambda i, t: (i, t, 0))
    xs = pl.BlockSpec((1, tm, d), lambda i, t: (i, t, 0))
    return pl.pallas_call(
        _out_kernel,
        out_shape=jax.ShapeDtypeStruct((b, n, d), F32),
        grid=(b, n // tm),
        in_specs=[tok, tok, tok, tok, _full((4 * GROUP_W, d)), xs,
                  pl.BlockSpec((1, 1, 6 * d), lambda i, t: (i, 0, 0))],
        out_specs=xs,
        compiler_params=_params("arbitrary", "arbitrary"),
        name="out_proj",
    )(ya, yb, yc, yd, w_out, x, mod)


def _cumsum_excl(m, tri):
    n = m.shape[1]
    carry = jnp.zeros((m.shape[0], 1), F32)
    outs = []
    for j in range(n // GROUP_W):
        blk = m[:, j * GROUP_W:(j + 1) * GROUP_W]
        inc = _dot(blk.astype(BF16), tri)
        outs.append(inc - blk + carry)
        carry = carry + inc[:, GROUP_W - 1:GROUP_W]
    return jnp.concatenate(outs, axis=1)


def _route_kernel(x_ref, mod_ref, g2_ref, rwt_ref, tri_ref, xn_ref, pos_ref, gate_ref, lg_ref, *, n, cap, rows):
    mod = mod_ref[0]
    sh = mod[:, 3 * D_MODEL:4 * D_MODEL]
    sc = mod[:, 4 * D_MODEL:5 * D_MODEL]
    for c in range(n // rows):
        x = x_ref[0, c * rows:(c + 1) * rows, :]
        ms = jnp.mean(x * x, axis=-1, keepdims=True)
        h = x * lax.rsqrt(ms + EPS) * g2_ref[...]
        h = (h * (1.0 + sc) + sh).astype(BF16)
        xn_ref[0, c * rows:(c + 1) * rows, :] = h
        lg_ref[:, c * rows:(c + 1) * rows] = lax.dot_general(
            rwt_ref[...], h, (((1,), (1,)), ((), ())), preferred_element_type=F32)
    lg = lg_ref[...]
    e = jnp.exp(lg - jnp.max(lg, axis=0, keepdims=True))
    aff = e / jnp.sum(e, axis=0, keepdims=True)
    gate_ref[0] = aff

    bits = pltpu.bitcast(aff, I32)

    def bisect(i, lo):
        cand = lo | jnp.left_shift(jnp.int32(1), 30 - i)
        cnt = jnp.sum(jnp.where(bits >= cand, 1.0, 0.0), axis=1, keepdims=True)
        return jnp.where(cnt >= cap, cand, lo)

    thr = lax.fori_loop(0, 31, bisect, jnp.zeros((N_EXPERTS, 1), I32))
    gt = bits > thr
    eq = bits == thr
    need = cap - jnp.sum(jnp.where(gt, 1.0, 0.0), axis=1, keepdims=True)
    tri = tri_ref[...]
    rank_eq = _cumsum_excl(jnp.where(eq, 1.0, 0.0), tri)
    sel = jnp.where(gt, 1.0, jnp.where(eq, jnp.where(rank_eq < need, 1.0, 0.0), 0.0))
    pos = _cumsum_excl(sel, tri)
    pos_ref[0] = jnp.where(sel > 0.0, pos, -1.0).astype(I32)


def _route(x, mod, g2, rwt, tri, cap):
    b, n, d = x.shape
    rows = min(n, 512)
    kern = functools.partial(_route_kernel, n=n, cap=cap, rows=rows)
    return pl.pallas_call(
        kern,
        out_shape=(jax.ShapeDtypeStruct((b, n, d), BF16),
                   jax.ShapeDtypeStruct((b, N_EXPERTS, n), I32),
                   jax.ShapeDtypeStruct((b, N_EXPERTS, n), F32)),
        grid=(b,),
        in_specs=[
            pl.BlockSpec((1, n, d), lambda i: (i, 0, 0)),
            pl.BlockSpec((1, 1, 6 * d), lambda i: (i, 0, 0)),
            _full((1, d)),
            _full((N_EXPERTS, d)),
            _full((GROUP_W, GROUP_W)),
        ],
        out_specs=(pl.BlockSpec((1, n, d), lambda i: (i, 0, 0)),
                   pl.BlockSpec((1, N_EXPERTS, n), lambda i: (i, 0, 0)),
                   pl.BlockSpec((1, N_EXPERTS, n), lambda i: (i, 0, 0))),
        scratch_shapes=[pltpu.VMEM((N_EXPERTS, n), F32)],
        compiler_params=_params("arbitrary"),
        name="router",
    )(x, mod, g2, rwt, tri)


def _ffn_kernel(xn_ref, pos_ref, gate_ref, wg_ref, wu_ref, wd_ref, o_ref, *, bb, cap, n):
    slot = lax.broadcasted_iota(I32, (cap, n), 0)
    xs, gs = [], []
    for i in range(bb):
        hit = pos_ref[i, 0] == slot
        onehot = jnp.where(hit, 1.0, 0.0).astype(BF16)
        gs.append(jnp.sum(jnp.where(hit, gate_ref[i, 0], 0.0), axis=1, keepdims=True))
        xs.append(_dot(onehot, xn_ref[i]).astype(BF16))
    xg = xs[0] if bb == 1 else jnp.concatenate(xs, axis=0)
    g = gs[0] if bb == 1 else jnp.concatenate(gs, axis=0)
    hid = (jax.nn.silu(_dot(xg, wg_ref[0])) * _dot(xg, wu_ref[0])).astype(BF16)
    o = _dot(hid, wd_ref[0]) * g
    for i in range(bb):
        o_ref[i, 0] = o[i * cap:(i + 1) * cap].astype(BF16)


def _expert_ffn(xn, pos, gate, wg, wu, wd, cap, bb):
    b, n, d = xn.shape
    pos4 = pos.reshape(b, N_EXPERTS, 1, n)
    gate4 = gate.reshape(b, N_EXPERTS, 1, n)
    wspec = pl.BlockSpec((1, d, d), lambda e, j: (e, 0, 0))
    sel = pl.BlockSpec((bb, 1, 1, n), lambda e, j: (j, e, 0, 0))
    kern = functools.partial(_ffn_kernel, bb=bb, cap=cap, n=n)
    return pl.pallas_call(
        kern,
        out_shape=jax.ShapeDtypeStruct((b, N_EXPERTS, cap, d), BF16),
        grid=(N_EXPERTS, b // bb),
        in_specs=[pl.BlockSpec((bb, n, d), lambda e, j: (j, 0, 0)), sel, sel, wspec, wspec, wspec],
        out_specs=pl.BlockSpec((bb, 1, cap, d), lambda e, j: (j, e, 0, 0)),
        compiler_params=_params("arbitrary", "arbitrary"),
        name="expert_ffn",
    )(xn, pos4, gate4, wg, wu, wd)


def _scatter_kernel(x_ref, mod_ref, pt_ref, o_ref, out_ref, *, tn, cap):
    pos_t = pt_ref[0]
    if cap % 128 == 0:
        slot = lax.broadcasted_iota(I32, (tn, cap), 1)
        onehot = jnp.concatenate(
            [jnp.where(pos_t[:, e:e + 1] == slot, 1.0, 0.0).astype(BF16) for e in range(N_EXPERTS)], axis=1)
    else:
        slot = lax.broadcasted_iota(I32, (tn, N_EXPERTS * cap), 1)
        acc = jnp.zeros((tn, N_EXPERTS * cap), F32)
        for e in range(N_EXPERTS):
            pe = pos_t[:, e:e + 1]
            acc = jnp.where(jnp.where(pe >= 0, pe + e * cap, -1) == slot, 1.0, acc)
        onehot = acc.astype(BF16)
    y = _dot(onehot, o_ref[0].reshape(N_EXPERTS * cap, D_MODEL))
    g = mod_ref[0][:, 5 * D_MODEL:6 * D_MODEL]
    out_ref[0] = x_ref[0] + g * y


def _scatter(x, mod, pos_t, o, cap, tn):
    b, n, d = x.shape
    xs = pl.BlockSpec((1, tn, d), lambda i, t: (i, t, 0))
    kern = functools.partial(_scatter_kernel, tn=tn, cap=cap)
    return pl.pallas_call(
        kern,
        out_shape=jax.ShapeDtypeStruct((b, n, d), F32),
        grid=(b, n // tn),
        in_specs=[xs,
                  pl.BlockSpec((1, 1, 6 * d), lambda i, t: (i, 0, 0)),
                  pl.BlockSpec((1, tn, N_EXPERTS), lambda i, t: (i, t, 0)),
                  pl.BlockSpec((1, N_EXPERTS, cap, d), lambda i, t: (i, 0, 0, 0))],
        out_specs=xs,
        compiler_params=_params("arbitrary", "arbitrary"),
        name="scatter_add",
    )(x, mod, pos_t, o)


@functools.lru_cache(maxsize=None)
def _np_consts():
    lane = np.arange(GROUP_W)
    seg32 = (lane[:, None] // DIFF_D == lane[None, :] // DIFF_D).astype(np.float32)
    seg64 = (lane[:, None] // HEAD_DIM == lane[None, :] // HEAD_DIM).astype(np.float32)
    ang = 2.0 * np.pi * ((lane[:, None] % HEAD_DIM) * (lane[None, :] % HEAD_DIM) % HEAD_DIM) / HEAD_DIM
    cc = np.cos(ang) * seg64
    ss = np.sin(ang) * seg64
    tri = (lane[:, None] <= lane[None, :]).astype(np.float32)
    return dict(seg32=seg32, seg64=seg64, cc=cc, ss=ss, tri=tri)


@functools.lru_cache(maxsize=None)
def _np_dft(n):
    idx = (np.arange(n, dtype=np.int64)[:, None] * np.arange(n, dtype=np.int64)[None, :]) % n
    ang = 2.0 * np.pi * idx.astype(np.float64) / n
    return np.cos(ang).astype(np.float32), np.sin(ang).astype(np.float32)


@functools.lru_cache(maxsize=None)
def _np_rope(n):
    half = DIFF_D // 2
    inv = 1.0 / (ROPE_BASE ** (np.arange(0, half, 2, dtype=np.float32) / half))
    t = np.arange(n)
    row = (t // GRID_W).astype(np.float32)[:, None] * inv
    col = (t % GRID_W).astype(np.float32)[:, None] * inv
    nf = inv.shape[0]
    d = np.arange(GROUP_W) % DIFF_D
    f = d % nf
    is_col = d >= half
    second = (d % half) >= nf
    ang = np.where(is_col[None, :], col[:, f], row[:, f])
    c = np.cos(ang).astype(np.float32)
    s = np.sin(ang).astype(np.float32)
    s = np.where(second[None, :], s, -s)
    return c, s


@functools.lru_cache(maxsize=None)
def _np_na_index(n_rows):
    n_steps = n_rows // NA_QROWS
    reps = [0, 1, 2, n_steps - 2, n_steps - 1]
    tq = NA_QROWS * GRID_W
    roff = np.zeros((len(reps), tq, NA_WIN), np.int32)
    coff = np.zeros_like(roff)
    valid = np.zeros(roff.shape, bool)
    for ci, t in enumerate(reps):
        ws = int(np.clip(NA_QROWS * t - NA_ROWS // 2, 0, n_rows - NA_WIN_ROWS))
        qi = np.arange(tq)
        r = NA_QROWS * t + qi // GRID_W
        qcol = qi % GRID_W
        kk = np.arange(NA_WIN)
        krow = ws + kk // GRID_W
        kcol = kk % GRID_W
        rstart = np.clip(r - NA_ROWS // 2, 0, n_rows - NA_ROWS)
        wstart = np.clip(qcol - NA_COLS // 2, 0, GRID_W - NA_COLS)
        vr = (krow[None, :] >= rstart[:, None]) & (krow[None, :] < rstart[:, None] + NA_ROWS)
        vc = (kcol[None, :] >= wstart[:, None]) & (kcol[None, :] < wstart[:, None] + NA_COLS)
        valid[ci] = vr & vc
        roff[ci] = np.clip(krow[None, :] - r[:, None] + NA_ROWS - 1, 0, 2 * NA_ROWS - 2)
        coff[ci] = np.clip(kcol[None, :] - qcol[:, None], -(NA_COLS - 1), NA_COLS - 1) + NA_COLS - 1
    return roff, coff, valid


def _na_table(rpb, n_rows):
    roff, coff, valid = _np_na_index(n_rows)
    bias = rpb.astype(F32)[:, roff, coff] * LOG2E
    bias = jnp.where(valid[None], bias, NEG_INF)
    return bias.transpose(1, 0, 2, 3).reshape(roff.shape[0], HEADS * roff.shape[1], NA_WIN)


def _moe(x, mod, g2, rwt, tri, wg, wu, wd, bb, tn):
    n = x.shape[1]
    cap = CAPACITY_FACTOR * n // N_EXPERTS
    xn, pos, gate = _route(x, mod, g2, rwt, tri, cap)
    o = _expert_ffn(xn, pos, gate, wg, wu, wd, cap, bb)
    return _scatter(x, mod, pos.transpose(0, 2, 1), o, cap, tn)


def kernel(x, c, ctx, c_ctx, ada_w, ada_b, norm1_g, norm2_g, w_in, w_out, head_out_g, sgu_w, sgu_b, diff_qn_g, diff_kn_g, diff_lambda, na_qn_g, na_kn_g, na_rpb, router_w, exp_w_gate, exp_w_up, exp_w_down):
    b, n, d = x.shape
    n_ctx = ctx.shape[1]
    npc = _np_consts()
    consts = {k: jnp.asarray(v, BF16) for k, v in npc.items()}
    seg64, tri = consts["seg64"], consts["tri"]
    cn, sn = (jnp.asarray(a, BF16) for a in _np_dft(n))
    cn_c, sn_c = (jnp.asarray(a, BF16) for a in _np_dft(n_ctx))
    rope = tuple(jnp.asarray(a, F32) for a in _np_rope(n))

    pad = (-(b + 1)) % 8
    c_rows = jnp.concatenate([c, c_ctx[None, :], jnp.zeros((pad, d), F32)], axis=0)
    mod_all = _modulation(c_rows, ada_w, ada_b)

    xc = ctx
    for l in range(DEPTH):
        last = l == DEPTH - 1
        lam_init = 0.8 - 0.6 * math.exp(-0.3 * l)
        mod = mod_all[l, :b][:, None, :]
        mod_c = jnp.broadcast_to(mod_all[l, b][None, None, :], (b, 1, 6 * d))
        g1 = norm1_g[l][None, :]
        g2 = norm2_g[l][None, :]
        w_in_l = w_in[l].astype(BF16)
        w_out_l = w_out[l].astype(BF16)
        sguw = sgu_w[l].astype(BF16)
        sgub = jnp.repeat(sgu_b[l].T, HEAD_DIM, axis=1)
        hg = head_out_g[l].reshape(4, GROUP_W)
        vec = jnp.stack([jnp.tile(diff_qn_g[l], GROUP_W // DIFF_D), jnp.tile(diff_kn_g[l], GROUP_W // DIFF_D),
                         jnp.tile(na_qn_g[l], HEADS), jnp.tile(na_kn_g[l], HEADS),
                         hg[0], hg[1], hg[2], hg[3]], axis=0).astype(F32)
        lam = diff_lambda[l].astype(F32)
        table = _na_table(na_rpb[l], n // GRID_W)
        rwt = router_w[l].T.astype(BF16)
        wg = exp_w_gate[l].astype(BF16)
        wu = exp_w_up[l].astype(BF16)
        wd = exp_w_down[l].astype(BF16)

        ya, zc, zs, qc, kct, vc, qd, kdt, vd = _in_proj(x, mod, g1, w_in_l, consts, sguw, sgub, vec, rope, 512)
        ya_c, zc_c, zs_c, qc_c, kct_c, vc_c, qd_c, kdt_c, vd_c = _in_proj(
            xc, mod_c, g1, w_in_l, consts, sguw, sgub, vec, None, n_ctx)

        yb = _fourier(zc, zs, cn, sn, seg64, vec, 1024)
        yc = _attention(qc, [(kct, vc), (kct_c, vc_c)], lam, lam_init, seg64, vec, 6, 1.0 - lam_init, 256)
        yd = _na_attention(qd, kdt, vd, kdt_c, vd_c, table, seg64, vec)
        x = _out_proj(ya, yb, yc, yd, w_out_l, x, mod, 512)
        x = _moe(x, mod, g2, rwt, tri, wg, wu, wd, 1, 512)

        if not last:
            yb_c = _fourier(zc_c, zs_c, cn_c, sn_c, seg64, vec, n_ctx)
            yc_c = _attention(qc_c, [(kct_c, vc_c)], lam, lam_init, seg64, vec, 6, 1.0 - lam_init, n_ctx)
            yd_c = _attention(qd_c, [(kdt_c, vd_c)], None, 0.0, seg64, vec, 7, 1.0, n_ctx)
            xc = _out_proj(ya_c, yb_c, yc_c, yd_c, w_out_l, xc, mod_c, n_ctx)
            xc = _moe(xc, mod_c, g2, rwt, tri, wg, wu, wd, b, n_ctx)
    return x
```

```python
import functools
import math

import numpy as np
import jax
import jax.numpy as jnp
from jax import lax
from jax.experimental import pallas as pl
from jax.experimental.pallas import tpu as pltpu

F32 = jnp.float32
BF16 = jnp.bfloat16
I32 = jnp.int32

D_MODEL = 1024
DEPTH = 2
GRID_W = 64
HEAD_DIM = 64
LOG2_HEAD_DIM = 6
GROUP_W = 256
HEADS = GROUP_W // HEAD_DIM
CHUNK = 128
DIFF_D = HEAD_DIM // 2
NA_ROWS = 8
NA_COLS = 16
N_EXPERTS = 16
CAPACITY_FACTOR = 2
ROPE_BASE = 10000.0
EPS = 1e-6
IN_W = 9 * GROUP_W
LOG2E = 1.4426950408889634

VMEM_LIMIT_BYTES = 56 * 1024 * 1024
NA_QROWS = 2
NA_WIN_ROWS = NA_ROWS + 2
NA_WIN = NA_WIN_ROWS * GRID_W
NEG_INF = float("-inf")


def _dot(a, b):
    return jnp.dot(a, b, preferred_element_type=F32)


def _params(*sem):
    return pltpu.CompilerParams(dimension_semantics=sem, vmem_limit_bytes=VMEM_LIMIT_BYTES)


def _full(shape):
    nd = len(shape)
    return pl.BlockSpec(shape, lambda *_: (0,) * nd)


def _seg_rms(x, seg, width):
    x2 = x * x
    hi = x2.astype(BF16)
    lo = (x2 - hi.astype(F32)).astype(BF16)
    ss = _dot(hi, seg) + _dot(lo, seg)
    return x * lax.rsqrt(ss * (1.0 / width) + EPS)


def _mod_kernel(c_ref, w_ref, b_ref, o_ref):
    s = jax.nn.silu(c_ref[...]).astype(BF16)
    o_ref[0] = _dot(s, w_ref[0].astype(BF16)) + b_ref[0]


def _modulation(c_rows, ada_w, ada_b):
    depth, d, w6 = ada_w.shape
    r = c_rows.shape[0]
    tn = 1024
    return pl.pallas_call(
        _mod_kernel,
        out_shape=jax.ShapeDtypeStruct((depth, r, w6), F32),
        grid=(depth, w6 // tn),
        in_specs=[
            pl.BlockSpec((r, d), lambda l, j: (0, 0)),
            pl.BlockSpec((1, d, tn), lambda l, j: (l, 0, j)),
            pl.BlockSpec((1, 1, tn), lambda l, j: (l, 0, j)),
        ],
        out_specs=pl.BlockSpec((1, r, tn), lambda l, j: (l, 0, j)),
        compiler_params=_params("arbitrary", "arbitrary"),
        name="modulation",
    )(c_rows, ada_w, ada_b.reshape(depth, 1, w6))


def _rope(x, c, s, lane):
    fwd = pltpu.roll(x, GROUP_W - 8, 1)
    bwd = pltpu.roll(x, 8, 1)
    partner = jnp.where((lane & 8) == 0, fwd, bwd)
    return x * c + partner * s


def _in_kernel(*refs, tm, use_rope, c_scale, d_scale):
    (x_ref, mod_ref, g1_ref, w_ref, seg32_ref, seg64_ref, cc_ref, ss_ref, sguw_ref, sgub_ref, vec_ref) = refs[:11]
    rest = refs[11:]
    if use_rope:
        ropec_ref, ropes_ref = rest[:2]
        rest = rest[2:]
    ya_ref, zc_ref, zs_ref, qc_ref, kct_ref, vc_ref, qd_ref, kdt_ref, vd_ref = rest

    x = x_ref[0]
    mod = mod_ref[0]
    sh = mod[:, 0:D_MODEL]
    sc = mod[:, D_MODEL:2 * D_MODEL]
    ms = jnp.mean(x * x, axis=-1, keepdims=True)
    h = x * lax.rsqrt(ms + EPS) * g1_ref[...]
    h = (h * (1.0 + sc) + sh).astype(BF16)
    lane = lax.broadcasted_iota(I32, (1, GROUP_W), 1)
    head = lane >> LOG2_HEAD_DIM
    seg32 = seg32_ref[...]
    seg64 = seg64_ref[...]
    vec = vec_ref[...]

    z = jax.nn.gelu(_dot(h, w_ref[:, 0:2 * GROUP_W]))
    u = z[:, 0:GROUP_W]
    vn = _seg_rms(z[:, GROUP_W:2 * GROUP_W], seg64, HEAD_DIM).astype(BF16)
    rows = []
    for c in range(tm // CHUNK):
        vch = vn[c * CHUNK:(c + 1) * CHUNK]
        sv = jnp.zeros((CHUNK, GROUP_W), F32)
        for hh in range(HEADS):
            sv = jnp.where(head == hh, _dot(sguw_ref[hh], vch), sv)
        rows.append(sv + sgub_ref[...])
    ya = u * jnp.concatenate(rows, axis=0)
    ya_ref[0] = (_seg_rms(ya, seg64, HEAD_DIM) * vec[4:5]).astype(BF16)

    zb = _dot(h, w_ref[:, 2 * GROUP_W:3 * GROUP_W]).astype(BF16)
    zc_ref[0] = _dot(zb, cc_ref[...]).astype(BF16)
    zs_ref[0] = _dot(zb, ss_ref[...]).astype(BF16)

    pc = _dot(h, w_ref[:, 3 * GROUP_W:6 * GROUP_W])
    q = _seg_rms(pc[:, 0:GROUP_W], seg32, DIFF_D) * vec[0:1]
    k = _seg_rms(pc[:, GROUP_W:2 * GROUP_W], seg32, DIFF_D) * vec[1:2]
    if use_rope:
        rc = ropec_ref[...]
        rs = ropes_ref[...]
        q = _rope(q, rc, rs, lane)
        k = _rope(k, rc, rs, lane)
    qc_ref[0] = (q * c_scale).astype(BF16)
    kct_ref[0] = k.T.astype(BF16)
    vc_ref[0] = pc[:, 2 * GROUP_W:3 * GROUP_W].astype(BF16)

    pd = _dot(h, w_ref[:, 6 * GROUP_W:9 * GROUP_W])
    qd = _seg_rms(pd[:, 0:GROUP_W], seg64, HEAD_DIM) * vec[2:3]
    kd = _seg_rms(pd[:, GROUP_W:2 * GROUP_W], seg64, HEAD_DIM) * vec[3:4]
    qd_ref[0] = (qd * d_scale).astype(BF16)
    kdt_ref[0] = kd.T.astype(BF16)
    vd_ref[0] = pd[:, 2 * GROUP_W:3 * GROUP_W].astype(BF16)


def _in_proj(x, mod, g1, w_in, consts, sguw, sgub, vec, rope, tm):
    b, n, d = x.shape
    use_rope = rope is not None
    tok = pl.BlockSpec((1, tm, GROUP_W), lambda i, t: (i, t, 0))
    tok_t = pl.BlockSpec((1, GROUP_W, tm), lambda i, t: (i, 0, t))
    in_specs = [
        pl.BlockSpec((1, tm, d), lambda i, t: (i, t, 0)),
        pl.BlockSpec((1, 1, 6 * d), lambda i, t: (i, 0, 0)),
        _full((1, d)),
        _full((d, IN_W)),
        _full((GROUP_W, GROUP_W)), _full((GROUP_W, GROUP_W)), _full((GROUP_W, GROUP_W)), _full((GROUP_W, GROUP_W)),
        _full((HEADS, CHUNK, CHUNK)),
        _full((CHUNK, GROUP_W)),
        _full((8, GROUP_W)),
    ]
    args = [x, mod, g1, w_in, consts["seg32"], consts["seg64"], consts["cc"], consts["ss"], sguw, sgub, vec]
    if use_rope:
        in_specs += [pl.BlockSpec((tm, GROUP_W), lambda i, t: (t, 0))] * 2
        args += list(rope)
    sd = jax.ShapeDtypeStruct((b, n, GROUP_W), BF16)
    sdt = jax.ShapeDtypeStruct((b, GROUP_W, n), BF16)
    kern = functools.partial(
        _in_kernel, tm=tm, use_rope=use_rope,
        c_scale=(DIFF_D ** -0.5) * LOG2E, d_scale=(HEAD_DIM ** -0.5) * LOG2E)
    return pl.pallas_call(
        kern,
        out_shape=(sd, sd, sd, sd, sdt, sd, sd, sdt, sd),
        grid=(b, n // tm),
        in_specs=in_specs,
        out_specs=(tok, tok, tok, tok, tok_t, tok, tok, tok_t, tok),
        compiler_params=_params("arbitrary", "arbitrary"),
        name="in_proj",
    )(*args)


def _fourier_kernel(cn_ref, sn_ref, zc_ref, zs_ref, seg64_ref, vec_ref, o_ref, *, norm):
    y = (_dot(cn_ref[...], zc_ref[0]) - _dot(sn_ref[...], zs_ref[0])) * norm
    o_ref[0] = (_seg_rms(y, seg64_ref[...], HEAD_DIM) * vec_ref[5:6]).astype(BF16)


def _fourier(zc, zs, cn, sn, seg64, vec, tn):
    b, n, _ = zc.shape
    kern = functools.partial(_fourier_kernel, norm=1.0 / math.sqrt(n * HEAD_DIM))
    return pl.pallas_call(
        kern,
        out_shape=jax.ShapeDtypeStruct((b, n, GROUP_W), BF16),
        grid=(n // tn, b),
        in_specs=[
            pl.BlockSpec((tn, n), lambda t, i: (t, 0)),
            pl.BlockSpec((tn, n), lambda t, i: (t, 0)),
            pl.BlockSpec((1, n, GROUP_W), lambda t, i: (i, 0, 0)),
            pl.BlockSpec((1, n, GROUP_W), lambda t, i: (i, 0, 0)),
            _full((GROUP_W, GROUP_W)),
            _full((8, GROUP_W)),
        ],
        out_specs=pl.BlockSpec((1, tn, GROUP_W), lambda t, i: (i, t, 0)),
        compiler_params=_params("arbitrary", "arbitrary"),
        name="fourier",
    )(cn, sn, zc, zs, seg64, vec)


def _attn_kernel(*refs, n_src, diff, lam_init, chunk, tq, vec_row, out_scale):
    q_ref = refs[0]
    srcs = [(refs[1 + 2 * i], refs[2 + 2 * i]) for i in range(n_src)]
    rest = refs[1 + 2 * n_src:]
    if diff:
        lam_ref = rest[0]
        rest = rest[1:]
    seg64_ref, vec_ref, o_ref = rest

    q = q_ref[0]
    lane = lax.broadcasted_iota(I32, (1, GROUP_W), 1)
    shift = LOG2_HEAD_DIM - 1 if diff else LOG2_HEAD_DIM
    n_maps = GROUP_W >> shift
    if diff:
        lf = lam_ref[...]
        lam = (jnp.exp(jnp.sum(lf[0:1] * lf[1:2], axis=-1, keepdims=True))
               - jnp.exp(jnp.sum(lf[2:3] * lf[3:4], axis=-1, keepdims=True)) + lam_init)

    def body(i, out):
        qm = jnp.where((lane >> shift) == i, q, jnp.zeros_like(q))
        m = jnp.full((tq, 1), NEG_INF, F32)
        l = jnp.zeros((tq, 1), F32)
        acc = jnp.zeros((tq, GROUP_W), F32)
        for kt_ref, v_ref in srcs:
            nk = kt_ref.shape[2]
            for c0 in range(0, nk, chunk):
                ck = min(chunk, nk - c0)
                s = _dot(qm, kt_ref[0, :, c0:c0 + ck])
                m_new = jnp.maximum(m, jnp.max(s, axis=-1, keepdims=True))
                alpha = jnp.exp2(m - m_new)
                p = jnp.exp2(s - m_new)
                l = alpha * l + jnp.sum(p, axis=-1, keepdims=True)
                acc = alpha * acc + _dot(p.astype(BF16), v_ref[0, c0:c0 + ck, :])
                m = m_new
        o = acc * (1.0 / l)
        if diff:
            hd = i >> 1
            o = o * jnp.where((i & 1) == 0, 1.0, -lam)
        else:
            hd = i
        return jnp.where((lane >> LOG2_HEAD_DIM) == hd, out + o, out)

    out = lax.fori_loop(0, n_maps, body, jnp.zeros((tq, GROUP_W), F32))
    y = _seg_rms(out, seg64_ref[...], HEAD_DIM) * vec_ref[vec_row:vec_row + 1]
    o_ref[0] = (y * out_scale).astype(BF16)


def _attention(q, srcs, lam, lam_init, seg64, vec, vec_row, out_scale, tq, chunk=512):
    b, nq, _ = q.shape
    diff = lam is not None
    in_specs = [pl.BlockSpec((1, tq, GROUP_W), lambda i, t: (i, t, 0))]
    args = [q]
    for kt, v in srcs:
        nk = v.shape[1]
        in_specs += [pl.BlockSpec((1, GROUP_W, nk), lambda i, t: (i, 0, 0)),
                     pl.BlockSpec((1, nk, GROUP_W), lambda i, t: (i, 0, 0))]
        args += [kt, v]
    if diff:
        in_specs.append(_full((4, DIFF_D)))
        args.append(lam)
    in_specs += [_full((GROUP_W, GROUP_W)), _full((8, GROUP_W))]
    args += [seg64, vec]
    kern = functools.partial(_attn_kernel, n_src=len(srcs), diff=diff, lam_init=lam_init, chunk=chunk, tq=tq,
                             vec_row=vec_row, out_scale=out_scale)
    return pl.pallas_call(
        kern,
        out_shape=jax.ShapeDtypeStruct((b, nq, GROUP_W), BF16),
        grid=(b, nq // tq),
        in_specs=in_specs,
        out_specs=pl.BlockSpec((1, tq, GROUP_W), lambda i, t: (i, t, 0)),
        compiler_params=_params("arbitrary", "arbitrary"),
        name="diff_attention" if diff else "ctx_attention",
    )(*args)


def _na_kernel(q_ref, kt_ref, v_ref, ktc_ref, vc_ref, tab_ref, seg64_ref, vec_ref, o_ref, *, n_rows):
    t = pl.program_id(1)
    n_steps = n_rows // NA_QROWS
    ws = jnp.clip(NA_QROWS * t - NA_ROWS // 2, 0, n_rows - NA_WIN_ROWS)
    k0 = pl.multiple_of(ws * GRID_W, 128)
    tid = jnp.where(t < 2, t, jnp.where(t < n_steps - 2, 2, t - (n_steps - 5)))
    tq = NA_QROWS * GRID_W

    q = q_ref[0]
    lane = lax.broadcasted_iota(I32, (1, GROUP_W), 1)
    head = lane >> LOG2_HEAD_DIM
    qs = jnp.concatenate([jnp.where(head == hh, q, jnp.zeros_like(q)) for hh in range(HEADS)], axis=0)
    s_loc = _dot(qs, kt_ref[0, :, pl.ds(k0, NA_WIN)]) + tab_ref[tid]
    s_ctx = _dot(qs, ktc_ref[0])
    m = jnp.maximum(jnp.max(s_loc, axis=-1, keepdims=True), jnp.max(s_ctx, axis=-1, keepdims=True))
    p_loc = jnp.exp2(s_loc - m)
    p_ctx = jnp.exp2(s_ctx - m)
    l = jnp.sum(p_loc, axis=-1, keepdims=True) + jnp.sum(p_ctx, axis=-1, keepdims=True)
    o = _dot(p_loc.astype(BF16), v_ref[0, pl.ds(k0, NA_WIN), :]) + _dot(p_ctx.astype(BF16), vc_ref[0])
    o = o * (1.0 / l)
    out = jnp.zeros((tq, GROUP_W), F32)
    for hh in range(HEADS):
        out = jnp.where(head == hh, o[hh * tq:(hh + 1) * tq], out)
    o_ref[0] = (_seg_rms(out, seg64_ref[...], HEAD_DIM) * vec_ref[7:8]).astype(BF16)


def _na_attention(q, kt, v, ktc, vc, table, seg64, vec):
    b, n, _ = q.shape
    nc = vc.shape[1]
    tq = NA_QROWS * GRID_W
    kern = functools.partial(_na_kernel, n_rows=n // GRID_W)
    return pl.pallas_call(
        kern,
        out_shape=jax.ShapeDtypeStruct((b, n, GROUP_W), BF16),
        grid=(b, n // tq),
        in_specs=[
            pl.BlockSpec((1, tq, GROUP_W), lambda i, t: (i, t, 0)),
            pl.BlockSpec((1, GROUP_W, n), lambda i, t: (i, 0, 0)),
            pl.BlockSpec((1, n, GROUP_W), lambda i, t: (i, 0, 0)),
            pl.BlockSpec((1, GROUP_W, nc), lambda i, t: (i, 0, 0)),
            pl.BlockSpec((1, nc, GROUP_W), lambda i, t: (i, 0, 0)),
            _full(table.shape),
            _full((GROUP_W, GROUP_W)),
            _full((8, GROUP_W)),
        ],
        out_specs=pl.BlockSpec((1, tq, GROUP_W), lambda i, t: (i, t, 0)),
        compiler_params=_params("arbitrary", "arbitrary"),
        name="neighbourhood_attention",
    )(q, kt, v, ktc, vc, table, seg64, vec)


def _out_kernel(ya_ref, yb_ref, yc_ref, yd_ref, w_ref, x_ref, mod_ref, o_ref):
    y = jnp.concatenate([ya_ref[0], yb_ref[0], yc_ref[0], yd_ref[0]], axis=-1)
    g = mod_ref[0][:, 2 * D_MODEL:3 * D_MODEL]
    o_ref[0] = x_ref[0] + g * _dot(y, w_ref[...])


def _out_proj(ya, yb, yc, yd, w_out, x, mod, tm):
    b, n, d = x.shape
    tok = pl.BlockSpec((1, tm, GROUP_W), lambda i, t: (i, t, 0))
    xs = pl.BlockSpec((1, tm, d), lambda i, t: (i, t, 0))
    return pl.pallas_call(
        _out_kernel,
        out_shape=jax.ShapeDtypeStruct((b, n, d), F32),
        grid=(b, n // tm),
        in_specs=[tok, tok, tok, tok, _full((4 * GROUP_W, d)), xs,
                  pl.BlockSpec((1, 1, 6 * d), lambda i, t: (i, 0, 0))],
        out_specs=xs,
        compiler_params=_params("arbitrary", "arbitrary"),
        name="out_proj",
    )(ya, yb, yc, yd, w_out, x, mod)


def _cumsum_excl(m, tri):
    n = m.shape[1]
    carry = jnp.zeros((m.shape[0], 1), F32)
    outs = []
    for j in range(n // GROUP_W):
        blk = m[:, j * GROUP_W:(j + 1) * GROUP_W]
        inc = _dot(blk.astype(BF16), tri)
        outs.append(inc - blk + carry)
        carry = carry + inc[:, GROUP_W - 1:GROUP_W]
    return jnp.concatenate(outs, axis=1)


def _route_kernel(x_ref, mod_ref, g2_ref, rwt_ref, tri_ref, xn_ref, pos_ref, gate_ref, lg_ref, *, n, cap, rows):
    mod = mod_ref[0]
    sh = mod[:, 3 * D_MODEL:4 * D_MODEL]
    sc = mod[:, 4 * D_MODEL:5 * D_MODEL]
    for c in range(n // rows):
        x = x_ref[0, c * rows:(c + 1) * rows, :]
        ms = jnp.mean(x * x, axis=-1, keepdims=True)
        h = x * lax.rsqrt(ms + EPS) * g2_ref[...]
        h = (h * (1.0 + sc) + sh).astype(BF16)
        xn_ref[0, c * rows:(c + 1) * rows, :] = h
        lg_ref[:, c * rows:(c + 1) * rows] = lax.dot_general(
            rwt_ref[...], h, (((1,), (1,)), ((), ())), preferred_element_type=F32)
    lg = lg_ref[...]
    e = jnp.exp(lg - jnp.max(lg, axis=0, keepdims=True))
    aff = e / jnp.sum(e, axis=0, keepdims=True)
    gate_ref[0] = aff

    def unresolved(state):
        lo, hi = state
        return jnp.max(jnp.where(lo < hi, 1.0, 0.0)) > 0.0

    def bisect(state):
        lo, hi = state
        mid = 0.5 * (lo + hi)
        mid = jnp.where(mid > lo, mid, hi)
        ge = aff >= mid
        cnt = jnp.sum(jnp.where(ge, 1.0, 0.0), axis=1, keepdims=True)
        least_ge = jnp.min(jnp.where(ge, aff, jnp.inf), axis=1, keepdims=True)
        most_lt = jnp.max(jnp.where(ge, NEG_INF, aff), axis=1, keepdims=True)
        up = cnt >= cap
        return jnp.where(up, least_ge, lo), jnp.where(up, hi, most_lt)

    thr, _ = lax.while_loop(unresolved, bisect, (jnp.min(aff, axis=1, keepdims=True),
                                                 jnp.max(aff, axis=1, keepdims=True)))
    gt = aff > thr
    eq = aff == thr
    need = cap - jnp.sum(jnp.where(gt, 1.0, 0.0), axis=1, keepdims=True)
    tri = tri_ref[...]
    rank_eq = _cumsum_excl(jnp.where(eq, 1.0, 0.0), tri)
    sel = jnp.where(gt, 1.0, jnp.where(eq, jnp.where(rank_eq < need, 1.0, 0.0), 0.0))
    pos = _cumsum_excl(sel, tri)
    pos_ref[0] = jnp.where(sel > 0.0, pos, -1.0).astype(I32)


def _route(x, mod, g2, rwt, tri, cap):
    b, n, d = x.shape
    rows = min(n, 512)
    kern = functools.partial(_route_kernel, n=n, cap=cap, rows=rows)
    return pl.pallas_call(
        kern,
        out_shape=(jax.ShapeDtypeStruct((b, n, d), BF16),
                   jax.ShapeDtypeStruct((b, N_EXPERTS, n), I32),
                   jax.ShapeDtypeStruct((b, N_EXPERTS, n), F32)),
        grid=(b,),
        in_specs=[
            pl.BlockSpec((1, n, d), lambda i: (i, 0, 0)),
            pl.BlockSpec((1, 1, 6 * d), lambda i: (i, 0, 0)),
            _full((1, d)),
            _full((N_EXPERTS, d)),
            _full((GROUP_W, GROUP_W)),
        ],
        out_specs=(pl.BlockSpec((1, n, d), lambda i: (i, 0, 0)),
                   pl.BlockSpec((1, N_EXPERTS, n), lambda i: (i, 0, 0)),
                   pl.BlockSpec((1, N_EXPERTS, n), lambda i: (i, 0, 0))),
        scratch_shapes=[pltpu.VMEM((N_EXPERTS, n), F32)],
        compiler_params=_params("arbitrary"),
        name="router",
    )(x, mod, g2, rwt, tri)


def _ffn_kernel(xn_ref, pos_ref, gate_ref, wg_ref, wu_ref, wd_ref, o_ref, *, bb, cap, n):
    slot = lax.broadcasted_iota(I32, (cap, n), 0)
    xs, gs = [], []
    for i in range(bb):
        hit = pos_ref[i, 0] == slot
        onehot = jnp.where(hit, 1.0, 0.0).astype(BF16)
        gs.append(jnp.sum(jnp.where(hit, gate_ref[i, 0], 0.0), axis=1, keepdims=True))
        xs.append(_dot(onehot, xn_ref[i]).astype(BF16))
    xg = xs[0] if bb == 1 else jnp.concatenate(xs, axis=0)
    g = gs[0] if bb == 1 else jnp.concatenate(gs, axis=0)
    hid = (jax.nn.silu(_dot(xg, wg_ref[0])) * _dot(xg, wu_ref[0])).astype(BF16)
    o = _dot(hid, wd_ref[0]) * g
    for i in range(bb):
        o_ref[i, 0] = o[i * cap:(i + 1) * cap].astype(BF16)


def _expert_ffn(xn, pos, gate, wg, wu, wd, cap, bb):
    b, n, d = xn.shape
    pos4 = pos.reshape(b, N_EXPERTS, 1, n)
    gate4 = gate.reshape(b, N_EXPERTS, 1, n)
    wspec = pl.BlockSpec((1, d, d), lambda e, j: (e, 0, 0))
    sel = pl.BlockSpec((bb, 1, 1, n), lambda e, j: (j, e, 0, 0))
    kern = functools.partial(_ffn_kernel, bb=bb, cap=cap, n=n)
    return pl.pallas_call(
        kern,
        out_shape=jax.ShapeDtypeStruct((b, N_EXPERTS, cap, d), BF16),
        grid=(N_EXPERTS, b // bb),
        in_specs=[pl.BlockSpec((bb, n, d), lambda e, j: (j, 0, 0)), sel, sel, wspec, wspec, wspec],
        out_specs=pl.BlockSpec((bb, 1, cap, d), lambda e, j: (j, e, 0, 0)),
        compiler_params=_params("arbitrary", "arbitrary"),
        name="expert_ffn",
    )(xn, pos4, gate4, wg, wu, wd)


def _scatter_kernel(x_ref, mod_ref, pt_ref, o_ref, out_ref, *, tn, cap):
    pos_t = pt_ref[0]
    if cap % 128 == 0:
        slot = lax.broadcasted_iota(I32, (tn, cap), 1)
        onehot = jnp.concatenate(
            [jnp.where(pos_t[:, e:e + 1] == slot, 1.0, 0.0).astype(BF16) for e in range(N_EXPERTS)], axis=1)
    else:
        slot = lax.broadcasted_iota(I32, (tn, N_EXPERTS * cap), 1)
        acc = jnp.zeros((tn, N_EXPERTS * cap), F32)
        for e in range(N_EXPERTS):
            pe = pos_t[:, e:e + 1]
            acc = jnp.where(jnp.where(pe >= 0, pe + e * cap, -1) == slot, 1.0, acc)
        onehot = acc.astype(BF16)
    y = _dot(onehot, o_ref[0].reshape(N_EXPERTS * cap, D_MODEL))
    g = mod_ref[0][:, 5 * D_MODEL:6 * D_MODEL]
    out_ref[0] = x_ref[0] + g * y


def _scatter(x, mod, pos_t, o, cap, tn):
    b, n, d = x.shape
    xs = pl.BlockSpec((1, tn, d), lambda i, t: (i, t, 0))
    kern = functools.partial(_scatter_kernel, tn=tn, cap=cap)
    return pl.pallas_call(
        kern,
        out_shape=jax.ShapeDtypeStruct((b, n, d), F32),
        grid=(b, n // tn),
        in_specs=[xs,
                  pl.BlockSpec((1, 1, 6 * d), lambda i, t: (i, 0, 0)),
                  pl.BlockSpec((1, tn, N_EXPERTS), lambda i, t: (i, t, 0)),
                  pl.BlockSpec((1, N_EXPERTS, cap, d), lambda i, t: (i, 0, 0, 0))],
        out_specs=xs,
        compiler_params=_params("arbitrary", "arbitrary"),
        name="scatter_add",
    )(x, mod, pos_t, o)


@functools.lru_cache(maxsize=None)
def _np_consts():
    lane = np.arange(GROUP_W)
    seg32 = (lane[:, None] // DIFF_D == lane[None, :] // DIFF_D).astype(np.float32)
    seg64 = (lane[:, None] // HEAD_DIM == lane[None, :] // HEAD_DIM).astype(np.float32)
    ang = 2.0 * np.pi * ((lane[:, None] % HEAD_DIM) * (lane[None, :] % HEAD_DIM) % HEAD_DIM) / HEAD_DIM
    cc = np.cos(ang) * seg64
    ss = np.sin(ang) * seg64
    tri = (lane[:, None] <= lane[None, :]).astype(np.float32)
    return dict(seg32=seg32, seg64=seg64, cc=cc, ss=ss, tri=tri)


@functools.lru_cache(maxsize=None)
def _np_dft(n):
    idx = (np.arange(n, dtype=np.int64)[:, None] * np.arange(n, dtype=np.int64)[None, :]) % n
    ang = 2.0 * np.pi * idx.astype(np.float64) / n
    return np.cos(ang).astype(np.float32), np.sin(ang).astype(np.float32)


@functools.lru_cache(maxsize=None)
def _np_rope(n):
    half = DIFF_D // 2
    inv = 1.0 / (ROPE_BASE ** (np.arange(0, half, 2, dtype=np.float32) / half))
    t = np.arange(n)
    row = (t // GRID_W).astype(np.float32)[:, None] * inv
    col = (t % GRID_W).astype(np.float32)[:, None] * inv
    nf = inv.shape[0]
    d = np.arange(GROUP_W) % DIFF_D
    f = d % nf
    is_col = d >= half
    second = (d % half) >= nf
    ang = np.where(is_col[None, :], col[:, f], row[:, f])
    c = np.cos(ang).astype(np.float32)
    s = np.sin(ang).astype(np.float32)
    s = np.where(second[None, :], s, -s)
    return c, s


@functools.lru_cache(maxsize=None)
def _np_na_index(n_rows):
    n_steps = n_rows // NA_QROWS
    reps = [0, 1, 2, n_steps - 2, n_steps - 1]
    tq = NA_QROWS * GRID_W
    roff = np.zeros((len(reps), NA_QROWS, NA_WIN_ROWS), np.int32)
    valid = np.zeros((len(reps), tq, NA_WIN), bool)
    for ci, t in enumerate(reps):
        ws = int(np.clip(NA_QROWS * t - NA_ROWS // 2, 0, n_rows - NA_WIN_ROWS))
        qi = np.arange(tq)
        r = NA_QROWS * t + qi // GRID_W
        qcol = qi % GRID_W
        kk = np.arange(NA_WIN)
        krow = ws + kk // GRID_W
        kcol = kk % GRID_W
        rstart = np.clip(r - NA_ROWS // 2, 0, n_rows - NA_ROWS)
        wstart = np.clip(qcol - NA_COLS // 2, 0, GRID_W - NA_COLS)
        vr = (krow[None, :] >= rstart[:, None]) & (krow[None, :] < rstart[:, None] + NA_ROWS)
        vc = (kcol[None, :] >= wstart[:, None]) & (kcol[None, :] < wstart[:, None] + NA_COLS)
        valid[ci] = vr & vc
        rows_q = NA_QROWS * t + np.arange(NA_QROWS)
        rows_k = ws + np.arange(NA_WIN_ROWS)
        roff[ci] = np.clip(rows_k[None, :] - rows_q[:, None] + NA_ROWS - 1, 0, 2 * NA_ROWS - 2)
    return roff, valid


def _na_table(rpb, n_rows):
    roff, valid = _np_na_index(n_rows)
    n_cls = roff.shape[0]
    padw = GRID_W - NA_COLS
    padded = jnp.pad(rpb.astype(F32), ((0, 0), (0, 0), (padw, padw)))
    toep = jnp.stack([padded[:, :, GRID_W - 1 - q:2 * GRID_W - 1 - q] for q in range(GRID_W)], axis=2)
    blocks = jnp.take(toep, roff.reshape(-1), axis=1)
    blocks = blocks.reshape(HEADS, n_cls, NA_QROWS, NA_WIN_ROWS, GRID_W, GRID_W)
    bias = blocks.transpose(1, 0, 2, 4, 3, 5).reshape(n_cls, HEADS, NA_QROWS * GRID_W, NA_WIN)
    bias = jnp.where(valid[:, None], bias * LOG2E, NEG_INF)
    return bias.reshape(n_cls, HEADS * NA_QROWS * GRID_W, NA_WIN)


def _moe(x, mod, g2, rwt, tri, wg, wu, wd, bb, tn):
    n = x.shape[1]
    cap = CAPACITY_FACTOR * n // N_EXPERTS
    xn, pos, gate = _route(x, mod, g2, rwt, tri, cap)
    o = _expert_ffn(xn, pos, gate, wg, wu, wd, cap, bb)
    return _scatter(x, mod, pos.transpose(0, 2, 1), o, cap, tn)


def kernel(x, c, ctx, c_ctx, ada_w, ada_b, norm1_g, norm2_g, w_in, w_out, head_out_g, sgu_w, sgu_b, diff_qn_g, diff_kn_g, diff_lambda, na_qn_g, na_kn_g, na_rpb, router_w, exp_w_gate, exp_w_up, exp_w_down):
    b, n, d = x.shape
    n_ctx = ctx.shape[1]
    npc = _np_consts()
    consts = {k: jnp.asarray(v, F32).astype(BF16) for k, v in npc.items()}
    seg64, tri = consts["seg64"], consts["tri"]
    cn, sn = (jnp.asarray(a, F32).astype(BF16) for a in _np_dft(n))
    cn_c, sn_c = (jnp.asarray(a, F32).astype(BF16) for a in _np_dft(n_ctx))
    rope = tuple(jnp.asarray(a, F32) for a in _np_rope(n))

    pad = (-(b + 1)) % 8
    c_rows = jnp.concatenate([c, c_ctx[None, :], jnp.zeros((pad, d), F32)], axis=0)
    mod_all = _modulation(c_rows, ada_w, ada_b)

    xc = ctx
    for l in range(DEPTH):
        last = l == DEPTH - 1
        lam_init = 0.8 - 0.6 * math.exp(-0.3 * l)
        mod = mod_all[l, :b][:, None, :]
        mod_c = jnp.broadcast_to(mod_all[l, b][None, None, :], (b, 1, 6 * d))
        g1 = norm1_g[l][None, :]
        g2 = norm2_g[l][None, :]
        w_in_l = w_in[l].astype(BF16)
        w_out_l = w_out[l].astype(BF16)
        sguw = sgu_w[l].astype(BF16)
        sgub = jnp.repeat(sgu_b[l].T, HEAD_DIM, axis=1)
        hg = head_out_g[l].reshape(4, GROUP_W)
        vec = jnp.stack([jnp.tile(diff_qn_g[l], GROUP_W // DIFF_D), jnp.tile(diff_kn_g[l], GROUP_W // DIFF_D),
                         jnp.tile(na_qn_g[l], HEADS), jnp.tile(na_kn_g[l], HEADS),
                         hg[0], hg[1], hg[2], hg[3]], axis=0).astype(F32)
        lam = diff_lambda[l].astype(F32)
        table = _na_table(na_rpb[l], n // GRID_W)
        rwt = router_w[l].T.astype(BF16)
        wg = exp_w_gate[l].astype(BF16)
        wu = exp_w_up[l].astype(BF16)
        wd = exp_w_down[l].astype(BF16)

        ya, zc, zs, qc, kct, vc, qd, kdt, vd = _in_proj(x, mod, g1, w_in_l, consts, sguw, sgub, vec, rope, 512)
        ya_c, zc_c, zs_c, qc_c, kct_c, vc_c, qd_c, kdt_c, vd_c = _in_proj(
            xc, mod_c, g1, w_in_l, consts, sguw, sgub, vec, None, n_ctx)

        yb = _fourier(zc, zs, cn, sn, seg64, vec, 1024)
        yc = _attention(qc, [(kct, vc), (kct_c, vc_c)], lam, lam_init, seg64, vec, 6, 1.0 - lam_init, 256)
        yd = _na_attention(qd, kdt, vd, kdt_c, vd_c, table, seg64, vec)
        x = _out_proj(ya, yb, yc, yd, w_out_l, x, mod, 512)
        x = _moe(x, mod, g2, rwt, tri, wg, wu, wd, 1, 512)

        if not last:
            yb_c = _fourier(zc_c, zs_c, cn_c, sn_c, seg64, vec, n_ctx)
            yc_c = _attention(qc_c, [(kct_c, vc_c)], lam, lam_init, seg64, vec, 6, 1.0 - lam_init, n_ctx)
            yd_c = _attention(qd_c, [(kdt_c, vd_c)], None, 0.0, seg64, vec, 7, 1.0, n_ctx)
            xc = _out_proj(ya_c, yb_c, yc_c, yd_c, w_out_l, xc, mod_c, n_ctx)
            xc = _moe(xc, mod_c, g2, rwt, tri, wg, wu, wd, b, n_ctx)
    return x
```

```python
import functools
import math

import numpy as np
import jax
import jax.numpy as jnp
from jax import lax
from jax.experimental import pallas as pl
from jax.experimental.pallas import tpu as pltpu

F32 = jnp.float32
BF16 = jnp.bfloat16
I32 = jnp.int32

D_MODEL = 1024
DEPTH = 2
GRID_W = 64
HEAD_DIM = 64
LOG2_HEAD_DIM = 6
GROUP_W = 256
HEADS = GROUP_W // HEAD_DIM
CHUNK = 128
DIFF_D = HEAD_DIM // 2
NA_ROWS = 8
NA_COLS = 16
N_EXPERTS = 16
CAPACITY_FACTOR = 2
ROPE_BASE = 10000.0
EPS = 1e-6
IN_W = 9 * GROUP_W
LOG2E = 1.4426950408889634

VMEM_LIMIT_BYTES = 56 * 1024 * 1024
NA_QROWS = 2
NA_WIN_ROWS = NA_ROWS + 2
NA_WIN = NA_WIN_ROWS * GRID_W
NEG_INF = float("-inf")
VEC_ROWS = 16
SAFE_EXP2_BOUND = 48.0
BOUND_SLACK = 1.02


def _dot(a, b):
    return jnp.dot(a, b, preferred_element_type=F32)


def _params(*sem):
    return pltpu.CompilerParams(dimension_semantics=sem, vmem_limit_bytes=VMEM_LIMIT_BYTES)


def _full(shape):
    nd = len(shape)
    return pl.BlockSpec(shape, lambda *_: (0,) * nd)


def _seg_rms(x, seg, width):
    x2 = x * x
    hi = x2.astype(BF16)
    lo = (x2 - hi.astype(F32)).astype(BF16)
    ss = _dot(hi, seg) + _dot(lo, seg)
    return x * lax.rsqrt(ss * (1.0 / width) + EPS)


def _mod_kernel(c_ref, w_ref, b_ref, o_ref):
    s = jax.nn.silu(c_ref[...]).astype(BF16)
    o_ref[0] = _dot(s, w_ref[0].astype(BF16)) + b_ref[0]


def _modulation(c_rows, ada_w, ada_b):
    depth, d, w6 = ada_w.shape
    r = c_rows.shape[0]
    tn = 1024
    return pl.pallas_call(
        _mod_kernel,
        out_shape=jax.ShapeDtypeStruct((depth, r, w6), F32),
        grid=(depth, w6 // tn),
        in_specs=[
            pl.BlockSpec((r, d), lambda l, j: (0, 0)),
            pl.BlockSpec((1, d, tn), lambda l, j: (l, 0, j)),
            pl.BlockSpec((1, 1, tn), lambda l, j: (l, 0, j)),
        ],
        out_specs=pl.BlockSpec((1, r, tn), lambda l, j: (l, 0, j)),
        compiler_params=_params("arbitrary", "arbitrary"),
        name="modulation",
    )(c_rows, ada_w, ada_b.reshape(depth, 1, w6))


def _rope(x, c, s, lane):
    fwd = pltpu.roll(x, GROUP_W - 8, 1)
    bwd = pltpu.roll(x, 8, 1)
    partner = jnp.where((lane & 8) == 0, fwd, bwd)
    return x * c + partner * s


def _in_kernel(*refs, tm, use_rope, c_scale, d_scale):
    (x_ref, mod_ref, g1_ref, w_ref, seg32_ref, seg64_ref, cc_ref, ss_ref, sguw_ref, sgub_ref, vec_ref) = refs[:11]
    rest = refs[11:]
    if use_rope:
        ropec_ref, ropes_ref = rest[:2]
        rest = rest[2:]
    ya_ref, zc_ref, zs_ref, qc_ref, kct_ref, vc_ref, qd_ref, kdt_ref, vd_ref = rest

    x = x_ref[0]
    mod = mod_ref[0]
    sh = mod[:, 0:D_MODEL]
    sc = mod[:, D_MODEL:2 * D_MODEL]
    ms = jnp.mean(x * x, axis=-1, keepdims=True)
    h = x * lax.rsqrt(ms + EPS) * g1_ref[...]
    h = (h * (1.0 + sc) + sh).astype(BF16)
    lane = lax.broadcasted_iota(I32, (1, GROUP_W), 1)
    head = lane >> LOG2_HEAD_DIM
    seg32 = seg32_ref[...]
    seg64 = seg64_ref[...]
    vec = vec_ref[...]

    z = jax.nn.gelu(_dot(h, w_ref[:, 0:2 * GROUP_W]))
    u = z[:, 0:GROUP_W]
    vn = _seg_rms(z[:, GROUP_W:2 * GROUP_W], seg64, HEAD_DIM).astype(BF16)
    rows = []
    for c in range(tm // CHUNK):
        vch = vn[c * CHUNK:(c + 1) * CHUNK]
        sv = jnp.zeros((CHUNK, GROUP_W), F32)
        for hh in range(HEADS):
            sv = jnp.where(head == hh, _dot(sguw_ref[hh], vch), sv)
        rows.append(sv + sgub_ref[...])
    ya = u * jnp.concatenate(rows, axis=0)
    ya_ref[0] = (_seg_rms(ya, seg64, HEAD_DIM) * vec[4:5]).astype(BF16)

    zb = _dot(h, w_ref[:, 2 * GROUP_W:3 * GROUP_W]).astype(BF16)
    zc_ref[0] = _dot(zb, cc_ref[...]).astype(BF16)
    zs_ref[0] = _dot(zb, ss_ref[...]).astype(BF16)

    pc = _dot(h, w_ref[:, 3 * GROUP_W:6 * GROUP_W])
    q = _seg_rms(pc[:, 0:GROUP_W], seg32, DIFF_D) * vec[0:1]
    k = _seg_rms(pc[:, GROUP_W:2 * GROUP_W], seg32, DIFF_D) * vec[1:2]
    if use_rope:
        rc = ropec_ref[...]
        rs = ropes_ref[...]
        q = _rope(q, rc, rs, lane)
        k = _rope(k, rc, rs, lane)
    qc_ref[0] = (q * c_scale).astype(BF16)
    kct_ref[0] = k.T.astype(BF16)
    vc_ref[0] = pc[:, 2 * GROUP_W:3 * GROUP_W].astype(BF16)

    pd = _dot(h, w_ref[:, 6 * GROUP_W:9 * GROUP_W])
    qd = _seg_rms(pd[:, 0:GROUP_W], seg64, HEAD_DIM) * vec[2:3]
    kd = _seg_rms(pd[:, GROUP_W:2 * GROUP_W], seg64, HEAD_DIM) * vec[3:4]
    qd_ref[0] = (qd * d_scale).astype(BF16)
    kdt_ref[0] = kd.T.astype(BF16)
    vd_ref[0] = pd[:, 2 * GROUP_W:3 * GROUP_W].astype(BF16)


def _in_proj(x, mod, g1, w_in, consts, sguw, sgub, vec, rope, tm):
    b, n, d = x.shape
    use_rope = rope is not None
    tok = pl.BlockSpec((1, tm, GROUP_W), lambda i, t: (i, t, 0))
    tok_t = pl.BlockSpec((1, GROUP_W, tm), lambda i, t: (i, 0, t))
    in_specs = [
        pl.BlockSpec((1, tm, d), lambda i, t: (i, t, 0)),
        pl.BlockSpec((1, 1, 6 * d), lambda i, t: (i, 0, 0)),
        _full((1, d)),
        _full((d, IN_W)),
        _full((GROUP_W, GROUP_W)), _full((GROUP_W, GROUP_W)), _full((GROUP_W, GROUP_W)), _full((GROUP_W, GROUP_W)),
        _full((HEADS, CHUNK, CHUNK)),
        _full((CHUNK, GROUP_W)),
        _full((VEC_ROWS, GROUP_W)),
    ]
    args = [x, mod, g1, w_in, consts["seg32"], consts["seg64"], consts["cc"], consts["ss"], sguw, sgub, vec]
    if use_rope:
        in_specs += [pl.BlockSpec((tm, GROUP_W), lambda i, t: (t, 0))] * 2
        args += list(rope)
    sd = jax.ShapeDtypeStruct((b, n, GROUP_W), BF16)
    sdt = jax.ShapeDtypeStruct((b, GROUP_W, n), BF16)
    kern = functools.partial(
        _in_kernel, tm=tm, use_rope=use_rope,
        c_scale=(DIFF_D ** -0.5) * LOG2E, d_scale=(HEAD_DIM ** -0.5) * LOG2E)
    return pl.pallas_call(
        kern,
        out_shape=(sd, sd, sd, sd, sdt, sd, sd, sdt, sd),
        grid=(b, n // tm),
        in_specs=in_specs,
        out_specs=(tok, tok, tok, tok, tok_t, tok, tok, tok_t, tok),
        compiler_params=_params("arbitrary", "arbitrary"),
        name="in_proj",
    )(*args)


def _fourier_kernel(cn_ref, sn_ref, zc_ref, zs_ref, seg64_ref, vec_ref, o_ref, *, norm):
    y = (_dot(cn_ref[...], zc_ref[0]) - _dot(sn_ref[...], zs_ref[0])) * norm
    o_ref[0] = (_seg_rms(y, seg64_ref[...], HEAD_DIM) * vec_ref[5:6]).astype(BF16)


def _fourier(zc, zs, cn, sn, seg64, vec, tn):
    b, n, _ = zc.shape
    kern = functools.partial(_fourier_kernel, norm=1.0 / math.sqrt(n * HEAD_DIM))
    return pl.pallas_call(
        kern,
        out_shape=jax.ShapeDtypeStruct((b, n, GROUP_W), BF16),
        grid=(n // tn, b),
        in_specs=[
            pl.BlockSpec((tn, n), lambda t, i: (t, 0)),
            pl.BlockSpec((tn, n), lambda t, i: (t, 0)),
            pl.BlockSpec((1, n, GROUP_W), lambda t, i: (i, 0, 0)),
            pl.BlockSpec((1, n, GROUP_W), lambda t, i: (i, 0, 0)),
            _full((GROUP_W, GROUP_W)),
            _full((VEC_ROWS, GROUP_W)),
        ],
        out_specs=pl.BlockSpec((1, tn, GROUP_W), lambda t, i: (i, t, 0)),
        compiler_params=_params("arbitrary", "arbitrary"),
        name="fourier",
    )(cn, sn, zc, zs, seg64, vec)


def _attn_kernel(*refs, diff, lam_init, chunk, tq, vec_row, out_scale):
    q_ref, kt_ref, v_ref = refs[:3]
    rest = refs[3:]
    if diff:
        lam_ref = rest[0]
        rest = rest[1:]
    seg64_ref, vec_ref, o_ref = rest

    q = q_ref[0]
    lane = lax.broadcasted_iota(I32, (1, GROUP_W), 1)
    if diff:
        lf = lam_ref[...]
        lam = (jnp.exp(jnp.sum(lf[0:1] * lf[1:2], axis=-1, keepdims=True))
               - jnp.exp(jnp.sum(lf[2:3] * lf[3:4], axis=-1, keepdims=True)) + lam_init)

    nk = kt_ref.shape[2]
    chunks = [(c0, min(chunk, nk - c0)) for c0 in range(0, nk, chunk)]

    width = DIFF_D if diff else HEAD_DIM
    gq = vec_ref[0:1] if diff else vec_ref[2:3]
    gk = vec_ref[1:2] if diff else vec_ref[3:4]
    bound = jnp.max(jnp.abs(gq)) * jnp.max(jnp.abs(gk)) * (math.sqrt(width) * LOG2E * BOUND_SLACK)

    def attend_bounded(sel):
        qm = jnp.where(sel, q, jnp.zeros_like(q))
        l = jnp.zeros((tq, 1), F32)
        acc = jnp.zeros((tq, GROUP_W), F32)
        for c0, ck in chunks:
            p = jnp.exp2(_dot(qm, kt_ref[0, :, c0:c0 + ck]) - bound)
            l = l + jnp.sum(p, axis=-1, keepdims=True)
            acc = acc + _dot(p.astype(BF16), v_ref[0, c0:c0 + ck, :])
        return acc * (1.0 / l)

    def attend_online(sel):
        qm = jnp.where(sel, q, jnp.zeros_like(q))
        m = jnp.full((tq, 1), NEG_INF, F32)
        l = jnp.zeros((tq, 1), F32)
        acc = jnp.zeros((tq, GROUP_W), F32)
        for c0, ck in chunks:
            s = _dot(qm, kt_ref[0, :, c0:c0 + ck])
            m_new = jnp.maximum(m, jnp.max(s, axis=-1, keepdims=True))
            alpha = jnp.exp2(m - m_new)
            p = jnp.exp2(s - m_new)
            l = alpha * l + jnp.sum(p, axis=-1, keepdims=True)
            acc = alpha * acc + _dot(p.astype(BF16), v_ref[0, c0:c0 + ck, :])
            m = m_new
        return acc * (1.0 / l)

    def run(attend, unroll):
        def body(h, out):
            if diff:
                o = (attend((lane >> (LOG2_HEAD_DIM - 1)) == 2 * h)
                     - lam * attend((lane >> (LOG2_HEAD_DIM - 1)) == 2 * h + 1))
            else:
                o = attend((lane >> LOG2_HEAD_DIM) == h)
            return jnp.where((lane >> LOG2_HEAD_DIM) == h, o, out)

        out = lax.fori_loop(0, HEADS, body, jnp.zeros((tq, GROUP_W), F32), unroll=unroll)
        y = _seg_rms(out, seg64_ref[...], HEAD_DIM) * vec_ref[vec_row:vec_row + 1]
        o_ref[0] = (y * out_scale).astype(BF16)

    small = bound <= SAFE_EXP2_BOUND
    pl.when(small)(lambda: run(attend_bounded, True))
    pl.when(jnp.logical_not(small))(lambda: run(attend_online, False))


def _attention(q, kt, v, lam, lam_init, seg64, vec, vec_row, out_scale, tq, chunk=768):
    b, nq, _ = q.shape
    nk = v.shape[1]
    diff = lam is not None
    in_specs = [pl.BlockSpec((1, tq, GROUP_W), lambda i, t: (i, t, 0)),
                pl.BlockSpec((1, GROUP_W, nk), lambda i, t: (i, 0, 0)),
                pl.BlockSpec((1, nk, GROUP_W), lambda i, t: (i, 0, 0))]
    args = [q, kt, v]
    if diff:
        in_specs.append(_full((4, DIFF_D)))
        args.append(lam)
    in_specs += [_full((GROUP_W, GROUP_W)), _full((VEC_ROWS, GROUP_W))]
    args += [seg64, vec]
    kern = functools.partial(_attn_kernel, diff=diff, lam_init=lam_init, chunk=chunk, tq=tq, vec_row=vec_row,
                             out_scale=out_scale)
    return pl.pallas_call(
        kern,
        out_shape=jax.ShapeDtypeStruct((b, nq, GROUP_W), BF16),
        grid=(b, nq // tq),
        in_specs=in_specs,
        out_specs=pl.BlockSpec((1, tq, GROUP_W), lambda i, t: (i, t, 0)),
        compiler_params=_params("arbitrary", "arbitrary"),
        name="diff_attention" if diff else "ctx_attention",
    )(*args)


def _na_kernel(q_ref, kt_ref, v_ref, ktc_ref, vc_ref, tab_ref, seg64_ref, vec_ref, o_ref, *, n_rows, pairs):
    n_steps = n_rows // NA_QROWS
    tq = NA_QROWS * GRID_W
    lane = lax.broadcasted_iota(I32, (1, GROUP_W), 1)
    head = lane >> LOG2_HEAD_DIM
    ktc = ktc_ref[0]
    vc = vc_ref[0]
    bound = (jnp.max(jnp.abs(vec_ref[2:3])) * jnp.max(jnp.abs(vec_ref[3:4])) * (math.sqrt(HEAD_DIM) * LOG2E * BOUND_SLACK)
             + jnp.max(vec_ref[8:9]))

    def run(bounded):
        for pi in range(pairs):
            t = pl.program_id(1) * pairs + pi
            ws = jnp.clip(NA_QROWS * t - NA_ROWS // 2, 0, n_rows - NA_WIN_ROWS)
            k0 = pl.multiple_of(ws * GRID_W, 128)
            tid = jnp.where(t < 2, t, jnp.where(t < n_steps - 2, 2, t - (n_steps - 5)))
            q = q_ref[0, pi * tq:(pi + 1) * tq, :]
            qs = jnp.concatenate([jnp.where(head == hh, q, jnp.zeros_like(q)) for hh in range(HEADS)], axis=0)
            s_loc = _dot(qs, kt_ref[0, :, pl.ds(k0, NA_WIN)]) + tab_ref[tid]
            s_ctx = _dot(qs, ktc)
            if bounded:
                m = bound
            else:
                m = jnp.maximum(jnp.max(s_loc, axis=-1, keepdims=True), jnp.max(s_ctx, axis=-1, keepdims=True))
            p_loc = jnp.exp2(s_loc - m)
            p_ctx = jnp.exp2(s_ctx - m)
            l = jnp.sum(p_loc, axis=-1, keepdims=True) + jnp.sum(p_ctx, axis=-1, keepdims=True)
            o = _dot(p_loc.astype(BF16), v_ref[0, pl.ds(k0, NA_WIN), :]) + _dot(p_ctx.astype(BF16), vc)
            o = o * (1.0 / l)
            out = jnp.zeros((tq, GROUP_W), F32)
            for hh in range(HEADS):
                out = jnp.where(head == hh, o[hh * tq:(hh + 1) * tq], out)
            o_ref[0, pi * tq:(pi + 1) * tq, :] = (
                _seg_rms(out, seg64_ref[...], HEAD_DIM) * vec_ref[7:8]).astype(BF16)

    small = bound <= SAFE_EXP2_BOUND
    pl.when(small)(lambda: run(True))
    pl.when(jnp.logical_not(small))(lambda: run(False))


def _na_attention(q, kt, v, ktc, vc, table, seg64, vec, pairs):
    b, n, _ = q.shape
    nc = vc.shape[1]
    tq = pairs * NA_QROWS * GRID_W
    kern = functools.partial(_na_kernel, n_rows=n // GRID_W, pairs=pairs)
    return pl.pallas_call(
        kern,
        out_shape=jax.ShapeDtypeStruct((b, n, GROUP_W), BF16),
        grid=(b, n // tq),
        in_specs=[
            pl.BlockSpec((1, tq, GROUP_W), lambda i, t: (i, t, 0)),
            pl.BlockSpec((1, GROUP_W, n), lambda i, t: (i, 0, 0)),
            pl.BlockSpec((1, n, GROUP_W), lambda i, t: (i, 0, 0)),
            pl.BlockSpec((1, GROUP_W, nc), lambda i, t: (i, 0, 0)),
            pl.BlockSpec((1, nc, GROUP_W), lambda i, t: (i, 0, 0)),
            _full(table.shape),
            _full((GROUP_W, GROUP_W)),
            _full((VEC_ROWS, GROUP_W)),
        ],
        out_specs=pl.BlockSpec((1, tq, GROUP_W), lambda i, t: (i, t, 0)),
        compiler_params=_params("arbitrary", "arbitrary"),
        name="neighbourhood_attention",
    )(q, kt, v, ktc, vc, table, seg64, vec)


def _out_kernel(ya_ref, yb_ref, yc_ref, yd_ref, w_ref, x_ref, mod_ref, o_ref):
    y = jnp.concatenate([ya_ref[0], yb_ref[0], yc_ref[0], yd_ref[0]], axis=-1)
    g = mod_ref[0][:, 2 * D_MODEL:3 * D_MODEL]
    o_ref[0] = x_ref[0] + g * _dot(y, w_ref[...])


def _out_proj(ya, yb, yc, yd, w_out, x, mod, tm):
    b, n, d = x.shape
    tok = pl.BlockSpec((1, tm, GROUP_W), lambda i, t: (i, t, 0))
    xs = pl.BlockSpec((1, tm, d), lambda i, t: (i, t, 0))
    return pl.pallas_call(
        _out_kernel,
        out_shape=jax.ShapeDtypeStruct((b, n, d), F32),
        grid=(b, n // tm),
        in_specs=[tok, tok, tok, tok, _full((4 * GROUP_W, d)), xs,
                  pl.BlockSpec((1, 1, 6 * d), lambda i, t: (i, 0, 0))],
        out_specs=xs,
        compiler_params=_params("arbitrary", "arbitrary"),
        name="out_proj",
    )(ya, yb, yc, yd, w_out, x, mod)


def _cumsum_excl(m, tri):
    n = m.shape[1]
    carry = jnp.zeros((m.shape[0], 1), F32)
    outs = []
    for j in range(n // GROUP_W):
        blk = m[:, j * GROUP_W:(j + 1) * GROUP_W]
        inc = _dot(blk.astype(BF16), tri)
        outs.append(inc - blk + carry)
        carry = carry + inc[:, GROUP_W - 1:GROUP_W]
    return jnp.concatenate(outs, axis=1)


def _route_kernel(x_ref, mod_ref, g2_ref, rwt_ref, tri_ref, xn_ref, pos_ref, gate_ref, lg_ref, *, n, cap, rows):
    mod = mod_ref[0]
    sh = mod[:, 3 * D_MODEL:4 * D_MODEL]
    sc = mod[:, 4 * D_MODEL:5 * D_MODEL]
    for c in range(n // rows):
        x = x_ref[0, c * rows:(c + 1) * rows, :]
        ms = jnp.mean(x * x, axis=-1, keepdims=True)
        h = x * lax.rsqrt(ms + EPS) * g2_ref[...]
        h = (h * (1.0 + sc) + sh).astype(BF16)
        xn_ref[0, c * rows:(c + 1) * rows, :] = h
        lg_ref[:, c * rows:(c + 1) * rows] = lax.dot_general(
            rwt_ref[...], h, (((1,), (1,)), ((), ())), preferred_element_type=F32)
    lg = lg_ref[...]
    e = jnp.exp(lg - jnp.max(lg, axis=0, keepdims=True))
    aff = e / jnp.sum(e, axis=0, keepdims=True)
    gate_ref[0] = aff

    def unresolved(state):
        lo, hi = state
        return jnp.max(jnp.where(lo < hi, 1.0, 0.0)) > 0.0

    def bisect(state):
        lo, hi = state
        mid = 0.5 * (lo + hi)
        mid = jnp.where(mid > lo, mid, hi)
        ge = aff >= mid
        cnt = jnp.sum(jnp.where(ge, 1.0, 0.0), axis=1, keepdims=True)
        least_ge = jnp.min(jnp.where(ge, aff, jnp.inf), axis=1, keepdims=True)
        most_lt = jnp.max(jnp.where(ge, NEG_INF, aff), axis=1, keepdims=True)
        up = cnt >= cap
        return jnp.where(up, least_ge, lo), jnp.where(up, hi, most_lt)

    thr, _ = lax.while_loop(unresolved, bisect, (jnp.min(aff, axis=1, keepdims=True),
                                                 jnp.max(aff, axis=1, keepdims=True)))
    gt = aff > thr
    eq = aff == thr
    need = cap - jnp.sum(jnp.where(gt, 1.0, 0.0), axis=1, keepdims=True)
    tri = tri_ref[...]
    rank_eq = _cumsum_excl(jnp.where(eq, 1.0, 0.0), tri)
    sel = jnp.where(gt, 1.0, jnp.where(eq, jnp.where(rank_eq < need, 1.0, 0.0), 0.0))
    pos = _cumsum_excl(sel, tri)
    pos_ref[0] = jnp.where(sel > 0.0, pos, -1.0).astype(I32)


def _route(x, mod, g2, rwt, tri, cap):
    b, n, d = x.shape
    rows = min(n, 512)
    kern = functools.partial(_route_kernel, n=n, cap=cap, rows=rows)
    return pl.pallas_call(
        kern,
        out_shape=(jax.ShapeDtypeStruct((b, n, d), BF16),
                   jax.ShapeDtypeStruct((b, N_EXPERTS, n), I32),
                   jax.ShapeDtypeStruct((b, N_EXPERTS, n), F32)),
        grid=(b,),
        in_specs=[
            pl.BlockSpec((1, n, d), lambda i: (i, 0, 0)),
            pl.BlockSpec((1, 1, 6 * d), lambda i: (i, 0, 0)),
            _full((1, d)),
            _full((N_EXPERTS, d)),
            _full((GROUP_W, GROUP_W)),
        ],
        out_specs=(pl.BlockSpec((1, n, d), lambda i: (i, 0, 0)),
                   pl.BlockSpec((1, N_EXPERTS, n), lambda i: (i, 0, 0)),
                   pl.BlockSpec((1, N_EXPERTS, n), lambda i: (i, 0, 0))),
        scratch_shapes=[pltpu.VMEM((N_EXPERTS, n), F32)],
        compiler_params=_params("arbitrary"),
        name="router",
    )(x, mod, g2, rwt, tri)


def _ffn_kernel(xn_ref, pos_ref, gate_ref, wg32_ref, wu32_ref, wd32_ref, o_ref, wg_ref, wu_ref, wd_ref, *, bb, cap, n):
    @pl.when(pl.program_id(1) == 0)
    def _():
        wg_ref[...] = wg32_ref[0, 0].astype(BF16)
        wu_ref[...] = wu32_ref[0, 0].astype(BF16)
        wd_ref[...] = wd32_ref[0, 0].astype(BF16)

    slot = lax.broadcasted_iota(I32, (cap, n), 0)
    xs, gs = [], []
    for i in range(bb):
        hit = pos_ref[i, 0] == slot
        onehot = jnp.where(hit, 1.0, 0.0).astype(BF16)
        gs.append(jnp.sum(jnp.where(hit, gate_ref[i, 0], 0.0), axis=1, keepdims=True))
        xs.append(_dot(onehot, xn_ref[i]).astype(BF16))
    xg = xs[0] if bb == 1 else jnp.concatenate(xs, axis=0)
    g = gs[0] if bb == 1 else jnp.concatenate(gs, axis=0)
    hid = (jax.nn.silu(_dot(xg, wg_ref[...])) * _dot(xg, wu_ref[...])).astype(BF16)
    o = _dot(hid, wd_ref[...]) * g
    for i in range(bb):
        o_ref[i, 0] = o[i * cap:(i + 1) * cap].astype(BF16)


def _expert_ffn(xn, pos, gate, wg, wu, wd, layer, cap, bb):
    b, n, d = xn.shape
    pos4 = pos.reshape(b, N_EXPERTS, 1, n)
    gate4 = gate.reshape(b, N_EXPERTS, 1, n)
    wspec = pl.BlockSpec((1, 1, d, d), lambda e, j: (layer, e, 0, 0))
    sel = pl.BlockSpec((bb, 1, 1, n), lambda e, j: (j, e, 0, 0))
    kern = functools.partial(_ffn_kernel, bb=bb, cap=cap, n=n)
    return pl.pallas_call(
        kern,
        out_shape=jax.ShapeDtypeStruct((b, N_EXPERTS, cap, d), BF16),
        grid=(N_EXPERTS, b // bb),
        in_specs=[pl.BlockSpec((bb, n, d), lambda e, j: (j, 0, 0)), sel, sel, wspec, wspec, wspec],
        out_specs=pl.BlockSpec((bb, 1, cap, d), lambda e, j: (j, e, 0, 0)),
        scratch_shapes=[pltpu.VMEM((d, d), BF16)] * 3,
        compiler_params=_params("arbitrary", "arbitrary"),
        name="expert_ffn",
    )(xn, pos4, gate4, wg, wu, wd)


def _scatter_kernel(x_ref, mod_ref, pt_ref, o_ref, out_ref, *, tn, cap):
    pos_t = pt_ref[0]
    if cap % 128 == 0:
        slot = lax.broadcasted_iota(I32, (tn, cap), 1)
        onehot = jnp.concatenate(
            [jnp.where(pos_t[:, e:e + 1] == slot, 1.0, 0.0).astype(BF16) for e in range(N_EXPERTS)], axis=1)
    else:
        slot = lax.broadcasted_iota(I32, (tn, N_EXPERTS * cap), 1)
        acc = jnp.zeros((tn, N_EXPERTS * cap), F32)
        for e in range(N_EXPERTS):
            pe = pos_t[:, e:e + 1]
            acc = jnp.where(jnp.where(pe >= 0, pe + e * cap, -1) == slot, 1.0, acc)
        onehot = acc.astype(BF16)
    y = _dot(onehot, o_ref[0].reshape(N_EXPERTS * cap, D_MODEL))
    g = mod_ref[0][:, 5 * D_MODEL:6 * D_MODEL]
    out_ref[0] = x_ref[0] + g * y


def _scatter(x, mod, pos_t, o, cap, tn):
    b, n, d = x.shape
    xs = pl.BlockSpec((1, tn, d), lambda i, t: (i, t, 0))
    kern = functools.partial(_scatter_kernel, tn=tn, cap=cap)
    return pl.pallas_call(
        kern,
        out_shape=jax.ShapeDtypeStruct((b, n, d), F32),
        grid=(b, n // tn),
        in_specs=[xs,
                  pl.BlockSpec((1, 1, 6 * d), lambda i, t: (i, 0, 0)),
                  pl.BlockSpec((1, tn, N_EXPERTS), lambda i, t: (i, t, 0)),
                  pl.BlockSpec((1, N_EXPERTS, cap, d), lambda i, t: (i, 0, 0, 0))],
        out_specs=xs,
        compiler_params=_params("arbitrary", "arbitrary"),
        name="scatter_add",
    )(x, mod, pos_t, o)


@functools.lru_cache(maxsize=None)
def _np_consts():
    lane = np.arange(GROUP_W)
    seg32 = (lane[:, None] // DIFF_D == lane[None, :] // DIFF_D).astype(np.float32)
    seg64 = (lane[:, None] // HEAD_DIM == lane[None, :] // HEAD_DIM).astype(np.float32)
    ang = 2.0 * np.pi * ((lane[:, None] % HEAD_DIM) * (lane[None, :] % HEAD_DIM) % HEAD_DIM) / HEAD_DIM
    cc = np.cos(ang) * seg64
    ss = np.sin(ang) * seg64
    tri = (lane[:, None] <= lane[None, :]).astype(np.float32)
    return dict(seg32=seg32, seg64=seg64, cc=cc, ss=ss, tri=tri)


@functools.lru_cache(maxsize=None)
def _np_dft(n):
    idx = (np.arange(n, dtype=np.int64)[:, None] * np.arange(n, dtype=np.int64)[None, :]) % n
    ang = 2.0 * np.pi * idx.astype(np.float64) / n
    return np.cos(ang).astype(np.float32), np.sin(ang).astype(np.float32)


@functools.lru_cache(maxsize=None)
def _np_rope(n):
    half = DIFF_D // 2
    inv = 1.0 / (ROPE_BASE ** (np.arange(0, half, 2, dtype=np.float32) / half))
    t = np.arange(n)
    row = (t // GRID_W).astype(np.float32)[:, None] * inv
    col = (t % GRID_W).astype(np.float32)[:, None] * inv
    nf = inv.shape[0]
    d = np.arange(GROUP_W) % DIFF_D
    f = d % nf
    is_col = d >= half
    second = (d % half) >= nf
    ang = np.where(is_col[None, :], col[:, f], row[:, f])
    c = np.cos(ang).astype(np.float32)
    s = np.sin(ang).astype(np.float32)
    s = np.where(second[None, :], s, -s)
    return c, s


@functools.lru_cache(maxsize=None)
def _np_na_index(n_rows):
    n_steps = n_rows // NA_QROWS
    reps = [0, 1, 2, n_steps - 2, n_steps - 1]
    tq = NA_QROWS * GRID_W
    roff = np.zeros((len(reps), NA_QROWS, NA_WIN_ROWS), np.int32)
    valid = np.zeros((len(reps), tq, NA_WIN), bool)
    for ci, t in enumerate(reps):
        ws = int(np.clip(NA_QROWS * t - NA_ROWS // 2, 0, n_rows - NA_WIN_ROWS))
        qi = np.arange(tq)
        r = NA_QROWS * t + qi // GRID_W
        qcol = qi % GRID_W
        kk = np.arange(NA_WIN)
        krow = ws + kk // GRID_W
        kcol = kk % GRID_W
        rstart = np.clip(r - NA_ROWS // 2, 0, n_rows - NA_ROWS)
        wstart = np.clip(qcol - NA_COLS // 2, 0, GRID_W - NA_COLS)
        vr = (krow[None, :] >= rstart[:, None]) & (krow[None, :] < rstart[:, None] + NA_ROWS)
        vc = (kcol[None, :] >= wstart[:, None]) & (kcol[None, :] < wstart[:, None] + NA_COLS)
        valid[ci] = vr & vc
        rows_q = NA_QROWS * t + np.arange(NA_QROWS)
        rows_k = ws + np.arange(NA_WIN_ROWS)
        roff[ci] = np.clip(rows_k[None, :] - rows_q[:, None] + NA_ROWS - 1, 0, 2 * NA_ROWS - 2)
    return roff, valid


def _na_table(rpb, n_rows):
    roff, valid = _np_na_index(n_rows)
    n_cls = roff.shape[0]
    padw = GRID_W - NA_COLS
    padded = jnp.pad(rpb.astype(F32), ((0, 0), (0, 0), (padw, padw)))
    toep = jnp.stack([padded[:, :, GRID_W - 1 - q:2 * GRID_W - 1 - q] for q in range(GRID_W)], axis=2)
    blocks = jnp.take(toep, roff.reshape(-1), axis=1)
    blocks = blocks.reshape(HEADS, n_cls, NA_QROWS, NA_WIN_ROWS, GRID_W, GRID_W)
    bias = blocks.transpose(1, 0, 2, 4, 3, 5).reshape(n_cls, HEADS, NA_QROWS * GRID_W, NA_WIN)
    bias = jnp.where(valid[:, None], bias * LOG2E, NEG_INF)
    return bias.reshape(n_cls, HEADS * NA_QROWS * GRID_W, NA_WIN)


def _moe(x, mod, g2, rwt, tri, wg, wu, wd, layer, bb, tn):
    n = x.shape[1]
    cap = CAPACITY_FACTOR * n // N_EXPERTS
    xn, pos, gate = _route(x, mod, g2, rwt, tri, cap)
    o = _expert_ffn(xn, pos, gate, wg, wu, wd, layer, cap, bb)
    return _scatter(x, mod, pos.transpose(0, 2, 1), o, cap, tn)


def kernel(x, c, ctx, c_ctx, ada_w, ada_b, norm1_g, norm2_g, w_in, w_out, head_out_g, sgu_w, sgu_b, diff_qn_g, diff_kn_g, diff_lambda, na_qn_g, na_kn_g, na_rpb, router_w, exp_w_gate, exp_w_up, exp_w_down):
    b, n, d = x.shape
    n_ctx = ctx.shape[1]
    npc = _np_consts()
    consts = {k: jnp.asarray(v, F32).astype(BF16) for k, v in npc.items()}
    seg64, tri = consts["seg64"], consts["tri"]
    cn, sn = (jnp.asarray(a, F32).astype(BF16) for a in _np_dft(n))
    cn_c, sn_c = (jnp.asarray(a, F32).astype(BF16) for a in _np_dft(n_ctx))
    rope = tuple(jnp.asarray(a, F32) for a in _np_rope(n))

    pad = (-(b + 1)) % 8
    c_rows = jnp.concatenate([c, c_ctx[None, :], jnp.zeros((pad, d), F32)], axis=0)
    mod_all = _modulation(c_rows, ada_w, ada_b)

    xc = ctx
    for l in range(DEPTH):
        last = l == DEPTH - 1
        lam_init = 0.8 - 0.6 * math.exp(-0.3 * l)
        mod = mod_all[l, :b][:, None, :]
        mod_c = jnp.broadcast_to(mod_all[l, b][None, None, :], (b, 1, 6 * d))
        g1 = norm1_g[l][None, :]
        g2 = norm2_g[l][None, :]
        w_in_l = w_in[l].astype(BF16)
        w_out_l = w_out[l].astype(BF16)
        sguw = sgu_w[l].astype(BF16)
        sgub = jnp.repeat(sgu_b[l].T, HEAD_DIM, axis=1)
        hg = head_out_g[l].reshape(4, GROUP_W)
        vec = jnp.stack([jnp.tile(diff_qn_g[l], GROUP_W // DIFF_D), jnp.tile(diff_kn_g[l], GROUP_W // DIFF_D),
                         jnp.tile(na_qn_g[l], HEADS), jnp.tile(na_kn_g[l], HEADS),
                         hg[0], hg[1], hg[2], hg[3],
                         jnp.broadcast_to(jnp.max(jnp.abs(na_rpb[l])) * LOG2E, (GROUP_W,))], axis=0).astype(F32)
        vec = jnp.pad(vec, ((0, VEC_ROWS - vec.shape[0]), (0, 0)))
        lam = diff_lambda[l].astype(F32)
        table = _na_table(na_rpb[l], n // GRID_W)
        rwt = router_w[l].T.astype(BF16)
        experts = (exp_w_gate, exp_w_up, exp_w_down)

        ya, zc, zs, qc, kct, vc, qd, kdt, vd = _in_proj(x, mod, g1, w_in_l, consts, sguw, sgub, vec, rope, 512)
        ya_c, zc_c, zs_c, qc_c, kct_c, vc_c, qd_c, kdt_c, vd_c = _in_proj(
            xc, mod_c, g1, w_in_l, consts, sguw, sgub, vec, None, n_ctx)

        yb = _fourier(zc, zs, cn, sn, seg64, vec, 1024)
        kct_all = jnp.concatenate([kct, kct_c], axis=2)
        vc_all = jnp.concatenate([vc, vc_c], axis=1)
        yc = _attention(qc, kct_all, vc_all, lam, lam_init, seg64, vec, 6, 1.0 - lam_init, 256)
        yd = _na_attention(qd, kdt, vd, kdt_c, vd_c, table, seg64, vec, 4)
        x = _out_proj(ya, yb, yc, yd, w_out_l, x, mod, 512)
        x = _moe(x, mod, g2, rwt, tri, *experts, l, 1, 512)

        if not last:
            yb_c = _fourier(zc_c, zs_c, cn_c, sn_c, seg64, vec, n_ctx)
            yc_c = _attention(qc_c, kct_c, vc_c, lam, lam_init, seg64, vec, 6, 1.0 - lam_init, n_ctx)
            yd_c = _attention(qd_c, kdt_c, vd_c, None, 0.0, seg64, vec, 7, 1.0, n_ctx)
            xc = _out_proj(ya_c, yb_c, yc_c, yd_c, w_out_l, xc, mod_c, n_ctx)
            xc = _moe(xc, mod_c, g2, rwt, tri, *experts, l, b // 2, n_ctx)
    return x
```

```python
import functools
import math

import numpy as np
import jax
import jax.numpy as jnp
from jax import lax
from jax.experimental import pallas as pl
from jax.experimental.pallas import tpu as pltpu

F32 = jnp.float32
BF16 = jnp.bfloat16
I32 = jnp.int32

D_MODEL = 1024
DEPTH = 2
GRID_W = 64
LANES = 128
HEAD_DIM = 64
LOG2_HEAD_DIM = 6
GROUP_W = 256
HEADS = GROUP_W // HEAD_DIM
CHUNK = 128
DIFF_D = HEAD_DIM // 2
NA_ROWS = 8
NA_COLS = 16
N_EXPERTS = 16
CAPACITY_FACTOR = 2
ROPE_BASE = 10000.0
EPS = 1e-6
IN_W = 9 * GROUP_W
LOG2E = 1.4426950408889634

VMEM_LIMIT_BYTES = 56 * 1024 * 1024
NA_QROWS = 2
NA_WIN_ROWS = NA_ROWS + 2
NA_WIN = NA_WIN_ROWS * GRID_W
NA_HEAD_STACK = 2
NEG_INF = float("-inf")
VEC_ROWS = 16
SAFE_EXP2_BOUND = 48.0
BOUND_SLACK = 1.02


def _dot(a, b):
    return jnp.dot(a, b, preferred_element_type=F32)


def _params(*sem):
    return pltpu.CompilerParams(dimension_semantics=sem, vmem_limit_bytes=VMEM_LIMIT_BYTES)


def _full(shape):
    nd = len(shape)
    return pl.BlockSpec(shape, lambda *_: (0,) * nd)


def _seg_rms(x, seg, width):
    ss = _dot((x * x).astype(BF16), seg)
    return x * lax.rsqrt(ss * (1.0 / width) + EPS)


def _mod_kernel(c_ref, w_ref, b_ref, o_ref):
    s = jax.nn.silu(c_ref[...]).astype(BF16)
    o_ref[0] = _dot(s, w_ref[0].astype(BF16)) + b_ref[0]


def _modulation(c_rows, ada_w, ada_b):
    depth, d, w6 = ada_w.shape
    r = c_rows.shape[0]
    tn = 1024
    return pl.pallas_call(
        _mod_kernel,
        out_shape=jax.ShapeDtypeStruct((depth, r, w6), F32),
        grid=(depth, w6 // tn),
        in_specs=[
            pl.BlockSpec((r, d), lambda l, j: (0, 0)),
            pl.BlockSpec((1, d, tn), lambda l, j: (l, 0, j)),
            pl.BlockSpec((1, 1, tn), lambda l, j: (l, 0, j)),
        ],
        out_specs=pl.BlockSpec((1, r, tn), lambda l, j: (l, 0, j)),
        compiler_params=_params("arbitrary", "arbitrary"),
        name="modulation",
    )(c_rows, ada_w, ada_b.reshape(depth, 1, w6))


def _rope(x, c, s, lane):
    fwd = pltpu.roll(x, GROUP_W - 8, 1)
    bwd = pltpu.roll(x, 8, 1)
    partner = jnp.where((lane & 8) == 0, fwd, bwd)
    return x * c + partner * s


def _in_kernel(*refs, tm, use_rope, c_scale, d_scale):
    (x_ref, mod_ref, g1_ref, w_ref, seg32_ref, seg64_ref, cc_ref, ss_ref, sguw_ref, sgub_ref, vec_ref) = refs[:11]
    rest = refs[11:]
    if use_rope:
        ropec_ref, ropes_ref = rest[:2]
        rest = rest[2:]
    ya_ref, zc_ref, zs_ref, qc_ref, kct_ref, vc_ref, qd_ref, kdt_ref, vd_ref = rest

    x = x_ref[0]
    mod = mod_ref[0]
    sh = mod[:, 0:D_MODEL]
    sc = mod[:, D_MODEL:2 * D_MODEL]
    ms = jnp.mean(x * x, axis=-1, keepdims=True)
    h = x * lax.rsqrt(ms + EPS) * g1_ref[...]
    h = (h * (1.0 + sc) + sh).astype(BF16)
    lane = lax.broadcasted_iota(I32, (1, GROUP_W), 1)
    head = lane >> LOG2_HEAD_DIM
    seg32 = seg32_ref[...]
    seg64 = seg64_ref[...]
    vec = vec_ref[...]

    z = jax.nn.gelu(_dot(h, w_ref[:, 0:2 * GROUP_W]))
    u = z[:, 0:GROUP_W]
    vn = _seg_rms(z[:, GROUP_W:2 * GROUP_W], seg64, HEAD_DIM).astype(BF16)
    rows = []
    for c in range(tm // CHUNK):
        vch = vn[c * CHUNK:(c + 1) * CHUNK]
        sv = jnp.zeros((CHUNK, GROUP_W), F32)
        for hh in range(HEADS):
            sv = jnp.where(head == hh, _dot(sguw_ref[hh], vch), sv)
        rows.append(sv + sgub_ref[...])
    ya = u * jnp.concatenate(rows, axis=0)
    ya_ref[0] = (_seg_rms(ya, seg64, HEAD_DIM) * vec[4:5]).astype(BF16)

    zb = _dot(h, w_ref[:, 2 * GROUP_W:3 * GROUP_W]).astype(BF16)
    zc_ref[0] = _dot(zb, cc_ref[...]).astype(BF16)
    zs_ref[0] = _dot(zb, ss_ref[...]).astype(BF16)

    pc = _dot(h, w_ref[:, 3 * GROUP_W:6 * GROUP_W])
    q = _seg_rms(pc[:, 0:GROUP_W], seg32, DIFF_D) * vec[0:1]
    k = _seg_rms(pc[:, GROUP_W:2 * GROUP_W], seg32, DIFF_D) * vec[1:2]
    if use_rope:
        rc = ropec_ref[...]
        rs = ropes_ref[...]
        q = _rope(q, rc, rs, lane)
        k = _rope(k, rc, rs, lane)
    qc_ref[0] = (q * c_scale).astype(BF16)
    kct_ref[0] = k.T.astype(BF16)
    vc_ref[0] = pc[:, 2 * GROUP_W:3 * GROUP_W].astype(BF16)

    pd = _dot(h, w_ref[:, 6 * GROUP_W:9 * GROUP_W])
    qd = _seg_rms(pd[:, 0:GROUP_W], seg64, HEAD_DIM) * vec[2:3]
    kd = _seg_rms(pd[:, GROUP_W:2 * GROUP_W], seg64, HEAD_DIM) * vec[3:4]
    qd_ref[0] = (qd * d_scale).astype(BF16)
    kdt_ref[0] = kd.T.astype(BF16)
    vd_ref[0] = pd[:, 2 * GROUP_W:3 * GROUP_W].astype(BF16)


def _in_proj(x, mod, g1, w_in, consts, sguw, sgub, vec, rope, tm):
    b, n, d = x.shape
    use_rope = rope is not None
    tok = pl.BlockSpec((1, tm, GROUP_W), lambda i, t: (i, t, 0))
    tok_t = pl.BlockSpec((1, GROUP_W, tm), lambda i, t: (i, 0, t))
    in_specs = [
        pl.BlockSpec((1, tm, d), lambda i, t: (i, t, 0)),
        pl.BlockSpec((1, 1, 6 * d), lambda i, t: (i, 0, 0)),
        _full((1, d)),
        _full((d, IN_W)),
        _full((GROUP_W, GROUP_W)), _full((GROUP_W, GROUP_W)), _full((GROUP_W, GROUP_W)), _full((GROUP_W, GROUP_W)),
        _full((HEADS, CHUNK, CHUNK)),
        _full((CHUNK, GROUP_W)),
        _full((VEC_ROWS, GROUP_W)),
    ]
    args = [x, mod, g1, w_in, consts["seg32"], consts["seg64"], consts["cc"], consts["ss"], sguw, sgub, vec]
    if use_rope:
        in_specs += [pl.BlockSpec((tm, GROUP_W), lambda i, t: (t, 0))] * 2
        args += list(rope)
    sd = jax.ShapeDtypeStruct((b, n, GROUP_W), BF16)
    sdt = jax.ShapeDtypeStruct((b, GROUP_W, n), BF16)
    kern = functools.partial(
        _in_kernel, tm=tm, use_rope=use_rope,
        c_scale=(DIFF_D ** -0.5) * LOG2E, d_scale=(HEAD_DIM ** -0.5) * LOG2E)
    return pl.pallas_call(
        kern,
        out_shape=(sd, sd, sd, sd, sdt, sd, sd, sdt, sd),
        grid=(b, n // tm),
        in_specs=in_specs,
        out_specs=(tok, tok, tok, tok, tok_t, tok, tok, tok_t, tok),
        compiler_params=_params("arbitrary", "arbitrary"),
        name="in_proj",
    )(*args)


def _fourier_kernel(cn_ref, sn_ref, zc_ref, zs_ref, seg64_ref, vec_ref, o_ref, *, norm):
    y = (_dot(cn_ref[...], zc_ref[0]) - _dot(sn_ref[...], zs_ref[0])) * norm
    o_ref[0] = (_seg_rms(y, seg64_ref[...], HEAD_DIM) * vec_ref[5:6]).astype(BF16)


def _fourier(zc, zs, cn, sn, seg64, vec, tn):
    b, n, _ = zc.shape
    kern = functools.partial(_fourier_kernel, norm=1.0 / math.sqrt(n * HEAD_DIM))
    return pl.pallas_call(
        kern,
        out_shape=jax.ShapeDtypeStruct((b, n, GROUP_W), BF16),
        grid=(n // tn, b),
        in_specs=[
            pl.BlockSpec((tn, n), lambda t, i: (t, 0)),
            pl.BlockSpec((tn, n), lambda t, i: (t, 0)),
            pl.BlockSpec((1, n, GROUP_W), lambda t, i: (i, 0, 0)),
            pl.BlockSpec((1, n, GROUP_W), lambda t, i: (i, 0, 0)),
            _full((GROUP_W, GROUP_W)),
            _full((VEC_ROWS, GROUP_W)),
        ],
        out_specs=pl.BlockSpec((1, tn, GROUP_W), lambda t, i: (i, t, 0)),
        compiler_params=_params("arbitrary", "arbitrary"),
        name="fourier",
    )(cn, sn, zc, zs, seg64, vec)


def _attn_kernel(*refs, n_src, diff, lam_init, chunk, tq, vec_row, out_scale):
    q_ref = refs[0]
    rest = refs[1 + 2 * n_src:]
    if diff:
        lam_ref = rest[0]
        rest = rest[1:]
    seg64_ref, vec_ref, o_ref = rest

    q = q_ref[0]
    lane = lax.broadcasted_iota(I32, (1, GROUP_W), 1)
    if diff:
        lf = lam_ref[...]
        lam = (jnp.exp(jnp.sum(lf[0:1] * lf[1:2], axis=-1, keepdims=True))
               - jnp.exp(jnp.sum(lf[2:3] * lf[3:4], axis=-1, keepdims=True)) + lam_init)

    chunks = []
    for i in range(n_src):
        kt_ref, v_ref = refs[1 + 2 * i], refs[2 + 2 * i]
        nk = kt_ref.shape[2]
        chunks += [(kt_ref, v_ref, c0, min(chunk, nk - c0)) for c0 in range(0, nk, chunk)]

    width = DIFF_D if diff else HEAD_DIM
    gq = vec_ref[0:1] if diff else vec_ref[2:3]
    gk = vec_ref[1:2] if diff else vec_ref[3:4]
    bound = jnp.max(jnp.abs(gq)) * jnp.max(jnp.abs(gk)) * (math.sqrt(width) * LOG2E * BOUND_SLACK)

    def attend_bounded(sel):
        qm = jnp.where(sel, q, jnp.zeros_like(q))
        l = jnp.zeros((tq, 1), F32)
        acc = jnp.zeros((tq, GROUP_W), F32)
        for kt_ref, v_ref, c0, ck in chunks:
            p = jnp.exp2(_dot(qm, kt_ref[0, :, c0:c0 + ck]) - bound)
            l = l + jnp.sum(p, axis=-1, keepdims=True)
            acc = acc + _dot(p.astype(BF16), v_ref[0, c0:c0 + ck, :])
        return acc * (1.0 / l)

    def attend_online(sel):
        qm = jnp.where(sel, q, jnp.zeros_like(q))
        m = jnp.full((tq, 1), NEG_INF, F32)
        l = jnp.zeros((tq, 1), F32)
        acc = jnp.zeros((tq, GROUP_W), F32)
        for kt_ref, v_ref, c0, ck in chunks:
            s = _dot(qm, kt_ref[0, :, c0:c0 + ck])
            m_new = jnp.maximum(m, jnp.max(s, axis=-1, keepdims=True))
            alpha = jnp.exp2(m - m_new)
            p = jnp.exp2(s - m_new)
            l = alpha * l + jnp.sum(p, axis=-1, keepdims=True)
            acc = alpha * acc + _dot(p.astype(BF16), v_ref[0, c0:c0 + ck, :])
            m = m_new
        return acc * (1.0 / l)

    def run(attend, unroll):
        def body(h, out):
            if diff:
                o = (attend((lane >> (LOG2_HEAD_DIM - 1)) == 2 * h)
                     - lam * attend((lane >> (LOG2_HEAD_DIM - 1)) == 2 * h + 1))
            else:
                o = attend((lane >> LOG2_HEAD_DIM) == h)
            return jnp.where((lane >> LOG2_HEAD_DIM) == h, o, out)

        out = lax.fori_loop(0, HEADS, body, jnp.zeros((tq, GROUP_W), F32), unroll=unroll)
        y = _seg_rms(out, seg64_ref[...], HEAD_DIM) * vec_ref[vec_row:vec_row + 1]
        o_ref[0] = (y * out_scale).astype(BF16)

    small = bound <= SAFE_EXP2_BOUND
    pl.when(small)(lambda: run(attend_bounded, True))
    pl.when(jnp.logical_not(small))(lambda: run(attend_online, False))


def _attention(q, srcs, lam, lam_init, seg64, vec, vec_row, out_scale, tq, chunk=768):
    b, nq, _ = q.shape
    diff = lam is not None
    in_specs = [pl.BlockSpec((1, tq, GROUP_W), lambda i, t: (i, t, 0))]
    args = [q]
    for kt, v in srcs:
        nk = v.shape[1]
        in_specs += [pl.BlockSpec((1, GROUP_W, nk), lambda i, t: (i, 0, 0)),
                     pl.BlockSpec((1, nk, GROUP_W), lambda i, t: (i, 0, 0))]
        args += [kt, v]
    if diff:
        in_specs.append(_full((4, DIFF_D)))
        args.append(lam)
    in_specs += [_full((GROUP_W, GROUP_W)), _full((VEC_ROWS, GROUP_W))]
    args += [seg64, vec]
    kern = functools.partial(_attn_kernel, n_src=len(srcs), diff=diff, lam_init=lam_init, chunk=chunk, tq=tq,
                             vec_row=vec_row, out_scale=out_scale)
    return pl.pallas_call(
        kern,
        out_shape=jax.ShapeDtypeStruct((b, nq, GROUP_W), BF16),
        grid=(b, nq // tq),
        in_specs=in_specs,
        out_specs=pl.BlockSpec((1, tq, GROUP_W), lambda i, t: (i, t, 0)),
        compiler_params=_params("arbitrary", "arbitrary"),
        name="diff_attention" if diff else "ctx_attention",
    )(*args)


def _na_kernel(q_ref, kt_ref, v_ref, ktc_ref, vc_ref, tab_ref, seg64_ref, vec_ref, o_ref, *, n_rows, pairs):
    n_steps = n_rows // NA_QROWS
    tq = NA_QROWS * GRID_W
    lane = lax.broadcasted_iota(I32, (1, GROUP_W), 1)
    head = lane >> LOG2_HEAD_DIM
    ktc = ktc_ref[0]
    vc = vc_ref[0]
    bound = (jnp.max(jnp.abs(vec_ref[2:3])) * jnp.max(jnp.abs(vec_ref[3:4])) * (math.sqrt(HEAD_DIM) * LOG2E * BOUND_SLACK)
             + jnp.max(vec_ref[8:9]))

    def run(bounded):
        for pi in range(pairs):
            t = pl.program_id(1) * pairs + pi
            ws = jnp.clip(NA_QROWS * t - NA_ROWS // 2, 0, n_rows - NA_WIN_ROWS)
            k0 = pl.multiple_of(ws * GRID_W, 128)
            tid = jnp.where(t < 2, t, jnp.where(t < n_steps - 2, 2, t - (n_steps - 5)))
            q = q_ref[0, pi * tq:(pi + 1) * tq, :]
            out = jnp.zeros((tq, GROUP_W), F32)
            for h0 in range(0, HEADS, NA_HEAD_STACK):
                hs = range(h0, h0 + NA_HEAD_STACK)
                qs = jnp.concatenate([jnp.where(head == hh, q, jnp.zeros_like(q)) for hh in hs], axis=0)
                s_loc = (_dot(qs, kt_ref[0, :, pl.ds(k0, NA_WIN)])
                         + tab_ref[tid, h0 * tq:(h0 + NA_HEAD_STACK) * tq, :])
                s_ctx = _dot(qs, ktc)
                if bounded:
                    m = bound
                else:
                    m = jnp.maximum(jnp.max(s_loc, axis=-1, keepdims=True), jnp.max(s_ctx, axis=-1, keepdims=True))
                p_loc = jnp.exp2(s_loc - m)
                p_ctx = jnp.exp2(s_ctx - m)
                l = jnp.sum(p_loc, axis=-1, keepdims=True) + jnp.sum(p_ctx, axis=-1, keepdims=True)
                o = _dot(p_loc.astype(BF16), v_ref[0, pl.ds(k0, NA_WIN), :]) + _dot(p_ctx.astype(BF16), vc)
                o = o * (1.0 / l)
                for i, hh in enumerate(hs):
                    out = jnp.where(head == hh, o[i * tq:(i + 1) * tq], out)
            o_ref[0, pi * tq:(pi + 1) * tq, :] = (
                _seg_rms(out, seg64_ref[...], HEAD_DIM) * vec_ref[7:8]).astype(BF16)

    small = bound <= SAFE_EXP2_BOUND
    pl.when(small)(lambda: run(True))
    pl.when(jnp.logical_not(small))(lambda: run(False))


def _na_attention(q, kt, v, ktc, vc, table, seg64, vec, pairs):
    b, n, _ = q.shape
    nc = vc.shape[1]
    tq = pairs * NA_QROWS * GRID_W
    kern = functools.partial(_na_kernel, n_rows=n // GRID_W, pairs=pairs)
    return pl.pallas_call(
        kern,
        out_shape=jax.ShapeDtypeStruct((b, n, GROUP_W), BF16),
        grid=(b, n // tq),
        in_specs=[
            pl.BlockSpec((1, tq, GROUP_W), lambda i, t: (i, t, 0)),
            pl.BlockSpec((1, GROUP_W, n), lambda i, t: (i, 0, 0)),
            pl.BlockSpec((1, n, GROUP_W), lambda i, t: (i, 0, 0)),
            pl.BlockSpec((1, GROUP_W, nc), lambda i, t: (i, 0, 0)),
            pl.BlockSpec((1, nc, GROUP_W), lambda i, t: (i, 0, 0)),
            _full(table.shape),
            _full((GROUP_W, GROUP_W)),
            _full((VEC_ROWS, GROUP_W)),
        ],
        out_specs=pl.BlockSpec((1, tq, GROUP_W), lambda i, t: (i, t, 0)),
        compiler_params=_params("arbitrary", "arbitrary"),
        name="neighbourhood_attention",
    )(q, kt, v, ktc, vc, table, seg64, vec)


def _out_kernel(ya_ref, yb_ref, yc_ref, yd_ref, w_ref, x_ref, mod_ref, g2_ref, rwt_ref, o_ref, xn_ref, lg_ref):
    y = jnp.concatenate([ya_ref[0], yb_ref[0], yc_ref[0], yd_ref[0]], axis=-1)
    mod = mod_ref[0]
    x = x_ref[0] + mod[:, 2 * D_MODEL:3 * D_MODEL] * _dot(y, w_ref[...])
    o_ref[0] = x
    sh = mod[:, 3 * D_MODEL:4 * D_MODEL]
    sc = mod[:, 4 * D_MODEL:5 * D_MODEL]
    ms = jnp.mean(x * x, axis=-1, keepdims=True)
    h = x * lax.rsqrt(ms + EPS) * g2_ref[...]
    h = (h * (1.0 + sc) + sh).astype(BF16)
    xn_ref[0] = h
    lg_ref[0] = lax.dot_general(rwt_ref[...], h, (((1,), (1,)), ((), ())), preferred_element_type=F32)


def _out_proj(ya, yb, yc, yd, w_out, x, mod, g2, rwt, tm):
    b, n, d = x.shape
    tok = pl.BlockSpec((1, tm, GROUP_W), lambda i, t: (i, t, 0))
    xs = pl.BlockSpec((1, tm, d), lambda i, t: (i, t, 0))
    return pl.pallas_call(
        _out_kernel,
        out_shape=(jax.ShapeDtypeStruct((b, n, d), F32),
                   jax.ShapeDtypeStruct((b, n, d), BF16),
                   jax.ShapeDtypeStruct((b, N_EXPERTS, n), F32)),
        grid=(b, n // tm),
        in_specs=[tok, tok, tok, tok, _full((4 * GROUP_W, d)), xs,
                  pl.BlockSpec((1, 1, 6 * d), lambda i, t: (i, 0, 0)),
                  _full((1, d)), _full((N_EXPERTS, d))],
        out_specs=(xs, xs, pl.BlockSpec((1, N_EXPERTS, tm), lambda i, t: (i, 0, t))),
        compiler_params=_params("arbitrary", "arbitrary"),
        name="out_proj",
    )(ya, yb, yc, yd, w_out, x, mod, g2, rwt)


def _cumsum_excl(m, tri):
    n = m.shape[1]
    carry = jnp.zeros((m.shape[0], 1), F32)
    outs = []
    for j in range(n // GROUP_W):
        blk = m[:, j * GROUP_W:(j + 1) * GROUP_W]
        inc = _dot(blk.astype(BF16), tri)
        outs.append(inc - blk + carry)
        carry = carry + inc[:, GROUP_W - 1:GROUP_W]
    return jnp.concatenate(outs, axis=1)


def _route_kernel(lg_ref, tri_ref, pos_ref, gate_ref, post_ref, *, n, cap):
    lg = lg_ref[0]
    e = jnp.exp(lg - jnp.max(lg, axis=0, keepdims=True))
    aff = e / jnp.sum(e, axis=0, keepdims=True)
    gate_ref[0] = aff

    def unresolved(state):
        lo, hi = state
        return jnp.max(jnp.where(lo < hi, 1.0, 0.0)) > 0.0

    def bisect(state):
        lo, hi = state
        mid = 0.5 * (lo + hi)
        mid = jnp.where(mid > lo, mid, hi)
        ge = aff >= mid
        cnt = jnp.sum(jnp.where(ge, 1.0, 0.0), axis=1, keepdims=True)
        least_ge = jnp.min(jnp.where(ge, aff, jnp.inf), axis=1, keepdims=True)
        most_lt = jnp.max(jnp.where(ge, NEG_INF, aff), axis=1, keepdims=True)
        up = cnt >= cap
        return jnp.where(up, least_ge, lo), jnp.where(up, hi, most_lt)

    thr, _ = lax.while_loop(unresolved, bisect, (jnp.min(aff, axis=1, keepdims=True),
                                                 jnp.max(aff, axis=1, keepdims=True)))
    gt = aff > thr
    eq = aff == thr
    need = cap - jnp.sum(jnp.where(gt, 1.0, 0.0), axis=1, keepdims=True)
    tri = tri_ref[...]
    rank_eq = _cumsum_excl(jnp.where(eq, 1.0, 0.0), tri)
    sel = jnp.where(gt, 1.0, jnp.where(eq, jnp.where(rank_eq < need, 1.0, 0.0), 0.0))
    pos = jnp.where(sel > 0.0, _cumsum_excl(sel, tri), -1.0)
    pos_ref[0] = pos.astype(I32)
    padded = jnp.concatenate([pos, jnp.full((LANES - N_EXPERTS, n), -1.0, F32)], axis=0)
    post_ref[0] = padded.T.astype(I32)


def _route(logits, tri, cap):
    b, _, n = logits.shape
    em = pl.BlockSpec((1, N_EXPERTS, n), lambda i: (i, 0, 0))
    kern = functools.partial(_route_kernel, n=n, cap=cap)
    return pl.pallas_call(
        kern,
        out_shape=(jax.ShapeDtypeStruct((b, N_EXPERTS, n), I32),
                   jax.ShapeDtypeStruct((b, N_EXPERTS, n), F32),
                   jax.ShapeDtypeStruct((b, n, LANES), I32)),
        grid=(b,),
        in_specs=[em, _full((GROUP_W, GROUP_W))],
        out_specs=(em, em, pl.BlockSpec((1, n, LANES), lambda i: (i, 0, 0))),
        compiler_params=_params("arbitrary"),
        name="router",
    )(logits, tri)


def _ffn_kernel(xn_ref, pos_ref, gate_ref, wg32_ref, wu32_ref, wd32_ref, o_ref, wg_ref, wu_ref, wd_ref, *, bb, cap, n):
    @pl.when(pl.program_id(1) == 0)
    def _():
        wg_ref[...] = wg32_ref[0, 0].astype(BF16)
        wu_ref[...] = wu32_ref[0, 0].astype(BF16)
        wd_ref[...] = wd32_ref[0, 0].astype(BF16)

    slot = lax.broadcasted_iota(I32, (cap, n), 0)
    xs, gs = [], []
    for i in range(bb):
        hit = pos_ref[i, 0] == slot
        onehot = jnp.where(hit, 1.0, 0.0).astype(BF16)
        gs.append(jnp.sum(jnp.where(hit, gate_ref[i, 0], 0.0), axis=1, keepdims=True))
        xs.append(_dot(onehot, xn_ref[i]).astype(BF16))
    xg = xs[0] if bb == 1 else jnp.concatenate(xs, axis=0)
    g = gs[0] if bb == 1 else jnp.concatenate(gs, axis=0)
    hid = (jax.nn.silu(_dot(xg, wg_ref[...])) * _dot(xg, wu_ref[...])).astype(BF16)
    o = _dot(hid, wd_ref[...]) * g
    for i in range(bb):
        o_ref[i, 0] = o[i * cap:(i + 1) * cap].astype(BF16)


def _expert_ffn(xn, pos, gate, wg, wu, wd, layer, cap, bb):
    b, n, d = xn.shape
    pos4 = pos.reshape(b, N_EXPERTS, 1, n)
    gate4 = gate.reshape(b, N_EXPERTS, 1, n)
    wspec = pl.BlockSpec((1, 1, d, d), lambda e, j: (layer, e, 0, 0))
    sel = pl.BlockSpec((bb, 1, 1, n), lambda e, j: (j, e, 0, 0))
    kern = functools.partial(_ffn_kernel, bb=bb, cap=cap, n=n)
    return pl.pallas_call(
        kern,
        out_shape=jax.ShapeDtypeStruct((b, N_EXPERTS, cap, d), BF16),
        grid=(N_EXPERTS, b // bb),
        in_specs=[pl.BlockSpec((bb, n, d), lambda e, j: (j, 0, 0)), sel, sel, wspec, wspec, wspec],
        out_specs=pl.BlockSpec((bb, 1, cap, d), lambda e, j: (j, e, 0, 0)),
        scratch_shapes=[pltpu.VMEM((d, d), BF16)] * 3,
        compiler_params=_params("arbitrary", "arbitrary"),
        name="expert_ffn",
    )(xn, pos4, gate4, wg, wu, wd)


def _scatter_kernel(x_ref, mod_ref, pt_ref, o_ref, out_ref, *, tn, cap):
    pos_t = pt_ref[0]
    if cap % 128 == 0:
        slot = lax.broadcasted_iota(I32, (tn, cap), 1)
        onehot = jnp.concatenate(
            [jnp.where(pos_t[:, e:e + 1] == slot, 1.0, 0.0).astype(BF16) for e in range(N_EXPERTS)], axis=1)
    else:
        slot = lax.broadcasted_iota(I32, (tn, N_EXPERTS * cap), 1)
        acc = jnp.zeros((tn, N_EXPERTS * cap), F32)
        for e in range(N_EXPERTS):
            pe = pos_t[:, e:e + 1]
            acc = jnp.where(jnp.where(pe >= 0, pe + e * cap, -1) == slot, 1.0, acc)
        onehot = acc.astype(BF16)
    y = _dot(onehot, o_ref[0].reshape(N_EXPERTS * cap, D_MODEL))
    g = mod_ref[0][:, 5 * D_MODEL:6 * D_MODEL]
    out_ref[0] = x_ref[0] + g * y


def _scatter(x, mod, pos_t, o, cap, tn):
    b, n, d = x.shape
    xs = pl.BlockSpec((1, tn, d), lambda i, t: (i, t, 0))
    kern = functools.partial(_scatter_kernel, tn=tn, cap=cap)
    return pl.pallas_call(
        kern,
        out_shape=jax.ShapeDtypeStruct((b, n, d), F32),
        grid=(b, n // tn),
        in_specs=[xs,
                  pl.BlockSpec((1, 1, 6 * d), lambda i, t: (i, 0, 0)),
                  pl.BlockSpec((1, tn, LANES), lambda i, t: (i, t, 0)),
                  pl.BlockSpec((1, N_EXPERTS, cap, d), lambda i, t: (i, 0, 0, 0))],
        out_specs=xs,
        compiler_params=_params("arbitrary", "arbitrary"),
        name="scatter_add",
    )(x, mod, pos_t, o)


@functools.lru_cache(maxsize=None)
def _np_consts():
    lane = np.arange(GROUP_W)
    seg32 = (lane[:, None] // DIFF_D == lane[None, :] // DIFF_D).astype(np.float32)
    seg64 = (lane[:, None] // HEAD_DIM == lane[None, :] // HEAD_DIM).astype(np.float32)
    ang = 2.0 * np.pi * ((lane[:, None] % HEAD_DIM) * (lane[None, :] % HEAD_DIM) % HEAD_DIM) / HEAD_DIM
    cc = np.cos(ang) * seg64
    ss = np.sin(ang) * seg64
    tri = (lane[:, None] <= lane[None, :]).astype(np.float32)
    return dict(seg32=seg32, seg64=seg64, cc=cc, ss=ss, tri=tri)


@functools.lru_cache(maxsize=None)
def _np_dft(n):
    idx = (np.arange(n, dtype=np.int64)[:, None] * np.arange(n, dtype=np.int64)[None, :]) % n
    ang = 2.0 * np.pi * idx.astype(np.float64) / n
    return np.cos(ang).astype(np.float32), np.sin(ang).astype(np.float32)


@functools.lru_cache(maxsize=None)
def _np_rope(n):
    half = DIFF_D // 2
    inv = 1.0 / (ROPE_BASE ** (np.arange(0, half, 2, dtype=np.float32) / half))
    t = np.arange(n)
    row = (t // GRID_W).astype(np.float32)[:, None] * inv
    col = (t % GRID_W).astype(np.float32)[:, None] * inv
    nf = inv.shape[0]
    d = np.arange(GROUP_W) % DIFF_D
    f = d % nf
    is_col = d >= half
    second = (d % half) >= nf
    ang = np.where(is_col[None, :], col[:, f], row[:, f])
    c = np.cos(ang).astype(np.float32)
    s = np.sin(ang).astype(np.float32)
    s = np.where(second[None, :], s, -s)
    return c, s


@functools.lru_cache(maxsize=None)
def _np_na_index(n_rows):
    n_steps = n_rows // NA_QROWS
    reps = [0, 1, 2, n_steps - 2, n_steps - 1]
    tq = NA_QROWS * GRID_W
    roff = np.zeros((len(reps), NA_QROWS, NA_WIN_ROWS), np.int32)
    valid = np.zeros((len(reps), tq, NA_WIN), bool)
    for ci, t in enumerate(reps):
        ws = int(np.clip(NA_QROWS * t - NA_ROWS // 2, 0, n_rows - NA_WIN_ROWS))
        qi = np.arange(tq)
        r = NA_QROWS * t + qi // GRID_W
        qcol = qi % GRID_W
        kk = np.arange(NA_WIN)
        krow = ws + kk // GRID_W
        kcol = kk % GRID_W
        rstart = np.clip(r - NA_ROWS // 2, 0, n_rows - NA_ROWS)
        wstart = np.clip(qcol - NA_COLS // 2, 0, GRID_W - NA_COLS)
        vr = (krow[None, :] >= rstart[:, None]) & (krow[None, :] < rstart[:, None] + NA_ROWS)
        vc = (kcol[None, :] >= wstart[:, None]) & (kcol[None, :] < wstart[:, None] + NA_COLS)
        valid[ci] = vr & vc
        rows_q = NA_QROWS * t + np.arange(NA_QROWS)
        rows_k = ws + np.arange(NA_WIN_ROWS)
        roff[ci] = np.clip(rows_k[None, :] - rows_q[:, None] + NA_ROWS - 1, 0, 2 * NA_ROWS - 2)
    return roff, valid


def _na_table(rpb, n_rows):
    roff, valid = _np_na_index(n_rows)
    n_cls = roff.shape[0]
    padw = GRID_W - NA_COLS
    padded = jnp.pad(rpb.astype(F32), ((0, 0), (0, 0), (padw, padw)))
    toep = jnp.stack([padded[:, :, GRID_W - 1 - q:2 * GRID_W - 1 - q] for q in range(GRID_W)], axis=2)
    blocks = jnp.take(toep, roff.reshape(-1), axis=1)
    blocks = blocks.reshape(HEADS, n_cls, NA_QROWS, NA_WIN_ROWS, GRID_W, GRID_W)
    bias = blocks.transpose(1, 0, 2, 4, 3, 5).reshape(n_cls, HEADS, NA_QROWS * GRID_W, NA_WIN)
    bias = jnp.where(valid[:, None], bias * LOG2E, NEG_INF)
    return bias.reshape(n_cls, HEADS * NA_QROWS * GRID_W, NA_WIN)


def _moe(x, xn, logits, mod, tri, wg, wu, wd, layer, bb, tn):
    n = x.shape[1]
    cap = CAPACITY_FACTOR * n // N_EXPERTS
    pos, gate, pos_t = _route(logits, tri, cap)
    o = _expert_ffn(xn, pos, gate, wg, wu, wd, layer, cap, bb)
    return _scatter(x, mod, pos_t, o, cap, tn)


def kernel(x, c, ctx, c_ctx, ada_w, ada_b, norm1_g, norm2_g, w_in, w_out, head_out_g, sgu_w, sgu_b, diff_qn_g, diff_kn_g, diff_lambda, na_qn_g, na_kn_g, na_rpb, router_w, exp_w_gate, exp_w_up, exp_w_down):
    b, n, d = x.shape
    n_ctx = ctx.shape[1]
    npc = _np_consts()
    consts = {k: jnp.asarray(v, F32).astype(BF16) for k, v in npc.items()}
    seg64, tri = consts["seg64"], consts["tri"]
    cn, sn = (jnp.asarray(a, F32).astype(BF16) for a in _np_dft(n))
    cn_c, sn_c = (jnp.asarray(a, F32).astype(BF16) for a in _np_dft(n_ctx))
    rope = tuple(jnp.asarray(a, F32) for a in _np_rope(n))

    pad = (-(b + 1)) % 8
    c_rows = jnp.concatenate([c, c_ctx[None, :], jnp.zeros((pad, d), F32)], axis=0)
    mod_all = _modulation(c_rows, ada_w, ada_b)

    xc = ctx
    for l in range(DEPTH):
        last = l == DEPTH - 1
        lam_init = 0.8 - 0.6 * math.exp(-0.3 * l)
        mod = mod_all[l, :b][:, None, :]
        mod_c = jnp.broadcast_to(mod_all[l, b][None, None, :], (b, 1, 6 * d))
        g1 = norm1_g[l][None, :]
        g2 = norm2_g[l][None, :]
        w_in_l = w_in[l].astype(BF16)
        w_out_l = w_out[l].astype(BF16)
        sguw = sgu_w[l].astype(BF16)
        sgub = jnp.repeat(sgu_b[l].T, HEAD_DIM, axis=1)
        hg = head_out_g[l].reshape(4, GROUP_W)
        vec = jnp.stack([jnp.tile(diff_qn_g[l], GROUP_W // DIFF_D), jnp.tile(diff_kn_g[l], GROUP_W // DIFF_D),
                         jnp.tile(na_qn_g[l], HEADS), jnp.tile(na_kn_g[l], HEADS),
                         hg[0], hg[1], hg[2], hg[3],
                         jnp.broadcast_to(jnp.max(jnp.abs(na_rpb[l])) * LOG2E, (GROUP_W,))], axis=0).astype(F32)
        vec = jnp.pad(vec, ((0, VEC_ROWS - vec.shape[0]), (0, 0)))
        lam = diff_lambda[l].astype(F32)
        table = _na_table(na_rpb[l], n // GRID_W)
        rwt = router_w[l].T.astype(BF16)
        experts = (exp_w_gate, exp_w_up, exp_w_down)

        ya, zc, zs, qc, kct, vc, qd, kdt, vd = _in_proj(x, mod, g1, w_in_l, consts, sguw, sgub, vec, rope, 512)
        ya_c, zc_c, zs_c, qc_c, kct_c, vc_c, qd_c, kdt_c, vd_c = _in_proj(
            xc, mod_c, g1, w_in_l, consts, sguw, sgub, vec, None, n_ctx)

        yb = _fourier(zc, zs, cn, sn, seg64, vec, 1024)
        yc = _attention(qc, [(kct, vc), (kct_c, vc_c)], lam, lam_init, seg64, vec, 6, 1.0 - lam_init, 256)
        yd = _na_attention(qd, kdt, vd, kdt_c, vd_c, table, seg64, vec, 4)
        x, xn, logits = _out_proj(ya, yb, yc, yd, w_out_l, x, mod, g2, rwt, 512)
        x = _moe(x, xn, logits, mod, tri, *experts, l, 1, 512)

        if not last:
            yb_c = _fourier(zc_c, zs_c, cn_c, sn_c, seg64, vec, n_ctx)
            yc_c = _attention(qc_c, [(kct_c, vc_c)], lam, lam_init, seg64, vec, 6, 1.0 - lam_init, n_ctx)
            yd_c = _attention(qd_c, [(kdt_c, vd_c)], None, 0.0, seg64, vec, 7, 1.0, n_ctx)
            xc, xn_c, logits_c = _out_proj(ya_c, yb_c, yc_c, yd_c, w_out_l, xc, mod_c, g2, rwt, n_ctx)
            xc = _moe(xc, xn_c, logits_c, mod_c, tri, *experts, l, b, n_ctx)
    return x
```

```python
import functools
import math

import numpy as np
import jax
import jax.numpy as jnp
from jax import lax
from jax.experimental import pallas as pl
from jax.experimental.pallas import tpu as pltpu

F32 = jnp.float32
BF16 = jnp.bfloat16
I32 = jnp.int32

D_MODEL = 1024
DEPTH = 2
GRID_W = 64
LANES = 128
HEAD_DIM = 64
LOG2_HEAD_DIM = 6
GROUP_W = 256
HEADS = GROUP_W // HEAD_DIM
CHUNK = 128
DIFF_D = HEAD_DIM // 2
NA_ROWS = 8
NA_COLS = 16
N_EXPERTS = 16
CAPACITY_FACTOR = 2
ROPE_BASE = 10000.0
EPS = 1e-6
IN_W = 9 * GROUP_W
LOG2E = 1.4426950408889634

VMEM_LIMIT_BYTES = 56 * 1024 * 1024
NA_QROWS = 2
NA_WIN_ROWS = NA_ROWS + 2
NA_WIN = NA_WIN_ROWS * GRID_W
NA_HEAD_STACK = 1
NEG_INF = float("-inf")
VEC_ROWS = 16
SAFE_EXP2_BOUND = 48.0
BOUND_SLACK = 1.02


def _dot(a, b):
    return jnp.dot(a, b, preferred_element_type=F32)


def _params(*sem):
    return pltpu.CompilerParams(dimension_semantics=sem, vmem_limit_bytes=VMEM_LIMIT_BYTES)


def _full(shape):
    nd = len(shape)
    return pl.BlockSpec(shape, lambda *_: (0,) * nd)


def _seg_rms(x, seg, width):
    ss = _dot((x * x).astype(BF16), seg)
    return x * lax.rsqrt(ss * (1.0 / width) + EPS)


def _mod_kernel(c_ref, w_ref, b_ref, o_ref):
    s = jax.nn.silu(c_ref[...]).astype(BF16)
    o_ref[0] = _dot(s, w_ref[0].astype(BF16)) + b_ref[0]


def _modulation(c_rows, ada_w, ada_b):
    depth, d, w6 = ada_w.shape
    r = c_rows.shape[0]
    tn = 1024
    return pl.pallas_call(
        _mod_kernel,
        out_shape=jax.ShapeDtypeStruct((depth, r, w6), F32),
        grid=(depth, w6 // tn),
        in_specs=[
            pl.BlockSpec((r, d), lambda l, j: (0, 0)),
            pl.BlockSpec((1, d, tn), lambda l, j: (l, 0, j)),
            pl.BlockSpec((1, 1, tn), lambda l, j: (l, 0, j)),
        ],
        out_specs=pl.BlockSpec((1, r, tn), lambda l, j: (l, 0, j)),
        compiler_params=_params("arbitrary", "arbitrary"),
        name="modulation",
    )(c_rows, ada_w, ada_b.reshape(depth, 1, w6))


def _rope(x, c, s, lane):
    fwd = pltpu.roll(x, GROUP_W - 8, 1)
    bwd = pltpu.roll(x, 8, 1)
    partner = jnp.where((lane & 8) == 0, fwd, bwd)
    return x * c + partner * s


def _in_kernel(*refs, tm, use_rope, c_scale, d_scale):
    (x_ref, mod_ref, g1_ref, w_ref, seg32_ref, seg64_ref, cc_ref, ss_ref, sguw_ref, sgub_ref, vec_ref) = refs[:11]
    rest = refs[11:]
    if use_rope:
        ropec_ref, ropes_ref = rest[:2]
        rest = rest[2:]
    ya_ref, zc_ref, zs_ref, qc_ref, kct_ref, vc_ref, qd_ref, kdt_ref, vd_ref = rest

    x = x_ref[0]
    mod = mod_ref[0]
    sh = mod[:, 0:D_MODEL]
    sc = mod[:, D_MODEL:2 * D_MODEL]
    ms = jnp.mean(x * x, axis=-1, keepdims=True)
    h = x * lax.rsqrt(ms + EPS) * g1_ref[...]
    h = (h * (1.0 + sc) + sh).astype(BF16)
    lane = lax.broadcasted_iota(I32, (1, GROUP_W), 1)
    head = lane >> LOG2_HEAD_DIM
    seg32 = seg32_ref[...]
    seg64 = seg64_ref[...]
    vec = vec_ref[...]

    z = jax.nn.gelu(_dot(h, w_ref[:, 0:2 * GROUP_W]))
    u = z[:, 0:GROUP_W]
    vn = _seg_rms(z[:, GROUP_W:2 * GROUP_W], seg64, HEAD_DIM).astype(BF16)
    rows = []
    for c in range(tm // CHUNK):
        vch = vn[c * CHUNK:(c + 1) * CHUNK]
        sv = jnp.zeros((CHUNK, GROUP_W), F32)
        for hh in range(HEADS):
            sv = jnp.where(head == hh, _dot(sguw_ref[hh], vch), sv)
        rows.append(sv + sgub_ref[...])
    ya = u * jnp.concatenate(rows, axis=0)
    ya_ref[0] = (_seg_rms(ya, seg64, HEAD_DIM) * vec[4:5]).astype(BF16)

    zb = _dot(h, w_ref[:, 2 * GROUP_W:3 * GROUP_W]).astype(BF16)
    zc_ref[0] = _dot(zb, cc_ref[...]).astype(BF16)
    zs_ref[0] = _dot(zb, ss_ref[...]).astype(BF16)

    pc = _dot(h, w_ref[:, 3 * GROUP_W:6 * GROUP_W])
    q = _seg_rms(pc[:, 0:GROUP_W], seg32, DIFF_D) * vec[0:1]
    k = _seg_rms(pc[:, GROUP_W:2 * GROUP_W], seg32, DIFF_D) * vec[1:2]
    if use_rope:
        rc = ropec_ref[...]
        rs = ropes_ref[...]
        q = _rope(q, rc, rs, lane)
        k = _rope(k, rc, rs, lane)
    qc_ref[0] = (q * c_scale).astype(BF16)
    kct_ref[0] = k.T.astype(BF16)
    vc_ref[0] = pc[:, 2 * GROUP_W:3 * GROUP_W].astype(BF16)

    pd = _dot(h, w_ref[:, 6 * GROUP_W:9 * GROUP_W])
    qd = _seg_rms(pd[:, 0:GROUP_W], seg64, HEAD_DIM) * vec[2:3]
    kd = _seg_rms(pd[:, GROUP_W:2 * GROUP_W], seg64, HEAD_DIM) * vec[3:4]
    qd_ref[0] = (qd * d_scale).astype(BF16)
    kdt_ref[0] = kd.T.astype(BF16)
    vd_ref[0] = pd[:, 2 * GROUP_W:3 * GROUP_W].astype(BF16)


def _in_proj(x, mod, g1, w_in, consts, sguw, sgub, vec, rope, tm):
    b, n, d = x.shape
    use_rope = rope is not None
    tok = pl.BlockSpec((1, tm, GROUP_W), lambda i, t: (i, t, 0))
    tok_t = pl.BlockSpec((1, GROUP_W, tm), lambda i, t: (i, 0, t))
    in_specs = [
        pl.BlockSpec((1, tm, d), lambda i, t: (i, t, 0)),
        pl.BlockSpec((1, 1, 6 * d), lambda i, t: (i, 0, 0)),
        _full((1, d)),
        _full((d, IN_W)),
        _full((GROUP_W, GROUP_W)), _full((GROUP_W, GROUP_W)), _full((GROUP_W, GROUP_W)), _full((GROUP_W, GROUP_W)),
        _full((HEADS, CHUNK, CHUNK)),
        _full((CHUNK, GROUP_W)),
        _full((VEC_ROWS, GROUP_W)),
    ]
    args = [x, mod, g1, w_in, consts["seg32"], consts["seg64"], consts["cc"], consts["ss"], sguw, sgub, vec]
    if use_rope:
        in_specs += [pl.BlockSpec((tm, GROUP_W), lambda i, t: (t, 0))] * 2
        args += list(rope)
    sd = jax.ShapeDtypeStruct((b, n, GROUP_W), BF16)
    sdt = jax.ShapeDtypeStruct((b, GROUP_W, n), BF16)
    kern = functools.partial(
        _in_kernel, tm=tm, use_rope=use_rope,
        c_scale=(DIFF_D ** -0.5) * LOG2E, d_scale=(HEAD_DIM ** -0.5) * LOG2E)
    return pl.pallas_call(
        kern,
        out_shape=(sd, sd, sd, sd, sdt, sd, sd, sdt, sd),
        grid=(b, n // tm),
        in_specs=in_specs,
        out_specs=(tok, tok, tok, tok, tok_t, tok, tok, tok_t, tok),
        compiler_params=_params("arbitrary", "arbitrary"),
        name="in_proj",
    )(*args)


def _fourier_kernel(cn_ref, sn_ref, zc_ref, zs_ref, seg64_ref, vec_ref, o_ref, *, norm):
    y = (_dot(cn_ref[...], zc_ref[0]) - _dot(sn_ref[...], zs_ref[0])) * norm
    o_ref[0] = (_seg_rms(y, seg64_ref[...], HEAD_DIM) * vec_ref[5:6]).astype(BF16)


def _fourier(zc, zs, cn, sn, seg64, vec, tn):
    b, n, _ = zc.shape
    kern = functools.partial(_fourier_kernel, norm=1.0 / math.sqrt(n * HEAD_DIM))
    return pl.pallas_call(
        kern,
        out_shape=jax.ShapeDtypeStruct((b, n, GROUP_W), BF16),
        grid=(n // tn, b),
        in_specs=[
            pl.BlockSpec((tn, n), lambda t, i: (t, 0)),
            pl.BlockSpec((tn, n), lambda t, i: (t, 0)),
            pl.BlockSpec((1, n, GROUP_W), lambda t, i: (i, 0, 0)),
            pl.BlockSpec((1, n, GROUP_W), lambda t, i: (i, 0, 0)),
            _full((GROUP_W, GROUP_W)),
            _full((VEC_ROWS, GROUP_W)),
        ],
        out_specs=pl.BlockSpec((1, tn, GROUP_W), lambda t, i: (i, t, 0)),
        compiler_params=_params("arbitrary", "arbitrary"),
        name="fourier",
    )(cn, sn, zc, zs, seg64, vec)


def _attn_kernel(*refs, n_src, diff, lam_init, chunk, tq, vec_row, out_scale):
    q_ref = refs[0]
    rest = refs[1 + 2 * n_src:]
    if diff:
        lam_ref = rest[0]
        rest = rest[1:]
    seg64_ref, vec_ref, o_ref = rest

    q = q_ref[0]
    lane = lax.broadcasted_iota(I32, (1, GROUP_W), 1)
    if diff:
        lf = lam_ref[...]
        lam = (jnp.exp(jnp.sum(lf[0:1] * lf[1:2], axis=-1, keepdims=True))
               - jnp.exp(jnp.sum(lf[2:3] * lf[3:4], axis=-1, keepdims=True)) + lam_init)

    chunks = []
    for i in range(n_src):
        kt_ref, v_ref = refs[1 + 2 * i], refs[2 + 2 * i]
        nk = kt_ref.shape[2]
        chunks += [(kt_ref, v_ref, c0, min(chunk, nk - c0)) for c0 in range(0, nk, chunk)]

    width = DIFF_D if diff else HEAD_DIM
    gq = vec_ref[0:1] if diff else vec_ref[2:3]
    gk = vec_ref[1:2] if diff else vec_ref[3:4]
    bound = jnp.max(jnp.abs(gq)) * jnp.max(jnp.abs(gk)) * (math.sqrt(width) * LOG2E * BOUND_SLACK)

    def attend_bounded(sel):
        qm = jnp.where(sel, q, jnp.zeros_like(q))
        l = jnp.zeros((tq, 1), F32)
        acc = jnp.zeros((tq, GROUP_W), F32)
        for kt_ref, v_ref, c0, ck in chunks:
            p = jnp.exp2(_dot(qm, kt_ref[0, :, c0:c0 + ck]) - bound)
            l = l + jnp.sum(p, axis=-1, keepdims=True)
            acc = acc + _dot(p.astype(BF16), v_ref[0, c0:c0 + ck, :])
        return acc * (1.0 / l)

    def attend_online(sel):
        qm = jnp.where(sel, q, jnp.zeros_like(q))
        m = jnp.full((tq, 1), NEG_INF, F32)
        l = jnp.zeros((tq, 1), F32)
        acc = jnp.zeros((tq, GROUP_W), F32)
        for kt_ref, v_ref, c0, ck in chunks:
            s = _dot(qm, kt_ref[0, :, c0:c0 + ck])
            m_new = jnp.maximum(m, jnp.max(s, axis=-1, keepdims=True))
            alpha = jnp.exp2(m - m_new)
            p = jnp.exp2(s - m_new)
            l = alpha * l + jnp.sum(p, axis=-1, keepdims=True)
            acc = alpha * acc + _dot(p.astype(BF16), v_ref[0, c0:c0 + ck, :])
            m = m_new
        return acc * (1.0 / l)

    def head_out(attend, h):
        if diff:
            return (attend((lane >> (LOG2_HEAD_DIM - 1)) == 2 * h)
                    - lam * attend((lane >> (LOG2_HEAD_DIM - 1)) == 2 * h + 1))
        return attend((lane >> LOG2_HEAD_DIM) == h)

    def finish(out):
        y = _seg_rms(out, seg64_ref[...], HEAD_DIM) * vec_ref[vec_row:vec_row + 1]
        o_ref[0] = (y * out_scale).astype(BF16)

    def run_bounded():
        out = jnp.zeros((tq, GROUP_W), F32)
        for h in range(HEADS):
            out = jnp.where((lane >> LOG2_HEAD_DIM) == h, head_out(attend_bounded, h), out)
        finish(out)

    def run_online():
        def body(h, out):
            return jnp.where((lane >> LOG2_HEAD_DIM) == h, head_out(attend_online, h), out)

        finish(lax.fori_loop(0, HEADS, body, jnp.zeros((tq, GROUP_W), F32)))

    small = bound <= SAFE_EXP2_BOUND
    pl.when(small)(run_bounded)
    pl.when(jnp.logical_not(small))(run_online)


def _attention(q, srcs, lam, lam_init, seg64, vec, vec_row, out_scale, tq, chunk=768):
    b, nq, _ = q.shape
    diff = lam is not None
    in_specs = [pl.BlockSpec((1, tq, GROUP_W), lambda i, t: (i, t, 0))]
    args = [q]
    for kt, v in srcs:
        nk = v.shape[1]
        in_specs += [pl.BlockSpec((1, GROUP_W, nk), lambda i, t: (i, 0, 0)),
                     pl.BlockSpec((1, nk, GROUP_W), lambda i, t: (i, 0, 0))]
        args += [kt, v]
    if diff:
        in_specs.append(_full((4, DIFF_D)))
        args.append(lam)
    in_specs += [_full((GROUP_W, GROUP_W)), _full((VEC_ROWS, GROUP_W))]
    args += [seg64, vec]
    kern = functools.partial(_attn_kernel, n_src=len(srcs), diff=diff, lam_init=lam_init, chunk=chunk, tq=tq,
                             vec_row=vec_row, out_scale=out_scale)
    return pl.pallas_call(
        kern,
        out_shape=jax.ShapeDtypeStruct((b, nq, GROUP_W), BF16),
        grid=(b, nq // tq),
        in_specs=in_specs,
        out_specs=pl.BlockSpec((1, tq, GROUP_W), lambda i, t: (i, t, 0)),
        compiler_params=_params("arbitrary", "arbitrary"),
        name="diff_attention" if diff else "ctx_attention",
    )(*args)


def _na_kernel(q_ref, kt_ref, v_ref, ktc_ref, vc_ref, tab_ref, seg64_ref, vec_ref, o_ref, *, n_rows, pairs):
    n_steps = n_rows // NA_QROWS
    tq = NA_QROWS * GRID_W
    lane = lax.broadcasted_iota(I32, (1, GROUP_W), 1)
    head = lane >> LOG2_HEAD_DIM
    ktc = ktc_ref[0]
    vc = vc_ref[0]
    bound = (jnp.max(jnp.abs(vec_ref[2:3])) * jnp.max(jnp.abs(vec_ref[3:4])) * (math.sqrt(HEAD_DIM) * LOG2E * BOUND_SLACK)
             + jnp.max(vec_ref[8:9]))

    def run(bounded):
        for pi in range(pairs):
            t = pl.program_id(1) * pairs + pi
            ws = jnp.clip(NA_QROWS * t - NA_ROWS // 2, 0, n_rows - NA_WIN_ROWS)
            k0 = pl.multiple_of(ws * GRID_W, 128)
            tid = jnp.where(t < 2, t, jnp.where(t < n_steps - 2, 2, t - (n_steps - 5)))
            q = q_ref[0, pi * tq:(pi + 1) * tq, :]
            out = jnp.zeros((tq, GROUP_W), F32)
            for h0 in range(0, HEADS, NA_HEAD_STACK):
                hs = range(h0, h0 + NA_HEAD_STACK)
                qs = jnp.concatenate([jnp.where(head == hh, q, jnp.zeros_like(q)) for hh in hs], axis=0)
                s_loc = (_dot(qs, kt_ref[0, :, pl.ds(k0, NA_WIN)])
                         + tab_ref[tid, h0 * tq:(h0 + NA_HEAD_STACK) * tq, :])
                s_ctx = _dot(qs, ktc)
                if bounded:
                    m = bound
                else:
                    m = jnp.maximum(jnp.max(s_loc, axis=-1, keepdims=True), jnp.max(s_ctx, axis=-1, keepdims=True))
                p_loc = jnp.exp2(s_loc - m)
                p_ctx = jnp.exp2(s_ctx - m)
                l = jnp.sum(p_loc, axis=-1, keepdims=True) + jnp.sum(p_ctx, axis=-1, keepdims=True)
                o = _dot(p_loc.astype(BF16), v_ref[0, pl.ds(k0, NA_WIN), :]) + _dot(p_ctx.astype(BF16), vc)
                o = o * (1.0 / l)
                for i, hh in enumerate(hs):
                    out = jnp.where(head == hh, o[i * tq:(i + 1) * tq], out)
            o_ref[0, pi * tq:(pi + 1) * tq, :] = (
                _seg_rms(out, seg64_ref[...], HEAD_DIM) * vec_ref[7:8]).astype(BF16)

    small = bound <= SAFE_EXP2_BOUND
    pl.when(small)(lambda: run(True))
    pl.when(jnp.logical_not(small))(lambda: run(False))


def _na_attention(q, kt, v, ktc, vc, table, seg64, vec, pairs):
    b, n, _ = q.shape
    nc = vc.shape[1]
    tq = pairs * NA_QROWS * GRID_W
    kern = functools.partial(_na_kernel, n_rows=n // GRID_W, pairs=pairs)
    return pl.pallas_call(
        kern,
        out_shape=jax.ShapeDtypeStruct((b, n, GROUP_W), BF16),
        grid=(b, n // tq),
        in_specs=[
            pl.BlockSpec((1, tq, GROUP_W), lambda i, t: (i, t, 0)),
            pl.BlockSpec((1, GROUP_W, n), lambda i, t: (i, 0, 0)),
            pl.BlockSpec((1, n, GROUP_W), lambda i, t: (i, 0, 0)),
            pl.BlockSpec((1, GROUP_W, nc), lambda i, t: (i, 0, 0)),
            pl.BlockSpec((1, nc, GROUP_W), lambda i, t: (i, 0, 0)),
            _full(table.shape),
            _full((GROUP_W, GROUP_W)),
            _full((VEC_ROWS, GROUP_W)),
        ],
        out_specs=pl.BlockSpec((1, tq, GROUP_W), lambda i, t: (i, t, 0)),
        compiler_params=_params("arbitrary", "arbitrary"),
        name="neighbourhood_attention",
    )(q, kt, v, ktc, vc, table, seg64, vec)


def _out_kernel(ya_ref, yb_ref, yc_ref, yd_ref, w_ref, x_ref, mod_ref, g2_ref, rwt_ref, o_ref, xn_ref, lg_ref):
    y = jnp.concatenate([ya_ref[0], yb_ref[0], yc_ref[0], yd_ref[0]], axis=-1)
    mod = mod_ref[0]
    x = x_ref[0] + mod[:, 2 * D_MODEL:3 * D_MODEL] * _dot(y, w_ref[...])
    o_ref[0] = x
    sh = mod[:, 3 * D_MODEL:4 * D_MODEL]
    sc = mod[:, 4 * D_MODEL:5 * D_MODEL]
    ms = jnp.mean(x * x, axis=-1, keepdims=True)
    h = x * lax.rsqrt(ms + EPS) * g2_ref[...]
    h = (h * (1.0 + sc) + sh).astype(BF16)
    xn_ref[0] = h
    lg_ref[0] = lax.dot_general(rwt_ref[...], h, (((1,), (1,)), ((), ())), preferred_element_type=F32)


def _out_proj(ya, yb, yc, yd, w_out, x, mod, g2, rwt, tm):
    b, n, d = x.shape
    tok = pl.BlockSpec((1, tm, GROUP_W), lambda i, t: (i, t, 0))
    xs = pl.BlockSpec((1, tm, d), lambda i, t: (i, t, 0))
    return pl.pallas_call(
        _out_kernel,
        out_shape=(jax.ShapeDtypeStruct((b, n, d), F32),
                   jax.ShapeDtypeStruct((b, n, d), BF16),
                   jax.ShapeDtypeStruct((b, N_EXPERTS, n), F32)),
        grid=(b, n // tm),
        in_specs=[tok, tok, tok, tok, _full((4 * GROUP_W, d)), xs,
                  pl.BlockSpec((1, 1, 6 * d), lambda i, t: (i, 0, 0)),
                  _full((1, d)), _full((N_EXPERTS, d))],
        out_specs=(xs, xs, pl.BlockSpec((1, N_EXPERTS, tm), lambda i, t: (i, 0, t))),
        compiler_params=_params("arbitrary", "arbitrary"),
        name="out_proj",
    )(ya, yb, yc, yd, w_out, x, mod, g2, rwt)


def _cumsum_excl(m, tri):
    n = m.shape[1]
    carry = jnp.zeros((m.shape[0], 1), F32)
    outs = []
    for j in range(n // GROUP_W):
        blk = m[:, j * GROUP_W:(j + 1) * GROUP_W]
        inc = _dot(blk.astype(BF16), tri)
        outs.append(inc - blk + carry)
        carry = carry + inc[:, GROUP_W - 1:GROUP_W]
    return jnp.concatenate(outs, axis=1)


def _route_kernel(lg_ref, tri_ref, pos_ref, gate_ref, post_ref, *, n, cap):
    lg = lg_ref[0]
    e = jnp.exp(lg - jnp.max(lg, axis=0, keepdims=True))
    aff = e / jnp.sum(e, axis=0, keepdims=True)
    gate_ref[0] = aff

    def unresolved(state):
        lo, hi = state
        return jnp.max(jnp.where(lo < hi, 1.0, 0.0)) > 0.0

    def bisect(state):
        lo, hi = state
        mid = 0.5 * (lo + hi)
        mid = jnp.where(mid > lo, mid, hi)
        ge = aff >= mid
        cnt = jnp.sum(jnp.where(ge, 1.0, 0.0), axis=1, keepdims=True)
        least_ge = jnp.min(jnp.where(ge, aff, jnp.inf), axis=1, keepdims=True)
        most_lt = jnp.max(jnp.where(ge, NEG_INF, aff), axis=1, keepdims=True)
        up = cnt >= cap
        return jnp.where(up, least_ge, lo), jnp.where(up, hi, most_lt)

    thr, _ = lax.while_loop(unresolved, bisect, (jnp.min(aff, axis=1, keepdims=True),
                                                 jnp.max(aff, axis=1, keepdims=True)))
    gt = aff > thr
    eq = aff == thr
    need = cap - jnp.sum(jnp.where(gt, 1.0, 0.0), axis=1, keepdims=True)
    tri = tri_ref[...]
    rank_eq = _cumsum_excl(jnp.where(eq, 1.0, 0.0), tri)
    sel = jnp.where(gt, 1.0, jnp.where(eq, jnp.where(rank_eq < need, 1.0, 0.0), 0.0))
    pos = jnp.where(sel > 0.0, _cumsum_excl(sel, tri), -1.0)
    pos_ref[0] = pos.astype(I32)
    padded = jnp.concatenate([pos, jnp.full((LANES - N_EXPERTS, n), -1.0, F32)], axis=0)
    post_ref[0] = padded.T.astype(I32)


def _route(logits, tri, cap):
    b, _, n = logits.shape
    em = pl.BlockSpec((1, N_EXPERTS, n), lambda i: (i, 0, 0))
    kern = functools.partial(_route_kernel, n=n, cap=cap)
    return pl.pallas_call(
        kern,
        out_shape=(jax.ShapeDtypeStruct((b, N_EXPERTS, n), I32),
                   jax.ShapeDtypeStruct((b, N_EXPERTS, n), F32),
                   jax.ShapeDtypeStruct((b, n, LANES), I32)),
        grid=(b,),
        in_specs=[em, _full((GROUP_W, GROUP_W))],
        out_specs=(em, em, pl.BlockSpec((1, n, LANES), lambda i: (i, 0, 0))),
        compiler_params=_params("arbitrary"),
        name="router",
    )(logits, tri)


def _ffn_kernel(xn_ref, pos_ref, gate_ref, wg32_ref, wu32_ref, wd32_ref, o_ref, wg_ref, wu_ref, wd_ref, *, bb, cap, n):
    @pl.when(pl.program_id(1) == 0)
    def _():
        wg_ref[...] = wg32_ref[0, 0].astype(BF16)
        wu_ref[...] = wu32_ref[0, 0].astype(BF16)
        wd_ref[...] = wd32_ref[0, 0].astype(BF16)

    slot = lax.broadcasted_iota(I32, (cap, n), 0)
    expert = pl.ds(pl.program_id(0), 1)
    xs, gs = [], []
    for i in range(bb):
        hit = pos_ref[i, expert, :] == slot
        onehot = jnp.where(hit, 1.0, 0.0).astype(BF16)
        gs.append(jnp.sum(jnp.where(hit, gate_ref[i, expert, :], 0.0), axis=1, keepdims=True))
        xs.append(_dot(onehot, xn_ref[i]).astype(BF16))
    xg = xs[0] if bb == 1 else jnp.concatenate(xs, axis=0)
    g = gs[0] if bb == 1 else jnp.concatenate(gs, axis=0)
    hid = (jax.nn.silu(_dot(xg, wg_ref[...])) * _dot(xg, wu_ref[...])).astype(BF16)
    o = _dot(hid, wd_ref[...]) * g
    for i in range(bb):
        o_ref[i, 0] = o[i * cap:(i + 1) * cap].astype(BF16)


def _expert_ffn(xn, pos, gate, wg, wu, wd, layer, cap, bb):
    b, n, d = xn.shape
    wspec = pl.BlockSpec((1, 1, d, d), lambda e, j: (layer, e, 0, 0))
    sel = pl.BlockSpec((bb, N_EXPERTS, n), lambda e, j: (j, 0, 0))
    kern = functools.partial(_ffn_kernel, bb=bb, cap=cap, n=n)
    return pl.pallas_call(
        kern,
        out_shape=jax.ShapeDtypeStruct((b, N_EXPERTS, cap, d), BF16),
        grid=(N_EXPERTS, b // bb),
        in_specs=[pl.BlockSpec((bb, n, d), lambda e, j: (j, 0, 0)), sel, sel, wspec, wspec, wspec],
        out_specs=pl.BlockSpec((bb, 1, cap, d), lambda e, j: (j, e, 0, 0)),
        scratch_shapes=[pltpu.VMEM((d, d), BF16)] * 3,
        compiler_params=_params("arbitrary", "arbitrary"),
        name="expert_ffn",
    )(xn, pos, gate, wg, wu, wd)


def _scatter_kernel(x_ref, mod_ref, pt_ref, o_ref, out_ref, *, tn, cap):
    pos_t = pt_ref[0]
    if cap % 128 == 0:
        slot = lax.broadcasted_iota(I32, (tn, cap), 1)
        onehot = jnp.concatenate(
            [jnp.where(pos_t[:, e:e + 1] == slot, 1.0, 0.0).astype(BF16) for e in range(N_EXPERTS)], axis=1)
    else:
        slot = lax.broadcasted_iota(I32, (tn, N_EXPERTS * cap), 1)
        acc = jnp.zeros((tn, N_EXPERTS * cap), F32)
        for e in range(N_EXPERTS):
            pe = pos_t[:, e:e + 1]
            acc = jnp.where(jnp.where(pe >= 0, pe + e * cap, -1) == slot, 1.0, acc)
        onehot = acc.astype(BF16)
    y = _dot(onehot, o_ref[0].reshape(N_EXPERTS * cap, D_MODEL))
    g = mod_ref[0][:, 5 * D_MODEL:6 * D_MODEL]
    out_ref[0] = x_ref[0] + g * y


def _scatter(x, mod, pos_t, o, cap, tn):
    b, n, d = x.shape
    xs = pl.BlockSpec((1, tn, d), lambda i, t: (i, t, 0))
    kern = functools.partial(_scatter_kernel, tn=tn, cap=cap)
    return pl.pallas_call(
        kern,
        out_shape=jax.ShapeDtypeStruct((b, n, d), F32),
        grid=(b, n // tn),
        in_specs=[xs,
                  pl.BlockSpec((1, 1, 6 * d), lambda i, t: (i, 0, 0)),
                  pl.BlockSpec((1, tn, LANES), lambda i, t: (i, t, 0)),
                  pl.BlockSpec((1, N_EXPERTS, cap, d), lambda i, t: (i, 0, 0, 0))],
        out_specs=xs,
        compiler_params=_params("arbitrary", "arbitrary"),
        name="scatter_add",
    )(x, mod, pos_t, o)


@functools.lru_cache(maxsize=None)
def _np_consts():
    lane = np.arange(GROUP_W)
    seg32 = (lane[:, None] // DIFF_D == lane[None, :] // DIFF_D).astype(np.float32)
    seg64 = (lane[:, None] // HEAD_DIM == lane[None, :] // HEAD_DIM).astype(np.float32)
    ang = 2.0 * np.pi * ((lane[:, None] % HEAD_DIM) * (lane[None, :] % HEAD_DIM) % HEAD_DIM) / HEAD_DIM
    cc = np.cos(ang) * seg64
    ss = np.sin(ang) * seg64
    tri = (lane[:, None] <= lane[None, :]).astype(np.float32)
    return dict(seg32=seg32, seg64=seg64, cc=cc, ss=ss, tri=tri)


@functools.lru_cache(maxsize=None)
def _np_dft(n):
    idx = (np.arange(n, dtype=np.int64)[:, None] * np.arange(n, dtype=np.int64)[None, :]) % n
    ang = 2.0 * np.pi * idx.astype(np.float64) / n
    return np.cos(ang).astype(np.float32), np.sin(ang).astype(np.float32)


@functools.lru_cache(maxsize=None)
def _np_rope(n):
    half = DIFF_D // 2
    inv = 1.0 / (ROPE_BASE ** (np.arange(0, half, 2, dtype=np.float32) / half))
    t = np.arange(n)
    row = (t // GRID_W).astype(np.float32)[:, None] * inv
    col = (t % GRID_W).astype(np.float32)[:, None] * inv
    nf = inv.shape[0]
    d = np.arange(GROUP_W) % DIFF_D
    f = d % nf
    is_col = d >= half
    second = (d % half) >= nf
    ang = np.where(is_col[None, :], col[:, f], row[:, f])
    c = np.cos(ang).astype(np.float32)
    s = np.sin(ang).astype(np.float32)
    s = np.where(second[None, :], s, -s)
    return c, s


@functools.lru_cache(maxsize=None)
def _np_na_index(n_rows):
    n_steps = n_rows // NA_QROWS
    reps = [0, 1, 2, n_steps - 2, n_steps - 1]
    tq = NA_QROWS * GRID_W
    roff = np.zeros((len(reps), NA_QROWS, NA_WIN_ROWS), np.int32)
    valid = np.zeros((len(reps), tq, NA_WIN), bool)
    for ci, t in enumerate(reps):
        ws = int(np.clip(NA_QROWS * t - NA_ROWS // 2, 0, n_rows - NA_WIN_ROWS))
        qi = np.arange(tq)
        r = NA_QROWS * t + qi // GRID_W
        qcol = qi % GRID_W
        kk = np.arange(NA_WIN)
        krow = ws + kk // GRID_W
        kcol = kk % GRID_W
        rstart = np.clip(r - NA_ROWS // 2, 0, n_rows - NA_ROWS)
        wstart = np.clip(qcol - NA_COLS // 2, 0, GRID_W - NA_COLS)
        vr = (krow[None, :] >= rstart[:, None]) & (krow[None, :] < rstart[:, None] + NA_ROWS)
        vc = (kcol[None, :] >= wstart[:, None]) & (kcol[None, :] < wstart[:, None] + NA_COLS)
        valid[ci] = vr & vc
        rows_q = NA_QROWS * t + np.arange(NA_QROWS)
        rows_k = ws + np.arange(NA_WIN_ROWS)
        roff[ci] = np.clip(rows_k[None, :] - rows_q[:, None] + NA_ROWS - 1, 0, 2 * NA_ROWS - 2)
    return roff, valid


def _na_table(rpb, n_rows):
    roff, valid = _np_na_index(n_rows)
    n_cls = roff.shape[0]
    padw = GRID_W - NA_COLS
    padded = jnp.pad(rpb.astype(F32), ((0, 0), (0, 0), (padw, padw)))
    toep = jnp.stack([padded[:, :, GRID_W - 1 - q:2 * GRID_W - 1 - q] for q in range(GRID_W)], axis=2)
    blocks = jnp.take(toep, roff.reshape(-1), axis=1)
    blocks = blocks.reshape(HEADS, n_cls, NA_QROWS, NA_WIN_ROWS, GRID_W, GRID_W)
    bias = blocks.transpose(1, 0, 2, 4, 3, 5).reshape(n_cls, HEADS, NA_QROWS * GRID_W, NA_WIN)
    bias = jnp.where(valid[:, None], bias * LOG2E, NEG_INF)
    return bias.reshape(n_cls, HEADS * NA_QROWS * GRID_W, NA_WIN)


def _moe(x, xn, logits, mod, tri, wg, wu, wd, layer, bb, tn):
    n = x.shape[1]
    cap = CAPACITY_FACTOR * n // N_EXPERTS
    pos, gate, pos_t = _route(logits, tri, cap)
    o = _expert_ffn(xn, pos, gate, wg, wu, wd, layer, cap, bb)
    return _scatter(x, mod, pos_t, o, cap, tn)


def kernel(x, c, ctx, c_ctx, ada_w, ada_b, norm1_g, norm2_g, w_in, w_out, head_out_g, sgu_w, sgu_b, diff_qn_g, diff_kn_g, diff_lambda, na_qn_g, na_kn_g, na_rpb, router_w, exp_w_gate, exp_w_up, exp_w_down):
    b, n, d = x.shape
    n_ctx = ctx.shape[1]
    npc = _np_consts()
    consts = {k: jnp.asarray(v, F32).astype(BF16) for k, v in npc.items()}
    seg64, tri = consts["seg64"], consts["tri"]
    cn, sn = (jnp.asarray(a, F32).astype(BF16) for a in _np_dft(n))
    cn_c, sn_c = (jnp.asarray(a, F32).astype(BF16) for a in _np_dft(n_ctx))
    rope = tuple(jnp.asarray(a, F32) for a in _np_rope(n))

    pad = (-(b + 1)) % 8
    c_rows = jnp.concatenate([c, c_ctx[None, :], jnp.zeros((pad, d), F32)], axis=0)
    mod_all = _modulation(c_rows, ada_w, ada_b)

    xc = ctx
    for l in range(DEPTH):
        last = l == DEPTH - 1
        lam_init = 0.8 - 0.6 * math.exp(-0.3 * l)
        mod = mod_all[l, :b][:, None, :]
        mod_c = jnp.broadcast_to(mod_all[l, b][None, None, :], (b, 1, 6 * d))
        g1 = norm1_g[l][None, :]
        g2 = norm2_g[l][None, :]
        w_in_l = w_in[l].astype(BF16)
        w_out_l = w_out[l].astype(BF16)
        sguw = sgu_w[l].astype(BF16)
        sgub = jnp.repeat(sgu_b[l].T, HEAD_DIM, axis=1)
        hg = head_out_g[l].reshape(4, GROUP_W)
        vec = jnp.stack([jnp.tile(diff_qn_g[l], GROUP_W // DIFF_D), jnp.tile(diff_kn_g[l], GROUP_W // DIFF_D),
                         jnp.tile(na_qn_g[l], HEADS), jnp.tile(na_kn_g[l], HEADS),
                         hg[0], hg[1], hg[2], hg[3],
                         jnp.broadcast_to(jnp.max(jnp.abs(na_rpb[l])) * LOG2E, (GROUP_W,))], axis=0).astype(F32)
        vec = jnp.pad(vec, ((0, VEC_ROWS - vec.shape[0]), (0, 0)))
        lam = diff_lambda[l].astype(F32)
        table = _na_table(na_rpb[l], n // GRID_W)
        rwt = router_w[l].T.astype(BF16)
        experts = (exp_w_gate, exp_w_up, exp_w_down)

        ya, zc, zs, qc, kct, vc, qd, kdt, vd = _in_proj(x, mod, g1, w_in_l, consts, sguw, sgub, vec, rope, 512)
        ya_c, zc_c, zs_c, qc_c, kct_c, vc_c, qd_c, kdt_c, vd_c = _in_proj(
            xc, mod_c, g1, w_in_l, consts, sguw, sgub, vec, None, n_ctx)

        yb = _fourier(zc, zs, cn, sn, seg64, vec, 1024)
        yc = _attention(qc, [(kct, vc), (kct_c, vc_c)], lam, lam_init, seg64, vec, 6, 1.0 - lam_init, 512, 512)
        yd = _na_attention(qd, kdt, vd, kdt_c, vd_c, table, seg64, vec, 4)
        x, xn, logits = _out_proj(ya, yb, yc, yd, w_out_l, x, mod, g2, rwt, 512)
        x = _moe(x, xn, logits, mod, tri, *experts, l, 1, 512)

        if not last:
            yb_c = _fourier(zc_c, zs_c, cn_c, sn_c, seg64, vec, n_ctx)
            yc_c = _attention(qc_c, [(kct_c, vc_c)], lam, lam_init, seg64, vec, 6, 1.0 - lam_init, n_ctx)
            yd_c = _attention(qd_c, [(kdt_c, vd_c)], None, 0.0, seg64, vec, 7, 1.0, n_ctx)
            xc, xn_c, logits_c = _out_proj(ya_c, yb_c, yc_c, yd_c, w_out_l, xc, mod_c, g2, rwt, n_ctx)
            xc = _moe(xc, xn_c, logits_c, mod_c, tri, *experts, l, b, n_ctx)
    return x
```

```python
import functools
import math

import numpy as np
import jax
import jax.numpy as jnp
from jax import lax
from jax.experimental import pallas as pl
from jax.experimental.pallas import tpu as pltpu

F32 = jnp.float32
BF16 = jnp.bfloat16
I32 = jnp.int32

D_MODEL = 1024
DEPTH = 2
GRID_W = 64
LANES = 128
HEAD_DIM = 64
LOG2_HEAD_DIM = 6
GROUP_W = 256
HEADS = GROUP_W // HEAD_DIM
CHUNK = 128
DIFF_D = HEAD_DIM // 2
NA_ROWS = 8
NA_COLS = 16
N_EXPERTS = 16
CAPACITY_FACTOR = 2
ROPE_BASE = 10000.0
EPS = 1e-6
IN_W = 9 * GROUP_W
LOG2E = 1.4426950408889634

VMEM_LIMIT_BYTES = 56 * 1024 * 1024
NA_QROWS = 2
NA_WIN_ROWS = NA_ROWS + 2
NA_WIN = NA_WIN_ROWS * GRID_W
NA_HEAD_STACK = 1
NEG_INF = float("-inf")
VEC_ROWS = 16
SLOT_WINDOW = 128
SAFE_EXP2_BOUND = 48.0
BOUND_SLACK = 1.02


def _dot(a, b):
    return jnp.dot(a, b, preferred_element_type=F32)


def _params(*sem):
    return pltpu.CompilerParams(dimension_semantics=sem, vmem_limit_bytes=VMEM_LIMIT_BYTES)


def _full(shape):
    nd = len(shape)
    return pl.BlockSpec(shape, lambda *_: (0,) * nd)


def _seg_rms(x, seg, width):
    ss = _dot((x * x).astype(BF16), seg)
    return x * lax.rsqrt(ss * (1.0 / width) + EPS)


def _mod_kernel(c_ref, w_ref, b_ref, o_ref):
    s = jax.nn.silu(c_ref[...]).astype(BF16)
    o_ref[0] = _dot(s, w_ref[0].astype(BF16)) + b_ref[0]


def _modulation(c_rows, ada_w, ada_b):
    depth, d, w6 = ada_w.shape
    r = c_rows.shape[0]
    tn = 1024
    return pl.pallas_call(
        _mod_kernel,
        out_shape=jax.ShapeDtypeStruct((depth, r, w6), F32),
        grid=(depth, w6 // tn),
        in_specs=[
            pl.BlockSpec((r, d), lambda l, j: (0, 0)),
            pl.BlockSpec((1, d, tn), lambda l, j: (l, 0, j)),
            pl.BlockSpec((1, 1, tn), lambda l, j: (l, 0, j)),
        ],
        out_specs=pl.BlockSpec((1, r, tn), lambda l, j: (l, 0, j)),
        compiler_params=_params("arbitrary", "arbitrary"),
        name="modulation",
    )(c_rows, ada_w, ada_b.reshape(depth, 1, w6))


def _rope(x, c, s, lane):
    fwd = pltpu.roll(x, GROUP_W - 8, 1)
    bwd = pltpu.roll(x, 8, 1)
    partner = jnp.where((lane & 8) == 0, fwd, bwd)
    return x * c + partner * s


def _in_kernel(*refs, tm, use_rope, c_scale, d_scale):
    (x_ref, mod_ref, g1_ref, w_ref, seg32_ref, seg64_ref, cc_ref, ss_ref, sguw_ref, sgub_ref, vec_ref) = refs[:11]
    rest = refs[11:]
    if use_rope:
        ropec_ref, ropes_ref = rest[:2]
        rest = rest[2:]
    ya_ref, zc_ref, zs_ref, qc_ref, kct_ref, vc_ref, qd_ref, kdt_ref, vd_ref = rest

    x = x_ref[0]
    mod = mod_ref[0]
    sh = mod[:, 0:D_MODEL]
    sc = mod[:, D_MODEL:2 * D_MODEL]
    ms = jnp.mean(x * x, axis=-1, keepdims=True)
    h = x * lax.rsqrt(ms + EPS) * g1_ref[...]
    h = (h * (1.0 + sc) + sh).astype(BF16)
    lane = lax.broadcasted_iota(I32, (1, GROUP_W), 1)
    head = lane >> LOG2_HEAD_DIM
    seg32 = seg32_ref[...]
    seg64 = seg64_ref[...]
    vec = vec_ref[...]

    z = jax.nn.gelu(_dot(h, w_ref[:, 0:2 * GROUP_W]))
    u = z[:, 0:GROUP_W]
    vn = _seg_rms(z[:, GROUP_W:2 * GROUP_W], seg64, HEAD_DIM).astype(BF16)
    rows = []
    for c in range(tm // CHUNK):
        vch = vn[c * CHUNK:(c + 1) * CHUNK]
        sv = jnp.zeros((CHUNK, GROUP_W), F32)
        for hh in range(HEADS):
            sv = jnp.where(head == hh, _dot(sguw_ref[hh], vch), sv)
        rows.append(sv + sgub_ref[...])
    ya = u * jnp.concatenate(rows, axis=0)
    ya_ref[0] = (_seg_rms(ya, seg64, HEAD_DIM) * vec[4:5]).astype(BF16)

    zb = _dot(h, w_ref[:, 2 * GROUP_W:3 * GROUP_W]).astype(BF16)
    zc_ref[0] = _dot(zb, cc_ref[...]).astype(BF16)
    zs_ref[0] = _dot(zb, ss_ref[...]).astype(BF16)

    pc = _dot(h, w_ref[:, 3 * GROUP_W:6 * GROUP_W])
    q = _seg_rms(pc[:, 0:GROUP_W], seg32, DIFF_D) * vec[0:1]
    k = _seg_rms(pc[:, GROUP_W:2 * GROUP_W], seg32, DIFF_D) * vec[1:2]
    if use_rope:
        rc = ropec_ref[...]
        rs = ropes_ref[...]
        q = _rope(q, rc, rs, lane)
        k = _rope(k, rc, rs, lane)
    qc_ref[0] = (q * c_scale).astype(BF16)
    kct_ref[0] = k.T.astype(BF16)
    vc_ref[0] = pc[:, 2 * GROUP_W:3 * GROUP_W].astype(BF16)

    pd = _dot(h, w_ref[:, 6 * GROUP_W:9 * GROUP_W])
    qd = _seg_rms(pd[:, 0:GROUP_W], seg64, HEAD_DIM) * vec[2:3]
    kd = _seg_rms(pd[:, GROUP_W:2 * GROUP_W], seg64, HEAD_DIM) * vec[3:4]
    qd_ref[0] = (qd * d_scale).astype(BF16)
    kdt_ref[0] = kd.T.astype(BF16)
    vd_ref[0] = pd[:, 2 * GROUP_W:3 * GROUP_W].astype(BF16)


def _in_proj(x, mod, g1, w_in, consts, sguw, sgub, vec, rope, tm):
    b, n, d = x.shape
    use_rope = rope is not None
    tok = pl.BlockSpec((1, tm, GROUP_W), lambda i, t: (i, t, 0))
    tok_t = pl.BlockSpec((1, GROUP_W, tm), lambda i, t: (i, 0, t))
    in_specs = [
        pl.BlockSpec((1, tm, d), lambda i, t: (i, t, 0)),
        pl.BlockSpec((1, 1, 6 * d), lambda i, t: (i, 0, 0)),
        _full((1, d)),
        _full((d, IN_W)),
        _full((GROUP_W, GROUP_W)), _full((GROUP_W, GROUP_W)), _full((GROUP_W, GROUP_W)), _full((GROUP_W, GROUP_W)),
        _full((HEADS, CHUNK, CHUNK)),
        _full((CHUNK, GROUP_W)),
        _full((VEC_ROWS, GROUP_W)),
    ]
    args = [x, mod, g1, w_in, consts["seg32"], consts["seg64"], consts["cc"], consts["ss"], sguw, sgub, vec]
    if use_rope:
        in_specs += [pl.BlockSpec((tm, GROUP_W), lambda i, t: (t, 0))] * 2
        args += list(rope)
    sd = jax.ShapeDtypeStruct((b, n, GROUP_W), BF16)
    sdt = jax.ShapeDtypeStruct((b, GROUP_W, n), BF16)
    kern = functools.partial(
        _in_kernel, tm=tm, use_rope=use_rope,
        c_scale=(DIFF_D ** -0.5) * LOG2E, d_scale=(HEAD_DIM ** -0.5) * LOG2E)
    return pl.pallas_call(
        kern,
        out_shape=(sd, sd, sd, sd, sdt, sd, sd, sdt, sd),
        grid=(b, n // tm),
        in_specs=in_specs,
        out_specs=(tok, tok, tok, tok, tok_t, tok, tok, tok_t, tok),
        compiler_params=_params("arbitrary", "arbitrary"),
        name="in_proj",
    )(*args)


def _fourier_kernel(cn_ref, sn_ref, zc_ref, zs_ref, seg64_ref, vec_ref, o_ref, *, norm):
    y = (_dot(cn_ref[...], zc_ref[0]) - _dot(sn_ref[...], zs_ref[0])) * norm
    o_ref[0] = (_seg_rms(y, seg64_ref[...], HEAD_DIM) * vec_ref[5:6]).astype(BF16)


def _fourier(zc, zs, cn, sn, seg64, vec, tn):
    b, n, _ = zc.shape
    kern = functools.partial(_fourier_kernel, norm=1.0 / math.sqrt(n * HEAD_DIM))
    return pl.pallas_call(
        kern,
        out_shape=jax.ShapeDtypeStruct((b, n, GROUP_W), BF16),
        grid=(n // tn, b),
        in_specs=[
            pl.BlockSpec((tn, n), lambda t, i: (t, 0)),
            pl.BlockSpec((tn, n), lambda t, i: (t, 0)),
            pl.BlockSpec((1, n, GROUP_W), lambda t, i: (i, 0, 0)),
            pl.BlockSpec((1, n, GROUP_W), lambda t, i: (i, 0, 0)),
            _full((GROUP_W, GROUP_W)),
            _full((VEC_ROWS, GROUP_W)),
        ],
        out_specs=pl.BlockSpec((1, tn, GROUP_W), lambda t, i: (i, t, 0)),
        compiler_params=_params("arbitrary", "arbitrary"),
        name="fourier",
    )(cn, sn, zc, zs, seg64, vec)


def _attn_kernel(*refs, n_src, diff, lam_init, chunk, tq, vec_row, out_scale):
    q_ref = refs[0]
    rest = refs[1 + 2 * n_src:]
    if diff:
        lam_ref = rest[0]
        rest = rest[1:]
    seg64_ref, vec_ref, o_ref = rest

    q = q_ref[0]
    lane = lax.broadcasted_iota(I32, (1, GROUP_W), 1)
    if diff:
        lf = lam_ref[...]
        lam = (jnp.exp(jnp.sum(lf[0:1] * lf[1:2], axis=-1, keepdims=True))
               - jnp.exp(jnp.sum(lf[2:3] * lf[3:4], axis=-1, keepdims=True)) + lam_init)

    chunks = []
    for i in range(n_src):
        kt_ref, v_ref = refs[1 + 2 * i], refs[2 + 2 * i]
        nk = kt_ref.shape[2]
        chunks += [(kt_ref, v_ref, c0, min(chunk, nk - c0)) for c0 in range(0, nk, chunk)]

    width = DIFF_D if diff else HEAD_DIM
    gq = vec_ref[0:1] if diff else vec_ref[2:3]
    gk = vec_ref[1:2] if diff else vec_ref[3:4]
    bound = jnp.max(jnp.abs(gq)) * jnp.max(jnp.abs(gk)) * (math.sqrt(width) * LOG2E * BOUND_SLACK)

    def attend_bounded(sel):
        qm = jnp.where(sel, q, jnp.zeros_like(q))
        l = jnp.zeros((tq, 1), F32)
        acc = jnp.zeros((tq, GROUP_W), F32)
        for kt_ref, v_ref, c0, ck in chunks:
            p = jnp.exp2(_dot(qm, kt_ref[0, :, c0:c0 + ck]) - bound)
            l = l + jnp.sum(p, axis=-1, keepdims=True)
            acc = acc + _dot(p.astype(BF16), v_ref[0, c0:c0 + ck, :])
        return acc * (1.0 / l)

    def attend_online(sel):
        qm = jnp.where(sel, q, jnp.zeros_like(q))
        m = jnp.full((tq, 1), NEG_INF, F32)
        l = jnp.zeros((tq, 1), F32)
        acc = jnp.zeros((tq, GROUP_W), F32)
        for kt_ref, v_ref, c0, ck in chunks:
            s = _dot(qm, kt_ref[0, :, c0:c0 + ck])
            m_new = jnp.maximum(m, jnp.max(s, axis=-1, keepdims=True))
            alpha = jnp.exp2(m - m_new)
            p = jnp.exp2(s - m_new)
            l = alpha * l + jnp.sum(p, axis=-1, keepdims=True)
            acc = alpha * acc + _dot(p.astype(BF16), v_ref[0, c0:c0 + ck, :])
            m = m_new
        return acc * (1.0 / l)

    def head_out(attend, h):
        if diff:
            return (attend((lane >> (LOG2_HEAD_DIM - 1)) == 2 * h)
                    - lam * attend((lane >> (LOG2_HEAD_DIM - 1)) == 2 * h + 1))
        return attend((lane >> LOG2_HEAD_DIM) == h)

    def finish(out):
        y = _seg_rms(out, seg64_ref[...], HEAD_DIM) * vec_ref[vec_row:vec_row + 1]
        o_ref[0] = (y * out_scale).astype(BF16)

    def run_bounded():
        out = jnp.zeros((tq, GROUP_W), F32)
        for h in range(HEADS):
            out = jnp.where((lane >> LOG2_HEAD_DIM) == h, head_out(attend_bounded, h), out)
        finish(out)

    def run_online():
        def body(h, out):
            return jnp.where((lane >> LOG2_HEAD_DIM) == h, head_out(attend_online, h), out)

        finish(lax.fori_loop(0, HEADS, body, jnp.zeros((tq, GROUP_W), F32)))

    small = bound <= SAFE_EXP2_BOUND
    pl.when(small)(run_bounded)
    pl.when(jnp.logical_not(small))(run_online)


def _attention(q, srcs, lam, lam_init, seg64, vec, vec_row, out_scale, tq, chunk=768):
    b, nq, _ = q.shape
    diff = lam is not None
    in_specs = [pl.BlockSpec((1, tq, GROUP_W), lambda i, t: (i, t, 0))]
    args = [q]
    for kt, v in srcs:
        nk = v.shape[1]
        in_specs += [pl.BlockSpec((1, GROUP_W, nk), lambda i, t: (i, 0, 0)),
                     pl.BlockSpec((1, nk, GROUP_W), lambda i, t: (i, 0, 0))]
        args += [kt, v]
    if diff:
        in_specs.append(_full((4, DIFF_D)))
        args.append(lam)
    in_specs += [_full((GROUP_W, GROUP_W)), _full((VEC_ROWS, GROUP_W))]
    args += [seg64, vec]
    kern = functools.partial(_attn_kernel, n_src=len(srcs), diff=diff, lam_init=lam_init, chunk=chunk, tq=tq,
                             vec_row=vec_row, out_scale=out_scale)
    return pl.pallas_call(
        kern,
        out_shape=jax.ShapeDtypeStruct((b, nq, GROUP_W), BF16),
        grid=(b, nq // tq),
        in_specs=in_specs,
        out_specs=pl.BlockSpec((1, tq, GROUP_W), lambda i, t: (i, t, 0)),
        compiler_params=_params("arbitrary", "arbitrary"),
        name="diff_attention" if diff else "ctx_attention",
    )(*args)


def _na_kernel(q_ref, kt_ref, v_ref, ktc_ref, vc_ref, tab_ref, seg64_ref, vec_ref, o_ref, *, n_rows, pairs):
    n_steps = n_rows // NA_QROWS
    tq = NA_QROWS * GRID_W
    lane = lax.broadcasted_iota(I32, (1, GROUP_W), 1)
    head = lane >> LOG2_HEAD_DIM
    ktc = ktc_ref[0]
    vc = vc_ref[0]
    bound = (jnp.max(jnp.abs(vec_ref[2:3])) * jnp.max(jnp.abs(vec_ref[3:4])) * (math.sqrt(HEAD_DIM) * LOG2E * BOUND_SLACK)
             + jnp.max(vec_ref[8:9]))

    def run(bounded):
        for pi in range(pairs):
            t = pl.program_id(1) * pairs + pi
            ws = jnp.clip(NA_QROWS * t - NA_ROWS // 2, 0, n_rows - NA_WIN_ROWS)
            k0 = pl.multiple_of(ws * GRID_W, 128)
            tid = jnp.where(t < 2, t, jnp.where(t < n_steps - 2, 2, t - (n_steps - 5)))
            q = q_ref[0, pi * tq:(pi + 1) * tq, :]
            out = jnp.zeros((tq, GROUP_W), F32)
            for h0 in range(0, HEADS, NA_HEAD_STACK):
                hs = range(h0, h0 + NA_HEAD_STACK)
                qs = jnp.concatenate([jnp.where(head == hh, q, jnp.zeros_like(q)) for hh in hs], axis=0)
                s_loc = (_dot(qs, kt_ref[0, :, pl.ds(k0, NA_WIN)])
                         + tab_ref[tid, h0 * tq:(h0 + NA_HEAD_STACK) * tq, :])
                s_ctx = _dot(qs, ktc)
                if bounded:
                    m = bound
                else:
                    m = jnp.maximum(jnp.max(s_loc, axis=-1, keepdims=True), jnp.max(s_ctx, axis=-1, keepdims=True))
                p_loc = jnp.exp2(s_loc - m)
                p_ctx = jnp.exp2(s_ctx - m)
                l = jnp.sum(p_loc, axis=-1, keepdims=True) + jnp.sum(p_ctx, axis=-1, keepdims=True)
                o = _dot(p_loc.astype(BF16), v_ref[0, pl.ds(k0, NA_WIN), :]) + _dot(p_ctx.astype(BF16), vc)
                o = o * (1.0 / l)
                for i, hh in enumerate(hs):
                    out = jnp.where(head == hh, o[i * tq:(i + 1) * tq], out)
            o_ref[0, pi * tq:(pi + 1) * tq, :] = (
                _seg_rms(out, seg64_ref[...], HEAD_DIM) * vec_ref[7:8]).astype(BF16)

    small = bound <= SAFE_EXP2_BOUND
    pl.when(small)(lambda: run(True))
    pl.when(jnp.logical_not(small))(lambda: run(False))


def _na_attention(q, kt, v, ktc, vc, table, seg64, vec, pairs):
    b, n, _ = q.shape
    nc = vc.shape[1]
    tq = pairs * NA_QROWS * GRID_W
    kern = functools.partial(_na_kernel, n_rows=n // GRID_W, pairs=pairs)
    return pl.pallas_call(
        kern,
        out_shape=jax.ShapeDtypeStruct((b, n, GROUP_W), BF16),
        grid=(b, n // tq),
        in_specs=[
            pl.BlockSpec((1, tq, GROUP_W), lambda i, t: (i, t, 0)),
            pl.BlockSpec((1, GROUP_W, n), lambda i, t: (i, 0, 0)),
            pl.BlockSpec((1, n, GROUP_W), lambda i, t: (i, 0, 0)),
            pl.BlockSpec((1, GROUP_W, nc), lambda i, t: (i, 0, 0)),
            pl.BlockSpec((1, nc, GROUP_W), lambda i, t: (i, 0, 0)),
            _full(table.shape),
            _full((GROUP_W, GROUP_W)),
            _full((VEC_ROWS, GROUP_W)),
        ],
        out_specs=pl.BlockSpec((1, tq, GROUP_W), lambda i, t: (i, t, 0)),
        compiler_params=_params("arbitrary", "arbitrary"),
        name="neighbourhood_attention",
    )(q, kt, v, ktc, vc, table, seg64, vec)


def _out_kernel(ya_ref, yb_ref, yc_ref, yd_ref, w_ref, x_ref, mod_ref, g2_ref, rwt_ref, o_ref, xn_ref, lg_ref):
    y = jnp.concatenate([ya_ref[0], yb_ref[0], yc_ref[0], yd_ref[0]], axis=-1)
    mod = mod_ref[0]
    x = x_ref[0] + mod[:, 2 * D_MODEL:3 * D_MODEL] * _dot(y, w_ref[...])
    o_ref[0] = x
    sh = mod[:, 3 * D_MODEL:4 * D_MODEL]
    sc = mod[:, 4 * D_MODEL:5 * D_MODEL]
    ms = jnp.mean(x * x, axis=-1, keepdims=True)
    h = x * lax.rsqrt(ms + EPS) * g2_ref[...]
    h = (h * (1.0 + sc) + sh).astype(BF16)
    xn_ref[0] = h
    lg_ref[0] = lax.dot_general(rwt_ref[...], h, (((1,), (1,)), ((), ())), preferred_element_type=F32)


def _out_proj(ya, yb, yc, yd, w_out, x, mod, g2, rwt, tm):
    b, n, d = x.shape
    tok = pl.BlockSpec((1, tm, GROUP_W), lambda i, t: (i, t, 0))
    xs = pl.BlockSpec((1, tm, d), lambda i, t: (i, t, 0))
    return pl.pallas_call(
        _out_kernel,
        out_shape=(jax.ShapeDtypeStruct((b, n, d), F32),
                   jax.ShapeDtypeStruct((b, n, d), BF16),
                   jax.ShapeDtypeStruct((b, N_EXPERTS, n), F32)),
        grid=(b, n // tm),
        in_specs=[tok, tok, tok, tok, _full((4 * GROUP_W, d)), xs,
                  pl.BlockSpec((1, 1, 6 * d), lambda i, t: (i, 0, 0)),
                  _full((1, d)), _full((N_EXPERTS, d))],
        out_specs=(xs, xs, pl.BlockSpec((1, N_EXPERTS, tm), lambda i, t: (i, 0, t))),
        compiler_params=_params("arbitrary", "arbitrary"),
        name="out_proj",
    )(ya, yb, yc, yd, w_out, x, mod, g2, rwt)


def _cumsum_excl(m, tri):
    n = m.shape[1]
    carry = jnp.zeros((m.shape[0], 1), F32)
    outs = []
    for j in range(n // GROUP_W):
        blk = m[:, j * GROUP_W:(j + 1) * GROUP_W]
        inc = _dot(blk.astype(BF16), tri)
        outs.append(inc - blk + carry)
        carry = carry + inc[:, GROUP_W - 1:GROUP_W]
    return jnp.concatenate(outs, axis=1)


def _route_kernel(lg_ref, tri_ref, pos_ref, gate_ref, post_ref, st_ref, *, n, cap, tile):
    lg = lg_ref[0]
    e = jnp.exp(lg - jnp.max(lg, axis=0, keepdims=True))
    aff = e / jnp.sum(e, axis=0, keepdims=True)
    gate_ref[0] = aff

    def unresolved(state):
        lo, hi = state
        return jnp.max(jnp.where(lo < hi, 1.0, 0.0)) > 0.0

    def bisect(state):
        lo, hi = state
        mid = 0.5 * (lo + hi)
        mid = jnp.where(mid > lo, mid, hi)
        ge = aff >= mid
        cnt = jnp.sum(jnp.where(ge, 1.0, 0.0), axis=1, keepdims=True)
        least_ge = jnp.min(jnp.where(ge, aff, jnp.inf), axis=1, keepdims=True)
        most_lt = jnp.max(jnp.where(ge, NEG_INF, aff), axis=1, keepdims=True)
        up = cnt >= cap
        return jnp.where(up, least_ge, lo), jnp.where(up, hi, most_lt)

    thr, _ = lax.while_loop(unresolved, bisect, (jnp.min(aff, axis=1, keepdims=True),
                                                 jnp.max(aff, axis=1, keepdims=True)))
    gt = aff > thr
    eq = aff == thr
    need = cap - jnp.sum(jnp.where(gt, 1.0, 0.0), axis=1, keepdims=True)
    tri = tri_ref[...]
    rank_eq = _cumsum_excl(jnp.where(eq, 1.0, 0.0), tri)
    sel = jnp.where(gt, 1.0, jnp.where(eq, jnp.where(rank_eq < need, 1.0, 0.0), 0.0))
    cum = _cumsum_excl(sel, tri)
    pos = jnp.where(sel > 0.0, cum, -1.0)
    pos_ref[0] = pos.astype(I32)
    tile_lane = lax.broadcasted_iota(I32, (N_EXPERTS, LANES), 1)
    starts = jnp.zeros((N_EXPERTS, LANES), F32)
    for t in range(n // tile):
        starts = jnp.where(tile_lane == t, cum[:, t * tile:t * tile + 1], starts)
    st_ref[0] = starts.astype(I32)
    padded = jnp.concatenate([pos, jnp.full((LANES - N_EXPERTS, n), -1.0, F32)], axis=0)
    post_ref[0] = padded.T.astype(I32)


def _route(logits, tri, cap, tile):
    b, _, n = logits.shape
    em = pl.BlockSpec((1, N_EXPERTS, n), lambda i: (i, 0, 0))
    kern = functools.partial(_route_kernel, n=n, cap=cap, tile=tile)
    return pl.pallas_call(
        kern,
        out_shape=(jax.ShapeDtypeStruct((b, N_EXPERTS, n), I32),
                   jax.ShapeDtypeStruct((b, N_EXPERTS, n), F32),
                   jax.ShapeDtypeStruct((b, n, LANES), I32),
                   jax.ShapeDtypeStruct((b, N_EXPERTS, LANES), I32)),
        grid=(b,),
        in_specs=[em, _full((GROUP_W, GROUP_W))],
        out_specs=(em, em, pl.BlockSpec((1, n, LANES), lambda i: (i, 0, 0)),
                   pl.BlockSpec((1, N_EXPERTS, LANES), lambda i: (i, 0, 0))),
        compiler_params=_params("arbitrary"),
        name="router",
    )(logits, tri)


def _ffn_kernel(st_ref, xn_ref, pos_ref, gate_ref, wg32_ref, wu32_ref, wd32_ref, o_ref, wg_ref, wu_ref, wd_ref, xg_ref,
                *, bb, cap, n, tile):
    @pl.when(pl.program_id(1) == 0)
    def _():
        wg_ref[...] = wg32_ref[0, 0].astype(BF16)
        wu_ref[...] = wu32_ref[0, 0].astype(BF16)
        wd_ref[...] = wd32_ref[0, 0].astype(BF16)

    expert = pl.ds(pl.program_id(0), 1)

    def step(gather):
        slot = lax.broadcasted_iota(I32, (cap, n), 0)
        hits = [pos_ref[i, expert, :] == slot for i in range(bb)]
        gs = [jnp.sum(jnp.where(hits[i], gate_ref[i, expert, :], 0.0), axis=1, keepdims=True) for i in range(bb)]
        g = gs[0] if bb == 1 else jnp.concatenate(gs, axis=0)
        gather(hits)
        xg = xg_ref[...].astype(BF16)
        hid = (jax.nn.silu(_dot(xg, wg_ref[...])) * _dot(xg, wu_ref[...])).astype(BF16)
        o = _dot(hid, wd_ref[...]) * g
        for i in range(bb):
            o_ref[i, 0] = o[i * cap:(i + 1) * cap].astype(BF16)

    def gather_dense(hits):
        for i in range(bb):
            xg_ref[i * cap:(i + 1) * cap, :] = _dot(jnp.where(hits[i], 1.0, 0.0).astype(BF16), xn_ref[i])

    if bb > 1 or cap <= SLOT_WINDOW:
        step(gather_dense)
        return

    nt = n // tile
    base = pl.program_id(1) * nt * N_EXPERTS + pl.program_id(0)
    win, fits = [], True
    for t in range(nt):
        start = st_ref[base + t * N_EXPERTS]
        end = st_ref[base + (t + 1) * N_EXPERTS] if t + 1 < nt else cap
        a = jnp.minimum((start >> 4) << 4, cap - SLOT_WINDOW)
        win.append(a)
        fits = jnp.logical_and(fits, end - a <= SLOT_WINDOW)

    def gather_windowed(hits):
        del hits
        xg_ref[...] = jnp.zeros_like(xg_ref)
        wslot = lax.broadcasted_iota(I32, (SLOT_WINDOW, tile), 0)
        for t in range(nt):
            p = pos_ref[0, expert, t * tile:(t + 1) * tile]
            onehot = jnp.where(p - win[t] == wslot, 1.0, 0.0).astype(BF16)
            rows = pl.ds(pl.multiple_of(win[t], 16), SLOT_WINDOW)
            xg_ref[rows, :] = xg_ref[rows, :] + _dot(onehot, xn_ref[0, t * tile:(t + 1) * tile, :])

    pl.when(fits)(lambda: step(gather_windowed))
    pl.when(jnp.logical_not(fits))(lambda: step(gather_dense))


def _expert_ffn(xn, pos, gate, starts, wg, wu, wd, layer, cap, bb, tile):
    b, n, d = xn.shape
    wspec = pl.BlockSpec((1, 1, d, d), lambda e, j, st: (layer, e, 0, 0))
    sel = pl.BlockSpec((bb, N_EXPERTS, n), lambda e, j, st: (j, 0, 0))
    kern = functools.partial(_ffn_kernel, bb=bb, cap=cap, n=n, tile=tile)
    return pl.pallas_call(
        kern,
        out_shape=jax.ShapeDtypeStruct((b, N_EXPERTS, cap, d), BF16),
        grid_spec=pltpu.PrefetchScalarGridSpec(
            num_scalar_prefetch=1,
            grid=(N_EXPERTS, b // bb),
            in_specs=[pl.BlockSpec((bb, n, d), lambda e, j, st: (j, 0, 0)), sel, sel, wspec, wspec, wspec],
            out_specs=pl.BlockSpec((bb, 1, cap, d), lambda e, j, st: (j, e, 0, 0)),
            scratch_shapes=[pltpu.VMEM((d, d), BF16)] * 3 + [pltpu.VMEM((bb * cap, d), F32)]),
        compiler_params=_params("arbitrary", "arbitrary"),
        name="expert_ffn",
    )(starts, xn, pos, gate, wg, wu, wd)


def _scatter_kernel(st_ref, x_ref, mod_ref, pt_ref, o_ref, out_ref, *, tn, cap, nt):
    pos_t = pt_ref[0]
    g = mod_ref[0][:, 5 * D_MODEL:6 * D_MODEL]

    def dense():
        if cap % LANES == 0:
            slot = lax.broadcasted_iota(I32, (tn, cap), 1)
            onehot = jnp.concatenate(
                [jnp.where(pos_t[:, e:e + 1] == slot, 1.0, 0.0).astype(BF16) for e in range(N_EXPERTS)], axis=1)
        else:
            slot = lax.broadcasted_iota(I32, (tn, N_EXPERTS * cap), 1)
            acc = jnp.zeros((tn, N_EXPERTS * cap), F32)
            for e in range(N_EXPERTS):
                pe = pos_t[:, e:e + 1]
                acc = jnp.where(jnp.where(pe >= 0, pe + e * cap, -1) == slot, 1.0, acc)
            onehot = acc.astype(BF16)
        y = _dot(onehot, o_ref[0].reshape(N_EXPERTS * cap, D_MODEL))
        out_ref[0] = x_ref[0] + g * y

    if cap <= SLOT_WINDOW:
        dense()
        return

    t = pl.program_id(1)
    base = (pl.program_id(0) * nt + t) * N_EXPERTS
    nxt = jnp.minimum(t + 1, nt - 1)
    nbase = (pl.program_id(0) * nt + nxt) * N_EXPERTS
    win, fits = [], True
    for e in range(N_EXPERTS):
        start = st_ref[base + e]
        end = jnp.where(t + 1 < nt, st_ref[nbase + e], cap)
        a = jnp.minimum((start >> 4) << 4, cap - SLOT_WINDOW)
        win.append(a)
        fits = jnp.logical_and(fits, end - a <= SLOT_WINDOW)

    @pl.when(fits)
    def _():
        slot = lax.broadcasted_iota(I32, (tn, SLOT_WINDOW), 1)
        onehot = jnp.concatenate(
            [jnp.where(pos_t[:, e:e + 1] - win[e] == slot, 1.0, 0.0).astype(BF16) for e in range(N_EXPERTS)], axis=1)
        rows = jnp.concatenate(
            [o_ref[0, e, pl.ds(pl.multiple_of(win[e], 16), SLOT_WINDOW), :] for e in range(N_EXPERTS)], axis=0)
        out_ref[0] = x_ref[0] + g * _dot(onehot, rows)

    pl.when(jnp.logical_not(fits))(dense)


def _scatter(x, mod, pos_t, starts, o, cap, tn):
    b, n, d = x.shape
    nt = n // tn
    xs = pl.BlockSpec((1, tn, d), lambda i, t, st: (i, t, 0))
    kern = functools.partial(_scatter_kernel, tn=tn, cap=cap, nt=nt)
    return pl.pallas_call(
        kern,
        out_shape=jax.ShapeDtypeStruct((b, n, d), F32),
        grid_spec=pltpu.PrefetchScalarGridSpec(
            num_scalar_prefetch=1,
            grid=(b, nt),
            in_specs=[xs,
                      pl.BlockSpec((1, 1, 6 * d), lambda i, t, st: (i, 0, 0)),
                      pl.BlockSpec((1, tn, LANES), lambda i, t, st: (i, t, 0)),
                      pl.BlockSpec((1, N_EXPERTS, cap, d), lambda i, t, st: (i, 0, 0, 0))],
            out_specs=xs),
        compiler_params=_params("arbitrary", "arbitrary"),
        name="scatter_add",
    )(starts, x, mod, pos_t, o)


@functools.lru_cache(maxsize=None)
def _np_consts():
    lane = np.arange(GROUP_W)
    seg32 = (lane[:, None] // DIFF_D == lane[None, :] // DIFF_D).astype(np.float32)
    seg64 = (lane[:, None] // HEAD_DIM == lane[None, :] // HEAD_DIM).astype(np.float32)
    ang = 2.0 * np.pi * ((lane[:, None] % HEAD_DIM) * (lane[None, :] % HEAD_DIM) % HEAD_DIM) / HEAD_DIM
    cc = np.cos(ang) * seg64
    ss = np.sin(ang) * seg64
    tri = (lane[:, None] <= lane[None, :]).astype(np.float32)
    return dict(seg32=seg32, seg64=seg64, cc=cc, ss=ss, tri=tri)


@functools.lru_cache(maxsize=None)
def _np_dft(n):
    idx = (np.arange(n, dtype=np.int64)[:, None] * np.arange(n, dtype=np.int64)[None, :]) % n
    ang = 2.0 * np.pi * idx.astype(np.float64) / n
    return np.cos(ang).astype(np.float32), np.sin(ang).astype(np.float32)


@functools.lru_cache(maxsize=None)
def _np_rope(n):
    half = DIFF_D // 2
    inv = 1.0 / (ROPE_BASE ** (np.arange(0, half, 2, dtype=np.float32) / half))
    t = np.arange(n)
    row = (t // GRID_W).astype(np.float32)[:, None] * inv
    col = (t % GRID_W).astype(np.float32)[:, None] * inv
    nf = inv.shape[0]
    d = np.arange(GROUP_W) % DIFF_D
    f = d % nf
    is_col = d >= half
    second = (d % half) >= nf
    ang = np.where(is_col[None, :], col[:, f], row[:, f])
    c = np.cos(ang).astype(np.float32)
    s = np.sin(ang).astype(np.float32)
    s = np.where(second[None, :], s, -s)
    return c, s


@functools.lru_cache(maxsize=None)
def _np_na_index(n_rows):
    n_steps = n_rows // NA_QROWS
    reps = [0, 1, 2, n_steps - 2, n_steps - 1]
    tq = NA_QROWS * GRID_W
    roff = np.zeros((len(reps), NA_QROWS, NA_WIN_ROWS), np.int32)
    valid = np.zeros((len(reps), tq, NA_WIN), bool)
    for ci, t in enumerate(reps):
        ws = int(np.clip(NA_QROWS * t - NA_ROWS // 2, 0, n_rows - NA_WIN_ROWS))
        qi = np.arange(tq)
        r = NA_QROWS * t + qi // GRID_W
        qcol = qi % GRID_W
        kk = np.arange(NA_WIN)
        krow = ws + kk // GRID_W
        kcol = kk % GRID_W
        rstart = np.clip(r - NA_ROWS // 2, 0, n_rows - NA_ROWS)
        wstart = np.clip(qcol - NA_COLS // 2, 0, GRID_W - NA_COLS)
        vr = (krow[None, :] >= rstart[:, None]) & (krow[None, :] < rstart[:, None] + NA_ROWS)
        vc = (kcol[None, :] >= wstart[:, None]) & (kcol[None, :] < wstart[:, None] + NA_COLS)
        valid[ci] = vr & vc
        rows_q = NA_QROWS * t + np.arange(NA_QROWS)
        rows_k = ws + np.arange(NA_WIN_ROWS)
        roff[ci] = np.clip(rows_k[None, :] - rows_q[:, None] + NA_ROWS - 1, 0, 2 * NA_ROWS - 2)
    return roff, valid


def _na_table(rpb, n_rows):
    roff, valid = _np_na_index(n_rows)
    n_cls = roff.shape[0]
    padw = GRID_W - NA_COLS
    padded = jnp.pad(rpb.astype(F32), ((0, 0), (0, 0), (padw, padw)))
    toep = jnp.stack([padded[:, :, GRID_W - 1 - q:2 * GRID_W - 1 - q] for q in range(GRID_W)], axis=2)
    blocks = jnp.take(toep, roff.reshape(-1), axis=1)
    blocks = blocks.reshape(HEADS, n_cls, NA_QROWS, NA_WIN_ROWS, GRID_W, GRID_W)
    bias = blocks.transpose(1, 0, 2, 4, 3, 5).reshape(n_cls, HEADS, NA_QROWS * GRID_W, NA_WIN)
    bias = jnp.where(valid[:, None], bias * LOG2E, NEG_INF)
    return bias.reshape(n_cls, HEADS * NA_QROWS * GRID_W, NA_WIN)


def _moe(x, xn, logits, mod, tri, wg, wu, wd, layer, bb, tn):
    n = x.shape[1]
    cap = CAPACITY_FACTOR * n // N_EXPERTS
    pos, gate, pos_t, starts = _route(logits, tri, cap, tn)
    starts = starts[:, :, :n // tn].transpose(0, 2, 1).reshape(-1)
    o = _expert_ffn(xn, pos, gate, starts, wg, wu, wd, layer, cap, bb, tn)
    return _scatter(x, mod, pos_t, starts, o, cap, tn)


def kernel(x, c, ctx, c_ctx, ada_w, ada_b, norm1_g, norm2_g, w_in, w_out, head_out_g, sgu_w, sgu_b, diff_qn_g, diff_kn_g, diff_lambda, na_qn_g, na_kn_g, na_rpb, router_w, exp_w_gate, exp_w_up, exp_w_down):
    b, n, d = x.shape
    n_ctx = ctx.shape[1]
    npc = _np_consts()
    consts = {k: jnp.asarray(v, F32).astype(BF16) for k, v in npc.items()}
    seg64, tri = consts["seg64"], consts["tri"]
    cn, sn = (jnp.asarray(a, F32).astype(BF16) for a in _np_dft(n))
    cn_c, sn_c = (jnp.asarray(a, F32).astype(BF16) for a in _np_dft(n_ctx))
    rope = tuple(jnp.asarray(a, F32) for a in _np_rope(n))

    pad = (-(b + 1)) % 8
    c_rows = jnp.concatenate([c, c_ctx[None, :], jnp.zeros((pad, d), F32)], axis=0)
    mod_all = _modulation(c_rows, ada_w, ada_b)

    xc = ctx
    for l in range(DEPTH):
        last = l == DEPTH - 1
        lam_init = 0.8 - 0.6 * math.exp(-0.3 * l)
        mod = mod_all[l, :b][:, None, :]
        mod_c = jnp.broadcast_to(mod_all[l, b][None, None, :], (b, 1, 6 * d))
        g1 = norm1_g[l][None, :]
        g2 = norm2_g[l][None, :]
        w_in_l = w_in[l].astype(BF16)
        w_out_l = w_out[l].astype(BF16)
        sguw = sgu_w[l].astype(BF16)
        sgub = jnp.repeat(sgu_b[l].T, HEAD_DIM, axis=1)
        hg = head_out_g[l].reshape(4, GROUP_W)
        vec = jnp.stack([jnp.tile(diff_qn_g[l], GROUP_W // DIFF_D), jnp.tile(diff_kn_g[l], GROUP_W // DIFF_D),
                         jnp.tile(na_qn_g[l], HEADS), jnp.tile(na_kn_g[l], HEADS),
                         hg[0], hg[1], hg[2], hg[3],
                         jnp.broadcast_to(jnp.max(jnp.abs(na_rpb[l])) * LOG2E, (GROUP_W,))], axis=0).astype(F32)
        vec = jnp.pad(vec, ((0, VEC_ROWS - vec.shape[0]), (0, 0)))
        lam = diff_lambda[l].astype(F32)
        table = _na_table(na_rpb[l], n // GRID_W)
        rwt = router_w[l].T.astype(BF16)
        experts = (exp_w_gate, exp_w_up, exp_w_down)

        ya, zc, zs, qc, kct, vc, qd, kdt, vd = _in_proj(x, mod, g1, w_in_l, consts, sguw, sgub, vec, rope, 512)
        ya_c, zc_c, zs_c, qc_c, kct_c, vc_c, qd_c, kdt_c, vd_c = _in_proj(
            xc, mod_c, g1, w_in_l, consts, sguw, sgub, vec, None, n_ctx)

        yb = _fourier(zc, zs, cn, sn, seg64, vec, 1024)
        yc = _attention(qc, [(kct, vc), (kct_c, vc_c)], lam, lam_init, seg64, vec, 6, 1.0 - lam_init, 512, 512)
        yd = _na_attention(qd, kdt, vd, kdt_c, vd_c, table, seg64, vec, 4)
        x, xn, logits = _out_proj(ya, yb, yc, yd, w_out_l, x, mod, g2, rwt, 512)
        x = _moe(x, xn, logits, mod, tri, *experts, l, 1, 512)

        if not last:
            yb_c = _fourier(zc_c, zs_c, cn_c, sn_c, seg64, vec, n_ctx)
            yc_c = _attention(qc_c, [(kct_c, vc_c)], lam, lam_init, seg64, vec, 6, 1.0 - lam_init, n_ctx)
            yd_c = _attention(qd_c, [(kdt_c, vd_c)], None, 0.0, seg64, vec, 7, 1.0, n_ctx)
            xc, xn_c, logits_c = _out_proj(ya_c, yb_c, yc_c, yd_c, w_out_l, xc, mod_c, g2, rwt, n_ctx)
            xc = _moe(xc, xn_c, logits_c, mod_c, tri, *experts, l, b, n_ctx)
    return x
```

```python
import functools
import math

import numpy as np
import jax
import jax.numpy as jnp
from jax import lax
from jax.experimental import pallas as pl
from jax.experimental.pallas import tpu as pltpu

F32 = jnp.float32
BF16 = jnp.bfloat16
I32 = jnp.int32

D_MODEL = 1024
DEPTH = 2
GRID_W = 64
LANES = 128
HEAD_DIM = 64
LOG2_HEAD_DIM = 6
GROUP_W = 256
HEADS = GROUP_W // HEAD_DIM
CHUNK = 128
DIFF_D = HEAD_DIM // 2
NA_ROWS = 8
NA_COLS = 16
N_EXPERTS = 16
CAPACITY_FACTOR = 2
ROPE_BASE = 10000.0
EPS = 1e-6
IN_W = 9 * GROUP_W
LOG2E = 1.4426950408889634

VMEM_LIMIT_BYTES = 56 * 1024 * 1024
NA_QROWS = 2
NA_WIN_ROWS = NA_ROWS + 2
NA_WIN = NA_WIN_ROWS * GRID_W
NA_HEAD_STACK = 1
NEG_INF = float("-inf")
VEC_ROWS = 16
SLOT_WINDOW = 128
SAFE_EXP2_BOUND = 48.0
BOUND_SLACK = 1.02


def _dot(a, b):
    return jnp.dot(a, b, preferred_element_type=F32)


def _params(*sem):
    return pltpu.CompilerParams(dimension_semantics=sem, vmem_limit_bytes=VMEM_LIMIT_BYTES)


def _full(shape):
    nd = len(shape)
    return pl.BlockSpec(shape, lambda *_: (0,) * nd)


def _seg_rms(x, seg, width):
    ss = _dot((x * x).astype(BF16), seg)
    return x * lax.rsqrt(ss * (1.0 / width) + EPS)


def _mod_kernel(c_ref, w_ref, b_ref, o_ref):
    s = jax.nn.silu(c_ref[...]).astype(BF16)
    o_ref[0] = _dot(s, w_ref[0].astype(BF16)) + b_ref[0]


def _modulation(c_rows, ada_w, ada_b):
    depth, d, w6 = ada_w.shape
    r = c_rows.shape[0]
    tn = 1024
    return pl.pallas_call(
        _mod_kernel,
        out_shape=jax.ShapeDtypeStruct((depth, r, w6), F32),
        grid=(depth, w6 // tn),
        in_specs=[
            pl.BlockSpec((r, d), lambda l, j: (0, 0)),
            pl.BlockSpec((1, d, tn), lambda l, j: (l, 0, j)),
            pl.BlockSpec((1, 1, tn), lambda l, j: (l, 0, j)),
        ],
        out_specs=pl.BlockSpec((1, r, tn), lambda l, j: (l, 0, j)),
        compiler_params=_params("arbitrary", "arbitrary"),
        name="modulation",
    )(c_rows, ada_w, ada_b.reshape(depth, 1, w6))


def _rope(x, c, s, lane):
    fwd = pltpu.roll(x, GROUP_W - 8, 1)
    bwd = pltpu.roll(x, 8, 1)
    partner = jnp.where((lane & 8) == 0, fwd, bwd)
    return x * c + partner * s


def _in_kernel(*refs, tm, use_rope, c_scale, d_scale):
    (x_ref, mod_ref, g1_ref, w_ref, seg32_ref, seg64_ref, cc_ref, ss_ref, sguw_ref, sgub_ref, vec_ref) = refs[:11]
    rest = refs[11:]
    if use_rope:
        ropec_ref, ropes_ref = rest[:2]
        rest = rest[2:]
    ya_ref, zc_ref, zs_ref, qc_ref, kct_ref, vc_ref, qd_ref, kdt_ref, vd_ref = rest

    x = x_ref[0]
    mod = mod_ref[0]
    sh = mod[:, 0:D_MODEL]
    sc = mod[:, D_MODEL:2 * D_MODEL]
    ms = jnp.mean(x * x, axis=-1, keepdims=True)
    h = x * lax.rsqrt(ms + EPS) * g1_ref[...]
    h = (h * (1.0 + sc) + sh).astype(BF16)
    lane = lax.broadcasted_iota(I32, (1, GROUP_W), 1)
    head = lane >> LOG2_HEAD_DIM
    seg32 = seg32_ref[...]
    seg64 = seg64_ref[...]
    vec = vec_ref[...]

    z = jax.nn.gelu(_dot(h, w_ref[:, 0:2 * GROUP_W]))
    u = z[:, 0:GROUP_W]
    vn = _seg_rms(z[:, GROUP_W:2 * GROUP_W], seg64, HEAD_DIM).astype(BF16)
    rows = []
    for c in range(tm // CHUNK):
        vch = vn[c * CHUNK:(c + 1) * CHUNK]
        sv = jnp.zeros((CHUNK, GROUP_W), F32)
        for hh in range(HEADS):
            sv = jnp.where(head == hh, _dot(sguw_ref[hh], vch), sv)
        rows.append(sv + sgub_ref[...])
    ya = u * jnp.concatenate(rows, axis=0)
    ya_ref[0] = (_seg_rms(ya, seg64, HEAD_DIM) * vec[4:5]).astype(BF16)

    zb = _dot(h, w_ref[:, 2 * GROUP_W:3 * GROUP_W]).astype(BF16)
    zc_ref[0] = _dot(zb, cc_ref[...]).astype(BF16)
    zs_ref[0] = _dot(zb, ss_ref[...]).astype(BF16)

    pc = _dot(h, w_ref[:, 3 * GROUP_W:6 * GROUP_W])
    q = _seg_rms(pc[:, 0:GROUP_W], seg32, DIFF_D) * vec[0:1]
    k = _seg_rms(pc[:, GROUP_W:2 * GROUP_W], seg32, DIFF_D) * vec[1:2]
    if use_rope:
        rc = ropec_ref[...]
        rs = ropes_ref[...]
        q = _rope(q, rc, rs, lane)
        k = _rope(k, rc, rs, lane)
    qc_ref[0] = (q * c_scale).astype(BF16)
    kct_ref[0] = k.T.astype(BF16)
    vc_ref[0] = pc[:, 2 * GROUP_W:3 * GROUP_W].astype(BF16)

    pd = _dot(h, w_ref[:, 6 * GROUP_W:9 * GROUP_W])
    qd = _seg_rms(pd[:, 0:GROUP_W], seg64, HEAD_DIM) * vec[2:3]
    kd = _seg_rms(pd[:, GROUP_W:2 * GROUP_W], seg64, HEAD_DIM) * vec[3:4]
    qd_ref[0] = (qd * d_scale).astype(BF16)
    kdt_ref[0] = kd.T.astype(BF16)
    vd_ref[0] = pd[:, 2 * GROUP_W:3 * GROUP_W].astype(BF16)


def _in_proj(x, mod, g1, w_in, consts, sguw, sgub, vec, rope, tm):
    b, n, d = x.shape
    use_rope = rope is not None
    tok = pl.BlockSpec((1, tm, GROUP_W), lambda i, t: (i, t, 0))
    tok_t = pl.BlockSpec((1, GROUP_W, tm), lambda i, t: (i, 0, t))
    in_specs = [
        pl.BlockSpec((1, tm, d), lambda i, t: (i, t, 0)),
        pl.BlockSpec((1, 1, 6 * d), lambda i, t: (i, 0, 0)),
        _full((1, d)),
        _full((d, IN_W)),
        _full((GROUP_W, GROUP_W)), _full((GROUP_W, GROUP_W)), _full((GROUP_W, GROUP_W)), _full((GROUP_W, GROUP_W)),
        _full((HEADS, CHUNK, CHUNK)),
        _full((CHUNK, GROUP_W)),
        _full((VEC_ROWS, GROUP_W)),
    ]
    args = [x, mod, g1, w_in, consts["seg32"], consts["seg64"], consts["cc"], consts["ss"], sguw, sgub, vec]
    if use_rope:
        in_specs += [pl.BlockSpec((tm, GROUP_W), lambda i, t: (t, 0))] * 2
        args += list(rope)
    sd = jax.ShapeDtypeStruct((b, n, GROUP_W), BF16)
    sdt = jax.ShapeDtypeStruct((b, GROUP_W, n), BF16)
    kern = functools.partial(
        _in_kernel, tm=tm, use_rope=use_rope,
        c_scale=(DIFF_D ** -0.5) * LOG2E, d_scale=(HEAD_DIM ** -0.5) * LOG2E)
    return pl.pallas_call(
        kern,
        out_shape=(sd, sd, sd, sd, sdt, sd, sd, sdt, sd),
        grid=(b, n // tm),
        in_specs=in_specs,
        out_specs=(tok, tok, tok, tok, tok_t, tok, tok, tok_t, tok),
        compiler_params=_params("arbitrary", "arbitrary"),
        name="in_proj",
    )(*args)


def _fourier_kernel(cn_ref, sn_ref, zc_ref, zs_ref, seg64_ref, vec_ref, o_ref, *, norm):
    y = (_dot(cn_ref[...], zc_ref[0]) - _dot(sn_ref[...], zs_ref[0])) * norm
    o_ref[0] = (_seg_rms(y, seg64_ref[...], HEAD_DIM) * vec_ref[5:6]).astype(BF16)


def _fourier(zc, zs, cn, sn, seg64, vec, tn):
    b, n, _ = zc.shape
    kern = functools.partial(_fourier_kernel, norm=1.0 / math.sqrt(n * HEAD_DIM))
    return pl.pallas_call(
        kern,
        out_shape=jax.ShapeDtypeStruct((b, n, GROUP_W), BF16),
        grid=(n // tn, b),
        in_specs=[
            pl.BlockSpec((tn, n), lambda t, i: (t, 0)),
            pl.BlockSpec((tn, n), lambda t, i: (t, 0)),
            pl.BlockSpec((1, n, GROUP_W), lambda t, i: (i, 0, 0)),
            pl.BlockSpec((1, n, GROUP_W), lambda t, i: (i, 0, 0)),
            _full((GROUP_W, GROUP_W)),
            _full((VEC_ROWS, GROUP_W)),
        ],
        out_specs=pl.BlockSpec((1, tn, GROUP_W), lambda t, i: (i, t, 0)),
        compiler_params=_params("arbitrary", "arbitrary"),
        name="fourier",
    )(cn, sn, zc, zs, seg64, vec)


def _attn_kernel(*refs, n_src, diff, lam_init, chunk, tq, vec_row, out_scale):
    q_ref = refs[0]
    rest = refs[1 + 2 * n_src:]
    if diff:
        lam_ref = rest[0]
        rest = rest[1:]
    seg64_ref, vec_ref, o_ref = rest

    q = q_ref[0]
    lane = lax.broadcasted_iota(I32, (1, GROUP_W), 1)
    if diff:
        lf = lam_ref[...]
        lam = (jnp.exp(jnp.sum(lf[0:1] * lf[1:2], axis=-1, keepdims=True))
               - jnp.exp(jnp.sum(lf[2:3] * lf[3:4], axis=-1, keepdims=True)) + lam_init)

    chunks = []
    for i in range(n_src):
        kt_ref, v_ref = refs[1 + 2 * i], refs[2 + 2 * i]
        nk = kt_ref.shape[2]
        chunks += [(kt_ref, v_ref, c0, min(chunk, nk - c0)) for c0 in range(0, nk, chunk)]

    width = DIFF_D if diff else HEAD_DIM
    gq = vec_ref[0:1] if diff else vec_ref[2:3]
    gk = vec_ref[1:2] if diff else vec_ref[3:4]
    bound = jnp.max(jnp.abs(gq)) * jnp.max(jnp.abs(gk)) * (math.sqrt(width) * LOG2E * BOUND_SLACK)

    def attend_bounded(sel):
        qm = jnp.where(sel, q, jnp.zeros_like(q))
        l = jnp.zeros((tq, 1), F32)
        acc = jnp.zeros((tq, GROUP_W), F32)
        for kt_ref, v_ref, c0, ck in chunks:
            p = jnp.exp2(_dot(qm, kt_ref[0, :, c0:c0 + ck]) - bound)
            l = l + jnp.sum(p, axis=-1, keepdims=True)
            acc = acc + _dot(p.astype(BF16), v_ref[0, c0:c0 + ck, :])
        return acc * (1.0 / l)

    def attend_online(sel):
        qm = jnp.where(sel, q, jnp.zeros_like(q))
        m = jnp.full((tq, 1), NEG_INF, F32)
        l = jnp.zeros((tq, 1), F32)
        acc = jnp.zeros((tq, GROUP_W), F32)
        for kt_ref, v_ref, c0, ck in chunks:
            s = _dot(qm, kt_ref[0, :, c0:c0 + ck])
            m_new = jnp.maximum(m, jnp.max(s, axis=-1, keepdims=True))
            alpha = jnp.exp2(m - m_new)
            p = jnp.exp2(s - m_new)
            l = alpha * l + jnp.sum(p, axis=-1, keepdims=True)
            acc = alpha * acc + _dot(p.astype(BF16), v_ref[0, c0:c0 + ck, :])
            m = m_new
        return acc * (1.0 / l)

    def head_out(attend, h):
        if diff:
            return (attend((lane >> (LOG2_HEAD_DIM - 1)) == 2 * h)
                    - lam * attend((lane >> (LOG2_HEAD_DIM - 1)) == 2 * h + 1))
        return attend((lane >> LOG2_HEAD_DIM) == h)

    def finish(out):
        y = _seg_rms(out, seg64_ref[...], HEAD_DIM) * vec_ref[vec_row:vec_row + 1]
        o_ref[0] = (y * out_scale).astype(BF16)

    def run_bounded():
        out = jnp.zeros((tq, GROUP_W), F32)
        for h in range(HEADS):
            out = jnp.where((lane >> LOG2_HEAD_DIM) == h, head_out(attend_bounded, h), out)
        finish(out)

    def run_online():
        def body(h, out):
            return jnp.where((lane >> LOG2_HEAD_DIM) == h, head_out(attend_online, h), out)

        finish(lax.fori_loop(0, HEADS, body, jnp.zeros((tq, GROUP_W), F32)))

    small = bound <= SAFE_EXP2_BOUND
    pl.when(small)(run_bounded)
    pl.when(jnp.logical_not(small))(run_online)


def _attention(q, srcs, lam, lam_init, seg64, vec, vec_row, out_scale, tq, chunk=768):
    b, nq, _ = q.shape
    diff = lam is not None
    in_specs = [pl.BlockSpec((1, tq, GROUP_W), lambda i, t: (i, t, 0))]
    args = [q]
    for kt, v in srcs:
        nk = v.shape[1]
        in_specs += [pl.BlockSpec((1, GROUP_W, nk), lambda i, t: (i, 0, 0)),
                     pl.BlockSpec((1, nk, GROUP_W), lambda i, t: (i, 0, 0))]
        args += [kt, v]
    if diff:
        in_specs.append(_full((4, DIFF_D)))
        args.append(lam)
    in_specs += [_full((GROUP_W, GROUP_W)), _full((VEC_ROWS, GROUP_W))]
    args += [seg64, vec]
    kern = functools.partial(_attn_kernel, n_src=len(srcs), diff=diff, lam_init=lam_init, chunk=chunk, tq=tq,
                             vec_row=vec_row, out_scale=out_scale)
    return pl.pallas_call(
        kern,
        out_shape=jax.ShapeDtypeStruct((b, nq, GROUP_W), BF16),
        grid=(b, nq // tq),
        in_specs=in_specs,
        out_specs=pl.BlockSpec((1, tq, GROUP_W), lambda i, t: (i, t, 0)),
        compiler_params=_params("arbitrary", "arbitrary"),
        name="diff_attention" if diff else "ctx_attention",
    )(*args)


def _na_kernel(q_ref, kt_ref, v_ref, ktc_ref, vc_ref, tab_ref, seg64_ref, vec_ref, o_ref, *, n_rows, pairs):
    n_steps = n_rows // NA_QROWS
    tq = NA_QROWS * GRID_W
    lane = lax.broadcasted_iota(I32, (1, GROUP_W), 1)
    head = lane >> LOG2_HEAD_DIM
    ktc = ktc_ref[0]
    vc = vc_ref[0]
    bound = (jnp.max(jnp.abs(vec_ref[2:3])) * jnp.max(jnp.abs(vec_ref[3:4])) * (math.sqrt(HEAD_DIM) * LOG2E * BOUND_SLACK)
             + jnp.max(vec_ref[8:9]))

    def run(bounded):
        for pi in range(pairs):
            t = pl.program_id(1) * pairs + pi
            ws = jnp.clip(NA_QROWS * t - NA_ROWS // 2, 0, n_rows - NA_WIN_ROWS)
            k0 = pl.multiple_of(ws * GRID_W, 128)
            tid = jnp.where(t < 2, t, jnp.where(t < n_steps - 2, 2, t - (n_steps - 5)))
            q = q_ref[0, pi * tq:(pi + 1) * tq, :]
            out = jnp.zeros((tq, GROUP_W), F32)
            for h0 in range(0, HEADS, NA_HEAD_STACK):
                hs = range(h0, h0 + NA_HEAD_STACK)
                qs = jnp.concatenate([jnp.where(head == hh, q, jnp.zeros_like(q)) for hh in hs], axis=0)
                s_loc = (_dot(qs, kt_ref[0, :, pl.ds(k0, NA_WIN)])
                         + tab_ref[0, tid, h0 * tq:(h0 + NA_HEAD_STACK) * tq, :])
                s_ctx = _dot(qs, ktc)
                if bounded:
                    m = bound
                else:
                    m = jnp.maximum(jnp.max(s_loc, axis=-1, keepdims=True), jnp.max(s_ctx, axis=-1, keepdims=True))
                p_loc = jnp.exp2(s_loc - m)
                p_ctx = jnp.exp2(s_ctx - m)
                l = jnp.sum(p_loc, axis=-1, keepdims=True) + jnp.sum(p_ctx, axis=-1, keepdims=True)
                o = _dot(p_loc.astype(BF16), v_ref[0, pl.ds(k0, NA_WIN), :]) + _dot(p_ctx.astype(BF16), vc)
                o = o * (1.0 / l)
                for i, hh in enumerate(hs):
                    out = jnp.where(head == hh, o[i * tq:(i + 1) * tq], out)
            o_ref[0, pi * tq:(pi + 1) * tq, :] = (
                _seg_rms(out, seg64_ref[...], HEAD_DIM) * vec_ref[7:8]).astype(BF16)

    small = bound <= SAFE_EXP2_BOUND
    pl.when(small)(lambda: run(True))
    pl.when(jnp.logical_not(small))(lambda: run(False))


def _na_attention(q, kt, v, ktc, vc, tables, layer, seg64, vec, pairs):
    b, n, _ = q.shape
    nc = vc.shape[1]
    tq = pairs * NA_QROWS * GRID_W
    kern = functools.partial(_na_kernel, n_rows=n // GRID_W, pairs=pairs)
    return pl.pallas_call(
        kern,
        out_shape=jax.ShapeDtypeStruct((b, n, GROUP_W), BF16),
        grid=(b, n // tq),
        in_specs=[
            pl.BlockSpec((1, tq, GROUP_W), lambda i, t: (i, t, 0)),
            pl.BlockSpec((1, GROUP_W, n), lambda i, t: (i, 0, 0)),
            pl.BlockSpec((1, n, GROUP_W), lambda i, t: (i, 0, 0)),
            pl.BlockSpec((1, GROUP_W, nc), lambda i, t: (i, 0, 0)),
            pl.BlockSpec((1, nc, GROUP_W), lambda i, t: (i, 0, 0)),
            pl.BlockSpec((1,) + tables.shape[1:], lambda i, t: (layer, 0, 0, 0)),
            _full((GROUP_W, GROUP_W)),
            _full((VEC_ROWS, GROUP_W)),
        ],
        out_specs=pl.BlockSpec((1, tq, GROUP_W), lambda i, t: (i, t, 0)),
        compiler_params=_params("arbitrary", "arbitrary"),
        name="neighbourhood_attention",
    )(q, kt, v, ktc, vc, tables, seg64, vec)


def _out_kernel(ya_ref, yb_ref, yc_ref, yd_ref, w_ref, x_ref, mod_ref, g2_ref, rwt_ref, o_ref, xn_ref, lg_ref):
    y = jnp.concatenate([ya_ref[0], yb_ref[0], yc_ref[0], yd_ref[0]], axis=-1)
    mod = mod_ref[0]
    x = x_ref[0] + mod[:, 2 * D_MODEL:3 * D_MODEL] * _dot(y, w_ref[...])
    o_ref[0] = x
    sh = mod[:, 3 * D_MODEL:4 * D_MODEL]
    sc = mod[:, 4 * D_MODEL:5 * D_MODEL]
    ms = jnp.mean(x * x, axis=-1, keepdims=True)
    h = x * lax.rsqrt(ms + EPS) * g2_ref[...]
    h = (h * (1.0 + sc) + sh).astype(BF16)
    xn_ref[0] = h
    lg_ref[0] = lax.dot_general(rwt_ref[...], h, (((1,), (1,)), ((), ())), preferred_element_type=F32)


def _out_proj(ya, yb, yc, yd, w_out, x, mod, g2, rwt, tm):
    b, n, d = x.shape
    tok = pl.BlockSpec((1, tm, GROUP_W), lambda i, t: (i, t, 0))
    xs = pl.BlockSpec((1, tm, d), lambda i, t: (i, t, 0))
    return pl.pallas_call(
        _out_kernel,
        out_shape=(jax.ShapeDtypeStruct((b, n, d), F32),
                   jax.ShapeDtypeStruct((b, n, d), BF16),
                   jax.ShapeDtypeStruct((b, N_EXPERTS, n), F32)),
        grid=(b, n // tm),
        in_specs=[tok, tok, tok, tok, _full((4 * GROUP_W, d)), xs,
                  pl.BlockSpec((1, 1, 6 * d), lambda i, t: (i, 0, 0)),
                  _full((1, d)), _full((N_EXPERTS, d))],
        out_specs=(xs, xs, pl.BlockSpec((1, N_EXPERTS, tm), lambda i, t: (i, 0, t))),
        compiler_params=_params("arbitrary", "arbitrary"),
        name="out_proj",
    )(ya, yb, yc, yd, w_out, x, mod, g2, rwt)


def _cumsum_excl(m, tri):
    n = m.shape[1]
    carry = jnp.zeros((m.shape[0], 1), F32)
    outs = []
    for j in range(n // GROUP_W):
        blk = m[:, j * GROUP_W:(j + 1) * GROUP_W]
        inc = _dot(blk.astype(BF16), tri)
        outs.append(inc - blk + carry)
        carry = carry + inc[:, GROUP_W - 1:GROUP_W]
    return jnp.concatenate(outs, axis=1)


def _route_kernel(lg_ref, tri_ref, pos_ref, gate_ref, post_ref, st_ref, *, n, cap, tile):
    lg = lg_ref[0]
    e = jnp.exp(lg - jnp.max(lg, axis=0, keepdims=True))
    aff = e / jnp.sum(e, axis=0, keepdims=True)
    gate_ref[0] = aff

    def unresolved(state):
        lo, hi = state
        return jnp.max(jnp.where(lo < hi, 1.0, 0.0)) > 0.0

    def bisect(state):
        lo, hi = state
        mid = 0.5 * (lo + hi)
        mid = jnp.where(mid > lo, mid, hi)
        ge = aff >= mid
        cnt = jnp.sum(jnp.where(ge, 1.0, 0.0), axis=1, keepdims=True)
        least_ge = jnp.min(jnp.where(ge, aff, jnp.inf), axis=1, keepdims=True)
        most_lt = jnp.max(jnp.where(ge, NEG_INF, aff), axis=1, keepdims=True)
        up = cnt >= cap
        return jnp.where(up, least_ge, lo), jnp.where(up, hi, most_lt)

    thr, _ = lax.while_loop(unresolved, bisect, (jnp.min(aff, axis=1, keepdims=True),
                                                 jnp.max(aff, axis=1, keepdims=True)))
    gt = aff > thr
    eq = aff == thr
    need = cap - jnp.sum(jnp.where(gt, 1.0, 0.0), axis=1, keepdims=True)
    tri = tri_ref[...]
    rank_eq = _cumsum_excl(jnp.where(eq, 1.0, 0.0), tri)
    sel = jnp.where(gt, 1.0, jnp.where(eq, jnp.where(rank_eq < need, 1.0, 0.0), 0.0))
    cum = _cumsum_excl(sel, tri)
    pos = jnp.where(sel > 0.0, cum, -1.0)
    pos_ref[0] = pos.astype(I32)
    tile_lane = lax.broadcasted_iota(I32, (N_EXPERTS, LANES), 1)
    starts = jnp.zeros((N_EXPERTS, LANES), F32)
    for t in range(n // tile):
        starts = jnp.where(tile_lane == t, cum[:, t * tile:t * tile + 1], starts)
    st_ref[0] = starts.astype(I32)
    padded = jnp.concatenate([pos, jnp.full((LANES - N_EXPERTS, n), -1.0, F32)], axis=0)
    post_ref[0] = padded.T.astype(I32)


def _route(logits, tri, cap, tile):
    b, _, n = logits.shape
    em = pl.BlockSpec((1, N_EXPERTS, n), lambda i: (i, 0, 0))
    kern = functools.partial(_route_kernel, n=n, cap=cap, tile=tile)
    return pl.pallas_call(
        kern,
        out_shape=(jax.ShapeDtypeStruct((b, N_EXPERTS, n), I32),
                   jax.ShapeDtypeStruct((b, N_EXPERTS, n), F32),
                   jax.ShapeDtypeStruct((b, n, LANES), I32),
                   jax.ShapeDtypeStruct((b, N_EXPERTS, LANES), I32)),
        grid=(b,),
        in_specs=[em, _full((GROUP_W, GROUP_W))],
        out_specs=(em, em, pl.BlockSpec((1, n, LANES), lambda i: (i, 0, 0)),
                   pl.BlockSpec((1, N_EXPERTS, LANES), lambda i: (i, 0, 0))),
        compiler_params=_params("arbitrary"),
        name="router",
    )(logits, tri)


def _ffn_kernel(st_ref, xn_ref, pos_ref, gate_ref, wg32_ref, wu32_ref, wd32_ref, o_ref, wg_ref, wu_ref, wd_ref, xg_ref,
                *, bb, cap, n, tile):
    @pl.when(pl.program_id(1) == 0)
    def _():
        wg_ref[...] = wg32_ref[0, 0].astype(BF16)
        wu_ref[...] = wu32_ref[0, 0].astype(BF16)
        wd_ref[...] = wd32_ref[0, 0].astype(BF16)

    expert = pl.ds(pl.program_id(0), 1)

    def step(gather):
        slot = lax.broadcasted_iota(I32, (cap, n), 0)
        hits = [pos_ref[i, expert, :] == slot for i in range(bb)]
        gs = [jnp.sum(jnp.where(hits[i], gate_ref[i, expert, :], 0.0), axis=1, keepdims=True) for i in range(bb)]
        g = gs[0] if bb == 1 else jnp.concatenate(gs, axis=0)
        gather(hits)
        xg = xg_ref[...].astype(BF16)
        hid = (jax.nn.silu(_dot(xg, wg_ref[...])) * _dot(xg, wu_ref[...])).astype(BF16)
        o = _dot(hid, wd_ref[...]) * g
        for i in range(bb):
            o_ref[i, 0] = o[i * cap:(i + 1) * cap].astype(BF16)

    def gather_dense(hits):
        for i in range(bb):
            xg_ref[i * cap:(i + 1) * cap, :] = _dot(jnp.where(hits[i], 1.0, 0.0).astype(BF16), xn_ref[i])

    if bb > 1 or cap <= SLOT_WINDOW:
        step(gather_dense)
        return

    nt = n // tile
    base = pl.program_id(1) * nt * N_EXPERTS + pl.program_id(0)
    win, fits = [], True
    for t in range(nt):
        start = st_ref[base + t * N_EXPERTS]
        end = st_ref[base + (t + 1) * N_EXPERTS] if t + 1 < nt else cap
        a = jnp.minimum((start >> 4) << 4, cap - SLOT_WINDOW)
        win.append(a)
        fits = jnp.logical_and(fits, end - a <= SLOT_WINDOW)

    def gather_windowed(hits):
        del hits
        xg_ref[...] = jnp.zeros_like(xg_ref)
        wslot = lax.broadcasted_iota(I32, (SLOT_WINDOW, tile), 0)
        for t in range(nt):
            p = pos_ref[0, expert, t * tile:(t + 1) * tile]
            onehot = jnp.where(p - win[t] == wslot, 1.0, 0.0).astype(BF16)
            rows = pl.ds(pl.multiple_of(win[t], 16), SLOT_WINDOW)
            xg_ref[rows, :] = xg_ref[rows, :] + _dot(onehot, xn_ref[0, t * tile:(t + 1) * tile, :])

    pl.when(fits)(lambda: step(gather_windowed))
    pl.when(jnp.logical_not(fits))(lambda: step(gather_dense))


def _expert_ffn(xn, pos, gate, starts, wg, wu, wd, layer, cap, bb, tile):
    b, n, d = xn.shape
    wspec = pl.BlockSpec((1, 1, d, d), lambda e, j, st: (layer, e, 0, 0))
    sel = pl.BlockSpec((bb, N_EXPERTS, n), lambda e, j, st: (j, 0, 0))
    kern = functools.partial(_ffn_kernel, bb=bb, cap=cap, n=n, tile=tile)
    return pl.pallas_call(
        kern,
        out_shape=jax.ShapeDtypeStruct((b, N_EXPERTS, cap, d), BF16),
        grid_spec=pltpu.PrefetchScalarGridSpec(
            num_scalar_prefetch=1,
            grid=(N_EXPERTS, b // bb),
            in_specs=[pl.BlockSpec((bb, n, d), lambda e, j, st: (j, 0, 0)), sel, sel, wspec, wspec, wspec],
            out_specs=pl.BlockSpec((bb, 1, cap, d), lambda e, j, st: (j, e, 0, 0)),
            scratch_shapes=[pltpu.VMEM((d, d), BF16)] * 3 + [pltpu.VMEM((bb * cap, d), F32)]),
        compiler_params=_params("arbitrary", "arbitrary"),
        name="expert_ffn",
    )(starts, xn, pos, gate, wg, wu, wd)


def _scatter_kernel(st_ref, x_ref, mod_ref, pt_ref, o_ref, out_ref, *, tn, cap, nt):
    pos_t = pt_ref[0]
    g = mod_ref[0][:, 5 * D_MODEL:6 * D_MODEL]

    def dense():
        if cap % LANES == 0:
            slot = lax.broadcasted_iota(I32, (tn, cap), 1)
            onehot = jnp.concatenate(
                [jnp.where(pos_t[:, e:e + 1] == slot, 1.0, 0.0).astype(BF16) for e in range(N_EXPERTS)], axis=1)
        else:
            slot = lax.broadcasted_iota(I32, (tn, N_EXPERTS * cap), 1)
            acc = jnp.zeros((tn, N_EXPERTS * cap), F32)
            for e in range(N_EXPERTS):
                pe = pos_t[:, e:e + 1]
                acc = jnp.where(jnp.where(pe >= 0, pe + e * cap, -1) == slot, 1.0, acc)
            onehot = acc.astype(BF16)
        y = _dot(onehot, o_ref[0].reshape(N_EXPERTS * cap, D_MODEL))
        out_ref[0] = x_ref[0] + g * y

    if cap <= SLOT_WINDOW:
        dense()
        return

    t = pl.program_id(1)
    base = (pl.program_id(0) * nt + t) * N_EXPERTS
    nxt = jnp.minimum(t + 1, nt - 1)
    nbase = (pl.program_id(0) * nt + nxt) * N_EXPERTS
    win, fits = [], True
    for e in range(N_EXPERTS):
        start = st_ref[base + e]
        end = jnp.where(t + 1 < nt, st_ref[nbase + e], cap)
        a = jnp.minimum((start >> 4) << 4, cap - SLOT_WINDOW)
        win.append(a)
        fits = jnp.logical_and(fits, end - a <= SLOT_WINDOW)

    @pl.when(fits)
    def _():
        slot = lax.broadcasted_iota(I32, (tn, SLOT_WINDOW), 1)
        onehot = jnp.concatenate(
            [jnp.where(pos_t[:, e:e + 1] - win[e] == slot, 1.0, 0.0).astype(BF16) for e in range(N_EXPERTS)], axis=1)
        rows = jnp.concatenate(
            [o_ref[0, e, pl.ds(pl.multiple_of(win[e], 16), SLOT_WINDOW), :] for e in range(N_EXPERTS)], axis=0)
        out_ref[0] = x_ref[0] + g * _dot(onehot, rows)

    pl.when(jnp.logical_not(fits))(dense)


def _scatter(x, mod, pos_t, starts, o, cap, tn):
    b, n, d = x.shape
    nt = n // tn
    xs = pl.BlockSpec((1, tn, d), lambda i, t, st: (i, t, 0))
    kern = functools.partial(_scatter_kernel, tn=tn, cap=cap, nt=nt)
    return pl.pallas_call(
        kern,
        out_shape=jax.ShapeDtypeStruct((b, n, d), F32),
        grid_spec=pltpu.PrefetchScalarGridSpec(
            num_scalar_prefetch=1,
            grid=(b, nt),
            in_specs=[xs,
                      pl.BlockSpec((1, 1, 6 * d), lambda i, t, st: (i, 0, 0)),
                      pl.BlockSpec((1, tn, LANES), lambda i, t, st: (i, t, 0)),
                      pl.BlockSpec((1, N_EXPERTS, cap, d), lambda i, t, st: (i, 0, 0, 0))],
            out_specs=xs),
        compiler_params=_params("arbitrary", "arbitrary"),
        name="scatter_add",
    )(starts, x, mod, pos_t, o)


@functools.lru_cache(maxsize=None)
def _np_consts():
    lane = np.arange(GROUP_W)
    seg32 = (lane[:, None] // DIFF_D == lane[None, :] // DIFF_D).astype(np.float32)
    seg64 = (lane[:, None] // HEAD_DIM == lane[None, :] // HEAD_DIM).astype(np.float32)
    ang = 2.0 * np.pi * ((lane[:, None] % HEAD_DIM) * (lane[None, :] % HEAD_DIM) % HEAD_DIM) / HEAD_DIM
    cc = np.cos(ang) * seg64
    ss = np.sin(ang) * seg64
    tri = (lane[:, None] <= lane[None, :]).astype(np.float32)
    return dict(seg32=seg32, seg64=seg64, cc=cc, ss=ss, tri=tri)


@functools.lru_cache(maxsize=None)
def _np_dft(n):
    idx = (np.arange(n, dtype=np.int64)[:, None] * np.arange(n, dtype=np.int64)[None, :]) % n
    ang = 2.0 * np.pi * idx.astype(np.float64) / n
    return np.cos(ang).astype(np.float32), np.sin(ang).astype(np.float32)


@functools.lru_cache(maxsize=None)
def _np_rope(n):
    half = DIFF_D // 2
    inv = 1.0 / (ROPE_BASE ** (np.arange(0, half, 2, dtype=np.float32) / half))
    t = np.arange(n)
    row = (t // GRID_W).astype(np.float32)[:, None] * inv
    col = (t % GRID_W).astype(np.float32)[:, None] * inv
    nf = inv.shape[0]
    d = np.arange(GROUP_W) % DIFF_D
    f = d % nf
    is_col = d >= half
    second = (d % half) >= nf
    ang = np.where(is_col[None, :], col[:, f], row[:, f])
    c = np.cos(ang).astype(np.float32)
    s = np.sin(ang).astype(np.float32)
    s = np.where(second[None, :], s, -s)
    return c, s


@functools.lru_cache(maxsize=None)
def _np_na_index(n_rows):
    n_steps = n_rows // NA_QROWS
    reps = [0, 1, 2, n_steps - 2, n_steps - 1]
    tq = NA_QROWS * GRID_W
    roff = np.zeros((len(reps), NA_QROWS, NA_WIN_ROWS), np.int32)
    valid = np.zeros((len(reps), tq, NA_WIN), bool)
    for ci, t in enumerate(reps):
        ws = int(np.clip(NA_QROWS * t - NA_ROWS // 2, 0, n_rows - NA_WIN_ROWS))
        qi = np.arange(tq)
        r = NA_QROWS * t + qi // GRID_W
        qcol = qi % GRID_W
        kk = np.arange(NA_WIN)
        krow = ws + kk // GRID_W
        kcol = kk % GRID_W
        rstart = np.clip(r - NA_ROWS // 2, 0, n_rows - NA_ROWS)
        wstart = np.clip(qcol - NA_COLS // 2, 0, GRID_W - NA_COLS)
        vr = (krow[None, :] >= rstart[:, None]) & (krow[None, :] < rstart[:, None] + NA_ROWS)
        vc = (kcol[None, :] >= wstart[:, None]) & (kcol[None, :] < wstart[:, None] + NA_COLS)
        valid[ci] = vr & vc
        rows_q = NA_QROWS * t + np.arange(NA_QROWS)
        rows_k = ws + np.arange(NA_WIN_ROWS)
        roff[ci] = np.clip(rows_k[None, :] - rows_q[:, None] + NA_ROWS - 1, 0, 2 * NA_ROWS - 2)
    return roff, valid


def _na_table_kernel(r_ref, mask_ref, o_ref, toep_ref, *, roff):
    n_off = 2 * NA_ROWS - 1
    for h in range(HEADS):
        for ro in range(n_off):
            row = jnp.broadcast_to(r_ref[0, h, ro:ro + 1, :], (GRID_W, LANES))
            toep_ref[h, ro] = pltpu.roll(row, 0, 1, stride=1, stride_axis=0)
    lane = lax.broadcasted_iota(I32, (GRID_W, LANES), 1)
    n_cls = roff.shape[0]
    for ci in range(n_cls):
        for h in range(HEADS):
            for rr in range(NA_QROWS):
                r0 = h * NA_QROWS * GRID_W + rr * GRID_W
                for kp in range(NA_WIN_ROWS // 2):
                    left = toep_ref[h, int(roff[ci, rr, 2 * kp])]
                    right = pltpu.roll(toep_ref[h, int(roff[ci, rr, 2 * kp + 1])], GRID_W, 1)
                    bias = jnp.where(lane < GRID_W, left, right) * LOG2E
                    o_ref[0, ci, r0:r0 + GRID_W, kp * LANES:(kp + 1) * LANES] = (
                        bias + mask_ref[ci, rr * GRID_W:(rr + 1) * GRID_W, kp * LANES:(kp + 1) * LANES])


def _na_tables(rpb_all, n_rows):
    roff, valid = _np_na_index(n_rows)
    n_cls = roff.shape[0]
    depth = rpb_all.shape[0]
    n_off = 2 * NA_ROWS - 1
    r = jnp.concatenate([rpb_all[..., NA_COLS - 1:], jnp.zeros(rpb_all.shape[:-1] + (LANES - 2 * NA_COLS + 1,), F32),
                         rpb_all[..., :NA_COLS - 1]], axis=-1).astype(F32)
    mask = jnp.asarray(np.where(valid, 0.0, NEG_INF).astype(np.float32))
    tq = NA_QROWS * GRID_W
    kern = functools.partial(_na_table_kernel, roff=roff)
    return pl.pallas_call(
        kern,
        out_shape=jax.ShapeDtypeStruct((depth, n_cls, HEADS * tq, NA_WIN), F32),
        grid=(depth,),
        in_specs=[pl.BlockSpec((1, HEADS, n_off, LANES), lambda l: (l, 0, 0, 0)),
                  _full((n_cls, tq, NA_WIN))],
        out_specs=pl.BlockSpec((1, n_cls, HEADS * tq, NA_WIN), lambda l: (l, 0, 0, 0)),
        scratch_shapes=[pltpu.VMEM((HEADS, n_off, GRID_W, LANES), F32)],
        compiler_params=_params("arbitrary"),
        name="na_bias_table",
    )(r, mask)


def _moe(x, xn, logits, mod, tri, wg, wu, wd, layer, bb, tn):
    n = x.shape[1]
    cap = CAPACITY_FACTOR * n // N_EXPERTS
    pos, gate, pos_t, starts = _route(logits, tri, cap, tn)
    starts = starts[:, :, :n // tn].transpose(0, 2, 1).reshape(-1)
    o = _expert_ffn(xn, pos, gate, starts, wg, wu, wd, layer, cap, bb, tn)
    return _scatter(x, mod, pos_t, starts, o, cap, tn)


def kernel(x, c, ctx, c_ctx, ada_w, ada_b, norm1_g, norm2_g, w_in, w_out, head_out_g, sgu_w, sgu_b, diff_qn_g, diff_kn_g, diff_lambda, na_qn_g, na_kn_g, na_rpb, router_w, exp_w_gate, exp_w_up, exp_w_down):
    b, n, d = x.shape
    n_ctx = ctx.shape[1]
    npc = _np_consts()
    consts = {k: jnp.asarray(v, F32).astype(BF16) for k, v in npc.items()}
    seg64, tri = consts["seg64"], consts["tri"]
    cn, sn = (jnp.asarray(a, F32).astype(BF16) for a in _np_dft(n))
    cn_c, sn_c = (jnp.asarray(a, F32).astype(BF16) for a in _np_dft(n_ctx))
    rope = tuple(jnp.asarray(a, F32) for a in _np_rope(n))

    pad = (-(b + 1)) % 8
    c_rows = jnp.concatenate([c, c_ctx[None, :], jnp.zeros((pad, d), F32)], axis=0)
    mod_all = _modulation(c_rows, ada_w, ada_b)
    na_tables = _na_tables(na_rpb, n // GRID_W)

    xc = ctx
    for l in range(DEPTH):
        last = l == DEPTH - 1
        lam_init = 0.8 - 0.6 * math.exp(-0.3 * l)
        mod = mod_all[l, :b][:, None, :]
        mod_c = jnp.broadcast_to(mod_all[l, b][None, None, :], (b, 1, 6 * d))
        g1 = norm1_g[l][None, :]
        g2 = norm2_g[l][None, :]
        w_in_l = w_in[l].astype(BF16)
        w_out_l = w_out[l].astype(BF16)
        sguw = sgu_w[l].astype(BF16)
        sgub = jnp.repeat(sgu_b[l].T, HEAD_DIM, axis=1)
        hg = head_out_g[l].reshape(4, GROUP_W)
        vec = jnp.stack([jnp.tile(diff_qn_g[l], GROUP_W // DIFF_D), jnp.tile(diff_kn_g[l], GROUP_W // DIFF_D),
                         jnp.tile(na_qn_g[l], HEADS), jnp.tile(na_kn_g[l], HEADS),
                         hg[0], hg[1], hg[2], hg[3],
                         jnp.broadcast_to(jnp.max(jnp.abs(na_rpb[l])) * LOG2E, (GROUP_W,))], axis=0).astype(F32)
        vec = jnp.pad(vec, ((0, VEC_ROWS - vec.shape[0]), (0, 0)))
        lam = diff_lambda[l].astype(F32)
        rwt = router_w[l].T.astype(BF16)
        experts = (exp_w_gate, exp_w_up, exp_w_down)

        ya, zc, zs, qc, kct, vc, qd, kdt, vd = _in_proj(x, mod, g1, w_in_l, consts, sguw, sgub, vec, rope, 512)
        ya_c, zc_c, zs_c, qc_c, kct_c, vc_c, qd_c, kdt_c, vd_c = _in_proj(
            xc, mod_c, g1, w_in_l, consts, sguw, sgub, vec, None, n_ctx)

        yb = _fourier(zc, zs, cn, sn, seg64, vec, 1024)
        yc = _attention(qc, [(kct, vc), (kct_c, vc_c)], lam, lam_init, seg64, vec, 6, 1.0 - lam_init, 512, 512)
        yd = _na_attention(qd, kdt, vd, kdt_c, vd_c, na_tables, l, seg64, vec, 4)
        x, xn, logits = _out_proj(ya, yb, yc, yd, w_out_l, x, mod, g2, rwt, 512)
        x = _moe(x, xn, logits, mod, tri, *experts, l, 1, 512)

        if not last:
            yb_c = _fourier(zc_c, zs_c, cn_c, sn_c, seg64, vec, n_ctx)
            yc_c = _attention(qc_c, [(kct_c, vc_c)], lam, lam_init, seg64, vec, 6, 1.0 - lam_init, n_ctx)
            yd_c = _attention(qd_c, [(kdt_c, vd_c)], None, 0.0, seg64, vec, 7, 1.0, n_ctx)
            xc, xn_c, logits_c = _out_proj(ya_c, yb_c, yc_c, yd_c, w_out_l, xc, mod_c, g2, rwt, n_ctx)
            xc = _moe(xc, xn_c, logits_c, mod_c, tri, *experts, l, b, n_ctx)
    return x
```

```python
import functools
import math

import numpy as np
import jax
import jax.numpy as jnp
from jax import lax
from jax.experimental import pallas as pl
from jax.experimental.pallas import tpu as pltpu

F32 = jnp.float32
BF16 = jnp.bfloat16
I32 = jnp.int32

D_MODEL = 1024
DEPTH = 2
GRID_W = 64
LANES = 128
HEAD_DIM = 64
LOG2_HEAD_DIM = 6
GROUP_W = 256
HEADS = GROUP_W // HEAD_DIM
CHUNK = 128
DIFF_D = HEAD_DIM // 2
NA_ROWS = 8
NA_COLS = 16
N_EXPERTS = 16
CAPACITY_FACTOR = 2
ROPE_BASE = 10000.0
EPS = 1e-6
IN_W = 9 * GROUP_W
LOG2E = 1.4426950408889634

VMEM_LIMIT_BYTES = 56 * 1024 * 1024
NA_QROWS = 2
NA_WIN_ROWS = NA_ROWS + 2
NA_WIN = NA_WIN_ROWS * GRID_W
NA_HEAD_STACK = 1
NEG_INF = float("-inf")
VEC_ROWS = 16
SLOT_WINDOW = 128
SAFE_EXP2_BOUND = 48.0
BOUND_SLACK = 1.02


def _dot(a, b):
    return jnp.dot(a, b, preferred_element_type=F32)


def _params(*sem):
    return pltpu.CompilerParams(dimension_semantics=sem, vmem_limit_bytes=VMEM_LIMIT_BYTES)


def _full(shape):
    nd = len(shape)
    return pl.BlockSpec(shape, lambda *_: (0,) * nd)


def _seg_rms(x, seg, width):
    ss = _dot((x * x).astype(BF16), seg)
    return x * lax.rsqrt(ss * (1.0 / width) + EPS)


def _mod_kernel(c_ref, w_ref, b_ref, o_ref):
    s = jax.nn.silu(c_ref[...]).astype(BF16)
    o_ref[0] = _dot(s, w_ref[0].astype(BF16)) + b_ref[0]


def _modulation(c_rows, ada_w, ada_b):
    depth, d, w6 = ada_w.shape
    r = c_rows.shape[0]
    tn = 1024
    return pl.pallas_call(
        _mod_kernel,
        out_shape=jax.ShapeDtypeStruct((depth, r, w6), F32),
        grid=(depth, w6 // tn),
        in_specs=[
            pl.BlockSpec((r, d), lambda l, j: (0, 0)),
            pl.BlockSpec((1, d, tn), lambda l, j: (l, 0, j)),
            pl.BlockSpec((1, 1, tn), lambda l, j: (l, 0, j)),
        ],
        out_specs=pl.BlockSpec((1, r, tn), lambda l, j: (l, 0, j)),
        compiler_params=_params("arbitrary", "arbitrary"),
        name="modulation",
    )(c_rows, ada_w, ada_b.reshape(depth, 1, w6))


def _rope(x, c, s, lane):
    fwd = pltpu.roll(x, GROUP_W - 8, 1)
    bwd = pltpu.roll(x, 8, 1)
    partner = jnp.where((lane & 8) == 0, fwd, bwd)
    return x * c + partner * s


def _in_kernel(*refs, tm, use_rope, c_scale, d_scale):
    (x_ref, mod_ref, g1_ref, w_ref, seg32_ref, seg64_ref, cc_ref, ss_ref, sguw_ref, sgub_ref, vec_ref) = refs[:11]
    rest = refs[11:]
    if use_rope:
        ropec_ref, ropes_ref = rest[:2]
        rest = rest[2:]
    ya_ref, zc_ref, zs_ref, qc_ref, kct_ref, vc_ref, qd_ref, kdt_ref, vd_ref = rest

    x = x_ref[0]
    mod = mod_ref[0]
    sh = mod[:, 0:D_MODEL]
    sc = mod[:, D_MODEL:2 * D_MODEL]
    ms = jnp.mean(x * x, axis=-1, keepdims=True)
    h = x * lax.rsqrt(ms + EPS) * g1_ref[...]
    h = (h * (1.0 + sc) + sh).astype(BF16)
    lane = lax.broadcasted_iota(I32, (1, GROUP_W), 1)
    head = lane >> LOG2_HEAD_DIM
    seg32 = seg32_ref[...]
    seg64 = seg64_ref[...]
    vec = vec_ref[...]

    z = jax.nn.gelu(_dot(h, w_ref[:, 0:2 * GROUP_W]))
    u = z[:, 0:GROUP_W]
    vn = _seg_rms(z[:, GROUP_W:2 * GROUP_W], seg64, HEAD_DIM).astype(BF16)
    rows = []
    for c in range(tm // CHUNK):
        vch = vn[c * CHUNK:(c + 1) * CHUNK]
        sv = jnp.zeros((CHUNK, GROUP_W), F32)
        for hh in range(HEADS):
            sv = jnp.where(head == hh, _dot(sguw_ref[hh], vch), sv)
        rows.append(sv + sgub_ref[...])
    ya = u * jnp.concatenate(rows, axis=0)
    ya_ref[0] = (_seg_rms(ya, seg64, HEAD_DIM) * vec[4:5]).astype(BF16)

    zb = _dot(h, w_ref[:, 2 * GROUP_W:3 * GROUP_W]).astype(BF16)
    zc_ref[0] = _dot(zb, cc_ref[...]).astype(BF16)
    zs_ref[0] = _dot(zb, ss_ref[...]).astype(BF16)

    pc = _dot(h, w_ref[:, 3 * GROUP_W:6 * GROUP_W])
    q = _seg_rms(pc[:, 0:GROUP_W], seg32, DIFF_D) * vec[0:1]
    k = _seg_rms(pc[:, GROUP_W:2 * GROUP_W], seg32, DIFF_D) * vec[1:2]
    if use_rope:
        rc = ropec_ref[...]
        rs = ropes_ref[...]
        q = _rope(q, rc, rs, lane)
        k = _rope(k, rc, rs, lane)
    qc_ref[0] = (q * c_scale).astype(BF16)
    kct_ref[0] = k.T.astype(BF16)
    vc_ref[0] = pc[:, 2 * GROUP_W:3 * GROUP_W].astype(BF16)

    pd = _dot(h, w_ref[:, 6 * GROUP_W:9 * GROUP_W])
    qd = _seg_rms(pd[:, 0:GROUP_W], seg64, HEAD_DIM) * vec[2:3]
    kd = _seg_rms(pd[:, GROUP_W:2 * GROUP_W], seg64, HEAD_DIM) * vec[3:4]
    qd_ref[0] = (qd * d_scale).astype(BF16)
    kdt_ref[0] = kd.T.astype(BF16)
    vd_ref[0] = pd[:, 2 * GROUP_W:3 * GROUP_W].astype(BF16)


def _in_proj(x, mod, g1, w_in, consts, sguw, sgub, vec, rope, tm):
    b, n, d = x.shape
    use_rope = rope is not None
    tok = pl.BlockSpec((1, tm, GROUP_W), lambda i, t: (i, t, 0))
    tok_t = pl.BlockSpec((1, GROUP_W, tm), lambda i, t: (i, 0, t))
    in_specs = [
        pl.BlockSpec((1, tm, d), lambda i, t: (i, t, 0)),
        pl.BlockSpec((1, 1, 6 * d), lambda i, t: (i, 0, 0)),
        _full((1, d)),
        _full((d, IN_W)),
        _full((GROUP_W, GROUP_W)), _full((GROUP_W, GROUP_W)), _full((GROUP_W, GROUP_W)), _full((GROUP_W, GROUP_W)),
        _full((HEADS, CHUNK, CHUNK)),
        _full((CHUNK, GROUP_W)),
        _full((VEC_ROWS, GROUP_W)),
    ]
    args = [x, mod, g1, w_in, consts["seg32"], consts["seg64"], consts["cc"], consts["ss"], sguw, sgub, vec]
    if use_rope:
        in_specs += [pl.BlockSpec((tm, GROUP_W), lambda i, t: (t, 0))] * 2
        args += list(rope)
    sd = jax.ShapeDtypeStruct((b, n, GROUP_W), BF16)
    sdt = jax.ShapeDtypeStruct((b, GROUP_W, n), BF16)
    kern = functools.partial(
        _in_kernel, tm=tm, use_rope=use_rope,
        c_scale=(DIFF_D ** -0.5) * LOG2E, d_scale=(HEAD_DIM ** -0.5) * LOG2E)
    return pl.pallas_call(
        kern,
        out_shape=(sd, sd, sd, sd, sdt, sd, sd, sdt, sd),
        grid=(b, n // tm),
        in_specs=in_specs,
        out_specs=(tok, tok, tok, tok, tok_t, tok, tok, tok_t, tok),
        compiler_params=_params("arbitrary", "arbitrary"),
        name="in_proj",
    )(*args)


def _fourier_kernel(cn_ref, sn_ref, zc_ref, zs_ref, seg64_ref, vec_ref, o_ref, *, norm):
    y = (_dot(cn_ref[...], zc_ref[0]) - _dot(sn_ref[...], zs_ref[0])) * norm
    o_ref[0] = (_seg_rms(y, seg64_ref[...], HEAD_DIM) * vec_ref[5:6]).astype(BF16)


def _fourier(zc, zs, cn, sn, seg64, vec, tn):
    b, n, _ = zc.shape
    kern = functools.partial(_fourier_kernel, norm=1.0 / math.sqrt(n * HEAD_DIM))
    return pl.pallas_call(
        kern,
        out_shape=jax.ShapeDtypeStruct((b, n, GROUP_W), BF16),
        grid=(n // tn, b),
        in_specs=[
            pl.BlockSpec((tn, n), lambda t, i: (t, 0)),
            pl.BlockSpec((tn, n), lambda t, i: (t, 0)),
            pl.BlockSpec((1, n, GROUP_W), lambda t, i: (i, 0, 0)),
            pl.BlockSpec((1, n, GROUP_W), lambda t, i: (i, 0, 0)),
            _full((GROUP_W, GROUP_W)),
            _full((VEC_ROWS, GROUP_W)),
        ],
        out_specs=pl.BlockSpec((1, tn, GROUP_W), lambda t, i: (i, t, 0)),
        compiler_params=_params("arbitrary", "arbitrary"),
        name="fourier",
    )(cn, sn, zc, zs, seg64, vec)


def _attn_kernel(*refs, n_src, diff, lam_init, chunk, tq, vec_row, out_scale):
    q_ref = refs[0]
    rest = refs[1 + 2 * n_src:]
    if diff:
        lam_ref = rest[0]
        rest = rest[1:]
    seg64_ref, vec_ref, o_ref = rest

    q = q_ref[0]
    lane = lax.broadcasted_iota(I32, (1, GROUP_W), 1)
    if diff:
        lf = lam_ref[...]
        lam = (jnp.exp(jnp.sum(lf[0:1] * lf[1:2], axis=-1, keepdims=True))
               - jnp.exp(jnp.sum(lf[2:3] * lf[3:4], axis=-1, keepdims=True)) + lam_init)

    chunks = []
    for i in range(n_src):
        kt_ref, v_ref = refs[1 + 2 * i], refs[2 + 2 * i]
        nk = kt_ref.shape[2]
        chunks += [(kt_ref, v_ref, c0, min(chunk, nk - c0)) for c0 in range(0, nk, chunk)]

    width = DIFF_D if diff else HEAD_DIM
    gq = vec_ref[0:1] if diff else vec_ref[2:3]
    gk = vec_ref[1:2] if diff else vec_ref[3:4]
    bound = jnp.max(jnp.abs(gq)) * jnp.max(jnp.abs(gk)) * (math.sqrt(width) * LOG2E * BOUND_SLACK)

    def attend_bounded(sel):
        qm = jnp.where(sel, q, jnp.zeros_like(q))
        l = jnp.zeros((tq, 1), F32)
        acc = jnp.zeros((tq, GROUP_W), F32)
        for kt_ref, v_ref, c0, ck in chunks:
            p = jnp.exp2(_dot(qm, kt_ref[0, :, c0:c0 + ck]) - bound)
            l = l + jnp.sum(p, axis=-1, keepdims=True)
            acc = acc + _dot(p.astype(BF16), v_ref[0, c0:c0 + ck, :])
        return acc * (1.0 / l)

    def attend_online(sel):
        qm = jnp.where(sel, q, jnp.zeros_like(q))
        m = jnp.full((tq, 1), NEG_INF, F32)
        l = jnp.zeros((tq, 1), F32)
        acc = jnp.zeros((tq, GROUP_W), F32)
        for kt_ref, v_ref, c0, ck in chunks:
            s = _dot(qm, kt_ref[0, :, c0:c0 + ck])
            m_new = jnp.maximum(m, jnp.max(s, axis=-1, keepdims=True))
            alpha = jnp.exp2(m - m_new)
            p = jnp.exp2(s - m_new)
            l = alpha * l + jnp.sum(p, axis=-1, keepdims=True)
            acc = alpha * acc + _dot(p.astype(BF16), v_ref[0, c0:c0 + ck, :])
            m = m_new
        return acc * (1.0 / l)

    def head_out(attend, h):
        if diff:
            return (attend((lane >> (LOG2_HEAD_DIM - 1)) == 2 * h)
                    - lam * attend((lane >> (LOG2_HEAD_DIM - 1)) == 2 * h + 1))
        return attend((lane >> LOG2_HEAD_DIM) == h)

    def finish(out):
        y = _seg_rms(out, seg64_ref[...], HEAD_DIM) * vec_ref[vec_row:vec_row + 1]
        o_ref[0] = (y * out_scale).astype(BF16)

    def run_bounded():
        out = jnp.zeros((tq, GROUP_W), F32)
        for h in range(HEADS):
            out = jnp.where((lane >> LOG2_HEAD_DIM) == h, head_out(attend_bounded, h), out)
        finish(out)

    def run_online():
        def body(h, out):
            return jnp.where((lane >> LOG2_HEAD_DIM) == h, head_out(attend_online, h), out)

        finish(lax.fori_loop(0, HEADS, body, jnp.zeros((tq, GROUP_W), F32)))

    small = bound <= SAFE_EXP2_BOUND
    pl.when(small)(run_bounded)
    pl.when(jnp.logical_not(small))(run_online)


def _attention(q, srcs, lam, lam_init, seg64, vec, vec_row, out_scale, tq, chunk=768):
    b, nq, _ = q.shape
    diff = lam is not None
    in_specs = [pl.BlockSpec((1, tq, GROUP_W), lambda i, t: (i, t, 0))]
    args = [q]
    for kt, v in srcs:
        nk = v.shape[1]
        in_specs += [pl.BlockSpec((1, GROUP_W, nk), lambda i, t: (i, 0, 0)),
                     pl.BlockSpec((1, nk, GROUP_W), lambda i, t: (i, 0, 0))]
        args += [kt, v]
    if diff:
        in_specs.append(_full((4, DIFF_D)))
        args.append(lam)
    in_specs += [_full((GROUP_W, GROUP_W)), _full((VEC_ROWS, GROUP_W))]
    args += [seg64, vec]
    kern = functools.partial(_attn_kernel, n_src=len(srcs), diff=diff, lam_init=lam_init, chunk=chunk, tq=tq,
                             vec_row=vec_row, out_scale=out_scale)
    return pl.pallas_call(
        kern,
        out_shape=jax.ShapeDtypeStruct((b, nq, GROUP_W), BF16),
        grid=(b, nq // tq),
        in_specs=in_specs,
        out_specs=pl.BlockSpec((1, tq, GROUP_W), lambda i, t: (i, t, 0)),
        compiler_params=_params("arbitrary", "arbitrary"),
        name="diff_attention" if diff else "ctx_attention",
    )(*args)


def _na_kernel(q_ref, kt_ref, v_ref, ktc_ref, vc_ref, tab_ref, seg64_ref, vec_ref, o_ref, *, n_rows, pairs):
    n_steps = n_rows // NA_QROWS
    tq = NA_QROWS * GRID_W
    lane = lax.broadcasted_iota(I32, (1, GROUP_W), 1)
    head = lane >> LOG2_HEAD_DIM
    ktc = ktc_ref[0]
    vc = vc_ref[0]
    bound = (jnp.max(jnp.abs(vec_ref[2:3])) * jnp.max(jnp.abs(vec_ref[3:4])) * (math.sqrt(HEAD_DIM) * LOG2E * BOUND_SLACK)
             + jnp.max(vec_ref[8:9]))

    def run(bounded):
        for pi in range(pairs):
            t = pl.program_id(1) * pairs + pi
            ws = jnp.clip(NA_QROWS * t - NA_ROWS // 2, 0, n_rows - NA_WIN_ROWS)
            k0 = pl.multiple_of(ws * GRID_W, 128)
            tid = jnp.where(t < 2, t, jnp.where(t < n_steps - 2, 2, t - (n_steps - 5)))
            q = q_ref[0, pi * tq:(pi + 1) * tq, :]
            out = jnp.zeros((tq, GROUP_W), F32)
            for h0 in range(0, HEADS, NA_HEAD_STACK):
                hs = range(h0, h0 + NA_HEAD_STACK)
                qs = jnp.concatenate([jnp.where(head == hh, q, jnp.zeros_like(q)) for hh in hs], axis=0)
                s_loc = (_dot(qs, kt_ref[0, :, pl.ds(k0, NA_WIN)])
                         + tab_ref[0, tid, h0 * tq:(h0 + NA_HEAD_STACK) * tq, :])
                s_ctx = _dot(qs, ktc)
                if bounded:
                    m = bound
                else:
                    m = jnp.maximum(jnp.max(s_loc, axis=-1, keepdims=True), jnp.max(s_ctx, axis=-1, keepdims=True))
                p_loc = jnp.exp2(s_loc - m)
                p_ctx = jnp.exp2(s_ctx - m)
                l = jnp.sum(p_loc, axis=-1, keepdims=True) + jnp.sum(p_ctx, axis=-1, keepdims=True)
                o = _dot(p_loc.astype(BF16), v_ref[0, pl.ds(k0, NA_WIN), :]) + _dot(p_ctx.astype(BF16), vc)
                o = o * (1.0 / l)
                for i, hh in enumerate(hs):
                    out = jnp.where(head == hh, o[i * tq:(i + 1) * tq], out)
            o_ref[0, pi * tq:(pi + 1) * tq, :] = (
                _seg_rms(out, seg64_ref[...], HEAD_DIM) * vec_ref[7:8]).astype(BF16)

    small = bound <= SAFE_EXP2_BOUND
    pl.when(small)(lambda: run(True))
    pl.when(jnp.logical_not(small))(lambda: run(False))


def _na_attention(q, kt, v, ktc, vc, tables, layer, seg64, vec, pairs):
    b, n, _ = q.shape
    nc = vc.shape[1]
    tq = pairs * NA_QROWS * GRID_W
    kern = functools.partial(_na_kernel, n_rows=n // GRID_W, pairs=pairs)
    return pl.pallas_call(
        kern,
        out_shape=jax.ShapeDtypeStruct((b, n, GROUP_W), BF16),
        grid=(b, n // tq),
        in_specs=[
            pl.BlockSpec((1, tq, GROUP_W), lambda i, t: (i, t, 0)),
            pl.BlockSpec((1, GROUP_W, n), lambda i, t: (i, 0, 0)),
            pl.BlockSpec((1, n, GROUP_W), lambda i, t: (i, 0, 0)),
            pl.BlockSpec((1, GROUP_W, nc), lambda i, t: (i, 0, 0)),
            pl.BlockSpec((1, nc, GROUP_W), lambda i, t: (i, 0, 0)),
            pl.BlockSpec((1,) + tables.shape[1:], lambda i, t: (layer, 0, 0, 0)),
            _full((GROUP_W, GROUP_W)),
            _full((VEC_ROWS, GROUP_W)),
        ],
        out_specs=pl.BlockSpec((1, tq, GROUP_W), lambda i, t: (i, t, 0)),
        compiler_params=_params("arbitrary", "arbitrary"),
        name="neighbourhood_attention",
    )(q, kt, v, ktc, vc, tables, seg64, vec)


def _out_kernel(ya_ref, yb_ref, yc_ref, yd_ref, w_ref, x_ref, mod_ref, g2_ref, rwt_ref, o_ref, xn_ref, lg_ref):
    y = jnp.concatenate([ya_ref[0], yb_ref[0], yc_ref[0], yd_ref[0]], axis=-1)
    mod = mod_ref[0]
    x = x_ref[0] + mod[:, 2 * D_MODEL:3 * D_MODEL] * _dot(y, w_ref[...])
    o_ref[0] = x
    sh = mod[:, 3 * D_MODEL:4 * D_MODEL]
    sc = mod[:, 4 * D_MODEL:5 * D_MODEL]
    ms = jnp.mean(x * x, axis=-1, keepdims=True)
    h = x * lax.rsqrt(ms + EPS) * g2_ref[...]
    h = (h * (1.0 + sc) + sh).astype(BF16)
    xn_ref[0] = h
    lg_ref[0] = lax.dot_general(rwt_ref[...], h, (((1,), (1,)), ((), ())), preferred_element_type=F32)


def _out_proj(ya, yb, yc, yd, w_out, x, mod, g2, rwt, tm):
    b, n, d = x.shape
    tok = pl.BlockSpec((1, tm, GROUP_W), lambda i, t: (i, t, 0))
    xs = pl.BlockSpec((1, tm, d), lambda i, t: (i, t, 0))
    return pl.pallas_call(
        _out_kernel,
        out_shape=(jax.ShapeDtypeStruct((b, n, d), F32),
                   jax.ShapeDtypeStruct((b, n, d), BF16),
                   jax.ShapeDtypeStruct((b, N_EXPERTS, n), F32)),
        grid=(b, n // tm),
        in_specs=[tok, tok, tok, tok, _full((4 * GROUP_W, d)), xs,
                  pl.BlockSpec((1, 1, 6 * d), lambda i, t: (i, 0, 0)),
                  _full((1, d)), _full((N_EXPERTS, d))],
        out_specs=(xs, xs, pl.BlockSpec((1, N_EXPERTS, tm), lambda i, t: (i, 0, t))),
        compiler_params=_params("arbitrary", "arbitrary"),
        name="out_proj",
    )(ya, yb, yc, yd, w_out, x, mod, g2, rwt)


def _cumsum_excl(m, tri):
    n = m.shape[1]
    carry = jnp.zeros((m.shape[0], 1), F32)
    outs = []
    for j in range(n // GROUP_W):
        blk = m[:, j * GROUP_W:(j + 1) * GROUP_W]
        inc = _dot(blk.astype(BF16), tri)
        outs.append(inc - blk + carry)
        carry = carry + inc[:, GROUP_W - 1:GROUP_W]
    return jnp.concatenate(outs, axis=1)


def _route_kernel(lg_ref, tri_ref, pos_ref, gate_ref, post_ref, st_ref, *, n, cap, tile):
    lg = lg_ref[0]
    e = jnp.exp(lg - jnp.max(lg, axis=0, keepdims=True))
    aff = e / jnp.sum(e, axis=0, keepdims=True)
    gate_ref[0] = aff

    def unresolved(state):
        lo, hi = state
        return jnp.max(jnp.where(lo < hi, 1.0, 0.0)) > 0.0

    def bisect(state):
        lo, hi = state
        mid = 0.5 * (lo + hi)
        mid = jnp.where(mid > lo, mid, hi)
        ge = aff >= mid
        cnt = jnp.sum(jnp.where(ge, 1.0, 0.0), axis=1, keepdims=True)
        least_ge = jnp.min(jnp.where(ge, aff, jnp.inf), axis=1, keepdims=True)
        most_lt = jnp.max(jnp.where(ge, NEG_INF, aff), axis=1, keepdims=True)
        up = cnt >= cap
        return jnp.where(up, least_ge, lo), jnp.where(up, hi, most_lt)

    thr, _ = lax.while_loop(unresolved, bisect, (jnp.min(aff, axis=1, keepdims=True),
                                                 jnp.max(aff, axis=1, keepdims=True)))
    gt = aff > thr
    eq = aff == thr
    need = cap - jnp.sum(jnp.where(gt, 1.0, 0.0), axis=1, keepdims=True)
    tri = tri_ref[...]
    rank_eq = _cumsum_excl(jnp.where(eq, 1.0, 0.0), tri)
    sel = jnp.where(gt, 1.0, jnp.where(eq, jnp.where(rank_eq < need, 1.0, 0.0), 0.0))
    cum = _cumsum_excl(sel, tri)
    pos = jnp.where(sel > 0.0, cum, -1.0)
    pos_ref[0] = pos.astype(I32)
    tile_lane = lax.broadcasted_iota(I32, (N_EXPERTS, LANES), 1)
    starts = jnp.zeros((N_EXPERTS, LANES), F32)
    for t in range(n // tile):
        starts = jnp.where(tile_lane == t, cum[:, t * tile:t * tile + 1], starts)
    st_ref[0] = starts.astype(I32)
    padded = jnp.concatenate([pos, jnp.full((LANES - N_EXPERTS, n), -1.0, F32)], axis=0)
    post_ref[0] = padded.T.astype(I32)


def _route(logits, tri, cap, tile):
    b, _, n = logits.shape
    em = pl.BlockSpec((1, N_EXPERTS, n), lambda i: (i, 0, 0))
    kern = functools.partial(_route_kernel, n=n, cap=cap, tile=tile)
    return pl.pallas_call(
        kern,
        out_shape=(jax.ShapeDtypeStruct((b, N_EXPERTS, n), I32),
                   jax.ShapeDtypeStruct((b, N_EXPERTS, n), F32),
                   jax.ShapeDtypeStruct((b, n, LANES), I32),
                   jax.ShapeDtypeStruct((b, N_EXPERTS, LANES), I32)),
        grid=(b,),
        in_specs=[em, _full((GROUP_W, GROUP_W))],
        out_specs=(em, em, pl.BlockSpec((1, n, LANES), lambda i: (i, 0, 0)),
                   pl.BlockSpec((1, N_EXPERTS, LANES), lambda i: (i, 0, 0))),
        compiler_params=_params("arbitrary"),
        name="router",
    )(logits, tri)


def _ffn_kernel(st_ref, xn_ref, pos_ref, gate_ref, wg32_ref, wu32_ref, wd32_ref, o_ref, wg_ref, wu_ref, wd_ref, xg_ref,
                *, bb, cap, n, tile):
    @pl.when(pl.program_id(1) == 0)
    def _():
        wg_ref[...] = wg32_ref[0, 0].astype(BF16)
        wu_ref[...] = wu32_ref[0, 0].astype(BF16)
        wd_ref[...] = wd32_ref[0, 0].astype(BF16)

    expert = pl.ds(pl.program_id(0), 1)

    def step(gather):
        slot = lax.broadcasted_iota(I32, (cap, n), 0)
        hits = [pos_ref[i, expert, :] == slot for i in range(bb)]
        gs = [jnp.sum(jnp.where(hits[i], gate_ref[i, expert, :], 0.0), axis=1, keepdims=True) for i in range(bb)]
        g = gs[0] if bb == 1 else jnp.concatenate(gs, axis=0)
        gather(hits)
        xg = xg_ref[...].astype(BF16)
        hid = (jax.nn.silu(_dot(xg, wg_ref[...])) * _dot(xg, wu_ref[...])).astype(BF16)
        o = _dot(hid, wd_ref[...]) * g
        for i in range(bb):
            o_ref[i, 0] = o[i * cap:(i + 1) * cap].astype(BF16)

    def gather_dense(hits):
        for i in range(bb):
            xg_ref[i * cap:(i + 1) * cap, :] = _dot(jnp.where(hits[i], 1.0, 0.0).astype(BF16), xn_ref[i])

    if cap <= SLOT_WINDOW:
        step(gather_dense)
        return

    nt = n // tile
    win, fits = [], True
    for i in range(bb):
        base = (pl.program_id(1) * bb + i) * nt * N_EXPERTS + pl.program_id(0)
        for t in range(nt):
            start = st_ref[base + t * N_EXPERTS]
            end = st_ref[base + (t + 1) * N_EXPERTS] if t + 1 < nt else cap
            a = jnp.minimum((start >> 4) << 4, cap - SLOT_WINDOW)
            win.append(a)
            fits = jnp.logical_and(fits, end - a <= SLOT_WINDOW)

    def gather_windowed(hits):
        del hits
        xg_ref[...] = jnp.zeros_like(xg_ref)
        wslot = lax.broadcasted_iota(I32, (SLOT_WINDOW, tile), 0)
        for i in range(bb):
            for t in range(nt):
                a = win[i * nt + t]
                p = pos_ref[i, expert, t * tile:(t + 1) * tile]
                onehot = jnp.where(p - a == wslot, 1.0, 0.0).astype(BF16)
                rows = pl.ds(pl.multiple_of(i * cap + a, 16), SLOT_WINDOW)
                xg_ref[rows, :] = xg_ref[rows, :] + _dot(onehot, xn_ref[i, t * tile:(t + 1) * tile, :])

    pl.when(fits)(lambda: step(gather_windowed))
    pl.when(jnp.logical_not(fits))(lambda: step(gather_dense))


def _expert_ffn(xn, pos, gate, starts, wg, wu, wd, layer, cap, bb, tile):
    b, n, d = xn.shape
    wspec = pl.BlockSpec((1, 1, d, d), lambda e, j, st: (layer, e, 0, 0))
    sel = pl.BlockSpec((bb, N_EXPERTS, n), lambda e, j, st: (j, 0, 0))
    kern = functools.partial(_ffn_kernel, bb=bb, cap=cap, n=n, tile=tile)
    return pl.pallas_call(
        kern,
        out_shape=jax.ShapeDtypeStruct((b, N_EXPERTS, cap, d), BF16),
        grid_spec=pltpu.PrefetchScalarGridSpec(
            num_scalar_prefetch=1,
            grid=(N_EXPERTS, b // bb),
            in_specs=[pl.BlockSpec((bb, n, d), lambda e, j, st: (j, 0, 0)), sel, sel, wspec, wspec, wspec],
            out_specs=pl.BlockSpec((bb, 1, cap, d), lambda e, j, st: (j, e, 0, 0)),
            scratch_shapes=[pltpu.VMEM((d, d), BF16)] * 3 + [pltpu.VMEM((bb * cap, d), F32)]),
        compiler_params=_params("arbitrary", "arbitrary"),
        name="expert_ffn",
    )(starts, xn, pos, gate, wg, wu, wd)


def _scatter_kernel(st_ref, x_ref, mod_ref, pt_ref, o_ref, out_ref, *, tn, cap, nt):
    pos_t = pt_ref[0]
    g = mod_ref[0][:, 5 * D_MODEL:6 * D_MODEL]

    def dense():
        if cap % LANES == 0:
            slot = lax.broadcasted_iota(I32, (tn, cap), 1)
            onehot = jnp.concatenate(
                [jnp.where(pos_t[:, e:e + 1] == slot, 1.0, 0.0).astype(BF16) for e in range(N_EXPERTS)], axis=1)
        else:
            slot = lax.broadcasted_iota(I32, (tn, N_EXPERTS * cap), 1)
            acc = jnp.zeros((tn, N_EXPERTS * cap), F32)
            for e in range(N_EXPERTS):
                pe = pos_t[:, e:e + 1]
                acc = jnp.where(jnp.where(pe >= 0, pe + e * cap, -1) == slot, 1.0, acc)
            onehot = acc.astype(BF16)
        y = _dot(onehot, o_ref[0].reshape(N_EXPERTS * cap, D_MODEL))
        out_ref[0] = x_ref[0] + g * y

    if cap <= SLOT_WINDOW:
        dense()
        return

    t = pl.program_id(1)
    base = (pl.program_id(0) * nt + t) * N_EXPERTS
    nxt = jnp.minimum(t + 1, nt - 1)
    nbase = (pl.program_id(0) * nt + nxt) * N_EXPERTS
    win, fits = [], True
    for e in range(N_EXPERTS):
        start = st_ref[base + e]
        end = jnp.where(t + 1 < nt, st_ref[nbase + e], cap)
        a = jnp.minimum((start >> 4) << 4, cap - SLOT_WINDOW)
        win.append(a)
        fits = jnp.logical_and(fits, end - a <= SLOT_WINDOW)

    @pl.when(fits)
    def _():
        slot = lax.broadcasted_iota(I32, (tn, SLOT_WINDOW), 1)
        onehot = jnp.concatenate(
            [jnp.where(pos_t[:, e:e + 1] - win[e] == slot, 1.0, 0.0).astype(BF16) for e in range(N_EXPERTS)], axis=1)
        rows = jnp.concatenate(
            [o_ref[0, e, pl.ds(pl.multiple_of(win[e], 16), SLOT_WINDOW), :] for e in range(N_EXPERTS)], axis=0)
        out_ref[0] = x_ref[0] + g * _dot(onehot, rows)

    pl.when(jnp.logical_not(fits))(dense)


def _scatter(x, mod, pos_t, starts, o, cap, tn):
    b, n, d = x.shape
    nt = n // tn
    xs = pl.BlockSpec((1, tn, d), lambda i, t, st: (i, t, 0))
    kern = functools.partial(_scatter_kernel, tn=tn, cap=cap, nt=nt)
    return pl.pallas_call(
        kern,
        out_shape=jax.ShapeDtypeStruct((b, n, d), F32),
        grid_spec=pltpu.PrefetchScalarGridSpec(
            num_scalar_prefetch=1,
            grid=(b, nt),
            in_specs=[xs,
                      pl.BlockSpec((1, 1, 6 * d), lambda i, t, st: (i, 0, 0)),
                      pl.BlockSpec((1, tn, LANES), lambda i, t, st: (i, t, 0)),
                      pl.BlockSpec((1, N_EXPERTS, cap, d), lambda i, t, st: (i, 0, 0, 0))],
            out_specs=xs),
        compiler_params=_params("arbitrary", "arbitrary"),
        name="scatter_add",
    )(starts, x, mod, pos_t, o)


@functools.lru_cache(maxsize=None)
def _np_consts():
    lane = np.arange(GROUP_W)
    seg32 = (lane[:, None] // DIFF_D == lane[None, :] // DIFF_D).astype(np.float32)
    seg64 = (lane[:, None] // HEAD_DIM == lane[None, :] // HEAD_DIM).astype(np.float32)
    ang = 2.0 * np.pi * ((lane[:, None] % HEAD_DIM) * (lane[None, :] % HEAD_DIM) % HEAD_DIM) / HEAD_DIM
    cc = np.cos(ang) * seg64
    ss = np.sin(ang) * seg64
    tri = (lane[:, None] <= lane[None, :]).astype(np.float32)
    return dict(seg32=seg32, seg64=seg64, cc=cc, ss=ss, tri=tri)


@functools.lru_cache(maxsize=None)
def _np_dft(n):
    idx = (np.arange(n, dtype=np.int64)[:, None] * np.arange(n, dtype=np.int64)[None, :]) % n
    ang = 2.0 * np.pi * idx.astype(np.float64) / n
    return np.cos(ang).astype(np.float32), np.sin(ang).astype(np.float32)


@functools.lru_cache(maxsize=None)
def _np_rope(n):
    half = DIFF_D // 2
    inv = 1.0 / (ROPE_BASE ** (np.arange(0, half, 2, dtype=np.float32) / half))
    t = np.arange(n)
    row = (t // GRID_W).astype(np.float32)[:, None] * inv
    col = (t % GRID_W).astype(np.float32)[:, None] * inv
    nf = inv.shape[0]
    d = np.arange(GROUP_W) % DIFF_D
    f = d % nf
    is_col = d >= half
    second = (d % half) >= nf
    ang = np.where(is_col[None, :], col[:, f], row[:, f])
    c = np.cos(ang).astype(np.float32)
    s = np.sin(ang).astype(np.float32)
    s = np.where(second[None, :], s, -s)
    return c, s


@functools.lru_cache(maxsize=None)
def _np_na_index(n_rows):
    n_steps = n_rows // NA_QROWS
    reps = [0, 1, 2, n_steps - 2, n_steps - 1]
    tq = NA_QROWS * GRID_W
    roff = np.zeros((len(reps), NA_QROWS, NA_WIN_ROWS), np.int32)
    valid = np.zeros((len(reps), tq, NA_WIN), bool)
    for ci, t in enumerate(reps):
        ws = int(np.clip(NA_QROWS * t - NA_ROWS // 2, 0, n_rows - NA_WIN_ROWS))
        qi = np.arange(tq)
        r = NA_QROWS * t + qi // GRID_W
        qcol = qi % GRID_W
        kk = np.arange(NA_WIN)
        krow = ws + kk // GRID_W
        kcol = kk % GRID_W
        rstart = np.clip(r - NA_ROWS // 2, 0, n_rows - NA_ROWS)
        wstart = np.clip(qcol - NA_COLS // 2, 0, GRID_W - NA_COLS)
        vr = (krow[None, :] >= rstart[:, None]) & (krow[None, :] < rstart[:, None] + NA_ROWS)
        vc = (kcol[None, :] >= wstart[:, None]) & (kcol[None, :] < wstart[:, None] + NA_COLS)
        valid[ci] = vr & vc
        rows_q = NA_QROWS * t + np.arange(NA_QROWS)
        rows_k = ws + np.arange(NA_WIN_ROWS)
        roff[ci] = np.clip(rows_k[None, :] - rows_q[:, None] + NA_ROWS - 1, 0, 2 * NA_ROWS - 2)
    return roff, valid


def _na_table_kernel(r_ref, mask_ref, o_ref, toep_ref, *, roff):
    n_off = 2 * NA_ROWS - 1
    for h in range(HEADS):
        for ro in range(n_off):
            row = jnp.broadcast_to(r_ref[0, h, ro:ro + 1, :], (GRID_W, LANES))
            toep_ref[h, ro] = pltpu.roll(row, 0, 1, stride=1, stride_axis=0)
    lane = lax.broadcasted_iota(I32, (GRID_W, LANES), 1)
    n_cls = roff.shape[0]
    for ci in range(n_cls):
        for h in range(HEADS):
            for rr in range(NA_QROWS):
                r0 = h * NA_QROWS * GRID_W + rr * GRID_W
                for kp in range(NA_WIN_ROWS // 2):
                    left = toep_ref[h, int(roff[ci, rr, 2 * kp])]
                    right = pltpu.roll(toep_ref[h, int(roff[ci, rr, 2 * kp + 1])], GRID_W, 1)
                    bias = jnp.where(lane < GRID_W, left, right) * LOG2E
                    o_ref[0, ci, r0:r0 + GRID_W, kp * LANES:(kp + 1) * LANES] = (
                        bias + mask_ref[ci, rr * GRID_W:(rr + 1) * GRID_W, kp * LANES:(kp + 1) * LANES])


def _na_tables(rpb_all, n_rows):
    roff, valid = _np_na_index(n_rows)
    n_cls = roff.shape[0]
    depth = rpb_all.shape[0]
    n_off = 2 * NA_ROWS - 1
    r = jnp.concatenate([rpb_all[..., NA_COLS - 1:], jnp.zeros(rpb_all.shape[:-1] + (LANES - 2 * NA_COLS + 1,), F32),
                         rpb_all[..., :NA_COLS - 1]], axis=-1).astype(F32)
    mask = jnp.asarray(np.where(valid, 0.0, NEG_INF).astype(np.float32))
    tq = NA_QROWS * GRID_W
    kern = functools.partial(_na_table_kernel, roff=roff)
    return pl.pallas_call(
        kern,
        out_shape=jax.ShapeDtypeStruct((depth, n_cls, HEADS * tq, NA_WIN), F32),
        grid=(depth,),
        in_specs=[pl.BlockSpec((1, HEADS, n_off, LANES), lambda l: (l, 0, 0, 0)),
                  _full((n_cls, tq, NA_WIN))],
        out_specs=pl.BlockSpec((1, n_cls, HEADS * tq, NA_WIN), lambda l: (l, 0, 0, 0)),
        scratch_shapes=[pltpu.VMEM((HEADS, n_off, GRID_W, LANES), F32)],
        compiler_params=_params("arbitrary"),
        name="na_bias_table",
    )(r, mask)


def _moe(x, xn, logits, mod, tri, wg, wu, wd, layer, bb, tn):
    n = x.shape[1]
    cap = CAPACITY_FACTOR * n // N_EXPERTS
    pos, gate, pos_t, starts = _route(logits, tri, cap, tn)
    starts = starts[:, :, :n // tn].transpose(0, 2, 1).reshape(-1)
    o = _expert_ffn(xn, pos, gate, starts, wg, wu, wd, layer, cap, bb, tn)
    return _scatter(x, mod, pos_t, starts, o, cap, tn)


def kernel(x, c, ctx, c_ctx, ada_w, ada_b, norm1_g, norm2_g, w_in, w_out, head_out_g, sgu_w, sgu_b, diff_qn_g, diff_kn_g, diff_lambda, na_qn_g, na_kn_g, na_rpb, router_w, exp_w_gate, exp_w_up, exp_w_down):
    b, n, d = x.shape
    n_ctx = ctx.shape[1]
    npc = _np_consts()
    consts = {k: jnp.asarray(v, F32).astype(BF16) for k, v in npc.items()}
    seg64, tri = consts["seg64"], consts["tri"]
    cn, sn = (jnp.asarray(a, F32).astype(BF16) for a in _np_dft(n))
    cn_c, sn_c = (jnp.asarray(a, F32).astype(BF16) for a in _np_dft(n_ctx))
    rope = tuple(jnp.asarray(a, F32) for a in _np_rope(n))

    pad = (-(b + 1)) % 8
    c_rows = jnp.concatenate([c, c_ctx[None, :], jnp.zeros((pad, d), F32)], axis=0)
    mod_all = _modulation(c_rows, ada_w, ada_b)
    na_tables = _na_tables(na_rpb, n // GRID_W)

    xc = ctx
    for l in range(DEPTH):
        last = l == DEPTH - 1
        lam_init = 0.8 - 0.6 * math.exp(-0.3 * l)
        mod = mod_all[l, :b][:, None, :]
        mod_c = jnp.broadcast_to(mod_all[l, b][None, None, :], (b, 1, 6 * d))
        g1 = norm1_g[l][None, :]
        g2 = norm2_g[l][None, :]
        w_in_l = w_in[l].astype(BF16)
        w_out_l = w_out[l].astype(BF16)
        sguw = sgu_w[l].astype(BF16)
        sgub = jnp.repeat(sgu_b[l].T, HEAD_DIM, axis=1)
        hg = head_out_g[l].reshape(4, GROUP_W)
        vec = jnp.stack([jnp.tile(diff_qn_g[l], GROUP_W // DIFF_D), jnp.tile(diff_kn_g[l], GROUP_W // DIFF_D),
                         jnp.tile(na_qn_g[l], HEADS), jnp.tile(na_kn_g[l], HEADS),
                         hg[0], hg[1], hg[2], hg[3],
                         jnp.broadcast_to(jnp.max(jnp.abs(na_rpb[l])) * LOG2E, (GROUP_W,))], axis=0).astype(F32)
        vec = jnp.pad(vec, ((0, VEC_ROWS - vec.shape[0]), (0, 0)))
        lam = diff_lambda[l].astype(F32)
        rwt = router_w[l].T.astype(BF16)
        experts = (exp_w_gate, exp_w_up, exp_w_down)

        ya, zc, zs, qc, kct, vc, qd, kdt, vd = _in_proj(x, mod, g1, w_in_l, consts, sguw, sgub, vec, rope, 512)
        ya_c, zc_c, zs_c, qc_c, kct_c, vc_c, qd_c, kdt_c, vd_c = _in_proj(
            xc, mod_c, g1, w_in_l, consts, sguw, sgub, vec, None, n_ctx)

        yb = _fourier(zc, zs, cn, sn, seg64, vec, 1024)
        yc = _attention(qc, [(kct, vc), (kct_c, vc_c)], lam, lam_init, seg64, vec, 6, 1.0 - lam_init, 512, 512)
        yd = _na_attention(qd, kdt, vd, kdt_c, vd_c, na_tables, l, seg64, vec, 4)
        x, xn, logits = _out_proj(ya, yb, yc, yd, w_out_l, x, mod, g2, rwt, 512)
        x = _moe(x, xn, logits, mod, tri, *experts, l, 2, 512)

        if not last:
            yb_c = _fourier(zc_c, zs_c, cn_c, sn_c, seg64, vec, n_ctx)
            yc_c = _attention(qc_c, [(kct_c, vc_c)], lam, lam_init, seg64, vec, 6, 1.0 - lam_init, n_ctx)
            yd_c = _attention(qd_c, [(kdt_c, vd_c)], None, 0.0, seg64, vec, 7, 1.0, n_ctx)
            xc, xn_c, logits_c = _out_proj(ya_c, yb_c, yc_c, yd_c, w_out_l, xc, mod_c, g2, rwt, n_ctx)
            xc = _moe(xc, xn_c, logits_c, mod_c, tri, *experts, l, b, n_ctx)
    return x
```

```python
import functools
import math

import numpy as np
import jax
import jax.numpy as jnp
from jax import lax
from jax.experimental import pallas as pl
from jax.experimental.pallas import tpu as pltpu

F32 = jnp.float32
BF16 = jnp.bfloat16
I32 = jnp.int32

D_MODEL = 1024
DEPTH = 2
GRID_W = 64
LANES = 128
HEAD_DIM = 64
LOG2_HEAD_DIM = 6
GROUP_W = 256
HEADS = GROUP_W // HEAD_DIM
CHUNK = 128
DIFF_D = HEAD_DIM // 2
NA_ROWS = 8
NA_COLS = 16
N_EXPERTS = 16
CAPACITY_FACTOR = 2
ROPE_BASE = 10000.0
EPS = 1e-6
IN_W = 9 * GROUP_W
LOG2E = 1.4426950408889634

VMEM_LIMIT_BYTES = 56 * 1024 * 1024
NA_QROWS = 2
NA_WIN_ROWS = NA_ROWS + 2
NA_WIN = NA_WIN_ROWS * GRID_W
NA_HEAD_STACK = 1
NEG_INF = float("-inf")
VEC_ROWS = 16
SLOT_WINDOW = 128
SAFE_EXP2_BOUND = 48.0
BOUND_SLACK = 1.02


def _dot(a, b):
    return jnp.dot(a, b, preferred_element_type=F32)


def _params(*sem):
    return pltpu.CompilerParams(dimension_semantics=sem, vmem_limit_bytes=VMEM_LIMIT_BYTES)


def _full(shape):
    nd = len(shape)
    return pl.BlockSpec(shape, lambda *_: (0,) * nd)


def _seg_rms(x, seg, width):
    ss = _dot((x * x).astype(BF16), seg)
    return x * lax.rsqrt(ss * (1.0 / width) + EPS)


def _mod_kernel(c_ref, w_ref, b_ref, o_ref):
    s = jax.nn.silu(c_ref[...]).astype(BF16)
    o_ref[0] = _dot(s, w_ref[0].astype(BF16)) + b_ref[0]


def _modulation(c_rows, ada_w, ada_b):
    depth, d, w6 = ada_w.shape
    r = c_rows.shape[0]
    tn = 1024
    return pl.pallas_call(
        _mod_kernel,
        out_shape=jax.ShapeDtypeStruct((depth, r, w6), F32),
        grid=(depth, w6 // tn),
        in_specs=[
            pl.BlockSpec((r, d), lambda l, j: (0, 0)),
            pl.BlockSpec((1, d, tn), lambda l, j: (l, 0, j)),
            pl.BlockSpec((1, 1, tn), lambda l, j: (l, 0, j)),
        ],
        out_specs=pl.BlockSpec((1, r, tn), lambda l, j: (l, 0, j)),
        compiler_params=_params("arbitrary", "arbitrary"),
        name="modulation",
    )(c_rows, ada_w, ada_b.reshape(depth, 1, w6))


def _rope(x, c, s, lane):
    fwd = pltpu.roll(x, GROUP_W - 8, 1)
    bwd = pltpu.roll(x, 8, 1)
    partner = jnp.where((lane & 8) == 0, fwd, bwd)
    return x * c + partner * s


def _in_kernel(*refs, tm, use_rope, c_scale, d_scale):
    (x_ref, mod_ref, g1_ref, w_ref, seg32_ref, seg64_ref, cc_ref, ss_ref, sguw_ref, sgub_ref, vec_ref) = refs[:11]
    rest = refs[11:]
    if use_rope:
        ropec_ref, ropes_ref = rest[:2]
        rest = rest[2:]
    ya_ref, zc_ref, zs_ref, qc_ref, kct_ref, vc_ref, qd_ref, kdt_ref, vd_ref = rest

    x = x_ref[0]
    mod = mod_ref[0]
    sh = mod[:, 0:D_MODEL]
    sc = mod[:, D_MODEL:2 * D_MODEL]
    ms = jnp.mean(x * x, axis=-1, keepdims=True)
    h = x * lax.rsqrt(ms + EPS) * g1_ref[...]
    h = (h * (1.0 + sc) + sh).astype(BF16)
    lane = lax.broadcasted_iota(I32, (1, GROUP_W), 1)
    head = lane >> LOG2_HEAD_DIM
    seg32 = seg32_ref[...]
    seg64 = seg64_ref[...]
    vec = vec_ref[...]

    z = jax.nn.gelu(_dot(h, w_ref[:, 0:2 * GROUP_W]))
    u = z[:, 0:GROUP_W]
    vn = _seg_rms(z[:, GROUP_W:2 * GROUP_W], seg64, HEAD_DIM).astype(BF16)
    rows = []
    for c in range(tm // CHUNK):
        vch = vn[c * CHUNK:(c + 1) * CHUNK]
        sv = jnp.zeros((CHUNK, GROUP_W), F32)
        for hh in range(HEADS):
            sv = jnp.where(head == hh, _dot(sguw_ref[hh], vch), sv)
        rows.append(sv + sgub_ref[...])
    ya = u * jnp.concatenate(rows, axis=0)
    ya_ref[0] = (_seg_rms(ya, seg64, HEAD_DIM) * vec[4:5]).astype(BF16)

    zb = _dot(h, w_ref[:, 2 * GROUP_W:3 * GROUP_W]).astype(BF16)
    zc_ref[0] = _dot(zb, cc_ref[...]).astype(BF16)
    zs_ref[0] = _dot(zb, ss_ref[...]).astype(BF16)

    pc = _dot(h, w_ref[:, 3 * GROUP_W:6 * GROUP_W])
    q = _seg_rms(pc[:, 0:GROUP_W], seg32, DIFF_D) * vec[0:1]
    k = _seg_rms(pc[:, GROUP_W:2 * GROUP_W], seg32, DIFF_D) * vec[1:2]
    if use_rope:
        rc = ropec_ref[...]
        rs = ropes_ref[...]
        q = _rope(q, rc, rs, lane)
        k = _rope(k, rc, rs, lane)
    qc_ref[0] = (q * c_scale).astype(BF16)
    kct_ref[0] = k.T.astype(BF16)
    vc_ref[0] = pc[:, 2 * GROUP_W:3 * GROUP_W].astype(BF16)

    pd = _dot(h, w_ref[:, 6 * GROUP_W:9 * GROUP_W])
    qd = _seg_rms(pd[:, 0:GROUP_W], seg64, HEAD_DIM) * vec[2:3]
    kd = _seg_rms(pd[:, GROUP_W:2 * GROUP_W], seg64, HEAD_DIM) * vec[3:4]
    qd_ref[0] = (qd * d_scale).astype(BF16)
    kdt_ref[0] = kd.T.astype(BF16)
    vd_ref[0] = pd[:, 2 * GROUP_W:3 * GROUP_W].astype(BF16)


def _in_proj(x, mod, g1, w_in, consts, sguw, sgub, vec, rope, tm):
    b, n, d = x.shape
    use_rope = rope is not None
    tok = pl.BlockSpec((1, tm, GROUP_W), lambda i, t: (i, t, 0))
    tok_t = pl.BlockSpec((1, GROUP_W, tm), lambda i, t: (i, 0, t))
    in_specs = [
        pl.BlockSpec((1, tm, d), lambda i, t: (i, t, 0)),
        pl.BlockSpec((1, 1, 6 * d), lambda i, t: (i, 0, 0)),
        _full((1, d)),
        _full((d, IN_W)),
        _full((GROUP_W, GROUP_W)), _full((GROUP_W, GROUP_W)), _full((GROUP_W, GROUP_W)), _full((GROUP_W, GROUP_W)),
        _full((HEADS, CHUNK, CHUNK)),
        _full((CHUNK, GROUP_W)),
        _full((VEC_ROWS, GROUP_W)),
    ]
    args = [x, mod, g1, w_in, consts["seg32"], consts["seg64"], consts["cc"], consts["ss"], sguw, sgub, vec]
    if use_rope:
        in_specs += [pl.BlockSpec((tm, GROUP_W), lambda i, t: (t, 0))] * 2
        args += list(rope)
    sd = jax.ShapeDtypeStruct((b, n, GROUP_W), BF16)
    sdt = jax.ShapeDtypeStruct((b, GROUP_W, n), BF16)
    kern = functools.partial(
        _in_kernel, tm=tm, use_rope=use_rope,
        c_scale=(DIFF_D ** -0.5) * LOG2E, d_scale=(HEAD_DIM ** -0.5) * LOG2E)
    return pl.pallas_call(
        kern,
        out_shape=(sd, sd, sd, sd, sdt, sd, sd, sdt, sd),
        grid=(b, n // tm),
        in_specs=in_specs,
        out_specs=(tok, tok, tok, tok, tok_t, tok, tok, tok_t, tok),
        compiler_params=_params("arbitrary", "arbitrary"),
        name="in_proj",
    )(*args)


def _fourier_kernel(cn_ref, sn_ref, zc_ref, zs_ref, seg64_ref, vec_ref, o_ref, *, norm):
    y = (_dot(cn_ref[...], zc_ref[0]) - _dot(sn_ref[...], zs_ref[0])) * norm
    o_ref[0] = (_seg_rms(y, seg64_ref[...], HEAD_DIM) * vec_ref[5:6]).astype(BF16)


def _fourier(zc, zs, cn, sn, seg64, vec, tn):
    b, n, _ = zc.shape
    kern = functools.partial(_fourier_kernel, norm=1.0 / math.sqrt(n * HEAD_DIM))
    return pl.pallas_call(
        kern,
        out_shape=jax.ShapeDtypeStruct((b, n, GROUP_W), BF16),
        grid=(n // tn, b),
        in_specs=[
            pl.BlockSpec((tn, n), lambda t, i: (t, 0)),
            pl.BlockSpec((tn, n), lambda t, i: (t, 0)),
            pl.BlockSpec((1, n, GROUP_W), lambda t, i: (i, 0, 0)),
            pl.BlockSpec((1, n, GROUP_W), lambda t, i: (i, 0, 0)),
            _full((GROUP_W, GROUP_W)),
            _full((VEC_ROWS, GROUP_W)),
        ],
        out_specs=pl.BlockSpec((1, tn, GROUP_W), lambda t, i: (i, t, 0)),
        compiler_params=_params("arbitrary", "arbitrary"),
        name="fourier",
    )(cn, sn, zc, zs, seg64, vec)


def _attn_kernel(*refs, n_src, diff, lam_init, chunk, tq, vec_row, out_scale):
    q_ref = refs[0]
    rest = refs[1 + 2 * n_src:]
    if diff:
        lam_ref = rest[0]
        rest = rest[1:]
    seg64_ref, vec_ref, o_ref = rest

    q = q_ref[0]
    lane = lax.broadcasted_iota(I32, (1, GROUP_W), 1)
    if diff:
        lf = lam_ref[...]
        lam = (jnp.exp(jnp.sum(lf[0:1] * lf[1:2], axis=-1, keepdims=True))
               - jnp.exp(jnp.sum(lf[2:3] * lf[3:4], axis=-1, keepdims=True)) + lam_init)

    chunks = []
    for i in range(n_src):
        kt_ref, v_ref = refs[1 + 2 * i], refs[2 + 2 * i]
        nk = kt_ref.shape[2]
        chunks += [(kt_ref, v_ref, c0, min(chunk, nk - c0)) for c0 in range(0, nk, chunk)]

    width = DIFF_D if diff else HEAD_DIM
    gq = vec_ref[0:1] if diff else vec_ref[2:3]
    gk = vec_ref[1:2] if diff else vec_ref[3:4]
    bound = jnp.max(jnp.abs(gq)) * jnp.max(jnp.abs(gk)) * (math.sqrt(width) * LOG2E * BOUND_SLACK)

    def attend_bounded(sel):
        qm = jnp.where(sel, q, jnp.zeros_like(q))
        l = jnp.zeros((tq, 1), F32)
        acc = jnp.zeros((tq, GROUP_W), F32)
        for kt_ref, v_ref, c0, ck in chunks:
            p = jnp.exp2(_dot(qm, kt_ref[0, :, c0:c0 + ck]) - bound)
            l = l + jnp.sum(p, axis=-1, keepdims=True)
            acc = acc + _dot(p.astype(BF16), v_ref[0, c0:c0 + ck, :])
        return acc * (1.0 / l)

    def attend_online(sel):
        qm = jnp.where(sel, q, jnp.zeros_like(q))
        m = jnp.full((tq, 1), NEG_INF, F32)
        l = jnp.zeros((tq, 1), F32)
        acc = jnp.zeros((tq, GROUP_W), F32)
        for kt_ref, v_ref, c0, ck in chunks:
            s = _dot(qm, kt_ref[0, :, c0:c0 + ck])
            m_new = jnp.maximum(m, jnp.max(s, axis=-1, keepdims=True))
            alpha = jnp.exp2(m - m_new)
            p = jnp.exp2(s - m_new)
            l = alpha * l + jnp.sum(p, axis=-1, keepdims=True)
            acc = alpha * acc + _dot(p.astype(BF16), v_ref[0, c0:c0 + ck, :])
            m = m_new
        return acc * (1.0 / l)

    def head_out(attend, h):
        if diff:
            return (attend((lane >> (LOG2_HEAD_DIM - 1)) == 2 * h)
                    - lam * attend((lane >> (LOG2_HEAD_DIM - 1)) == 2 * h + 1))
        return attend((lane >> LOG2_HEAD_DIM) == h)

    def finish(out):
        y = _seg_rms(out, seg64_ref[...], HEAD_DIM) * vec_ref[vec_row:vec_row + 1]
        o_ref[0] = (y * out_scale).astype(BF16)

    def run_bounded():
        out = jnp.zeros((tq, GROUP_W), F32)
        for h in range(HEADS):
            out = jnp.where((lane >> LOG2_HEAD_DIM) == h, head_out(attend_bounded, h), out)
        finish(out)

    def run_online():
        def body(h, out):
            return jnp.where((lane >> LOG2_HEAD_DIM) == h, head_out(attend_online, h), out)

        finish(lax.fori_loop(0, HEADS, body, jnp.zeros((tq, GROUP_W), F32)))

    small = bound <= SAFE_EXP2_BOUND
    pl.when(small)(run_bounded)
    pl.when(jnp.logical_not(small))(run_online)


def _attention(q, srcs, lam, lam_init, seg64, vec, vec_row, out_scale, tq, chunk=768):
    b, nq, _ = q.shape
    diff = lam is not None
    in_specs = [pl.BlockSpec((1, tq, GROUP_W), lambda i, t: (i, t, 0))]
    args = [q]
    for kt, v in srcs:
        nk = v.shape[1]
        in_specs += [pl.BlockSpec((1, GROUP_W, nk), lambda i, t: (i, 0, 0)),
                     pl.BlockSpec((1, nk, GROUP_W), lambda i, t: (i, 0, 0))]
        args += [kt, v]
    if diff:
        in_specs.append(_full((4, DIFF_D)))
        args.append(lam)
    in_specs += [_full((GROUP_W, GROUP_W)), _full((VEC_ROWS, GROUP_W))]
    args += [seg64, vec]
    kern = functools.partial(_attn_kernel, n_src=len(srcs), diff=diff, lam_init=lam_init, chunk=chunk, tq=tq,
                             vec_row=vec_row, out_scale=out_scale)
    return pl.pallas_call(
        kern,
        out_shape=jax.ShapeDtypeStruct((b, nq, GROUP_W), BF16),
        grid=(b, nq // tq),
        in_specs=in_specs,
        out_specs=pl.BlockSpec((1, tq, GROUP_W), lambda i, t: (i, t, 0)),
        compiler_params=_params("arbitrary", "arbitrary"),
        name="diff_attention" if diff else "ctx_attention",
    )(*args)


def _na_kernel(q_ref, kt_ref, v_ref, ktc_ref, vc_ref, tab_ref, seg64_ref, vec_ref, o_ref, *, n_rows, pairs):
    n_steps = n_rows // NA_QROWS
    tq = NA_QROWS * GRID_W
    lane = lax.broadcasted_iota(I32, (1, GROUP_W), 1)
    head = lane >> LOG2_HEAD_DIM
    ktc = ktc_ref[0]
    vc = vc_ref[0]
    bound = (jnp.max(jnp.abs(vec_ref[2:3])) * jnp.max(jnp.abs(vec_ref[3:4])) * (math.sqrt(HEAD_DIM) * LOG2E * BOUND_SLACK)
             + jnp.max(vec_ref[8:9]))

    def run(bounded):
        for pi in range(pairs):
            t = pl.program_id(1) * pairs + pi
            ws = jnp.clip(NA_QROWS * t - NA_ROWS // 2, 0, n_rows - NA_WIN_ROWS)
            k0 = pl.multiple_of(ws * GRID_W, 128)
            tid = jnp.where(t < 2, t, jnp.where(t < n_steps - 2, 2, t - (n_steps - 5)))
            q = q_ref[0, pi * tq:(pi + 1) * tq, :]
            out = jnp.zeros((tq, GROUP_W), F32)
            for h0 in range(0, HEADS, NA_HEAD_STACK):
                hs = range(h0, h0 + NA_HEAD_STACK)
                qs = jnp.concatenate([jnp.where(head == hh, q, jnp.zeros_like(q)) for hh in hs], axis=0)
                s_loc = (_dot(qs, kt_ref[0, :, pl.ds(k0, NA_WIN)])
                         + tab_ref[0, tid, h0 * tq:(h0 + NA_HEAD_STACK) * tq, :])
                s_ctx = _dot(qs, ktc)
                if bounded:
                    m = bound
                else:
                    m = jnp.maximum(jnp.max(s_loc, axis=-1, keepdims=True), jnp.max(s_ctx, axis=-1, keepdims=True))
                p_loc = jnp.exp2(s_loc - m)
                p_ctx = jnp.exp2(s_ctx - m)
                l = jnp.sum(p_loc, axis=-1, keepdims=True) + jnp.sum(p_ctx, axis=-1, keepdims=True)
                o = _dot(p_loc.astype(BF16), v_ref[0, pl.ds(k0, NA_WIN), :]) + _dot(p_ctx.astype(BF16), vc)
                o = o * (1.0 / l)
                for i, hh in enumerate(hs):
                    out = jnp.where(head == hh, o[i * tq:(i + 1) * tq], out)
            o_ref[0, pi * tq:(pi + 1) * tq, :] = (
                _seg_rms(out, seg64_ref[...], HEAD_DIM) * vec_ref[7:8]).astype(BF16)

    small = bound <= SAFE_EXP2_BOUND
    pl.when(small)(lambda: run(True))
    pl.when(jnp.logical_not(small))(lambda: run(False))


def _na_attention(q, kt, v, ktc, vc, tables, layer, seg64, vec, pairs):
    b, n, _ = q.shape
    nc = vc.shape[1]
    tq = pairs * NA_QROWS * GRID_W
    kern = functools.partial(_na_kernel, n_rows=n // GRID_W, pairs=pairs)
    return pl.pallas_call(
        kern,
        out_shape=jax.ShapeDtypeStruct((b, n, GROUP_W), BF16),
        grid=(b, n // tq),
        in_specs=[
            pl.BlockSpec((1, tq, GROUP_W), lambda i, t: (i, t, 0)),
            pl.BlockSpec((1, GROUP_W, n), lambda i, t: (i, 0, 0)),
            pl.BlockSpec((1, n, GROUP_W), lambda i, t: (i, 0, 0)),
            pl.BlockSpec((1, GROUP_W, nc), lambda i, t: (i, 0, 0)),
            pl.BlockSpec((1, nc, GROUP_W), lambda i, t: (i, 0, 0)),
            pl.BlockSpec((1,) + tables.shape[1:], lambda i, t: (layer, 0, 0, 0)),
            _full((GROUP_W, GROUP_W)),
            _full((VEC_ROWS, GROUP_W)),
        ],
        out_specs=pl.BlockSpec((1, tq, GROUP_W), lambda i, t: (i, t, 0)),
        compiler_params=_params("arbitrary", "arbitrary"),
        name="neighbourhood_attention",
    )(q, kt, v, ktc, vc, tables, seg64, vec)


def _out_kernel(ya_ref, yb_ref, yc_ref, yd_ref, w_ref, x_ref, mod_ref, g2_ref, rwt_ref, o_ref, xn_ref, lg_ref):
    y = jnp.concatenate([ya_ref[0], yb_ref[0], yc_ref[0], yd_ref[0]], axis=-1)
    mod = mod_ref[0]
    x = x_ref[0] + mod[:, 2 * D_MODEL:3 * D_MODEL] * _dot(y, w_ref[...])
    o_ref[0] = x
    sh = mod[:, 3 * D_MODEL:4 * D_MODEL]
    sc = mod[:, 4 * D_MODEL:5 * D_MODEL]
    ms = jnp.mean(x * x, axis=-1, keepdims=True)
    h = x * lax.rsqrt(ms + EPS) * g2_ref[...]
    h = (h * (1.0 + sc) + sh).astype(BF16)
    xn_ref[0] = h
    lg_ref[0] = lax.dot_general(rwt_ref[...], h, (((1,), (1,)), ((), ())), preferred_element_type=F32)


def _out_proj(ya, yb, yc, yd, w_out, x, mod, g2, rwt, tm):
    b, n, d = x.shape
    tok = pl.BlockSpec((1, tm, GROUP_W), lambda i, t: (i, t, 0))
    xs = pl.BlockSpec((1, tm, d), lambda i, t: (i, t, 0))
    return pl.pallas_call(
        _out_kernel,
        out_shape=(jax.ShapeDtypeStruct((b, n, d), F32),
                   jax.ShapeDtypeStruct((b, n, d), BF16),
                   jax.ShapeDtypeStruct((b, N_EXPERTS, n), F32)),
        grid=(b, n // tm),
        in_specs=[tok, tok, tok, tok, _full((4 * GROUP_W, d)), xs,
                  pl.BlockSpec((1, 1, 6 * d), lambda i, t: (i, 0, 0)),
                  _full((1, d)), _full((N_EXPERTS, d))],
        out_specs=(xs, xs, pl.BlockSpec((1, N_EXPERTS, tm), lambda i, t: (i, 0, t))),
        compiler_params=_params("arbitrary", "arbitrary"),
        name="out_proj",
    )(ya, yb, yc, yd, w_out, x, mod, g2, rwt)


def _cumsum_excl(m, tri):
    n = m.shape[1]
    carry = jnp.zeros((m.shape[0], 1), F32)
    outs = []
    for j in range(n // GROUP_W):
        blk = m[:, j * GROUP_W:(j + 1) * GROUP_W]
        inc = _dot(blk.astype(BF16), tri)
        outs.append(inc - blk + carry)
        carry = carry + inc[:, GROUP_W - 1:GROUP_W]
    return jnp.concatenate(outs, axis=1)


def _route_kernel(lg_ref, tri_ref, pos_ref, gate_ref, post_ref, st_ref, *, n, cap, tile):
    lg = lg_ref[0]
    e = jnp.exp(lg - jnp.max(lg, axis=0, keepdims=True))
    aff = e / jnp.sum(e, axis=0, keepdims=True)
    gate_ref[0] = aff

    def unresolved(state):
        lo, hi = state
        return jnp.max(jnp.where(lo < hi, 1.0, 0.0)) > 0.0

    def bisect(state):
        lo, hi = state
        mid = 0.5 * (lo + hi)
        mid = jnp.where(mid > lo, mid, hi)
        ge = aff >= mid
        cnt = jnp.sum(jnp.where(ge, 1.0, 0.0), axis=1, keepdims=True)
        least_ge = jnp.min(jnp.where(ge, aff, jnp.inf), axis=1, keepdims=True)
        most_lt = jnp.max(jnp.where(ge, NEG_INF, aff), axis=1, keepdims=True)
        up = cnt >= cap
        return jnp.where(up, least_ge, lo), jnp.where(up, hi, most_lt)

    thr, _ = lax.while_loop(unresolved, bisect, (jnp.min(aff, axis=1, keepdims=True),
                                                 jnp.max(aff, axis=1, keepdims=True)))
    gt = aff > thr
    eq = aff == thr
    need = cap - jnp.sum(jnp.where(gt, 1.0, 0.0), axis=1, keepdims=True)
    tri = tri_ref[...]
    rank_eq = _cumsum_excl(jnp.where(eq, 1.0, 0.0), tri)
    sel = jnp.where(gt, 1.0, jnp.where(eq, jnp.where(rank_eq < need, 1.0, 0.0), 0.0))
    cum = _cumsum_excl(sel, tri)
    pos = jnp.where(sel > 0.0, cum, -1.0)
    pos_ref[0] = pos.astype(I32)
    tile_lane = lax.broadcasted_iota(I32, (N_EXPERTS, LANES), 1)
    starts = jnp.zeros((N_EXPERTS, LANES), F32)
    for t in range(n // tile):
        starts = jnp.where(tile_lane == t, cum[:, t * tile:t * tile + 1], starts)
    st_ref[0] = starts.astype(I32)
    padded = jnp.concatenate([pos, jnp.full((LANES - N_EXPERTS, n), -1.0, F32)], axis=0)
    post_ref[0] = padded.T.astype(I32)


def _route(logits, tri, cap, tile):
    b, _, n = logits.shape
    em = pl.BlockSpec((1, N_EXPERTS, n), lambda i: (i, 0, 0))
    kern = functools.partial(_route_kernel, n=n, cap=cap, tile=tile)
    return pl.pallas_call(
        kern,
        out_shape=(jax.ShapeDtypeStruct((b, N_EXPERTS, n), I32),
                   jax.ShapeDtypeStruct((b, N_EXPERTS, n), F32),
                   jax.ShapeDtypeStruct((b, n, LANES), I32),
                   jax.ShapeDtypeStruct((b, N_EXPERTS, LANES), I32)),
        grid=(b,),
        in_specs=[em, _full((GROUP_W, GROUP_W))],
        out_specs=(em, em, pl.BlockSpec((1, n, LANES), lambda i: (i, 0, 0)),
                   pl.BlockSpec((1, N_EXPERTS, LANES), lambda i: (i, 0, 0))),
        compiler_params=_params("arbitrary"),
        name="router",
    )(logits, tri)


def _ffn_kernel(st_ref, xn_ref, pos_ref, gate_ref, wg32_ref, wu32_ref, wd32_ref, o_ref, wg_ref, wu_ref, wd_ref, xg_ref,
                *, bb, cap, n, tile):
    @pl.when(pl.program_id(1) == 0)
    def _():
        wg_ref[...] = wg32_ref[0, 0].astype(BF16)
        wu_ref[...] = wu32_ref[0, 0].astype(BF16)
        wd_ref[...] = wd32_ref[0, 0].astype(BF16)

    expert = pl.ds(pl.program_id(0), 1)

    def step(gather):
        slot = lax.broadcasted_iota(I32, (cap, n), 0)
        hits = [pos_ref[i, expert, :] == slot for i in range(bb)]
        gs = [jnp.sum(jnp.where(hits[i], gate_ref[i, expert, :], 0.0), axis=1, keepdims=True) for i in range(bb)]
        g = gs[0] if bb == 1 else jnp.concatenate(gs, axis=0)
        gather(hits)
        xg = xg_ref[...].astype(BF16)
        hid = (jax.nn.silu(_dot(xg, wg_ref[...])) * _dot(xg, wu_ref[...])).astype(BF16)
        o = _dot(hid, wd_ref[...]) * g
        for i in range(bb):
            o_ref[i, 0] = o[i * cap:(i + 1) * cap].astype(BF16)

    def gather_dense(hits):
        for i in range(bb):
            xg_ref[i * cap:(i + 1) * cap, :] = _dot(jnp.where(hits[i], 1.0, 0.0).astype(BF16), xn_ref[i])

    if cap <= SLOT_WINDOW:
        step(gather_dense)
        return

    nt = n // tile
    win, fits = [], True
    for i in range(bb):
        base = (pl.program_id(1) * bb + i) * nt * N_EXPERTS + pl.program_id(0)
        for t in range(nt):
            start = st_ref[base + t * N_EXPERTS]
            end = st_ref[base + (t + 1) * N_EXPERTS] if t + 1 < nt else cap
            a = jnp.minimum((start >> 4) << 4, cap - SLOT_WINDOW)
            win.append(a)
            fits = jnp.logical_and(fits, end - a <= SLOT_WINDOW)

    def gather_windowed(hits):
        del hits
        xg_ref[...] = jnp.zeros_like(xg_ref)
        wslot = lax.broadcasted_iota(I32, (SLOT_WINDOW, tile), 0)
        for i in range(bb):
            for t in range(nt):
                a = win[i * nt + t]
                p = pos_ref[i, expert, t * tile:(t + 1) * tile]
                onehot = jnp.where(p - a == wslot, 1.0, 0.0).astype(BF16)
                rows = pl.ds(pl.multiple_of(i * cap + a, 16), SLOT_WINDOW)
                xg_ref[rows, :] = xg_ref[rows, :] + _dot(onehot, xn_ref[i, t * tile:(t + 1) * tile, :])

    pl.when(fits)(lambda: step(gather_windowed))
    pl.when(jnp.logical_not(fits))(lambda: step(gather_dense))


def _expert_ffn(xn, pos, gate, starts, wg, wu, wd, layer, cap, bb, tile):
    b, n, d = xn.shape
    wspec = pl.BlockSpec((1, 1, d, d), lambda e, j, st: (layer, e, 0, 0))
    sel = pl.BlockSpec((bb, N_EXPERTS, n), lambda e, j, st: (j, 0, 0))
    kern = functools.partial(_ffn_kernel, bb=bb, cap=cap, n=n, tile=tile)
    return pl.pallas_call(
        kern,
        out_shape=jax.ShapeDtypeStruct((b, N_EXPERTS, cap, d), BF16),
        grid_spec=pltpu.PrefetchScalarGridSpec(
            num_scalar_prefetch=1,
            grid=(N_EXPERTS, b // bb),
            in_specs=[pl.BlockSpec((bb, n, d), lambda e, j, st: (j, 0, 0)), sel, sel, wspec, wspec, wspec],
            out_specs=pl.BlockSpec((bb, 1, cap, d), lambda e, j, st: (j, e, 0, 0)),
            scratch_shapes=[pltpu.VMEM((d, d), BF16)] * 3 + [pltpu.VMEM((bb * cap, d), F32)]),
        compiler_params=_params("arbitrary", "arbitrary"),
        name="expert_ffn",
    )(starts, xn, pos, gate, wg, wu, wd)


def _scatter_kernel(st_ref, x_ref, mod_ref, pt_ref, o_ref, out_ref, *, tn, cap, nt, sub):
    for s in range(sub):
        _scatter_tile(st_ref, x_ref.at[0, s * tn:(s + 1) * tn], mod_ref, pt_ref.at[0, s * tn:(s + 1) * tn], o_ref,
                      out_ref.at[0, s * tn:(s + 1) * tn], pl.program_id(1) * sub + s, tn=tn, cap=cap, nt=nt)


def _scatter_tile(st_ref, x_ref, mod_ref, pt_ref, o_ref, out_ref, t, *, tn, cap, nt):
    pos_t = pt_ref[...]
    g = mod_ref[0][:, 5 * D_MODEL:6 * D_MODEL]

    def dense():
        if cap % LANES == 0:
            slot = lax.broadcasted_iota(I32, (tn, cap), 1)
            onehot = jnp.concatenate(
                [jnp.where(pos_t[:, e:e + 1] == slot, 1.0, 0.0).astype(BF16) for e in range(N_EXPERTS)], axis=1)
        else:
            slot = lax.broadcasted_iota(I32, (tn, N_EXPERTS * cap), 1)
            acc = jnp.zeros((tn, N_EXPERTS * cap), F32)
            for e in range(N_EXPERTS):
                pe = pos_t[:, e:e + 1]
                acc = jnp.where(jnp.where(pe >= 0, pe + e * cap, -1) == slot, 1.0, acc)
            onehot = acc.astype(BF16)
        y = _dot(onehot, o_ref[0].reshape(N_EXPERTS * cap, D_MODEL))
        out_ref[...] = x_ref[...] + g * y

    if cap <= SLOT_WINDOW:
        dense()
        return

    base = (pl.program_id(0) * nt + t) * N_EXPERTS
    nxt = jnp.minimum(t + 1, nt - 1)
    nbase = (pl.program_id(0) * nt + nxt) * N_EXPERTS
    win, fits = [], True
    for e in range(N_EXPERTS):
        start = st_ref[base + e]
        end = jnp.where(t + 1 < nt, st_ref[nbase + e], cap)
        a = jnp.minimum((start >> 4) << 4, cap - SLOT_WINDOW)
        win.append(a)
        fits = jnp.logical_and(fits, end - a <= SLOT_WINDOW)

    @pl.when(fits)
    def _():
        slot = lax.broadcasted_iota(I32, (tn, SLOT_WINDOW), 1)
        onehot = jnp.concatenate(
            [jnp.where(pos_t[:, e:e + 1] - win[e] == slot, 1.0, 0.0).astype(BF16) for e in range(N_EXPERTS)], axis=1)
        rows = jnp.concatenate(
            [o_ref[0, e, pl.ds(pl.multiple_of(win[e], 16), SLOT_WINDOW), :] for e in range(N_EXPERTS)], axis=0)
        out_ref[...] = x_ref[...] + g * _dot(onehot, rows)

    pl.when(jnp.logical_not(fits))(dense)


def _scatter(x, mod, pos_t, starts, o, cap, tn, sub):
    b, n, d = x.shape
    nt = n // tn
    xs = pl.BlockSpec((1, sub * tn, d), lambda i, t, st: (i, t, 0))
    kern = functools.partial(_scatter_kernel, tn=tn, cap=cap, nt=nt, sub=sub)
    return pl.pallas_call(
        kern,
        out_shape=jax.ShapeDtypeStruct((b, n, d), F32),
        grid_spec=pltpu.PrefetchScalarGridSpec(
            num_scalar_prefetch=1,
            grid=(b, nt // sub),
            in_specs=[xs,
                      pl.BlockSpec((1, 1, 6 * d), lambda i, t, st: (i, 0, 0)),
                      pl.BlockSpec((1, sub * tn, LANES), lambda i, t, st: (i, t, 0)),
                      pl.BlockSpec((1, N_EXPERTS, cap, d), lambda i, t, st: (i, 0, 0, 0))],
            out_specs=xs),
        compiler_params=_params("arbitrary", "arbitrary"),
        name="scatter_add",
    )(starts, x, mod, pos_t, o)


@functools.lru_cache(maxsize=None)
def _np_consts():
    lane = np.arange(GROUP_W)
    seg32 = (lane[:, None] // DIFF_D == lane[None, :] // DIFF_D).astype(np.float32)
    seg64 = (lane[:, None] // HEAD_DIM == lane[None, :] // HEAD_DIM).astype(np.float32)
    ang = 2.0 * np.pi * ((lane[:, None] % HEAD_DIM) * (lane[None, :] % HEAD_DIM) % HEAD_DIM) / HEAD_DIM
    cc = np.cos(ang) * seg64
    ss = np.sin(ang) * seg64
    tri = (lane[:, None] <= lane[None, :]).astype(np.float32)
    return dict(seg32=seg32, seg64=seg64, cc=cc, ss=ss, tri=tri)


@functools.lru_cache(maxsize=None)
def _np_dft(n):
    idx = (np.arange(n, dtype=np.int64)[:, None] * np.arange(n, dtype=np.int64)[None, :]) % n
    ang = 2.0 * np.pi * idx.astype(np.float64) / n
    return np.cos(ang).astype(np.float32), np.sin(ang).astype(np.float32)


@functools.lru_cache(maxsize=None)
def _np_rope(n):
    half = DIFF_D // 2
    inv = 1.0 / (ROPE_BASE ** (np.arange(0, half, 2, dtype=np.float32) / half))
    t = np.arange(n)
    row = (t // GRID_W).astype(np.float32)[:, None] * inv
    col = (t % GRID_W).astype(np.float32)[:, None] * inv
    nf = inv.shape[0]
    d = np.arange(GROUP_W) % DIFF_D
    f = d % nf
    is_col = d >= half
    second = (d % half) >= nf
    ang = np.where(is_col[None, :], col[:, f], row[:, f])
    c = np.cos(ang).astype(np.float32)
    s = np.sin(ang).astype(np.float32)
    s = np.where(second[None, :], s, -s)
    return c, s


@functools.lru_cache(maxsize=None)
def _np_na_index(n_rows):
    n_steps = n_rows // NA_QROWS
    reps = [0, 1, 2, n_steps - 2, n_steps - 1]
    tq = NA_QROWS * GRID_W
    roff = np.zeros((len(reps), NA_QROWS, NA_WIN_ROWS), np.int32)
    valid = np.zeros((len(reps), tq, NA_WIN), bool)
    for ci, t in enumerate(reps):
        ws = int(np.clip(NA_QROWS * t - NA_ROWS // 2, 0, n_rows - NA_WIN_ROWS))
        qi = np.arange(tq)
        r = NA_QROWS * t + qi // GRID_W
        qcol = qi % GRID_W
        kk = np.arange(NA_WIN)
        krow = ws + kk // GRID_W
        kcol = kk % GRID_W
        rstart = np.clip(r - NA_ROWS // 2, 0, n_rows - NA_ROWS)
        wstart = np.clip(qcol - NA_COLS // 2, 0, GRID_W - NA_COLS)
        vr = (krow[None, :] >= rstart[:, None]) & (krow[None, :] < rstart[:, None] + NA_ROWS)
        vc = (kcol[None, :] >= wstart[:, None]) & (kcol[None, :] < wstart[:, None] + NA_COLS)
        valid[ci] = vr & vc
        rows_q = NA_QROWS * t + np.arange(NA_QROWS)
        rows_k = ws + np.arange(NA_WIN_ROWS)
        roff[ci] = np.clip(rows_k[None, :] - rows_q[:, None] + NA_ROWS - 1, 0, 2 * NA_ROWS - 2)
    return roff, valid


def _na_table_kernel(r_ref, mask_ref, o_ref, toep_ref, *, roff):
    n_off = 2 * NA_ROWS - 1
    for h in range(HEADS):
        for ro in range(n_off):
            row = jnp.broadcast_to(r_ref[0, h, ro:ro + 1, :], (GRID_W, LANES))
            toep_ref[h, ro] = pltpu.roll(row, 0, 1, stride=1, stride_axis=0)
    lane = lax.broadcasted_iota(I32, (GRID_W, LANES), 1)
    n_cls = roff.shape[0]
    for ci in range(n_cls):
        for h in range(HEADS):
            for rr in range(NA_QROWS):
                r0 = h * NA_QROWS * GRID_W + rr * GRID_W
                for kp in range(NA_WIN_ROWS // 2):
                    left = toep_ref[h, int(roff[ci, rr, 2 * kp])]
                    right = pltpu.roll(toep_ref[h, int(roff[ci, rr, 2 * kp + 1])], GRID_W, 1)
                    bias = jnp.where(lane < GRID_W, left, right) * LOG2E
                    o_ref[0, ci, r0:r0 + GRID_W, kp * LANES:(kp + 1) * LANES] = (
                        bias + mask_ref[ci, rr * GRID_W:(rr + 1) * GRID_W, kp * LANES:(kp + 1) * LANES])


def _na_tables(rpb_all, n_rows):
    roff, valid = _np_na_index(n_rows)
    n_cls = roff.shape[0]
    depth = rpb_all.shape[0]
    n_off = 2 * NA_ROWS - 1
    r = jnp.concatenate([rpb_all[..., NA_COLS - 1:], jnp.zeros(rpb_all.shape[:-1] + (LANES - 2 * NA_COLS + 1,), F32),
                         rpb_all[..., :NA_COLS - 1]], axis=-1).astype(F32)
    mask = jnp.asarray(np.where(valid, 0.0, NEG_INF).astype(np.float32))
    tq = NA_QROWS * GRID_W
    kern = functools.partial(_na_table_kernel, roff=roff)
    return pl.pallas_call(
        kern,
        out_shape=jax.ShapeDtypeStruct((depth, n_cls, HEADS * tq, NA_WIN), F32),
        grid=(depth,),
        in_specs=[pl.BlockSpec((1, HEADS, n_off, LANES), lambda l: (l, 0, 0, 0)),
                  _full((n_cls, tq, NA_WIN))],
        out_specs=pl.BlockSpec((1, n_cls, HEADS * tq, NA_WIN), lambda l: (l, 0, 0, 0)),
        scratch_shapes=[pltpu.VMEM((HEADS, n_off, GRID_W, LANES), F32)],
        compiler_params=_params("arbitrary"),
        name="na_bias_table",
    )(r, mask)


def _moe(x, xn, logits, mod, tri, wg, wu, wd, layer, bb, tn):
    n = x.shape[1]
    cap = CAPACITY_FACTOR * n // N_EXPERTS
    pos, gate, pos_t, starts = _route(logits, tri, cap, tn)
    starts = starts[:, :, :n // tn].transpose(0, 2, 1).reshape(-1)
    o = _expert_ffn(xn, pos, gate, starts, wg, wu, wd, layer, cap, bb, tn)
    return _scatter(x, mod, pos_t, starts, o, cap, tn, min(2, n // tn))


def kernel(x, c, ctx, c_ctx, ada_w, ada_b, norm1_g, norm2_g, w_in, w_out, head_out_g, sgu_w, sgu_b, diff_qn_g, diff_kn_g, diff_lambda, na_qn_g, na_kn_g, na_rpb, router_w, exp_w_gate, exp_w_up, exp_w_down):
    b, n, d = x.shape
    n_ctx = ctx.shape[1]
    npc = _np_consts()
    consts = {k: jnp.asarray(v, F32).astype(BF16) for k, v in npc.items()}
    seg64, tri = consts["seg64"], consts["tri"]
    cn, sn = (jnp.asarray(a, F32).astype(BF16) for a in _np_dft(n))
    cn_c, sn_c = (jnp.asarray(a, F32).astype(BF16) for a in _np_dft(n_ctx))
    rope = tuple(jnp.asarray(a, F32) for a in _np_rope(n))

    pad = (-(b + 1)) % 8
    c_rows = jnp.concatenate([c, c_ctx[None, :], jnp.zeros((pad, d), F32)], axis=0)
    mod_all = _modulation(c_rows, ada_w, ada_b)
    na_tables = _na_tables(na_rpb, n // GRID_W)

    xc = ctx
    for l in range(DEPTH):
        last = l == DEPTH - 1
        lam_init = 0.8 - 0.6 * math.exp(-0.3 * l)
        mod = mod_all[l, :b][:, None, :]
        mod_c = jnp.broadcast_to(mod_all[l, b][None, None, :], (b, 1, 6 * d))
        g1 = norm1_g[l][None, :]
        g2 = norm2_g[l][None, :]
        w_in_l = w_in[l].astype(BF16)
        w_out_l = w_out[l].astype(BF16)
        sguw = sgu_w[l].astype(BF16)
        sgub = jnp.repeat(sgu_b[l].T, HEAD_DIM, axis=1)
        hg = head_out_g[l].reshape(4, GROUP_W)
        vec = jnp.stack([jnp.tile(diff_qn_g[l], GROUP_W // DIFF_D), jnp.tile(diff_kn_g[l], GROUP_W // DIFF_D),
                         jnp.tile(na_qn_g[l], HEADS), jnp.tile(na_kn_g[l], HEADS),
                         hg[0], hg[1], hg[2], hg[3],
                         jnp.broadcast_to(jnp.max(jnp.abs(na_rpb[l])) * LOG2E, (GROUP_W,))], axis=0).astype(F32)
        vec = jnp.pad(vec, ((0, VEC_ROWS - vec.shape[0]), (0, 0)))
        lam = diff_lambda[l].astype(F32)
        rwt = router_w[l].T.astype(BF16)
        experts = (exp_w_gate, exp_w_up, exp_w_down)

        ya, zc, zs, qc, kct, vc, qd, kdt, vd = _in_proj(x, mod, g1, w_in_l, consts, sguw, sgub, vec, rope, 1024)
        ya_c, zc_c, zs_c, qc_c, kct_c, vc_c, qd_c, kdt_c, vd_c = _in_proj(
            xc, mod_c, g1, w_in_l, consts, sguw, sgub, vec, None, n_ctx)

        yb = _fourier(zc, zs, cn, sn, seg64, vec, 1024)
        yc = _attention(qc, [(kct, vc), (kct_c, vc_c)], lam, lam_init, seg64, vec, 6, 1.0 - lam_init, 512, 512)
        yd = _na_attention(qd, kdt, vd, kdt_c, vd_c, na_tables, l, seg64, vec, 8)
        x, xn, logits = _out_proj(ya, yb, yc, yd, w_out_l, x, mod, g2, rwt, 1024)
        x = _moe(x, xn, logits, mod, tri, *experts, l, 2, 512)

        if not last:
            yb_c = _fourier(zc_c, zs_c, cn_c, sn_c, seg64, vec, n_ctx)
            yc_c = _attention(qc_c, [(kct_c, vc_c)], lam, lam_init, seg64, vec, 6, 1.0 - lam_init, n_ctx)
            yd_c = _attention(qd_c, [(kdt_c, vd_c)], None, 0.0, seg64, vec, 7, 1.0, n_ctx)
            xc, xn_c, logits_c = _out_proj(ya_c, yb_c, yc_c, yd_c, w_out_l, xc, mod_c, g2, rwt, n_ctx)
            xc = _moe(xc, xn_c, logits_c, mod_c, tri, *experts, l, b, n_ctx)
    return x
```

```python
import functools
import math

import numpy as np
import jax
import jax.numpy as jnp
from jax import lax
from jax.experimental import pallas as pl
from jax.experimental.pallas import tpu as pltpu

F32 = jnp.float32
BF16 = jnp.bfloat16
I32 = jnp.int32

D_MODEL = 1024
DEPTH = 2
GRID_W = 64
LANES = 128
HEAD_DIM = 64
LOG2_HEAD_DIM = 6
GROUP_W = 256
HEADS = GROUP_W // HEAD_DIM
CHUNK = 128
DIFF_D = HEAD_DIM // 2
NA_ROWS = 8
NA_COLS = 16
N_EXPERTS = 16
CAPACITY_FACTOR = 2
ROPE_BASE = 10000.0
EPS = 1e-6
IN_W = 9 * GROUP_W
LOG2E = 1.4426950408889634

VMEM_LIMIT_BYTES = 56 * 1024 * 1024
NA_QROWS = 2
NA_WIN_ROWS = NA_ROWS + 2
NA_WIN = NA_WIN_ROWS * GRID_W
NA_HEAD_STACK = 1
NEG_INF = float("-inf")
VEC_ROWS = 16
SLOT_WINDOW = 128
SAFE_EXP2_BOUND = 48.0
BOUND_SLACK = 1.02


def _dot(a, b):
    return jnp.dot(a, b, preferred_element_type=F32)


def _params(*sem):
    return pltpu.CompilerParams(dimension_semantics=sem, vmem_limit_bytes=VMEM_LIMIT_BYTES)


def _full(shape):
    nd = len(shape)
    return pl.BlockSpec(shape, lambda *_: (0,) * nd)


def _seg_rms(x, seg, width):
    ss = _dot((x * x).astype(BF16), seg)
    return x * lax.rsqrt(ss * (1.0 / width) + EPS)


def _mod_kernel(c_ref, w_ref, b_ref, o_ref):
    s = jax.nn.silu(c_ref[...]).astype(BF16)
    o_ref[0] = _dot(s, w_ref[0].astype(BF16)) + b_ref[0]


def _modulation(c_rows, ada_w, ada_b):
    depth, d, w6 = ada_w.shape
    r = c_rows.shape[0]
    tn = 1024
    return pl.pallas_call(
        _mod_kernel,
        out_shape=jax.ShapeDtypeStruct((depth, r, w6), F32),
        grid=(depth, w6 // tn),
        in_specs=[
            pl.BlockSpec((r, d), lambda l, j: (0, 0)),
            pl.BlockSpec((1, d, tn), lambda l, j: (l, 0, j)),
            pl.BlockSpec((1, 1, tn), lambda l, j: (l, 0, j)),
        ],
        out_specs=pl.BlockSpec((1, r, tn), lambda l, j: (l, 0, j)),
        compiler_params=_params("arbitrary", "arbitrary"),
        name="modulation",
    )(c_rows, ada_w, ada_b.reshape(depth, 1, w6))


def _rope(x, c, s, lane):
    fwd = pltpu.roll(x, GROUP_W - 8, 1)
    bwd = pltpu.roll(x, 8, 1)
    partner = jnp.where((lane & 8) == 0, fwd, bwd)
    return x * c + partner * s


def _in_kernel(*refs, tm, use_rope, c_scale, d_scale):
    (x_ref, mod_ref, g1_ref, w_ref, seg32_ref, seg64_ref, cc_ref, ss_ref, sguw_ref, sgub_ref, vec_ref) = refs[:11]
    rest = refs[11:]
    if use_rope:
        ropec_ref, ropes_ref = rest[:2]
        rest = rest[2:]
    ya_ref, zc_ref, zs_ref, qc_ref, kct_ref, vc_ref, qd_ref, kdt_ref, vd_ref = rest

    x = x_ref[0]
    mod = mod_ref[0]
    sh = mod[:, 0:D_MODEL]
    sc = mod[:, D_MODEL:2 * D_MODEL]
    ms = jnp.mean(x * x, axis=-1, keepdims=True)
    h = x * lax.rsqrt(ms + EPS) * g1_ref[...]
    h = (h * (1.0 + sc) + sh).astype(BF16)
    lane = lax.broadcasted_iota(I32, (1, GROUP_W), 1)
    head = lane >> LOG2_HEAD_DIM
    seg32 = seg32_ref[...]
    seg64 = seg64_ref[...]
    vec = vec_ref[...]

    z = jax.nn.gelu(_dot(h, w_ref[:, 0:2 * GROUP_W]))
    u = z[:, 0:GROUP_W]
    vn = _seg_rms(z[:, GROUP_W:2 * GROUP_W], seg64, HEAD_DIM).astype(BF16)
    rows = []
    for c in range(tm // CHUNK):
        vch = vn[c * CHUNK:(c + 1) * CHUNK]
        sv = jnp.zeros((CHUNK, GROUP_W), F32)
        for hh in range(HEADS):
            sv = jnp.where(head == hh, _dot(sguw_ref[hh], vch), sv)
        rows.append(sv + sgub_ref[...])
    ya = u * jnp.concatenate(rows, axis=0)
    ya_ref[0] = (_seg_rms(ya, seg64, HEAD_DIM) * vec[4:5]).astype(BF16)

    zb = _dot(h, w_ref[:, 2 * GROUP_W:3 * GROUP_W]).astype(BF16)
    zc_ref[0] = _dot(zb, cc_ref[...]).astype(BF16)
    zs_ref[0] = _dot(zb, ss_ref[...]).astype(BF16)

    pc = _dot(h, w_ref[:, 3 * GROUP_W:6 * GROUP_W])
    q = _seg_rms(pc[:, 0:GROUP_W], seg32, DIFF_D) * vec[0:1]
    k = _seg_rms(pc[:, GROUP_W:2 * GROUP_W], seg32, DIFF_D) * vec[1:2]
    if use_rope:
        rc = ropec_ref[...]
        rs = ropes_ref[...]
        q = _rope(q, rc, rs, lane)
        k = _rope(k, rc, rs, lane)
    qc_ref[0] = (q * c_scale).astype(BF16)
    kct_ref[0] = k.T.astype(BF16)
    vc_ref[0] = pc[:, 2 * GROUP_W:3 * GROUP_W].astype(BF16)

    pd = _dot(h, w_ref[:, 6 * GROUP_W:9 * GROUP_W])
    qd = _seg_rms(pd[:, 0:GROUP_W], seg64, HEAD_DIM) * vec[2:3]
    kd = _seg_rms(pd[:, GROUP_W:2 * GROUP_W], seg64, HEAD_DIM) * vec[3:4]
    qd_ref[0] = (qd * d_scale).astype(BF16)
    kdt_ref[0] = kd.T.astype(BF16)
    vd_ref[0] = pd[:, 2 * GROUP_W:3 * GROUP_W].astype(BF16)


def _in_proj(x, mod, g1, w_in, consts, sguw, sgub, vec, rope, tm):
    b, n, d = x.shape
    use_rope = rope is not None
    tok = pl.BlockSpec((1, tm, GROUP_W), lambda i, t: (i, t, 0))
    tok_t = pl.BlockSpec((1, GROUP_W, tm), lambda i, t: (i, 0, t))
    in_specs = [
        pl.BlockSpec((1, tm, d), lambda i, t: (i, t, 0)),
        pl.BlockSpec((1, 1, 6 * d), lambda i, t: (i, 0, 0)),
        _full((1, d)),
        _full((d, IN_W)),
        _full((GROUP_W, GROUP_W)), _full((GROUP_W, GROUP_W)), _full((GROUP_W, GROUP_W)), _full((GROUP_W, GROUP_W)),
        _full((HEADS, CHUNK, CHUNK)),
        _full((CHUNK, GROUP_W)),
        _full((VEC_ROWS, GROUP_W)),
    ]
    args = [x, mod, g1, w_in, consts["seg32"], consts["seg64"], consts["cc"], consts["ss"], sguw, sgub, vec]
    if use_rope:
        in_specs += [pl.BlockSpec((tm, GROUP_W), lambda i, t: (t, 0))] * 2
        args += list(rope)
    sd = jax.ShapeDtypeStruct((b, n, GROUP_W), BF16)
    sdt = jax.ShapeDtypeStruct((b, GROUP_W, n), BF16)
    kern = functools.partial(
        _in_kernel, tm=tm, use_rope=use_rope,
        c_scale=(DIFF_D ** -0.5) * LOG2E, d_scale=(HEAD_DIM ** -0.5) * LOG2E)
    return pl.pallas_call(
        kern,
        out_shape=(sd, sd, sd, sd, sdt, sd, sd, sdt, sd),
        grid=(b, n // tm),
        in_specs=in_specs,
        out_specs=(tok, tok, tok, tok, tok_t, tok, tok, tok_t, tok),
        compiler_params=_params("arbitrary", "arbitrary"),
        name="in_proj",
    )(*args)


def _fourier_kernel(cn_ref, sn_ref, zc_ref, zs_ref, seg64_ref, vec_ref, o_ref, *, norm):
    y = (_dot(cn_ref[...], zc_ref[0]) - _dot(sn_ref[...], zs_ref[0])) * norm
    o_ref[0] = (_seg_rms(y, seg64_ref[...], HEAD_DIM) * vec_ref[5:6]).astype(BF16)


def _fourier(zc, zs, cn, sn, seg64, vec, tn):
    b, n, _ = zc.shape
    kern = functools.partial(_fourier_kernel, norm=1.0 / math.sqrt(n * HEAD_DIM))
    return pl.pallas_call(
        kern,
        out_shape=jax.ShapeDtypeStruct((b, n, GROUP_W), BF16),
        grid=(n // tn, b),
        in_specs=[
            pl.BlockSpec((tn, n), lambda t, i: (t, 0)),
            pl.BlockSpec((tn, n), lambda t, i: (t, 0)),
            pl.BlockSpec((1, n, GROUP_W), lambda t, i: (i, 0, 0)),
            pl.BlockSpec((1, n, GROUP_W), lambda t, i: (i, 0, 0)),
            _full((GROUP_W, GROUP_W)),
            _full((VEC_ROWS, GROUP_W)),
        ],
        out_specs=pl.BlockSpec((1, tn, GROUP_W), lambda t, i: (i, t, 0)),
        compiler_params=_params("arbitrary", "arbitrary"),
        name="fourier",
    )(cn, sn, zc, zs, seg64, vec)


def _attn_kernel(*refs, n_src, diff, lam_init, chunk, tq, vec_row, out_scale):
    q_ref = refs[0]
    rest = refs[1 + 2 * n_src:]
    if diff:
        lam_ref = rest[0]
        rest = rest[1:]
    seg64_ref, vec_ref, o_ref = rest

    q = q_ref[0]
    lane = lax.broadcasted_iota(I32, (1, GROUP_W), 1)
    if diff:
        lf = lam_ref[...]
        lam = (jnp.exp(jnp.sum(lf[0:1] * lf[1:2], axis=-1, keepdims=True))
               - jnp.exp(jnp.sum(lf[2:3] * lf[3:4], axis=-1, keepdims=True)) + lam_init)

    chunks = []
    for i in range(n_src):
        kt_ref, v_ref = refs[1 + 2 * i], refs[2 + 2 * i]
        nk = kt_ref.shape[2]
        chunks += [(kt_ref, v_ref, c0, min(chunk, nk - c0)) for c0 in range(0, nk, chunk)]

    width = DIFF_D if diff else HEAD_DIM
    gq = vec_ref[0:1] if diff else vec_ref[2:3]
    gk = vec_ref[1:2] if diff else vec_ref[3:4]
    bound = jnp.max(jnp.abs(gq)) * jnp.max(jnp.abs(gk)) * (math.sqrt(width) * LOG2E * BOUND_SLACK)

    def attend_bounded(sel):
        qm = jnp.where(sel, q, jnp.zeros_like(q))
        l = jnp.zeros((tq, 1), F32)
        acc = jnp.zeros((tq, GROUP_W), F32)
        for kt_ref, v_ref, c0, ck in chunks:
            p = jnp.exp2(_dot(qm, kt_ref[0, :, c0:c0 + ck]) - bound)
            l = l + jnp.sum(p, axis=-1, keepdims=True)
            acc = acc + _dot(p.astype(BF16), v_ref[0, c0:c0 + ck, :])
        return acc * (1.0 / l)

    def attend_online(sel):
        qm = jnp.where(sel, q, jnp.zeros_like(q))
        m = jnp.full((tq, 1), NEG_INF, F32)
        l = jnp.zeros((tq, 1), F32)
        acc = jnp.zeros((tq, GROUP_W), F32)
        for kt_ref, v_ref, c0, ck in chunks:
            s = _dot(qm, kt_ref[0, :, c0:c0 + ck])
            m_new = jnp.maximum(m, jnp.max(s, axis=-1, keepdims=True))
            alpha = jnp.exp2(m - m_new)
            p = jnp.exp2(s - m_new)
            l = alpha * l + jnp.sum(p, axis=-1, keepdims=True)
            acc = alpha * acc + _dot(p.astype(BF16), v_ref[0, c0:c0 + ck, :])
            m = m_new
        return acc * (1.0 / l)

    def head_out(attend, h):
        if diff:
            return (attend((lane >> (LOG2_HEAD_DIM - 1)) == 2 * h)
                    - lam * attend((lane >> (LOG2_HEAD_DIM - 1)) == 2 * h + 1))
        return attend((lane >> LOG2_HEAD_DIM) == h)

    def finish(out):
        y = _seg_rms(out, seg64_ref[...], HEAD_DIM) * vec_ref[vec_row:vec_row + 1]
        o_ref[0] = (y * out_scale).astype(BF16)

    def run_bounded():
        out = jnp.zeros((tq, GROUP_W), F32)
        for h in range(HEADS):
            out = jnp.where((lane >> LOG2_HEAD_DIM) == h, head_out(attend_bounded, h), out)
        finish(out)

    def run_online():
        def body(h, out):
            return jnp.where((lane >> LOG2_HEAD_DIM) == h, head_out(attend_online, h), out)

        finish(lax.fori_loop(0, HEADS, body, jnp.zeros((tq, GROUP_W), F32)))

    small = bound <= SAFE_EXP2_BOUND
    pl.when(small)(run_bounded)
    pl.when(jnp.logical_not(small))(run_online)


def _attention(q, srcs, lam, lam_init, seg64, vec, vec_row, out_scale, tq, chunk=768):
    b, nq, _ = q.shape
    diff = lam is not None
    in_specs = [pl.BlockSpec((1, tq, GROUP_W), lambda i, t: (i, t, 0))]
    args = [q]
    for kt, v in srcs:
        nk = v.shape[1]
        in_specs += [pl.BlockSpec((1, GROUP_W, nk), lambda i, t: (i, 0, 0)),
                     pl.BlockSpec((1, nk, GROUP_W), lambda i, t: (i, 0, 0))]
        args += [kt, v]
    if diff:
        in_specs.append(_full((4, DIFF_D)))
        args.append(lam)
    in_specs += [_full((GROUP_W, GROUP_W)), _full((VEC_ROWS, GROUP_W))]
    args += [seg64, vec]
    kern = functools.partial(_attn_kernel, n_src=len(srcs), diff=diff, lam_init=lam_init, chunk=chunk, tq=tq,
                             vec_row=vec_row, out_scale=out_scale)
    return pl.pallas_call(
        kern,
        out_shape=jax.ShapeDtypeStruct((b, nq, GROUP_W), BF16),
        grid=(b, nq // tq),
        in_specs=in_specs,
        out_specs=pl.BlockSpec((1, tq, GROUP_W), lambda i, t: (i, t, 0)),
        compiler_params=_params("arbitrary", "arbitrary"),
        name="diff_attention" if diff else "ctx_attention",
    )(*args)


def _na_kernel(q_ref, kt_ref, v_ref, ktc_ref, vc_ref, tab_ref, seg64_ref, vec_ref, o_ref, *, n_rows, pairs):
    n_steps = n_rows // NA_QROWS
    tq = NA_QROWS * GRID_W
    lane = lax.broadcasted_iota(I32, (1, GROUP_W), 1)
    head = lane >> LOG2_HEAD_DIM
    ktc = ktc_ref[0]
    vc = vc_ref[0]
    bound = (jnp.max(jnp.abs(vec_ref[2:3])) * jnp.max(jnp.abs(vec_ref[3:4])) * (math.sqrt(HEAD_DIM) * LOG2E * BOUND_SLACK)
             + jnp.max(vec_ref[8:9]))

    def run(bounded):
        for pi in range(pairs):
            t = pl.program_id(1) * pairs + pi
            ws = jnp.clip(NA_QROWS * t - NA_ROWS // 2, 0, n_rows - NA_WIN_ROWS)
            k0 = pl.multiple_of(ws * GRID_W, 128)
            tid = jnp.where(t < 2, t, jnp.where(t < n_steps - 2, 2, t - (n_steps - 5)))
            q = q_ref[0, pi * tq:(pi + 1) * tq, :]
            out = jnp.zeros((tq, GROUP_W), F32)
            for h0 in range(0, HEADS, NA_HEAD_STACK):
                hs = range(h0, h0 + NA_HEAD_STACK)
                qs = jnp.concatenate([jnp.where(head == hh, q, jnp.zeros_like(q)) for hh in hs], axis=0)
                s_loc = (_dot(qs, kt_ref[0, :, pl.ds(k0, NA_WIN)])
                         + tab_ref[0, tid, h0 * tq:(h0 + NA_HEAD_STACK) * tq, :])
                s_ctx = _dot(qs, ktc)
                if bounded:
                    m = bound
                else:
                    m = jnp.maximum(jnp.max(s_loc, axis=-1, keepdims=True), jnp.max(s_ctx, axis=-1, keepdims=True))
                p_loc = jnp.exp2(s_loc - m)
                p_ctx = jnp.exp2(s_ctx - m)
                l = jnp.sum(p_loc, axis=-1, keepdims=True) + jnp.sum(p_ctx, axis=-1, keepdims=True)
                o = _dot(p_loc.astype(BF16), v_ref[0, pl.ds(k0, NA_WIN), :]) + _dot(p_ctx.astype(BF16), vc)
                o = o * (1.0 / l)
                for i, hh in enumerate(hs):
                    out = jnp.where(head == hh, o[i * tq:(i + 1) * tq], out)
            o_ref[0, pi * tq:(pi + 1) * tq, :] = (
                _seg_rms(out, seg64_ref[...], HEAD_DIM) * vec_ref[7:8]).astype(BF16)

    small = bound <= SAFE_EXP2_BOUND
    pl.when(small)(lambda: run(True))
    pl.when(jnp.logical_not(small))(lambda: run(False))


def _na_attention(q, kt, v, ktc, vc, tables, layer, seg64, vec, pairs):
    b, n, _ = q.shape
    nc = vc.shape[1]
    tq = pairs * NA_QROWS * GRID_W
    kern = functools.partial(_na_kernel, n_rows=n // GRID_W, pairs=pairs)
    return pl.pallas_call(
        kern,
        out_shape=jax.ShapeDtypeStruct((b, n, GROUP_W), BF16),
        grid=(b, n // tq),
        in_specs=[
            pl.BlockSpec((1, tq, GROUP_W), lambda i, t: (i, t, 0)),
            pl.BlockSpec((1, GROUP_W, n), lambda i, t: (i, 0, 0)),
            pl.BlockSpec((1, n, GROUP_W), lambda i, t: (i, 0, 0)),
            pl.BlockSpec((1, GROUP_W, nc), lambda i, t: (i, 0, 0)),
            pl.BlockSpec((1, nc, GROUP_W), lambda i, t: (i, 0, 0)),
            pl.BlockSpec((1,) + tables.shape[1:], lambda i, t: (layer, 0, 0, 0)),
            _full((GROUP_W, GROUP_W)),
            _full((VEC_ROWS, GROUP_W)),
        ],
        out_specs=pl.BlockSpec((1, tq, GROUP_W), lambda i, t: (i, t, 0)),
        compiler_params=_params("arbitrary", "arbitrary"),
        name="neighbourhood_attention",
    )(q, kt, v, ktc, vc, tables, seg64, vec)


def _out_kernel(ya_ref, yb_ref, yc_ref, yd_ref, w_ref, x_ref, mod_ref, g2_ref, rwt_ref, o_ref, xn_ref, lg_ref):
    y = jnp.concatenate([ya_ref[0], yb_ref[0], yc_ref[0], yd_ref[0]], axis=-1)
    mod = mod_ref[0]
    x = x_ref[0] + mod[:, 2 * D_MODEL:3 * D_MODEL] * _dot(y, w_ref[...])
    o_ref[0] = x
    sh = mod[:, 3 * D_MODEL:4 * D_MODEL]
    sc = mod[:, 4 * D_MODEL:5 * D_MODEL]
    ms = jnp.mean(x * x, axis=-1, keepdims=True)
    h = x * lax.rsqrt(ms + EPS) * g2_ref[...]
    h = (h * (1.0 + sc) + sh).astype(BF16)
    xn_ref[0] = h
    lg_ref[0] = lax.dot_general(rwt_ref[...], h, (((1,), (1,)), ((), ())), preferred_element_type=F32)


def _out_proj(ya, yb, yc, yd, w_out, x, mod, g2, rwt, tm):
    b, n, d = x.shape
    tok = pl.BlockSpec((1, tm, GROUP_W), lambda i, t: (i, t, 0))
    xs = pl.BlockSpec((1, tm, d), lambda i, t: (i, t, 0))
    return pl.pallas_call(
        _out_kernel,
        out_shape=(jax.ShapeDtypeStruct((b, n, d), F32),
                   jax.ShapeDtypeStruct((b, n, d), BF16),
                   jax.ShapeDtypeStruct((b, N_EXPERTS, n), F32)),
        grid=(b, n // tm),
        in_specs=[tok, tok, tok, tok, _full((4 * GROUP_W, d)), xs,
                  pl.BlockSpec((1, 1, 6 * d), lambda i, t: (i, 0, 0)),
                  _full((1, d)), _full((N_EXPERTS, d))],
        out_specs=(xs, xs, pl.BlockSpec((1, N_EXPERTS, tm), lambda i, t: (i, 0, t))),
        compiler_params=_params("arbitrary", "arbitrary"),
        name="out_proj",
    )(ya, yb, yc, yd, w_out, x, mod, g2, rwt)


def _cumsum_excl(m, tri):
    n = m.shape[1]
    carry = jnp.zeros((m.shape[0], 1), F32)
    outs = []
    for j in range(n // GROUP_W):
        blk = m[:, j * GROUP_W:(j + 1) * GROUP_W]
        inc = _dot(blk.astype(BF16), tri)
        outs.append(inc - blk + carry)
        carry = carry + inc[:, GROUP_W - 1:GROUP_W]
    return jnp.concatenate(outs, axis=1)


def _route_kernel(lg_ref, tri_ref, pos_ref, gate_ref, post_ref, st_ref, *, n, cap, tile, nb):
    affs = []
    for i in range(nb):
        lg = lg_ref[i]
        e = jnp.exp(lg - jnp.max(lg, axis=0, keepdims=True))
        affs.append(e / jnp.sum(e, axis=0, keepdims=True))
        gate_ref[i] = affs[i]
    aff = affs[0] if nb == 1 else jnp.concatenate(affs, axis=0)

    def unresolved(state):
        lo, hi = state
        return jnp.max(jnp.where(lo < hi, 1.0, 0.0)) > 0.0

    def bisect(state):
        lo, hi = state
        mid = 0.5 * (lo + hi)
        mid = jnp.where(mid > lo, mid, hi)
        ge = aff >= mid
        cnt = jnp.sum(jnp.where(ge, 1.0, 0.0), axis=1, keepdims=True)
        least_ge = jnp.min(jnp.where(ge, aff, jnp.inf), axis=1, keepdims=True)
        most_lt = jnp.max(jnp.where(ge, NEG_INF, aff), axis=1, keepdims=True)
        up = cnt >= cap
        return jnp.where(up, least_ge, lo), jnp.where(up, hi, most_lt)

    thr, _ = lax.while_loop(unresolved, bisect, (jnp.min(aff, axis=1, keepdims=True),
                                                 jnp.max(aff, axis=1, keepdims=True)))
    gt = aff > thr
    eq = aff == thr
    need = cap - jnp.sum(jnp.where(gt, 1.0, 0.0), axis=1, keepdims=True)
    tri = tri_ref[...]
    rank_eq = _cumsum_excl(jnp.where(eq, 1.0, 0.0), tri)
    sel = jnp.where(gt, 1.0, jnp.where(eq, jnp.where(rank_eq < need, 1.0, 0.0), 0.0))
    cum = _cumsum_excl(sel, tri)
    pos = jnp.where(sel > 0.0, cum, -1.0)
    tile_lane = lax.broadcasted_iota(I32, (nb * N_EXPERTS, LANES), 1)
    starts = jnp.zeros((nb * N_EXPERTS, LANES), F32)
    for t in range(n // tile):
        starts = jnp.where(tile_lane == t, cum[:, t * tile:t * tile + 1], starts)
    for i in range(nb):
        rows = slice(i * N_EXPERTS, (i + 1) * N_EXPERTS)
        pos_ref[i] = pos[rows].astype(I32)
        st_ref[i] = starts[rows].astype(I32)
        padded = jnp.concatenate([pos[rows], jnp.full((LANES - N_EXPERTS, n), -1.0, F32)], axis=0)
        post_ref[i] = padded.T.astype(I32)


def _route(logits, tri, cap, tile, nb=2):
    b, _, n = logits.shape
    em = pl.BlockSpec((nb, N_EXPERTS, n), lambda i: (i, 0, 0))
    kern = functools.partial(_route_kernel, n=n, cap=cap, tile=tile, nb=nb)
    return pl.pallas_call(
        kern,
        out_shape=(jax.ShapeDtypeStruct((b, N_EXPERTS, n), I32),
                   jax.ShapeDtypeStruct((b, N_EXPERTS, n), F32),
                   jax.ShapeDtypeStruct((b, n, LANES), I32),
                   jax.ShapeDtypeStruct((b, N_EXPERTS, LANES), I32)),
        grid=(b // nb,),
        in_specs=[em, _full((GROUP_W, GROUP_W))],
        out_specs=(em, em, pl.BlockSpec((nb, n, LANES), lambda i: (i, 0, 0)),
                   pl.BlockSpec((nb, N_EXPERTS, LANES), lambda i: (i, 0, 0))),
        compiler_params=_params("arbitrary"),
        name="router",
    )(logits, tri)


def _ffn_kernel(st_ref, xn_ref, pos_ref, gate_ref, wg32_ref, wu32_ref, wd32_ref, o_ref, wg_ref, wu_ref, wd_ref, xg_ref,
                *, bb, cap, n, tile):
    @pl.when(pl.program_id(1) == 0)
    def _():
        wg_ref[...] = wg32_ref[0, 0].astype(BF16)
        wu_ref[...] = wu32_ref[0, 0].astype(BF16)
        wd_ref[...] = wd32_ref[0, 0].astype(BF16)

    expert = pl.ds(pl.program_id(0), 1)

    def step(gather):
        slot = lax.broadcasted_iota(I32, (cap, n), 0)
        hits = [pos_ref[i, expert, :] == slot for i in range(bb)]
        gs = [jnp.sum(jnp.where(hits[i], gate_ref[i, expert, :], 0.0), axis=1, keepdims=True) for i in range(bb)]
        g = gs[0] if bb == 1 else jnp.concatenate(gs, axis=0)
        gather(hits)
        xg = xg_ref[...].astype(BF16)
        hid = (jax.nn.silu(_dot(xg, wg_ref[...])) * _dot(xg, wu_ref[...])).astype(BF16)
        o = _dot(hid, wd_ref[...]) * g
        for i in range(bb):
            o_ref[i, 0] = o[i * cap:(i + 1) * cap].astype(BF16)

    def gather_dense(hits):
        for i in range(bb):
            xg_ref[i * cap:(i + 1) * cap, :] = _dot(jnp.where(hits[i], 1.0, 0.0).astype(BF16), xn_ref[i])

    if cap <= SLOT_WINDOW:
        step(gather_dense)
        return

    nt = n // tile
    win, fits = [], True
    for i in range(bb):
        base = (pl.program_id(1) * bb + i) * nt * N_EXPERTS + pl.program_id(0)
        for t in range(nt):
            start = st_ref[base + t * N_EXPERTS]
            end = st_ref[base + (t + 1) * N_EXPERTS] if t + 1 < nt else cap
            a = jnp.minimum((start >> 4) << 4, cap - SLOT_WINDOW)
            win.append(a)
            fits = jnp.logical_and(fits, end - a <= SLOT_WINDOW)

    def gather_windowed(hits):
        del hits
        xg_ref[...] = jnp.zeros_like(xg_ref)
        wslot = lax.broadcasted_iota(I32, (SLOT_WINDOW, tile), 0)
        for i in range(bb):
            for t in range(nt):
                a = win[i * nt + t]
                p = pos_ref[i, expert, t * tile:(t + 1) * tile]
                onehot = jnp.where(p - a == wslot, 1.0, 0.0).astype(BF16)
                rows = pl.ds(pl.multiple_of(i * cap + a, 16), SLOT_WINDOW)
                xg_ref[rows, :] = xg_ref[rows, :] + _dot(onehot, xn_ref[i, t * tile:(t + 1) * tile, :])

    pl.when(fits)(lambda: step(gather_windowed))
    pl.when(jnp.logical_not(fits))(lambda: step(gather_dense))


def _expert_ffn(xn, pos, gate, starts, wg, wu, wd, layer, cap, bb, tile):
    b, n, d = xn.shape
    wspec = pl.BlockSpec((1, 1, d, d), lambda e, j, st: (layer, e, 0, 0))
    sel = pl.BlockSpec((bb, N_EXPERTS, n), lambda e, j, st: (j, 0, 0))
    kern = functools.partial(_ffn_kernel, bb=bb, cap=cap, n=n, tile=tile)
    return pl.pallas_call(
        kern,
        out_shape=jax.ShapeDtypeStruct((b, N_EXPERTS, cap, d), BF16),
        grid_spec=pltpu.PrefetchScalarGridSpec(
            num_scalar_prefetch=1,
            grid=(N_EXPERTS, b // bb),
            in_specs=[pl.BlockSpec((bb, n, d), lambda e, j, st: (j, 0, 0)), sel, sel, wspec, wspec, wspec],
            out_specs=pl.BlockSpec((bb, 1, cap, d), lambda e, j, st: (j, e, 0, 0)),
            scratch_shapes=[pltpu.VMEM((d, d), BF16)] * 3 + [pltpu.VMEM((bb * cap, d), F32)]),
        compiler_params=_params("arbitrary", "arbitrary"),
        name="expert_ffn",
    )(starts, xn, pos, gate, wg, wu, wd)


def _scatter_kernel(st_ref, x_ref, mod_ref, pt_ref, o_ref, out_ref, *, tn, cap, nt, sub):
    for s in range(sub):
        _scatter_tile(st_ref, x_ref.at[0, s * tn:(s + 1) * tn], mod_ref, pt_ref.at[0, s * tn:(s + 1) * tn], o_ref,
                      out_ref.at[0, s * tn:(s + 1) * tn], pl.program_id(1) * sub + s, tn=tn, cap=cap, nt=nt)


def _scatter_tile(st_ref, x_ref, mod_ref, pt_ref, o_ref, out_ref, t, *, tn, cap, nt):
    pos_t = pt_ref[...]
    g = mod_ref[0][:, 5 * D_MODEL:6 * D_MODEL]

    def dense():
        if cap % LANES == 0:
            slot = lax.broadcasted_iota(I32, (tn, cap), 1)
            onehot = jnp.concatenate(
                [jnp.where(pos_t[:, e:e + 1] == slot, 1.0, 0.0).astype(BF16) for e in range(N_EXPERTS)], axis=1)
        else:
            slot = lax.broadcasted_iota(I32, (tn, N_EXPERTS * cap), 1)
            acc = jnp.zeros((tn, N_EXPERTS * cap), F32)
            for e in range(N_EXPERTS):
                pe = pos_t[:, e:e + 1]
                acc = jnp.where(jnp.where(pe >= 0, pe + e * cap, -1) == slot, 1.0, acc)
            onehot = acc.astype(BF16)
        y = _dot(onehot, o_ref[0].reshape(N_EXPERTS * cap, D_MODEL))
        out_ref[...] = x_ref[...] + g * y

    if cap <= SLOT_WINDOW:
        dense()
        return

    base = (pl.program_id(0) * nt + t) * N_EXPERTS
    nxt = jnp.minimum(t + 1, nt - 1)
    nbase = (pl.program_id(0) * nt + nxt) * N_EXPERTS
    win, fits = [], True
    for e in range(N_EXPERTS):
        start = st_ref[base + e]
        end = jnp.where(t + 1 < nt, st_ref[nbase + e], cap)
        a = jnp.minimum((start >> 4) << 4, cap - SLOT_WINDOW)
        win.append(a)
        fits = jnp.logical_and(fits, end - a <= SLOT_WINDOW)

    @pl.when(fits)
    def _():
        slot = lax.broadcasted_iota(I32, (tn, SLOT_WINDOW), 1)
        onehot = jnp.concatenate(
            [jnp.where(pos_t[:, e:e + 1] - win[e] == slot, 1.0, 0.0).astype(BF16) for e in range(N_EXPERTS)], axis=1)
        rows = jnp.concatenate(
            [o_ref[0, e, pl.ds(pl.multiple_of(win[e], 16), SLOT_WINDOW), :] for e in range(N_EXPERTS)], axis=0)
        out_ref[...] = x_ref[...] + g * _dot(onehot, rows)

    pl.when(jnp.logical_not(fits))(dense)


def _scatter(x, mod, pos_t, starts, o, cap, tn, sub):
    b, n, d = x.shape
    nt = n // tn
    xs = pl.BlockSpec((1, sub * tn, d), lambda i, t, st: (i, t, 0))
    kern = functools.partial(_scatter_kernel, tn=tn, cap=cap, nt=nt, sub=sub)
    return pl.pallas_call(
        kern,
        out_shape=jax.ShapeDtypeStruct((b, n, d), F32),
        grid_spec=pltpu.PrefetchScalarGridSpec(
            num_scalar_prefetch=1,
            grid=(b, nt // sub),
            in_specs=[xs,
                      pl.BlockSpec((1, 1, 6 * d), lambda i, t, st: (i, 0, 0)),
                      pl.BlockSpec((1, sub * tn, LANES), lambda i, t, st: (i, t, 0)),
                      pl.BlockSpec((1, N_EXPERTS, cap, d), lambda i, t, st: (i, 0, 0, 0))],
            out_specs=xs),
        compiler_params=_params("arbitrary", "arbitrary"),
        name="scatter_add",
    )(starts, x, mod, pos_t, o)


@functools.lru_cache(maxsize=None)
def _np_consts():
    lane = np.arange(GROUP_W)
    seg32 = (lane[:, None] // DIFF_D == lane[None, :] // DIFF_D).astype(np.float32)
    seg64 = (lane[:, None] // HEAD_DIM == lane[None, :] // HEAD_DIM).astype(np.float32)
    ang = 2.0 * np.pi * ((lane[:, None] % HEAD_DIM) * (lane[None, :] % HEAD_DIM) % HEAD_DIM) / HEAD_DIM
    cc = np.cos(ang) * seg64
    ss = np.sin(ang) * seg64
    tri = (lane[:, None] <= lane[None, :]).astype(np.float32)
    return dict(seg32=seg32, seg64=seg64, cc=cc, ss=ss, tri=tri)


@functools.lru_cache(maxsize=None)
def _np_dft(n):
    idx = (np.arange(n, dtype=np.int64)[:, None] * np.arange(n, dtype=np.int64)[None, :]) % n
    ang = 2.0 * np.pi * idx.astype(np.float64) / n
    return np.cos(ang).astype(np.float32), np.sin(ang).astype(np.float32)


@functools.lru_cache(maxsize=None)
def _np_rope(n):
    half = DIFF_D // 2
    inv = 1.0 / (ROPE_BASE ** (np.arange(0, half, 2, dtype=np.float32) / half))
    t = np.arange(n)
    row = (t // GRID_W).astype(np.float32)[:, None] * inv
    col = (t % GRID_W).astype(np.float32)[:, None] * inv
    nf = inv.shape[0]
    d = np.arange(GROUP_W) % DIFF_D
    f = d % nf
    is_col = d >= half
    second = (d % half) >= nf
    ang = np.where(is_col[None, :], col[:, f], row[:, f])
    c = np.cos(ang).astype(np.float32)
    s = np.sin(ang).astype(np.float32)
    s = np.where(second[None, :], s, -s)
    return c, s


@functools.lru_cache(maxsize=None)
def _np_na_index(n_rows):
    n_steps = n_rows // NA_QROWS
    reps = [0, 1, 2, n_steps - 2, n_steps - 1]
    tq = NA_QROWS * GRID_W
    roff = np.zeros((len(reps), NA_QROWS, NA_WIN_ROWS), np.int32)
    valid = np.zeros((len(reps), tq, NA_WIN), bool)
    for ci, t in enumerate(reps):
        ws = int(np.clip(NA_QROWS * t - NA_ROWS // 2, 0, n_rows - NA_WIN_ROWS))
        qi = np.arange(tq)
        r = NA_QROWS * t + qi // GRID_W
        qcol = qi % GRID_W
        kk = np.arange(NA_WIN)
        krow = ws + kk // GRID_W
        kcol = kk % GRID_W
        rstart = np.clip(r - NA_ROWS // 2, 0, n_rows - NA_ROWS)
        wstart = np.clip(qcol - NA_COLS // 2, 0, GRID_W - NA_COLS)
        vr = (krow[None, :] >= rstart[:, None]) & (krow[None, :] < rstart[:, None] + NA_ROWS)
        vc = (kcol[None, :] >= wstart[:, None]) & (kcol[None, :] < wstart[:, None] + NA_COLS)
        valid[ci] = vr & vc
        rows_q = NA_QROWS * t + np.arange(NA_QROWS)
        rows_k = ws + np.arange(NA_WIN_ROWS)
        roff[ci] = np.clip(rows_k[None, :] - rows_q[:, None] + NA_ROWS - 1, 0, 2 * NA_ROWS - 2)
    return roff, valid


def _na_table_kernel(r_ref, mask_ref, o_ref, toep_ref, *, roff):
    n_off = 2 * NA_ROWS - 1
    for h in range(HEADS):
        for ro in range(n_off):
            row = jnp.broadcast_to(r_ref[0, h, ro:ro + 1, :], (GRID_W, LANES))
            toep_ref[h, ro] = pltpu.roll(row, 0, 1, stride=1, stride_axis=0)
    lane = lax.broadcasted_iota(I32, (GRID_W, LANES), 1)
    n_cls = roff.shape[0]
    for ci in range(n_cls):
        for h in range(HEADS):
            for rr in range(NA_QROWS):
                r0 = h * NA_QROWS * GRID_W + rr * GRID_W
                for kp in range(NA_WIN_ROWS // 2):
                    left = toep_ref[h, int(roff[ci, rr, 2 * kp])]
                    right = pltpu.roll(toep_ref[h, int(roff[ci, rr, 2 * kp + 1])], GRID_W, 1)
                    bias = jnp.where(lane < GRID_W, left, right) * LOG2E
                    o_ref[0, ci, r0:r0 + GRID_W, kp * LANES:(kp + 1) * LANES] = (
                        bias + mask_ref[ci, rr * GRID_W:(rr + 1) * GRID_W, kp * LANES:(kp + 1) * LANES])


def _na_tables(rpb_all, n_rows):
    roff, valid = _np_na_index(n_rows)
    n_cls = roff.shape[0]
    depth = rpb_all.shape[0]
    n_off = 2 * NA_ROWS - 1
    r = jnp.concatenate([rpb_all[..., NA_COLS - 1:], jnp.zeros(rpb_all.shape[:-1] + (LANES - 2 * NA_COLS + 1,), F32),
                         rpb_all[..., :NA_COLS - 1]], axis=-1).astype(F32)
    mask = jnp.asarray(np.where(valid, 0.0, NEG_INF).astype(np.float32))
    tq = NA_QROWS * GRID_W
    kern = functools.partial(_na_table_kernel, roff=roff)
    return pl.pallas_call(
        kern,
        out_shape=jax.ShapeDtypeStruct((depth, n_cls, HEADS * tq, NA_WIN), F32),
        grid=(depth,),
        in_specs=[pl.BlockSpec((1, HEADS, n_off, LANES), lambda l: (l, 0, 0, 0)),
                  _full((n_cls, tq, NA_WIN))],
        out_specs=pl.BlockSpec((1, n_cls, HEADS * tq, NA_WIN), lambda l: (l, 0, 0, 0)),
        scratch_shapes=[pltpu.VMEM((HEADS, n_off, GRID_W, LANES), F32)],
        compiler_params=_params("arbitrary"),
        name="na_bias_table",
    )(r, mask)


def _moe(x, xn, logits, mod, tri, wg, wu, wd, layer, bb, tn):
    n = x.shape[1]
    cap = CAPACITY_FACTOR * n // N_EXPERTS
    pos, gate, pos_t, starts = _route(logits, tri, cap, tn)
    starts = starts[:, :, :n // tn].transpose(0, 2, 1).reshape(-1)
    o = _expert_ffn(xn, pos, gate, starts, wg, wu, wd, layer, cap, bb, tn)
    return _scatter(x, mod, pos_t, starts, o, cap, tn, min(2, n // tn))


def kernel(x, c, ctx, c_ctx, ada_w, ada_b, norm1_g, norm2_g, w_in, w_out, head_out_g, sgu_w, sgu_b, diff_qn_g, diff_kn_g, diff_lambda, na_qn_g, na_kn_g, na_rpb, router_w, exp_w_gate, exp_w_up, exp_w_down):
    b, n, d = x.shape
    n_ctx = ctx.shape[1]
    npc = _np_consts()
    consts = {k: jnp.asarray(v, F32).astype(BF16) for k, v in npc.items()}
    seg64, tri = consts["seg64"], consts["tri"]
    cn, sn = (jnp.asarray(a, F32).astype(BF16) for a in _np_dft(n))
    cn_c, sn_c = (jnp.asarray(a, F32).astype(BF16) for a in _np_dft(n_ctx))
    rope = tuple(jnp.asarray(a, F32) for a in _np_rope(n))

    pad = (-(b + 1)) % 8
    c_rows = jnp.concatenate([c, c_ctx[None, :], jnp.zeros((pad, d), F32)], axis=0)
    mod_all = _modulation(c_rows, ada_w, ada_b)
    na_tables = _na_tables(na_rpb, n // GRID_W)

    xc = ctx
    for l in range(DEPTH):
        last = l == DEPTH - 1
        lam_init = 0.8 - 0.6 * math.exp(-0.3 * l)
        mod = mod_all[l, :b][:, None, :]
        mod_c = jnp.broadcast_to(mod_all[l, b][None, None, :], (b, 1, 6 * d))
        g1 = norm1_g[l][None, :]
        g2 = norm2_g[l][None, :]
        w_in_l = w_in[l].astype(BF16)
        w_out_l = w_out[l].astype(BF16)
        sguw = sgu_w[l].astype(BF16)
        sgub = jnp.repeat(sgu_b[l].T, HEAD_DIM, axis=1)
        hg = head_out_g[l].reshape(4, GROUP_W)
        vec = jnp.stack([jnp.tile(diff_qn_g[l], GROUP_W // DIFF_D), jnp.tile(diff_kn_g[l], GROUP_W // DIFF_D),
                         jnp.tile(na_qn_g[l], HEADS), jnp.tile(na_kn_g[l], HEADS),
                         hg[0], hg[1], hg[2], hg[3],
                         jnp.broadcast_to(jnp.max(jnp.abs(na_rpb[l])) * LOG2E, (GROUP_W,))], axis=0).astype(F32)
        vec = jnp.pad(vec, ((0, VEC_ROWS - vec.shape[0]), (0, 0)))
        lam = diff_lambda[l].astype(F32)
        rwt = router_w[l].T.astype(BF16)
        experts = (exp_w_gate, exp_w_up, exp_w_down)

        ya, zc, zs, qc, kct, vc, qd, kdt, vd = _in_proj(x, mod, g1, w_in_l, consts, sguw, sgub, vec, rope, 1024)
        ya_c, zc_c, zs_c, qc_c, kct_c, vc_c, qd_c, kdt_c, vd_c = _in_proj(
            xc, mod_c, g1, w_in_l, consts, sguw, sgub, vec, None, n_ctx)

        yb = _fourier(zc, zs, cn, sn, seg64, vec, 1024)
        yc = _attention(qc, [(kct, vc), (kct_c, vc_c)], lam, lam_init, seg64, vec, 6, 1.0 - lam_init, 512, 512)
        yd = _na_attention(qd, kdt, vd, kdt_c, vd_c, na_tables, l, seg64, vec, 8)
        x, xn, logits = _out_proj(ya, yb, yc, yd, w_out_l, x, mod, g2, rwt, 1024)
        x = _moe(x, xn, logits, mod, tri, *experts, l, 2, 512)

        if not last:
            yb_c = _fourier(zc_c, zs_c, cn_c, sn_c, seg64, vec, n_ctx)
            yc_c = _attention(qc_c, [(kct_c, vc_c)], lam, lam_init, seg64, vec, 6, 1.0 - lam_init, n_ctx)
            yd_c = _attention(qd_c, [(kdt_c, vd_c)], None, 0.0, seg64, vec, 7, 1.0, n_ctx)
            xc, xn_c, logits_c = _out_proj(ya_c, yb_c, yc_c, yd_c, w_out_l, xc, mod_c, g2, rwt, n_ctx)
            xc = _moe(xc, xn_c, logits_c, mod_c, tri, *experts, l, b, n_ctx)
    return x
```

```python
import functools
import math
from typing import NamedTuple

import numpy as np
import jax
import jax.numpy as jnp
from jax import lax
from jax.experimental import pallas as pl
from jax.experimental.pallas import tpu as pltpu

F32 = jnp.float32
BF16 = jnp.bfloat16
I32 = jnp.int32

D_MODEL = 1024
DEPTH = 2
GRID_W = 64
LANES = 128
HEAD_DIM = 64
LOG2_HEAD_DIM = 6
GROUP_W = 256
HEADS = GROUP_W // HEAD_DIM
CHUNK = 128
DIFF_D = HEAD_DIM // 2
NA_ROWS = 8
NA_COLS = 16
N_EXPERTS = 16
CAPACITY_FACTOR = 2
ROPE_BASE = 10000.0
EPS = 1e-6
IN_W = 9 * GROUP_W
LOG2E = 1.4426950408889634

VMEM_LIMIT_BYTES = 56 * 1024 * 1024
NA_QROWS = 2
NA_WIN_ROWS = NA_ROWS + 2
NA_WIN = NA_WIN_ROWS * GRID_W
NA_HEAD_STACK = 1
NEG_INF = float("-inf")
VEC_ROWS = 16
SLOT_WINDOW = 128
SAFE_EXP2_BOUND = 48.0
BOUND_SLACK = 1.02


def _dot(a, b):
    return jnp.dot(a, b, preferred_element_type=F32)


def _params(*sem):
    return pltpu.CompilerParams(dimension_semantics=sem, vmem_limit_bytes=VMEM_LIMIT_BYTES)


def _full(shape):
    nd = len(shape)
    return pl.BlockSpec(shape, lambda *_: (0,) * nd)


def _seg_rms(x, seg, width):
    ss = _dot((x * x).astype(BF16), seg)
    return x * lax.rsqrt(ss * (1.0 / width) + EPS)


def _mod_kernel(c_ref, w_ref, b_ref, o_ref):
    s = jax.nn.silu(c_ref[...]).astype(BF16)
    o_ref[0] = _dot(s, w_ref[0].astype(BF16)) + b_ref[0]


def _modulation(c_rows, ada_w, ada_b):
    depth, d, w6 = ada_w.shape
    r = c_rows.shape[0]
    tn = 1024
    return pl.pallas_call(
        _mod_kernel,
        out_shape=jax.ShapeDtypeStruct((depth, r, w6), F32),
        grid=(depth, w6 // tn),
        in_specs=[
            pl.BlockSpec((r, d), lambda l, j: (0, 0)),
            pl.BlockSpec((1, d, tn), lambda l, j: (l, 0, j)),
            pl.BlockSpec((1, 1, tn), lambda l, j: (l, 0, j)),
        ],
        out_specs=pl.BlockSpec((1, r, tn), lambda l, j: (l, 0, j)),
        compiler_params=_params("arbitrary", "arbitrary"),
        name="modulation",
    )(c_rows, ada_w, ada_b.reshape(depth, 1, w6))


def _rope(x, c, s, lane):
    fwd = pltpu.roll(x, GROUP_W - 8, 1)
    bwd = pltpu.roll(x, 8, 1)
    partner = jnp.where((lane & 8) == 0, fwd, bwd)
    return x * c + partner * s


def _in_kernel(*refs, tm, use_rope, kv_only, c_scale, d_scale):
    (x_ref, mod_ref, g1_ref, w_ref, seg32_ref, seg64_ref, cc_ref, ss_ref, sguw_ref, sgub_ref, vec_ref) = refs[:11]
    rest = refs[11:]
    if use_rope:
        ropec_ref, ropes_ref = rest[:2]
        rest = rest[2:]
    if kv_only:
        kct_ref, vc_ref, kdt_ref, vd_ref = rest
    else:
        ya_ref, zc_ref, zs_ref, qc_ref, kct_ref, vc_ref, qd_ref, kdt_ref, vd_ref = rest

    x = x_ref[0]
    mod = mod_ref[0]
    sh = mod[:, 0:D_MODEL]
    sc = mod[:, D_MODEL:2 * D_MODEL]
    ms = jnp.mean(x * x, axis=-1, keepdims=True)
    h = x * lax.rsqrt(ms + EPS) * g1_ref[...]
    h = (h * (1.0 + sc) + sh).astype(BF16)
    lane = lax.broadcasted_iota(I32, (1, GROUP_W), 1)
    head = lane >> LOG2_HEAD_DIM
    seg32 = seg32_ref[...]
    seg64 = seg64_ref[...]
    vec = vec_ref[...]

    if kv_only:
        pc = _dot(h, w_ref[:, 4 * GROUP_W:6 * GROUP_W])
        k = _seg_rms(pc[:, 0:GROUP_W], seg32, DIFF_D) * vec[1:2]
        kct_ref[0] = k.T.astype(BF16)
        vc_ref[0] = pc[:, GROUP_W:2 * GROUP_W].astype(BF16)
        pd = _dot(h, w_ref[:, 7 * GROUP_W:9 * GROUP_W])
        kd = _seg_rms(pd[:, 0:GROUP_W], seg64, HEAD_DIM) * vec[3:4]
        kdt_ref[0] = kd.T.astype(BF16)
        vd_ref[0] = pd[:, GROUP_W:2 * GROUP_W].astype(BF16)
        return

    z = jax.nn.gelu(_dot(h, w_ref[:, 0:2 * GROUP_W]))
    u = z[:, 0:GROUP_W]
    vn = _seg_rms(z[:, GROUP_W:2 * GROUP_W], seg64, HEAD_DIM).astype(BF16)
    rows = []
    for c in range(tm // CHUNK):
        vch = vn[c * CHUNK:(c + 1) * CHUNK]
        stacked = jnp.concatenate([jnp.where(head == hh, vch, jnp.zeros_like(vch)) for hh in range(HEADS)], axis=0)
        rows.append(_dot(sguw_ref[...], stacked) + sgub_ref[...])
    ya = u * jnp.concatenate(rows, axis=0)
    ya_ref[0] = (_seg_rms(ya, seg64, HEAD_DIM) * vec[4:5]).astype(BF16)

    zb = _dot(h, w_ref[:, 2 * GROUP_W:3 * GROUP_W]).astype(BF16)
    zc_ref[0] = _dot(zb, cc_ref[...]).astype(BF16)
    zs_ref[0] = _dot(zb, ss_ref[...]).astype(BF16)

    pc = _dot(h, w_ref[:, 3 * GROUP_W:6 * GROUP_W])
    q = _seg_rms(pc[:, 0:GROUP_W], seg32, DIFF_D) * vec[0:1]
    k = _seg_rms(pc[:, GROUP_W:2 * GROUP_W], seg32, DIFF_D) * vec[1:2]
    if use_rope:
        rc = ropec_ref[...]
        rs = ropes_ref[...]
        q = _rope(q, rc, rs, lane)
        k = _rope(k, rc, rs, lane)
    qc_ref[0] = (q * c_scale).astype(BF16)
    kct_ref[0] = k.T.astype(BF16)
    vc_ref[0] = pc[:, 2 * GROUP_W:3 * GROUP_W].astype(BF16)

    pd = _dot(h, w_ref[:, 6 * GROUP_W:9 * GROUP_W])
    qd = _seg_rms(pd[:, 0:GROUP_W], seg64, HEAD_DIM) * vec[2:3]
    kd = _seg_rms(pd[:, GROUP_W:2 * GROUP_W], seg64, HEAD_DIM) * vec[3:4]
    qd_ref[0] = (qd * d_scale).astype(BF16)
    kdt_ref[0] = kd.T.astype(BF16)
    vd_ref[0] = pd[:, 2 * GROUP_W:3 * GROUP_W].astype(BF16)


def _in_proj(x, mod, g1, w_in, consts, sguw, sgub, vec, rope, tm, kv_only=False):
    b, n, d = x.shape
    use_rope = rope is not None
    tok = pl.BlockSpec((1, tm, GROUP_W), lambda i, t: (i, t, 0))
    tok_t = pl.BlockSpec((1, GROUP_W, tm), lambda i, t: (i, 0, t))
    in_specs = [
        pl.BlockSpec((1, tm, d), lambda i, t: (i, t, 0)),
        pl.BlockSpec((1, 1, 6 * d), lambda i, t: (i, 0, 0)),
        _full((1, d)),
        _full((d, IN_W)),
        _full((GROUP_W, GROUP_W)), _full((GROUP_W, GROUP_W)), _full((GROUP_W, GROUP_W)), _full((GROUP_W, GROUP_W)),
        _full((CHUNK, HEADS * CHUNK)),
        _full((CHUNK, GROUP_W)),
        _full((VEC_ROWS, GROUP_W)),
    ]
    args = [x, mod, g1, w_in, consts["seg32"], consts["seg64"], consts["cc"], consts["ss"], sguw, sgub, vec]
    if use_rope:
        in_specs += [pl.BlockSpec((tm, GROUP_W), lambda i, t: (t, 0))] * 2
        args += list(rope)
    sd = jax.ShapeDtypeStruct((b, n, GROUP_W), BF16)
    sdt = jax.ShapeDtypeStruct((b, GROUP_W, n), BF16)
    kern = functools.partial(
        _in_kernel, tm=tm, use_rope=use_rope, kv_only=kv_only,
        c_scale=(DIFF_D ** -0.5) * LOG2E, d_scale=(HEAD_DIM ** -0.5) * LOG2E)
    return pl.pallas_call(
        kern,
        out_shape=(sdt, sd, sdt, sd) if kv_only else (sd, sd, sd, sd, sdt, sd, sd, sdt, sd),
        grid=(b, n // tm),
        in_specs=in_specs,
        out_specs=(tok_t, tok, tok_t, tok) if kv_only else (tok, tok, tok, tok, tok_t, tok, tok, tok_t, tok),
        compiler_params=_params("arbitrary", "arbitrary"),
        name="in_proj",
    )(*args)


def _fourier_kernel(cn_ref, sn_ref, zc_ref, zs_ref, seg64_ref, vec_ref, o_ref, *, norm):
    y = (_dot(cn_ref[...], zc_ref[0]) - _dot(sn_ref[...], zs_ref[0])) * norm
    o_ref[0] = (_seg_rms(y, seg64_ref[...], HEAD_DIM) * vec_ref[5:6]).astype(BF16)


def _fourier(zc, zs, cn, sn, seg64, vec, tn):
    b, n, _ = zc.shape
    kern = functools.partial(_fourier_kernel, norm=1.0 / math.sqrt(n * HEAD_DIM))
    return pl.pallas_call(
        kern,
        out_shape=jax.ShapeDtypeStruct((b, n, GROUP_W), BF16),
        grid=(n // tn, b),
        in_specs=[
            pl.BlockSpec((tn, n), lambda t, i: (t, 0)),
            pl.BlockSpec((tn, n), lambda t, i: (t, 0)),
            pl.BlockSpec((1, n, GROUP_W), lambda t, i: (i, 0, 0)),
            pl.BlockSpec((1, n, GROUP_W), lambda t, i: (i, 0, 0)),
            _full((GROUP_W, GROUP_W)),
            _full((VEC_ROWS, GROUP_W)),
        ],
        out_specs=pl.BlockSpec((1, tn, GROUP_W), lambda t, i: (i, t, 0)),
        compiler_params=_params("arbitrary", "arbitrary"),
        name="fourier",
    )(cn, sn, zc, zs, seg64, vec)


def _attn_kernel(*refs, n_src, diff, lam_init, chunk, tq, vec_row, out_scale):
    q_ref = refs[0]
    rest = refs[1 + 2 * n_src:]
    if diff:
        lam_ref = rest[0]
        rest = rest[1:]
    seg64_ref, vec_ref, o_ref = rest

    q = q_ref[0]
    lane = lax.broadcasted_iota(I32, (1, GROUP_W), 1)
    if diff:
        lf = lam_ref[...]
        lam = (jnp.exp(jnp.sum(lf[0:1] * lf[1:2], axis=-1, keepdims=True))
               - jnp.exp(jnp.sum(lf[2:3] * lf[3:4], axis=-1, keepdims=True)) + lam_init)

    chunks = []
    for i in range(n_src):
        kt_ref, v_ref = refs[1 + 2 * i], refs[2 + 2 * i]
        nk = kt_ref.shape[2]
        chunks += [(kt_ref, v_ref, c0, min(chunk, nk - c0)) for c0 in range(0, nk, chunk)]

    width = DIFF_D if diff else HEAD_DIM
    gq = vec_ref[0:1] if diff else vec_ref[2:3]
    gk = vec_ref[1:2] if diff else vec_ref[3:4]
    bound = jnp.max(jnp.abs(gq)) * jnp.max(jnp.abs(gk)) * (math.sqrt(width) * LOG2E * BOUND_SLACK)

    def attend_bounded(sel):
        qm = jnp.where(sel, q, jnp.zeros_like(q))
        l = jnp.zeros((tq, 1), F32)
        acc = jnp.zeros((tq, GROUP_W), F32)
        for kt_ref, v_ref, c0, ck in chunks:
            p = jnp.exp2(_dot(qm, kt_ref[0, :, c0:c0 + ck]) - bound)
            l = l + jnp.sum(p, axis=-1, keepdims=True)
            acc = acc + _dot(p.astype(BF16), v_ref[0, c0:c0 + ck, :])
        return acc * (1.0 / l)

    def attend_online(sel):
        qm = jnp.where(sel, q, jnp.zeros_like(q))
        m = jnp.full((tq, 1), NEG_INF, F32)
        l = jnp.zeros((tq, 1), F32)
        acc = jnp.zeros((tq, GROUP_W), F32)
        for kt_ref, v_ref, c0, ck in chunks:
            s = _dot(qm, kt_ref[0, :, c0:c0 + ck])
            m_new = jnp.maximum(m, jnp.max(s, axis=-1, keepdims=True))
            alpha = jnp.exp2(m - m_new)
            p = jnp.exp2(s - m_new)
            l = alpha * l + jnp.sum(p, axis=-1, keepdims=True)
            acc = alpha * acc + _dot(p.astype(BF16), v_ref[0, c0:c0 + ck, :])
            m = m_new
        return acc * (1.0 / l)

    def head_out(attend, h):
        if diff:
            return (attend((lane >> (LOG2_HEAD_DIM - 1)) == 2 * h)
                    - lam * attend((lane >> (LOG2_HEAD_DIM - 1)) == 2 * h + 1))
        return attend((lane >> LOG2_HEAD_DIM) == h)

    def finish(out):
        y = _seg_rms(out, seg64_ref[...], HEAD_DIM) * vec_ref[vec_row:vec_row + 1]
        o_ref[0] = (y * out_scale).astype(BF16)

    def run_bounded():
        out = jnp.zeros((tq, GROUP_W), F32)
        for h in range(HEADS):
            out = jnp.where((lane >> LOG2_HEAD_DIM) == h, head_out(attend_bounded, h), out)
        finish(out)

    def run_online():
        def body(h, out):
            return jnp.where((lane >> LOG2_HEAD_DIM) == h, head_out(attend_online, h), out)

        finish(lax.fori_loop(0, HEADS, body, jnp.zeros((tq, GROUP_W), F32)))

    small = bound <= SAFE_EXP2_BOUND
    pl.when(small)(run_bounded)
    pl.when(jnp.logical_not(small))(run_online)


def _attention(q, srcs, lam, lam_init, seg64, vec, vec_row, out_scale, tq, chunk=768):
    b, nq, _ = q.shape
    diff = lam is not None
    in_specs = [pl.BlockSpec((1, tq, GROUP_W), lambda i, t: (i, t, 0))]
    args = [q]
    for kt, v in srcs:
        nk = v.shape[1]
        in_specs += [pl.BlockSpec((1, GROUP_W, nk), lambda i, t: (i, 0, 0)),
                     pl.BlockSpec((1, nk, GROUP_W), lambda i, t: (i, 0, 0))]
        args += [kt, v]
    if diff:
        in_specs.append(_full((4, DIFF_D)))
        args.append(lam)
    in_specs += [_full((GROUP_W, GROUP_W)), _full((VEC_ROWS, GROUP_W))]
    args += [seg64, vec]
    kern = functools.partial(_attn_kernel, n_src=len(srcs), diff=diff, lam_init=lam_init, chunk=chunk, tq=tq,
                             vec_row=vec_row, out_scale=out_scale)
    return pl.pallas_call(
        kern,
        out_shape=jax.ShapeDtypeStruct((b, nq, GROUP_W), BF16),
        grid=(b, nq // tq),
        in_specs=in_specs,
        out_specs=pl.BlockSpec((1, tq, GROUP_W), lambda i, t: (i, t, 0)),
        compiler_params=_params("arbitrary", "arbitrary"),
        name="diff_attention" if diff else "ctx_attention",
    )(*args)


def _na_kernel(q_ref, kt_ref, v_ref, ktc_ref, vc_ref, tab_ref, seg64_ref, vec_ref, o_ref, *, n_rows, pairs):
    n_steps = n_rows // NA_QROWS
    tq = NA_QROWS * GRID_W
    lane = lax.broadcasted_iota(I32, (1, GROUP_W), 1)
    head = lane >> LOG2_HEAD_DIM
    ktc = ktc_ref[0]
    vc = vc_ref[0]
    bound = (jnp.max(jnp.abs(vec_ref[2:3])) * jnp.max(jnp.abs(vec_ref[3:4])) * (math.sqrt(HEAD_DIM) * LOG2E * BOUND_SLACK)
             + jnp.max(vec_ref[8:9]))

    def run(bounded):
        for pi in range(pairs):
            t = pl.program_id(1) * pairs + pi
            ws = jnp.clip(NA_QROWS * t - NA_ROWS // 2, 0, n_rows - NA_WIN_ROWS)
            k0 = pl.multiple_of(ws * GRID_W, 128)
            tid = jnp.where(t < 2, t, jnp.where(t < n_steps - 2, 2, t - (n_steps - 5)))
            q = q_ref[0, pi * tq:(pi + 1) * tq, :]
            out = jnp.zeros((tq, GROUP_W), F32)
            for h0 in range(0, HEADS, NA_HEAD_STACK):
                hs = range(h0, h0 + NA_HEAD_STACK)
                qs = jnp.concatenate([jnp.where(head == hh, q, jnp.zeros_like(q)) for hh in hs], axis=0)
                s_loc = (_dot(qs, kt_ref[0, :, pl.ds(k0, NA_WIN)])
                         + tab_ref[0, tid, h0 * tq:(h0 + NA_HEAD_STACK) * tq, :])
                s_ctx = _dot(qs, ktc)
                if bounded:
                    m = bound
                else:
                    m = jnp.maximum(jnp.max(s_loc, axis=-1, keepdims=True), jnp.max(s_ctx, axis=-1, keepdims=True))
                p_loc = jnp.exp2(s_loc - m)
                p_ctx = jnp.exp2(s_ctx - m)
                l = jnp.sum(p_loc, axis=-1, keepdims=True) + jnp.sum(p_ctx, axis=-1, keepdims=True)
                o = _dot(p_loc.astype(BF16), v_ref[0, pl.ds(k0, NA_WIN), :]) + _dot(p_ctx.astype(BF16), vc)
                o = o * (1.0 / l)
                for i, hh in enumerate(hs):
                    out = jnp.where(head == hh, o[i * tq:(i + 1) * tq], out)
            o_ref[0, pi * tq:(pi + 1) * tq, :] = (
                _seg_rms(out, seg64_ref[...], HEAD_DIM) * vec_ref[7:8]).astype(BF16)

    small = bound <= SAFE_EXP2_BOUND
    pl.when(small)(lambda: run(True))
    pl.when(jnp.logical_not(small))(lambda: run(False))


def _na_attention(q, kt, v, ktc, vc, tables, layer, seg64, vec, pairs):
    b, n, _ = q.shape
    nc = vc.shape[1]
    tq = pairs * NA_QROWS * GRID_W
    kern = functools.partial(_na_kernel, n_rows=n // GRID_W, pairs=pairs)
    return pl.pallas_call(
        kern,
        out_shape=jax.ShapeDtypeStruct((b, n, GROUP_W), BF16),
        grid=(b, n // tq),
        in_specs=[
            pl.BlockSpec((1, tq, GROUP_W), lambda i, t: (i, t, 0)),
            pl.BlockSpec((1, GROUP_W, n), lambda i, t: (i, 0, 0)),
            pl.BlockSpec((1, n, GROUP_W), lambda i, t: (i, 0, 0)),
            pl.BlockSpec((1, GROUP_W, nc), lambda i, t: (i, 0, 0)),
            pl.BlockSpec((1, nc, GROUP_W), lambda i, t: (i, 0, 0)),
            pl.BlockSpec((1,) + tables.shape[1:], lambda i, t: (layer, 0, 0, 0)),
            _full((GROUP_W, GROUP_W)),
            _full((VEC_ROWS, GROUP_W)),
        ],
        out_specs=pl.BlockSpec((1, tq, GROUP_W), lambda i, t: (i, t, 0)),
        compiler_params=_params("arbitrary", "arbitrary"),
        name="neighbourhood_attention",
    )(q, kt, v, ktc, vc, tables, seg64, vec)


def _out_kernel(ya_ref, yb_ref, yc_ref, yd_ref, w_ref, x_ref, mod_ref, g2_ref, rwt_ref, o_ref, xn_ref, lg_ref):
    y = jnp.concatenate([ya_ref[0], yb_ref[0], yc_ref[0], yd_ref[0]], axis=-1)
    mod = mod_ref[0]
    x = x_ref[0] + mod[:, 2 * D_MODEL:3 * D_MODEL] * _dot(y, w_ref[...])
    o_ref[0] = x
    sh = mod[:, 3 * D_MODEL:4 * D_MODEL]
    sc = mod[:, 4 * D_MODEL:5 * D_MODEL]
    ms = jnp.mean(x * x, axis=-1, keepdims=True)
    h = x * lax.rsqrt(ms + EPS) * g2_ref[...]
    h = (h * (1.0 + sc) + sh).astype(BF16)
    xn_ref[0] = h
    lg_ref[0] = lax.dot_general(rwt_ref[...], h, (((1,), (1,)), ((), ())), preferred_element_type=F32)


def _out_proj(ya, yb, yc, yd, w_out, x, mod, g2, rwt, tm):
    b, n, d = x.shape
    tok = pl.BlockSpec((1, tm, GROUP_W), lambda i, t: (i, t, 0))
    xs = pl.BlockSpec((1, tm, d), lambda i, t: (i, t, 0))
    return pl.pallas_call(
        _out_kernel,
        out_shape=(jax.ShapeDtypeStruct((b, n, d), F32),
                   jax.ShapeDtypeStruct((b, n, d), BF16),
                   jax.ShapeDtypeStruct((b, N_EXPERTS, n), F32)),
        grid=(b, n // tm),
        in_specs=[tok, tok, tok, tok, _full((4 * GROUP_W, d)), xs,
                  pl.BlockSpec((1, 1, 6 * d), lambda i, t: (i, 0, 0)),
                  _full((1, d)), _full((N_EXPERTS, d))],
        out_specs=(xs, xs, pl.BlockSpec((1, N_EXPERTS, tm), lambda i, t: (i, 0, t))),
        compiler_params=_params("arbitrary", "arbitrary"),
        name="out_proj",
    )(ya, yb, yc, yd, w_out, x, mod, g2, rwt)


def _cumsum_excl(m, tri):
    n = m.shape[1]
    carry = jnp.zeros((m.shape[0], 1), F32)
    outs = []
    for j in range(n // GROUP_W):
        blk = m[:, j * GROUP_W:(j + 1) * GROUP_W]
        inc = _dot(blk.astype(BF16), tri)
        outs.append(inc - blk + carry)
        carry = carry + inc[:, GROUP_W - 1:GROUP_W]
    return jnp.concatenate(outs, axis=1)


def _route_kernel(lg_ref, tri_ref, pos_ref, gate_ref, post_ref, st_ref, *, n, cap, tile, nb):
    affs = []
    for i in range(nb):
        lg = lg_ref[i]
        e = jnp.exp(lg - jnp.max(lg, axis=0, keepdims=True))
        affs.append(e / jnp.sum(e, axis=0, keepdims=True))
        gate_ref[i] = affs[i]
    aff = affs[0] if nb == 1 else jnp.concatenate(affs, axis=0)

    def unresolved(state):
        lo, hi = state
        return jnp.max(jnp.where(lo < hi, 1.0, 0.0)) > 0.0

    def bisect(state):
        lo, hi = state
        mid = 0.5 * (lo + hi)
        mid = jnp.where(mid > lo, mid, hi)
        ge = aff >= mid
        cnt = jnp.sum(jnp.where(ge, 1.0, 0.0), axis=1, keepdims=True)
        least_ge = jnp.min(jnp.where(ge, aff, jnp.inf), axis=1, keepdims=True)
        most_lt = jnp.max(jnp.where(ge, NEG_INF, aff), axis=1, keepdims=True)
        up = cnt >= cap
        return jnp.where(up, least_ge, lo), jnp.where(up, hi, most_lt)

    thr, _ = lax.while_loop(unresolved, bisect, (jnp.min(aff, axis=1, keepdims=True),
                                                 jnp.max(aff, axis=1, keepdims=True)))
    gt = aff > thr
    eq = aff == thr
    need = cap - jnp.sum(jnp.where(gt, 1.0, 0.0), axis=1, keepdims=True)
    tri = tri_ref[...]
    rank_eq = _cumsum_excl(jnp.where(eq, 1.0, 0.0), tri)
    sel = jnp.where(gt, 1.0, jnp.where(eq, jnp.where(rank_eq < need, 1.0, 0.0), 0.0))
    cum = _cumsum_excl(sel, tri)
    pos = jnp.where(sel > 0.0, cum, -1.0)
    tile_lane = lax.broadcasted_iota(I32, (nb * N_EXPERTS, LANES), 1)
    starts = jnp.zeros((nb * N_EXPERTS, LANES), F32)
    for t in range(n // tile):
        starts = jnp.where(tile_lane == t, cum[:, t * tile:t * tile + 1], starts)
    for i in range(nb):
        rows = slice(i * N_EXPERTS, (i + 1) * N_EXPERTS)
        pos_ref[i] = pos[rows].astype(I32)
        st_ref[i] = starts[rows].astype(I32)
        padded = jnp.concatenate([pos[rows], jnp.full((LANES - N_EXPERTS, n), -1.0, F32)], axis=0)
        post_ref[i] = padded.T.astype(I32)


def _route(logits, tri, cap, tile, nb=2):
    b, _, n = logits.shape
    em = pl.BlockSpec((nb, N_EXPERTS, n), lambda i: (i, 0, 0))
    kern = functools.partial(_route_kernel, n=n, cap=cap, tile=tile, nb=nb)
    return pl.pallas_call(
        kern,
        out_shape=(jax.ShapeDtypeStruct((b, N_EXPERTS, n), I32),
                   jax.ShapeDtypeStruct((b, N_EXPERTS, n), F32),
                   jax.ShapeDtypeStruct((b, n, LANES), I32),
                   jax.ShapeDtypeStruct((b, N_EXPERTS, LANES), I32)),
        grid=(b // nb,),
        in_specs=[em, _full((GROUP_W, GROUP_W))],
        out_specs=(em, em, pl.BlockSpec((nb, n, LANES), lambda i: (i, 0, 0)),
                   pl.BlockSpec((nb, N_EXPERTS, LANES), lambda i: (i, 0, 0))),
        compiler_params=_params("arbitrary"),
        name="router",
    )(logits, tri)


def _ffn_kernel(st_ref, xn_ref, pos_ref, gate_ref, wg32_ref, wu32_ref, wd32_ref, o_ref, wg_ref, wu_ref, wd_ref, xg_ref,
                *, bb, cap, n, tile):
    @pl.when(pl.program_id(1) == 0)
    def _():
        wg_ref[...] = wg32_ref[0, 0].astype(BF16)
        wu_ref[...] = wu32_ref[0, 0].astype(BF16)
        wd_ref[...] = wd32_ref[0, 0].astype(BF16)

    expert = pl.ds(pl.program_id(0), 1)

    def step(gather):
        slot = lax.broadcasted_iota(I32, (cap, n), 0)
        hits = [pos_ref[i, expert, :] == slot for i in range(bb)]
        gs = [jnp.sum(jnp.where(hits[i], gate_ref[i, expert, :], 0.0), axis=1, keepdims=True) for i in range(bb)]
        g = gs[0] if bb == 1 else jnp.concatenate(gs, axis=0)
        gather(hits)
        xg = xg_ref[...].astype(BF16)
        hid = (jax.nn.silu(_dot(xg, wg_ref[...])) * _dot(xg, wu_ref[...])).astype(BF16)
        o = _dot(hid, wd_ref[...]) * g
        for i in range(bb):
            o_ref[i, 0] = o[i * cap:(i + 1) * cap].astype(BF16)

    def gather_dense(hits):
        for i in range(bb):
            xg_ref[i * cap:(i + 1) * cap, :] = _dot(jnp.where(hits[i], 1.0, 0.0).astype(BF16), xn_ref[i])

    if cap <= SLOT_WINDOW:
        step(gather_dense)
        return

    nt = n // tile
    win, fits = [], True
    for i in range(bb):
        base = (pl.program_id(1) * bb + i) * nt * N_EXPERTS + pl.program_id(0)
        for t in range(nt):
            start = st_ref[base + t * N_EXPERTS]
            end = st_ref[base + (t + 1) * N_EXPERTS] if t + 1 < nt else cap
            a = jnp.minimum((start >> 4) << 4, cap - SLOT_WINDOW)
            win.append(a)
            fits = jnp.logical_and(fits, end - a <= SLOT_WINDOW)

    def gather_windowed(hits):
        del hits
        xg_ref[...] = jnp.zeros_like(xg_ref)
        wslot = lax.broadcasted_iota(I32, (SLOT_WINDOW, tile), 0)
        for i in range(bb):
            for t in range(nt):
                a = win[i * nt + t]
                p = pos_ref[i, expert, t * tile:(t + 1) * tile]
                onehot = jnp.where(p - a == wslot, 1.0, 0.0).astype(BF16)
                rows = pl.ds(pl.multiple_of(i * cap + a, 16), SLOT_WINDOW)
                xg_ref[rows, :] = xg_ref[rows, :] + _dot(onehot, xn_ref[i, t * tile:(t + 1) * tile, :])

    pl.when(fits)(lambda: step(gather_windowed))
    pl.when(jnp.logical_not(fits))(lambda: step(gather_dense))


def _expert_ffn(xn, pos, gate, starts, wg, wu, wd, layer, cap, bb, tile):
    b, n, d = xn.shape
    wspec = pl.BlockSpec((1, 1, d, d), lambda e, j, st: (layer, e, 0, 0))
    sel = pl.BlockSpec((bb, N_EXPERTS, n), lambda e, j, st: (j, 0, 0))
    kern = functools.partial(_ffn_kernel, bb=bb, cap=cap, n=n, tile=tile)
    return pl.pallas_call(
        kern,
        out_shape=jax.ShapeDtypeStruct((b, N_EXPERTS, cap, d), BF16),
        grid_spec=pltpu.PrefetchScalarGridSpec(
            num_scalar_prefetch=1,
            grid=(N_EXPERTS, b // bb),
            in_specs=[pl.BlockSpec((bb, n, d), lambda e, j, st: (j, 0, 0)), sel, sel, wspec, wspec, wspec],
            out_specs=pl.BlockSpec((bb, 1, cap, d), lambda e, j, st: (j, e, 0, 0)),
            scratch_shapes=[pltpu.VMEM((d, d), BF16)] * 3 + [pltpu.VMEM((bb * cap, d), F32)]),
        compiler_params=_params("arbitrary", "arbitrary"),
        name="expert_ffn",
    )(starts, xn, pos, gate, wg, wu, wd)


def _scatter_kernel(st_ref, x_ref, mod_ref, pt_ref, o_ref, out_ref, *, tn, cap, nt, sub):
    for s in range(sub):
        _scatter_tile(st_ref, x_ref.at[0, s * tn:(s + 1) * tn], mod_ref, pt_ref.at[0, s * tn:(s + 1) * tn], o_ref,
                      out_ref.at[0, s * tn:(s + 1) * tn], pl.program_id(1) * sub + s, tn=tn, cap=cap, nt=nt)


def _scatter_tile(st_ref, x_ref, mod_ref, pt_ref, o_ref, out_ref, t, *, tn, cap, nt):
    pos_t = pt_ref[...]
    g = mod_ref[0][:, 5 * D_MODEL:6 * D_MODEL]

    def dense():
        if cap % LANES == 0:
            slot = lax.broadcasted_iota(I32, (tn, cap), 1)
            onehot = jnp.concatenate(
                [jnp.where(pos_t[:, e:e + 1] == slot, 1.0, 0.0).astype(BF16) for e in range(N_EXPERTS)], axis=1)
        else:
            slot = lax.broadcasted_iota(I32, (tn, N_EXPERTS * cap), 1)
            acc = jnp.zeros((tn, N_EXPERTS * cap), F32)
            for e in range(N_EXPERTS):
                pe = pos_t[:, e:e + 1]
                acc = jnp.where(jnp.where(pe >= 0, pe + e * cap, -1) == slot, 1.0, acc)
            onehot = acc.astype(BF16)
        y = _dot(onehot, o_ref[0].reshape(N_EXPERTS * cap, D_MODEL))
        out_ref[...] = x_ref[...] + g * y

    if cap <= SLOT_WINDOW:
        dense()
        return

    base = (pl.program_id(0) * nt + t) * N_EXPERTS
    nxt = jnp.minimum(t + 1, nt - 1)
    nbase = (pl.program_id(0) * nt + nxt) * N_EXPERTS
    win, fits = [], True
    for e in range(N_EXPERTS):
        start = st_ref[base + e]
        end = jnp.where(t + 1 < nt, st_ref[nbase + e], cap)
        a = jnp.minimum((start >> 4) << 4, cap - SLOT_WINDOW)
        win.append(a)
        fits = jnp.logical_and(fits, end - a <= SLOT_WINDOW)

    @pl.when(fits)
    def _():
        slot = lax.broadcasted_iota(I32, (tn, SLOT_WINDOW), 1)
        onehot = jnp.concatenate(
            [jnp.where(pos_t[:, e:e + 1] - win[e] == slot, 1.0, 0.0).astype(BF16) for e in range(N_EXPERTS)], axis=1)
        rows = jnp.concatenate(
            [o_ref[0, e, pl.ds(pl.multiple_of(win[e], 16), SLOT_WINDOW), :] for e in range(N_EXPERTS)], axis=0)
        out_ref[...] = x_ref[...] + g * _dot(onehot, rows)

    pl.when(jnp.logical_not(fits))(dense)


def _scatter(x, mod, pos_t, starts, o, cap, tn, sub):
    b, n, d = x.shape
    nt = n // tn
    xs = pl.BlockSpec((1, sub * tn, d), lambda i, t, st: (i, t, 0))
    kern = functools.partial(_scatter_kernel, tn=tn, cap=cap, nt=nt, sub=sub)
    return pl.pallas_call(
        kern,
        out_shape=jax.ShapeDtypeStruct((b, n, d), F32),
        grid_spec=pltpu.PrefetchScalarGridSpec(
            num_scalar_prefetch=1,
            grid=(b, nt // sub),
            in_specs=[xs,
                      pl.BlockSpec((1, 1, 6 * d), lambda i, t, st: (i, 0, 0)),
                      pl.BlockSpec((1, sub * tn, LANES), lambda i, t, st: (i, t, 0)),
                      pl.BlockSpec((1, N_EXPERTS, cap, d), lambda i, t, st: (i, 0, 0, 0))],
            out_specs=xs),
        compiler_params=_params("arbitrary", "arbitrary"),
        name="scatter_add",
    )(starts, x, mod, pos_t, o)


@functools.lru_cache(maxsize=None)
def _np_consts():
    lane = np.arange(GROUP_W)
    seg32 = (lane[:, None] // DIFF_D == lane[None, :] // DIFF_D).astype(np.float32)
    seg64 = (lane[:, None] // HEAD_DIM == lane[None, :] // HEAD_DIM).astype(np.float32)
    ang = 2.0 * np.pi * ((lane[:, None] % HEAD_DIM) * (lane[None, :] % HEAD_DIM) % HEAD_DIM) / HEAD_DIM
    cc = np.cos(ang) * seg64
    ss = np.sin(ang) * seg64
    tri = (lane[:, None] <= lane[None, :]).astype(np.float32)
    return dict(seg32=seg32, seg64=seg64, cc=cc, ss=ss, tri=tri)


@functools.lru_cache(maxsize=None)
def _np_dft(n):
    idx = (np.arange(n, dtype=np.int64)[:, None] * np.arange(n, dtype=np.int64)[None, :]) % n
    ang = 2.0 * np.pi * idx.astype(np.float64) / n
    return np.cos(ang).astype(np.float32), np.sin(ang).astype(np.float32)


@functools.lru_cache(maxsize=None)
def _np_rope(n):
    half = DIFF_D // 2
    inv = 1.0 / (ROPE_BASE ** (np.arange(0, half, 2, dtype=np.float32) / half))
    t = np.arange(n)
    row = (t // GRID_W).astype(np.float32)[:, None] * inv
    col = (t % GRID_W).astype(np.float32)[:, None] * inv
    nf = inv.shape[0]
    d = np.arange(GROUP_W) % DIFF_D
    f = d % nf
    is_col = d >= half
    second = (d % half) >= nf
    ang = np.where(is_col[None, :], col[:, f], row[:, f])
    c = np.cos(ang).astype(np.float32)
    s = np.sin(ang).astype(np.float32)
    s = np.where(second[None, :], s, -s)
    return c, s


@functools.lru_cache(maxsize=None)
def _np_na_index(n_rows):
    n_steps = n_rows // NA_QROWS
    reps = [0, 1, 2, n_steps - 2, n_steps - 1]
    tq = NA_QROWS * GRID_W
    roff = np.zeros((len(reps), NA_QROWS, NA_WIN_ROWS), np.int32)
    valid = np.zeros((len(reps), tq, NA_WIN), bool)
    for ci, t in enumerate(reps):
        ws = int(np.clip(NA_QROWS * t - NA_ROWS // 2, 0, n_rows - NA_WIN_ROWS))
        qi = np.arange(tq)
        r = NA_QROWS * t + qi // GRID_W
        qcol = qi % GRID_W
        kk = np.arange(NA_WIN)
        krow = ws + kk // GRID_W
        kcol = kk % GRID_W
        rstart = np.clip(r - NA_ROWS // 2, 0, n_rows - NA_ROWS)
        wstart = np.clip(qcol - NA_COLS // 2, 0, GRID_W - NA_COLS)
        vr = (krow[None, :] >= rstart[:, None]) & (krow[None, :] < rstart[:, None] + NA_ROWS)
        vc = (kcol[None, :] >= wstart[:, None]) & (kcol[None, :] < wstart[:, None] + NA_COLS)
        valid[ci] = vr & vc
        rows_q = NA_QROWS * t + np.arange(NA_QROWS)
        rows_k = ws + np.arange(NA_WIN_ROWS)
        roff[ci] = np.clip(rows_k[None, :] - rows_q[:, None] + NA_ROWS - 1, 0, 2 * NA_ROWS - 2)
    return roff, valid


def _na_table_kernel(r_ref, mask_ref, o_ref, toep_ref, *, roff):
    n_off = 2 * NA_ROWS - 1
    for h in range(HEADS):
        for ro in range(n_off):
            row = jnp.broadcast_to(r_ref[0, h, ro:ro + 1, :], (GRID_W, LANES))
            toep_ref[h, ro] = pltpu.roll(row, 0, 1, stride=1, stride_axis=0)
    lane = lax.broadcasted_iota(I32, (GRID_W, LANES), 1)
    n_cls = roff.shape[0]
    for ci in range(n_cls):
        for h in range(HEADS):
            for rr in range(NA_QROWS):
                r0 = h * NA_QROWS * GRID_W + rr * GRID_W
                for kp in range(NA_WIN_ROWS // 2):
                    left = toep_ref[h, int(roff[ci, rr, 2 * kp])]
                    right = pltpu.roll(toep_ref[h, int(roff[ci, rr, 2 * kp + 1])], GRID_W, 1)
                    bias = jnp.where(lane < GRID_W, left, right) * LOG2E
                    o_ref[0, ci, r0:r0 + GRID_W, kp * LANES:(kp + 1) * LANES] = (
                        bias + mask_ref[ci, rr * GRID_W:(rr + 1) * GRID_W, kp * LANES:(kp + 1) * LANES])


def _na_tables(rpb_all, n_rows):
    roff, valid = _np_na_index(n_rows)
    n_cls = roff.shape[0]
    depth = rpb_all.shape[0]
    n_off = 2 * NA_ROWS - 1
    r = jnp.concatenate([rpb_all[..., NA_COLS - 1:], jnp.zeros(rpb_all.shape[:-1] + (LANES - 2 * NA_COLS + 1,), F32),
                         rpb_all[..., :NA_COLS - 1]], axis=-1).astype(F32)
    mask = jnp.asarray(np.where(valid, 0.0, NEG_INF).astype(np.float32))
    tq = NA_QROWS * GRID_W
    kern = functools.partial(_na_table_kernel, roff=roff)
    return pl.pallas_call(
        kern,
        out_shape=jax.ShapeDtypeStruct((depth, n_cls, HEADS * tq, NA_WIN), F32),
        grid=(depth,),
        in_specs=[pl.BlockSpec((1, HEADS, n_off, LANES), lambda l: (l, 0, 0, 0)),
                  _full((n_cls, tq, NA_WIN))],
        out_specs=pl.BlockSpec((1, n_cls, HEADS * tq, NA_WIN), lambda l: (l, 0, 0, 0)),
        scratch_shapes=[pltpu.VMEM((HEADS, n_off, GRID_W, LANES), F32)],
        compiler_params=_params("arbitrary"),
        name="na_bias_table",
    )(r, mask)


class _Tiles(NamedTuple):
    proj: int
    attn: int
    na_pairs: int
    moe: int
    ffn_samples: int


def _latent_tiles(n):
    return _Tiles(proj=min(n, 1024), attn=min(n, 512), na_pairs=min(8, n // (NA_QROWS * GRID_W)),
                  moe=min(n, 512), ffn_samples=2)
def _moe(x, xn, logits, mod, tri, wg, wu, wd, layer, bb, tn):
    n = x.shape[1]
    cap = CAPACITY_FACTOR * n // N_EXPERTS
    pos, gate, pos_t, starts = _route(logits, tri, cap, tn)
    starts = starts[:, :, :n // tn].transpose(0, 2, 1).reshape(-1)
    o = _expert_ffn(xn, pos, gate, starts, wg, wu, wd, layer, cap, bb, tn)
    return _scatter(x, mod, pos_t, starts, o, cap, tn, min(2, n // tn))


def kernel(x, c, ctx, c_ctx, ada_w, ada_b, norm1_g, norm2_g, w_in, w_out, head_out_g, sgu_w, sgu_b, diff_qn_g, diff_kn_g, diff_lambda, na_qn_g, na_kn_g, na_rpb, router_w, exp_w_gate, exp_w_up, exp_w_down):
    b, n, d = x.shape
    n_ctx = ctx.shape[1]
    npc = _np_consts()
    consts = {k: jnp.asarray(v, F32).astype(BF16) for k, v in npc.items()}
    seg64, tri = consts["seg64"], consts["tri"]
    cn, sn = (jnp.asarray(a, F32).astype(BF16) for a in _np_dft(n))
    cn_c, sn_c = (jnp.asarray(a, F32).astype(BF16) for a in _np_dft(n_ctx))
    rope = tuple(jnp.asarray(a, F32) for a in _np_rope(n))

    pad = (-(b + 1)) % 8
    c_rows = jnp.concatenate([c, c_ctx[None, :], jnp.zeros((pad, d), F32)], axis=0)
    mod_all = _modulation(c_rows, ada_w, ada_b)
    na_tables = _na_tables(na_rpb, n // GRID_W)

    xc = ctx
    for l in range(DEPTH):
        last = l == DEPTH - 1
        lam_init = 0.8 - 0.6 * math.exp(-0.3 * l)
        mod = mod_all[l, :b][:, None, :]
        mod_c = jnp.broadcast_to(mod_all[l, b][None, None, :], (b, 1, 6 * d))
        g1 = norm1_g[l][None, :]
        g2 = norm2_g[l][None, :]
        w_in_l = w_in[l].astype(BF16)
        w_out_l = w_out[l].astype(BF16)
        sguw = sgu_w[l].astype(BF16).transpose(1, 0, 2).reshape(CHUNK, HEADS * CHUNK)
        sgub = jnp.repeat(sgu_b[l].T, HEAD_DIM, axis=1)
        hg = head_out_g[l].reshape(4, GROUP_W)
        vec = jnp.stack([jnp.tile(diff_qn_g[l], GROUP_W // DIFF_D), jnp.tile(diff_kn_g[l], GROUP_W // DIFF_D),
                         jnp.tile(na_qn_g[l], HEADS), jnp.tile(na_kn_g[l], HEADS),
                         hg[0], hg[1], hg[2], hg[3],
                         jnp.broadcast_to(jnp.max(jnp.abs(na_rpb[l])) * LOG2E, (GROUP_W,))], axis=0).astype(F32)
        vec = jnp.pad(vec, ((0, VEC_ROWS - vec.shape[0]), (0, 0)))
        lam = diff_lambda[l].astype(F32)
        rwt = router_w[l].T.astype(BF16)
        experts = (exp_w_gate, exp_w_up, exp_w_down)

        tl = _latent_tiles(n)
        ya, zc, zs, qc, kct, vc, qd, kdt, vd = _in_proj(x, mod, g1, w_in_l, consts, sguw, sgub, vec, rope, tl.proj)
        if last:
            kct_c, vc_c, kdt_c, vd_c = _in_proj(
                xc, mod_c, g1, w_in_l, consts, sguw, sgub, vec, None, n_ctx, kv_only=True)
        else:
            ya_c, zc_c, zs_c, qc_c, kct_c, vc_c, qd_c, kdt_c, vd_c = _in_proj(
                xc, mod_c, g1, w_in_l, consts, sguw, sgub, vec, None, n_ctx)

        yb = _fourier(zc, zs, cn, sn, seg64, vec, tl.proj)
        yc = _attention(qc, [(kct, vc), (kct_c, vc_c)], lam, lam_init, seg64, vec, 6, 1.0 - lam_init,
                        tl.attn, tl.attn)
        yd = _na_attention(qd, kdt, vd, kdt_c, vd_c, na_tables, l, seg64, vec, tl.na_pairs)
        x, xn, logits = _out_proj(ya, yb, yc, yd, w_out_l, x, mod, g2, rwt, tl.proj)
        x = _moe(x, xn, logits, mod, tri, *experts, l, tl.ffn_samples, tl.moe)

        if not last:
            yb_c = _fourier(zc_c, zs_c, cn_c, sn_c, seg64, vec, n_ctx)
            yc_c = _attention(qc_c, [(kct_c, vc_c)], lam, lam_init, seg64, vec, 6, 1.0 - lam_init, n_ctx)
            yd_c = _attention(qd_c, [(kdt_c, vd_c)], None, 0.0, seg64, vec, 7, 1.0, n_ctx)
            xc, xn_c, logits_c = _out_proj(ya_c, yb_c, yc_c, yd_c, w_out_l, xc, mod_c, g2, rwt, n_ctx)
            xc = _moe(xc, xn_c, logits_c, mod_c, tri, *experts, l, b, n_ctx)
    return x
```

```python
import functools
import math
from typing import NamedTuple

import numpy as np
import jax
import jax.numpy as jnp
from jax import lax
from jax.experimental import pallas as pl
from jax.experimental.pallas import tpu as pltpu

F32 = jnp.float32
BF16 = jnp.bfloat16
I32 = jnp.int32

D_MODEL = 1024
DEPTH = 2
GRID_W = 64
LANES = 128
HEAD_DIM = 64
LOG2_HEAD_DIM = 6
GROUP_W = 256
HEADS = GROUP_W // HEAD_DIM
CHUNK = 128
DIFF_D = HEAD_DIM // 2
NA_ROWS = 8
NA_COLS = 16
N_EXPERTS = 16
CAPACITY_FACTOR = 2
ROPE_BASE = 10000.0
EPS = 1e-6
IN_W = 9 * GROUP_W
LOG2E = 1.4426950408889634

VMEM_LIMIT_BYTES = 56 * 1024 * 1024
NA_QROWS = 2
NA_WIN_ROWS = NA_ROWS + 2
NA_WIN = NA_WIN_ROWS * GRID_W
NA_HEAD_STACK = 1
NEG_INF = float("-inf")
VEC_ROWS = 16
SLOT_WINDOW = 128
SAFE_EXP2_BOUND = 48.0
BOUND_SLACK = 1.02


def _dot(a, b):
    return jnp.dot(a, b, preferred_element_type=F32)


def _params(*sem):
    return pltpu.CompilerParams(dimension_semantics=sem, vmem_limit_bytes=VMEM_LIMIT_BYTES)


def _full(shape):
    nd = len(shape)
    return pl.BlockSpec(shape, lambda *_: (0,) * nd)


def _seg_rms(x, seg, width):
    ss = _dot((x * x).astype(BF16), seg)
    return x * lax.rsqrt(ss * (1.0 / width) + EPS)


def _mod_kernel(c_ref, w_ref, b_ref, o_ref):
    s = jax.nn.silu(c_ref[...]).astype(BF16)
    o_ref[0] = _dot(s, w_ref[0].astype(BF16)) + b_ref[0]


def _modulation(c_rows, ada_w, ada_b):
    depth, d, w6 = ada_w.shape
    r = c_rows.shape[0]
    tn = 1024
    return pl.pallas_call(
        _mod_kernel,
        out_shape=jax.ShapeDtypeStruct((depth, r, w6), F32),
        grid=(depth, w6 // tn),
        in_specs=[
            pl.BlockSpec((r, d), lambda l, j: (0, 0)),
            pl.BlockSpec((1, d, tn), lambda l, j: (l, 0, j)),
            pl.BlockSpec((1, 1, tn), lambda l, j: (l, 0, j)),
        ],
        out_specs=pl.BlockSpec((1, r, tn), lambda l, j: (l, 0, j)),
        compiler_params=_params("arbitrary", "arbitrary"),
        name="modulation",
    )(c_rows, ada_w, ada_b.reshape(depth, 1, w6))


def _rope(x, c, s, lane):
    fwd = pltpu.roll(x, GROUP_W - 8, 1)
    bwd = pltpu.roll(x, 8, 1)
    partner = jnp.where((lane & 8) == 0, fwd, bwd)
    return x * c + partner * s


def _in_kernel(*refs, tm, use_rope, kv_only, c_scale, d_scale):
    (x_ref, mod_ref, g1_ref, w_ref, seg32_ref, seg64_ref, cc_ref, ss_ref, sguw_ref, sgub_ref, vec_ref) = refs[:11]
    rest = refs[11:]
    if use_rope:
        ropec_ref, ropes_ref = rest[:2]
        rest = rest[2:]
    if kv_only:
        kct_ref, vc_ref, kdt_ref, vd_ref = rest
    else:
        ya_ref, zc_ref, zs_ref, qc_ref, kct_ref, vc_ref, qd_ref, kdt_ref, vd_ref = rest

    x = x_ref[0]
    mod = mod_ref[0]
    sh = mod[:, 0:D_MODEL]
    sc = mod[:, D_MODEL:2 * D_MODEL]
    ms = jnp.mean(x * x, axis=-1, keepdims=True)
    h = x * lax.rsqrt(ms + EPS) * g1_ref[...]
    h = (h * (1.0 + sc) + sh).astype(BF16)
    lane = lax.broadcasted_iota(I32, (1, GROUP_W), 1)
    head = lane >> LOG2_HEAD_DIM
    seg32 = seg32_ref[...]
    seg64 = seg64_ref[...]
    vec = vec_ref[...]

    if kv_only:
        pc = _dot(h, w_ref[:, 4 * GROUP_W:6 * GROUP_W])
        k = _seg_rms(pc[:, 0:GROUP_W], seg32, DIFF_D) * vec[1:2]
        kct_ref[0] = k.T.astype(BF16)
        vc_ref[0] = pc[:, GROUP_W:2 * GROUP_W].astype(BF16)
        pd = _dot(h, w_ref[:, 7 * GROUP_W:9 * GROUP_W])
        kd = _seg_rms(pd[:, 0:GROUP_W], seg64, HEAD_DIM) * vec[3:4]
        kdt_ref[0] = kd.T.astype(BF16)
        vd_ref[0] = pd[:, GROUP_W:2 * GROUP_W].astype(BF16)
        return

    z = jax.nn.gelu(_dot(h, w_ref[:, 0:2 * GROUP_W]))
    u = z[:, 0:GROUP_W]
    vn = _seg_rms(z[:, GROUP_W:2 * GROUP_W], seg64, HEAD_DIM).astype(BF16)
    rows = []
    for c in range(tm // CHUNK):
        vch = vn[c * CHUNK:(c + 1) * CHUNK]
        stacked = jnp.concatenate([jnp.where(head == hh, vch, jnp.zeros_like(vch)) for hh in range(HEADS)], axis=0)
        rows.append(_dot(sguw_ref[...], stacked) + sgub_ref[...])
    ya = u * jnp.concatenate(rows, axis=0)
    ya_ref[0] = (_seg_rms(ya, seg64, HEAD_DIM) * vec[4:5]).astype(BF16)

    zb = _dot(h, w_ref[:, 2 * GROUP_W:3 * GROUP_W]).astype(BF16)
    zc_ref[0] = _dot(zb, cc_ref[...]).astype(BF16)
    zs_ref[0] = _dot(zb, ss_ref[...]).astype(BF16)

    pc = _dot(h, w_ref[:, 3 * GROUP_W:6 * GROUP_W])
    q = _seg_rms(pc[:, 0:GROUP_W], seg32, DIFF_D) * vec[0:1]
    k = _seg_rms(pc[:, GROUP_W:2 * GROUP_W], seg32, DIFF_D) * vec[1:2]
    if use_rope:
        rc = ropec_ref[...]
        rs = ropes_ref[...]
        q = _rope(q, rc, rs, lane)
        k = _rope(k, rc, rs, lane)
    qc_ref[0] = (q * c_scale).astype(BF16)
    kct_ref[0] = k.T.astype(BF16)
    vc_ref[0] = pc[:, 2 * GROUP_W:3 * GROUP_W].astype(BF16)

    pd = _dot(h, w_ref[:, 6 * GROUP_W:9 * GROUP_W])
    qd = _seg_rms(pd[:, 0:GROUP_W], seg64, HEAD_DIM) * vec[2:3]
    kd = _seg_rms(pd[:, GROUP_W:2 * GROUP_W], seg64, HEAD_DIM) * vec[3:4]
    qd_ref[0] = (qd * d_scale).astype(BF16)
    kdt_ref[0] = kd.T.astype(BF16)
    vd_ref[0] = pd[:, 2 * GROUP_W:3 * GROUP_W].astype(BF16)


def _in_proj(x, mod, g1, w_in, consts, sguw, sgub, vec, rope, tm, kv_only=False):
    b, n, d = x.shape
    use_rope = rope is not None
    tok = pl.BlockSpec((1, tm, GROUP_W), lambda i, t: (i, t, 0))
    tok_t = pl.BlockSpec((1, GROUP_W, tm), lambda i, t: (i, 0, t))
    in_specs = [
        pl.BlockSpec((1, tm, d), lambda i, t: (i, t, 0)),
        pl.BlockSpec((1, 1, 6 * d), lambda i, t: (i, 0, 0)),
        _full((1, d)),
        _full((d, IN_W)),
        _full((GROUP_W, GROUP_W)), _full((GROUP_W, GROUP_W)), _full((GROUP_W, GROUP_W)), _full((GROUP_W, GROUP_W)),
        _full((CHUNK, HEADS * CHUNK)),
        _full((CHUNK, GROUP_W)),
        _full((VEC_ROWS, GROUP_W)),
    ]
    args = [x, mod, g1, w_in, consts["seg32"], consts["seg64"], consts["cc"], consts["ss"], sguw, sgub, vec]
    if use_rope:
        in_specs += [pl.BlockSpec((tm, GROUP_W), lambda i, t: (t, 0))] * 2
        args += list(rope)
    sd = jax.ShapeDtypeStruct((b, n, GROUP_W), BF16)
    sdt = jax.ShapeDtypeStruct((b, GROUP_W, n), BF16)
    kern = functools.partial(
        _in_kernel, tm=tm, use_rope=use_rope, kv_only=kv_only,
        c_scale=(DIFF_D ** -0.5) * LOG2E, d_scale=(HEAD_DIM ** -0.5) * LOG2E)
    return pl.pallas_call(
        kern,
        out_shape=(sdt, sd, sdt, sd) if kv_only else (sd, sd, sd, sd, sdt, sd, sd, sdt, sd),
        grid=(b, n // tm),
        in_specs=in_specs,
        out_specs=(tok_t, tok, tok_t, tok) if kv_only else (tok, tok, tok, tok, tok_t, tok, tok, tok_t, tok),
        compiler_params=_params("arbitrary", "arbitrary"),
        name="in_proj",
    )(*args)


def _fourier_kernel(cn_ref, sn_ref, zc_ref, zs_ref, seg64_ref, vec_ref, o_ref, *, norm):
    y = (_dot(cn_ref[...], zc_ref[0]) - _dot(sn_ref[...], zs_ref[0])) * norm
    o_ref[0] = (_seg_rms(y, seg64_ref[...], HEAD_DIM) * vec_ref[5:6]).astype(BF16)


def _fourier(zc, zs, cn, sn, seg64, vec, tn):
    b, n, _ = zc.shape
    kern = functools.partial(_fourier_kernel, norm=1.0 / math.sqrt(n * HEAD_DIM))
    return pl.pallas_call(
        kern,
        out_shape=jax.ShapeDtypeStruct((b, n, GROUP_W), BF16),
        grid=(n // tn, b),
        in_specs=[
            pl.BlockSpec((tn, n), lambda t, i: (t, 0)),
            pl.BlockSpec((tn, n), lambda t, i: (t, 0)),
            pl.BlockSpec((1, n, GROUP_W), lambda t, i: (i, 0, 0)),
            pl.BlockSpec((1, n, GROUP_W), lambda t, i: (i, 0, 0)),
            _full((GROUP_W, GROUP_W)),
            _full((VEC_ROWS, GROUP_W)),
        ],
        out_specs=pl.BlockSpec((1, tn, GROUP_W), lambda t, i: (i, t, 0)),
        compiler_params=_params("arbitrary", "arbitrary"),
        name="fourier",
    )(cn, sn, zc, zs, seg64, vec)


def _attn_kernel(*refs, n_src, diff, lam_init, chunk, tq, vec_row, out_scale):
    q_ref = refs[0]
    rest = refs[1 + 2 * n_src:]
    if diff:
        lam_ref = rest[0]
        rest = rest[1:]
    seg64_ref, vec_ref, o_ref = rest

    q = q_ref[0]
    lane = lax.broadcasted_iota(I32, (1, GROUP_W), 1)
    if diff:
        lf = lam_ref[...]
        lam = (jnp.exp(jnp.sum(lf[0:1] * lf[1:2], axis=-1, keepdims=True))
               - jnp.exp(jnp.sum(lf[2:3] * lf[3:4], axis=-1, keepdims=True)) + lam_init)

    chunks = []
    for i in range(n_src):
        kt_ref, v_ref = refs[1 + 2 * i], refs[2 + 2 * i]
        nk = kt_ref.shape[2]
        chunks += [(kt_ref, v_ref, c0, min(chunk, nk - c0)) for c0 in range(0, nk, chunk)]

    width = DIFF_D if diff else HEAD_DIM
    gq = vec_ref[0:1] if diff else vec_ref[2:3]
    gk = vec_ref[1:2] if diff else vec_ref[3:4]
    bound = jnp.max(jnp.abs(gq)) * jnp.max(jnp.abs(gk)) * (math.sqrt(width) * LOG2E * BOUND_SLACK)

    def attend_bounded(sel):
        qm = jnp.where(sel, q, jnp.zeros_like(q))
        l = jnp.zeros((tq, 1), F32)
        acc = jnp.zeros((tq, GROUP_W), F32)
        for kt_ref, v_ref, c0, ck in chunks:
            p = jnp.exp2(_dot(qm, kt_ref[0, :, c0:c0 + ck]) - bound)
            l = l + jnp.sum(p, axis=-1, keepdims=True)
            acc = acc + _dot(p.astype(BF16), v_ref[0, c0:c0 + ck, :])
        return acc * (1.0 / l)

    def attend_online(sel):
        qm = jnp.where(sel, q, jnp.zeros_like(q))
        m = jnp.full((tq, 1), NEG_INF, F32)
        l = jnp.zeros((tq, 1), F32)
        acc = jnp.zeros((tq, GROUP_W), F32)
        for kt_ref, v_ref, c0, ck in chunks:
            s = _dot(qm, kt_ref[0, :, c0:c0 + ck])
            m_new = jnp.maximum(m, jnp.max(s, axis=-1, keepdims=True))
            alpha = jnp.exp2(m - m_new)
            p = jnp.exp2(s - m_new)
            l = alpha * l + jnp.sum(p, axis=-1, keepdims=True)
            acc = alpha * acc + _dot(p.astype(BF16), v_ref[0, c0:c0 + ck, :])
            m = m_new
        return acc * (1.0 / l)

    def head_out(attend, h):
        if diff:
            return (attend((lane >> (LOG2_HEAD_DIM - 1)) == 2 * h)
                    - lam * attend((lane >> (LOG2_HEAD_DIM - 1)) == 2 * h + 1))
        return attend((lane >> LOG2_HEAD_DIM) == h)

    def finish(out):
        y = _seg_rms(out, seg64_ref[...], HEAD_DIM) * vec_ref[vec_row:vec_row + 1]
        o_ref[0] = (y * out_scale).astype(BF16)

    def run_bounded():
        out = jnp.zeros((tq, GROUP_W), F32)
        for h in range(HEADS):
            out = jnp.where((lane >> LOG2_HEAD_DIM) == h, head_out(attend_bounded, h), out)
        finish(out)

    def run_online():
        def body(h, out):
            return jnp.where((lane >> LOG2_HEAD_DIM) == h, head_out(attend_online, h), out)

        finish(lax.fori_loop(0, HEADS, body, jnp.zeros((tq, GROUP_W), F32)))

    small = bound <= SAFE_EXP2_BOUND
    pl.when(small)(run_bounded)
    pl.when(jnp.logical_not(small))(run_online)


def _attention(q, srcs, lam, lam_init, seg64, vec, vec_row, out_scale, tq, chunk=768):
    b, nq, _ = q.shape
    diff = lam is not None
    in_specs = [pl.BlockSpec((1, tq, GROUP_W), lambda i, t: (i, t, 0))]
    args = [q]
    for kt, v in srcs:
        nk = v.shape[1]
        in_specs += [pl.BlockSpec((1, GROUP_W, nk), lambda i, t: (i, 0, 0)),
                     pl.BlockSpec((1, nk, GROUP_W), lambda i, t: (i, 0, 0))]
        args += [kt, v]
    if diff:
        in_specs.append(_full((4, DIFF_D)))
        args.append(lam)
    in_specs += [_full((GROUP_W, GROUP_W)), _full((VEC_ROWS, GROUP_W))]
    args += [seg64, vec]
    kern = functools.partial(_attn_kernel, n_src=len(srcs), diff=diff, lam_init=lam_init, chunk=chunk, tq=tq,
                             vec_row=vec_row, out_scale=out_scale)
    return pl.pallas_call(
        kern,
        out_shape=jax.ShapeDtypeStruct((b, nq, GROUP_W), BF16),
        grid=(b, nq // tq),
        in_specs=in_specs,
        out_specs=pl.BlockSpec((1, tq, GROUP_W), lambda i, t: (i, t, 0)),
        compiler_params=_params("arbitrary", "arbitrary"),
        name="diff_attention" if diff else "ctx_attention",
    )(*args)


def _na_kernel(q_ref, kt_ref, v_ref, ktc_ref, vc_ref, tab_ref, seg64_ref, vec_ref, o_ref, *, n_rows, pairs):
    n_steps = n_rows // NA_QROWS
    tq = NA_QROWS * GRID_W
    lane = lax.broadcasted_iota(I32, (1, GROUP_W), 1)
    head = lane >> LOG2_HEAD_DIM
    ktc = ktc_ref[0]
    vc = vc_ref[0]
    bound = (jnp.max(jnp.abs(vec_ref[2:3])) * jnp.max(jnp.abs(vec_ref[3:4])) * (math.sqrt(HEAD_DIM) * LOG2E * BOUND_SLACK)
             + jnp.max(vec_ref[8:9]))

    def run(bounded):
        for pi in range(pairs):
            t = pl.program_id(1) * pairs + pi
            ws = jnp.clip(NA_QROWS * t - NA_ROWS // 2, 0, n_rows - NA_WIN_ROWS)
            k0 = pl.multiple_of(ws * GRID_W, 128)
            tid = jnp.where(t < 2, t, jnp.where(t < n_steps - 2, 2, t - (n_steps - 5)))
            q = q_ref[0, pi * tq:(pi + 1) * tq, :]
            out = jnp.zeros((tq, GROUP_W), F32)
            for h0 in range(0, HEADS, NA_HEAD_STACK):
                hs = range(h0, h0 + NA_HEAD_STACK)
                qs = jnp.concatenate([jnp.where(head == hh, q, jnp.zeros_like(q)) for hh in hs], axis=0)
                s_loc = (_dot(qs, kt_ref[0, :, pl.ds(k0, NA_WIN)])
                         + tab_ref[0, tid, h0 * tq:(h0 + NA_HEAD_STACK) * tq, :])
                s_ctx = _dot(qs, ktc)
                if bounded:
                    m = bound
                else:
                    m = jnp.maximum(jnp.max(s_loc, axis=-1, keepdims=True), jnp.max(s_ctx, axis=-1, keepdims=True))
                p_loc = jnp.exp2(s_loc - m)
                p_ctx = jnp.exp2(s_ctx - m)
                l = jnp.sum(p_loc, axis=-1, keepdims=True) + jnp.sum(p_ctx, axis=-1, keepdims=True)
                o = _dot(p_loc.astype(BF16), v_ref[0, pl.ds(k0, NA_WIN), :]) + _dot(p_ctx.astype(BF16), vc)
                o = o * (1.0 / l)
                for i, hh in enumerate(hs):
                    out = jnp.where(head == hh, o[i * tq:(i + 1) * tq], out)
            o_ref[0, pi * tq:(pi + 1) * tq, :] = (
                _seg_rms(out, seg64_ref[...], HEAD_DIM) * vec_ref[7:8]).astype(BF16)

    small = bound <= SAFE_EXP2_BOUND
    pl.when(small)(lambda: run(True))
    pl.when(jnp.logical_not(small))(lambda: run(False))


def _na_attention(q, kt, v, ktc, vc, tables, layer, seg64, vec, pairs):
    b, n, _ = q.shape
    nc = vc.shape[1]
    tq = pairs * NA_QROWS * GRID_W
    kern = functools.partial(_na_kernel, n_rows=n // GRID_W, pairs=pairs)
    return pl.pallas_call(
        kern,
        out_shape=jax.ShapeDtypeStruct((b, n, GROUP_W), BF16),
        grid=(b, n // tq),
        in_specs=[
            pl.BlockSpec((1, tq, GROUP_W), lambda i, t: (i, t, 0)),
            pl.BlockSpec((1, GROUP_W, n), lambda i, t: (i, 0, 0)),
            pl.BlockSpec((1, n, GROUP_W), lambda i, t: (i, 0, 0)),
            pl.BlockSpec((1, GROUP_W, nc), lambda i, t: (i, 0, 0)),
            pl.BlockSpec((1, nc, GROUP_W), lambda i, t: (i, 0, 0)),
            pl.BlockSpec((1,) + tables.shape[1:], lambda i, t: (layer, 0, 0, 0)),
            _full((GROUP_W, GROUP_W)),
            _full((VEC_ROWS, GROUP_W)),
        ],
        out_specs=pl.BlockSpec((1, tq, GROUP_W), lambda i, t: (i, t, 0)),
        compiler_params=_params("arbitrary", "arbitrary"),
        name="neighbourhood_attention",
    )(q, kt, v, ktc, vc, tables, seg64, vec)


def _out_kernel(ya_ref, yb_ref, yc_ref, yd_ref, w_ref, x_ref, mod_ref, g2_ref, rwt_ref, o_ref, xn_ref, lg_ref):
    y = jnp.concatenate([ya_ref[0], yb_ref[0], yc_ref[0], yd_ref[0]], axis=-1)
    mod = mod_ref[0]
    x = x_ref[0] + mod[:, 2 * D_MODEL:3 * D_MODEL] * _dot(y, w_ref[...])
    o_ref[0] = x
    sh = mod[:, 3 * D_MODEL:4 * D_MODEL]
    sc = mod[:, 4 * D_MODEL:5 * D_MODEL]
    ms = jnp.mean(x * x, axis=-1, keepdims=True)
    h = x * lax.rsqrt(ms + EPS) * g2_ref[...]
    h = (h * (1.0 + sc) + sh).astype(BF16)
    xn_ref[0] = h
    lg_ref[0] = lax.dot_general(rwt_ref[...], h, (((1,), (1,)), ((), ())), preferred_element_type=F32)


def _out_proj(ya, yb, yc, yd, w_out, x, mod, g2, rwt, tm):
    b, n, d = x.shape
    tok = pl.BlockSpec((1, tm, GROUP_W), lambda i, t: (i, t, 0))
    xs = pl.BlockSpec((1, tm, d), lambda i, t: (i, t, 0))
    return pl.pallas_call(
        _out_kernel,
        out_shape=(jax.ShapeDtypeStruct((b, n, d), F32),
                   jax.ShapeDtypeStruct((b, n, d), BF16),
                   jax.ShapeDtypeStruct((b, N_EXPERTS, n), F32)),
        grid=(b, n // tm),
        in_specs=[tok, tok, tok, tok, _full((4 * GROUP_W, d)), xs,
                  pl.BlockSpec((1, 1, 6 * d), lambda i, t: (i, 0, 0)),
                  _full((1, d)), _full((N_EXPERTS, d))],
        out_specs=(xs, xs, pl.BlockSpec((1, N_EXPERTS, tm), lambda i, t: (i, 0, t))),
        compiler_params=_params("arbitrary", "arbitrary"),
        name="out_proj",
    )(ya, yb, yc, yd, w_out, x, mod, g2, rwt)


def _cumsum_excl(m, tri):
    n = m.shape[1]
    carry = jnp.zeros((m.shape[0], 1), F32)
    outs = []
    for j in range(n // GROUP_W):
        blk = m[:, j * GROUP_W:(j + 1) * GROUP_W]
        inc = _dot(blk.astype(BF16), tri)
        outs.append(inc - blk + carry)
        carry = carry + inc[:, GROUP_W - 1:GROUP_W]
    return jnp.concatenate(outs, axis=1)


def _route_kernel(lg_ref, tri_ref, pos_ref, gate_ref, post_ref, st_ref, *, n, cap, tile, nb):
    affs = []
    for i in range(nb):
        lg = lg_ref[i]
        e = jnp.exp(lg - jnp.max(lg, axis=0, keepdims=True))
        affs.append(e / jnp.sum(e, axis=0, keepdims=True))
        gate_ref[i] = affs[i]
    aff = affs[0] if nb == 1 else jnp.concatenate(affs, axis=0)

    def unresolved(state):
        lo, hi = state
        return jnp.max(jnp.where(lo < hi, 1.0, 0.0)) > 0.0

    def bisect(state):
        lo, hi = state
        mid = 0.5 * (lo + hi)
        mid = jnp.where(mid > lo, mid, hi)
        ge = aff >= mid
        cnt = jnp.sum(jnp.where(ge, 1.0, 0.0), axis=1, keepdims=True)
        least_ge = jnp.min(jnp.where(ge, aff, jnp.inf), axis=1, keepdims=True)
        most_lt = jnp.max(jnp.where(ge, NEG_INF, aff), axis=1, keepdims=True)
        up = cnt >= cap
        return jnp.where(up, least_ge, lo), jnp.where(up, hi, most_lt)

    thr, _ = lax.while_loop(unresolved, bisect, (jnp.min(aff, axis=1, keepdims=True),
                                                 jnp.max(aff, axis=1, keepdims=True)))
    gt = aff > thr
    eq = aff == thr
    need = cap - jnp.sum(jnp.where(gt, 1.0, 0.0), axis=1, keepdims=True)
    tri = tri_ref[...]
    rank_eq = _cumsum_excl(jnp.where(eq, 1.0, 0.0), tri)
    sel = jnp.where(gt, 1.0, jnp.where(eq, jnp.where(rank_eq < need, 1.0, 0.0), 0.0))
    cum = _cumsum_excl(sel, tri)
    pos = jnp.where(sel > 0.0, cum, -1.0)
    tile_lane = lax.broadcasted_iota(I32, (nb * N_EXPERTS, LANES), 1)
    starts = jnp.zeros((nb * N_EXPERTS, LANES), F32)
    for t in range(n // tile):
        starts = jnp.where(tile_lane == t, cum[:, t * tile:t * tile + 1], starts)
    for i in range(nb):
        rows = slice(i * N_EXPERTS, (i + 1) * N_EXPERTS)
        pos_ref[i] = pos[rows].astype(I32)
        st_ref[i] = starts[rows].astype(I32)
        padded = jnp.concatenate([pos[rows], jnp.full((LANES - N_EXPERTS, n), -1.0, F32)], axis=0)
        post_ref[i] = padded.T.astype(I32)


def _route(logits, tri, cap, tile, nb=2):
    b, _, n = logits.shape
    em = pl.BlockSpec((nb, N_EXPERTS, n), lambda i: (i, 0, 0))
    kern = functools.partial(_route_kernel, n=n, cap=cap, tile=tile, nb=nb)
    return pl.pallas_call(
        kern,
        out_shape=(jax.ShapeDtypeStruct((b, N_EXPERTS, n), I32),
                   jax.ShapeDtypeStruct((b, N_EXPERTS, n), F32),
                   jax.ShapeDtypeStruct((b, n, LANES), I32),
                   jax.ShapeDtypeStruct((b, N_EXPERTS, LANES), I32)),
        grid=(b // nb,),
        in_specs=[em, _full((GROUP_W, GROUP_W))],
        out_specs=(em, em, pl.BlockSpec((nb, n, LANES), lambda i: (i, 0, 0)),
                   pl.BlockSpec((nb, N_EXPERTS, LANES), lambda i: (i, 0, 0))),
        compiler_params=_params("arbitrary"),
        name="router",
    )(logits, tri)


def _ffn_kernel(st_ref, xn_ref, pos_ref, gate_ref, wg32_ref, wu32_ref, wd32_ref, o_ref, wg_ref, wu_ref, wd_ref, xg_ref,
                *, bb, cap, n, tile, stride):
    @pl.when(pl.program_id(1) == 0)
    def _():
        wg_ref[...] = wg32_ref[0, 0].astype(BF16)
        wu_ref[...] = wu32_ref[0, 0].astype(BF16)
        wd_ref[...] = wd32_ref[0, 0].astype(BF16)

    expert = pl.ds(pl.program_id(0), 1)

    def step(gather):
        slot = lax.broadcasted_iota(I32, (cap, n), 0)
        hits = [pos_ref[i, expert, :] == slot for i in range(bb)]
        gs = [jnp.sum(jnp.where(hits[i], gate_ref[i, expert, :], 0.0), axis=1, keepdims=True) for i in range(bb)]
        g = gs[0] if bb == 1 else jnp.concatenate(gs, axis=0)
        gather(hits)
        xg = xg_ref[...].astype(BF16)
        hid = (jax.nn.silu(_dot(xg, wg_ref[...])) * _dot(xg, wu_ref[...])).astype(BF16)
        o = _dot(hid, wd_ref[...]) * g
        for i in range(bb):
            o_ref[i, 0] = o[i * cap:(i + 1) * cap].astype(BF16)

    def gather_dense(hits):
        for i in range(bb):
            xg_ref[i * cap:(i + 1) * cap, :] = _dot(jnp.where(hits[i], 1.0, 0.0).astype(BF16), xn_ref[i])

    if cap <= SLOT_WINDOW:
        step(gather_dense)
        return

    nt = n // tile
    win, fits = [], True
    for i in range(bb):
        base = (pl.program_id(1) * bb + i) * nt * stride * N_EXPERTS + pl.program_id(0)
        for t in range(nt):
            start = st_ref[base + t * stride * N_EXPERTS]
            end = st_ref[base + (t + 1) * stride * N_EXPERTS] if t + 1 < nt else cap
            a = jnp.minimum((start >> 4) << 4, cap - SLOT_WINDOW)
            win.append(a)
            fits = jnp.logical_and(fits, end - a <= SLOT_WINDOW)

    def gather_windowed(hits):
        del hits
        xg_ref[...] = jnp.zeros_like(xg_ref)
        wslot = lax.broadcasted_iota(I32, (SLOT_WINDOW, tile), 0)
        for i in range(bb):
            for t in range(nt):
                a = win[i * nt + t]
                p = pos_ref[i, expert, t * tile:(t + 1) * tile]
                onehot = jnp.where(p - a == wslot, 1.0, 0.0).astype(BF16)
                rows = pl.ds(pl.multiple_of(i * cap + a, 16), SLOT_WINDOW)
                xg_ref[rows, :] = xg_ref[rows, :] + _dot(onehot, xn_ref[i, t * tile:(t + 1) * tile, :])

    pl.when(fits)(lambda: step(gather_windowed))
    pl.when(jnp.logical_not(fits))(lambda: step(gather_dense))


def _expert_ffn(xn, pos, gate, starts, wg, wu, wd, layer, cap, bb, tile, stride):
    b, n, d = xn.shape
    wspec = pl.BlockSpec((1, 1, d, d), lambda e, j, st: (layer, e, 0, 0))
    sel = pl.BlockSpec((bb, N_EXPERTS, n), lambda e, j, st: (j, 0, 0))
    kern = functools.partial(_ffn_kernel, bb=bb, cap=cap, n=n, tile=tile, stride=stride)
    return pl.pallas_call(
        kern,
        out_shape=jax.ShapeDtypeStruct((b, N_EXPERTS, cap, d), BF16),
        grid_spec=pltpu.PrefetchScalarGridSpec(
            num_scalar_prefetch=1,
            grid=(N_EXPERTS, b // bb),
            in_specs=[pl.BlockSpec((bb, n, d), lambda e, j, st: (j, 0, 0)), sel, sel, wspec, wspec, wspec],
            out_specs=pl.BlockSpec((bb, 1, cap, d), lambda e, j, st: (j, e, 0, 0)),
            scratch_shapes=[pltpu.VMEM((d, d), BF16)] * 3 + [pltpu.VMEM((bb * cap, d), F32)]),
        compiler_params=_params("arbitrary", "arbitrary"),
        name="expert_ffn",
    )(starts, xn, pos, gate, wg, wu, wd)


def _scatter_kernel(st_ref, x_ref, mod_ref, pt_ref, o_ref, out_ref, *, tn, cap, nt, sub, window):
    for s in range(sub):
        _scatter_tile(st_ref, x_ref.at[0, s * tn:(s + 1) * tn], mod_ref, pt_ref.at[0, s * tn:(s + 1) * tn], o_ref,
                      out_ref.at[0, s * tn:(s + 1) * tn], pl.program_id(1) * sub + s,
                      tn=tn, cap=cap, nt=nt, window=window)


def _scatter_tile(st_ref, x_ref, mod_ref, pt_ref, o_ref, out_ref, t, *, tn, cap, nt, window):
    pos_t = pt_ref[...]
    g = mod_ref[0][:, 5 * D_MODEL:6 * D_MODEL]

    def dense():
        if cap % LANES == 0:
            slot = lax.broadcasted_iota(I32, (tn, cap), 1)
            onehot = jnp.concatenate(
                [jnp.where(pos_t[:, e:e + 1] == slot, 1.0, 0.0).astype(BF16) for e in range(N_EXPERTS)], axis=1)
        else:
            slot = lax.broadcasted_iota(I32, (tn, N_EXPERTS * cap), 1)
            acc = jnp.zeros((tn, N_EXPERTS * cap), F32)
            for e in range(N_EXPERTS):
                pe = pos_t[:, e:e + 1]
                acc = jnp.where(jnp.where(pe >= 0, pe + e * cap, -1) == slot, 1.0, acc)
            onehot = acc.astype(BF16)
        y = _dot(onehot, o_ref[0].reshape(N_EXPERTS * cap, D_MODEL))
        out_ref[...] = x_ref[...] + g * y

    if cap <= window:
        dense()
        return

    base = (pl.program_id(0) * nt + t) * N_EXPERTS
    nxt = jnp.minimum(t + 1, nt - 1)
    nbase = (pl.program_id(0) * nt + nxt) * N_EXPERTS
    win, fits = [], True
    for e in range(N_EXPERTS):
        start = st_ref[base + e]
        end = jnp.where(t + 1 < nt, st_ref[nbase + e], cap)
        a = jnp.minimum((start >> 4) << 4, cap - window)
        win.append(a)
        fits = jnp.logical_and(fits, end - a <= window)

    @pl.when(fits)
    def _():
        per = LANES // window
        lane = lax.broadcasted_iota(I32, (tn, LANES), 1)
        groups = []
        for e0 in range(0, N_EXPERTS, per):
            target = jnp.full((tn, LANES), -1, I32)
            for i in range(per):
                rel = pos_t[:, e0 + i:e0 + i + 1] - win[e0 + i]
                here = (lane >> (window.bit_length() - 1)) == i if per > 1 else None
                shifted = jnp.where(rel >= 0, rel + i * window, -1)
                target = shifted if here is None else jnp.where(here, shifted, target)
            groups.append(jnp.where(target == lane, 1.0, 0.0).astype(BF16))
        onehot = jnp.concatenate(groups, axis=1)
        rows = jnp.concatenate(
            [o_ref[0, e, pl.ds(pl.multiple_of(win[e], 16), window), :] for e in range(N_EXPERTS)], axis=0)
        out_ref[...] = x_ref[...] + g * _dot(onehot, rows)

    pl.when(jnp.logical_not(fits))(dense)


def _scatter(x, mod, pos_t, starts, o, cap, tn, window, sub):
    b, n, d = x.shape
    nt = n // tn
    xs = pl.BlockSpec((1, sub * tn, d), lambda i, t, st: (i, t, 0))
    kern = functools.partial(_scatter_kernel, tn=tn, cap=cap, nt=nt, sub=sub, window=window)
    return pl.pallas_call(
        kern,
        out_shape=jax.ShapeDtypeStruct((b, n, d), F32),
        grid_spec=pltpu.PrefetchScalarGridSpec(
            num_scalar_prefetch=1,
            grid=(b, nt // sub),
            in_specs=[xs,
                      pl.BlockSpec((1, 1, 6 * d), lambda i, t, st: (i, 0, 0)),
                      pl.BlockSpec((1, sub * tn, LANES), lambda i, t, st: (i, t, 0)),
                      pl.BlockSpec((1, N_EXPERTS, cap, d), lambda i, t, st: (i, 0, 0, 0))],
            out_specs=xs),
        compiler_params=_params("arbitrary", "arbitrary"),
        name="scatter_add",
    )(starts, x, mod, pos_t, o)


@functools.lru_cache(maxsize=None)
def _np_consts():
    lane = np.arange(GROUP_W)
    seg32 = (lane[:, None] // DIFF_D == lane[None, :] // DIFF_D).astype(np.float32)
    seg64 = (lane[:, None] // HEAD_DIM == lane[None, :] // HEAD_DIM).astype(np.float32)
    ang = 2.0 * np.pi * ((lane[:, None] % HEAD_DIM) * (lane[None, :] % HEAD_DIM) % HEAD_DIM) / HEAD_DIM
    cc = np.cos(ang) * seg64
    ss = np.sin(ang) * seg64
    tri = (lane[:, None] <= lane[None, :]).astype(np.float32)
    return dict(seg32=seg32, seg64=seg64, cc=cc, ss=ss, tri=tri)


@functools.lru_cache(maxsize=None)
def _np_dft(n):
    idx = (np.arange(n, dtype=np.int64)[:, None] * np.arange(n, dtype=np.int64)[None, :]) % n
    ang = 2.0 * np.pi * idx.astype(np.float64) / n
    return np.cos(ang).astype(np.float32), np.sin(ang).astype(np.float32)


@functools.lru_cache(maxsize=None)
def _np_rope(n):
    half = DIFF_D // 2
    inv = 1.0 / (ROPE_BASE ** (np.arange(0, half, 2, dtype=np.float32) / half))
    t = np.arange(n)
    row = (t // GRID_W).astype(np.float32)[:, None] * inv
    col = (t % GRID_W).astype(np.float32)[:, None] * inv
    nf = inv.shape[0]
    d = np.arange(GROUP_W) % DIFF_D
    f = d % nf
    is_col = d >= half
    second = (d % half) >= nf
    ang = np.where(is_col[None, :], col[:, f], row[:, f])
    c = np.cos(ang).astype(np.float32)
    s = np.sin(ang).astype(np.float32)
    s = np.where(second[None, :], s, -s)
    return c, s


@functools.lru_cache(maxsize=None)
def _np_na_index(n_rows):
    n_steps = n_rows // NA_QROWS
    reps = [0, 1, 2, n_steps - 2, n_steps - 1]
    tq = NA_QROWS * GRID_W
    roff = np.zeros((len(reps), NA_QROWS, NA_WIN_ROWS), np.int32)
    valid = np.zeros((len(reps), tq, NA_WIN), bool)
    for ci, t in enumerate(reps):
        ws = int(np.clip(NA_QROWS * t - NA_ROWS // 2, 0, n_rows - NA_WIN_ROWS))
        qi = np.arange(tq)
        r = NA_QROWS * t + qi // GRID_W
        qcol = qi % GRID_W
        kk = np.arange(NA_WIN)
        krow = ws + kk // GRID_W
        kcol = kk % GRID_W
        rstart = np.clip(r - NA_ROWS // 2, 0, n_rows - NA_ROWS)
        wstart = np.clip(qcol - NA_COLS // 2, 0, GRID_W - NA_COLS)
        vr = (krow[None, :] >= rstart[:, None]) & (krow[None, :] < rstart[:, None] + NA_ROWS)
        vc = (kcol[None, :] >= wstart[:, None]) & (kcol[None, :] < wstart[:, None] + NA_COLS)
        valid[ci] = vr & vc
        rows_q = NA_QROWS * t + np.arange(NA_QROWS)
        rows_k = ws + np.arange(NA_WIN_ROWS)
        roff[ci] = np.clip(rows_k[None, :] - rows_q[:, None] + NA_ROWS - 1, 0, 2 * NA_ROWS - 2)
    return roff, valid


def _na_table_kernel(r_ref, mask_ref, o_ref, toep_ref, *, roff):
    n_off = 2 * NA_ROWS - 1
    for h in range(HEADS):
        for ro in range(n_off):
            row = jnp.broadcast_to(r_ref[0, h, ro:ro + 1, :], (GRID_W, LANES))
            toep_ref[h, ro] = pltpu.roll(row, 0, 1, stride=1, stride_axis=0)
    lane = lax.broadcasted_iota(I32, (GRID_W, LANES), 1)
    n_cls = roff.shape[0]
    for ci in range(n_cls):
        for h in range(HEADS):
            for rr in range(NA_QROWS):
                r0 = h * NA_QROWS * GRID_W + rr * GRID_W
                for kp in range(NA_WIN_ROWS // 2):
                    left = toep_ref[h, int(roff[ci, rr, 2 * kp])]
                    right = pltpu.roll(toep_ref[h, int(roff[ci, rr, 2 * kp + 1])], GRID_W, 1)
                    bias = jnp.where(lane < GRID_W, left, right) * LOG2E
                    o_ref[0, ci, r0:r0 + GRID_W, kp * LANES:(kp + 1) * LANES] = (
                        bias + mask_ref[ci, rr * GRID_W:(rr + 1) * GRID_W, kp * LANES:(kp + 1) * LANES])


def _na_tables(rpb_all, n_rows):
    roff, valid = _np_na_index(n_rows)
    n_cls = roff.shape[0]
    depth = rpb_all.shape[0]
    n_off = 2 * NA_ROWS - 1
    r = jnp.concatenate([rpb_all[..., NA_COLS - 1:], jnp.zeros(rpb_all.shape[:-1] + (LANES - 2 * NA_COLS + 1,), F32),
                         rpb_all[..., :NA_COLS - 1]], axis=-1).astype(F32)
    mask = jnp.asarray(np.where(valid, 0.0, NEG_INF).astype(np.float32))
    tq = NA_QROWS * GRID_W
    kern = functools.partial(_na_table_kernel, roff=roff)
    return pl.pallas_call(
        kern,
        out_shape=jax.ShapeDtypeStruct((depth, n_cls, HEADS * tq, NA_WIN), F32),
        grid=(depth,),
        in_specs=[pl.BlockSpec((1, HEADS, n_off, LANES), lambda l: (l, 0, 0, 0)),
                  _full((n_cls, tq, NA_WIN))],
        out_specs=pl.BlockSpec((1, n_cls, HEADS * tq, NA_WIN), lambda l: (l, 0, 0, 0)),
        scratch_shapes=[pltpu.VMEM((HEADS, n_off, GRID_W, LANES), F32)],
        compiler_params=_params("arbitrary"),
        name="na_bias_table",
    )(r, mask)


class _Tiles(NamedTuple):
    proj: int
    attn: int
    na_pairs: int
    moe: int
    ffn_samples: int


def _latent_tiles(n):
    return _Tiles(proj=min(n, 1024), attn=min(n, 512), na_pairs=min(8, n // (NA_QROWS * GRID_W)),
                  moe=min(n, 512), ffn_samples=2)
def _moe(x, xn, logits, mod, tri, wg, wu, wd, layer, bb, tn):
    n = x.shape[1]
    cap = CAPACITY_FACTOR * n // N_EXPERTS
    stride = 2 if cap > SLOT_WINDOW else 1
    ts = tn // stride
    pos, gate, pos_t, starts = _route(logits, tri, cap, ts)
    starts = starts[:, :, :n // ts].transpose(0, 2, 1).reshape(-1)
    o = _expert_ffn(xn, pos, gate, starts, wg, wu, wd, layer, cap, bb, tn, stride)
    return _scatter(x, mod, pos_t, starts, o, cap, ts, SLOT_WINDOW // stride, min(4, n // ts))


def kernel(x, c, ctx, c_ctx, ada_w, ada_b, norm1_g, norm2_g, w_in, w_out, head_out_g, sgu_w, sgu_b, diff_qn_g, diff_kn_g, diff_lambda, na_qn_g, na_kn_g, na_rpb, router_w, exp_w_gate, exp_w_up, exp_w_down):
    b, n, d = x.shape
    n_ctx = ctx.shape[1]
    npc = _np_consts()
    consts = {k: jnp.asarray(v, F32).astype(BF16) for k, v in npc.items()}
    seg64, tri = consts["seg64"], consts["tri"]
    cn, sn = (jnp.asarray(a, F32).astype(BF16) for a in _np_dft(n))
    cn_c, sn_c = (jnp.asarray(a, F32).astype(BF16) for a in _np_dft(n_ctx))
    rope = tuple(jnp.asarray(a, F32) for a in _np_rope(n))

    pad = (-(b + 1)) % 8
    c_rows = jnp.concatenate([c, c_ctx[None, :], jnp.zeros((pad, d), F32)], axis=0)
    mod_all = _modulation(c_rows, ada_w, ada_b)
    na_tables = _na_tables(na_rpb, n // GRID_W)

    xc = ctx
    for l in range(DEPTH):
        last = l == DEPTH - 1
        lam_init = 0.8 - 0.6 * math.exp(-0.3 * l)
        mod = mod_all[l, :b][:, None, :]
        mod_c = jnp.broadcast_to(mod_all[l, b][None, None, :], (b, 1, 6 * d))
        g1 = norm1_g[l][None, :]
        g2 = norm2_g[l][None, :]
        w_in_l = w_in[l].astype(BF16)
        w_out_l = w_out[l].astype(BF16)
        sguw = sgu_w[l].astype(BF16).transpose(1, 0, 2).reshape(CHUNK, HEADS * CHUNK)
        sgub = jnp.repeat(sgu_b[l].T, HEAD_DIM, axis=1)
        hg = head_out_g[l].reshape(4, GROUP_W)
        vec = jnp.stack([jnp.tile(diff_qn_g[l], GROUP_W // DIFF_D), jnp.tile(diff_kn_g[l], GROUP_W // DIFF_D),
                         jnp.tile(na_qn_g[l], HEADS), jnp.tile(na_kn_g[l], HEADS),
                         hg[0], hg[1], hg[2], hg[3],
                         jnp.broadcast_to(jnp.max(jnp.abs(na_rpb[l])) * LOG2E, (GROUP_W,))], axis=0).astype(F32)
        vec = jnp.pad(vec, ((0, VEC_ROWS - vec.shape[0]), (0, 0)))
        lam = diff_lambda[l].astype(F32)
        rwt = router_w[l].T.astype(BF16)
        experts = (exp_w_gate, exp_w_up, exp_w_down)

        tl = _latent_tiles(n)
        ya, zc, zs, qc, kct, vc, qd, kdt, vd = _in_proj(x, mod, g1, w_in_l, consts, sguw, sgub, vec, rope, tl.proj)
        if last:
            kct_c, vc_c, kdt_c, vd_c = _in_proj(
                xc, mod_c, g1, w_in_l, consts, sguw, sgub, vec, None, n_ctx, kv_only=True)
        else:
            ya_c, zc_c, zs_c, qc_c, kct_c, vc_c, qd_c, kdt_c, vd_c = _in_proj(
                xc, mod_c, g1, w_in_l, consts, sguw, sgub, vec, None, n_ctx)

        yb = _fourier(zc, zs, cn, sn, seg64, vec, tl.proj)
        yc = _attention(qc, [(kct, vc), (kct_c, vc_c)], lam, lam_init, seg64, vec, 6, 1.0 - lam_init,
                        tl.attn, tl.attn)
        yd = _na_attention(qd, kdt, vd, kdt_c, vd_c, na_tables, l, seg64, vec, tl.na_pairs)
        x, xn, logits = _out_proj(ya, yb, yc, yd, w_out_l, x, mod, g2, rwt, tl.proj)
        x = _moe(x, xn, logits, mod, tri, *experts, l, tl.ffn_samples, tl.moe)

        if not last:
            yb_c = _fourier(zc_c, zs_c, cn_c, sn_c, seg64, vec, n_ctx)
            yc_c = _attention(qc_c, [(kct_c, vc_c)], lam, lam_init, seg64, vec, 6, 1.0 - lam_init, n_ctx)
            yd_c = _attention(qd_c, [(kdt_c, vd_c)], None, 0.0, seg64, vec, 7, 1.0, n_ctx)
            xc, xn_c, logits_c = _out_proj(ya_c, yb_c, yc_c, yd_c, w_out_l, xc, mod_c, g2, rwt, n_ctx)
            xc = _moe(xc, xn_c, logits_c, mod_c, tri, *experts, l, b, n_ctx)
    return x
```

```python
import functools
import math
from typing import NamedTuple

import numpy as np
import jax
import jax.numpy as jnp
from jax import lax
from jax.experimental import pallas as pl
from jax.experimental.pallas import tpu as pltpu

F32 = jnp.float32
BF16 = jnp.bfloat16
I32 = jnp.int32

D_MODEL = 1024
DEPTH = 2
GRID_W = 64
LANES = 128
HEAD_DIM = 64
LOG2_HEAD_DIM = 6
GROUP_W = 256
HEADS = GROUP_W // HEAD_DIM
CHUNK = 128
DIFF_D = HEAD_DIM // 2
NA_ROWS = 8
NA_COLS = 16
N_EXPERTS = 16
CAPACITY_FACTOR = 2
ROPE_BASE = 10000.0
EPS = 1e-6
IN_W = 9 * GROUP_W
LOG2E = 1.4426950408889634

VMEM_LIMIT_BYTES = 56 * 1024 * 1024
NA_QROWS = 2
NA_WIN_ROWS = NA_ROWS + 2
NA_WIN = NA_WIN_ROWS * GRID_W
NA_HEAD_STACK = 1
NEG_INF = float("-inf")
VEC_ROWS = 16
SLOT_WINDOW = 128
SAFE_EXP2_BOUND = 48.0
BOUND_SLACK = 1.02


def _dot(a, b):
    return jnp.dot(a, b, preferred_element_type=F32)


def _params(*sem):
    return pltpu.CompilerParams(dimension_semantics=sem, vmem_limit_bytes=VMEM_LIMIT_BYTES)


def _full(shape):
    nd = len(shape)
    return pl.BlockSpec(shape, lambda *_: (0,) * nd)


def _seg_rms(x, seg, width):
    ss = _dot((x * x).astype(BF16), seg)
    return x * lax.rsqrt(ss * (1.0 / width) + EPS)


def _mod_kernel(c_ref, w_ref, b_ref, o_ref):
    s = jax.nn.silu(c_ref[...]).astype(BF16)
    o_ref[0] = _dot(s, w_ref[0].astype(BF16)) + b_ref[0]


def _modulation(c_rows, ada_w, ada_b):
    depth, d, w6 = ada_w.shape
    r = c_rows.shape[0]
    tn = 1024
    return pl.pallas_call(
        _mod_kernel,
        out_shape=jax.ShapeDtypeStruct((depth, r, w6), F32),
        grid=(depth, w6 // tn),
        in_specs=[
            pl.BlockSpec((r, d), lambda l, j: (0, 0)),
            pl.BlockSpec((1, d, tn), lambda l, j: (l, 0, j)),
            pl.BlockSpec((1, 1, tn), lambda l, j: (l, 0, j)),
        ],
        out_specs=pl.BlockSpec((1, r, tn), lambda l, j: (l, 0, j)),
        compiler_params=_params("arbitrary", "arbitrary"),
        name="modulation",
    )(c_rows, ada_w, ada_b.reshape(depth, 1, w6))


def _rope(x, c, s, lane):
    fwd = pltpu.roll(x, GROUP_W - 8, 1)
    bwd = pltpu.roll(x, 8, 1)
    partner = jnp.where((lane & 8) == 0, fwd, bwd)
    return x * c + partner * s


def _in_kernel(*refs, tm, use_rope, kv_only, c_scale, d_scale):
    (x_ref, mod_ref, g1_ref, w_ref, seg32_ref, seg64_ref, cc_ref, ss_ref, sguw_ref, sgub_ref, vec_ref) = refs[:11]
    rest = refs[11:]
    if use_rope:
        ropec_ref, ropes_ref = rest[:2]
        rest = rest[2:]
    if kv_only:
        kct_ref, vc_ref, kdt_ref, vd_ref = rest
    else:
        ya_ref, zc_ref, zs_ref, qc_ref, kct_ref, vc_ref, qd_ref, kdt_ref, vd_ref = rest

    x = x_ref[0]
    mod = mod_ref[0]
    sh = mod[:, 0:D_MODEL]
    sc = mod[:, D_MODEL:2 * D_MODEL]
    ms = jnp.mean(x * x, axis=-1, keepdims=True)
    h = x * lax.rsqrt(ms + EPS) * g1_ref[...]
    h = (h * (1.0 + sc) + sh).astype(BF16)
    lane = lax.broadcasted_iota(I32, (1, GROUP_W), 1)
    head = lane >> LOG2_HEAD_DIM
    seg32 = seg32_ref[...]
    seg64 = seg64_ref[...]
    vec = vec_ref[...]

    if kv_only:
        pc = _dot(h, w_ref[:, 4 * GROUP_W:6 * GROUP_W])
        k = _seg_rms(pc[:, 0:GROUP_W], seg32, DIFF_D) * vec[1:2]
        kct_ref[0] = k.T.astype(BF16)
        vc_ref[0] = pc[:, GROUP_W:2 * GROUP_W].astype(BF16)
        pd = _dot(h, w_ref[:, 7 * GROUP_W:9 * GROUP_W])
        kd = _seg_rms(pd[:, 0:GROUP_W], seg64, HEAD_DIM) * vec[3:4]
        kdt_ref[0] = kd.T.astype(BF16)
        vd_ref[0] = pd[:, GROUP_W:2 * GROUP_W].astype(BF16)
        return

    z = jax.nn.gelu(_dot(h, w_ref[:, 0:2 * GROUP_W]))
    u = z[:, 0:GROUP_W]
    vn = _seg_rms(z[:, GROUP_W:2 * GROUP_W], seg64, HEAD_DIM).astype(BF16)
    rows = []
    for c in range(tm // CHUNK):
        vch = vn[c * CHUNK:(c + 1) * CHUNK]
        stacked = jnp.concatenate([jnp.where(head == hh, vch, jnp.zeros_like(vch)) for hh in range(HEADS)], axis=0)
        rows.append(_dot(sguw_ref[...], stacked) + sgub_ref[...])
    ya = u * jnp.concatenate(rows, axis=0)
    ya_ref[0] = (_seg_rms(ya, seg64, HEAD_DIM) * vec[4:5]).astype(BF16)

    zb = _dot(h, w_ref[:, 2 * GROUP_W:3 * GROUP_W]).astype(BF16)
    zc_ref[0] = _dot(zb, cc_ref[...]).astype(BF16)
    zs_ref[0] = _dot(zb, ss_ref[...]).astype(BF16)

    pc = _dot(h, w_ref[:, 3 * GROUP_W:6 * GROUP_W])
    q = _seg_rms(pc[:, 0:GROUP_W], seg32, DIFF_D) * vec[0:1]
    k = _seg_rms(pc[:, GROUP_W:2 * GROUP_W], seg32, DIFF_D) * vec[1:2]
    if use_rope:
        rc = ropec_ref[...]
        rs = ropes_ref[...]
        q = _rope(q, rc, rs, lane)
        k = _rope(k, rc, rs, lane)
    qc_ref[0] = (q * c_scale).astype(BF16)
    kct_ref[0] = k.T.astype(BF16)
    vc_ref[0] = pc[:, 2 * GROUP_W:3 * GROUP_W].astype(BF16)

    pd = _dot(h, w_ref[:, 6 * GROUP_W:9 * GROUP_W])
    qd = _seg_rms(pd[:, 0:GROUP_W], seg64, HEAD_DIM) * vec[2:3]
    kd = _seg_rms(pd[:, GROUP_W:2 * GROUP_W], seg64, HEAD_DIM) * vec[3:4]
    qd_ref[0] = (qd * d_scale).astype(BF16)
    kdt_ref[0] = kd.T.astype(BF16)
    vd_ref[0] = pd[:, 2 * GROUP_W:3 * GROUP_W].astype(BF16)


def _in_proj(x, mod, g1, w_in, consts, sguw, sgub, vec, rope, tm, kv_only=False):
    b, n, d = x.shape
    use_rope = rope is not None
    tok = pl.BlockSpec((1, tm, GROUP_W), lambda i, t: (i, t, 0))
    tok_t = pl.BlockSpec((1, GROUP_W, tm), lambda i, t: (i, 0, t))
    in_specs = [
        pl.BlockSpec((1, tm, d), lambda i, t: (i, t, 0)),
        pl.BlockSpec((1, 1, 6 * d), lambda i, t: (i, 0, 0)),
        _full((1, d)),
        _full((d, IN_W)),
        _full((GROUP_W, GROUP_W)), _full((GROUP_W, GROUP_W)), _full((GROUP_W, GROUP_W)), _full((GROUP_W, GROUP_W)),
        _full((CHUNK, HEADS * CHUNK)),
        _full((CHUNK, GROUP_W)),
        _full((VEC_ROWS, GROUP_W)),
    ]
    args = [x, mod, g1, w_in, consts["seg32"], consts["seg64"], consts["cc"], consts["ss"], sguw, sgub, vec]
    if use_rope:
        in_specs += [pl.BlockSpec((tm, GROUP_W), lambda i, t: (t, 0))] * 2
        args += list(rope)
    sd = jax.ShapeDtypeStruct((b, n, GROUP_W), BF16)
    sdt = jax.ShapeDtypeStruct((b, GROUP_W, n), BF16)
    kern = functools.partial(
        _in_kernel, tm=tm, use_rope=use_rope, kv_only=kv_only,
        c_scale=(DIFF_D ** -0.5) * LOG2E, d_scale=(HEAD_DIM ** -0.5) * LOG2E)
    return pl.pallas_call(
        kern,
        out_shape=(sdt, sd, sdt, sd) if kv_only else (sd, sd, sd, sd, sdt, sd, sd, sdt, sd),
        grid=(b, n // tm),
        in_specs=in_specs,
        out_specs=(tok_t, tok, tok_t, tok) if kv_only else (tok, tok, tok, tok, tok_t, tok, tok, tok_t, tok),
        compiler_params=_params("arbitrary", "arbitrary"),
        name="in_proj",
    )(*args)


def _fourier_kernel(cn_ref, sn_ref, zc_ref, zs_ref, seg64_ref, vec_ref, o_ref, *, norm):
    y = (_dot(cn_ref[...], zc_ref[0]) + _dot(sn_ref[...], zs_ref[0])) * norm
    o_ref[0] = (_seg_rms(y, seg64_ref[...], HEAD_DIM) * vec_ref[5:6]).astype(BF16)


def _fourier(zc, zs, cn, sn, seg64, vec, tn):
    b, n, _ = zc.shape
    kern = functools.partial(_fourier_kernel, norm=1.0 / math.sqrt(n * HEAD_DIM))
    return pl.pallas_call(
        kern,
        out_shape=jax.ShapeDtypeStruct((b, n, GROUP_W), BF16),
        grid=(n // tn, b),
        in_specs=[
            pl.BlockSpec((tn, n), lambda t, i: (t, 0)),
            pl.BlockSpec((tn, n), lambda t, i: (t, 0)),
            pl.BlockSpec((1, n, GROUP_W), lambda t, i: (i, 0, 0)),
            pl.BlockSpec((1, n, GROUP_W), lambda t, i: (i, 0, 0)),
            _full((GROUP_W, GROUP_W)),
            _full((VEC_ROWS, GROUP_W)),
        ],
        out_specs=pl.BlockSpec((1, tn, GROUP_W), lambda t, i: (i, t, 0)),
        compiler_params=_params("arbitrary", "arbitrary"),
        name="fourier",
    )(cn, sn, zc, zs, seg64, vec)


def _attn_kernel(*refs, n_src, diff, lam_init, chunk, tq, vec_row, out_scale):
    q_ref = refs[0]
    rest = refs[1 + 2 * n_src:]
    if diff:
        lam_ref = rest[0]
        rest = rest[1:]
    seg64_ref, vec_ref, o_ref = rest

    q = q_ref[0]
    lane = lax.broadcasted_iota(I32, (1, GROUP_W), 1)
    if diff:
        lf = lam_ref[...]
        lam = (jnp.exp(jnp.sum(lf[0:1] * lf[1:2], axis=-1, keepdims=True))
               - jnp.exp(jnp.sum(lf[2:3] * lf[3:4], axis=-1, keepdims=True)) + lam_init)

    chunks = []
    for i in range(n_src):
        kt_ref, v_ref = refs[1 + 2 * i], refs[2 + 2 * i]
        nk = kt_ref.shape[2]
        chunks += [(kt_ref, v_ref, c0, min(chunk, nk - c0)) for c0 in range(0, nk, chunk)]

    width = DIFF_D if diff else HEAD_DIM
    gq = vec_ref[0:1] if diff else vec_ref[2:3]
    gk = vec_ref[1:2] if diff else vec_ref[3:4]
    bound = jnp.max(jnp.abs(gq)) * jnp.max(jnp.abs(gk)) * (math.sqrt(width) * LOG2E * BOUND_SLACK)

    def attend_bounded(sel):
        qm = jnp.where(sel, q, jnp.zeros_like(q))
        l = jnp.zeros((tq, 1), F32)
        acc = jnp.zeros((tq, GROUP_W), F32)
        for kt_ref, v_ref, c0, ck in chunks:
            p = jnp.exp2(_dot(qm, kt_ref[0, :, c0:c0 + ck]) - bound)
            l = l + jnp.sum(p, axis=-1, keepdims=True)
            acc = acc + _dot(p.astype(BF16), v_ref[0, c0:c0 + ck, :])
        return acc * (1.0 / l)

    def attend_online(sel):
        qm = jnp.where(sel, q, jnp.zeros_like(q))
        m = jnp.full((tq, 1), NEG_INF, F32)
        l = jnp.zeros((tq, 1), F32)
        acc = jnp.zeros((tq, GROUP_W), F32)
        for kt_ref, v_ref, c0, ck in chunks:
            s = _dot(qm, kt_ref[0, :, c0:c0 + ck])
            m_new = jnp.maximum(m, jnp.max(s, axis=-1, keepdims=True))
            alpha = jnp.exp2(m - m_new)
            p = jnp.exp2(s - m_new)
            l = alpha * l + jnp.sum(p, axis=-1, keepdims=True)
            acc = alpha * acc + _dot(p.astype(BF16), v_ref[0, c0:c0 + ck, :])
            m = m_new
        return acc * (1.0 / l)

    def head_out(attend, h):
        if diff:
            return (attend((lane >> (LOG2_HEAD_DIM - 1)) == 2 * h)
                    - lam * attend((lane >> (LOG2_HEAD_DIM - 1)) == 2 * h + 1))
        return attend((lane >> LOG2_HEAD_DIM) == h)

    def finish(out):
        y = _seg_rms(out, seg64_ref[...], HEAD_DIM) * vec_ref[vec_row:vec_row + 1]
        o_ref[0] = (y * out_scale).astype(BF16)

    def run_bounded():
        out = jnp.zeros((tq, GROUP_W), F32)
        for h in range(HEADS):
            out = jnp.where((lane >> LOG2_HEAD_DIM) == h, head_out(attend_bounded, h), out)
        finish(out)

    def run_online():
        def body(h, out):
            return jnp.where((lane >> LOG2_HEAD_DIM) == h, head_out(attend_online, h), out)

        finish(lax.fori_loop(0, HEADS, body, jnp.zeros((tq, GROUP_W), F32)))

    small = bound <= SAFE_EXP2_BOUND
    pl.when(small)(run_bounded)
    pl.when(jnp.logical_not(small))(run_online)


def _attention(q, srcs, lam, lam_init, seg64, vec, vec_row, out_scale, tq, chunk=768):
    b, nq, _ = q.shape
    diff = lam is not None
    in_specs = [pl.BlockSpec((1, tq, GROUP_W), lambda i, t: (i, t, 0))]
    args = [q]
    for kt, v in srcs:
        nk = v.shape[1]
        in_specs += [pl.BlockSpec((1, GROUP_W, nk), lambda i, t: (i, 0, 0)),
                     pl.BlockSpec((1, nk, GROUP_W), lambda i, t: (i, 0, 0))]
        args += [kt, v]
    if diff:
        in_specs.append(_full((4, DIFF_D)))
        args.append(lam)
    in_specs += [_full((GROUP_W, GROUP_W)), _full((VEC_ROWS, GROUP_W))]
    args += [seg64, vec]
    kern = functools.partial(_attn_kernel, n_src=len(srcs), diff=diff, lam_init=lam_init, chunk=chunk, tq=tq,
                             vec_row=vec_row, out_scale=out_scale)
    return pl.pallas_call(
        kern,
        out_shape=jax.ShapeDtypeStruct((b, nq, GROUP_W), BF16),
        grid=(b, nq // tq),
        in_specs=in_specs,
        out_specs=pl.BlockSpec((1, tq, GROUP_W), lambda i, t: (i, t, 0)),
        compiler_params=_params("arbitrary", "arbitrary"),
        name="diff_attention" if diff else "ctx_attention",
    )(*args)


def _na_kernel(q_ref, kt_ref, v_ref, ktc_ref, vc_ref, tab_ref, seg64_ref, vec_ref, o_ref, *, n_rows, pairs):
    n_steps = n_rows // NA_QROWS
    tq = NA_QROWS * GRID_W
    lane = lax.broadcasted_iota(I32, (1, GROUP_W), 1)
    head = lane >> LOG2_HEAD_DIM
    ktc = ktc_ref[0]
    vc = vc_ref[0]
    bound = (jnp.max(jnp.abs(vec_ref[2:3])) * jnp.max(jnp.abs(vec_ref[3:4])) * (math.sqrt(HEAD_DIM) * LOG2E * BOUND_SLACK)
             + jnp.max(vec_ref[8:9]))

    def run(bounded):
        for pi in range(pairs):
            t = pl.program_id(1) * pairs + pi
            ws = jnp.clip(NA_QROWS * t - NA_ROWS // 2, 0, n_rows - NA_WIN_ROWS)
            k0 = pl.multiple_of(ws * GRID_W, 128)
            tid = jnp.where(t < 2, t, jnp.where(t < n_steps - 2, 2, t - (n_steps - 5)))
            q = q_ref[0, pi * tq:(pi + 1) * tq, :]
            out = jnp.zeros((tq, GROUP_W), F32)
            for h0 in range(0, HEADS, NA_HEAD_STACK):
                hs = range(h0, h0 + NA_HEAD_STACK)
                qs = jnp.concatenate([jnp.where(head == hh, q, jnp.zeros_like(q)) for hh in hs], axis=0)
                s_loc = (_dot(qs, kt_ref[0, :, pl.ds(k0, NA_WIN)])
                         + tab_ref[0, tid, h0 * tq:(h0 + NA_HEAD_STACK) * tq, :])
                s_ctx = _dot(qs, ktc)
                if bounded:
                    m = bound
                else:
                    m = jnp.maximum(jnp.max(s_loc, axis=-1, keepdims=True), jnp.max(s_ctx, axis=-1, keepdims=True))
                p_loc = jnp.exp2(s_loc - m)
                p_ctx = jnp.exp2(s_ctx - m)
                l = jnp.sum(p_loc, axis=-1, keepdims=True) + jnp.sum(p_ctx, axis=-1, keepdims=True)
                o = _dot(p_loc.astype(BF16), v_ref[0, pl.ds(k0, NA_WIN), :]) + _dot(p_ctx.astype(BF16), vc)
                o = o * (1.0 / l)
                for i, hh in enumerate(hs):
                    out = jnp.where(head == hh, o[i * tq:(i + 1) * tq], out)
            o_ref[0, pi * tq:(pi + 1) * tq, :] = (
                _seg_rms(out, seg64_ref[...], HEAD_DIM) * vec_ref[7:8]).astype(BF16)

    small = bound <= SAFE_EXP2_BOUND
    pl.when(small)(lambda: run(True))
    pl.when(jnp.logical_not(small))(lambda: run(False))


def _na_attention(q, kt, v, ktc, vc, tables, layer, seg64, vec, pairs):
    b, n, _ = q.shape
    nc = vc.shape[1]
    tq = pairs * NA_QROWS * GRID_W
    kern = functools.partial(_na_kernel, n_rows=n // GRID_W, pairs=pairs)
    return pl.pallas_call(
        kern,
        out_shape=jax.ShapeDtypeStruct((b, n, GROUP_W), BF16),
        grid=(b, n // tq),
        in_specs=[
            pl.BlockSpec((1, tq, GROUP_W), lambda i, t: (i, t, 0)),
            pl.BlockSpec((1, GROUP_W, n), lambda i, t: (i, 0, 0)),
            pl.BlockSpec((1, n, GROUP_W), lambda i, t: (i, 0, 0)),
            pl.BlockSpec((1, GROUP_W, nc), lambda i, t: (i, 0, 0)),
            pl.BlockSpec((1, nc, GROUP_W), lambda i, t: (i, 0, 0)),
            pl.BlockSpec((1,) + tables.shape[1:], lambda i, t: (layer, 0, 0, 0)),
            _full((GROUP_W, GROUP_W)),
            _full((VEC_ROWS, GROUP_W)),
        ],
        out_specs=pl.BlockSpec((1, tq, GROUP_W), lambda i, t: (i, t, 0)),
        compiler_params=_params("arbitrary", "arbitrary"),
        name="neighbourhood_attention",
    )(q, kt, v, ktc, vc, tables, seg64, vec)


def _out_kernel(ya_ref, yb_ref, yc_ref, yd_ref, w_ref, x_ref, mod_ref, g2_ref, rwt_ref, o_ref, xn_ref, lg_ref):
    y = jnp.concatenate([ya_ref[0], yb_ref[0], yc_ref[0], yd_ref[0]], axis=-1)
    mod = mod_ref[0]
    x = x_ref[0] + mod[:, 2 * D_MODEL:3 * D_MODEL] * _dot(y, w_ref[...])
    o_ref[0] = x
    sh = mod[:, 3 * D_MODEL:4 * D_MODEL]
    sc = mod[:, 4 * D_MODEL:5 * D_MODEL]
    ms = jnp.mean(x * x, axis=-1, keepdims=True)
    h = x * lax.rsqrt(ms + EPS) * g2_ref[...]
    h = (h * (1.0 + sc) + sh).astype(BF16)
    xn_ref[0] = h
    lg_ref[0] = lax.dot_general(rwt_ref[...], h, (((1,), (1,)), ((), ())), preferred_element_type=F32)


def _out_proj(ya, yb, yc, yd, w_out, x, mod, g2, rwt, tm):
    b, n, d = x.shape
    tok = pl.BlockSpec((1, tm, GROUP_W), lambda i, t: (i, t, 0))
    xs = pl.BlockSpec((1, tm, d), lambda i, t: (i, t, 0))
    return pl.pallas_call(
        _out_kernel,
        out_shape=(jax.ShapeDtypeStruct((b, n, d), F32),
                   jax.ShapeDtypeStruct((b, n, d), BF16),
                   jax.ShapeDtypeStruct((b, N_EXPERTS, n), F32)),
        grid=(b, n // tm),
        in_specs=[tok, tok, tok, tok, _full((4 * GROUP_W, d)), xs,
                  pl.BlockSpec((1, 1, 6 * d), lambda i, t: (i, 0, 0)),
                  _full((1, d)), _full((N_EXPERTS, d))],
        out_specs=(xs, xs, pl.BlockSpec((1, N_EXPERTS, tm), lambda i, t: (i, 0, t))),
        compiler_params=_params("arbitrary", "arbitrary"),
        name="out_proj",
    )(ya, yb, yc, yd, w_out, x, mod, g2, rwt)


def _cumsum_excl(m, tri):
    n = m.shape[1]
    carry = jnp.zeros((m.shape[0], 1), F32)
    outs = []
    for j in range(n // GROUP_W):
        blk = m[:, j * GROUP_W:(j + 1) * GROUP_W]
        inc = _dot(blk.astype(BF16), tri)
        outs.append(inc - blk + carry)
        carry = carry + inc[:, GROUP_W - 1:GROUP_W]
    return jnp.concatenate(outs, axis=1)


def _route_kernel(lg_ref, tri_ref, pos_ref, gate_ref, post_ref, st_ref, *, n, cap, tile, nb):
    affs = []
    for i in range(nb):
        lg = lg_ref[i]
        e = jnp.exp(lg - jnp.max(lg, axis=0, keepdims=True))
        affs.append(e / jnp.sum(e, axis=0, keepdims=True))
        gate_ref[i] = affs[i]
    aff = affs[0] if nb == 1 else jnp.concatenate(affs, axis=0)

    def unresolved(state):
        lo, hi = state
        return jnp.max(jnp.where(lo < hi, 1.0, 0.0)) > 0.0

    def bisect(state):
        lo, hi = state
        mid = 0.5 * (lo + hi)
        mid = jnp.where(mid > lo, mid, hi)
        ge = aff >= mid
        cnt = jnp.sum(jnp.where(ge, 1.0, 0.0), axis=1, keepdims=True)
        least_ge = jnp.min(jnp.where(ge, aff, jnp.inf), axis=1, keepdims=True)
        most_lt = jnp.max(jnp.where(ge, NEG_INF, aff), axis=1, keepdims=True)
        up = cnt >= cap
        return jnp.where(up, least_ge, lo), jnp.where(up, hi, most_lt)

    thr, _ = lax.while_loop(unresolved, bisect, (jnp.min(aff, axis=1, keepdims=True),
                                                 jnp.max(aff, axis=1, keepdims=True)))
    gt = aff > thr
    eq = aff == thr
    need = cap - jnp.sum(jnp.where(gt, 1.0, 0.0), axis=1, keepdims=True)
    tri = tri_ref[...]
    rank_eq = _cumsum_excl(jnp.where(eq, 1.0, 0.0), tri)
    sel = jnp.where(gt, 1.0, jnp.where(eq, jnp.where(rank_eq < need, 1.0, 0.0), 0.0))
    cum = _cumsum_excl(sel, tri)
    pos = jnp.where(sel > 0.0, cum, -1.0)
    tile_lane = lax.broadcasted_iota(I32, (nb * N_EXPERTS, LANES), 1)
    starts = jnp.zeros((nb * N_EXPERTS, LANES), F32)
    for t in range(n // tile):
        starts = jnp.where(tile_lane == t, cum[:, t * tile:t * tile + 1], starts)
    for i in range(nb):
        rows = slice(i * N_EXPERTS, (i + 1) * N_EXPERTS)
        pos_ref[i] = pos[rows].astype(I32)
        st_ref[i] = starts[rows].astype(I32)
        padded = jnp.concatenate([pos[rows], jnp.full((LANES - N_EXPERTS, n), -1.0, F32)], axis=0)
        post_ref[i] = padded.T.astype(I32)


def _route(logits, tri, cap, tile, nb=2):
    b, _, n = logits.shape
    em = pl.BlockSpec((nb, N_EXPERTS, n), lambda i: (i, 0, 0))
    kern = functools.partial(_route_kernel, n=n, cap=cap, tile=tile, nb=nb)
    return pl.pallas_call(
        kern,
        out_shape=(jax.ShapeDtypeStruct((b, N_EXPERTS, n), I32),
                   jax.ShapeDtypeStruct((b, N_EXPERTS, n), F32),
                   jax.ShapeDtypeStruct((b, n, LANES), I32),
                   jax.ShapeDtypeStruct((b, N_EXPERTS, LANES), I32)),
        grid=(b // nb,),
        in_specs=[em, _full((GROUP_W, GROUP_W))],
        out_specs=(em, em, pl.BlockSpec((nb, n, LANES), lambda i: (i, 0, 0)),
                   pl.BlockSpec((nb, N_EXPERTS, LANES), lambda i: (i, 0, 0))),
        compiler_params=_params("arbitrary"),
        name="router",
    )(logits, tri)


def _ffn_kernel(st_ref, xn_ref, pos_ref, gate_ref, wg32_ref, wu32_ref, wd32_ref, o_ref, wg_ref, wu_ref, wd_ref, xg_ref,
                *, bb, cap, n, tile, stride):
    @pl.when(pl.program_id(1) == 0)
    def _():
        wg_ref[...] = wg32_ref[0, 0].astype(BF16)
        wu_ref[...] = wu32_ref[0, 0].astype(BF16)
        wd_ref[...] = wd32_ref[0, 0].astype(BF16)

    expert = pl.ds(pl.program_id(0), 1)

    def step(gather):
        slot = lax.broadcasted_iota(I32, (cap, n), 0)
        hits = [pos_ref[i, expert, :] == slot for i in range(bb)]
        gs = [jnp.sum(jnp.where(hits[i], gate_ref[i, expert, :], 0.0), axis=1, keepdims=True) for i in range(bb)]
        g = gs[0] if bb == 1 else jnp.concatenate(gs, axis=0)
        gather(hits)
        xg = xg_ref[...].astype(BF16)
        hid = (jax.nn.silu(_dot(xg, wg_ref[...])) * _dot(xg, wu_ref[...])).astype(BF16)
        o = _dot(hid, wd_ref[...]) * g
        for i in range(bb):
            o_ref[i, 0] = o[i * cap:(i + 1) * cap].astype(BF16)

    def gather_dense(hits):
        for i in range(bb):
            xg_ref[i * cap:(i + 1) * cap, :] = _dot(jnp.where(hits[i], 1.0, 0.0).astype(BF16), xn_ref[i])

    if cap <= SLOT_WINDOW:
        step(gather_dense)
        return

    nt = n // tile
    win, fits = [], True
    for i in range(bb):
        base = (pl.program_id(1) * bb + i) * nt * stride * N_EXPERTS + pl.program_id(0)
        for t in range(nt):
            start = st_ref[base + t * stride * N_EXPERTS]
            end = st_ref[base + (t + 1) * stride * N_EXPERTS] if t + 1 < nt else cap
            a = jnp.minimum((start >> 4) << 4, cap - SLOT_WINDOW)
            win.append(a)
            fits = jnp.logical_and(fits, end - a <= SLOT_WINDOW)

    def gather_windowed(hits):
        del hits
        xg_ref[...] = jnp.zeros_like(xg_ref)
        wslot = lax.broadcasted_iota(I32, (SLOT_WINDOW, tile), 0)
        for i in range(bb):
            for t in range(nt):
                a = win[i * nt + t]
                p = pos_ref[i, expert, t * tile:(t + 1) * tile]
                onehot = jnp.where(p - a == wslot, 1.0, 0.0).astype(BF16)
                rows = pl.ds(pl.multiple_of(i * cap + a, 16), SLOT_WINDOW)
                xg_ref[rows, :] = xg_ref[rows, :] + _dot(onehot, xn_ref[i, t * tile:(t + 1) * tile, :])

    pl.when(fits)(lambda: step(gather_windowed))
    pl.when(jnp.logical_not(fits))(lambda: step(gather_dense))


def _expert_ffn(xn, pos, gate, starts, wg, wu, wd, layer, cap, bb, tile, stride):
    b, n, d = xn.shape
    wspec = pl.BlockSpec((1, 1, d, d), lambda e, j, st: (layer, e, 0, 0))
    sel = pl.BlockSpec((bb, N_EXPERTS, n), lambda e, j, st: (j, 0, 0))
    kern = functools.partial(_ffn_kernel, bb=bb, cap=cap, n=n, tile=tile, stride=stride)
    return pl.pallas_call(
        kern,
        out_shape=jax.ShapeDtypeStruct((b, N_EXPERTS, cap, d), BF16),
        grid_spec=pltpu.PrefetchScalarGridSpec(
            num_scalar_prefetch=1,
            grid=(N_EXPERTS, b // bb),
            in_specs=[pl.BlockSpec((bb, n, d), lambda e, j, st: (j, 0, 0)), sel, sel, wspec, wspec, wspec],
            out_specs=pl.BlockSpec((bb, 1, cap, d), lambda e, j, st: (j, e, 0, 0)),
            scratch_shapes=[pltpu.VMEM((d, d), BF16)] * 3 + [pltpu.VMEM((bb * cap, d), F32)]),
        compiler_params=_params("arbitrary", "arbitrary"),
        name="expert_ffn",
    )(starts, xn, pos, gate, wg, wu, wd)


def _scatter_kernel(st_ref, x_ref, mod_ref, pt_ref, o_ref, out_ref, *, tn, cap, nt, sub, window):
    for s in range(sub):
        _scatter_tile(st_ref, x_ref.at[0, s * tn:(s + 1) * tn], mod_ref, pt_ref.at[0, s * tn:(s + 1) * tn], o_ref,
                      out_ref.at[0, s * tn:(s + 1) * tn], pl.program_id(1) * sub + s,
                      tn=tn, cap=cap, nt=nt, window=window)


def _scatter_tile(st_ref, x_ref, mod_ref, pt_ref, o_ref, out_ref, t, *, tn, cap, nt, window):
    pos_t = pt_ref[...]
    g = mod_ref[0][:, 5 * D_MODEL:6 * D_MODEL]

    def dense():
        if cap % LANES == 0:
            slot = lax.broadcasted_iota(I32, (tn, cap), 1)
            onehot = jnp.concatenate(
                [jnp.where(pos_t[:, e:e + 1] == slot, 1.0, 0.0).astype(BF16) for e in range(N_EXPERTS)], axis=1)
        else:
            slot = lax.broadcasted_iota(I32, (tn, N_EXPERTS * cap), 1)
            acc = jnp.zeros((tn, N_EXPERTS * cap), F32)
            for e in range(N_EXPERTS):
                pe = pos_t[:, e:e + 1]
                acc = jnp.where(jnp.where(pe >= 0, pe + e * cap, -1) == slot, 1.0, acc)
            onehot = acc.astype(BF16)
        y = _dot(onehot, o_ref[0].reshape(N_EXPERTS * cap, D_MODEL))
        out_ref[...] = x_ref[...] + g * y

    if cap <= window:
        dense()
        return

    base = (pl.program_id(0) * nt + t) * N_EXPERTS
    nxt = jnp.minimum(t + 1, nt - 1)
    nbase = (pl.program_id(0) * nt + nxt) * N_EXPERTS
    win, fits = [], True
    for e in range(N_EXPERTS):
        start = st_ref[base + e]
        end = jnp.where(t + 1 < nt, st_ref[nbase + e], cap)
        a = jnp.minimum((start >> 4) << 4, cap - window)
        win.append(a)
        fits = jnp.logical_and(fits, end - a <= window)

    @pl.when(fits)
    def _():
        per = LANES // window
        lane = lax.broadcasted_iota(I32, (tn, LANES), 1)
        groups = []
        for e0 in range(0, N_EXPERTS, per):
            target = jnp.full((tn, LANES), -1, I32)
            for i in range(per):
                rel = pos_t[:, e0 + i:e0 + i + 1] - win[e0 + i]
                here = (lane >> (window.bit_length() - 1)) == i if per > 1 else None
                shifted = jnp.where(rel >= 0, rel + i * window, -1)
                target = shifted if here is None else jnp.where(here, shifted, target)
            groups.append(jnp.where(target == lane, 1.0, 0.0).astype(BF16))
        onehot = jnp.concatenate(groups, axis=1)
        rows = jnp.concatenate(
            [o_ref[0, e, pl.ds(pl.multiple_of(win[e], 16), window), :] for e in range(N_EXPERTS)], axis=0)
        out_ref[...] = x_ref[...] + g * _dot(onehot, rows)

    pl.when(jnp.logical_not(fits))(dense)


def _scatter(x, mod, pos_t, starts, o, cap, tn, window, sub):
    b, n, d = x.shape
    nt = n // tn
    xs = pl.BlockSpec((1, sub * tn, d), lambda i, t, st: (i, t, 0))
    kern = functools.partial(_scatter_kernel, tn=tn, cap=cap, nt=nt, sub=sub, window=window)
    return pl.pallas_call(
        kern,
        out_shape=jax.ShapeDtypeStruct((b, n, d), F32),
        grid_spec=pltpu.PrefetchScalarGridSpec(
            num_scalar_prefetch=1,
            grid=(b, nt // sub),
            in_specs=[xs,
                      pl.BlockSpec((1, 1, 6 * d), lambda i, t, st: (i, 0, 0)),
                      pl.BlockSpec((1, sub * tn, LANES), lambda i, t, st: (i, t, 0)),
                      pl.BlockSpec((1, N_EXPERTS, cap, d), lambda i, t, st: (i, 0, 0, 0))],
            out_specs=xs),
        compiler_params=_params("arbitrary", "arbitrary"),
        name="scatter_add",
    )(starts, x, mod, pos_t, o)


@functools.lru_cache(maxsize=None)
def _np_consts():
    lane = np.arange(GROUP_W)
    seg32 = (lane[:, None] // DIFF_D == lane[None, :] // DIFF_D).astype(np.float32)
    seg64 = (lane[:, None] // HEAD_DIM == lane[None, :] // HEAD_DIM).astype(np.float32)
    ang = 2.0 * np.pi * ((lane[:, None] % HEAD_DIM) * (lane[None, :] % HEAD_DIM) % HEAD_DIM) / HEAD_DIM
    cc = np.cos(ang) * seg64
    ss = np.sin(ang) * seg64
    tri = (lane[:, None] <= lane[None, :]).astype(np.float32)
    return dict(seg32=seg32, seg64=seg64, cc=cc, ss=ss, tri=tri)


@functools.lru_cache(maxsize=None)
def _np_dft(n):
    idx = (np.arange(n, dtype=np.int64)[:, None] * np.arange(n, dtype=np.int64)[None, :]) % n
    ang = 2.0 * np.pi * idx.astype(np.float64) / n
    return np.cos(ang).astype(np.float32), (-np.sin(ang)).astype(np.float32)


@functools.lru_cache(maxsize=None)
def _np_rope(n):
    half = DIFF_D // 2
    inv = 1.0 / (ROPE_BASE ** (np.arange(0, half, 2, dtype=np.float32) / half))
    t = np.arange(n)
    row = (t // GRID_W).astype(np.float32)[:, None] * inv
    col = (t % GRID_W).astype(np.float32)[:, None] * inv
    nf = inv.shape[0]
    d = np.arange(GROUP_W) % DIFF_D
    f = d % nf
    is_col = d >= half
    second = (d % half) >= nf
    ang = np.where(is_col[None, :], col[:, f], row[:, f])
    c = np.cos(ang).astype(np.float32)
    s = np.sin(ang).astype(np.float32)
    s = np.where(second[None, :], s, -s)
    return c, s


@functools.lru_cache(maxsize=None)
def _np_na_index(n_rows):
    n_steps = n_rows // NA_QROWS
    reps = [0, 1, 2, n_steps - 2, n_steps - 1]
    tq = NA_QROWS * GRID_W
    roff = np.zeros((len(reps), NA_QROWS, NA_WIN_ROWS), np.int32)
    valid = np.zeros((len(reps), tq, NA_WIN), bool)
    for ci, t in enumerate(reps):
        ws = int(np.clip(NA_QROWS * t - NA_ROWS // 2, 0, n_rows - NA_WIN_ROWS))
        qi = np.arange(tq)
        r = NA_QROWS * t + qi // GRID_W
        qcol = qi % GRID_W
        kk = np.arange(NA_WIN)
        krow = ws + kk // GRID_W
        kcol = kk % GRID_W
        rstart = np.clip(r - NA_ROWS // 2, 0, n_rows - NA_ROWS)
        wstart = np.clip(qcol - NA_COLS // 2, 0, GRID_W - NA_COLS)
        vr = (krow[None, :] >= rstart[:, None]) & (krow[None, :] < rstart[:, None] + NA_ROWS)
        vc = (kcol[None, :] >= wstart[:, None]) & (kcol[None, :] < wstart[:, None] + NA_COLS)
        valid[ci] = vr & vc
        rows_q = NA_QROWS * t + np.arange(NA_QROWS)
        rows_k = ws + np.arange(NA_WIN_ROWS)
        roff[ci] = np.clip(rows_k[None, :] - rows_q[:, None] + NA_ROWS - 1, 0, 2 * NA_ROWS - 2)
    return roff, valid


def _na_table_kernel(r_ref, mask_ref, o_ref, toep_ref, *, roff):
    n_off = 2 * NA_ROWS - 1
    for h in range(HEADS):
        for ro in range(n_off):
            row = jnp.broadcast_to(r_ref[0, h, ro:ro + 1, :], (GRID_W, LANES))
            toep_ref[h, ro] = pltpu.roll(row, 0, 1, stride=1, stride_axis=0)
    lane = lax.broadcasted_iota(I32, (GRID_W, LANES), 1)
    n_cls = roff.shape[0]
    for ci in range(n_cls):
        for h in range(HEADS):
            for rr in range(NA_QROWS):
                r0 = h * NA_QROWS * GRID_W + rr * GRID_W
                for kp in range(NA_WIN_ROWS // 2):
                    left = toep_ref[h, int(roff[ci, rr, 2 * kp])]
                    right = pltpu.roll(toep_ref[h, int(roff[ci, rr, 2 * kp + 1])], GRID_W, 1)
                    bias = jnp.where(lane < GRID_W, left, right) * LOG2E
                    o_ref[0, ci, r0:r0 + GRID_W, kp * LANES:(kp + 1) * LANES] = (
                        bias + mask_ref[ci, rr * GRID_W:(rr + 1) * GRID_W, kp * LANES:(kp + 1) * LANES])


def _na_tables(rpb_all, n_rows):
    roff, valid = _np_na_index(n_rows)
    n_cls = roff.shape[0]
    depth = rpb_all.shape[0]
    n_off = 2 * NA_ROWS - 1
    r = jnp.concatenate([rpb_all[..., NA_COLS - 1:], jnp.zeros(rpb_all.shape[:-1] + (LANES - 2 * NA_COLS + 1,), F32),
                         rpb_all[..., :NA_COLS - 1]], axis=-1).astype(F32)
    mask = jnp.asarray(np.where(valid, 0.0, NEG_INF).astype(np.float32))
    tq = NA_QROWS * GRID_W
    kern = functools.partial(_na_table_kernel, roff=roff)
    return pl.pallas_call(
        kern,
        out_shape=jax.ShapeDtypeStruct((depth, n_cls, HEADS * tq, NA_WIN), F32),
        grid=(depth,),
        in_specs=[pl.BlockSpec((1, HEADS, n_off, LANES), lambda l: (l, 0, 0, 0)),
                  _full((n_cls, tq, NA_WIN))],
        out_specs=pl.BlockSpec((1, n_cls, HEADS * tq, NA_WIN), lambda l: (l, 0, 0, 0)),
        scratch_shapes=[pltpu.VMEM((HEADS, n_off, GRID_W, LANES), F32)],
        compiler_params=_params("arbitrary"),
        name="na_bias_table",
    )(r, mask)


class _Tiles(NamedTuple):
    proj: int
    attn: int
    na_pairs: int
    moe: int
    ffn_samples: int


def _latent_tiles(n):
    return _Tiles(proj=min(n, 1024), attn=min(n, 512), na_pairs=min(8, n // (NA_QROWS * GRID_W)),
                  moe=min(n, 512), ffn_samples=2)
def _moe(x, xn, logits, mod, tri, wg, wu, wd, layer, bb, tn):
    n = x.shape[1]
    cap = CAPACITY_FACTOR * n // N_EXPERTS
    stride = 2 if cap > SLOT_WINDOW else 1
    ts = tn // stride
    pos, gate, pos_t, starts = _route(logits, tri, cap, ts)
    starts = starts[:, :, :n // ts].transpose(0, 2, 1).reshape(-1)
    o = _expert_ffn(xn, pos, gate, starts, wg, wu, wd, layer, cap, bb, tn, stride)
    return _scatter(x, mod, pos_t, starts, o, cap, ts, SLOT_WINDOW // stride, min(4, n // ts))


def kernel(x, c, ctx, c_ctx, ada_w, ada_b, norm1_g, norm2_g, w_in, w_out, head_out_g, sgu_w, sgu_b, diff_qn_g, diff_kn_g, diff_lambda, na_qn_g, na_kn_g, na_rpb, router_w, exp_w_gate, exp_w_up, exp_w_down):
    b, n, d = x.shape
    n_ctx = ctx.shape[1]
    npc = _np_consts()
    consts = {k: jnp.asarray(v, F32).astype(BF16) for k, v in npc.items()}
    seg64, tri = consts["seg64"], consts["tri"]
    cn, sn = (jnp.asarray(a, F32).astype(BF16) for a in _np_dft(n))
    cn_c, sn_c = (jnp.asarray(a, F32).astype(BF16) for a in _np_dft(n_ctx))
    rope = tuple(jnp.asarray(a, F32) for a in _np_rope(n))

    pad = (-(b + 1)) % 8
    c_rows = jnp.concatenate([c, c_ctx[None, :], jnp.zeros((pad, d), F32)], axis=0)
    mod_all = _modulation(c_rows, ada_w, ada_b)
    na_tables = _na_tables(na_rpb, n // GRID_W)

    xc = ctx
    for l in range(DEPTH):
        last = l == DEPTH - 1
        lam_init = 0.8 - 0.6 * math.exp(-0.3 * l)
        mod = mod_all[l, :b][:, None, :]
        mod_c = jnp.broadcast_to(mod_all[l, b][None, None, :], (b, 1, 6 * d))
        g1 = norm1_g[l][None, :]
        g2 = norm2_g[l][None, :]
        w_in_l = w_in[l].astype(BF16)
        w_out_l = w_out[l].astype(BF16)
        sguw = sgu_w[l].astype(BF16).transpose(1, 0, 2).reshape(CHUNK, HEADS * CHUNK)
        sgub = jnp.repeat(sgu_b[l].T, HEAD_DIM, axis=1)
        hg = head_out_g[l].reshape(4, GROUP_W)
        vec = jnp.stack([jnp.tile(diff_qn_g[l], GROUP_W // DIFF_D), jnp.tile(diff_kn_g[l], GROUP_W // DIFF_D),
                         jnp.tile(na_qn_g[l], HEADS), jnp.tile(na_kn_g[l], HEADS),
                         hg[0], hg[1], hg[2], hg[3],
                         jnp.broadcast_to(jnp.max(jnp.abs(na_rpb[l])) * LOG2E, (GROUP_W,))], axis=0).astype(F32)
        vec = jnp.pad(vec, ((0, VEC_ROWS - vec.shape[0]), (0, 0)))
        lam = diff_lambda[l].astype(F32)
        rwt = router_w[l].T.astype(BF16)
        experts = (exp_w_gate, exp_w_up, exp_w_down)

        tl = _latent_tiles(n)
        ya, zc, zs, qc, kct, vc, qd, kdt, vd = _in_proj(x, mod, g1, w_in_l, consts, sguw, sgub, vec, rope, tl.proj)
        if last:
            kct_c, vc_c, kdt_c, vd_c = _in_proj(
                xc, mod_c, g1, w_in_l, consts, sguw, sgub, vec, None, n_ctx, kv_only=True)
        else:
            ya_c, zc_c, zs_c, qc_c, kct_c, vc_c, qd_c, kdt_c, vd_c = _in_proj(
                xc, mod_c, g1, w_in_l, consts, sguw, sgub, vec, None, n_ctx)

        yb = _fourier(zc, zs, cn, sn, seg64, vec, tl.proj)
        yc = _attention(qc, [(kct, vc), (kct_c, vc_c)], lam, lam_init, seg64, vec, 6, 1.0 - lam_init,
                        tl.attn, tl.attn)
        yd = _na_attention(qd, kdt, vd, kdt_c, vd_c, na_tables, l, seg64, vec, tl.na_pairs)
        x, xn, logits = _out_proj(ya, yb, yc, yd, w_out_l, x, mod, g2, rwt, tl.proj)
        x = _moe(x, xn, logits, mod, tri, *experts, l, tl.ffn_samples, tl.moe)

        if not last:
            yb_c = _fourier(zc_c, zs_c, cn_c, sn_c, seg64, vec, n_ctx)
            yc_c = _attention(qc_c, [(kct_c, vc_c)], lam, lam_init, seg64, vec, 6, 1.0 - lam_init, n_ctx)
            yd_c = _attention(qd_c, [(kdt_c, vd_c)], None, 0.0, seg64, vec, 7, 1.0, n_ctx)
            xc, xn_c, logits_c = _out_proj(ya_c, yb_c, yc_c, yd_c, w_out_l, xc, mod_c, g2, rwt, n_ctx)
            xc = _moe(xc, xn_c, logits_c, mod_c, tri, *experts, l, b, n_ctx)
    return x
```

```python
import functools
import math
from typing import NamedTuple

import numpy as np
import jax
import jax.numpy as jnp
from jax import lax
from jax.experimental import pallas as pl
from jax.experimental.pallas import tpu as pltpu

F32 = jnp.float32
BF16 = jnp.bfloat16
I32 = jnp.int32

D_MODEL = 1024
DEPTH = 2
GRID_W = 64
LANES = 128
HEAD_DIM = 64
LOG2_HEAD_DIM = 6
GROUP_W = 256
HEADS = GROUP_W // HEAD_DIM
CHUNK = 128
DIFF_D = HEAD_DIM // 2
NA_ROWS = 8
NA_COLS = 16
N_EXPERTS = 16
CAPACITY_FACTOR = 2
ROPE_BASE = 10000.0
EPS = 1e-6
IN_W = 9 * GROUP_W
LOG2E = 1.4426950408889634

VMEM_LIMIT_BYTES = 56 * 1024 * 1024
NA_QROWS = 2
NA_WIN_ROWS = NA_ROWS + 2
NA_WIN = NA_WIN_ROWS * GRID_W
NA_HEAD_STACK = 1
NEG_INF = float("-inf")
VEC_ROWS = 16
SLOT_WINDOW = 128
SAFE_EXP2_BOUND = 48.0
BOUND_SLACK = 1.02


def _dot(a, b):
    return jnp.dot(a, b, preferred_element_type=F32)


def _params(*sem):
    return pltpu.CompilerParams(dimension_semantics=sem, vmem_limit_bytes=VMEM_LIMIT_BYTES)


def _full(shape):
    nd = len(shape)
    return pl.BlockSpec(shape, lambda *_: (0,) * nd)


def _seg_rms(x, seg, width):
    ss = _dot((x * x).astype(BF16), seg)
    return x * lax.rsqrt(ss * (1.0 / width) + EPS)


def _mod_kernel(c_ref, w_ref, b_ref, o_ref):
    s = jax.nn.silu(c_ref[...]).astype(BF16)
    o_ref[0] = _dot(s, w_ref[0].astype(BF16)) + b_ref[0]


def _modulation(c_rows, ada_w, ada_b):
    depth, d, w6 = ada_w.shape
    r = c_rows.shape[0]
    tn = 1024
    return pl.pallas_call(
        _mod_kernel,
        out_shape=jax.ShapeDtypeStruct((depth, r, w6), F32),
        grid=(depth, w6 // tn),
        in_specs=[
            pl.BlockSpec((r, d), lambda l, j: (0, 0)),
            pl.BlockSpec((1, d, tn), lambda l, j: (l, 0, j)),
            pl.BlockSpec((1, 1, tn), lambda l, j: (l, 0, j)),
        ],
        out_specs=pl.BlockSpec((1, r, tn), lambda l, j: (l, 0, j)),
        compiler_params=_params("arbitrary", "arbitrary"),
        name="modulation",
    )(c_rows, ada_w, ada_b.reshape(depth, 1, w6))


def _rope(x, c, s, lane):
    fwd = pltpu.roll(x, GROUP_W - 8, 1)
    bwd = pltpu.roll(x, 8, 1)
    partner = jnp.where((lane & 8) == 0, fwd, bwd)
    return x * c + partner * s


def _in_kernel(*refs, tm, use_rope, kv_only, c_scale, d_scale, scatter):
    if scatter is not None:
        st_ref, refs = refs[0], refs[1:]
    (x_ref, mod_ref, g1_ref, w_ref, seg32_ref, seg64_ref, cc_ref, ss_ref, sguw_ref, sgub_ref, vec_ref) = refs[:11]
    rest = refs[11:]
    if use_rope:
        ropec_ref, ropes_ref = rest[:2]
        rest = rest[2:]
    if scatter is not None:
        modprev_ref, pt_ref, o_ref, x2_ref = rest[:4]
        rest = rest[4:]
    if kv_only:
        kct_ref, vc_ref, kdt_ref, vd_ref = rest
    else:
        ya_ref, zc_ref, zs_ref, qc_ref, kct_ref, vc_ref, qd_ref, kdt_ref, vd_ref = rest

    if scatter is not None:
        tn = scatter["tn"]
        for s in range(tm // tn):
            rows = slice(s * tn, (s + 1) * tn)
            _scatter_tile(st_ref, x_ref.at[0, rows], modprev_ref, pt_ref.at[0, rows], o_ref, x2_ref.at[0, rows],
                          pl.program_id(1) * (tm // tn) + s, **scatter)
        x = x2_ref[0]
    else:
        x = x_ref[0]
    mod = mod_ref[0]
    sh = mod[:, 0:D_MODEL]
    sc = mod[:, D_MODEL:2 * D_MODEL]
    ms = jnp.mean(x * x, axis=-1, keepdims=True)
    h = x * lax.rsqrt(ms + EPS) * g1_ref[...]
    h = (h * (1.0 + sc) + sh).astype(BF16)
    lane = lax.broadcasted_iota(I32, (1, GROUP_W), 1)
    head = lane >> LOG2_HEAD_DIM
    seg32 = seg32_ref[...]
    seg64 = seg64_ref[...]
    vec = vec_ref[...]

    if kv_only:
        pc = _dot(h, w_ref[:, 4 * GROUP_W:6 * GROUP_W])
        k = _seg_rms(pc[:, 0:GROUP_W], seg32, DIFF_D) * vec[1:2]
        kct_ref[0] = k.T.astype(BF16)
        vc_ref[0] = pc[:, GROUP_W:2 * GROUP_W].astype(BF16)
        pd = _dot(h, w_ref[:, 7 * GROUP_W:9 * GROUP_W])
        kd = _seg_rms(pd[:, 0:GROUP_W], seg64, HEAD_DIM) * vec[3:4]
        kdt_ref[0] = kd.T.astype(BF16)
        vd_ref[0] = pd[:, GROUP_W:2 * GROUP_W].astype(BF16)
        return

    z = jax.nn.gelu(_dot(h, w_ref[:, 0:2 * GROUP_W]))
    u = z[:, 0:GROUP_W]
    vn = _seg_rms(z[:, GROUP_W:2 * GROUP_W], seg64, HEAD_DIM).astype(BF16)
    rows = []
    for c in range(tm // CHUNK):
        vch = vn[c * CHUNK:(c + 1) * CHUNK]
        stacked = jnp.concatenate([jnp.where(head == hh, vch, jnp.zeros_like(vch)) for hh in range(HEADS)], axis=0)
        rows.append(_dot(sguw_ref[...], stacked) + sgub_ref[...])
    ya = u * jnp.concatenate(rows, axis=0)
    ya_ref[0] = (_seg_rms(ya, seg64, HEAD_DIM) * vec[4:5]).astype(BF16)

    zb = _dot(h, w_ref[:, 2 * GROUP_W:3 * GROUP_W]).astype(BF16)
    zc_ref[0] = _dot(zb, cc_ref[...]).astype(BF16)
    zs_ref[0] = _dot(zb, ss_ref[...]).astype(BF16)

    pc = _dot(h, w_ref[:, 3 * GROUP_W:6 * GROUP_W])
    q = _seg_rms(pc[:, 0:GROUP_W], seg32, DIFF_D) * vec[0:1]
    k = _seg_rms(pc[:, GROUP_W:2 * GROUP_W], seg32, DIFF_D) * vec[1:2]
    if use_rope:
        rc = ropec_ref[...]
        rs = ropes_ref[...]
        q = _rope(q, rc, rs, lane)
        k = _rope(k, rc, rs, lane)
    qc_ref[0] = (q * c_scale).astype(BF16)
    kct_ref[0] = k.T.astype(BF16)
    vc_ref[0] = pc[:, 2 * GROUP_W:3 * GROUP_W].astype(BF16)

    pd = _dot(h, w_ref[:, 6 * GROUP_W:9 * GROUP_W])
    qd = _seg_rms(pd[:, 0:GROUP_W], seg64, HEAD_DIM) * vec[2:3]
    kd = _seg_rms(pd[:, GROUP_W:2 * GROUP_W], seg64, HEAD_DIM) * vec[3:4]
    qd_ref[0] = (qd * d_scale).astype(BF16)
    kdt_ref[0] = kd.T.astype(BF16)
    vd_ref[0] = pd[:, 2 * GROUP_W:3 * GROUP_W].astype(BF16)


def _in_proj(x, mod, g1, w_in, consts, sguw, sgub, vec, rope, tm, kv_only=False, pending=None):
    b, n, d = x.shape
    use_rope = rope is not None
    tok = pl.BlockSpec((1, tm, GROUP_W), lambda i, t, *_: (i, t, 0))
    tok_t = pl.BlockSpec((1, GROUP_W, tm), lambda i, t, *_: (i, 0, t))
    xs = pl.BlockSpec((1, tm, d), lambda i, t, *_: (i, t, 0))
    modspec = pl.BlockSpec((1, 1, 6 * d), lambda i, t, *_: (i, 0, 0))
    in_specs = [
        xs,
        modspec,
        _full((1, d)),
        _full((d, IN_W)),
        _full((GROUP_W, GROUP_W)), _full((GROUP_W, GROUP_W)), _full((GROUP_W, GROUP_W)), _full((GROUP_W, GROUP_W)),
        _full((CHUNK, HEADS * CHUNK)),
        _full((CHUNK, GROUP_W)),
        _full((VEC_ROWS, GROUP_W)),
    ]
    args = [x, mod, g1, w_in, consts["seg32"], consts["seg64"], consts["cc"], consts["ss"], sguw, sgub, vec]
    if use_rope:
        in_specs += [pl.BlockSpec((tm, GROUP_W), lambda i, t, *_: (t, 0))] * 2
        args += list(rope)
    sd = jax.ShapeDtypeStruct((b, n, GROUP_W), BF16)
    sdt = jax.ShapeDtypeStruct((b, GROUP_W, n), BF16)
    out_shape = (sdt, sd, sdt, sd) if kv_only else (sd, sd, sd, sd, sdt, sd, sd, sdt, sd)
    out_specs = (tok_t, tok, tok_t, tok) if kv_only else (tok, tok, tok, tok, tok_t, tok, tok, tok_t, tok)
    scatter = None
    prefetch = []
    if pending is not None:
        scatter = dict(tn=pending.tile, cap=pending.cap, nt=n // pending.tile, window=pending.window)
        prefetch = [pending.starts]
        in_specs += [modspec,
                     pl.BlockSpec((1, tm, LANES), lambda i, t, *_: (i, t, 0)),
                     pl.BlockSpec((1, N_EXPERTS, pending.cap, d), lambda i, t, *_: (i, 0, 0, 0))]
        args += [pending.mod, pending.pos_t, pending.o]
        out_shape = (jax.ShapeDtypeStruct((b, n, d), F32),) + out_shape
        out_specs = (xs,) + out_specs
    kern = functools.partial(
        _in_kernel, tm=tm, use_rope=use_rope, kv_only=kv_only, scatter=scatter,
        c_scale=(DIFF_D ** -0.5) * LOG2E, d_scale=(HEAD_DIM ** -0.5) * LOG2E)
    return pl.pallas_call(
        kern,
        out_shape=out_shape,
        grid_spec=pltpu.PrefetchScalarGridSpec(
            num_scalar_prefetch=len(prefetch), grid=(b, n // tm), in_specs=in_specs, out_specs=out_specs),
        compiler_params=_params("arbitrary", "arbitrary"),
        name="in_proj",
    )(*prefetch, *args)


def _fourier_kernel(cn_ref, sn_ref, zc_ref, zs_ref, seg64_ref, vec_ref, o_ref, *, norm):
    y = (_dot(cn_ref[...], zc_ref[0]) - _dot(sn_ref[...], zs_ref[0])) * norm
    o_ref[0] = (_seg_rms(y, seg64_ref[...], HEAD_DIM) * vec_ref[5:6]).astype(BF16)


def _fourier(zc, zs, cn, sn, seg64, vec, tn):
    b, n, _ = zc.shape
    kern = functools.partial(_fourier_kernel, norm=1.0 / math.sqrt(n * HEAD_DIM))
    return pl.pallas_call(
        kern,
        out_shape=jax.ShapeDtypeStruct((b, n, GROUP_W), BF16),
        grid=(n // tn, b),
        in_specs=[
            pl.BlockSpec((tn, n), lambda t, i: (t, 0)),
            pl.BlockSpec((tn, n), lambda t, i: (t, 0)),
            pl.BlockSpec((1, n, GROUP_W), lambda t, i: (i, 0, 0)),
            pl.BlockSpec((1, n, GROUP_W), lambda t, i: (i, 0, 0)),
            _full((GROUP_W, GROUP_W)),
            _full((VEC_ROWS, GROUP_W)),
        ],
        out_specs=pl.BlockSpec((1, tn, GROUP_W), lambda t, i: (i, t, 0)),
        compiler_params=_params("arbitrary", "arbitrary"),
        name="fourier",
    )(cn, sn, zc, zs, seg64, vec)


def _attn_kernel(*refs, n_src, diff, lam_init, chunk, tq, vec_row, out_scale):
    q_ref = refs[0]
    rest = refs[1 + 2 * n_src:]
    if diff:
        lam_ref = rest[0]
        rest = rest[1:]
    seg64_ref, vec_ref, o_ref = rest

    q = q_ref[0]
    lane = lax.broadcasted_iota(I32, (1, GROUP_W), 1)
    if diff:
        lf = lam_ref[...]
        lam = (jnp.exp(jnp.sum(lf[0:1] * lf[1:2], axis=-1, keepdims=True))
               - jnp.exp(jnp.sum(lf[2:3] * lf[3:4], axis=-1, keepdims=True)) + lam_init)

    chunks = []
    for i in range(n_src):
        kt_ref, v_ref = refs[1 + 2 * i], refs[2 + 2 * i]
        nk = kt_ref.shape[2]
        chunks += [(kt_ref, v_ref, c0, min(chunk, nk - c0)) for c0 in range(0, nk, chunk)]

    width = DIFF_D if diff else HEAD_DIM
    gq = vec_ref[0:1] if diff else vec_ref[2:3]
    gk = vec_ref[1:2] if diff else vec_ref[3:4]
    bound = jnp.max(jnp.abs(gq)) * jnp.max(jnp.abs(gk)) * (math.sqrt(width) * LOG2E * BOUND_SLACK)

    def attend_bounded(sel):
        qm = jnp.where(sel, q, jnp.zeros_like(q))
        l = jnp.zeros((tq, 1), F32)
        acc = jnp.zeros((tq, GROUP_W), F32)
        for kt_ref, v_ref, c0, ck in chunks:
            p = jnp.exp2(_dot(qm, kt_ref[0, :, c0:c0 + ck]) - bound)
            l = l + jnp.sum(p, axis=-1, keepdims=True)
            acc = acc + _dot(p.astype(BF16), v_ref[0, c0:c0 + ck, :])
        return acc * (1.0 / l)

    def attend_online(sel):
        qm = jnp.where(sel, q, jnp.zeros_like(q))
        m = jnp.full((tq, 1), NEG_INF, F32)
        l = jnp.zeros((tq, 1), F32)
        acc = jnp.zeros((tq, GROUP_W), F32)
        for kt_ref, v_ref, c0, ck in chunks:
            s = _dot(qm, kt_ref[0, :, c0:c0 + ck])
            m_new = jnp.maximum(m, jnp.max(s, axis=-1, keepdims=True))
            alpha = jnp.exp2(m - m_new)
            p = jnp.exp2(s - m_new)
            l = alpha * l + jnp.sum(p, axis=-1, keepdims=True)
            acc = alpha * acc + _dot(p.astype(BF16), v_ref[0, c0:c0 + ck, :])
            m = m_new
        return acc * (1.0 / l)

    def head_out(attend, h):
        if diff:
            return (attend((lane >> (LOG2_HEAD_DIM - 1)) == 2 * h)
                    - lam * attend((lane >> (LOG2_HEAD_DIM - 1)) == 2 * h + 1))
        return attend((lane >> LOG2_HEAD_DIM) == h)

    def finish(out):
        y = _seg_rms(out, seg64_ref[...], HEAD_DIM) * vec_ref[vec_row:vec_row + 1]
        o_ref[0] = (y * out_scale).astype(BF16)

    def run_bounded():
        out = jnp.zeros((tq, GROUP_W), F32)
        for h in range(HEADS):
            out = jnp.where((lane >> LOG2_HEAD_DIM) == h, head_out(attend_bounded, h), out)
        finish(out)

    def run_online():
        def body(h, out):
            return jnp.where((lane >> LOG2_HEAD_DIM) == h, head_out(attend_online, h), out)

        finish(lax.fori_loop(0, HEADS, body, jnp.zeros((tq, GROUP_W), F32)))

    small = bound <= SAFE_EXP2_BOUND
    pl.when(small)(run_bounded)
    pl.when(jnp.logical_not(small))(run_online)


def _attention(q, srcs, lam, lam_init, seg64, vec, vec_row, out_scale, tq, chunk=768):
    b, nq, _ = q.shape
    diff = lam is not None
    in_specs = [pl.BlockSpec((1, tq, GROUP_W), lambda i, t: (i, t, 0))]
    args = [q]
    for kt, v in srcs:
        nk = v.shape[1]
        in_specs += [pl.BlockSpec((1, GROUP_W, nk), lambda i, t: (i, 0, 0)),
                     pl.BlockSpec((1, nk, GROUP_W), lambda i, t: (i, 0, 0))]
        args += [kt, v]
    if diff:
        in_specs.append(_full((4, DIFF_D)))
        args.append(lam)
    in_specs += [_full((GROUP_W, GROUP_W)), _full((VEC_ROWS, GROUP_W))]
    args += [seg64, vec]
    kern = functools.partial(_attn_kernel, n_src=len(srcs), diff=diff, lam_init=lam_init, chunk=chunk, tq=tq,
                             vec_row=vec_row, out_scale=out_scale)
    return pl.pallas_call(
        kern,
        out_shape=jax.ShapeDtypeStruct((b, nq, GROUP_W), BF16),
        grid=(b, nq // tq),
        in_specs=in_specs,
        out_specs=pl.BlockSpec((1, tq, GROUP_W), lambda i, t: (i, t, 0)),
        compiler_params=_params("arbitrary", "arbitrary"),
        name="diff_attention" if diff else "ctx_attention",
    )(*args)


def _na_kernel(q_ref, kt_ref, v_ref, ktc_ref, vc_ref, tab_ref, seg64_ref, vec_ref, o_ref, *, n_rows, pairs):
    n_steps = n_rows // NA_QROWS
    tq = NA_QROWS * GRID_W
    lane = lax.broadcasted_iota(I32, (1, GROUP_W), 1)
    head = lane >> LOG2_HEAD_DIM
    ktc = ktc_ref[0]
    vc = vc_ref[0]
    bound = (jnp.max(jnp.abs(vec_ref[2:3])) * jnp.max(jnp.abs(vec_ref[3:4])) * (math.sqrt(HEAD_DIM) * LOG2E * BOUND_SLACK)
             + jnp.max(vec_ref[8:9]))

    def run(bounded):
        for pi in range(pairs):
            t = pl.program_id(1) * pairs + pi
            ws = jnp.clip(NA_QROWS * t - NA_ROWS // 2, 0, n_rows - NA_WIN_ROWS)
            k0 = pl.multiple_of(ws * GRID_W, 128)
            tid = jnp.where(t < 2, t, jnp.where(t < n_steps - 2, 2, t - (n_steps - 5)))
            q = q_ref[0, pi * tq:(pi + 1) * tq, :]
            out = jnp.zeros((tq, GROUP_W), F32)
            for h0 in range(0, HEADS, NA_HEAD_STACK):
                hs = range(h0, h0 + NA_HEAD_STACK)
                qs = jnp.concatenate([jnp.where(head == hh, q, jnp.zeros_like(q)) for hh in hs], axis=0)
                s_loc = (_dot(qs, kt_ref[0, :, pl.ds(k0, NA_WIN)])
                         + tab_ref[0, tid, h0 * tq:(h0 + NA_HEAD_STACK) * tq, :])
                s_ctx = _dot(qs, ktc)
                if bounded:
                    m = bound
                else:
                    m = jnp.maximum(jnp.max(s_loc, axis=-1, keepdims=True), jnp.max(s_ctx, axis=-1, keepdims=True))
                p_loc = jnp.exp2(s_loc - m)
                p_ctx = jnp.exp2(s_ctx - m)
                l = jnp.sum(p_loc, axis=-1, keepdims=True) + jnp.sum(p_ctx, axis=-1, keepdims=True)
                o = _dot(p_loc.astype(BF16), v_ref[0, pl.ds(k0, NA_WIN), :]) + _dot(p_ctx.astype(BF16), vc)
                o = o * (1.0 / l)
                for i, hh in enumerate(hs):
                    out = jnp.where(head == hh, o[i * tq:(i + 1) * tq], out)
            o_ref[0, pi * tq:(pi + 1) * tq, :] = (
                _seg_rms(out, seg64_ref[...], HEAD_DIM) * vec_ref[7:8]).astype(BF16)

    small = bound <= SAFE_EXP2_BOUND
    pl.when(small)(lambda: run(True))
    pl.when(jnp.logical_not(small))(lambda: run(False))


def _na_attention(q, kt, v, ktc, vc, tables, layer, seg64, vec, pairs):
    b, n, _ = q.shape
    nc = vc.shape[1]
    tq = pairs * NA_QROWS * GRID_W
    kern = functools.partial(_na_kernel, n_rows=n // GRID_W, pairs=pairs)
    return pl.pallas_call(
        kern,
        out_shape=jax.ShapeDtypeStruct((b, n, GROUP_W), BF16),
        grid=(b, n // tq),
        in_specs=[
            pl.BlockSpec((1, tq, GROUP_W), lambda i, t: (i, t, 0)),
            pl.BlockSpec((1, GROUP_W, n), lambda i, t: (i, 0, 0)),
            pl.BlockSpec((1, n, GROUP_W), lambda i, t: (i, 0, 0)),
            pl.BlockSpec((1, GROUP_W, nc), lambda i, t: (i, 0, 0)),
            pl.BlockSpec((1, nc, GROUP_W), lambda i, t: (i, 0, 0)),
            pl.BlockSpec((1,) + tables.shape[1:], lambda i, t: (layer, 0, 0, 0)),
            _full((GROUP_W, GROUP_W)),
            _full((VEC_ROWS, GROUP_W)),
        ],
        out_specs=pl.BlockSpec((1, tq, GROUP_W), lambda i, t: (i, t, 0)),
        compiler_params=_params("arbitrary", "arbitrary"),
        name="neighbourhood_attention",
    )(q, kt, v, ktc, vc, tables, seg64, vec)


def _out_kernel(ya_ref, yb_ref, yc_ref, yd_ref, w_ref, x_ref, mod_ref, g2_ref, rwt_ref, o_ref, xn_ref, lg_ref):
    y = jnp.concatenate([ya_ref[0], yb_ref[0], yc_ref[0], yd_ref[0]], axis=-1)
    mod = mod_ref[0]
    x = x_ref[0] + mod[:, 2 * D_MODEL:3 * D_MODEL] * _dot(y, w_ref[...])
    o_ref[0] = x
    sh = mod[:, 3 * D_MODEL:4 * D_MODEL]
    sc = mod[:, 4 * D_MODEL:5 * D_MODEL]
    ms = jnp.mean(x * x, axis=-1, keepdims=True)
    h = x * lax.rsqrt(ms + EPS) * g2_ref[...]
    h = (h * (1.0 + sc) + sh).astype(BF16)
    xn_ref[0] = h
    lg_ref[0] = lax.dot_general(rwt_ref[...], h, (((1,), (1,)), ((), ())), preferred_element_type=F32)


def _out_proj(ya, yb, yc, yd, w_out, x, mod, g2, rwt, tm):
    b, n, d = x.shape
    tok = pl.BlockSpec((1, tm, GROUP_W), lambda i, t: (i, t, 0))
    xs = pl.BlockSpec((1, tm, d), lambda i, t: (i, t, 0))
    return pl.pallas_call(
        _out_kernel,
        out_shape=(jax.ShapeDtypeStruct((b, n, d), F32),
                   jax.ShapeDtypeStruct((b, n, d), BF16),
                   jax.ShapeDtypeStruct((b, N_EXPERTS, n), F32)),
        grid=(b, n // tm),
        in_specs=[tok, tok, tok, tok, _full((4 * GROUP_W, d)), xs,
                  pl.BlockSpec((1, 1, 6 * d), lambda i, t: (i, 0, 0)),
                  _full((1, d)), _full((N_EXPERTS, d))],
        out_specs=(xs, xs, pl.BlockSpec((1, N_EXPERTS, tm), lambda i, t: (i, 0, t))),
        compiler_params=_params("arbitrary", "arbitrary"),
        name="out_proj",
    )(ya, yb, yc, yd, w_out, x, mod, g2, rwt)


def _cumsum_excl(m, tri):
    n = m.shape[1]
    carry = jnp.zeros((m.shape[0], 1), F32)
    outs = []
    for j in range(n // GROUP_W):
        blk = m[:, j * GROUP_W:(j + 1) * GROUP_W]
        inc = _dot(blk.astype(BF16), tri)
        outs.append(inc - blk + carry)
        carry = carry + inc[:, GROUP_W - 1:GROUP_W]
    return jnp.concatenate(outs, axis=1)


def _route_kernel(lg_ref, tri_ref, pos_ref, gate_ref, post_ref, st_ref, *, n, cap, tile, nb):
    affs = []
    for i in range(nb):
        lg = lg_ref[i]
        e = jnp.exp(lg - jnp.max(lg, axis=0, keepdims=True))
        affs.append(e / jnp.sum(e, axis=0, keepdims=True))
        gate_ref[i] = affs[i]
    aff = affs[0] if nb == 1 else jnp.concatenate(affs, axis=0)

    def unresolved(state):
        lo, hi = state
        return jnp.max(jnp.where(lo < hi, 1.0, 0.0)) > 0.0

    def bisect(state):
        lo, hi = state
        mid = 0.5 * (lo + hi)
        mid = jnp.where(mid > lo, mid, hi)
        ge = aff >= mid
        cnt = jnp.sum(jnp.where(ge, 1.0, 0.0), axis=1, keepdims=True)
        least_ge = jnp.min(jnp.where(ge, aff, jnp.inf), axis=1, keepdims=True)
        most_lt = jnp.max(jnp.where(ge, NEG_INF, aff), axis=1, keepdims=True)
        up = cnt >= cap
        return jnp.where(up, least_ge, lo), jnp.where(up, hi, most_lt)

    thr, _ = lax.while_loop(unresolved, bisect, (jnp.min(aff, axis=1, keepdims=True),
                                                 jnp.max(aff, axis=1, keepdims=True)))
    gt = aff > thr
    eq = aff == thr
    need = cap - jnp.sum(jnp.where(gt, 1.0, 0.0), axis=1, keepdims=True)
    tri = tri_ref[...]
    rank_eq = _cumsum_excl(jnp.where(eq, 1.0, 0.0), tri)
    sel = jnp.where(gt, 1.0, jnp.where(eq, jnp.where(rank_eq < need, 1.0, 0.0), 0.0))
    cum = _cumsum_excl(sel, tri)
    pos = jnp.where(sel > 0.0, cum, -1.0)
    tile_lane = lax.broadcasted_iota(I32, (nb * N_EXPERTS, LANES), 1)
    starts = jnp.zeros((nb * N_EXPERTS, LANES), F32)
    for t in range(n // tile):
        starts = jnp.where(tile_lane == t, cum[:, t * tile:t * tile + 1], starts)
    for i in range(nb):
        rows = slice(i * N_EXPERTS, (i + 1) * N_EXPERTS)
        pos_ref[i] = pos[rows].astype(I32)
        st_ref[i] = starts[rows].astype(I32)
        padded = jnp.concatenate([pos[rows], jnp.full((LANES - N_EXPERTS, n), -1.0, F32)], axis=0)
        post_ref[i] = padded.T.astype(I32)


def _route(logits, tri, cap, tile, nb=2):
    b, _, n = logits.shape
    em = pl.BlockSpec((nb, N_EXPERTS, n), lambda i: (i, 0, 0))
    kern = functools.partial(_route_kernel, n=n, cap=cap, tile=tile, nb=nb)
    return pl.pallas_call(
        kern,
        out_shape=(jax.ShapeDtypeStruct((b, N_EXPERTS, n), I32),
                   jax.ShapeDtypeStruct((b, N_EXPERTS, n), F32),
                   jax.ShapeDtypeStruct((b, n, LANES), I32),
                   jax.ShapeDtypeStruct((b, N_EXPERTS, LANES), I32)),
        grid=(b // nb,),
        in_specs=[em, _full((GROUP_W, GROUP_W))],
        out_specs=(em, em, pl.BlockSpec((nb, n, LANES), lambda i: (i, 0, 0)),
                   pl.BlockSpec((nb, N_EXPERTS, LANES), lambda i: (i, 0, 0))),
        compiler_params=_params("arbitrary"),
        name="router",
    )(logits, tri)


def _ffn_kernel(st_ref, xn_ref, pos_ref, gate_ref, wg32_ref, wu32_ref, wd32_ref, o_ref, wg_ref, wu_ref, wd_ref, xg_ref,
                *, bb, cap, n, tile, stride):
    @pl.when(pl.program_id(1) == 0)
    def _():
        wg_ref[...] = wg32_ref[0, 0].astype(BF16)
        wu_ref[...] = wu32_ref[0, 0].astype(BF16)
        wd_ref[...] = wd32_ref[0, 0].astype(BF16)

    expert = pl.ds(pl.program_id(0), 1)

    def step(gather):
        slot = lax.broadcasted_iota(I32, (cap, n), 0)
        hits = [pos_ref[i, expert, :] == slot for i in range(bb)]
        gs = [jnp.sum(jnp.where(hits[i], gate_ref[i, expert, :], 0.0), axis=1, keepdims=True) for i in range(bb)]
        g = gs[0] if bb == 1 else jnp.concatenate(gs, axis=0)
        gather(hits)
        xg = xg_ref[...].astype(BF16)
        hid = (jax.nn.silu(_dot(xg, wg_ref[...])) * _dot(xg, wu_ref[...])).astype(BF16)
        o = _dot(hid, wd_ref[...]) * g
        for i in range(bb):
            o_ref[i, 0] = o[i * cap:(i + 1) * cap].astype(BF16)

    def gather_dense(hits):
        for i in range(bb):
            xg_ref[i * cap:(i + 1) * cap, :] = _dot(jnp.where(hits[i], 1.0, 0.0).astype(BF16), xn_ref[i])

    if cap <= SLOT_WINDOW:
        step(gather_dense)
        return

    nt = n // tile
    win, fits = [], True
    for i in range(bb):
        base = (pl.program_id(1) * bb + i) * nt * stride * N_EXPERTS + pl.program_id(0)
        for t in range(nt):
            start = st_ref[base + t * stride * N_EXPERTS]
            end = st_ref[base + (t + 1) * stride * N_EXPERTS] if t + 1 < nt else cap
            a = jnp.minimum((start >> 4) << 4, cap - SLOT_WINDOW)
            win.append(a)
            fits = jnp.logical_and(fits, end - a <= SLOT_WINDOW)

    def gather_windowed(hits):
        del hits
        xg_ref[...] = jnp.zeros_like(xg_ref)
        wslot = lax.broadcasted_iota(I32, (SLOT_WINDOW, tile), 0)
        for i in range(bb):
            for t in range(nt):
                a = win[i * nt + t]
                p = pos_ref[i, expert, t * tile:(t + 1) * tile]
                onehot = jnp.where(p - a == wslot, 1.0, 0.0).astype(BF16)
                rows = pl.ds(pl.multiple_of(i * cap + a, 16), SLOT_WINDOW)
                xg_ref[rows, :] = xg_ref[rows, :] + _dot(onehot, xn_ref[i, t * tile:(t + 1) * tile, :])

    pl.when(fits)(lambda: step(gather_windowed))
    pl.when(jnp.logical_not(fits))(lambda: step(gather_dense))


def _expert_ffn(xn, pos, gate, starts, wg, wu, wd, layer, cap, bb, tile, stride):
    b, n, d = xn.shape
    wspec = pl.BlockSpec((1, 1, d, d), lambda e, j, st: (layer, e, 0, 0))
    sel = pl.BlockSpec((bb, N_EXPERTS, n), lambda e, j, st: (j, 0, 0))
    kern = functools.partial(_ffn_kernel, bb=bb, cap=cap, n=n, tile=tile, stride=stride)
    return pl.pallas_call(
        kern,
        out_shape=jax.ShapeDtypeStruct((b, N_EXPERTS, cap, d), BF16),
        grid_spec=pltpu.PrefetchScalarGridSpec(
            num_scalar_prefetch=1,
            grid=(N_EXPERTS, b // bb),
            in_specs=[pl.BlockSpec((bb, n, d), lambda e, j, st: (j, 0, 0)), sel, sel, wspec, wspec, wspec],
            out_specs=pl.BlockSpec((bb, 1, cap, d), lambda e, j, st: (j, e, 0, 0)),
            scratch_shapes=[pltpu.VMEM((d, d), BF16)] * 3 + [pltpu.VMEM((bb * cap, d), F32)]),
        compiler_params=_params("arbitrary", "arbitrary"),
        name="expert_ffn",
    )(starts, xn, pos, gate, wg, wu, wd)


def _scatter_kernel(st_ref, x_ref, mod_ref, pt_ref, o_ref, out_ref, *, tn, cap, nt, sub, window):
    for s in range(sub):
        _scatter_tile(st_ref, x_ref.at[0, s * tn:(s + 1) * tn], mod_ref, pt_ref.at[0, s * tn:(s + 1) * tn], o_ref,
                      out_ref.at[0, s * tn:(s + 1) * tn], pl.program_id(1) * sub + s,
                      tn=tn, cap=cap, nt=nt, window=window)


def _scatter_tile(st_ref, x_ref, mod_ref, pt_ref, o_ref, out_ref, t, *, tn, cap, nt, window):
    pos_t = pt_ref[...]
    g = mod_ref[0][:, 5 * D_MODEL:6 * D_MODEL]

    def dense():
        if cap % LANES == 0:
            slot = lax.broadcasted_iota(I32, (tn, cap), 1)
            onehot = jnp.concatenate(
                [jnp.where(pos_t[:, e:e + 1] == slot, 1.0, 0.0).astype(BF16) for e in range(N_EXPERTS)], axis=1)
        else:
            slot = lax.broadcasted_iota(I32, (tn, N_EXPERTS * cap), 1)
            acc = jnp.zeros((tn, N_EXPERTS * cap), F32)
            for e in range(N_EXPERTS):
                pe = pos_t[:, e:e + 1]
                acc = jnp.where(jnp.where(pe >= 0, pe + e * cap, -1) == slot, 1.0, acc)
            onehot = acc.astype(BF16)
        y = _dot(onehot, o_ref[0].reshape(N_EXPERTS * cap, D_MODEL))
        out_ref[...] = x_ref[...] + g * y

    if cap <= window:
        dense()
        return

    base = (pl.program_id(0) * nt + t) * N_EXPERTS
    nxt = jnp.minimum(t + 1, nt - 1)
    nbase = (pl.program_id(0) * nt + nxt) * N_EXPERTS
    win, fits = [], True
    for e in range(N_EXPERTS):
        start = st_ref[base + e]
        end = jnp.where(t + 1 < nt, st_ref[nbase + e], cap)
        a = jnp.minimum((start >> 4) << 4, cap - window)
        win.append(a)
        fits = jnp.logical_and(fits, end - a <= window)

    @pl.when(fits)
    def _():
        per = LANES // window
        lane = lax.broadcasted_iota(I32, (tn, LANES), 1)
        groups = []
        for e0 in range(0, N_EXPERTS, per):
            target = jnp.full((tn, LANES), -1, I32)
            for i in range(per):
                rel = pos_t[:, e0 + i:e0 + i + 1] - win[e0 + i]
                here = (lane >> (window.bit_length() - 1)) == i if per > 1 else None
                shifted = jnp.where(rel >= 0, rel + i * window, -1)
                target = shifted if here is None else jnp.where(here, shifted, target)
            groups.append(jnp.where(target == lane, 1.0, 0.0).astype(BF16))
        onehot = jnp.concatenate(groups, axis=1)
        rows = jnp.concatenate(
            [o_ref[0, e, pl.ds(pl.multiple_of(win[e], 16), window), :] for e in range(N_EXPERTS)], axis=0)
        out_ref[...] = x_ref[...] + g * _dot(onehot, rows)

    pl.when(jnp.logical_not(fits))(dense)


def _scatter(x, mod, pos_t, starts, o, cap, tn, window, sub):
    b, n, d = x.shape
    nt = n // tn
    xs = pl.BlockSpec((1, sub * tn, d), lambda i, t, st: (i, t, 0))
    kern = functools.partial(_scatter_kernel, tn=tn, cap=cap, nt=nt, sub=sub, window=window)
    return pl.pallas_call(
        kern,
        out_shape=jax.ShapeDtypeStruct((b, n, d), F32),
        grid_spec=pltpu.PrefetchScalarGridSpec(
            num_scalar_prefetch=1,
            grid=(b, nt // sub),
            in_specs=[xs,
                      pl.BlockSpec((1, 1, 6 * d), lambda i, t, st: (i, 0, 0)),
                      pl.BlockSpec((1, sub * tn, LANES), lambda i, t, st: (i, t, 0)),
                      pl.BlockSpec((1, N_EXPERTS, cap, d), lambda i, t, st: (i, 0, 0, 0))],
            out_specs=xs),
        compiler_params=_params("arbitrary", "arbitrary"),
        name="scatter_add",
    )(starts, x, mod, pos_t, o)


@functools.lru_cache(maxsize=None)
def _np_consts():
    lane = np.arange(GROUP_W)
    seg32 = (lane[:, None] // DIFF_D == lane[None, :] // DIFF_D).astype(np.float32)
    seg64 = (lane[:, None] // HEAD_DIM == lane[None, :] // HEAD_DIM).astype(np.float32)
    ang = 2.0 * np.pi * ((lane[:, None] % HEAD_DIM) * (lane[None, :] % HEAD_DIM) % HEAD_DIM) / HEAD_DIM
    cc = np.cos(ang) * seg64
    ss = np.sin(ang) * seg64
    tri = (lane[:, None] <= lane[None, :]).astype(np.float32)
    return dict(seg32=seg32, seg64=seg64, cc=cc, ss=ss, tri=tri)


@functools.lru_cache(maxsize=None)
def _np_dft(n):
    idx = (np.arange(n, dtype=np.int64)[:, None] * np.arange(n, dtype=np.int64)[None, :]) % n
    ang = 2.0 * np.pi * idx.astype(np.float64) / n
    return np.cos(ang).astype(np.float32), np.sin(ang).astype(np.float32)


@functools.lru_cache(maxsize=None)
def _np_rope(n):
    half = DIFF_D // 2
    inv = 1.0 / (ROPE_BASE ** (np.arange(0, half, 2, dtype=np.float32) / half))
    t = np.arange(n)
    row = (t // GRID_W).astype(np.float32)[:, None] * inv
    col = (t % GRID_W).astype(np.float32)[:, None] * inv
    nf = inv.shape[0]
    d = np.arange(GROUP_W) % DIFF_D
    f = d % nf
    is_col = d >= half
    second = (d % half) >= nf
    ang = np.where(is_col[None, :], col[:, f], row[:, f])
    c = np.cos(ang).astype(np.float32)
    s = np.sin(ang).astype(np.float32)
    s = np.where(second[None, :], s, -s)
    return c, s


@functools.lru_cache(maxsize=None)
def _np_na_index(n_rows):
    n_steps = n_rows // NA_QROWS
    reps = [0, 1, 2, n_steps - 2, n_steps - 1]
    tq = NA_QROWS * GRID_W
    roff = np.zeros((len(reps), NA_QROWS, NA_WIN_ROWS), np.int32)
    valid = np.zeros((len(reps), tq, NA_WIN), bool)
    for ci, t in enumerate(reps):
        ws = int(np.clip(NA_QROWS * t - NA_ROWS // 2, 0, n_rows - NA_WIN_ROWS))
        qi = np.arange(tq)
        r = NA_QROWS * t + qi // GRID_W
        qcol = qi % GRID_W
        kk = np.arange(NA_WIN)
        krow = ws + kk // GRID_W
        kcol = kk % GRID_W
        rstart = np.clip(r - NA_ROWS // 2, 0, n_rows - NA_ROWS)
        wstart = np.clip(qcol - NA_COLS // 2, 0, GRID_W - NA_COLS)
        vr = (krow[None, :] >= rstart[:, None]) & (krow[None, :] < rstart[:, None] + NA_ROWS)
        vc = (kcol[None, :] >= wstart[:, None]) & (kcol[None, :] < wstart[:, None] + NA_COLS)
        valid[ci] = vr & vc
        rows_q = NA_QROWS * t + np.arange(NA_QROWS)
        rows_k = ws + np.arange(NA_WIN_ROWS)
        roff[ci] = np.clip(rows_k[None, :] - rows_q[:, None] + NA_ROWS - 1, 0, 2 * NA_ROWS - 2)
    return roff, valid


def _na_table_kernel(r_ref, mask_ref, o_ref, toep_ref, *, roff):
    n_off = 2 * NA_ROWS - 1
    for h in range(HEADS):
        for ro in range(n_off):
            row = jnp.broadcast_to(r_ref[0, h, ro:ro + 1, :], (GRID_W, LANES))
            toep_ref[h, ro] = pltpu.roll(row, 0, 1, stride=1, stride_axis=0)
    lane = lax.broadcasted_iota(I32, (GRID_W, LANES), 1)
    n_cls = roff.shape[0]
    for ci in range(n_cls):
        for h in range(HEADS):
            for rr in range(NA_QROWS):
                r0 = h * NA_QROWS * GRID_W + rr * GRID_W
                for kp in range(NA_WIN_ROWS // 2):
                    left = toep_ref[h, int(roff[ci, rr, 2 * kp])]
                    right = pltpu.roll(toep_ref[h, int(roff[ci, rr, 2 * kp + 1])], GRID_W, 1)
                    bias = jnp.where(lane < GRID_W, left, right) * LOG2E
                    o_ref[0, ci, r0:r0 + GRID_W, kp * LANES:(kp + 1) * LANES] = (
                        bias + mask_ref[ci, rr * GRID_W:(rr + 1) * GRID_W, kp * LANES:(kp + 1) * LANES])


def _na_tables(rpb_all, n_rows):
    roff, valid = _np_na_index(n_rows)
    n_cls = roff.shape[0]
    depth = rpb_all.shape[0]
    n_off = 2 * NA_ROWS - 1
    r = jnp.concatenate([rpb_all[..., NA_COLS - 1:], jnp.zeros(rpb_all.shape[:-1] + (LANES - 2 * NA_COLS + 1,), F32),
                         rpb_all[..., :NA_COLS - 1]], axis=-1).astype(F32)
    mask = jnp.asarray(np.where(valid, 0.0, NEG_INF).astype(np.float32))
    tq = NA_QROWS * GRID_W
    kern = functools.partial(_na_table_kernel, roff=roff)
    return pl.pallas_call(
        kern,
        out_shape=jax.ShapeDtypeStruct((depth, n_cls, HEADS * tq, NA_WIN), F32),
        grid=(depth,),
        in_specs=[pl.BlockSpec((1, HEADS, n_off, LANES), lambda l: (l, 0, 0, 0)),
                  _full((n_cls, tq, NA_WIN))],
        out_specs=pl.BlockSpec((1, n_cls, HEADS * tq, NA_WIN), lambda l: (l, 0, 0, 0)),
        scratch_shapes=[pltpu.VMEM((HEADS, n_off, GRID_W, LANES), F32)],
        compiler_params=_params("arbitrary"),
        name="na_bias_table",
    )(r, mask)


class _Tiles(NamedTuple):
    proj: int
    proj_fused: int
    attn: int
    na_pairs: int
    moe: int
    ffn_samples: int


def _latent_tiles(n):
    return _Tiles(proj=min(n, 1024), proj_fused=min(n, 512), attn=min(n, 512),
                  na_pairs=min(8, n // (NA_QROWS * GRID_W)),
                  moe=min(n, 512), ffn_samples=2)
class _PendingMoe(NamedTuple):
    mod: jax.Array
    pos_t: jax.Array
    starts: jax.Array
    o: jax.Array
    cap: int
    tile: int
    window: int


def _moe(x, xn, logits, mod, tri, wg, wu, wd, layer, bb, tn, defer=False):
    n = x.shape[1]
    cap = CAPACITY_FACTOR * n // N_EXPERTS
    stride = 2 if cap > SLOT_WINDOW else 1
    ts = tn // stride
    pos, gate, pos_t, starts = _route(logits, tri, cap, ts)
    starts = starts[:, :, :n // ts].transpose(0, 2, 1).reshape(-1)
    o = _expert_ffn(xn, pos, gate, starts, wg, wu, wd, layer, cap, bb, tn, stride)
    if defer:
        return x, _PendingMoe(mod, pos_t, starts, o, cap, ts, SLOT_WINDOW // stride)
    return _scatter(x, mod, pos_t, starts, o, cap, ts, SLOT_WINDOW // stride, min(4, n // ts)), None


def kernel(x, c, ctx, c_ctx, ada_w, ada_b, norm1_g, norm2_g, w_in, w_out, head_out_g, sgu_w, sgu_b, diff_qn_g, diff_kn_g, diff_lambda, na_qn_g, na_kn_g, na_rpb, router_w, exp_w_gate, exp_w_up, exp_w_down):
    b, n, d = x.shape
    n_ctx = ctx.shape[1]
    npc = _np_consts()
    consts = {k: jnp.asarray(v, F32).astype(BF16) for k, v in npc.items()}
    seg64, tri = consts["seg64"], consts["tri"]
    cn, sn = (jnp.asarray(a, F32).astype(BF16) for a in _np_dft(n))
    cn_c, sn_c = (jnp.asarray(a, F32).astype(BF16) for a in _np_dft(n_ctx))
    rope = tuple(jnp.asarray(a, F32) for a in _np_rope(n))

    pad = (-(b + 1)) % 8
    c_rows = jnp.concatenate([c, c_ctx[None, :], jnp.zeros((pad, d), F32)], axis=0)
    mod_all = _modulation(c_rows, ada_w, ada_b)
    na_tables = _na_tables(na_rpb, n // GRID_W)

    xc = ctx
    pending = None
    for l in range(DEPTH):
        last = l == DEPTH - 1
        lam_init = 0.8 - 0.6 * math.exp(-0.3 * l)
        mod = mod_all[l, :b][:, None, :]
        mod_c = jnp.broadcast_to(mod_all[l, b][None, None, :], (b, 1, 6 * d))
        g1 = norm1_g[l][None, :]
        g2 = norm2_g[l][None, :]
        w_in_l = w_in[l].astype(BF16)
        w_out_l = w_out[l].astype(BF16)
        sguw = sgu_w[l].astype(BF16).transpose(1, 0, 2).reshape(CHUNK, HEADS * CHUNK)
        sgub = jnp.repeat(sgu_b[l].T, HEAD_DIM, axis=1)
        hg = head_out_g[l].reshape(4, GROUP_W)
        vec = jnp.stack([jnp.tile(diff_qn_g[l], GROUP_W // DIFF_D), jnp.tile(diff_kn_g[l], GROUP_W // DIFF_D),
                         jnp.tile(na_qn_g[l], HEADS), jnp.tile(na_kn_g[l], HEADS),
                         hg[0], hg[1], hg[2], hg[3],
                         jnp.broadcast_to(jnp.max(jnp.abs(na_rpb[l])) * LOG2E, (GROUP_W,))], axis=0).astype(F32)
        vec = jnp.pad(vec, ((0, VEC_ROWS - vec.shape[0]), (0, 0)))
        lam = diff_lambda[l].astype(F32)
        rwt = router_w[l].T.astype(BF16)
        experts = (exp_w_gate, exp_w_up, exp_w_down)

        tl = _latent_tiles(n)
        outs = _in_proj(x, mod, g1, w_in_l, consts, sguw, sgub, vec, rope,
                        tl.proj if pending is None else tl.proj_fused, pending=pending)
        if pending is not None:
            x, outs = outs[0], outs[1:]
        ya, zc, zs, qc, kct, vc, qd, kdt, vd = outs
        if last:
            kct_c, vc_c, kdt_c, vd_c = _in_proj(
                xc, mod_c, g1, w_in_l, consts, sguw, sgub, vec, None, n_ctx, kv_only=True)
        else:
            ya_c, zc_c, zs_c, qc_c, kct_c, vc_c, qd_c, kdt_c, vd_c = _in_proj(
                xc, mod_c, g1, w_in_l, consts, sguw, sgub, vec, None, n_ctx)

        yb = _fourier(zc, zs, cn, sn, seg64, vec, tl.proj)
        yc = _attention(qc, [(kct, vc), (kct_c, vc_c)], lam, lam_init, seg64, vec, 6, 1.0 - lam_init,
                        tl.attn, tl.attn)
        yd = _na_attention(qd, kdt, vd, kdt_c, vd_c, na_tables, l, seg64, vec, tl.na_pairs)
        x, xn, logits = _out_proj(ya, yb, yc, yd, w_out_l, x, mod, g2, rwt, tl.proj)
        x, pending = _moe(x, xn, logits, mod, tri, *experts, l, tl.ffn_samples, tl.moe, defer=not last)

        if not last:
            yb_c = _fourier(zc_c, zs_c, cn_c, sn_c, seg64, vec, n_ctx)
            yc_c = _attention(qc_c, [(kct_c, vc_c)], lam, lam_init, seg64, vec, 6, 1.0 - lam_init, n_ctx)
            yd_c = _attention(qd_c, [(kdt_c, vd_c)], None, 0.0, seg64, vec, 7, 1.0, n_ctx)
            xc, xn_c, logits_c = _out_proj(ya_c, yb_c, yc_c, yd_c, w_out_l, xc, mod_c, g2, rwt, n_ctx)
            xc, _ = _moe(xc, xn_c, logits_c, mod_c, tri, *experts, l, b, n_ctx)
    return x
```

```python
import functools
import math
from typing import NamedTuple

import numpy as np
import jax
import jax.numpy as jnp
from jax import lax
from jax.experimental import pallas as pl
from jax.experimental.pallas import tpu as pltpu

F32 = jnp.float32
BF16 = jnp.bfloat16
I32 = jnp.int32

D_MODEL = 1024
DEPTH = 2
GRID_W = 64
LANES = 128
HEAD_DIM = 64
LOG2_HEAD_DIM = 6
GROUP_W = 256
HEADS = GROUP_W // HEAD_DIM
CHUNK = 128
DIFF_D = HEAD_DIM // 2
NA_ROWS = 8
NA_COLS = 16
N_EXPERTS = 16
CAPACITY_FACTOR = 2
ROPE_BASE = 10000.0
EPS = 1e-6
IN_W = 9 * GROUP_W
LOG2E = 1.4426950408889634

VMEM_LIMIT_BYTES = 56 * 1024 * 1024
NA_QROWS = 2
NA_WIN_ROWS = NA_ROWS + 2
NA_WIN = NA_WIN_ROWS * GRID_W
NA_HEAD_STACK = 1
NEG_INF = float("-inf")
VEC_ROWS = 16
SLOT_WINDOW = 128
SAFE_EXP2_BOUND = 48.0
BOUND_SLACK = 1.02


def _dot(a, b):
    return jnp.dot(a, b, preferred_element_type=F32)


def _params(*sem):
    return pltpu.CompilerParams(dimension_semantics=sem, vmem_limit_bytes=VMEM_LIMIT_BYTES)


def _full(shape):
    nd = len(shape)
    return pl.BlockSpec(shape, lambda *_: (0,) * nd)


def _seg_rms(x, seg, width):
    ss = _dot((x * x).astype(BF16), seg)
    return x * lax.rsqrt(ss * (1.0 / width) + EPS)


def _mod_kernel(c_ref, w_ref, b_ref, o_ref):
    s = jax.nn.silu(c_ref[...]).astype(BF16)
    o_ref[0] = _dot(s, w_ref[0].astype(BF16)) + b_ref[0]


def _modulation(c_rows, ada_w, ada_b):
    depth, d, w6 = ada_w.shape
    r = c_rows.shape[0]
    tn = 1024
    return pl.pallas_call(
        _mod_kernel,
        out_shape=jax.ShapeDtypeStruct((depth, r, w6), F32),
        grid=(depth, w6 // tn),
        in_specs=[
            pl.BlockSpec((r, d), lambda l, j: (0, 0)),
            pl.BlockSpec((1, d, tn), lambda l, j: (l, 0, j)),
            pl.BlockSpec((1, 1, tn), lambda l, j: (l, 0, j)),
        ],
        out_specs=pl.BlockSpec((1, r, tn), lambda l, j: (l, 0, j)),
        compiler_params=_params("arbitrary", "arbitrary"),
        name="modulation",
    )(c_rows, ada_w, ada_b.reshape(depth, 1, w6))


def _rope(x, c, s, lane):
    fwd = pltpu.roll(x, GROUP_W - 8, 1)
    bwd = pltpu.roll(x, 8, 1)
    partner = jnp.where((lane & 8) == 0, fwd, bwd)
    return x * c + partner * s


def _in_kernel(*refs, tm, use_rope, kv_only, c_scale, d_scale):
    (x_ref, mod_ref, g1_ref, w_ref, seg32_ref, seg64_ref, cc_ref, ss_ref, sguw_ref, sgub_ref, vec_ref) = refs[:11]
    rest = refs[11:]
    if use_rope:
        ropec_ref, ropes_ref = rest[:2]
        rest = rest[2:]
    if kv_only:
        kct_ref, vc_ref, kdt_ref, vd_ref = rest
    else:
        ya_ref, zc_ref, zs_ref, qc_ref, kct_ref, vc_ref, qd_ref, kdt_ref, vd_ref = rest

    x = x_ref[0]
    mod = mod_ref[0]
    sh = mod[:, 0:D_MODEL]
    sc = mod[:, D_MODEL:2 * D_MODEL]
    ms = jnp.mean(x * x, axis=-1, keepdims=True)
    h = x * lax.rsqrt(ms + EPS) * g1_ref[...]
    h = (h * (1.0 + sc) + sh).astype(BF16)
    lane = lax.broadcasted_iota(I32, (1, GROUP_W), 1)
    head = lane >> LOG2_HEAD_DIM
    seg32 = seg32_ref[...]
    seg64 = seg64_ref[...]
    vec = vec_ref[...]

    if kv_only:
        pc = _dot(h, w_ref[:, 4 * GROUP_W:6 * GROUP_W])
        k = _seg_rms(pc[:, 0:GROUP_W], seg32, DIFF_D) * vec[1:2]
        kct_ref[0] = k.T.astype(BF16)
        vc_ref[0] = pc[:, GROUP_W:2 * GROUP_W].astype(BF16)
        pd = _dot(h, w_ref[:, 7 * GROUP_W:9 * GROUP_W])
        kd = _seg_rms(pd[:, 0:GROUP_W], seg64, HEAD_DIM) * vec[3:4]
        kdt_ref[0] = kd.T.astype(BF16)
        vd_ref[0] = pd[:, GROUP_W:2 * GROUP_W].astype(BF16)
        return

    z = jax.nn.gelu(_dot(h, w_ref[:, 0:2 * GROUP_W]))
    u = z[:, 0:GROUP_W]
    vn = _seg_rms(z[:, GROUP_W:2 * GROUP_W], seg64, HEAD_DIM).astype(BF16)
    rows = []
    for c in range(tm // CHUNK):
        vch = vn[c * CHUNK:(c + 1) * CHUNK]
        stacked = jnp.concatenate([jnp.where(head == hh, vch, jnp.zeros_like(vch)) for hh in range(HEADS)], axis=0)
        rows.append(_dot(sguw_ref[...], stacked) + sgub_ref[...])
    ya = u * jnp.concatenate(rows, axis=0)
    ya_ref[0] = (_seg_rms(ya, seg64, HEAD_DIM) * vec[4:5]).astype(BF16)

    zb = _dot(h, w_ref[:, 2 * GROUP_W:3 * GROUP_W]).astype(BF16)
    zc_ref[0] = _dot(zb, cc_ref[...]).astype(BF16)
    zs_ref[0] = _dot(zb, ss_ref[...]).astype(BF16)

    pc = _dot(h, w_ref[:, 3 * GROUP_W:6 * GROUP_W])
    q = _seg_rms(pc[:, 0:GROUP_W], seg32, DIFF_D) * vec[0:1]
    k = _seg_rms(pc[:, GROUP_W:2 * GROUP_W], seg32, DIFF_D) * vec[1:2]
    if use_rope:
        rc = ropec_ref[...]
        rs = ropes_ref[...]
        q = _rope(q, rc, rs, lane)
        k = _rope(k, rc, rs, lane)
    qc_ref[0] = (q * c_scale).astype(BF16)
    kct_ref[0] = k.T.astype(BF16)
    vc_ref[0] = pc[:, 2 * GROUP_W:3 * GROUP_W].astype(BF16)

    pd = _dot(h, w_ref[:, 6 * GROUP_W:9 * GROUP_W])
    qd = _seg_rms(pd[:, 0:GROUP_W], seg64, HEAD_DIM) * vec[2:3]
    kd = _seg_rms(pd[:, GROUP_W:2 * GROUP_W], seg64, HEAD_DIM) * vec[3:4]
    qd_ref[0] = (qd * d_scale).astype(BF16)
    kdt_ref[0] = kd.T.astype(BF16)
    vd_ref[0] = pd[:, 2 * GROUP_W:3 * GROUP_W].astype(BF16)


def _in_proj(x, mod, g1, w_in, consts, sguw, sgub, vec, rope, tm, kv_only=False):
    b, n, d = x.shape
    use_rope = rope is not None
    tok = pl.BlockSpec((1, tm, GROUP_W), lambda i, t: (i, t, 0))
    tok_t = pl.BlockSpec((1, GROUP_W, tm), lambda i, t: (i, 0, t))
    in_specs = [
        pl.BlockSpec((1, tm, d), lambda i, t: (i, t, 0)),
        pl.BlockSpec((1, 1, 6 * d), lambda i, t: (i, 0, 0)),
        _full((1, d)),
        _full((d, IN_W)),
        _full((GROUP_W, GROUP_W)), _full((GROUP_W, GROUP_W)), _full((GROUP_W, GROUP_W)), _full((GROUP_W, GROUP_W)),
        _full((CHUNK, HEADS * CHUNK)),
        _full((CHUNK, GROUP_W)),
        _full((VEC_ROWS, GROUP_W)),
    ]
    args = [x, mod, g1, w_in, consts["seg32"], consts["seg64"], consts["cc"], consts["ss"], sguw, sgub, vec]
    if use_rope:
        in_specs += [pl.BlockSpec((tm, GROUP_W), lambda i, t: (t, 0))] * 2
        args += list(rope)
    sd = jax.ShapeDtypeStruct((b, n, GROUP_W), BF16)
    sdt = jax.ShapeDtypeStruct((b, GROUP_W, n), BF16)
    kern = functools.partial(
        _in_kernel, tm=tm, use_rope=use_rope, kv_only=kv_only,
        c_scale=(DIFF_D ** -0.5) * LOG2E, d_scale=(HEAD_DIM ** -0.5) * LOG2E)
    return pl.pallas_call(
        kern,
        out_shape=(sdt, sd, sdt, sd) if kv_only else (sd, sd, sd, sd, sdt, sd, sd, sdt, sd),
        grid=(b, n // tm),
        in_specs=in_specs,
        out_specs=(tok_t, tok, tok_t, tok) if kv_only else (tok, tok, tok, tok, tok_t, tok, tok, tok_t, tok),
        compiler_params=_params("arbitrary", "arbitrary"),
        name="in_proj",
    )(*args)


def _fourier_kernel(cn_ref, sn_ref, zc_ref, zs_ref, seg64_ref, vec_ref, o_ref, *, norm):
    y = (_dot(cn_ref[...], zc_ref[0]) - _dot(sn_ref[...], zs_ref[0])) * norm
    o_ref[0] = (_seg_rms(y, seg64_ref[...], HEAD_DIM) * vec_ref[5:6]).astype(BF16)


def _fourier(zc, zs, cn, sn, seg64, vec, tn):
    b, n, _ = zc.shape
    kern = functools.partial(_fourier_kernel, norm=1.0 / math.sqrt(n * HEAD_DIM))
    return pl.pallas_call(
        kern,
        out_shape=jax.ShapeDtypeStruct((b, n, GROUP_W), BF16),
        grid=(n // tn, b),
        in_specs=[
            pl.BlockSpec((tn, n), lambda t, i: (t, 0)),
            pl.BlockSpec((tn, n), lambda t, i: (t, 0)),
            pl.BlockSpec((1, n, GROUP_W), lambda t, i: (i, 0, 0)),
            pl.BlockSpec((1, n, GROUP_W), lambda t, i: (i, 0, 0)),
            _full((GROUP_W, GROUP_W)),
            _full((VEC_ROWS, GROUP_W)),
        ],
        out_specs=pl.BlockSpec((1, tn, GROUP_W), lambda t, i: (i, t, 0)),
        compiler_params=_params("arbitrary", "arbitrary"),
        name="fourier",
    )(cn, sn, zc, zs, seg64, vec)


def _attn_kernel(*refs, n_src, diff, lam_init, chunk, tq, vec_row, out_scale):
    q_ref = refs[0]
    rest = refs[1 + 2 * n_src:]
    if diff:
        lam_ref = rest[0]
        rest = rest[1:]
    seg64_ref, vec_ref, o_ref = rest

    q = q_ref[0]
    lane = lax.broadcasted_iota(I32, (1, GROUP_W), 1)
    if diff:
        lf = lam_ref[...]
        lam = (jnp.exp(jnp.sum(lf[0:1] * lf[1:2], axis=-1, keepdims=True))
               - jnp.exp(jnp.sum(lf[2:3] * lf[3:4], axis=-1, keepdims=True)) + lam_init)

    chunks = []
    for i in range(n_src):
        kt_ref, v_ref = refs[1 + 2 * i], refs[2 + 2 * i]
        nk = kt_ref.shape[2]
        chunks += [(kt_ref, v_ref, c0, min(chunk, nk - c0)) for c0 in range(0, nk, chunk)]

    width = DIFF_D if diff else HEAD_DIM
    gq = vec_ref[0:1] if diff else vec_ref[2:3]
    gk = vec_ref[1:2] if diff else vec_ref[3:4]
    bound = jnp.max(jnp.abs(gq)) * jnp.max(jnp.abs(gk)) * (math.sqrt(width) * LOG2E * BOUND_SLACK)

    def attend_bounded(sel):
        qm = jnp.where(sel, q, jnp.zeros_like(q))
        l = jnp.zeros((tq, 1), F32)
        acc = jnp.zeros((tq, GROUP_W), F32)
        for kt_ref, v_ref, c0, ck in chunks:
            p = jnp.exp2(_dot(qm, kt_ref[0, :, c0:c0 + ck]) - bound)
            l = l + jnp.sum(p, axis=-1, keepdims=True)
            acc = acc + _dot(p.astype(BF16), v_ref[0, c0:c0 + ck, :])
        return acc * (1.0 / l)

    def attend_online(sel):
        qm = jnp.where(sel, q, jnp.zeros_like(q))
        m = jnp.full((tq, 1), NEG_INF, F32)
        l = jnp.zeros((tq, 1), F32)
        acc = jnp.zeros((tq, GROUP_W), F32)
        for kt_ref, v_ref, c0, ck in chunks:
            s = _dot(qm, kt_ref[0, :, c0:c0 + ck])
            m_new = jnp.maximum(m, jnp.max(s, axis=-1, keepdims=True))
            alpha = jnp.exp2(m - m_new)
            p = jnp.exp2(s - m_new)
            l = alpha * l + jnp.sum(p, axis=-1, keepdims=True)
            acc = alpha * acc + _dot(p.astype(BF16), v_ref[0, c0:c0 + ck, :])
            m = m_new
        return acc * (1.0 / l)

    def head_out(attend, h):
        if diff:
            return (attend((lane >> (LOG2_HEAD_DIM - 1)) == 2 * h)
                    - lam * attend((lane >> (LOG2_HEAD_DIM - 1)) == 2 * h + 1))
        return attend((lane >> LOG2_HEAD_DIM) == h)

    def finish(out):
        y = _seg_rms(out, seg64_ref[...], HEAD_DIM) * vec_ref[vec_row:vec_row + 1]
        o_ref[0] = (y * out_scale).astype(BF16)

    def run_bounded():
        out = jnp.zeros((tq, GROUP_W), F32)
        for h in range(HEADS):
            out = jnp.where((lane >> LOG2_HEAD_DIM) == h, head_out(attend_bounded, h), out)
        finish(out)

    def run_online():
        def body(h, out):
            return jnp.where((lane >> LOG2_HEAD_DIM) == h, head_out(attend_online, h), out)

        finish(lax.fori_loop(0, HEADS, body, jnp.zeros((tq, GROUP_W), F32)))

    small = bound <= SAFE_EXP2_BOUND
    pl.when(small)(run_bounded)
    pl.when(jnp.logical_not(small))(run_online)


def _attention(q, srcs, lam, lam_init, seg64, vec, vec_row, out_scale, tq, chunk=768):
    b, nq, _ = q.shape
    diff = lam is not None
    in_specs = [pl.BlockSpec((1, tq, GROUP_W), lambda i, t: (i, t, 0))]
    args = [q]
    for kt, v in srcs:
        nk = v.shape[1]
        in_specs += [pl.BlockSpec((1, GROUP_W, nk), lambda i, t: (i, 0, 0)),
                     pl.BlockSpec((1, nk, GROUP_W), lambda i, t: (i, 0, 0))]
        args += [kt, v]
    if diff:
        in_specs.append(_full((4, DIFF_D)))
        args.append(lam)
    in_specs += [_full((GROUP_W, GROUP_W)), _full((VEC_ROWS, GROUP_W))]
    args += [seg64, vec]
    kern = functools.partial(_attn_kernel, n_src=len(srcs), diff=diff, lam_init=lam_init, chunk=chunk, tq=tq,
                             vec_row=vec_row, out_scale=out_scale)
    return pl.pallas_call(
        kern,
        out_shape=jax.ShapeDtypeStruct((b, nq, GROUP_W), BF16),
        grid=(b, nq // tq),
        in_specs=in_specs,
        out_specs=pl.BlockSpec((1, tq, GROUP_W), lambda i, t: (i, t, 0)),
        compiler_params=_params("arbitrary", "arbitrary"),
        name="diff_attention" if diff else "ctx_attention",
    )(*args)


def _na_kernel(q_ref, kt_ref, v_ref, ktc_ref, vc_ref, tab_ref, seg64_ref, vec_ref, o_ref, *, n_rows, pairs):
    n_steps = n_rows // NA_QROWS
    tq = NA_QROWS * GRID_W
    lane = lax.broadcasted_iota(I32, (1, GROUP_W), 1)
    head = lane >> LOG2_HEAD_DIM
    ktc = ktc_ref[0]
    vc = vc_ref[0]
    bound = (jnp.max(jnp.abs(vec_ref[2:3])) * jnp.max(jnp.abs(vec_ref[3:4])) * (math.sqrt(HEAD_DIM) * LOG2E * BOUND_SLACK)
             + jnp.max(vec_ref[8:9]))

    def run(bounded):
        for pi in range(pairs):
            t = pl.program_id(1) * pairs + pi
            ws = jnp.clip(NA_QROWS * t - NA_ROWS // 2, 0, n_rows - NA_WIN_ROWS)
            k0 = pl.multiple_of(ws * GRID_W, 128)
            tid = jnp.where(t < 2, t, jnp.where(t < n_steps - 2, 2, t - (n_steps - 5)))
            q = q_ref[0, pi * tq:(pi + 1) * tq, :]
            out = jnp.zeros((tq, GROUP_W), F32)
            for h0 in range(0, HEADS, NA_HEAD_STACK):
                hs = range(h0, h0 + NA_HEAD_STACK)
                qs = jnp.concatenate([jnp.where(head == hh, q, jnp.zeros_like(q)) for hh in hs], axis=0)
                s_loc = (_dot(qs, kt_ref[0, :, pl.ds(k0, NA_WIN)])
                         + tab_ref[0, tid, h0 * tq:(h0 + NA_HEAD_STACK) * tq, :])
                s_ctx = _dot(qs, ktc)
                if bounded:
                    m = bound
                else:
                    m = jnp.maximum(jnp.max(s_loc, axis=-1, keepdims=True), jnp.max(s_ctx, axis=-1, keepdims=True))
                p_loc = jnp.exp2(s_loc - m)
                p_ctx = jnp.exp2(s_ctx - m)
                l = jnp.sum(p_loc, axis=-1, keepdims=True) + jnp.sum(p_ctx, axis=-1, keepdims=True)
                o = _dot(p_loc.astype(BF16), v_ref[0, pl.ds(k0, NA_WIN), :]) + _dot(p_ctx.astype(BF16), vc)
                o = o * (1.0 / l)
                for i, hh in enumerate(hs):
                    out = jnp.where(head == hh, o[i * tq:(i + 1) * tq], out)
            o_ref[0, pi * tq:(pi + 1) * tq, :] = (
                _seg_rms(out, seg64_ref[...], HEAD_DIM) * vec_ref[7:8]).astype(BF16)

    small = bound <= SAFE_EXP2_BOUND
    pl.when(small)(lambda: run(True))
    pl.when(jnp.logical_not(small))(lambda: run(False))


def _na_attention(q, kt, v, ktc, vc, tables, layer, seg64, vec, pairs):
    b, n, _ = q.shape
    nc = vc.shape[1]
    tq = pairs * NA_QROWS * GRID_W
    kern = functools.partial(_na_kernel, n_rows=n // GRID_W, pairs=pairs)
    return pl.pallas_call(
        kern,
        out_shape=jax.ShapeDtypeStruct((b, n, GROUP_W), BF16),
        grid=(b, n // tq),
        in_specs=[
            pl.BlockSpec((1, tq, GROUP_W), lambda i, t: (i, t, 0)),
            pl.BlockSpec((1, GROUP_W, n), lambda i, t: (i, 0, 0)),
            pl.BlockSpec((1, n, GROUP_W), lambda i, t: (i, 0, 0)),
            pl.BlockSpec((1, GROUP_W, nc), lambda i, t: (i, 0, 0)),
            pl.BlockSpec((1, nc, GROUP_W), lambda i, t: (i, 0, 0)),
            pl.BlockSpec((1,) + tables.shape[1:], lambda i, t: (layer, 0, 0, 0)),
            _full((GROUP_W, GROUP_W)),
            _full((VEC_ROWS, GROUP_W)),
        ],
        out_specs=pl.BlockSpec((1, tq, GROUP_W), lambda i, t: (i, t, 0)),
        compiler_params=_params("arbitrary", "arbitrary"),
        name="neighbourhood_attention",
    )(q, kt, v, ktc, vc, tables, seg64, vec)


def _out_kernel(ya_ref, yb_ref, yc_ref, yd_ref, w_ref, x_ref, mod_ref, g2_ref, rwt_ref, o_ref, xn_ref, lg_ref):
    y = jnp.concatenate([ya_ref[0], yb_ref[0], yc_ref[0], yd_ref[0]], axis=-1)
    mod = mod_ref[0]
    x = x_ref[0] + mod[:, 2 * D_MODEL:3 * D_MODEL] * _dot(y, w_ref[...])
    o_ref[0] = x
    sh = mod[:, 3 * D_MODEL:4 * D_MODEL]
    sc = mod[:, 4 * D_MODEL:5 * D_MODEL]
    ms = jnp.mean(x * x, axis=-1, keepdims=True)
    h = x * lax.rsqrt(ms + EPS) * g2_ref[...]
    h = (h * (1.0 + sc) + sh).astype(BF16)
    xn_ref[0] = h
    lg_ref[0] = lax.dot_general(rwt_ref[...], h, (((1,), (1,)), ((), ())), preferred_element_type=F32)


def _out_proj(ya, yb, yc, yd, w_out, x, mod, g2, rwt, tm):
    b, n, d = x.shape
    tok = pl.BlockSpec((1, tm, GROUP_W), lambda i, t: (i, t, 0))
    xs = pl.BlockSpec((1, tm, d), lambda i, t: (i, t, 0))
    return pl.pallas_call(
        _out_kernel,
        out_shape=(jax.ShapeDtypeStruct((b, n, d), F32),
                   jax.ShapeDtypeStruct((b, n, d), BF16),
                   jax.ShapeDtypeStruct((b, N_EXPERTS, n), F32)),
        grid=(b, n // tm),
        in_specs=[tok, tok, tok, tok, _full((4 * GROUP_W, d)), xs,
                  pl.BlockSpec((1, 1, 6 * d), lambda i, t: (i, 0, 0)),
                  _full((1, d)), _full((N_EXPERTS, d))],
        out_specs=(xs, xs, pl.BlockSpec((1, N_EXPERTS, tm), lambda i, t: (i, 0, t))),
        compiler_params=_params("arbitrary", "arbitrary"),
        name="out_proj",
    )(ya, yb, yc, yd, w_out, x, mod, g2, rwt)


def _cumsum_excl(m, tri):
    n = m.shape[1]
    carry = jnp.zeros((m.shape[0], 1), F32)
    outs = []
    for j in range(n // GROUP_W):
        blk = m[:, j * GROUP_W:(j + 1) * GROUP_W]
        inc = _dot(blk.astype(BF16), tri)
        outs.append(inc - blk + carry)
        carry = carry + inc[:, GROUP_W - 1:GROUP_W]
    return jnp.concatenate(outs, axis=1)


def _route_kernel(lg_ref, tri_ref, pos_ref, gate_ref, post_ref, st_ref, *, n, cap, tile, nb):
    affs = []
    for i in range(nb):
        lg = lg_ref[i]
        e = jnp.exp(lg - jnp.max(lg, axis=0, keepdims=True))
        affs.append(e / jnp.sum(e, axis=0, keepdims=True))
        gate_ref[i] = affs[i]
    aff = affs[0] if nb == 1 else jnp.concatenate(affs, axis=0)

    def unresolved(state):
        lo, hi = state
        return jnp.max(jnp.where(lo < hi, 1.0, 0.0)) > 0.0

    def bisect(state):
        lo, hi = state
        mid = 0.5 * (lo + hi)
        mid = jnp.where(mid > lo, mid, hi)
        ge = aff >= mid
        cnt = jnp.sum(jnp.where(ge, 1.0, 0.0), axis=1, keepdims=True)
        least_ge = jnp.min(jnp.where(ge, aff, jnp.inf), axis=1, keepdims=True)
        most_lt = jnp.max(jnp.where(ge, NEG_INF, aff), axis=1, keepdims=True)
        up = cnt >= cap
        return jnp.where(up, least_ge, lo), jnp.where(up, hi, most_lt)

    thr, _ = lax.while_loop(unresolved, bisect, (jnp.min(aff, axis=1, keepdims=True),
                                                 jnp.max(aff, axis=1, keepdims=True)))
    gt = aff > thr
    eq = aff == thr
    need = cap - jnp.sum(jnp.where(gt, 1.0, 0.0), axis=1, keepdims=True)
    tri = tri_ref[...]
    rank_eq = _cumsum_excl(jnp.where(eq, 1.0, 0.0), tri)
    sel = jnp.where(gt, 1.0, jnp.where(eq, jnp.where(rank_eq < need, 1.0, 0.0), 0.0))
    cum = _cumsum_excl(sel, tri)
    pos = jnp.where(sel > 0.0, cum, -1.0)
    tile_lane = lax.broadcasted_iota(I32, (nb * N_EXPERTS, LANES), 1)
    starts = jnp.zeros((nb * N_EXPERTS, LANES), F32)
    for t in range(n // tile):
        starts = jnp.where(tile_lane == t, cum[:, t * tile:t * tile + 1], starts)
    for i in range(nb):
        rows = slice(i * N_EXPERTS, (i + 1) * N_EXPERTS)
        pos_ref[i] = pos[rows].astype(I32)
        st_ref[i] = starts[rows].astype(I32)
        padded = jnp.concatenate([pos[rows], jnp.full((LANES - N_EXPERTS, n), -1.0, F32)], axis=0)
        post_ref[i] = padded.T.astype(I32)


def _route(logits, tri, cap, tile, nb=4):
    b, _, n = logits.shape
    em = pl.BlockSpec((nb, N_EXPERTS, n), lambda i: (i, 0, 0))
    kern = functools.partial(_route_kernel, n=n, cap=cap, tile=tile, nb=nb)
    return pl.pallas_call(
        kern,
        out_shape=(jax.ShapeDtypeStruct((b, N_EXPERTS, n), I32),
                   jax.ShapeDtypeStruct((b, N_EXPERTS, n), F32),
                   jax.ShapeDtypeStruct((b, n, LANES), I32),
                   jax.ShapeDtypeStruct((b, N_EXPERTS, LANES), I32)),
        grid=(b // nb,),
        in_specs=[em, _full((GROUP_W, GROUP_W))],
        out_specs=(em, em, pl.BlockSpec((nb, n, LANES), lambda i: (i, 0, 0)),
                   pl.BlockSpec((nb, N_EXPERTS, LANES), lambda i: (i, 0, 0))),
        compiler_params=_params("arbitrary"),
        name="router",
    )(logits, tri)


def _ffn_kernel(st_ref, xn_ref, pos_ref, gate_ref, wg32_ref, wu32_ref, wd32_ref, o_ref, wg_ref, wu_ref, wd_ref, xg_ref,
                *, bb, cap, n, tile, stride):
    @pl.when(pl.program_id(1) == 0)
    def _():
        wg_ref[...] = wg32_ref[0, 0].astype(BF16)
        wu_ref[...] = wu32_ref[0, 0].astype(BF16)
        wd_ref[...] = wd32_ref[0, 0].astype(BF16)

    expert = pl.ds(pl.program_id(0), 1)

    def step(gather):
        slot = lax.broadcasted_iota(I32, (cap, n), 0)
        hits = [pos_ref[i, expert, :] == slot for i in range(bb)]
        gs = [jnp.sum(jnp.where(hits[i], gate_ref[i, expert, :], 0.0), axis=1, keepdims=True) for i in range(bb)]
        g = gs[0] if bb == 1 else jnp.concatenate(gs, axis=0)
        gather(hits)
        xg = xg_ref[...].astype(BF16)
        hid = (jax.nn.silu(_dot(xg, wg_ref[...])) * _dot(xg, wu_ref[...])).astype(BF16)
        o = _dot(hid, wd_ref[...]) * g
        for i in range(bb):
            o_ref[i, 0] = o[i * cap:(i + 1) * cap].astype(BF16)

    def gather_dense(hits):
        for i in range(bb):
            xg_ref[i * cap:(i + 1) * cap, :] = _dot(jnp.where(hits[i], 1.0, 0.0).astype(BF16), xn_ref[i])

    if cap <= SLOT_WINDOW:
        step(gather_dense)
        return

    nt = n // tile
    win, fits = [], True
    for i in range(bb):
        base = (pl.program_id(1) * bb + i) * nt * stride * N_EXPERTS + pl.program_id(0)
        for t in range(nt):
            start = st_ref[base + t * stride * N_EXPERTS]
            end = st_ref[base + (t + 1) * stride * N_EXPERTS] if t + 1 < nt else cap
            a = jnp.minimum((start >> 4) << 4, cap - SLOT_WINDOW)
            win.append(a)
            fits = jnp.logical_and(fits, end - a <= SLOT_WINDOW)

    def gather_windowed(hits):
        del hits
        xg_ref[...] = jnp.zeros_like(xg_ref)
        wslot = lax.broadcasted_iota(I32, (SLOT_WINDOW, tile), 0)
        for i in range(bb):
            for t in range(nt):
                a = win[i * nt + t]
                p = pos_ref[i, expert, t * tile:(t + 1) * tile]
                onehot = jnp.where(p - a == wslot, 1.0, 0.0).astype(BF16)
                rows = pl.ds(pl.multiple_of(i * cap + a, 16), SLOT_WINDOW)
                xg_ref[rows, :] = xg_ref[rows, :] + _dot(onehot, xn_ref[i, t * tile:(t + 1) * tile, :])

    pl.when(fits)(lambda: step(gather_windowed))
    pl.when(jnp.logical_not(fits))(lambda: step(gather_dense))


def _expert_ffn(xn, pos, gate, starts, wg, wu, wd, layer, cap, bb, tile, stride):
    b, n, d = xn.shape
    wspec = pl.BlockSpec((1, 1, d, d), lambda e, j, st: (layer, e, 0, 0))
    sel = pl.BlockSpec((bb, N_EXPERTS, n), lambda e, j, st: (j, 0, 0))
    kern = functools.partial(_ffn_kernel, bb=bb, cap=cap, n=n, tile=tile, stride=stride)
    return pl.pallas_call(
        kern,
        out_shape=jax.ShapeDtypeStruct((b, N_EXPERTS, cap, d), BF16),
        grid_spec=pltpu.PrefetchScalarGridSpec(
            num_scalar_prefetch=1,
            grid=(N_EXPERTS, b // bb),
            in_specs=[pl.BlockSpec((bb, n, d), lambda e, j, st: (j, 0, 0)), sel, sel, wspec, wspec, wspec],
            out_specs=pl.BlockSpec((bb, 1, cap, d), lambda e, j, st: (j, e, 0, 0)),
            scratch_shapes=[pltpu.VMEM((d, d), BF16)] * 3 + [pltpu.VMEM((bb * cap, d), F32)]),
        compiler_params=_params("arbitrary", "arbitrary"),
        name="expert_ffn",
    )(starts, xn, pos, gate, wg, wu, wd)


def _scatter_kernel(st_ref, x_ref, mod_ref, pt_ref, o_ref, out_ref, *, tn, cap, nt, sub, window):
    for s in range(sub):
        _scatter_tile(st_ref, x_ref.at[0, s * tn:(s + 1) * tn], mod_ref, pt_ref.at[0, s * tn:(s + 1) * tn], o_ref,
                      out_ref.at[0, s * tn:(s + 1) * tn], pl.program_id(1) * sub + s,
                      tn=tn, cap=cap, nt=nt, window=window)


def _scatter_tile(st_ref, x_ref, mod_ref, pt_ref, o_ref, out_ref, t, *, tn, cap, nt, window):
    pos_t = pt_ref[...]
    g = mod_ref[0][:, 5 * D_MODEL:6 * D_MODEL]

    def dense():
        if cap % LANES == 0:
            slot = lax.broadcasted_iota(I32, (tn, cap), 1)
            onehot = jnp.concatenate(
                [jnp.where(pos_t[:, e:e + 1] == slot, 1.0, 0.0).astype(BF16) for e in range(N_EXPERTS)], axis=1)
        else:
            slot = lax.broadcasted_iota(I32, (tn, N_EXPERTS * cap), 1)
            acc = jnp.zeros((tn, N_EXPERTS * cap), F32)
            for e in range(N_EXPERTS):
                pe = pos_t[:, e:e + 1]
                acc = jnp.where(jnp.where(pe >= 0, pe + e * cap, -1) == slot, 1.0, acc)
            onehot = acc.astype(BF16)
        y = _dot(onehot, o_ref[0].reshape(N_EXPERTS * cap, D_MODEL))
        out_ref[...] = x_ref[...] + g * y

    if cap <= window:
        dense()
        return

    base = (pl.program_id(0) * nt + t) * N_EXPERTS
    nxt = jnp.minimum(t + 1, nt - 1)
    nbase = (pl.program_id(0) * nt + nxt) * N_EXPERTS
    win, fits = [], True
    for e in range(N_EXPERTS):
        start = st_ref[base + e]
        end = jnp.where(t + 1 < nt, st_ref[nbase + e], cap)
        a = jnp.minimum((start >> 4) << 4, cap - window)
        win.append(a)
        fits = jnp.logical_and(fits, end - a <= window)

    @pl.when(fits)
    def _():
        per = LANES // window
        lane = lax.broadcasted_iota(I32, (tn, LANES), 1)
        groups = []
        for e0 in range(0, N_EXPERTS, per):
            target = jnp.full((tn, LANES), -1, I32)
            for i in range(per):
                rel = pos_t[:, e0 + i:e0 + i + 1] - win[e0 + i]
                here = (lane >> (window.bit_length() - 1)) == i if per > 1 else None
                shifted = jnp.where(rel >= 0, rel + i * window, -1)
                target = shifted if here is None else jnp.where(here, shifted, target)
            groups.append(jnp.where(target == lane, 1.0, 0.0).astype(BF16))
        onehot = jnp.concatenate(groups, axis=1)
        rows = jnp.concatenate(
            [o_ref[0, e, pl.ds(pl.multiple_of(win[e], 16), window), :] for e in range(N_EXPERTS)], axis=0)
        out_ref[...] = x_ref[...] + g * _dot(onehot, rows)

    pl.when(jnp.logical_not(fits))(dense)


def _scatter(x, mod, pos_t, starts, o, cap, tn, window, sub):
    b, n, d = x.shape
    nt = n // tn
    xs = pl.BlockSpec((1, sub * tn, d), lambda i, t, st: (i, t, 0))
    kern = functools.partial(_scatter_kernel, tn=tn, cap=cap, nt=nt, sub=sub, window=window)
    return pl.pallas_call(
        kern,
        out_shape=jax.ShapeDtypeStruct((b, n, d), F32),
        grid_spec=pltpu.PrefetchScalarGridSpec(
            num_scalar_prefetch=1,
            grid=(b, nt // sub),
            in_specs=[xs,
                      pl.BlockSpec((1, 1, 6 * d), lambda i, t, st: (i, 0, 0)),
                      pl.BlockSpec((1, sub * tn, LANES), lambda i, t, st: (i, t, 0)),
                      pl.BlockSpec((1, N_EXPERTS, cap, d), lambda i, t, st: (i, 0, 0, 0))],
            out_specs=xs),
        compiler_params=_params("arbitrary", "arbitrary"),
        name="scatter_add",
    )(starts, x, mod, pos_t, o)


@functools.lru_cache(maxsize=None)
def _np_consts():
    lane = np.arange(GROUP_W)
    seg32 = (lane[:, None] // DIFF_D == lane[None, :] // DIFF_D).astype(np.float32)
    seg64 = (lane[:, None] // HEAD_DIM == lane[None, :] // HEAD_DIM).astype(np.float32)
    ang = 2.0 * np.pi * ((lane[:, None] % HEAD_DIM) * (lane[None, :] % HEAD_DIM) % HEAD_DIM) / HEAD_DIM
    cc = np.cos(ang) * seg64
    ss = np.sin(ang) * seg64
    tri = (lane[:, None] <= lane[None, :]).astype(np.float32)
    return dict(seg32=seg32, seg64=seg64, cc=cc, ss=ss, tri=tri)


@functools.lru_cache(maxsize=None)
def _np_dft(n):
    idx = (np.arange(n, dtype=np.int64)[:, None] * np.arange(n, dtype=np.int64)[None, :]) % n
    ang = 2.0 * np.pi * idx.astype(np.float64) / n
    return np.cos(ang).astype(np.float32), np.sin(ang).astype(np.float32)


@functools.lru_cache(maxsize=None)
def _np_rope(n):
    half = DIFF_D // 2
    inv = 1.0 / (ROPE_BASE ** (np.arange(0, half, 2, dtype=np.float32) / half))
    t = np.arange(n)
    row = (t // GRID_W).astype(np.float32)[:, None] * inv
    col = (t % GRID_W).astype(np.float32)[:, None] * inv
    nf = inv.shape[0]
    d = np.arange(GROUP_W) % DIFF_D
    f = d % nf
    is_col = d >= half
    second = (d % half) >= nf
    ang = np.where(is_col[None, :], col[:, f], row[:, f])
    c = np.cos(ang).astype(np.float32)
    s = np.sin(ang).astype(np.float32)
    s = np.where(second[None, :], s, -s)
    return c, s


@functools.lru_cache(maxsize=None)
def _np_na_index(n_rows):
    n_steps = n_rows // NA_QROWS
    reps = [0, 1, 2, n_steps - 2, n_steps - 1]
    tq = NA_QROWS * GRID_W
    roff = np.zeros((len(reps), NA_QROWS, NA_WIN_ROWS), np.int32)
    valid = np.zeros((len(reps), tq, NA_WIN), bool)
    for ci, t in enumerate(reps):
        ws = int(np.clip(NA_QROWS * t - NA_ROWS // 2, 0, n_rows - NA_WIN_ROWS))
        qi = np.arange(tq)
        r = NA_QROWS * t + qi // GRID_W
        qcol = qi % GRID_W
        kk = np.arange(NA_WIN)
        krow = ws + kk // GRID_W
        kcol = kk % GRID_W
        rstart = np.clip(r - NA_ROWS // 2, 0, n_rows - NA_ROWS)
        wstart = np.clip(qcol - NA_COLS // 2, 0, GRID_W - NA_COLS)
        vr = (krow[None, :] >= rstart[:, None]) & (krow[None, :] < rstart[:, None] + NA_ROWS)
        vc = (kcol[None, :] >= wstart[:, None]) & (kcol[None, :] < wstart[:, None] + NA_COLS)
        valid[ci] = vr & vc
        rows_q = NA_QROWS * t + np.arange(NA_QROWS)
        rows_k = ws + np.arange(NA_WIN_ROWS)
        roff[ci] = np.clip(rows_k[None, :] - rows_q[:, None] + NA_ROWS - 1, 0, 2 * NA_ROWS - 2)
    return roff, valid


def _na_table_kernel(r_ref, mask_ref, o_ref, toep_ref, *, roff):
    n_off = 2 * NA_ROWS - 1
    for h in range(HEADS):
        for ro in range(n_off):
            row = jnp.broadcast_to(r_ref[0, h, ro:ro + 1, :], (GRID_W, LANES))
            toep_ref[h, ro] = pltpu.roll(row, 0, 1, stride=1, stride_axis=0)
    lane = lax.broadcasted_iota(I32, (GRID_W, LANES), 1)
    n_cls = roff.shape[0]
    for ci in range(n_cls):
        for h in range(HEADS):
            for rr in range(NA_QROWS):
                r0 = h * NA_QROWS * GRID_W + rr * GRID_W
                for kp in range(NA_WIN_ROWS // 2):
                    left = toep_ref[h, int(roff[ci, rr, 2 * kp])]
                    right = pltpu.roll(toep_ref[h, int(roff[ci, rr, 2 * kp + 1])], GRID_W, 1)
                    bias = jnp.where(lane < GRID_W, left, right) * LOG2E
                    o_ref[0, ci, r0:r0 + GRID_W, kp * LANES:(kp + 1) * LANES] = (
                        bias + mask_ref[ci, rr * GRID_W:(rr + 1) * GRID_W, kp * LANES:(kp + 1) * LANES])


def _na_tables(rpb_all, n_rows):
    roff, valid = _np_na_index(n_rows)
    n_cls = roff.shape[0]
    depth = rpb_all.shape[0]
    n_off = 2 * NA_ROWS - 1
    r = jnp.concatenate([rpb_all[..., NA_COLS - 1:], jnp.zeros(rpb_all.shape[:-1] + (LANES - 2 * NA_COLS + 1,), F32),
                         rpb_all[..., :NA_COLS - 1]], axis=-1).astype(F32)
    mask = jnp.asarray(np.where(valid, 0.0, NEG_INF).astype(np.float32))
    tq = NA_QROWS * GRID_W
    kern = functools.partial(_na_table_kernel, roff=roff)
    return pl.pallas_call(
        kern,
        out_shape=jax.ShapeDtypeStruct((depth, n_cls, HEADS * tq, NA_WIN), F32),
        grid=(depth,),
        in_specs=[pl.BlockSpec((1, HEADS, n_off, LANES), lambda l: (l, 0, 0, 0)),
                  _full((n_cls, tq, NA_WIN))],
        out_specs=pl.BlockSpec((1, n_cls, HEADS * tq, NA_WIN), lambda l: (l, 0, 0, 0)),
        scratch_shapes=[pltpu.VMEM((HEADS, n_off, GRID_W, LANES), F32)],
        compiler_params=_params("arbitrary"),
        name="na_bias_table",
    )(r, mask)


class _Tiles(NamedTuple):
    proj: int
    attn: int
    na_pairs: int
    moe: int
    ffn_samples: int


def _latent_tiles(n):
    return _Tiles(proj=min(n, 1024), attn=min(n, 512), na_pairs=min(8, n // (NA_QROWS * GRID_W)),
                  moe=min(n, 512), ffn_samples=2)
def _moe(x, xn, logits, mod, tri, wg, wu, wd, layer, bb, tn):
    n = x.shape[1]
    cap = CAPACITY_FACTOR * n // N_EXPERTS
    stride = 2 if cap > SLOT_WINDOW else 1
    ts = tn // stride
    pos, gate, pos_t, starts = _route(logits, tri, cap, ts)
    starts = starts[:, :, :n // ts].transpose(0, 2, 1).reshape(-1)
    o = _expert_ffn(xn, pos, gate, starts, wg, wu, wd, layer, cap, bb, tn, stride)
    return _scatter(x, mod, pos_t, starts, o, cap, ts, SLOT_WINDOW // stride, min(4, n // ts))


def kernel(x, c, ctx, c_ctx, ada_w, ada_b, norm1_g, norm2_g, w_in, w_out, head_out_g, sgu_w, sgu_b, diff_qn_g, diff_kn_g, diff_lambda, na_qn_g, na_kn_g, na_rpb, router_w, exp_w_gate, exp_w_up, exp_w_down):
    b, n, d = x.shape
    n_ctx = ctx.shape[1]
    npc = _np_consts()
    consts = {k: jnp.asarray(v, F32).astype(BF16) for k, v in npc.items()}
    seg64, tri = consts["seg64"], consts["tri"]
    cn, sn = (jnp.asarray(a, F32).astype(BF16) for a in _np_dft(n))
    cn_c, sn_c = (jnp.asarray(a, F32).astype(BF16) for a in _np_dft(n_ctx))
    rope = tuple(jnp.asarray(a, F32) for a in _np_rope(n))

    pad = (-(b + 1)) % 8
    c_rows = jnp.concatenate([c, c_ctx[None, :], jnp.zeros((pad, d), F32)], axis=0)
    mod_all = _modulation(c_rows, ada_w, ada_b)
    na_tables = _na_tables(na_rpb, n // GRID_W)

    xc = ctx
    for l in range(DEPTH):
        last = l == DEPTH - 1
        lam_init = 0.8 - 0.6 * math.exp(-0.3 * l)
        mod = mod_all[l, :b][:, None, :]
        mod_c = jnp.broadcast_to(mod_all[l, b][None, None, :], (b, 1, 6 * d))
        g1 = norm1_g[l][None, :]
        g2 = norm2_g[l][None, :]
        w_in_l = w_in[l].astype(BF16)
        w_out_l = w_out[l].astype(BF16)
        sguw = sgu_w[l].astype(BF16).transpose(1, 0, 2).reshape(CHUNK, HEADS * CHUNK)
        sgub = jnp.repeat(sgu_b[l].T, HEAD_DIM, axis=1)
        hg = head_out_g[l].reshape(4, GROUP_W)
        vec = jnp.stack([jnp.tile(diff_qn_g[l], GROUP_W // DIFF_D), jnp.tile(diff_kn_g[l], GROUP_W // DIFF_D),
                         jnp.tile(na_qn_g[l], HEADS), jnp.tile(na_kn_g[l], HEADS),
                         hg[0], hg[1], hg[2], hg[3],
                         jnp.broadcast_to(jnp.max(jnp.abs(na_rpb[l])) * LOG2E, (GROUP_W,))], axis=0).astype(F32)
        vec = jnp.pad(vec, ((0, VEC_ROWS - vec.shape[0]), (0, 0)))
        lam = diff_lambda[l].astype(F32)
        rwt = router_w[l].T.astype(BF16)
        experts = (exp_w_gate, exp_w_up, exp_w_down)

        tl = _latent_tiles(n)
        ya, zc, zs, qc, kct, vc, qd, kdt, vd = _in_proj(x, mod, g1, w_in_l, consts, sguw, sgub, vec, rope, tl.proj)
        if last:
            kct_c, vc_c, kdt_c, vd_c = _in_proj(
                xc, mod_c, g1, w_in_l, consts, sguw, sgub, vec, None, n_ctx, kv_only=True)
        else:
            ya_c, zc_c, zs_c, qc_c, kct_c, vc_c, qd_c, kdt_c, vd_c = _in_proj(
                xc, mod_c, g1, w_in_l, consts, sguw, sgub, vec, None, n_ctx)

        yb = _fourier(zc, zs, cn, sn, seg64, vec, tl.proj)
        yc = _attention(qc, [(kct, vc), (kct_c, vc_c)], lam, lam_init, seg64, vec, 6, 1.0 - lam_init,
                        tl.attn, tl.attn)
        yd = _na_attention(qd, kdt, vd, kdt_c, vd_c, na_tables, l, seg64, vec, tl.na_pairs)
        x, xn, logits = _out_proj(ya, yb, yc, yd, w_out_l, x, mod, g2, rwt, tl.proj)
        x = _moe(x, xn, logits, mod, tri, *experts, l, tl.ffn_samples, tl.moe)

        if not last:
            yb_c = _fourier(zc_c, zs_c, cn_c, sn_c, seg64, vec, n_ctx)
            yc_c = _attention(qc_c, [(kct_c, vc_c)], lam, lam_init, seg64, vec, 6, 1.0 - lam_init, n_ctx)
            yd_c = _attention(qd_c, [(kdt_c, vd_c)], None, 0.0, seg64, vec, 7, 1.0, n_ctx)
            xc, xn_c, logits_c = _out_proj(ya_c, yb_c, yc_c, yd_c, w_out_l, xc, mod_c, g2, rwt, n_ctx)
            xc = _moe(xc, xn_c, logits_c, mod_c, tri, *experts, l, b, n_ctx)
    return x
```

```python
import functools
import math
from typing import NamedTuple

import numpy as np
import jax
import jax.numpy as jnp
from jax import lax
from jax.experimental import pallas as pl
from jax.experimental.pallas import tpu as pltpu

F32 = jnp.float32
BF16 = jnp.bfloat16
I32 = jnp.int32

D_MODEL = 1024
DEPTH = 2
GRID_W = 64
LANES = 128
HEAD_DIM = 64
LOG2_HEAD_DIM = 6
GROUP_W = 256
HEADS = GROUP_W // HEAD_DIM
CHUNK = 128
DIFF_D = HEAD_DIM // 2
NA_ROWS = 8
NA_COLS = 16
N_EXPERTS = 16
CAPACITY_FACTOR = 2
ROPE_BASE = 10000.0
EPS = 1e-6
IN_W = 9 * GROUP_W
LOG2E = 1.4426950408889634

VMEM_LIMIT_BYTES = 56 * 1024 * 1024
NA_QROWS = 2
NA_WIN_ROWS = NA_ROWS + 2
NA_WIN = NA_WIN_ROWS * GRID_W
NA_HEAD_STACK = 1
NEG_INF = float("-inf")
VEC_ROWS = 16
SLOT_WINDOW = 128
SAFE_EXP2_BOUND = 48.0
BOUND_SLACK = 1.02


def _dot(a, b):
    return jnp.dot(a, b, preferred_element_type=F32)


def _params(*sem):
    return pltpu.CompilerParams(dimension_semantics=sem, vmem_limit_bytes=VMEM_LIMIT_BYTES)


def _full(shape):
    nd = len(shape)
    return pl.BlockSpec(shape, lambda *_: (0,) * nd)


def _seg_rms(x, seg, width):
    ss = _dot((x * x).astype(BF16), seg)
    return x * lax.rsqrt(ss * (1.0 / width) + EPS)


def _mod_kernel(c_ref, w_ref, b_ref, o_ref):
    s = jax.nn.silu(c_ref[...]).astype(BF16)
    o_ref[0] = _dot(s, w_ref[0].astype(BF16)) + b_ref[0]


def _modulation(c_rows, ada_w, ada_b):
    depth, d, w6 = ada_w.shape
    r = c_rows.shape[0]
    tn = 1024
    return pl.pallas_call(
        _mod_kernel,
        out_shape=jax.ShapeDtypeStruct((depth, r, w6), F32),
        grid=(depth, w6 // tn),
        in_specs=[
            pl.BlockSpec((r, d), lambda l, j: (0, 0)),
            pl.BlockSpec((1, d, tn), lambda l, j: (l, 0, j)),
            pl.BlockSpec((1, 1, tn), lambda l, j: (l, 0, j)),
        ],
        out_specs=pl.BlockSpec((1, r, tn), lambda l, j: (l, 0, j)),
        compiler_params=_params("arbitrary", "arbitrary"),
        name="modulation",
    )(c_rows, ada_w, ada_b.reshape(depth, 1, w6))


def _rope(x, c, s, lane):
    fwd = pltpu.roll(x, GROUP_W - 8, 1)
    bwd = pltpu.roll(x, 8, 1)
    partner = jnp.where((lane & 8) == 0, fwd, bwd)
    return x * c + partner * s


def _in_kernel(*refs, tm, use_rope, kv_only, c_scale, d_scale):
    (x_ref, mod_ref, g1_ref, w_ref, seg32_ref, seg64_ref, cc_ref, ss_ref, sguw_ref, sgub_ref, vec_ref) = refs[:11]
    rest = refs[11:]
    if use_rope:
        ropec_ref, ropes_ref = rest[:2]
        rest = rest[2:]
    if kv_only:
        kct_ref, vc_ref, kdt_ref, vd_ref = rest
    else:
        ya_ref, zc_ref, zs_ref, qc_ref, kct_ref, vc_ref, qd_ref, kdt_ref, vd_ref = rest

    x = x_ref[0]
    mod = mod_ref[0]
    sh = mod[:, 0:D_MODEL]
    sc = mod[:, D_MODEL:2 * D_MODEL]
    ms = jnp.mean(x * x, axis=-1, keepdims=True)
    h = x * lax.rsqrt(ms + EPS) * g1_ref[...]
    h = (h * (1.0 + sc) + sh).astype(BF16)
    lane = lax.broadcasted_iota(I32, (1, GROUP_W), 1)
    head = lane >> LOG2_HEAD_DIM
    seg32 = seg32_ref[...]
    seg64 = seg64_ref[...]
    vec = vec_ref[...]

    if kv_only:
        pc = _dot(h, w_ref[:, 4 * GROUP_W:6 * GROUP_W])
        k = _seg_rms(pc[:, 0:GROUP_W], seg32, DIFF_D) * vec[1:2]
        kct_ref[0] = k.T.astype(BF16)
        vc_ref[0] = pc[:, GROUP_W:2 * GROUP_W].astype(BF16)
        pd = _dot(h, w_ref[:, 7 * GROUP_W:9 * GROUP_W])
        kd = _seg_rms(pd[:, 0:GROUP_W], seg64, HEAD_DIM) * vec[3:4]
        kdt_ref[0] = kd.T.astype(BF16)
        vd_ref[0] = pd[:, GROUP_W:2 * GROUP_W].astype(BF16)
        return

    z = jax.nn.gelu(_dot(h, w_ref[:, 0:2 * GROUP_W]))
    u = z[:, 0:GROUP_W]
    vn = _seg_rms(z[:, GROUP_W:2 * GROUP_W], seg64, HEAD_DIM).astype(BF16)
    rows = []
    for c in range(tm // CHUNK):
        vch = vn[c * CHUNK:(c + 1) * CHUNK]
        stacked = jnp.concatenate([jnp.where(head == hh, vch, jnp.zeros_like(vch)) for hh in range(HEADS)], axis=0)
        rows.append(_dot(sguw_ref[...], stacked) + sgub_ref[...])
    ya = u * jnp.concatenate(rows, axis=0)
    ya_ref[0] = (_seg_rms(ya, seg64, HEAD_DIM) * vec[4:5]).astype(BF16)

    zb = _dot(h, w_ref[:, 2 * GROUP_W:3 * GROUP_W]).astype(BF16)
    zc_ref[0] = _dot(zb, cc_ref[...]).astype(BF16)
    zs_ref[0] = _dot(zb, ss_ref[...]).astype(BF16)

    pc = _dot(h, w_ref[:, 3 * GROUP_W:6 * GROUP_W])
    q = _seg_rms(pc[:, 0:GROUP_W], seg32, DIFF_D) * vec[0:1]
    k = _seg_rms(pc[:, GROUP_W:2 * GROUP_W], seg32, DIFF_D) * vec[1:2]
    if use_rope:
        rc = ropec_ref[...]
        rs = ropes_ref[...]
        q = _rope(q, rc, rs, lane)
        k = _rope(k, rc, rs, lane)
    qc_ref[0] = (q * c_scale).astype(BF16)
    kct_ref[0] = k.T.astype(BF16)
    vc_ref[0] = pc[:, 2 * GROUP_W:3 * GROUP_W].astype(BF16)

    pd = _dot(h, w_ref[:, 6 * GROUP_W:9 * GROUP_W])
    qd = _seg_rms(pd[:, 0:GROUP_W], seg64, HEAD_DIM) * vec[2:3]
    kd = _seg_rms(pd[:, GROUP_W:2 * GROUP_W], seg64, HEAD_DIM) * vec[3:4]
    qd_ref[0] = (qd * d_scale).astype(BF16)
    kdt_ref[0] = kd.T.astype(BF16)
    vd_ref[0] = pd[:, 2 * GROUP_W:3 * GROUP_W].astype(BF16)


def _in_proj(x, mod, g1, w_in, consts, sguw, sgub, vec, rope, tm, kv_only=False):
    b, n, d = x.shape
    use_rope = rope is not None
    tok = pl.BlockSpec((1, tm, GROUP_W), lambda i, t: (i, t, 0))
    tok_t = pl.BlockSpec((1, GROUP_W, tm), lambda i, t: (i, 0, t))
    in_specs = [
        pl.BlockSpec((1, tm, d), lambda i, t: (i, t, 0)),
        pl.BlockSpec((1, 1, 6 * d), lambda i, t: (i, 0, 0)),
        _full((1, d)),
        _full((d, IN_W)),
        _full((GROUP_W, GROUP_W)), _full((GROUP_W, GROUP_W)), _full((GROUP_W, GROUP_W)), _full((GROUP_W, GROUP_W)),
        _full((CHUNK, HEADS * CHUNK)),
        _full((CHUNK, GROUP_W)),
        _full((VEC_ROWS, GROUP_W)),
    ]
    args = [x, mod, g1, w_in, consts["seg32"], consts["seg64"], consts["cc"], consts["ss"], sguw, sgub, vec]
    if use_rope:
        in_specs += [pl.BlockSpec((tm, GROUP_W), lambda i, t: (t, 0))] * 2
        args += list(rope)
    sd = jax.ShapeDtypeStruct((b, n, GROUP_W), BF16)
    sdt = jax.ShapeDtypeStruct((b, GROUP_W, n), BF16)
    kern = functools.partial(
        _in_kernel, tm=tm, use_rope=use_rope, kv_only=kv_only,
        c_scale=(DIFF_D ** -0.5) * LOG2E, d_scale=(HEAD_DIM ** -0.5) * LOG2E)
    return pl.pallas_call(
        kern,
        out_shape=(sdt, sd, sdt, sd) if kv_only else (sd, sd, sd, sd, sdt, sd, sd, sdt, sd),
        grid=(b, n // tm),
        in_specs=in_specs,
        out_specs=(tok_t, tok, tok_t, tok) if kv_only else (tok, tok, tok, tok, tok_t, tok, tok, tok_t, tok),
        compiler_params=_params("arbitrary", "arbitrary"),
        name="in_proj",
    )(*args)


def _fourier_kernel(cn_ref, sn_ref, zc_ref, zs_ref, seg64_ref, vec_ref, o_ref, *, norm):
    y = (_dot(cn_ref[...], zc_ref[0]) - _dot(sn_ref[...], zs_ref[0])) * norm
    o_ref[0] = (_seg_rms(y, seg64_ref[...], HEAD_DIM) * vec_ref[5:6]).astype(BF16)


def _fourier(zc, zs, cn, sn, seg64, vec, tn):
    b, n, _ = zc.shape
    kern = functools.partial(_fourier_kernel, norm=1.0 / math.sqrt(n * HEAD_DIM))
    return pl.pallas_call(
        kern,
        out_shape=jax.ShapeDtypeStruct((b, n, GROUP_W), BF16),
        grid=(n // tn, b),
        in_specs=[
            pl.BlockSpec((tn, n), lambda t, i: (t, 0)),
            pl.BlockSpec((tn, n), lambda t, i: (t, 0)),
            pl.BlockSpec((1, n, GROUP_W), lambda t, i: (i, 0, 0)),
            pl.BlockSpec((1, n, GROUP_W), lambda t, i: (i, 0, 0)),
            _full((GROUP_W, GROUP_W)),
            _full((VEC_ROWS, GROUP_W)),
        ],
        out_specs=pl.BlockSpec((1, tn, GROUP_W), lambda t, i: (i, t, 0)),
        compiler_params=_params("arbitrary", "arbitrary"),
        name="fourier",
    )(cn, sn, zc, zs, seg64, vec)


def _attn_kernel(*refs, n_src, diff, lam_init, chunk, tq, vec_row, out_scale):
    q_ref = refs[0]
    rest = refs[1 + 2 * n_src:]
    if diff:
        lam_ref = rest[0]
        rest = rest[1:]
    seg64_ref, vec_ref, o_ref = rest

    q = q_ref[0]
    lane = lax.broadcasted_iota(I32, (1, GROUP_W), 1)
    if diff:
        lf = lam_ref[...]
        lam = (jnp.exp(jnp.sum(lf[0:1] * lf[1:2], axis=-1, keepdims=True))
               - jnp.exp(jnp.sum(lf[2:3] * lf[3:4], axis=-1, keepdims=True)) + lam_init)

    chunks = []
    for i in range(n_src):
        kt_ref, v_ref = refs[1 + 2 * i], refs[2 + 2 * i]
        nk = kt_ref.shape[2]
        chunks += [(kt_ref, v_ref, c0, min(chunk, nk - c0)) for c0 in range(0, nk, chunk)]

    width = DIFF_D if diff else HEAD_DIM
    gq = vec_ref[0:1] if diff else vec_ref[2:3]
    gk = vec_ref[1:2] if diff else vec_ref[3:4]
    bound = jnp.max(jnp.abs(gq)) * jnp.max(jnp.abs(gk)) * (math.sqrt(width) * LOG2E * BOUND_SLACK)

    def attend_bounded(sel):
        qm = jnp.where(sel, q, jnp.zeros_like(q))
        l = jnp.zeros((tq, 1), F32)
        acc = jnp.zeros((tq, GROUP_W), F32)
        for kt_ref, v_ref, c0, ck in chunks:
            p = jnp.exp2(_dot(qm, kt_ref[0, :, c0:c0 + ck]))
            l = l + jnp.sum(p, axis=-1, keepdims=True)
            acc = acc + _dot(p.astype(BF16), v_ref[0, c0:c0 + ck, :])
        return acc * (1.0 / l)

    def attend_online(sel):
        qm = jnp.where(sel, q, jnp.zeros_like(q))
        m = jnp.full((tq, 1), NEG_INF, F32)
        l = jnp.zeros((tq, 1), F32)
        acc = jnp.zeros((tq, GROUP_W), F32)
        for kt_ref, v_ref, c0, ck in chunks:
            s = _dot(qm, kt_ref[0, :, c0:c0 + ck])
            m_new = jnp.maximum(m, jnp.max(s, axis=-1, keepdims=True))
            alpha = jnp.exp2(m - m_new)
            p = jnp.exp2(s - m_new)
            l = alpha * l + jnp.sum(p, axis=-1, keepdims=True)
            acc = alpha * acc + _dot(p.astype(BF16), v_ref[0, c0:c0 + ck, :])
            m = m_new
        return acc * (1.0 / l)

    def head_out(attend, h):
        if diff:
            return (attend((lane >> (LOG2_HEAD_DIM - 1)) == 2 * h)
                    - lam * attend((lane >> (LOG2_HEAD_DIM - 1)) == 2 * h + 1))
        return attend((lane >> LOG2_HEAD_DIM) == h)

    def finish(out):
        y = _seg_rms(out, seg64_ref[...], HEAD_DIM) * vec_ref[vec_row:vec_row + 1]
        o_ref[0] = (y * out_scale).astype(BF16)

    def run_bounded():
        out = jnp.zeros((tq, GROUP_W), F32)
        for h in range(HEADS):
            out = jnp.where((lane >> LOG2_HEAD_DIM) == h, head_out(attend_bounded, h), out)
        finish(out)

    def run_online():
        def body(h, out):
            return jnp.where((lane >> LOG2_HEAD_DIM) == h, head_out(attend_online, h), out)

        finish(lax.fori_loop(0, HEADS, body, jnp.zeros((tq, GROUP_W), F32)))

    small = bound <= SAFE_EXP2_BOUND
    pl.when(small)(run_bounded)
    pl.when(jnp.logical_not(small))(run_online)


def _attention(q, srcs, lam, lam_init, seg64, vec, vec_row, out_scale, tq, chunk=768):
    b, nq, _ = q.shape
    diff = lam is not None
    in_specs = [pl.BlockSpec((1, tq, GROUP_W), lambda i, t: (i, t, 0))]
    args = [q]
    for kt, v in srcs:
        nk = v.shape[1]
        in_specs += [pl.BlockSpec((1, GROUP_W, nk), lambda i, t: (i, 0, 0)),
                     pl.BlockSpec((1, nk, GROUP_W), lambda i, t: (i, 0, 0))]
        args += [kt, v]
    if diff:
        in_specs.append(_full((4, DIFF_D)))
        args.append(lam)
    in_specs += [_full((GROUP_W, GROUP_W)), _full((VEC_ROWS, GROUP_W))]
    args += [seg64, vec]
    kern = functools.partial(_attn_kernel, n_src=len(srcs), diff=diff, lam_init=lam_init, chunk=chunk, tq=tq,
                             vec_row=vec_row, out_scale=out_scale)
    return pl.pallas_call(
        kern,
        out_shape=jax.ShapeDtypeStruct((b, nq, GROUP_W), BF16),
        grid=(b, nq // tq),
        in_specs=in_specs,
        out_specs=pl.BlockSpec((1, tq, GROUP_W), lambda i, t: (i, t, 0)),
        compiler_params=_params("arbitrary", "arbitrary"),
        name="diff_attention" if diff else "ctx_attention",
    )(*args)


def _na_kernel(q_ref, kt_ref, v_ref, ktc_ref, vc_ref, tab_ref, seg64_ref, vec_ref, o_ref, *, n_rows, pairs):
    n_steps = n_rows // NA_QROWS
    tq = NA_QROWS * GRID_W
    lane = lax.broadcasted_iota(I32, (1, GROUP_W), 1)
    head = lane >> LOG2_HEAD_DIM
    ktc = ktc_ref[0]
    vc = vc_ref[0]
    bound = (jnp.max(jnp.abs(vec_ref[2:3])) * jnp.max(jnp.abs(vec_ref[3:4])) * (math.sqrt(HEAD_DIM) * LOG2E * BOUND_SLACK)
             + jnp.max(vec_ref[8:9]))

    def run(bounded):
        for pi in range(pairs):
            t = pl.program_id(1) * pairs + pi
            ws = jnp.clip(NA_QROWS * t - NA_ROWS // 2, 0, n_rows - NA_WIN_ROWS)
            k0 = pl.multiple_of(ws * GRID_W, 128)
            tid = jnp.where(t < 2, t, jnp.where(t < n_steps - 2, 2, t - (n_steps - 5)))
            q = q_ref[0, pi * tq:(pi + 1) * tq, :]
            out = jnp.zeros((tq, GROUP_W), F32)
            for h0 in range(0, HEADS, NA_HEAD_STACK):
                hs = range(h0, h0 + NA_HEAD_STACK)
                qs = jnp.concatenate([jnp.where(head == hh, q, jnp.zeros_like(q)) for hh in hs], axis=0)
                s_loc = (_dot(qs, kt_ref[0, :, pl.ds(k0, NA_WIN)])
                         + tab_ref[0, tid, h0 * tq:(h0 + NA_HEAD_STACK) * tq, :])
                s_ctx = _dot(qs, ktc)
                if bounded:
                    p_loc = jnp.exp2(s_loc)
                    p_ctx = jnp.exp2(s_ctx)
                else:
                    m = jnp.maximum(jnp.max(s_loc, axis=-1, keepdims=True), jnp.max(s_ctx, axis=-1, keepdims=True))
                    p_loc = jnp.exp2(s_loc - m)
                    p_ctx = jnp.exp2(s_ctx - m)
                l = jnp.sum(p_loc, axis=-1, keepdims=True) + jnp.sum(p_ctx, axis=-1, keepdims=True)
                o = _dot(p_loc.astype(BF16), v_ref[0, pl.ds(k0, NA_WIN), :]) + _dot(p_ctx.astype(BF16), vc)
                o = o * (1.0 / l)
                for i, hh in enumerate(hs):
                    out = jnp.where(head == hh, o[i * tq:(i + 1) * tq], out)
            o_ref[0, pi * tq:(pi + 1) * tq, :] = (
                _seg_rms(out, seg64_ref[...], HEAD_DIM) * vec_ref[7:8]).astype(BF16)

    small = bound <= SAFE_EXP2_BOUND
    pl.when(small)(lambda: run(True))
    pl.when(jnp.logical_not(small))(lambda: run(False))


def _na_attention(q, kt, v, ktc, vc, tables, layer, seg64, vec, pairs):
    b, n, _ = q.shape
    nc = vc.shape[1]
    tq = pairs * NA_QROWS * GRID_W
    kern = functools.partial(_na_kernel, n_rows=n // GRID_W, pairs=pairs)
    return pl.pallas_call(
        kern,
        out_shape=jax.ShapeDtypeStruct((b, n, GROUP_W), BF16),
        grid=(b, n // tq),
        in_specs=[
            pl.BlockSpec((1, tq, GROUP_W), lambda i, t: (i, t, 0)),
            pl.BlockSpec((1, GROUP_W, n), lambda i, t: (i, 0, 0)),
            pl.BlockSpec((1, n, GROUP_W), lambda i, t: (i, 0, 0)),
            pl.BlockSpec((1, GROUP_W, nc), lambda i, t: (i, 0, 0)),
            pl.BlockSpec((1, nc, GROUP_W), lambda i, t: (i, 0, 0)),
            pl.BlockSpec((1,) + tables.shape[1:], lambda i, t: (layer, 0, 0, 0)),
            _full((GROUP_W, GROUP_W)),
            _full((VEC_ROWS, GROUP_W)),
        ],
        out_specs=pl.BlockSpec((1, tq, GROUP_W), lambda i, t: (i, t, 0)),
        compiler_params=_params("arbitrary", "arbitrary"),
        name="neighbourhood_attention",
    )(q, kt, v, ktc, vc, tables, seg64, vec)


def _out_kernel(ya_ref, yb_ref, yc_ref, yd_ref, w_ref, x_ref, mod_ref, g2_ref, rwt_ref, o_ref, xn_ref, lg_ref):
    y = jnp.concatenate([ya_ref[0], yb_ref[0], yc_ref[0], yd_ref[0]], axis=-1)
    mod = mod_ref[0]
    x = x_ref[0] + mod[:, 2 * D_MODEL:3 * D_MODEL] * _dot(y, w_ref[...])
    o_ref[0] = x
    sh = mod[:, 3 * D_MODEL:4 * D_MODEL]
    sc = mod[:, 4 * D_MODEL:5 * D_MODEL]
    ms = jnp.mean(x * x, axis=-1, keepdims=True)
    h = x * lax.rsqrt(ms + EPS) * g2_ref[...]
    h = (h * (1.0 + sc) + sh).astype(BF16)
    xn_ref[0] = h
    lg_ref[0] = lax.dot_general(rwt_ref[...], h, (((1,), (1,)), ((), ())), preferred_element_type=F32)


def _out_proj(ya, yb, yc, yd, w_out, x, mod, g2, rwt, tm):
    b, n, d = x.shape
    tok = pl.BlockSpec((1, tm, GROUP_W), lambda i, t: (i, t, 0))
    xs = pl.BlockSpec((1, tm, d), lambda i, t: (i, t, 0))
    return pl.pallas_call(
        _out_kernel,
        out_shape=(jax.ShapeDtypeStruct((b, n, d), F32),
                   jax.ShapeDtypeStruct((b, n, d), BF16),
                   jax.ShapeDtypeStruct((b, N_EXPERTS, n), F32)),
        grid=(b, n // tm),
        in_specs=[tok, tok, tok, tok, _full((4 * GROUP_W, d)), xs,
                  pl.BlockSpec((1, 1, 6 * d), lambda i, t: (i, 0, 0)),
                  _full((1, d)), _full((N_EXPERTS, d))],
        out_specs=(xs, xs, pl.BlockSpec((1, N_EXPERTS, tm), lambda i, t: (i, 0, t))),
        compiler_params=_params("arbitrary", "arbitrary"),
        name="out_proj",
    )(ya, yb, yc, yd, w_out, x, mod, g2, rwt)


def _cumsum_excl(m, tri):
    n = m.shape[1]
    carry = jnp.zeros((m.shape[0], 1), F32)
    outs = []
    for j in range(n // GROUP_W):
        blk = m[:, j * GROUP_W:(j + 1) * GROUP_W]
        inc = _dot(blk.astype(BF16), tri)
        outs.append(inc - blk + carry)
        carry = carry + inc[:, GROUP_W - 1:GROUP_W]
    return jnp.concatenate(outs, axis=1)


def _route_kernel(lg_ref, tri_ref, pos_ref, gate_ref, post_ref, st_ref, *, n, cap, tile, nb):
    affs = []
    for i in range(nb):
        lg = lg_ref[i]
        e = jnp.exp(lg - jnp.max(lg, axis=0, keepdims=True))
        affs.append(e / jnp.sum(e, axis=0, keepdims=True))
        gate_ref[i] = affs[i]
    aff = affs[0] if nb == 1 else jnp.concatenate(affs, axis=0)

    def unresolved(state):
        lo, hi = state
        return jnp.max(jnp.where(lo < hi, 1.0, 0.0)) > 0.0

    def bisect(state):
        lo, hi = state
        mid = 0.5 * (lo + hi)
        mid = jnp.where(mid > lo, mid, hi)
        ge = aff >= mid
        cnt = jnp.sum(jnp.where(ge, 1.0, 0.0), axis=1, keepdims=True)
        least_ge = jnp.min(jnp.where(ge, aff, jnp.inf), axis=1, keepdims=True)
        most_lt = jnp.max(jnp.where(ge, NEG_INF, aff), axis=1, keepdims=True)
        up = cnt >= cap
        return jnp.where(up, least_ge, lo), jnp.where(up, hi, most_lt)

    thr, _ = lax.while_loop(unresolved, bisect, (jnp.min(aff, axis=1, keepdims=True),
                                                 jnp.max(aff, axis=1, keepdims=True)))
    gt = aff > thr
    eq = aff == thr
    need = cap - jnp.sum(jnp.where(gt, 1.0, 0.0), axis=1, keepdims=True)
    tri = tri_ref[...]
    rank_eq = _cumsum_excl(jnp.where(eq, 1.0, 0.0), tri)
    sel = jnp.where(gt, 1.0, jnp.where(eq, jnp.where(rank_eq < need, 1.0, 0.0), 0.0))
    cum = _cumsum_excl(sel, tri)
    pos = jnp.where(sel > 0.0, cum, -1.0)
    tile_lane = lax.broadcasted_iota(I32, (nb * N_EXPERTS, LANES), 1)
    starts = jnp.zeros((nb * N_EXPERTS, LANES), F32)
    for t in range(n // tile):
        starts = jnp.where(tile_lane == t, cum[:, t * tile:t * tile + 1], starts)
    for i in range(nb):
        rows = slice(i * N_EXPERTS, (i + 1) * N_EXPERTS)
        pos_ref[i] = pos[rows].astype(I32)
        st_ref[i] = starts[rows].astype(I32)
        padded = jnp.concatenate([pos[rows], jnp.full((LANES - N_EXPERTS, n), -1.0, F32)], axis=0)
        post_ref[i] = padded.T.astype(I32)


def _route(logits, tri, cap, tile):
    b, _, n = logits.shape
    nb = next(k for k in (4, 2, 1) if b % k == 0)
    em = pl.BlockSpec((nb, N_EXPERTS, n), lambda i: (i, 0, 0))
    kern = functools.partial(_route_kernel, n=n, cap=cap, tile=tile, nb=nb)
    return pl.pallas_call(
        kern,
        out_shape=(jax.ShapeDtypeStruct((b, N_EXPERTS, n), I32),
                   jax.ShapeDtypeStruct((b, N_EXPERTS, n), F32),
                   jax.ShapeDtypeStruct((b, n, LANES), I32),
                   jax.ShapeDtypeStruct((b, N_EXPERTS, LANES), I32)),
        grid=(b // nb,),
        in_specs=[em, _full((GROUP_W, GROUP_W))],
        out_specs=(em, em, pl.BlockSpec((nb, n, LANES), lambda i: (i, 0, 0)),
                   pl.BlockSpec((nb, N_EXPERTS, LANES), lambda i: (i, 0, 0))),
        compiler_params=_params("arbitrary"),
        name="router",
    )(logits, tri)


def _ffn_kernel(st_ref, xn_ref, pos_ref, gate_ref, wg32_ref, wu32_ref, wd32_ref, o_ref, wg_ref, wu_ref, wd_ref, xg_ref,
                *, bb, cap, n, tile, stride):
    @pl.when(pl.program_id(1) == 0)
    def _():
        wg_ref[...] = wg32_ref[0, 0].astype(BF16)
        wu_ref[...] = wu32_ref[0, 0].astype(BF16)
        wd_ref[...] = wd32_ref[0, 0].astype(BF16)

    expert = pl.ds(pl.program_id(0), 1)

    def step(gather):
        slot = lax.broadcasted_iota(I32, (cap, n), 0)
        hits = [pos_ref[i, expert, :] == slot for i in range(bb)]
        gs = [jnp.sum(jnp.where(hits[i], gate_ref[i, expert, :], 0.0), axis=1, keepdims=True) for i in range(bb)]
        g = gs[0] if bb == 1 else jnp.concatenate(gs, axis=0)
        gather(hits)
        xg = xg_ref[...].astype(BF16)
        hid = (jax.nn.silu(_dot(xg, wg_ref[...])) * _dot(xg, wu_ref[...])).astype(BF16)
        o = _dot(hid, wd_ref[...]) * g
        for i in range(bb):
            o_ref[i, 0] = o[i * cap:(i + 1) * cap].astype(BF16)

    def gather_dense(hits):
        for i in range(bb):
            xg_ref[i * cap:(i + 1) * cap, :] = _dot(jnp.where(hits[i], 1.0, 0.0).astype(BF16), xn_ref[i])

    if cap <= SLOT_WINDOW:
        step(gather_dense)
        return

    nt = n // tile
    win, fits = [], True
    for i in range(bb):
        base = (pl.program_id(1) * bb + i) * nt * stride * N_EXPERTS + pl.program_id(0)
        for t in range(nt):
            start = st_ref[base + t * stride * N_EXPERTS]
            end = st_ref[base + (t + 1) * stride * N_EXPERTS] if t + 1 < nt else cap
            a = jnp.minimum((start >> 4) << 4, cap - SLOT_WINDOW)
            win.append(a)
            fits = jnp.logical_and(fits, end - a <= SLOT_WINDOW)

    def gather_windowed(hits):
        del hits
        xg_ref[...] = jnp.zeros_like(xg_ref)
        wslot = lax.broadcasted_iota(I32, (SLOT_WINDOW, tile), 0)
        for i in range(bb):
            for t in range(nt):
                a = win[i * nt + t]
                p = pos_ref[i, expert, t * tile:(t + 1) * tile]
                onehot = jnp.where(p - a == wslot, 1.0, 0.0).astype(BF16)
                rows = pl.ds(pl.multiple_of(i * cap + a, 16), SLOT_WINDOW)
                xg_ref[rows, :] = xg_ref[rows, :] + _dot(onehot, xn_ref[i, t * tile:(t + 1) * tile, :])

    pl.when(fits)(lambda: step(gather_windowed))
    pl.when(jnp.logical_not(fits))(lambda: step(gather_dense))


def _expert_ffn(xn, pos, gate, starts, wg, wu, wd, layer, cap, bb, tile, stride):
    b, n, d = xn.shape
    assert b % bb == 0 and n % tile == 0, (b, bb, n, tile)
    wspec = pl.BlockSpec((1, 1, d, d), lambda e, j, st: (layer, e, 0, 0))
    sel = pl.BlockSpec((bb, N_EXPERTS, n), lambda e, j, st: (j, 0, 0))
    kern = functools.partial(_ffn_kernel, bb=bb, cap=cap, n=n, tile=tile, stride=stride)
    return pl.pallas_call(
        kern,
        out_shape=jax.ShapeDtypeStruct((b, N_EXPERTS, cap, d), BF16),
        grid_spec=pltpu.PrefetchScalarGridSpec(
            num_scalar_prefetch=1,
            grid=(N_EXPERTS, b // bb),
            in_specs=[pl.BlockSpec((bb, n, d), lambda e, j, st: (j, 0, 0)), sel, sel, wspec, wspec, wspec],
            out_specs=pl.BlockSpec((bb, 1, cap, d), lambda e, j, st: (j, e, 0, 0)),
            scratch_shapes=[pltpu.VMEM((d, d), BF16)] * 3 + [pltpu.VMEM((bb * cap, d), F32)]),
        compiler_params=_params("arbitrary", "arbitrary"),
        name="expert_ffn",
    )(starts, xn, pos, gate, wg, wu, wd)


def _scatter_kernel(st_ref, x_ref, mod_ref, pt_ref, o_ref, out_ref, *, tn, cap, nt, sub, window):
    for s in range(sub):
        _scatter_tile(st_ref, x_ref.at[0, s * tn:(s + 1) * tn], mod_ref, pt_ref.at[0, s * tn:(s + 1) * tn], o_ref,
                      out_ref.at[0, s * tn:(s + 1) * tn], pl.program_id(1) * sub + s,
                      tn=tn, cap=cap, nt=nt, window=window)


def _scatter_tile(st_ref, x_ref, mod_ref, pt_ref, o_ref, out_ref, t, *, tn, cap, nt, window):
    pos_t = pt_ref[...]
    g = mod_ref[0][:, 5 * D_MODEL:6 * D_MODEL]

    def dense():
        if cap % LANES == 0:
            slot = lax.broadcasted_iota(I32, (tn, cap), 1)
            onehot = jnp.concatenate(
                [jnp.where(pos_t[:, e:e + 1] == slot, 1.0, 0.0).astype(BF16) for e in range(N_EXPERTS)], axis=1)
        else:
            slot = lax.broadcasted_iota(I32, (tn, N_EXPERTS * cap), 1)
            acc = jnp.zeros((tn, N_EXPERTS * cap), F32)
            for e in range(N_EXPERTS):
                pe = pos_t[:, e:e + 1]
                acc = jnp.where(jnp.where(pe >= 0, pe + e * cap, -1) == slot, 1.0, acc)
            onehot = acc.astype(BF16)
        y = _dot(onehot, o_ref[0].reshape(N_EXPERTS * cap, D_MODEL))
        out_ref[...] = x_ref[...] + g * y

    if cap <= window:
        dense()
        return

    base = (pl.program_id(0) * nt + t) * N_EXPERTS
    nxt = jnp.minimum(t + 1, nt - 1)
    nbase = (pl.program_id(0) * nt + nxt) * N_EXPERTS
    win, fits = [], True
    for e in range(N_EXPERTS):
        start = st_ref[base + e]
        end = jnp.where(t + 1 < nt, st_ref[nbase + e], cap)
        a = jnp.minimum((start >> 4) << 4, cap - window)
        win.append(a)
        fits = jnp.logical_and(fits, end - a <= window)

    @pl.when(fits)
    def _():
        per = LANES // window
        lane = lax.broadcasted_iota(I32, (tn, LANES), 1)
        groups = []
        for e0 in range(0, N_EXPERTS, per):
            target = jnp.full((tn, LANES), -1, I32)
            for i in range(per):
                rel = pos_t[:, e0 + i:e0 + i + 1] - win[e0 + i]
                here = (lane >> (window.bit_length() - 1)) == i if per > 1 else None
                shifted = jnp.where(rel >= 0, rel + i * window, -1)
                target = shifted if here is None else jnp.where(here, shifted, target)
            groups.append(jnp.where(target == lane, 1.0, 0.0).astype(BF16))
        onehot = jnp.concatenate(groups, axis=1)
        rows = jnp.concatenate(
            [o_ref[0, e, pl.ds(pl.multiple_of(win[e], 16), window), :] for e in range(N_EXPERTS)], axis=0)
        out_ref[...] = x_ref[...] + g * _dot(onehot, rows)

    pl.when(jnp.logical_not(fits))(dense)


def _scatter(x, mod, pos_t, starts, o, cap, tn, window, sub):
    b, n, d = x.shape
    nt = n // tn
    xs = pl.BlockSpec((1, sub * tn, d), lambda i, t, st: (i, t, 0))
    kern = functools.partial(_scatter_kernel, tn=tn, cap=cap, nt=nt, sub=sub, window=window)
    return pl.pallas_call(
        kern,
        out_shape=jax.ShapeDtypeStruct((b, n, d), F32),
        grid_spec=pltpu.PrefetchScalarGridSpec(
            num_scalar_prefetch=1,
            grid=(b, nt // sub),
            in_specs=[xs,
                      pl.BlockSpec((1, 1, 6 * d), lambda i, t, st: (i, 0, 0)),
                      pl.BlockSpec((1, sub * tn, LANES), lambda i, t, st: (i, t, 0)),
                      pl.BlockSpec((1, N_EXPERTS, cap, d), lambda i, t, st: (i, 0, 0, 0))],
            out_specs=xs),
        compiler_params=_params("arbitrary", "arbitrary"),
        name="scatter_add",
    )(starts, x, mod, pos_t, o)


@functools.lru_cache(maxsize=None)
def _np_consts():
    lane = np.arange(GROUP_W)
    seg32 = (lane[:, None] // DIFF_D == lane[None, :] // DIFF_D).astype(np.float32)
    seg64 = (lane[:, None] // HEAD_DIM == lane[None, :] // HEAD_DIM).astype(np.float32)
    ang = 2.0 * np.pi * ((lane[:, None] % HEAD_DIM) * (lane[None, :] % HEAD_DIM) % HEAD_DIM) / HEAD_DIM
    cc = np.cos(ang) * seg64
    ss = np.sin(ang) * seg64
    tri = (lane[:, None] <= lane[None, :]).astype(np.float32)
    return dict(seg32=seg32, seg64=seg64, cc=cc, ss=ss, tri=tri)


@functools.lru_cache(maxsize=None)
def _np_dft(n):
    idx = (np.arange(n, dtype=np.int64)[:, None] * np.arange(n, dtype=np.int64)[None, :]) % n
    ang = 2.0 * np.pi * idx.astype(np.float64) / n
    return np.cos(ang).astype(np.float32), np.sin(ang).astype(np.float32)


@functools.lru_cache(maxsize=None)
def _np_rope(n):
    half = DIFF_D // 2
    inv = 1.0 / (ROPE_BASE ** (np.arange(0, half, 2, dtype=np.float32) / half))
    t = np.arange(n)
    row = (t // GRID_W).astype(np.float32)[:, None] * inv
    col = (t % GRID_W).astype(np.float32)[:, None] * inv
    nf = inv.shape[0]
    d = np.arange(GROUP_W) % DIFF_D
    f = d % nf
    is_col = d >= half
    second = (d % half) >= nf
    ang = np.where(is_col[None, :], col[:, f], row[:, f])
    c = np.cos(ang).astype(np.float32)
    s = np.sin(ang).astype(np.float32)
    s = np.where(second[None, :], s, -s)
    return c, s


@functools.lru_cache(maxsize=None)
def _np_na_index(n_rows):
    n_steps = n_rows // NA_QROWS
    reps = [0, 1, 2, n_steps - 2, n_steps - 1]
    tq = NA_QROWS * GRID_W
    roff = np.zeros((len(reps), NA_QROWS, NA_WIN_ROWS), np.int32)
    valid = np.zeros((len(reps), tq, NA_WIN), bool)
    for ci, t in enumerate(reps):
        ws = int(np.clip(NA_QROWS * t - NA_ROWS // 2, 0, n_rows - NA_WIN_ROWS))
        qi = np.arange(tq)
        r = NA_QROWS * t + qi // GRID_W
        qcol = qi % GRID_W
        kk = np.arange(NA_WIN)
        krow = ws + kk // GRID_W
        kcol = kk % GRID_W
        rstart = np.clip(r - NA_ROWS // 2, 0, n_rows - NA_ROWS)
        wstart = np.clip(qcol - NA_COLS // 2, 0, GRID_W - NA_COLS)
        vr = (krow[None, :] >= rstart[:, None]) & (krow[None, :] < rstart[:, None] + NA_ROWS)
        vc = (kcol[None, :] >= wstart[:, None]) & (kcol[None, :] < wstart[:, None] + NA_COLS)
        valid[ci] = vr & vc
        rows_q = NA_QROWS * t + np.arange(NA_QROWS)
        rows_k = ws + np.arange(NA_WIN_ROWS)
        roff[ci] = np.clip(rows_k[None, :] - rows_q[:, None] + NA_ROWS - 1, 0, 2 * NA_ROWS - 2)
    return roff, valid


def _na_table_kernel(r_ref, mask_ref, o_ref, toep_ref, *, roff):
    n_off = 2 * NA_ROWS - 1
    for h in range(HEADS):
        for ro in range(n_off):
            row = jnp.broadcast_to(r_ref[0, h, ro:ro + 1, :], (GRID_W, LANES))
            toep_ref[h, ro] = pltpu.roll(row, 0, 1, stride=1, stride_axis=0)
    lane = lax.broadcasted_iota(I32, (GRID_W, LANES), 1)
    n_cls = roff.shape[0]
    for ci in range(n_cls):
        for h in range(HEADS):
            for rr in range(NA_QROWS):
                r0 = h * NA_QROWS * GRID_W + rr * GRID_W
                for kp in range(NA_WIN_ROWS // 2):
                    left = toep_ref[h, int(roff[ci, rr, 2 * kp])]
                    right = pltpu.roll(toep_ref[h, int(roff[ci, rr, 2 * kp + 1])], GRID_W, 1)
                    bias = jnp.where(lane < GRID_W, left, right) * LOG2E
                    o_ref[0, ci, r0:r0 + GRID_W, kp * LANES:(kp + 1) * LANES] = (
                        bias + mask_ref[ci, rr * GRID_W:(rr + 1) * GRID_W, kp * LANES:(kp + 1) * LANES])


def _na_tables(rpb_all, n_rows):
    roff, valid = _np_na_index(n_rows)
    n_cls = roff.shape[0]
    depth = rpb_all.shape[0]
    n_off = 2 * NA_ROWS - 1
    r = jnp.concatenate([rpb_all[..., NA_COLS - 1:], jnp.zeros(rpb_all.shape[:-1] + (LANES - 2 * NA_COLS + 1,), F32),
                         rpb_all[..., :NA_COLS - 1]], axis=-1).astype(F32)
    mask = jnp.asarray(np.where(valid, 0.0, NEG_INF).astype(np.float32))
    tq = NA_QROWS * GRID_W
    kern = functools.partial(_na_table_kernel, roff=roff)
    return pl.pallas_call(
        kern,
        out_shape=jax.ShapeDtypeStruct((depth, n_cls, HEADS * tq, NA_WIN), F32),
        grid=(depth,),
        in_specs=[pl.BlockSpec((1, HEADS, n_off, LANES), lambda l: (l, 0, 0, 0)),
                  _full((n_cls, tq, NA_WIN))],
        out_specs=pl.BlockSpec((1, n_cls, HEADS * tq, NA_WIN), lambda l: (l, 0, 0, 0)),
        scratch_shapes=[pltpu.VMEM((HEADS, n_off, GRID_W, LANES), F32)],
        compiler_params=_params("arbitrary"),
        name="na_bias_table",
    )(r, mask)


class _Tiles(NamedTuple):
    proj: int
    attn: int
    na_pairs: int
    moe: int
    ffn_samples: int


def _latent_tiles(n):
    return _Tiles(proj=min(n, 1024), attn=min(n, 512), na_pairs=min(8, n // (NA_QROWS * GRID_W)),
                  moe=min(n, 512), ffn_samples=2)
def _moe(x, xn, logits, mod, tri, wg, wu, wd, layer, bb, tn):
    n = x.shape[1]
    cap = CAPACITY_FACTOR * n // N_EXPERTS
    stride = 2 if cap > SLOT_WINDOW else 1
    ts = tn // stride
    pos, gate, pos_t, starts = _route(logits, tri, cap, ts)
    starts = starts[:, :, :n // ts].transpose(0, 2, 1).reshape(-1)
    o = _expert_ffn(xn, pos, gate, starts, wg, wu, wd, layer, cap, bb, tn, stride)
    return _scatter(x, mod, pos_t, starts, o, cap, ts, SLOT_WINDOW // stride, min(4, n // ts))


def kernel(x, c, ctx, c_ctx, ada_w, ada_b, norm1_g, norm2_g, w_in, w_out, head_out_g, sgu_w, sgu_b, diff_qn_g, diff_kn_g, diff_lambda, na_qn_g, na_kn_g, na_rpb, router_w, exp_w_gate, exp_w_up, exp_w_down):
    b, n, d = x.shape
    n_ctx = ctx.shape[1]
    npc = _np_consts()
    consts = {k: jnp.asarray(v, F32).astype(BF16) for k, v in npc.items()}
    seg64, tri = consts["seg64"], consts["tri"]
    cn, sn = (jnp.asarray(a, F32).astype(BF16) for a in _np_dft(n))
    cn_c, sn_c = (jnp.asarray(a, F32).astype(BF16) for a in _np_dft(n_ctx))
    rope = tuple(jnp.asarray(a, F32) for a in _np_rope(n))

    pad = (-(b + 1)) % 8
    c_rows = jnp.concatenate([c, c_ctx[None, :], jnp.zeros((pad, d), F32)], axis=0)
    mod_all = _modulation(c_rows, ada_w, ada_b)
    na_tables = _na_tables(na_rpb, n // GRID_W)

    xc = ctx
    for l in range(DEPTH):
        last = l == DEPTH - 1
        lam_init = 0.8 - 0.6 * math.exp(-0.3 * l)
        mod = mod_all[l, :b][:, None, :]
        mod_c = jnp.broadcast_to(mod_all[l, b][None, None, :], (b, 1, 6 * d))
        g1 = norm1_g[l][None, :]
        g2 = norm2_g[l][None, :]
        w_in_l = w_in[l].astype(BF16)
        w_out_l = w_out[l].astype(BF16)
        sguw = sgu_w[l].astype(BF16).transpose(1, 0, 2).reshape(CHUNK, HEADS * CHUNK)
        sgub = jnp.repeat(sgu_b[l].T, HEAD_DIM, axis=1)
        hg = head_out_g[l].reshape(4, GROUP_W)
        vec = jnp.stack([jnp.tile(diff_qn_g[l], GROUP_W // DIFF_D), jnp.tile(diff_kn_g[l], GROUP_W // DIFF_D),
                         jnp.tile(na_qn_g[l], HEADS), jnp.tile(na_kn_g[l], HEADS),
                         hg[0], hg[1], hg[2], hg[3],
                         jnp.broadcast_to(jnp.max(jnp.abs(na_rpb[l])) * LOG2E, (GROUP_W,))], axis=0).astype(F32)
        vec = jnp.pad(vec, ((0, VEC_ROWS - vec.shape[0]), (0, 0)))
        lam = diff_lambda[l].astype(F32)
        rwt = router_w[l].T.astype(BF16)
        experts = (exp_w_gate, exp_w_up, exp_w_down)

        tl = _latent_tiles(n)
        ya, zc, zs, qc, kct, vc, qd, kdt, vd = _in_proj(x, mod, g1, w_in_l, consts, sguw, sgub, vec, rope, tl.proj)
        if last:
            kct_c, vc_c, kdt_c, vd_c = _in_proj(
                xc, mod_c, g1, w_in_l, consts, sguw, sgub, vec, None, n_ctx, kv_only=True)
        else:
            ya_c, zc_c, zs_c, qc_c, kct_c, vc_c, qd_c, kdt_c, vd_c = _in_proj(
                xc, mod_c, g1, w_in_l, consts, sguw, sgub, vec, None, n_ctx)

        yb = _fourier(zc, zs, cn, sn, seg64, vec, tl.proj)
        yc = _attention(qc, [(kct, vc), (kct_c, vc_c)], lam, lam_init, seg64, vec, 6, 1.0 - lam_init,
                        tl.attn, tl.attn)
        yd = _na_attention(qd, kdt, vd, kdt_c, vd_c, na_tables, l, seg64, vec, tl.na_pairs)
        x, xn, logits = _out_proj(ya, yb, yc, yd, w_out_l, x, mod, g2, rwt, tl.proj)
        x = _moe(x, xn, logits, mod, tri, *experts, l, tl.ffn_samples, tl.moe)

        if not last:
            yb_c = _fourier(zc_c, zs_c, cn_c, sn_c, seg64, vec, n_ctx)
            yc_c = _attention(qc_c, [(kct_c, vc_c)], lam, lam_init, seg64, vec, 6, 1.0 - lam_init, n_ctx)
            yd_c = _attention(qd_c, [(kdt_c, vd_c)], None, 0.0, seg64, vec, 7, 1.0, n_ctx)
            xc, xn_c, logits_c = _out_proj(ya_c, yb_c, yc_c, yd_c, w_out_l, xc, mod_c, g2, rwt, n_ctx)
            xc = _moe(xc, xn_c, logits_c, mod_c, tri, *experts, l, b, n_ctx)
    return x
```

```python
import functools
import math
from typing import NamedTuple

import numpy as np
import jax
import jax.numpy as jnp
from jax import lax
from jax.experimental import pallas as pl
from jax.experimental.pallas import tpu as pltpu

F32 = jnp.float32
BF16 = jnp.bfloat16
I32 = jnp.int32

D_MODEL = 1024
DEPTH = 2
GRID_W = 64
LANES = 128
HEAD_DIM = 64
LOG2_HEAD_DIM = 6
GROUP_W = 256
HEADS = GROUP_W // HEAD_DIM
CHUNK = 128
DIFF_D = HEAD_DIM // 2
NA_ROWS = 8
NA_COLS = 16
N_EXPERTS = 16
CAPACITY_FACTOR = 2
ROPE_BASE = 10000.0
EPS = 1e-6
IN_W = 9 * GROUP_W
LOG2E = 1.4426950408889634

VMEM_LIMIT_BYTES = 56 * 1024 * 1024
NA_QROWS = 2
NA_WIN_ROWS = NA_ROWS + 2
NA_WIN = NA_WIN_ROWS * GRID_W
NA_HEAD_STACK = 1
NEG_INF = float("-inf")
VEC_ROWS = 16
SLOT_WINDOW = 128
SAFE_EXP2_BOUND = 48.0
BOUND_SLACK = 1.02


def _dot(a, b):
    return jnp.dot(a, b, preferred_element_type=F32)


def _params(*sem):
    return pltpu.CompilerParams(dimension_semantics=sem, vmem_limit_bytes=VMEM_LIMIT_BYTES)


def _full(shape):
    nd = len(shape)
    return pl.BlockSpec(shape, lambda *_: (0,) * nd)


def _seg_rms(x, seg, width):
    ss = _dot((x * x).astype(BF16), seg)
    return x * lax.rsqrt(ss * (1.0 / width) + EPS)


def _mod_kernel(c_ref, w_ref, b_ref, o_ref):
    s = jax.nn.silu(c_ref[...]).astype(BF16)
    o_ref[0] = _dot(s, w_ref[0].astype(BF16)) + b_ref[0]


def _modulation(c_rows, ada_w, ada_b):
    depth, d, w6 = ada_w.shape
    r = c_rows.shape[0]
    tn = 1024
    return pl.pallas_call(
        _mod_kernel,
        out_shape=jax.ShapeDtypeStruct((depth, r, w6), F32),
        grid=(depth, w6 // tn),
        in_specs=[
            pl.BlockSpec((r, d), lambda l, j: (0, 0)),
            pl.BlockSpec((1, d, tn), lambda l, j: (l, 0, j)),
            pl.BlockSpec((1, 1, tn), lambda l, j: (l, 0, j)),
        ],
        out_specs=pl.BlockSpec((1, r, tn), lambda l, j: (l, 0, j)),
        compiler_params=_params("arbitrary", "arbitrary"),
        name="modulation",
    )(c_rows, ada_w, ada_b.reshape(depth, 1, w6))


def _rope(x, c, s, lane):
    fwd = pltpu.roll(x, GROUP_W - 8, 1)
    bwd = pltpu.roll(x, 8, 1)
    partner = jnp.where((lane & 8) == 0, fwd, bwd)
    return x * c + partner * s


def _in_kernel(*refs, tm, use_rope, kv_only, c_scale, d_scale):
    (x_ref, mod_ref, g1_ref, w_ref, seg32_ref, seg64_ref, cc_ref, ss_ref, sguw_ref, sgub_ref, vec_ref) = refs[:11]
    rest = refs[11:]
    if use_rope:
        ropec_ref, ropes_ref = rest[:2]
        rest = rest[2:]
    if kv_only:
        kct_ref, vc_ref, kdt_ref, vd_ref = rest
    else:
        ya_ref, zc_ref, zs_ref, qc_ref, kct_ref, vc_ref, qd_ref, kdt_ref, vd_ref = rest

    x = x_ref[0]
    mod = mod_ref[0]
    sh = mod[:, 0:D_MODEL]
    sc = mod[:, D_MODEL:2 * D_MODEL]
    ms = jnp.mean(x * x, axis=-1, keepdims=True)
    h = x * lax.rsqrt(ms + EPS) * g1_ref[...]
    h = (h * (1.0 + sc) + sh).astype(BF16)
    lane = lax.broadcasted_iota(I32, (1, GROUP_W), 1)
    head = lane >> LOG2_HEAD_DIM
    seg32 = seg32_ref[...]
    seg64 = seg64_ref[...]
    vec = vec_ref[...]

    if kv_only:
        pc = _dot(h, w_ref[:, 4 * GROUP_W:6 * GROUP_W])
        k = _seg_rms(pc[:, 0:GROUP_W], seg32, DIFF_D) * vec[1:2]
        kct_ref[0] = k.T.astype(BF16)
        vc_ref[0] = pc[:, GROUP_W:2 * GROUP_W].astype(BF16)
        pd = _dot(h, w_ref[:, 7 * GROUP_W:9 * GROUP_W])
        kd = _seg_rms(pd[:, 0:GROUP_W], seg64, HEAD_DIM) * vec[3:4]
        kdt_ref[0] = kd.T.astype(BF16)
        vd_ref[0] = pd[:, GROUP_W:2 * GROUP_W].astype(BF16)
        return

    z = jax.nn.gelu(_dot(h, w_ref[:, 0:2 * GROUP_W]))
    u = z[:, 0:GROUP_W]
    vn = _seg_rms(z[:, GROUP_W:2 * GROUP_W], seg64, HEAD_DIM).astype(BF16)
    rows = []
    for c in range(tm // CHUNK):
        vch = vn[c * CHUNK:(c + 1) * CHUNK]
        stacked = jnp.concatenate([jnp.where(head == hh, vch, jnp.zeros_like(vch)) for hh in range(HEADS)], axis=0)
        rows.append(_dot(sguw_ref[...], stacked) + sgub_ref[...])
    ya = u * jnp.concatenate(rows, axis=0)
    ya_ref[0] = (_seg_rms(ya, seg64, HEAD_DIM) * vec[4:5]).astype(BF16)

    zb = _dot(h, w_ref[:, 2 * GROUP_W:3 * GROUP_W]).astype(BF16)
    zc_ref[0] = _dot(zb, cc_ref[...]).astype(BF16)
    zs_ref[0] = _dot(zb, ss_ref[...]).astype(BF16)

    pc = _dot(h, w_ref[:, 3 * GROUP_W:6 * GROUP_W])
    q = _seg_rms(pc[:, 0:GROUP_W], seg32, DIFF_D) * vec[0:1]
    k = _seg_rms(pc[:, GROUP_W:2 * GROUP_W], seg32, DIFF_D) * vec[1:2]
    if use_rope:
        rc = ropec_ref[...]
        rs = ropes_ref[...]
        q = _rope(q, rc, rs, lane)
        k = _rope(k, rc, rs, lane)
    qc_ref[0] = (q * c_scale).astype(BF16)
    kct_ref[0] = k.T.astype(BF16)
    vc_ref[0] = pc[:, 2 * GROUP_W:3 * GROUP_W].astype(BF16)

    pd = _dot(h, w_ref[:, 6 * GROUP_W:9 * GROUP_W])
    qd = _seg_rms(pd[:, 0:GROUP_W], seg64, HEAD_DIM) * vec[2:3]
    kd = _seg_rms(pd[:, GROUP_W:2 * GROUP_W], seg64, HEAD_DIM) * vec[3:4]
    qd_ref[0] = (qd * d_scale).astype(BF16)
    kdt_ref[0] = kd.T.astype(BF16)
    vd_ref[0] = pd[:, 2 * GROUP_W:3 * GROUP_W].astype(BF16)


def _in_proj(x, mod, g1, w_in, consts, sguw, sgub, vec, rope, tm, kv_only=False):
    b, n, d = x.shape
    use_rope = rope is not None
    tok = pl.BlockSpec((1, tm, GROUP_W), lambda i, t: (i, t, 0))
    tok_t = pl.BlockSpec((1, GROUP_W, tm), lambda i, t: (i, 0, t))
    in_specs = [
        pl.BlockSpec((1, tm, d), lambda i, t: (i, t, 0)),
        pl.BlockSpec((1, 1, 6 * d), lambda i, t: (i, 0, 0)),
        _full((1, d)),
        _full((d, IN_W)),
        _full((GROUP_W, GROUP_W)), _full((GROUP_W, GROUP_W)), _full((GROUP_W, GROUP_W)), _full((GROUP_W, GROUP_W)),
        _full((CHUNK, HEADS * CHUNK)),
        _full((CHUNK, GROUP_W)),
        _full((VEC_ROWS, GROUP_W)),
    ]
    args = [x, mod, g1, w_in, consts["seg32"], consts["seg64"], consts["cc"], consts["ss"], sguw, sgub, vec]
    if use_rope:
        in_specs += [pl.BlockSpec((tm, GROUP_W), lambda i, t: (t, 0))] * 2
        args += list(rope)
    sd = jax.ShapeDtypeStruct((b, n, GROUP_W), BF16)
    sdt = jax.ShapeDtypeStruct((b, GROUP_W, n), BF16)
    kern = functools.partial(
        _in_kernel, tm=tm, use_rope=use_rope, kv_only=kv_only,
        c_scale=(DIFF_D ** -0.5) * LOG2E, d_scale=(HEAD_DIM ** -0.5) * LOG2E)
    return pl.pallas_call(
        kern,
        out_shape=(sdt, sd, sdt, sd) if kv_only else (sd, sd, sd, sd, sdt, sd, sd, sdt, sd),
        grid=(b, n // tm),
        in_specs=in_specs,
        out_specs=(tok_t, tok, tok_t, tok) if kv_only else (tok, tok, tok, tok, tok_t, tok, tok, tok_t, tok),
        compiler_params=_params("arbitrary", "arbitrary"),
        name="in_proj",
    )(*args)


def _fourier_kernel(cn_ref, sn_ref, zc_ref, zs_ref, seg64_ref, vec_ref, o_ref, *, norm):
    y = (_dot(cn_ref[...], zc_ref[0]) - _dot(sn_ref[...], zs_ref[0])) * norm
    o_ref[0] = (_seg_rms(y, seg64_ref[...], HEAD_DIM) * vec_ref[5:6]).astype(BF16)


def _fourier(zc, zs, cn, sn, seg64, vec, tn):
    b, n, _ = zc.shape
    kern = functools.partial(_fourier_kernel, norm=1.0 / math.sqrt(n * HEAD_DIM))
    return pl.pallas_call(
        kern,
        out_shape=jax.ShapeDtypeStruct((b, n, GROUP_W), BF16),
        grid=(n // tn, b),
        in_specs=[
            pl.BlockSpec((tn, n), lambda t, i: (t, 0)),
            pl.BlockSpec((tn, n), lambda t, i: (t, 0)),
            pl.BlockSpec((1, n, GROUP_W), lambda t, i: (i, 0, 0)),
            pl.BlockSpec((1, n, GROUP_W), lambda t, i: (i, 0, 0)),
            _full((GROUP_W, GROUP_W)),
            _full((VEC_ROWS, GROUP_W)),
        ],
        out_specs=pl.BlockSpec((1, tn, GROUP_W), lambda t, i: (i, t, 0)),
        compiler_params=_params("arbitrary", "arbitrary"),
        name="fourier",
    )(cn, sn, zc, zs, seg64, vec)


def _attn_kernel(*refs, n_src, diff, lam_init, chunk, tq, vec_row, out_scale):
    q_ref = refs[0]
    rest = refs[1 + 2 * n_src:]
    if diff:
        lam_ref = rest[0]
        rest = rest[1:]
    seg64_ref, vec_ref, o_ref = rest

    q = q_ref[0]
    lane = lax.broadcasted_iota(I32, (1, GROUP_W), 1)
    if diff:
        lf = lam_ref[...]
        lam = (jnp.exp(jnp.sum(lf[0:1] * lf[1:2], axis=-1, keepdims=True))
               - jnp.exp(jnp.sum(lf[2:3] * lf[3:4], axis=-1, keepdims=True)) + lam_init)

    chunks = []
    for i in range(n_src):
        kt_ref, v_ref = refs[1 + 2 * i], refs[2 + 2 * i]
        nk = kt_ref.shape[2]
        chunks += [(kt_ref, v_ref, c0, min(chunk, nk - c0)) for c0 in range(0, nk, chunk)]

    width = DIFF_D if diff else HEAD_DIM
    gq = vec_ref[0:1] if diff else vec_ref[2:3]
    gk = vec_ref[1:2] if diff else vec_ref[3:4]
    bound = jnp.max(jnp.abs(gq)) * jnp.max(jnp.abs(gk)) * (math.sqrt(width) * LOG2E * BOUND_SLACK)

    def attend_bounded(sel):
        qm = jnp.where(sel, q, jnp.zeros_like(q))
        l = jnp.zeros((tq, 1), F32)
        acc = jnp.zeros((tq, GROUP_W), F32)
        for kt_ref, v_ref, c0, ck in chunks:
            p = jnp.exp2(_dot(qm, kt_ref[0, :, c0:c0 + ck]))
            l = l + jnp.sum(p, axis=-1, keepdims=True)
            acc = acc + _dot(p.astype(BF16), v_ref[0, c0:c0 + ck, :])
        return acc * (1.0 / l)

    def attend_online(sel):
        qm = jnp.where(sel, q, jnp.zeros_like(q))
        m = jnp.full((tq, 1), NEG_INF, F32)
        l = jnp.zeros((tq, 1), F32)
        acc = jnp.zeros((tq, GROUP_W), F32)
        for kt_ref, v_ref, c0, ck in chunks:
            s = _dot(qm, kt_ref[0, :, c0:c0 + ck])
            m_new = jnp.maximum(m, jnp.max(s, axis=-1, keepdims=True))
            alpha = jnp.exp2(m - m_new)
            p = jnp.exp2(s - m_new)
            l = alpha * l + jnp.sum(p, axis=-1, keepdims=True)
            acc = alpha * acc + _dot(p.astype(BF16), v_ref[0, c0:c0 + ck, :])
            m = m_new
        return acc * (1.0 / l)

    def head_out(attend, h):
        if diff:
            return (attend((lane >> (LOG2_HEAD_DIM - 1)) == 2 * h)
                    - lam * attend((lane >> (LOG2_HEAD_DIM - 1)) == 2 * h + 1))
        return attend((lane >> LOG2_HEAD_DIM) == h)

    def finish(out):
        y = _seg_rms(out, seg64_ref[...], HEAD_DIM) * vec_ref[vec_row:vec_row + 1]
        o_ref[0] = (y * out_scale).astype(BF16)

    def run_bounded():
        out = jnp.zeros((tq, GROUP_W), F32)
        for h in range(HEADS):
            out = jnp.where((lane >> LOG2_HEAD_DIM) == h, head_out(attend_bounded, h), out)
        finish(out)

    def run_online():
        def body(h, out):
            return jnp.where((lane >> LOG2_HEAD_DIM) == h, head_out(attend_online, h), out)

        finish(lax.fori_loop(0, HEADS, body, jnp.zeros((tq, GROUP_W), F32)))

    small = bound <= SAFE_EXP2_BOUND
    pl.when(small)(run_bounded)
    pl.when(jnp.logical_not(small))(run_online)


def _attention(q, srcs, lam, lam_init, seg64, vec, vec_row, out_scale, tq, chunk=768):
    b, nq, _ = q.shape
    diff = lam is not None
    in_specs = [pl.BlockSpec((1, tq, GROUP_W), lambda i, t: (i, t, 0))]
    args = [q]
    for kt, v in srcs:
        nk = v.shape[1]
        in_specs += [pl.BlockSpec((1, GROUP_W, nk), lambda i, t: (i, 0, 0)),
                     pl.BlockSpec((1, nk, GROUP_W), lambda i, t: (i, 0, 0))]
        args += [kt, v]
    if diff:
        in_specs.append(_full((4, DIFF_D)))
        args.append(lam)
    in_specs += [_full((GROUP_W, GROUP_W)), _full((VEC_ROWS, GROUP_W))]
    args += [seg64, vec]
    kern = functools.partial(_attn_kernel, n_src=len(srcs), diff=diff, lam_init=lam_init, chunk=chunk, tq=tq,
                             vec_row=vec_row, out_scale=out_scale)
    return pl.pallas_call(
        kern,
        out_shape=jax.ShapeDtypeStruct((b, nq, GROUP_W), BF16),
        grid=(b, nq // tq),
        in_specs=in_specs,
        out_specs=pl.BlockSpec((1, tq, GROUP_W), lambda i, t: (i, t, 0)),
        compiler_params=_params("arbitrary", "arbitrary"),
        name="diff_attention" if diff else "ctx_attention",
    )(*args)


def _na_kernel(q_ref, kt_ref, v_ref, ktc_ref, vc_ref, tab_ref, seg64_ref, vec_ref, o_ref, *, n_rows, pairs):
    n_steps = n_rows // NA_QROWS
    tq = NA_QROWS * GRID_W
    lane = lax.broadcasted_iota(I32, (1, GROUP_W), 1)
    head = lane >> LOG2_HEAD_DIM
    ktc = ktc_ref[0]
    vc = vc_ref[0]
    bound = (jnp.max(jnp.abs(vec_ref[2:3])) * jnp.max(jnp.abs(vec_ref[3:4])) * (math.sqrt(HEAD_DIM) * LOG2E * BOUND_SLACK)
             + jnp.max(vec_ref[8:9]))

    def run(bounded):
        for pi in range(pairs):
            t = pl.program_id(1) * pairs + pi
            ws = jnp.clip(NA_QROWS * t - NA_ROWS // 2, 0, n_rows - NA_WIN_ROWS)
            k0 = pl.multiple_of(ws * GRID_W, 128)
            tid = jnp.where(t < 2, t, jnp.where(t < n_steps - 2, 2, t - (n_steps - 5)))
            q = q_ref[0, pi * tq:(pi + 1) * tq, :]
            out = jnp.zeros((tq, GROUP_W), F32)
            for h0 in range(0, HEADS, NA_HEAD_STACK):
                hs = range(h0, h0 + NA_HEAD_STACK)
                qs = jnp.concatenate([jnp.where(head == hh, q, jnp.zeros_like(q)) for hh in hs], axis=0)
                s_loc = (_dot(qs, kt_ref[0, :, pl.ds(k0, NA_WIN)])
                         + tab_ref[0, tid, h0 * tq:(h0 + NA_HEAD_STACK) * tq, :])
                s_ctx = _dot(qs, ktc)
                if bounded:
                    p_loc = jnp.exp2(s_loc)
                    p_ctx = jnp.exp2(s_ctx)
                else:
                    m = jnp.maximum(jnp.max(s_loc, axis=-1, keepdims=True), jnp.max(s_ctx, axis=-1, keepdims=True))
                    p_loc = jnp.exp2(s_loc - m)
                    p_ctx = jnp.exp2(s_ctx - m)
                l = jnp.sum(p_loc, axis=-1, keepdims=True) + jnp.sum(p_ctx, axis=-1, keepdims=True)
                o = _dot(p_loc.astype(BF16), v_ref[0, pl.ds(k0, NA_WIN), :]) + _dot(p_ctx.astype(BF16), vc)
                o = o * (1.0 / l)
                for i, hh in enumerate(hs):
                    out = jnp.where(head == hh, o[i * tq:(i + 1) * tq], out)
            o_ref[0, pi * tq:(pi + 1) * tq, :] = (
                _seg_rms(out, seg64_ref[...], HEAD_DIM) * vec_ref[7:8]).astype(BF16)

    small = bound <= SAFE_EXP2_BOUND
    pl.when(small)(lambda: run(True))
    pl.when(jnp.logical_not(small))(lambda: run(False))


def _na_attention(q, kt, v, ktc, vc, tables, layer, seg64, vec, pairs):
    b, n, _ = q.shape
    nc = vc.shape[1]
    tq = pairs * NA_QROWS * GRID_W
    kern = functools.partial(_na_kernel, n_rows=n // GRID_W, pairs=pairs)
    return pl.pallas_call(
        kern,
        out_shape=jax.ShapeDtypeStruct((b, n, GROUP_W), BF16),
        grid=(b, n // tq),
        in_specs=[
            pl.BlockSpec((1, tq, GROUP_W), lambda i, t: (i, t, 0)),
            pl.BlockSpec((1, GROUP_W, n), lambda i, t: (i, 0, 0)),
            pl.BlockSpec((1, n, GROUP_W), lambda i, t: (i, 0, 0)),
            pl.BlockSpec((1, GROUP_W, nc), lambda i, t: (i, 0, 0)),
            pl.BlockSpec((1, nc, GROUP_W), lambda i, t: (i, 0, 0)),
            pl.BlockSpec((1,) + tables.shape[1:], lambda i, t: (layer, 0, 0, 0)),
            _full((GROUP_W, GROUP_W)),
            _full((VEC_ROWS, GROUP_W)),
        ],
        out_specs=pl.BlockSpec((1, tq, GROUP_W), lambda i, t: (i, t, 0)),
        compiler_params=_params("arbitrary", "arbitrary"),
        name="neighbourhood_attention",
    )(q, kt, v, ktc, vc, tables, seg64, vec)


def _out_kernel(ya_ref, yb_ref, yc_ref, yd_ref, w_ref, x_ref, mod_ref, g2_ref, rwt_ref, o_ref, xn_ref, lg_ref):
    y = jnp.concatenate([ya_ref[0], yb_ref[0], yc_ref[0], yd_ref[0]], axis=-1)
    mod = mod_ref[0]
    x = x_ref[0] + mod[:, 2 * D_MODEL:3 * D_MODEL] * _dot(y, w_ref[...])
    o_ref[0] = x
    sh = mod[:, 3 * D_MODEL:4 * D_MODEL]
    sc = mod[:, 4 * D_MODEL:5 * D_MODEL]
    ms = jnp.mean(x * x, axis=-1, keepdims=True)
    h = x * lax.rsqrt(ms + EPS) * g2_ref[...]
    h = (h * (1.0 + sc) + sh).astype(BF16)
    xn_ref[0] = h
    lg_ref[0] = lax.dot_general(rwt_ref[...], h, (((1,), (1,)), ((), ())), preferred_element_type=F32)


def _out_proj(ya, yb, yc, yd, w_out, x, mod, g2, rwt, tm):
    b, n, d = x.shape
    tok = pl.BlockSpec((1, tm, GROUP_W), lambda i, t: (i, t, 0))
    xs = pl.BlockSpec((1, tm, d), lambda i, t: (i, t, 0))
    return pl.pallas_call(
        _out_kernel,
        out_shape=(jax.ShapeDtypeStruct((b, n, d), F32),
                   jax.ShapeDtypeStruct((b, n, d), BF16),
                   jax.ShapeDtypeStruct((b, N_EXPERTS, n), F32)),
        grid=(b, n // tm),
        in_specs=[tok, tok, tok, tok, _full((4 * GROUP_W, d)), xs,
                  pl.BlockSpec((1, 1, 6 * d), lambda i, t: (i, 0, 0)),
                  _full((1, d)), _full((N_EXPERTS, d))],
        out_specs=(xs, xs, pl.BlockSpec((1, N_EXPERTS, tm), lambda i, t: (i, 0, t))),
        compiler_params=_params("arbitrary", "arbitrary"),
        name="out_proj",
    )(ya, yb, yc, yd, w_out, x, mod, g2, rwt)


def _cumsum_excl(m, tri):
    n = m.shape[1]
    carry = jnp.zeros((m.shape[0], 1), F32)
    outs = []
    for j in range(n // GROUP_W):
        blk = m[:, j * GROUP_W:(j + 1) * GROUP_W]
        inc = _dot(blk.astype(BF16), tri)
        outs.append(inc - blk + carry)
        carry = carry + inc[:, GROUP_W - 1:GROUP_W]
    return jnp.concatenate(outs, axis=1)


def _route_kernel(lg_ref, tri_ref, pos_ref, gate_ref, post_ref, st_ref, *, n, cap, tile, nb):
    affs = []
    for i in range(nb):
        lg = lg_ref[i]
        e = jnp.exp(lg - jnp.max(lg, axis=0, keepdims=True))
        affs.append(e / jnp.sum(e, axis=0, keepdims=True))
        gate_ref[i] = affs[i]
    aff = affs[0] if nb == 1 else jnp.concatenate(affs, axis=0)

    def unresolved(state):
        lo, hi = state
        return jnp.max(jnp.where(lo < hi, 1.0, 0.0)) > 0.0

    def bisect(state):
        lo, hi = state
        mid = 0.5 * (lo + hi)
        mid = jnp.where(mid > lo, mid, hi)
        ge = aff >= mid
        cnt = jnp.sum(jnp.where(ge, 1.0, 0.0), axis=1, keepdims=True)
        least_ge = jnp.min(jnp.where(ge, aff, jnp.inf), axis=1, keepdims=True)
        most_lt = jnp.max(jnp.where(ge, NEG_INF, aff), axis=1, keepdims=True)
        up = cnt >= cap
        return jnp.where(up, least_ge, lo), jnp.where(up, hi, most_lt)

    thr, _ = lax.while_loop(unresolved, bisect, (jnp.min(aff, axis=1, keepdims=True),
                                                 jnp.max(aff, axis=1, keepdims=True)))
    gt = aff > thr
    eq = aff == thr
    need = cap - jnp.sum(jnp.where(gt, 1.0, 0.0), axis=1, keepdims=True)
    tri = tri_ref[...]
    rank_eq = _cumsum_excl(jnp.where(eq, 1.0, 0.0), tri)
    sel = jnp.where(gt, 1.0, jnp.where(eq, jnp.where(rank_eq < need, 1.0, 0.0), 0.0))
    cum = _cumsum_excl(sel, tri)
    pos = jnp.where(sel > 0.0, cum, -1.0)
    tile_lane = lax.broadcasted_iota(I32, (nb * N_EXPERTS, LANES), 1)
    starts = jnp.zeros((nb * N_EXPERTS, LANES), F32)
    for t in range(n // tile):
        starts = jnp.where(tile_lane == t, cum[:, t * tile:t * tile + 1], starts)
    for i in range(nb):
        rows = slice(i * N_EXPERTS, (i + 1) * N_EXPERTS)
        pos_ref[i] = pos[rows].astype(I32)
        st_ref[i] = starts[rows].astype(I32)
        padded = jnp.concatenate([pos[rows], jnp.full((LANES - N_EXPERTS, n), -1.0, F32)], axis=0)
        post_ref[i] = padded.T.astype(I32)


def _route(logits, tri, cap, tile):
    b, _, n = logits.shape
    nb = next(k for k in (4, 2, 1) if b % k == 0)
    em = pl.BlockSpec((nb, N_EXPERTS, n), lambda i: (i, 0, 0))
    kern = functools.partial(_route_kernel, n=n, cap=cap, tile=tile, nb=nb)
    return pl.pallas_call(
        kern,
        out_shape=(jax.ShapeDtypeStruct((b, N_EXPERTS, n), I32),
                   jax.ShapeDtypeStruct((b, N_EXPERTS, n), F32),
                   jax.ShapeDtypeStruct((b, n, LANES), I32),
                   jax.ShapeDtypeStruct((b, N_EXPERTS, LANES), I32)),
        grid=(b // nb,),
        in_specs=[em, _full((GROUP_W, GROUP_W))],
        out_specs=(em, em, pl.BlockSpec((nb, n, LANES), lambda i: (i, 0, 0)),
                   pl.BlockSpec((nb, N_EXPERTS, LANES), lambda i: (i, 0, 0))),
        compiler_params=_params("arbitrary"),
        name="router",
    )(logits, tri)


def _ffn_kernel(st_ref, xn_ref, pos_ref, gate_ref, wg32_ref, wu32_ref, wd32_ref, o_ref, wg_ref, wu_ref, wd_ref, xg_ref,
                *, bb, cap, n, tile, stride):
    @pl.when(pl.program_id(1) == 0)
    def _():
        wg_ref[...] = wg32_ref[0, 0].astype(BF16)
        wu_ref[...] = wu32_ref[0, 0].astype(BF16)
        wd_ref[...] = wd32_ref[0, 0].astype(BF16)

    expert = pl.ds(pl.program_id(0), 1)

    def step(gather):
        slot = lax.broadcasted_iota(I32, (cap, n), 0)
        hits = [pos_ref[i, expert, :] == slot for i in range(bb)]
        gs = [jnp.sum(jnp.where(hits[i], gate_ref[i, expert, :], 0.0), axis=1, keepdims=True) for i in range(bb)]
        g = gs[0] if bb == 1 else jnp.concatenate(gs, axis=0)
        gather(hits)
        xg = xg_ref[...].astype(BF16)
        hid = (jax.nn.silu(_dot(xg, wg_ref[...])) * _dot(xg, wu_ref[...])).astype(BF16)
        o = _dot(hid, wd_ref[...]) * g
        for i in range(bb):
            o_ref[i, 0] = o[i * cap:(i + 1) * cap].astype(BF16)

    def gather_dense(hits):
        for i in range(bb):
            xg_ref[i * cap:(i + 1) * cap, :] = _dot(jnp.where(hits[i], 1.0, 0.0).astype(BF16), xn_ref[i])

    if cap <= SLOT_WINDOW:
        step(gather_dense)
        return

    nt = n // tile
    win, fits = [], True
    for i in range(bb):
        base = (pl.program_id(1) * bb + i) * nt * stride * N_EXPERTS + pl.program_id(0)
        for t in range(nt):
            start = st_ref[base + t * stride * N_EXPERTS]
            end = st_ref[base + (t + 1) * stride * N_EXPERTS] if t + 1 < nt else cap
            a = jnp.minimum((start >> 4) << 4, cap - SLOT_WINDOW)
            win.append(a)
            fits = jnp.logical_and(fits, end - a <= SLOT_WINDOW)

    def gather_windowed(hits):
        del hits
        xg_ref[...] = jnp.zeros_like(xg_ref)
        wslot = lax.broadcasted_iota(I32, (SLOT_WINDOW, tile), 0)
        for i in range(bb):
            for t in range(nt):
                a = win[i * nt + t]
                p = pos_ref[i, expert, t * tile:(t + 1) * tile]
                onehot = jnp.where(p - a == wslot, 1.0, 0.0).astype(BF16)
                rows = pl.ds(pl.multiple_of(i * cap + a, 16), SLOT_WINDOW)
                xg_ref[rows, :] = xg_ref[rows, :] + _dot(onehot, xn_ref[i, t * tile:(t + 1) * tile, :])

    pl.when(fits)(lambda: step(gather_windowed))
    pl.when(jnp.logical_not(fits))(lambda: step(gather_dense))


def _expert_ffn(xn, pos, gate, starts, wg, wu, wd, layer, cap, bb, tile, stride):
    b, n, d = xn.shape
    assert b % bb == 0 and n % tile == 0, (b, bb, n, tile)
    wspec = pl.BlockSpec((1, 1, d, d), lambda e, j, st: (layer, e, 0, 0))
    sel = pl.BlockSpec((bb, N_EXPERTS, n), lambda e, j, st: (j, 0, 0))
    kern = functools.partial(_ffn_kernel, bb=bb, cap=cap, n=n, tile=tile, stride=stride)
    return pl.pallas_call(
        kern,
        out_shape=jax.ShapeDtypeStruct((b, N_EXPERTS, cap, d), BF16),
        grid_spec=pltpu.PrefetchScalarGridSpec(
            num_scalar_prefetch=1,
            grid=(N_EXPERTS, b // bb),
            in_specs=[pl.BlockSpec((bb, n, d), lambda e, j, st: (j, 0, 0)), sel, sel, wspec, wspec, wspec],
            out_specs=pl.BlockSpec((bb, 1, cap, d), lambda e, j, st: (j, e, 0, 0)),
            scratch_shapes=[pltpu.VMEM((d, d), BF16)] * 3 + [pltpu.VMEM((bb * cap, d), F32)]),
        compiler_params=_params("arbitrary", "arbitrary"),
        name="expert_ffn",
    )(starts, xn, pos, gate, wg, wu, wd)


def _scatter_kernel(st_ref, x_ref, mod_ref, pt_ref, o_ref, out_ref, *, tn, cap, nt, sub, window):
    for s in range(sub):
        _scatter_tile(st_ref, x_ref.at[0, s * tn:(s + 1) * tn], mod_ref, pt_ref.at[0, s * tn:(s + 1) * tn], o_ref,
                      out_ref.at[0, s * tn:(s + 1) * tn], pl.program_id(1) * sub + s,
                      tn=tn, cap=cap, nt=nt, window=window)


def _scatter_tile(st_ref, x_ref, mod_ref, pt_ref, o_ref, out_ref, t, *, tn, cap, nt, window):
    pos_t = pt_ref[...]
    g = mod_ref[0][:, 5 * D_MODEL:6 * D_MODEL]

    def dense():
        if cap % LANES == 0:
            slot = lax.broadcasted_iota(I32, (tn, cap), 1)
            onehot = jnp.concatenate(
                [jnp.where(pos_t[:, e:e + 1] == slot, 1.0, 0.0).astype(BF16) for e in range(N_EXPERTS)], axis=1)
        else:
            slot = lax.broadcasted_iota(I32, (tn, N_EXPERTS * cap), 1)
            acc = jnp.zeros((tn, N_EXPERTS * cap), F32)
            for e in range(N_EXPERTS):
                pe = pos_t[:, e:e + 1]
                acc = jnp.where(jnp.where(pe >= 0, pe + e * cap, -1) == slot, 1.0, acc)
            onehot = acc.astype(BF16)
        y = _dot(onehot, o_ref[0].reshape(N_EXPERTS * cap, D_MODEL))
        out_ref[...] = x_ref[...] + g * y

    if cap <= window:
        dense()
        return

    base = (pl.program_id(0) * nt + t) * N_EXPERTS
    nxt = jnp.minimum(t + 1, nt - 1)
    nbase = (pl.program_id(0) * nt + nxt) * N_EXPERTS
    win, fits = [], True
    for e in range(N_EXPERTS):
        start = st_ref[base + e]
        end = jnp.where(t + 1 < nt, st_ref[nbase + e], cap)
        a = jnp.minimum((start >> 4) << 4, cap - window)
        win.append(a)
        fits = jnp.logical_and(fits, end - a <= window)

    @pl.when(fits)
    def _():
        per = LANES // window
        lane = lax.broadcasted_iota(I32, (tn, LANES), 1)
        groups = []
        for e0 in range(0, N_EXPERTS, per):
            target = jnp.full((tn, LANES), -1, I32)
            for i in range(per):
                rel = pos_t[:, e0 + i:e0 + i + 1] - win[e0 + i]
                here = (lane >> (window.bit_length() - 1)) == i if per > 1 else None
                shifted = jnp.where(rel >= 0, rel + i * window, -1)
                target = shifted if here is None else jnp.where(here, shifted, target)
            groups.append(jnp.where(target == lane, 1.0, 0.0).astype(BF16))
        onehot = jnp.concatenate(groups, axis=1)
        rows = jnp.concatenate(
            [o_ref[0, e, pl.ds(pl.multiple_of(win[e], 16), window), :] for e in range(N_EXPERTS)], axis=0)
        out_ref[...] = x_ref[...] + g * _dot(onehot, rows)

    pl.when(jnp.logical_not(fits))(dense)


def _scatter(x, mod, pos_t, starts, o, cap, tn, window, sub):
    b, n, d = x.shape
    nt = n // tn
    xs = pl.BlockSpec((1, sub * tn, d), lambda i, t, st: (i, t, 0))
    kern = functools.partial(_scatter_kernel, tn=tn, cap=cap, nt=nt, sub=sub, window=window)
    return pl.pallas_call(
        kern,
        out_shape=jax.ShapeDtypeStruct((b, n, d), F32),
        grid_spec=pltpu.PrefetchScalarGridSpec(
            num_scalar_prefetch=1,
            grid=(b, nt // sub),
            in_specs=[xs,
                      pl.BlockSpec((1, 1, 6 * d), lambda i, t, st: (i, 0, 0)),
                      pl.BlockSpec((1, sub * tn, LANES), lambda i, t, st: (i, t, 0)),
                      pl.BlockSpec((1, N_EXPERTS, cap, d), lambda i, t, st: (i, 0, 0, 0))],
            out_specs=xs),
        compiler_params=_params("arbitrary", "arbitrary"),
        name="scatter_add",
    )(starts, x, mod, pos_t, o)


@functools.lru_cache(maxsize=None)
def _np_consts():
    lane = np.arange(GROUP_W)
    seg32 = (lane[:, None] // DIFF_D == lane[None, :] // DIFF_D).astype(np.float32)
    seg64 = (lane[:, None] // HEAD_DIM == lane[None, :] // HEAD_DIM).astype(np.float32)
    ang = 2.0 * np.pi * ((lane[:, None] % HEAD_DIM) * (lane[None, :] % HEAD_DIM) % HEAD_DIM) / HEAD_DIM
    cc = np.cos(ang) * seg64
    ss = np.sin(ang) * seg64
    tri = (lane[:, None] <= lane[None, :]).astype(np.float32)
    return dict(seg32=seg32, seg64=seg64, cc=cc, ss=ss, tri=tri)


@functools.lru_cache(maxsize=None)
def _np_dft(n):
    idx = (np.arange(n, dtype=np.int64)[:, None] * np.arange(n, dtype=np.int64)[None, :]) % n
    ang = 2.0 * np.pi * idx.astype(np.float64) / n
    return np.cos(ang).astype(np.float32), np.sin(ang).astype(np.float32)


@functools.lru_cache(maxsize=None)
def _np_rope(n):
    half = DIFF_D // 2
    inv = 1.0 / (ROPE_BASE ** (np.arange(0, half, 2, dtype=np.float32) / half))
    t = np.arange(n)
    row = (t // GRID_W).astype(np.float32)[:, None] * inv
    col = (t % GRID_W).astype(np.float32)[:, None] * inv
    nf = inv.shape[0]
    d = np.arange(GROUP_W) % DIFF_D
    f = d % nf
    is_col = d >= half
    second = (d % half) >= nf
    ang = np.where(is_col[None, :], col[:, f], row[:, f])
    c = np.cos(ang).astype(np.float32)
    s = np.sin(ang).astype(np.float32)
    s = np.where(second[None, :], s, -s)
    return c, s


@functools.lru_cache(maxsize=None)
def _np_na_index(n_rows):
    n_steps = n_rows // NA_QROWS
    reps = [0, 1, 2, n_steps - 2, n_steps - 1]
    tq = NA_QROWS * GRID_W
    roff = np.zeros((len(reps), NA_QROWS, NA_WIN_ROWS), np.int32)
    valid = np.zeros((len(reps), tq, NA_WIN), bool)
    for ci, t in enumerate(reps):
        ws = int(np.clip(NA_QROWS * t - NA_ROWS // 2, 0, n_rows - NA_WIN_ROWS))
        qi = np.arange(tq)
        r = NA_QROWS * t + qi // GRID_W
        qcol = qi % GRID_W
        kk = np.arange(NA_WIN)
        krow = ws + kk // GRID_W
        kcol = kk % GRID_W
        rstart = np.clip(r - NA_ROWS // 2, 0, n_rows - NA_ROWS)
        wstart = np.clip(qcol - NA_COLS // 2, 0, GRID_W - NA_COLS)
        vr = (krow[None, :] >= rstart[:, None]) & (krow[None, :] < rstart[:, None] + NA_ROWS)
        vc = (kcol[None, :] >= wstart[:, None]) & (kcol[None, :] < wstart[:, None] + NA_COLS)
        valid[ci] = vr & vc
        rows_q = NA_QROWS * t + np.arange(NA_QROWS)
        rows_k = ws + np.arange(NA_WIN_ROWS)
        roff[ci] = np.clip(rows_k[None, :] - rows_q[:, None] + NA_ROWS - 1, 0, 2 * NA_ROWS - 2)
    return roff, valid


def _na_table_kernel(r_ref, mask_ref, o_ref, toep_ref, *, roff):
    n_off = 2 * NA_ROWS - 1
    for h in range(HEADS):
        for ro in range(n_off):
            row = jnp.broadcast_to(r_ref[0, h, ro:ro + 1, :], (GRID_W, LANES))
            toep_ref[h, ro] = pltpu.roll(row, 0, 1, stride=1, stride_axis=0)
    lane = lax.broadcasted_iota(I32, (GRID_W, LANES), 1)
    n_cls = roff.shape[0]
    for ci in range(n_cls):
        for h in range(HEADS):
            for rr in range(NA_QROWS):
                r0 = h * NA_QROWS * GRID_W + rr * GRID_W
                for kp in range(NA_WIN_ROWS // 2):
                    left = toep_ref[h, int(roff[ci, rr, 2 * kp])]
                    right = pltpu.roll(toep_ref[h, int(roff[ci, rr, 2 * kp + 1])], GRID_W, 1)
                    bias = jnp.where(lane < GRID_W, left, right) * LOG2E
                    o_ref[0, ci, r0:r0 + GRID_W, kp * LANES:(kp + 1) * LANES] = (
                        bias + mask_ref[ci, rr * GRID_W:(rr + 1) * GRID_W, kp * LANES:(kp + 1) * LANES])


def _na_tables(rpb_all, n_rows):
    roff, valid = _np_na_index(n_rows)
    n_cls = roff.shape[0]
    depth = rpb_all.shape[0]
    n_off = 2 * NA_ROWS - 1
    r = jnp.concatenate([rpb_all[..., NA_COLS - 1:], jnp.zeros(rpb_all.shape[:-1] + (LANES - 2 * NA_COLS + 1,), F32),
                         rpb_all[..., :NA_COLS - 1]], axis=-1).astype(F32)
    mask = jnp.asarray(np.where(valid, 0.0, NEG_INF).astype(np.float32))
    tq = NA_QROWS * GRID_W
    kern = functools.partial(_na_table_kernel, roff=roff)
    return pl.pallas_call(
        kern,
        out_shape=jax.ShapeDtypeStruct((depth, n_cls, HEADS * tq, NA_WIN), F32),
        grid=(depth,),
        in_specs=[pl.BlockSpec((1, HEADS, n_off, LANES), lambda l: (l, 0, 0, 0)),
                  _full((n_cls, tq, NA_WIN))],
        out_specs=pl.BlockSpec((1, n_cls, HEADS * tq, NA_WIN), lambda l: (l, 0, 0, 0)),
        scratch_shapes=[pltpu.VMEM((HEADS, n_off, GRID_W, LANES), F32)],
        compiler_params=_params("arbitrary"),
        name="na_bias_table",
    )(r, mask)


class _Tiles(NamedTuple):
    proj: int
    dft: int
    attn: int
    na_pairs: int
    moe: int
    ffn_samples: int


def _latent_tiles(n):
    return _Tiles(proj=min(n, 1024), dft=min(n, 2048), attn=min(n, 512), na_pairs=min(8, n // (NA_QROWS * GRID_W)),
                  moe=min(n, 512), ffn_samples=2)
def _moe(x, xn, logits, mod, tri, wg, wu, wd, layer, bb, tn):
    n = x.shape[1]
    cap = CAPACITY_FACTOR * n // N_EXPERTS
    stride = 2 if cap > SLOT_WINDOW else 1
    ts = tn // stride
    pos, gate, pos_t, starts = _route(logits, tri, cap, ts)
    starts = starts[:, :, :n // ts].transpose(0, 2, 1).reshape(-1)
    o = _expert_ffn(xn, pos, gate, starts, wg, wu, wd, layer, cap, bb, tn, stride)
    return _scatter(x, mod, pos_t, starts, o, cap, ts, SLOT_WINDOW // stride, min(4, n // ts))


def kernel(x, c, ctx, c_ctx, ada_w, ada_b, norm1_g, norm2_g, w_in, w_out, head_out_g, sgu_w, sgu_b, diff_qn_g, diff_kn_g, diff_lambda, na_qn_g, na_kn_g, na_rpb, router_w, exp_w_gate, exp_w_up, exp_w_down):
    b, n, d = x.shape
    n_ctx = ctx.shape[1]
    npc = _np_consts()
    consts = {k: jnp.asarray(v, F32).astype(BF16) for k, v in npc.items()}
    seg64, tri = consts["seg64"], consts["tri"]
    cn, sn = (jnp.asarray(a, F32).astype(BF16) for a in _np_dft(n))
    cn_c, sn_c = (jnp.asarray(a, F32).astype(BF16) for a in _np_dft(n_ctx))
    rope = tuple(jnp.asarray(a, F32) for a in _np_rope(n))

    pad = (-(b + 1)) % 8
    c_rows = jnp.concatenate([c, c_ctx[None, :], jnp.zeros((pad, d), F32)], axis=0)
    mod_all = _modulation(c_rows, ada_w, ada_b)
    na_tables = _na_tables(na_rpb, n // GRID_W)

    xc = ctx
    for l in range(DEPTH):
        last = l == DEPTH - 1
        lam_init = 0.8 - 0.6 * math.exp(-0.3 * l)
        mod = mod_all[l, :b][:, None, :]
        mod_c = jnp.broadcast_to(mod_all[l, b][None, None, :], (b, 1, 6 * d))
        g1 = norm1_g[l][None, :]
        g2 = norm2_g[l][None, :]
        w_in_l = w_in[l].astype(BF16)
        w_out_l = w_out[l].astype(BF16)
        sguw = sgu_w[l].astype(BF16).transpose(1, 0, 2).reshape(CHUNK, HEADS * CHUNK)
        sgub = jnp.repeat(sgu_b[l].T, HEAD_DIM, axis=1)
        hg = head_out_g[l].reshape(4, GROUP_W)
        vec = jnp.stack([jnp.tile(diff_qn_g[l], GROUP_W // DIFF_D), jnp.tile(diff_kn_g[l], GROUP_W // DIFF_D),
                         jnp.tile(na_qn_g[l], HEADS), jnp.tile(na_kn_g[l], HEADS),
                         hg[0], hg[1], hg[2], hg[3],
                         jnp.broadcast_to(jnp.max(jnp.abs(na_rpb[l])) * LOG2E, (GROUP_W,))], axis=0).astype(F32)
        vec = jnp.pad(vec, ((0, VEC_ROWS - vec.shape[0]), (0, 0)))
        lam = diff_lambda[l].astype(F32)
        rwt = router_w[l].T.astype(BF16)
        experts = (exp_w_gate, exp_w_up, exp_w_down)

        tl = _latent_tiles(n)
        ya, zc, zs, qc, kct, vc, qd, kdt, vd = _in_proj(x, mod, g1, w_in_l, consts, sguw, sgub, vec, rope, tl.proj)
        if last:
            kct_c, vc_c, kdt_c, vd_c = _in_proj(
                xc, mod_c, g1, w_in_l, consts, sguw, sgub, vec, None, n_ctx, kv_only=True)
        else:
            ya_c, zc_c, zs_c, qc_c, kct_c, vc_c, qd_c, kdt_c, vd_c = _in_proj(
                xc, mod_c, g1, w_in_l, consts, sguw, sgub, vec, None, n_ctx)

        yb = _fourier(zc, zs, cn, sn, seg64, vec, tl.dft)
        yc = _attention(qc, [(kct, vc), (kct_c, vc_c)], lam, lam_init, seg64, vec, 6, 1.0 - lam_init,
                        tl.attn, tl.attn)
        yd = _na_attention(qd, kdt, vd, kdt_c, vd_c, na_tables, l, seg64, vec, tl.na_pairs)
        x, xn, logits = _out_proj(ya, yb, yc, yd, w_out_l, x, mod, g2, rwt, tl.proj)
        x = _moe(x, xn, logits, mod, tri, *experts, l, tl.ffn_samples, tl.moe)

        if not last:
            yb_c = _fourier(zc_c, zs_c, cn_c, sn_c, seg64, vec, n_ctx)
            yc_c = _attention(qc_c, [(kct_c, vc_c)], lam, lam_init, seg64, vec, 6, 1.0 - lam_init, n_ctx)
            yd_c = _attention(qd_c, [(kdt_c, vd_c)], None, 0.0, seg64, vec, 7, 1.0, n_ctx)
            xc, xn_c, logits_c = _out_proj(ya_c, yb_c, yc_c, yd_c, w_out_l, xc, mod_c, g2, rwt, n_ctx)
            xc = _moe(xc, xn_c, logits_c, mod_c, tri, *experts, l, b, n_ctx)
    return x
```

```python
import functools
import math
from typing import NamedTuple

import numpy as np
import jax
import jax.numpy as jnp
from jax import lax
from jax.experimental import pallas as pl
from jax.experimental.pallas import tpu as pltpu

F32 = jnp.float32
BF16 = jnp.bfloat16
I32 = jnp.int32

D_MODEL = 1024
DEPTH = 2
GRID_W = 64
LANES = 128
HEAD_DIM = 64
LOG2_HEAD_DIM = 6
GROUP_W = 256
HEADS = GROUP_W // HEAD_DIM
CHUNK = 128
DIFF_D = HEAD_DIM // 2
NA_ROWS = 8
NA_COLS = 16
N_EXPERTS = 16
CAPACITY_FACTOR = 2
ROPE_BASE = 10000.0
EPS = 1e-6
IN_W = 9 * GROUP_W
LOG2E = 1.4426950408889634

VMEM_LIMIT_BYTES = 56 * 1024 * 1024
NA_QROWS = 2
NA_WIN_ROWS = NA_ROWS + 2
NA_WIN = NA_WIN_ROWS * GRID_W
NA_HEAD_STACK = 1
NEG_INF = float("-inf")
VEC_ROWS = 16
SLOT_WINDOW = 128
SAFE_EXP2_BOUND = 48.0
BOUND_SLACK = 1.02


def _dot(a, b):
    return jnp.dot(a, b, preferred_element_type=F32)


def _params(*sem):
    return pltpu.CompilerParams(dimension_semantics=sem, vmem_limit_bytes=VMEM_LIMIT_BYTES)


def _full(shape):
    nd = len(shape)
    return pl.BlockSpec(shape, lambda *_: (0,) * nd)


def _seg_rms(x, seg, width):
    ss = _dot((x * x).astype(BF16), seg)
    return x * lax.rsqrt(ss * (1.0 / width) + EPS)


def _mod_kernel(c_ref, w_ref, b_ref, o_ref):
    s = jax.nn.silu(c_ref[...]).astype(BF16)
    o_ref[0] = _dot(s, w_ref[0].astype(BF16)) + b_ref[0]


def _modulation(c_rows, ada_w, ada_b):
    depth, d, w6 = ada_w.shape
    r = c_rows.shape[0]
    tn = 1024
    return pl.pallas_call(
        _mod_kernel,
        out_shape=jax.ShapeDtypeStruct((depth, r, w6), F32),
        grid=(depth, w6 // tn),
        in_specs=[
            pl.BlockSpec((r, d), lambda l, j: (0, 0)),
            pl.BlockSpec((1, d, tn), lambda l, j: (l, 0, j)),
            pl.BlockSpec((1, 1, tn), lambda l, j: (l, 0, j)),
        ],
        out_specs=pl.BlockSpec((1, r, tn), lambda l, j: (l, 0, j)),
        compiler_params=_params("arbitrary", "arbitrary"),
        name="modulation",
    )(c_rows, ada_w, ada_b.reshape(depth, 1, w6))


def _rope(x, c, s, lane):
    fwd = pltpu.roll(x, GROUP_W - 8, 1)
    bwd = pltpu.roll(x, 8, 1)
    partner = jnp.where((lane & 8) == 0, fwd, bwd)
    return x * c + partner * s


def _in_kernel(*refs, tm, use_rope, kv_only, c_scale, d_scale):
    (x_ref, mod_ref, g1_ref, w_ref, seg32_ref, seg64_ref, cc_ref, ss_ref, sguw_ref, sgub_ref, vec_ref) = refs[:11]
    rest = refs[11:]
    if use_rope:
        ropec_ref, ropes_ref = rest[:2]
        rest = rest[2:]
    if kv_only:
        kct_ref, vc_ref, kdt_ref, vd_ref = rest
    else:
        ya_ref, zc_ref, zs_ref, qc_ref, kct_ref, vc_ref, qd_ref, kdt_ref, vd_ref = rest

    x = x_ref[0]
    mod = mod_ref[0]
    sh = mod[:, 0:D_MODEL]
    sc = mod[:, D_MODEL:2 * D_MODEL]
    ms = jnp.mean(x * x, axis=-1, keepdims=True)
    h = x * lax.rsqrt(ms + EPS) * g1_ref[...]
    h = (h * (1.0 + sc) + sh).astype(BF16)
    lane = lax.broadcasted_iota(I32, (1, GROUP_W), 1)
    head = lane >> LOG2_HEAD_DIM
    seg32 = seg32_ref[...]
    seg64 = seg64_ref[...]
    vec = vec_ref[...]

    if kv_only:
        pc = _dot(h, w_ref[:, 4 * GROUP_W:6 * GROUP_W])
        k = _seg_rms(pc[:, 0:GROUP_W], seg32, DIFF_D) * vec[1:2]
        kct_ref[0] = k.T.astype(BF16)
        vc_ref[0] = pc[:, GROUP_W:2 * GROUP_W].astype(BF16)
        pd = _dot(h, w_ref[:, 7 * GROUP_W:9 * GROUP_W])
        kd = _seg_rms(pd[:, 0:GROUP_W], seg64, HEAD_DIM) * vec[3:4]
        kdt_ref[0] = kd.T.astype(BF16)
        vd_ref[0] = pd[:, GROUP_W:2 * GROUP_W].astype(BF16)
        return

    z = jax.nn.gelu(_dot(h, w_ref[:, 0:2 * GROUP_W]))
    u = z[:, 0:GROUP_W]
    vn = _seg_rms(z[:, GROUP_W:2 * GROUP_W], seg64, HEAD_DIM).astype(BF16)
    rows = []
    for c in range(tm // CHUNK):
        vch = vn[c * CHUNK:(c + 1) * CHUNK]
        stacked = jnp.concatenate([jnp.where(head == hh, vch, jnp.zeros_like(vch)) for hh in range(HEADS)], axis=0)
        rows.append(_dot(sguw_ref[...], stacked) + sgub_ref[...])
    ya = u * jnp.concatenate(rows, axis=0)
    ya_ref[0] = (_seg_rms(ya, seg64, HEAD_DIM) * vec[4:5]).astype(BF16)

    zb = _dot(h, w_ref[:, 2 * GROUP_W:3 * GROUP_W]).astype(BF16)
    zc_ref[0] = _dot(zb, cc_ref[...]).astype(BF16)
    zs_ref[0] = _dot(zb, ss_ref[...]).astype(BF16)

    pc = _dot(h, w_ref[:, 3 * GROUP_W:6 * GROUP_W])
    q = _seg_rms(pc[:, 0:GROUP_W], seg32, DIFF_D) * vec[0:1]
    k = _seg_rms(pc[:, GROUP_W:2 * GROUP_W], seg32, DIFF_D) * vec[1:2]
    if use_rope:
        rc = ropec_ref[...]
        rs = ropes_ref[...]
        q = _rope(q, rc, rs, lane)
        k = _rope(k, rc, rs, lane)
    qc_ref[0] = (q * c_scale).astype(BF16)
    kct_ref[0] = k.T.astype(BF16)
    vc_ref[0] = pc[:, 2 * GROUP_W:3 * GROUP_W].astype(BF16)

    pd = _dot(h, w_ref[:, 6 * GROUP_W:9 * GROUP_W])
    qd = _seg_rms(pd[:, 0:GROUP_W], seg64, HEAD_DIM) * vec[2:3]
    kd = _seg_rms(pd[:, GROUP_W:2 * GROUP_W], seg64, HEAD_DIM) * vec[3:4]
    qd_ref[0] = (qd * d_scale).astype(BF16)
    kdt_ref[0] = kd.T.astype(BF16)
    vd_ref[0] = pd[:, 2 * GROUP_W:3 * GROUP_W].astype(BF16)


def _in_proj(x, mod, g1, w_in, consts, sguw, sgub, vec, rope, tm, kv_only=False):
    b, n, d = x.shape
    use_rope = rope is not None
    tok = pl.BlockSpec((1, tm, GROUP_W), lambda i, t: (i, t, 0))
    tok_t = pl.BlockSpec((1, GROUP_W, tm), lambda i, t: (i, 0, t))
    in_specs = [
        pl.BlockSpec((1, tm, d), lambda i, t: (i, t, 0)),
        pl.BlockSpec((1, 1, 6 * d), lambda i, t: (i, 0, 0)),
        _full((1, d)),
        _full((d, IN_W)),
        _full((GROUP_W, GROUP_W)), _full((GROUP_W, GROUP_W)), _full((GROUP_W, GROUP_W)), _full((GROUP_W, GROUP_W)),
        _full((CHUNK, HEADS * CHUNK)),
        _full((CHUNK, GROUP_W)),
        _full((VEC_ROWS, GROUP_W)),
    ]
    args = [x, mod, g1, w_in, consts["seg32"], consts["seg64"], consts["cc"], consts["ss"], sguw, sgub, vec]
    if use_rope:
        in_specs += [pl.BlockSpec((tm, GROUP_W), lambda i, t: (t, 0))] * 2
        args += list(rope)
    sd = jax.ShapeDtypeStruct((b, n, GROUP_W), BF16)
    sdt = jax.ShapeDtypeStruct((b, GROUP_W, n), BF16)
    kern = functools.partial(
        _in_kernel, tm=tm, use_rope=use_rope, kv_only=kv_only,
        c_scale=(DIFF_D ** -0.5) * LOG2E, d_scale=(HEAD_DIM ** -0.5) * LOG2E)
    return pl.pallas_call(
        kern,
        out_shape=(sdt, sd, sdt, sd) if kv_only else (sd, sd, sd, sd, sdt, sd, sd, sdt, sd),
        grid=(b, n // tm),
        in_specs=in_specs,
        out_specs=(tok_t, tok, tok_t, tok) if kv_only else (tok, tok, tok, tok, tok_t, tok, tok, tok_t, tok),
        compiler_params=_params("arbitrary", "arbitrary"),
        name="in_proj",
    )(*args)


def _fourier_kernel(cn_ref, sn_ref, zc_ref, zs_ref, seg64_ref, vec_ref, o_ref, *, norm):
    y = (_dot(cn_ref[...], zc_ref[0]) - _dot(sn_ref[...], zs_ref[0])) * norm
    o_ref[0] = (_seg_rms(y, seg64_ref[...], HEAD_DIM) * vec_ref[5:6]).astype(BF16)


def _fourier(zc, zs, cn, sn, seg64, vec, tn):
    b, n, _ = zc.shape
    kern = functools.partial(_fourier_kernel, norm=1.0 / math.sqrt(n * HEAD_DIM))
    return pl.pallas_call(
        kern,
        out_shape=jax.ShapeDtypeStruct((b, n, GROUP_W), BF16),
        grid=(n // tn, b),
        in_specs=[
            pl.BlockSpec((tn, n), lambda t, i: (t, 0)),
            pl.BlockSpec((tn, n), lambda t, i: (t, 0)),
            pl.BlockSpec((1, n, GROUP_W), lambda t, i: (i, 0, 0)),
            pl.BlockSpec((1, n, GROUP_W), lambda t, i: (i, 0, 0)),
            _full((GROUP_W, GROUP_W)),
            _full((VEC_ROWS, GROUP_W)),
        ],
        out_specs=pl.BlockSpec((1, tn, GROUP_W), lambda t, i: (i, t, 0)),
        compiler_params=_params("arbitrary", "arbitrary"),
        name="fourier",
    )(cn, sn, zc, zs, seg64, vec)


def _attn_kernel(*refs, n_src, diff, lam_init, chunk, tq, vec_row, out_scale):
    q_ref = refs[0]
    rest = refs[1 + 2 * n_src:]
    if diff:
        lam_ref = rest[0]
        rest = rest[1:]
    seg64_ref, vec_ref, o_ref = rest

    q = q_ref[0]
    lane = lax.broadcasted_iota(I32, (1, GROUP_W), 1)
    if diff:
        lf = lam_ref[...]
        lam = (jnp.exp(jnp.sum(lf[0:1] * lf[1:2], axis=-1, keepdims=True))
               - jnp.exp(jnp.sum(lf[2:3] * lf[3:4], axis=-1, keepdims=True)) + lam_init)

    chunks = []
    for i in range(n_src):
        kt_ref, v_ref = refs[1 + 2 * i], refs[2 + 2 * i]
        nk = kt_ref.shape[2]
        chunks += [(kt_ref, v_ref, c0, min(chunk, nk - c0)) for c0 in range(0, nk, chunk)]

    width = DIFF_D if diff else HEAD_DIM
    gq = vec_ref[0:1] if diff else vec_ref[2:3]
    gk = vec_ref[1:2] if diff else vec_ref[3:4]
    bound = jnp.max(jnp.abs(gq)) * jnp.max(jnp.abs(gk)) * (math.sqrt(width) * LOG2E * BOUND_SLACK)

    def attend_bounded(sel):
        qm = jnp.where(sel, q, jnp.zeros_like(q))
        l = jnp.zeros((tq, 1), F32)
        acc = jnp.zeros((tq, GROUP_W), F32)
        for kt_ref, v_ref, c0, ck in chunks:
            p = jnp.exp2(_dot(qm, kt_ref[0, :, c0:c0 + ck]))
            l = l + jnp.sum(p, axis=-1, keepdims=True)
            acc = acc + _dot(p.astype(BF16), v_ref[0, c0:c0 + ck, :])
        return acc * (1.0 / l)

    def attend_online(sel):
        qm = jnp.where(sel, q, jnp.zeros_like(q))
        m = jnp.full((tq, 1), NEG_INF, F32)
        l = jnp.zeros((tq, 1), F32)
        acc = jnp.zeros((tq, GROUP_W), F32)
        for kt_ref, v_ref, c0, ck in chunks:
            s = _dot(qm, kt_ref[0, :, c0:c0 + ck])
            m_new = jnp.maximum(m, jnp.max(s, axis=-1, keepdims=True))
            alpha = jnp.exp2(m - m_new)
            p = jnp.exp2(s - m_new)
            l = alpha * l + jnp.sum(p, axis=-1, keepdims=True)
            acc = alpha * acc + _dot(p.astype(BF16), v_ref[0, c0:c0 + ck, :])
            m = m_new
        return acc * (1.0 / l)

    def head_out(attend, h):
        if diff:
            return (attend((lane >> (LOG2_HEAD_DIM - 1)) == 2 * h)
                    - lam * attend((lane >> (LOG2_HEAD_DIM - 1)) == 2 * h + 1))
        return attend((lane >> LOG2_HEAD_DIM) == h)

    def finish(out):
        y = _seg_rms(out, seg64_ref[...], HEAD_DIM) * vec_ref[vec_row:vec_row + 1]
        o_ref[0] = (y * out_scale).astype(BF16)

    def run_bounded():
        out = jnp.zeros((tq, GROUP_W), F32)
        for h in range(HEADS):
            out = jnp.where((lane >> LOG2_HEAD_DIM) == h, head_out(attend_bounded, h), out)
        finish(out)

    def run_online():
        def body(h, out):
            return jnp.where((lane >> LOG2_HEAD_DIM) == h, head_out(attend_online, h), out)

        finish(lax.fori_loop(0, HEADS, body, jnp.zeros((tq, GROUP_W), F32)))

    small = bound <= SAFE_EXP2_BOUND
    pl.when(small)(run_bounded)
    pl.when(jnp.logical_not(small))(run_online)


def _attention(q, srcs, lam, lam_init, seg64, vec, vec_row, out_scale, tq, chunk=768):
    b, nq, _ = q.shape
    diff = lam is not None
    in_specs = [pl.BlockSpec((1, tq, GROUP_W), lambda i, t: (i, t, 0))]
    args = [q]
    for kt, v in srcs:
        nk = v.shape[1]
        in_specs += [pl.BlockSpec((1, GROUP_W, nk), lambda i, t: (i, 0, 0)),
                     pl.BlockSpec((1, nk, GROUP_W), lambda i, t: (i, 0, 0))]
        args += [kt, v]
    if diff:
        in_specs.append(_full((4, DIFF_D)))
        args.append(lam)
    in_specs += [_full((GROUP_W, GROUP_W)), _full((VEC_ROWS, GROUP_W))]
    args += [seg64, vec]
    kern = functools.partial(_attn_kernel, n_src=len(srcs), diff=diff, lam_init=lam_init, chunk=chunk, tq=tq,
                             vec_row=vec_row, out_scale=out_scale)
    return pl.pallas_call(
        kern,
        out_shape=jax.ShapeDtypeStruct((b, nq, GROUP_W), BF16),
        grid=(b, nq // tq),
        in_specs=in_specs,
        out_specs=pl.BlockSpec((1, tq, GROUP_W), lambda i, t: (i, t, 0)),
        compiler_params=_params("arbitrary", "arbitrary"),
        name="diff_attention" if diff else "ctx_attention",
    )(*args)


def _na_kernel(q_ref, kt_ref, v_ref, ktc_ref, vc_ref, tab_ref, seg64_ref, vec_ref, o_ref, *, n_rows, pairs):
    n_steps = n_rows // NA_QROWS
    tq = NA_QROWS * GRID_W
    lane = lax.broadcasted_iota(I32, (1, GROUP_W), 1)
    head = lane >> LOG2_HEAD_DIM
    ktc = ktc_ref[0]
    vc = vc_ref[0]
    bound = (jnp.max(jnp.abs(vec_ref[2:3])) * jnp.max(jnp.abs(vec_ref[3:4])) * (math.sqrt(HEAD_DIM) * LOG2E * BOUND_SLACK)
             + jnp.max(vec_ref[8:9]))

    def run(bounded):
        for pi in range(pairs):
            t = pl.program_id(1) * pairs + pi
            ws = jnp.clip(NA_QROWS * t - NA_ROWS // 2, 0, n_rows - NA_WIN_ROWS)
            k0 = pl.multiple_of(ws * GRID_W, 128)
            tid = jnp.where(t < 2, t, jnp.where(t < n_steps - 2, 2, t - (n_steps - 5)))
            q = q_ref[0, pi * tq:(pi + 1) * tq, :]
            out = jnp.zeros((tq, GROUP_W), F32)
            for h0 in range(0, HEADS, NA_HEAD_STACK):
                hs = range(h0, h0 + NA_HEAD_STACK)
                qs = jnp.concatenate([jnp.where(head == hh, q, jnp.zeros_like(q)) for hh in hs], axis=0)
                s_loc = (_dot(qs, kt_ref[0, :, pl.ds(k0, NA_WIN)])
                         + tab_ref[0, tid, h0 * tq:(h0 + NA_HEAD_STACK) * tq, :])
                s_ctx = _dot(qs, ktc)
                if bounded:
                    p_loc = jnp.exp2(s_loc)
                    p_ctx = jnp.exp2(s_ctx)
                else:
                    m = jnp.maximum(jnp.max(s_loc, axis=-1, keepdims=True), jnp.max(s_ctx, axis=-1, keepdims=True))
                    p_loc = jnp.exp2(s_loc - m)
                    p_ctx = jnp.exp2(s_ctx - m)
                l = jnp.sum(p_loc, axis=-1, keepdims=True) + jnp.sum(p_ctx, axis=-1, keepdims=True)
                o = _dot(p_loc.astype(BF16), v_ref[0, pl.ds(k0, NA_WIN), :]) + _dot(p_ctx.astype(BF16), vc)
                o = o * (1.0 / l)
                for i, hh in enumerate(hs):
                    out = jnp.where(head == hh, o[i * tq:(i + 1) * tq], out)
            o_ref[0, pi * tq:(pi + 1) * tq, :] = (
                _seg_rms(out, seg64_ref[...], HEAD_DIM) * vec_ref[7:8]).astype(BF16)

    small = bound <= SAFE_EXP2_BOUND
    pl.when(small)(lambda: run(True))
    pl.when(jnp.logical_not(small))(lambda: run(False))


def _na_attention(q, kt, v, ktc, vc, tables, layer, seg64, vec, pairs):
    b, n, _ = q.shape
    nc = vc.shape[1]
    tq = pairs * NA_QROWS * GRID_W
    kern = functools.partial(_na_kernel, n_rows=n // GRID_W, pairs=pairs)
    return pl.pallas_call(
        kern,
        out_shape=jax.ShapeDtypeStruct((b, n, GROUP_W), BF16),
        grid=(b, n // tq),
        in_specs=[
            pl.BlockSpec((1, tq, GROUP_W), lambda i, t: (i, t, 0)),
            pl.BlockSpec((1, GROUP_W, n), lambda i, t: (i, 0, 0)),
            pl.BlockSpec((1, n, GROUP_W), lambda i, t: (i, 0, 0)),
            pl.BlockSpec((1, GROUP_W, nc), lambda i, t: (i, 0, 0)),
            pl.BlockSpec((1, nc, GROUP_W), lambda i, t: (i, 0, 0)),
            pl.BlockSpec((1,) + tables.shape[1:], lambda i, t: (layer, 0, 0, 0)),
            _full((GROUP_W, GROUP_W)),
            _full((VEC_ROWS, GROUP_W)),
        ],
        out_specs=pl.BlockSpec((1, tq, GROUP_W), lambda i, t: (i, t, 0)),
        compiler_params=_params("arbitrary", "arbitrary"),
        name="neighbourhood_attention",
    )(q, kt, v, ktc, vc, tables, seg64, vec)


def _out_kernel(ya_ref, yb_ref, yc_ref, yd_ref, w_ref, x_ref, mod_ref, g2_ref, rwt_ref, o_ref, xn_ref, lg_ref):
    y = jnp.concatenate([ya_ref[0], yb_ref[0], yc_ref[0], yd_ref[0]], axis=-1)
    mod = mod_ref[0]
    x = x_ref[0] + mod[:, 2 * D_MODEL:3 * D_MODEL] * _dot(y, w_ref[...])
    o_ref[0] = x
    sh = mod[:, 3 * D_MODEL:4 * D_MODEL]
    sc = mod[:, 4 * D_MODEL:5 * D_MODEL]
    ms = jnp.mean(x * x, axis=-1, keepdims=True)
    h = x * lax.rsqrt(ms + EPS) * g2_ref[...]
    h = (h * (1.0 + sc) + sh).astype(BF16)
    xn_ref[0] = h
    lg_ref[0] = lax.dot_general(rwt_ref[...], h, (((1,), (1,)), ((), ())), preferred_element_type=F32)


def _out_proj(ya, yb, yc, yd, w_out, x, mod, g2, rwt, tm):
    b, n, d = x.shape
    tok = pl.BlockSpec((1, tm, GROUP_W), lambda i, t: (i, t, 0))
    xs = pl.BlockSpec((1, tm, d), lambda i, t: (i, t, 0))
    return pl.pallas_call(
        _out_kernel,
        out_shape=(jax.ShapeDtypeStruct((b, n, d), F32),
                   jax.ShapeDtypeStruct((b, n, d), BF16),
                   jax.ShapeDtypeStruct((b, N_EXPERTS, n), F32)),
        grid=(b, n // tm),
        in_specs=[tok, tok, tok, tok, _full((4 * GROUP_W, d)), xs,
                  pl.BlockSpec((1, 1, 6 * d), lambda i, t: (i, 0, 0)),
                  _full((1, d)), _full((N_EXPERTS, d))],
        out_specs=(xs, xs, pl.BlockSpec((1, N_EXPERTS, tm), lambda i, t: (i, 0, t))),
        compiler_params=_params("arbitrary", "arbitrary"),
        name="out_proj",
    )(ya, yb, yc, yd, w_out, x, mod, g2, rwt)


def _cumsum_excl(m, tri):
    n = m.shape[1]
    carry = jnp.zeros((m.shape[0], 1), F32)
    outs = []
    for j in range(n // GROUP_W):
        blk = m[:, j * GROUP_W:(j + 1) * GROUP_W]
        inc = _dot(blk.astype(BF16), tri)
        outs.append(inc - blk + carry)
        carry = carry + inc[:, GROUP_W - 1:GROUP_W]
    return jnp.concatenate(outs, axis=1)


def _route_kernel(lg_ref, tri_ref, pos_ref, gate_ref, post_ref, st_ref, *, n, cap, tile, nb):
    affs = []
    for i in range(nb):
        lg = lg_ref[i]
        e = jnp.exp(lg - jnp.max(lg, axis=0, keepdims=True))
        affs.append(e / jnp.sum(e, axis=0, keepdims=True))
        gate_ref[i] = affs[i]
    aff = affs[0] if nb == 1 else jnp.concatenate(affs, axis=0)

    def unresolved(state):
        lo, hi = state
        return jnp.max(jnp.where(lo < hi, 1.0, 0.0)) > 0.0

    def bisect(state):
        lo, hi = state
        mid = 0.5 * (lo + hi)
        mid = jnp.where(mid > lo, mid, hi)
        ge = aff >= mid
        cnt = jnp.sum(jnp.where(ge, 1.0, 0.0), axis=1, keepdims=True)
        least_ge = jnp.min(jnp.where(ge, aff, jnp.inf), axis=1, keepdims=True)
        most_lt = jnp.max(jnp.where(ge, NEG_INF, aff), axis=1, keepdims=True)
        up = cnt >= cap
        return jnp.where(up, least_ge, lo), jnp.where(up, hi, most_lt)

    thr, _ = lax.while_loop(unresolved, bisect, (jnp.min(aff, axis=1, keepdims=True),
                                                 jnp.max(aff, axis=1, keepdims=True)))
    gt = aff > thr
    eq = aff == thr
    need = cap - jnp.sum(jnp.where(gt, 1.0, 0.0), axis=1, keepdims=True)
    tri = tri_ref[...]
    rank_eq = _cumsum_excl(jnp.where(eq, 1.0, 0.0), tri)
    sel = jnp.where(gt, 1.0, jnp.where(eq, jnp.where(rank_eq < need, 1.0, 0.0), 0.0))
    cum = _cumsum_excl(sel, tri)
    pos = jnp.where(sel > 0.0, cum, -1.0)
    tile_lane = lax.broadcasted_iota(I32, (nb * N_EXPERTS, LANES), 1)
    starts = jnp.zeros((nb * N_EXPERTS, LANES), F32)
    for t in range(n // tile):
        starts = jnp.where(tile_lane == t, cum[:, t * tile:t * tile + 1], starts)
    for i in range(nb):
        rows = slice(i * N_EXPERTS, (i + 1) * N_EXPERTS)
        pos_ref[i] = pos[rows].astype(I32)
        st_ref[i] = starts[rows].astype(I32)
        padded = jnp.concatenate([pos[rows], jnp.full((LANES - N_EXPERTS, n), -1.0, F32)], axis=0)
        post_ref[i] = padded.T.astype(I32)


def _route(logits, tri, cap, tile):
    b, _, n = logits.shape
    nb = next(k for k in (4, 2, 1) if b % k == 0)
    em = pl.BlockSpec((nb, N_EXPERTS, n), lambda i: (i, 0, 0))
    kern = functools.partial(_route_kernel, n=n, cap=cap, tile=tile, nb=nb)
    return pl.pallas_call(
        kern,
        out_shape=(jax.ShapeDtypeStruct((b, N_EXPERTS, n), I32),
                   jax.ShapeDtypeStruct((b, N_EXPERTS, n), F32),
                   jax.ShapeDtypeStruct((b, n, LANES), I32),
                   jax.ShapeDtypeStruct((b, N_EXPERTS, LANES), I32)),
        grid=(b // nb,),
        in_specs=[em, _full((GROUP_W, GROUP_W))],
        out_specs=(em, em, pl.BlockSpec((nb, n, LANES), lambda i: (i, 0, 0)),
                   pl.BlockSpec((nb, N_EXPERTS, LANES), lambda i: (i, 0, 0))),
        compiler_params=_params("arbitrary"),
        name="router",
    )(logits, tri)


def _ffn_kernel(st_ref, xn_ref, pos_ref, gate_ref, wg32_ref, wu32_ref, wd32_ref, o_ref, wg_ref, wu_ref, wd_ref, xg_ref,
                *, bb, cap, n, tile, stride):
    @pl.when(pl.program_id(1) == 0)
    def _():
        wg_ref[...] = wg32_ref[0, 0].astype(BF16)
        wu_ref[...] = wu32_ref[0, 0].astype(BF16)
        wd_ref[...] = wd32_ref[0, 0].astype(BF16)

    expert = pl.ds(pl.program_id(0), 1)

    def step(gather):
        slot = lax.broadcasted_iota(I32, (cap, n), 0)
        hits = [pos_ref[i, expert, :] == slot for i in range(bb)]
        gs = [jnp.sum(jnp.where(hits[i], gate_ref[i, expert, :], 0.0), axis=1, keepdims=True) for i in range(bb)]
        g = gs[0] if bb == 1 else jnp.concatenate(gs, axis=0)
        gather(hits)
        xg = xg_ref[...].astype(BF16)
        hid = (jax.nn.silu(_dot(xg, wg_ref[...])) * _dot(xg, wu_ref[...])).astype(BF16)
        o = _dot(hid, wd_ref[...]) * g
        for i in range(bb):
            o_ref[i, 0] = o[i * cap:(i + 1) * cap].astype(BF16)

    def gather_dense(hits):
        for i in range(bb):
            xg_ref[i * cap:(i + 1) * cap, :] = _dot(jnp.where(hits[i], 1.0, 0.0).astype(BF16), xn_ref[i])

    if cap <= SLOT_WINDOW:
        step(gather_dense)
        return

    nt = n // tile
    win, fits = [], True
    for i in range(bb):
        base = (pl.program_id(1) * bb + i) * nt * stride * N_EXPERTS + pl.program_id(0)
        for t in range(nt):
            start = st_ref[base + t * stride * N_EXPERTS]
            end = st_ref[base + (t + 1) * stride * N_EXPERTS] if t + 1 < nt else cap
            a = jnp.minimum((start >> 4) << 4, cap - SLOT_WINDOW)
            win.append(a)
            fits = jnp.logical_and(fits, end - a <= SLOT_WINDOW)

    def gather_windowed(hits):
        del hits
        xg_ref[...] = jnp.zeros_like(xg_ref)
        wslot = lax.broadcasted_iota(I32, (SLOT_WINDOW, tile), 0)
        for i in range(bb):
            for t in range(nt):
                a = win[i * nt + t]
                p = pos_ref[i, expert, t * tile:(t + 1) * tile]
                onehot = jnp.where(p - a == wslot, 1.0, 0.0).astype(BF16)
                rows = pl.ds(pl.multiple_of(i * cap + a, 16), SLOT_WINDOW)
                xg_ref[rows, :] = xg_ref[rows, :] + _dot(onehot, xn_ref[i, t * tile:(t + 1) * tile, :])

    pl.when(fits)(lambda: step(gather_windowed))
    pl.when(jnp.logical_not(fits))(lambda: step(gather_dense))


def _expert_ffn(xn, pos, gate, starts, wg, wu, wd, layer, cap, bb, tile, stride):
    b, n, d = xn.shape
    assert b % bb == 0 and n % tile == 0, (b, bb, n, tile)
    wspec = pl.BlockSpec((1, 1, d, d), lambda e, j, st: (layer, e, 0, 0))
    sel = pl.BlockSpec((bb, N_EXPERTS, n), lambda e, j, st: (j, 0, 0))
    kern = functools.partial(_ffn_kernel, bb=bb, cap=cap, n=n, tile=tile, stride=stride)
    return pl.pallas_call(
        kern,
        out_shape=jax.ShapeDtypeStruct((b, N_EXPERTS, cap, d), BF16),
        grid_spec=pltpu.PrefetchScalarGridSpec(
            num_scalar_prefetch=1,
            grid=(N_EXPERTS, b // bb),
            in_specs=[pl.BlockSpec((bb, n, d), lambda e, j, st: (j, 0, 0)), sel, sel, wspec, wspec, wspec],
            out_specs=pl.BlockSpec((bb, 1, cap, d), lambda e, j, st: (j, e, 0, 0)),
            scratch_shapes=[pltpu.VMEM((d, d), BF16)] * 3 + [pltpu.VMEM((bb * cap, d), F32)]),
        compiler_params=_params("arbitrary", "arbitrary"),
        name="expert_ffn",
    )(starts, xn, pos, gate, wg, wu, wd)


def _scatter_kernel(st_ref, x_ref, mod_ref, pt_ref, o_ref, out_ref, *, tn, cap, nt, sub, window):
    for s in range(sub):
        _scatter_tile(st_ref, x_ref.at[0, s * tn:(s + 1) * tn], mod_ref, pt_ref.at[0, s * tn:(s + 1) * tn], o_ref,
                      out_ref.at[0, s * tn:(s + 1) * tn], pl.program_id(1) * sub + s,
                      tn=tn, cap=cap, nt=nt, window=window)


def _scatter_tile(st_ref, x_ref, mod_ref, pt_ref, o_ref, out_ref, t, *, tn, cap, nt, window):
    pos_t = pt_ref[...]
    g = mod_ref[0][:, 5 * D_MODEL:6 * D_MODEL]

    def dense():
        if cap % LANES == 0:
            slot = lax.broadcasted_iota(I32, (tn, cap), 1)
            onehot = jnp.concatenate(
                [jnp.where(pos_t[:, e:e + 1] == slot, 1.0, 0.0).astype(BF16) for e in range(N_EXPERTS)], axis=1)
        else:
            slot = lax.broadcasted_iota(I32, (tn, N_EXPERTS * cap), 1)
            acc = jnp.zeros((tn, N_EXPERTS * cap), F32)
            for e in range(N_EXPERTS):
                pe = pos_t[:, e:e + 1]
                acc = jnp.where(jnp.where(pe >= 0, pe + e * cap, -1) == slot, 1.0, acc)
            onehot = acc.astype(BF16)
        y = _dot(onehot, o_ref[0].reshape(N_EXPERTS * cap, D_MODEL))
        out_ref[...] = x_ref[...] + g * y

    if cap <= window:
        dense()
        return

    base = (pl.program_id(0) * nt + t) * N_EXPERTS
    nxt = jnp.minimum(t + 1, nt - 1)
    nbase = (pl.program_id(0) * nt + nxt) * N_EXPERTS
    win, fits = [], True
    for e in range(N_EXPERTS):
        start = st_ref[base + e]
        end = jnp.where(t + 1 < nt, st_ref[nbase + e], cap)
        a = jnp.minimum((start >> 4) << 4, cap - window)
        win.append(a)
        fits = jnp.logical_and(fits, end - a <= window)

    @pl.when(fits)
    def _():
        per = LANES // window
        lane = lax.broadcasted_iota(I32, (tn, LANES), 1)
        groups = []
        for e0 in range(0, N_EXPERTS, per):
            target = jnp.full((tn, LANES), -1, I32)
            for i in range(per):
                rel = pos_t[:, e0 + i:e0 + i + 1] - win[e0 + i]
                here = (lane >> (window.bit_length() - 1)) == i if per > 1 else None
                shifted = jnp.where(rel >= 0, rel + i * window, -1)
                target = shifted if here is None else jnp.where(here, shifted, target)
            groups.append(jnp.where(target == lane, 1.0, 0.0).astype(BF16))
        onehot = jnp.concatenate(groups, axis=1)
        rows = jnp.concatenate(
            [o_ref[0, e, pl.ds(pl.multiple_of(win[e], 16), window), :] for e in range(N_EXPERTS)], axis=0)
        out_ref[...] = x_ref[...] + g * _dot(onehot, rows)

    pl.when(jnp.logical_not(fits))(dense)


def _scatter(x, mod, pos_t, starts, o, cap, tn, window, sub):
    b, n, d = x.shape
    nt = n // tn
    xs = pl.BlockSpec((1, sub * tn, d), lambda i, t, st: (i, t, 0))
    kern = functools.partial(_scatter_kernel, tn=tn, cap=cap, nt=nt, sub=sub, window=window)
    return pl.pallas_call(
        kern,
        out_shape=jax.ShapeDtypeStruct((b, n, d), F32),
        grid_spec=pltpu.PrefetchScalarGridSpec(
            num_scalar_prefetch=1,
            grid=(b, nt // sub),
            in_specs=[xs,
                      pl.BlockSpec((1, 1, 6 * d), lambda i, t, st: (i, 0, 0)),
                      pl.BlockSpec((1, sub * tn, LANES), lambda i, t, st: (i, t, 0)),
                      pl.BlockSpec((1, N_EXPERTS, cap, d), lambda i, t, st: (i, 0, 0, 0))],
            out_specs=xs),
        compiler_params=_params("arbitrary", "arbitrary"),
        name="scatter_add",
    )(starts, x, mod, pos_t, o)


@functools.lru_cache(maxsize=None)
def _np_consts():
    lane = np.arange(GROUP_W)
    seg32 = (lane[:, None] // DIFF_D == lane[None, :] // DIFF_D).astype(np.float32)
    seg64 = (lane[:, None] // HEAD_DIM == lane[None, :] // HEAD_DIM).astype(np.float32)
    ang = 2.0 * np.pi * ((lane[:, None] % HEAD_DIM) * (lane[None, :] % HEAD_DIM) % HEAD_DIM) / HEAD_DIM
    cc = np.cos(ang) * seg64
    ss = np.sin(ang) * seg64
    tri = (lane[:, None] <= lane[None, :]).astype(np.float32)
    return dict(seg32=seg32, seg64=seg64, cc=cc, ss=ss, tri=tri)


@functools.lru_cache(maxsize=None)
def _np_dft(n):
    idx = (np.arange(n, dtype=np.int64)[:, None] * np.arange(n, dtype=np.int64)[None, :]) % n
    ang = 2.0 * np.pi * idx.astype(np.float64) / n
    return np.cos(ang).astype(np.float32), np.sin(ang).astype(np.float32)


@functools.lru_cache(maxsize=None)
def _np_rope(n):
    half = DIFF_D // 2
    inv = 1.0 / (ROPE_BASE ** (np.arange(0, half, 2, dtype=np.float32) / half))
    t = np.arange(n)
    row = (t // GRID_W).astype(np.float32)[:, None] * inv
    col = (t % GRID_W).astype(np.float32)[:, None] * inv
    nf = inv.shape[0]
    d = np.arange(GROUP_W) % DIFF_D
    f = d % nf
    is_col = d >= half
    second = (d % half) >= nf
    ang = np.where(is_col[None, :], col[:, f], row[:, f])
    c = np.cos(ang).astype(np.float32)
    s = np.sin(ang).astype(np.float32)
    s = np.where(second[None, :], s, -s)
    return c, s


@functools.lru_cache(maxsize=None)
def _np_na_index(n_rows):
    n_steps = n_rows // NA_QROWS
    reps = [0, 1, 2, n_steps - 2, n_steps - 1]
    tq = NA_QROWS * GRID_W
    roff = np.zeros((len(reps), NA_QROWS, NA_WIN_ROWS), np.int32)
    valid = np.zeros((len(reps), tq, NA_WIN), bool)
    for ci, t in enumerate(reps):
        ws = int(np.clip(NA_QROWS * t - NA_ROWS // 2, 0, n_rows - NA_WIN_ROWS))
        qi = np.arange(tq)
        r = NA_QROWS * t + qi // GRID_W
        qcol = qi % GRID_W
        kk = np.arange(NA_WIN)
        krow = ws + kk // GRID_W
        kcol = kk % GRID_W
        rstart = np.clip(r - NA_ROWS // 2, 0, n_rows - NA_ROWS)
        wstart = np.clip(qcol - NA_COLS // 2, 0, GRID_W - NA_COLS)
        vr = (krow[None, :] >= rstart[:, None]) & (krow[None, :] < rstart[:, None] + NA_ROWS)
        vc = (kcol[None, :] >= wstart[:, None]) & (kcol[None, :] < wstart[:, None] + NA_COLS)
        valid[ci] = vr & vc
        rows_q = NA_QROWS * t + np.arange(NA_QROWS)
        rows_k = ws + np.arange(NA_WIN_ROWS)
        roff[ci] = np.clip(rows_k[None, :] - rows_q[:, None] + NA_ROWS - 1, 0, 2 * NA_ROWS - 2)
    return roff, valid


def _na_table_kernel(r_ref, mask_ref, o_ref, toep_ref, *, roff):
    n_off = 2 * NA_ROWS - 1
    for h in range(HEADS):
        for ro in range(n_off):
            row = jnp.broadcast_to(r_ref[0, h, ro:ro + 1, :], (GRID_W, LANES))
            toep_ref[h, ro] = pltpu.roll(row, 0, 1, stride=1, stride_axis=0)
    lane = lax.broadcasted_iota(I32, (GRID_W, LANES), 1)
    n_cls = roff.shape[0]
    for ci in range(n_cls):
        for h in range(HEADS):
            for rr in range(NA_QROWS):
                r0 = h * NA_QROWS * GRID_W + rr * GRID_W
                for kp in range(NA_WIN_ROWS // 2):
                    left = toep_ref[h, int(roff[ci, rr, 2 * kp])]
                    right = pltpu.roll(toep_ref[h, int(roff[ci, rr, 2 * kp + 1])], GRID_W, 1)
                    bias = jnp.where(lane < GRID_W, left, right) * LOG2E
                    o_ref[0, ci, r0:r0 + GRID_W, kp * LANES:(kp + 1) * LANES] = (
                        bias + mask_ref[ci, rr * GRID_W:(rr + 1) * GRID_W, kp * LANES:(kp + 1) * LANES])


def _na_tables(rpb_all, n_rows):
    roff, valid = _np_na_index(n_rows)
    n_cls = roff.shape[0]
    depth = rpb_all.shape[0]
    n_off = 2 * NA_ROWS - 1
    r = jnp.concatenate([rpb_all[..., NA_COLS - 1:], jnp.zeros(rpb_all.shape[:-1] + (LANES - 2 * NA_COLS + 1,), F32),
                         rpb_all[..., :NA_COLS - 1]], axis=-1).astype(F32)
    mask = jnp.asarray(np.where(valid, 0.0, NEG_INF).astype(np.float32))
    tq = NA_QROWS * GRID_W
    kern = functools.partial(_na_table_kernel, roff=roff)
    return pl.pallas_call(
        kern,
        out_shape=jax.ShapeDtypeStruct((depth, n_cls, HEADS * tq, NA_WIN), F32),
        grid=(depth,),
        in_specs=[pl.BlockSpec((1, HEADS, n_off, LANES), lambda l: (l, 0, 0, 0)),
                  _full((n_cls, tq, NA_WIN))],
        out_specs=pl.BlockSpec((1, n_cls, HEADS * tq, NA_WIN), lambda l: (l, 0, 0, 0)),
        scratch_shapes=[pltpu.VMEM((HEADS, n_off, GRID_W, LANES), F32)],
        compiler_params=_params("arbitrary"),
        name="na_bias_table",
    )(r, mask)


class _Tiles(NamedTuple):
    proj: int
    dft: int
    attn: int
    na_pairs: int
    moe: int
    ffn_samples: int


def _latent_tiles(n):
    return _Tiles(proj=min(n, 1024), dft=min(n, 2048), attn=min(n, 512), na_pairs=min(8, n // (NA_QROWS * GRID_W)),
                  moe=min(n, 512), ffn_samples=2)
def _moe(x, xn, logits, mod, tri, wg, wu, wd, layer, bb, tn):
    n = x.shape[1]
    cap = CAPACITY_FACTOR * n // N_EXPERTS
    stride = 2 if cap > SLOT_WINDOW else 1
    ts = tn // stride
    pos, gate, pos_t, starts = _route(logits, tri, cap, ts)
    starts = starts[:, :, :n // ts].transpose(0, 2, 1).reshape(-1)
    o = _expert_ffn(xn, pos, gate, starts, wg, wu, wd, layer, cap, bb, tn, stride)
    return _scatter(x, mod, pos_t, starts, o, cap, ts, SLOT_WINDOW // stride, min(8, n // ts))


def kernel(x, c, ctx, c_ctx, ada_w, ada_b, norm1_g, norm2_g, w_in, w_out, head_out_g, sgu_w, sgu_b, diff_qn_g, diff_kn_g, diff_lambda, na_qn_g, na_kn_g, na_rpb, router_w, exp_w_gate, exp_w_up, exp_w_down):
    b, n, d = x.shape
    n_ctx = ctx.shape[1]
    npc = _np_consts()
    consts = {k: jnp.asarray(v, F32).astype(BF16) for k, v in npc.items()}
    seg64, tri = consts["seg64"], consts["tri"]
    cn, sn = (jnp.asarray(a, F32).astype(BF16) for a in _np_dft(n))
    cn_c, sn_c = (jnp.asarray(a, F32).astype(BF16) for a in _np_dft(n_ctx))
    rope = tuple(jnp.asarray(a, F32) for a in _np_rope(n))

    pad = (-(b + 1)) % 8
    c_rows = jnp.concatenate([c, c_ctx[None, :], jnp.zeros((pad, d), F32)], axis=0)
    mod_all = _modulation(c_rows, ada_w, ada_b)
    na_tables = _na_tables(na_rpb, n // GRID_W)

    xc = ctx
    for l in range(DEPTH):
        last = l == DEPTH - 1
        lam_init = 0.8 - 0.6 * math.exp(-0.3 * l)
        mod = mod_all[l, :b][:, None, :]
        mod_c = jnp.broadcast_to(mod_all[l, b][None, None, :], (b, 1, 6 * d))
        g1 = norm1_g[l][None, :]
        g2 = norm2_g[l][None, :]
        w_in_l = w_in[l].astype(BF16)
        w_out_l = w_out[l].astype(BF16)
        sguw = sgu_w[l].astype(BF16).transpose(1, 0, 2).reshape(CHUNK, HEADS * CHUNK)
        sgub = jnp.repeat(sgu_b[l].T, HEAD_DIM, axis=1)
        hg = head_out_g[l].reshape(4, GROUP_W)
        vec = jnp.stack([jnp.tile(diff_qn_g[l], GROUP_W // DIFF_D), jnp.tile(diff_kn_g[l], GROUP_W // DIFF_D),
                         jnp.tile(na_qn_g[l], HEADS), jnp.tile(na_kn_g[l], HEADS),
                         hg[0], hg[1], hg[2], hg[3],
                         jnp.broadcast_to(jnp.max(jnp.abs(na_rpb[l])) * LOG2E, (GROUP_W,))], axis=0).astype(F32)
        vec = jnp.pad(vec, ((0, VEC_ROWS - vec.shape[0]), (0, 0)))
        lam = diff_lambda[l].astype(F32)
        rwt = router_w[l].T.astype(BF16)
        experts = (exp_w_gate, exp_w_up, exp_w_down)

        tl = _latent_tiles(n)
        ya, zc, zs, qc, kct, vc, qd, kdt, vd = _in_proj(x, mod, g1, w_in_l, consts, sguw, sgub, vec, rope, tl.proj)
        if last:
            kct_c, vc_c, kdt_c, vd_c = _in_proj(
                xc, mod_c, g1, w_in_l, consts, sguw, sgub, vec, None, n_ctx, kv_only=True)
        else:
            ya_c, zc_c, zs_c, qc_c, kct_c, vc_c, qd_c, kdt_c, vd_c = _in_proj(
                xc, mod_c, g1, w_in_l, consts, sguw, sgub, vec, None, n_ctx)

        yb = _fourier(zc, zs, cn, sn, seg64, vec, tl.dft)
        yc = _attention(qc, [(kct, vc), (kct_c, vc_c)], lam, lam_init, seg64, vec, 6, 1.0 - lam_init,
                        tl.attn, tl.attn)
        yd = _na_attention(qd, kdt, vd, kdt_c, vd_c, na_tables, l, seg64, vec, tl.na_pairs)
        x, xn, logits = _out_proj(ya, yb, yc, yd, w_out_l, x, mod, g2, rwt, tl.proj)
        x = _moe(x, xn, logits, mod, tri, *experts, l, tl.ffn_samples, tl.moe)

        if not last:
            yb_c = _fourier(zc_c, zs_c, cn_c, sn_c, seg64, vec, n_ctx)
            yc_c = _attention(qc_c, [(kct_c, vc_c)], lam, lam_init, seg64, vec, 6, 1.0 - lam_init, n_ctx)
            yd_c = _attention(qd_c, [(kdt_c, vd_c)], None, 0.0, seg64, vec, 7, 1.0, n_ctx)
            xc, xn_c, logits_c = _out_proj(ya_c, yb_c, yc_c, yd_c, w_out_l, xc, mod_c, g2, rwt, n_ctx)
            xc = _moe(xc, xn_c, logits_c, mod_c, tri, *experts, l, b, n_ctx)
    return x
```

```python
import functools
import math
from typing import NamedTuple

import numpy as np
import jax
import jax.numpy as jnp
from jax import lax
from jax.experimental import pallas as pl
from jax.experimental.pallas import tpu as pltpu

F32 = jnp.float32
BF16 = jnp.bfloat16
I32 = jnp.int32

D_MODEL = 1024
DEPTH = 2
GRID_W = 64
LANES = 128
HEAD_DIM = 64
LOG2_HEAD_DIM = 6
GROUP_W = 256
HEADS = GROUP_W // HEAD_DIM
CHUNK = 128
DIFF_D = HEAD_DIM // 2
NA_ROWS = 8
NA_COLS = 16
N_EXPERTS = 16
CAPACITY_FACTOR = 2
ROPE_BASE = 10000.0
EPS = 1e-6
IN_W = 9 * GROUP_W
LOG2E = 1.4426950408889634

VMEM_LIMIT_BYTES = 60 * 1024 * 1024
NA_QROWS = 2
NA_WIN_ROWS = NA_ROWS + 2
NA_WIN = NA_WIN_ROWS * GRID_W
NA_HEAD_STACK = 1
NEG_INF = float("-inf")
VEC_ROWS = 16
SLOT_WINDOW = 128
SAFE_EXP2_BOUND = 48.0
BOUND_SLACK = 1.02


def _dot(a, b):
    return jnp.dot(a, b, preferred_element_type=F32)


def _params(*sem):
    return pltpu.CompilerParams(dimension_semantics=sem, vmem_limit_bytes=VMEM_LIMIT_BYTES)


def _full(shape):
    nd = len(shape)
    return pl.BlockSpec(shape, lambda *_: (0,) * nd)


def _seg_rms(x, seg, width):
    ss = _dot((x * x).astype(BF16), seg)
    return x * lax.rsqrt(ss * (1.0 / width) + EPS)


def _mod_kernel(c_ref, w_ref, b_ref, o_ref):
    s = jax.nn.silu(c_ref[...]).astype(BF16)
    o_ref[0] = _dot(s, w_ref[0].astype(BF16)) + b_ref[0]


def _modulation(c_rows, ada_w, ada_b):
    depth, d, w6 = ada_w.shape
    r = c_rows.shape[0]
    tn = 1024
    return pl.pallas_call(
        _mod_kernel,
        out_shape=jax.ShapeDtypeStruct((depth, r, w6), F32),
        grid=(depth, w6 // tn),
        in_specs=[
            pl.BlockSpec((r, d), lambda l, j: (0, 0)),
            pl.BlockSpec((1, d, tn), lambda l, j: (l, 0, j)),
            pl.BlockSpec((1, 1, tn), lambda l, j: (l, 0, j)),
        ],
        out_specs=pl.BlockSpec((1, r, tn), lambda l, j: (l, 0, j)),
        compiler_params=_params("arbitrary", "arbitrary"),
        name="modulation",
    )(c_rows, ada_w, ada_b.reshape(depth, 1, w6))


def _rope(x, c, s, lane):
    fwd = pltpu.roll(x, GROUP_W - 8, 1)
    bwd = pltpu.roll(x, 8, 1)
    partner = jnp.where((lane & 8) == 0, fwd, bwd)
    return x * c + partner * s


def _in_kernel(*refs, tm, use_rope, kv_only, c_scale, d_scale):
    (x_ref, mod_ref, g1_ref, w_ref, seg32_ref, seg64_ref, cc_ref, ss_ref, sguw_ref, sgub_ref, vec_ref) = refs[:11]
    rest = refs[11:]
    if use_rope:
        ropec_ref, ropes_ref = rest[:2]
        rest = rest[2:]
    if kv_only:
        kct_ref, vc_ref, kdt_ref, vd_ref = rest
    else:
        ya_ref, zc_ref, zs_ref, qc_ref, kct_ref, vc_ref, qd_ref, kdt_ref, vd_ref = rest

    x = x_ref[0]
    mod = mod_ref[0]
    sh = mod[:, 0:D_MODEL]
    sc = mod[:, D_MODEL:2 * D_MODEL]
    ms = jnp.mean(x * x, axis=-1, keepdims=True)
    h = x * lax.rsqrt(ms + EPS) * g1_ref[...]
    h = (h * (1.0 + sc) + sh).astype(BF16)
    lane = lax.broadcasted_iota(I32, (1, GROUP_W), 1)
    head = lane >> LOG2_HEAD_DIM
    seg32 = seg32_ref[...]
    seg64 = seg64_ref[...]
    vec = vec_ref[...]

    if kv_only:
        pc = _dot(h, w_ref[:, 4 * GROUP_W:6 * GROUP_W])
        k = _seg_rms(pc[:, 0:GROUP_W], seg32, DIFF_D) * vec[1:2]
        kct_ref[0] = k.T.astype(BF16)
        vc_ref[0] = pc[:, GROUP_W:2 * GROUP_W].astype(BF16)
        pd = _dot(h, w_ref[:, 7 * GROUP_W:9 * GROUP_W])
        kd = _seg_rms(pd[:, 0:GROUP_W], seg64, HEAD_DIM) * vec[3:4]
        kdt_ref[0] = kd.T.astype(BF16)
        vd_ref[0] = pd[:, GROUP_W:2 * GROUP_W].astype(BF16)
        return

    z = jax.nn.gelu(_dot(h, w_ref[:, 0:2 * GROUP_W]))
    u = z[:, 0:GROUP_W]
    vn = _seg_rms(z[:, GROUP_W:2 * GROUP_W], seg64, HEAD_DIM).astype(BF16)
    rows = []
    for c in range(tm // CHUNK):
        vch = vn[c * CHUNK:(c + 1) * CHUNK]
        stacked = jnp.concatenate([jnp.where(head == hh, vch, jnp.zeros_like(vch)) for hh in range(HEADS)], axis=0)
        rows.append(_dot(sguw_ref[...], stacked) + sgub_ref[...])
    ya = u * jnp.concatenate(rows, axis=0)
    ya_ref[0] = (_seg_rms(ya, seg64, HEAD_DIM) * vec[4:5]).astype(BF16)

    zb = _dot(h, w_ref[:, 2 * GROUP_W:3 * GROUP_W]).astype(BF16)
    zc_ref[0] = _dot(zb, cc_ref[...]).astype(BF16)
    zs_ref[0] = _dot(zb, ss_ref[...]).astype(BF16)

    pc = _dot(h, w_ref[:, 3 * GROUP_W:6 * GROUP_W])
    q = _seg_rms(pc[:, 0:GROUP_W], seg32, DIFF_D) * vec[0:1]
    k = _seg_rms(pc[:, GROUP_W:2 * GROUP_W], seg32, DIFF_D) * vec[1:2]
    if use_rope:
        rc = ropec_ref[...]
        rs = ropes_ref[...]
        q = _rope(q, rc, rs, lane)
        k = _rope(k, rc, rs, lane)
    qc_ref[0] = (q * c_scale).astype(BF16)
    kct_ref[0] = k.T.astype(BF16)
    vc_ref[0] = pc[:, 2 * GROUP_W:3 * GROUP_W].astype(BF16)

    pd = _dot(h, w_ref[:, 6 * GROUP_W:9 * GROUP_W])
    qd = _seg_rms(pd[:, 0:GROUP_W], seg64, HEAD_DIM) * vec[2:3]
    kd = _seg_rms(pd[:, GROUP_W:2 * GROUP_W], seg64, HEAD_DIM) * vec[3:4]
    qd_ref[0] = (qd * d_scale).astype(BF16)
    kdt_ref[0] = kd.T.astype(BF16)
    vd_ref[0] = pd[:, 2 * GROUP_W:3 * GROUP_W].astype(BF16)


def _in_proj(x, mod, g1, w_in, consts, sguw, sgub, vec, rope, tm, kv_only=False):
    b, n, d = x.shape
    use_rope = rope is not None
    tok = pl.BlockSpec((1, tm, GROUP_W), lambda i, t: (i, t, 0))
    tok_t = pl.BlockSpec((1, GROUP_W, tm), lambda i, t: (i, 0, t))
    in_specs = [
        pl.BlockSpec((1, tm, d), lambda i, t: (i, t, 0)),
        pl.BlockSpec((1, 1, 6 * d), lambda i, t: (i, 0, 0)),
        _full((1, d)),
        _full((d, IN_W)),
        _full((GROUP_W, GROUP_W)), _full((GROUP_W, GROUP_W)), _full((GROUP_W, GROUP_W)), _full((GROUP_W, GROUP_W)),
        _full((CHUNK, HEADS * CHUNK)),
        _full((CHUNK, GROUP_W)),
        _full((VEC_ROWS, GROUP_W)),
    ]
    args = [x, mod, g1, w_in, consts["seg32"], consts["seg64"], consts["cc"], consts["ss"], sguw, sgub, vec]
    if use_rope:
        in_specs += [pl.BlockSpec((tm, GROUP_W), lambda i, t: (t, 0))] * 2
        args += list(rope)
    sd = jax.ShapeDtypeStruct((b, n, GROUP_W), BF16)
    sdt = jax.ShapeDtypeStruct((b, GROUP_W, n), BF16)
    kern = functools.partial(
        _in_kernel, tm=tm, use_rope=use_rope, kv_only=kv_only,
        c_scale=(DIFF_D ** -0.5) * LOG2E, d_scale=(HEAD_DIM ** -0.5) * LOG2E)
    return pl.pallas_call(
        kern,
        out_shape=(sdt, sd, sdt, sd) if kv_only else (sd, sd, sd, sd, sdt, sd, sd, sdt, sd),
        grid=(b, n // tm),
        in_specs=in_specs,
        out_specs=(tok_t, tok, tok_t, tok) if kv_only else (tok, tok, tok, tok, tok_t, tok, tok, tok_t, tok),
        compiler_params=_params("arbitrary", "arbitrary"),
        name="in_proj",
    )(*args)


def _fourier_kernel(cn_ref, sn_ref, zc_ref, zs_ref, seg64_ref, vec_ref, o_ref, *, norm):
    y = (_dot(cn_ref[...], zc_ref[0]) - _dot(sn_ref[...], zs_ref[0])) * norm
    o_ref[0] = (_seg_rms(y, seg64_ref[...], HEAD_DIM) * vec_ref[5:6]).astype(BF16)


def _fourier(zc, zs, cn, sn, seg64, vec, tn):
    b, n, _ = zc.shape
    kern = functools.partial(_fourier_kernel, norm=1.0 / math.sqrt(n * HEAD_DIM))
    return pl.pallas_call(
        kern,
        out_shape=jax.ShapeDtypeStruct((b, n, GROUP_W), BF16),
        grid=(n // tn, b),
        in_specs=[
            pl.BlockSpec((tn, n), lambda t, i: (t, 0)),
            pl.BlockSpec((tn, n), lambda t, i: (t, 0)),
            pl.BlockSpec((1, n, GROUP_W), lambda t, i: (i, 0, 0)),
            pl.BlockSpec((1, n, GROUP_W), lambda t, i: (i, 0, 0)),
            _full((GROUP_W, GROUP_W)),
            _full((VEC_ROWS, GROUP_W)),
        ],
        out_specs=pl.BlockSpec((1, tn, GROUP_W), lambda t, i: (i, t, 0)),
        compiler_params=_params("arbitrary", "arbitrary"),
        name="fourier",
    )(cn, sn, zc, zs, seg64, vec)


def _attn_kernel(*refs, n_src, diff, lam_init, chunk, tq, vec_row, out_scale):
    q_ref = refs[0]
    rest = refs[1 + 2 * n_src:]
    if diff:
        lam_ref = rest[0]
        rest = rest[1:]
    seg64_ref, vec_ref, o_ref = rest

    q = q_ref[0]
    lane = lax.broadcasted_iota(I32, (1, GROUP_W), 1)
    if diff:
        lf = lam_ref[...]
        lam = (jnp.exp(jnp.sum(lf[0:1] * lf[1:2], axis=-1, keepdims=True))
               - jnp.exp(jnp.sum(lf[2:3] * lf[3:4], axis=-1, keepdims=True)) + lam_init)

    chunks = []
    for i in range(n_src):
        kt_ref, v_ref = refs[1 + 2 * i], refs[2 + 2 * i]
        nk = kt_ref.shape[2]
        chunks += [(kt_ref, v_ref, c0, min(chunk, nk - c0)) for c0 in range(0, nk, chunk)]

    width = DIFF_D if diff else HEAD_DIM
    gq = vec_ref[0:1] if diff else vec_ref[2:3]
    gk = vec_ref[1:2] if diff else vec_ref[3:4]
    bound = jnp.max(jnp.abs(gq)) * jnp.max(jnp.abs(gk)) * (math.sqrt(width) * LOG2E * BOUND_SLACK)

    def attend_bounded(sel):
        qm = jnp.where(sel, q, jnp.zeros_like(q))
        l = jnp.zeros((tq, 1), F32)
        acc = jnp.zeros((tq, GROUP_W), F32)
        for kt_ref, v_ref, c0, ck in chunks:
            p = jnp.exp2(_dot(qm, kt_ref[0, :, c0:c0 + ck]))
            l = l + jnp.sum(p, axis=-1, keepdims=True)
            acc = acc + _dot(p.astype(BF16), v_ref[0, c0:c0 + ck, :])
        return acc * (1.0 / l)

    def attend_online(sel):
        qm = jnp.where(sel, q, jnp.zeros_like(q))
        m = jnp.full((tq, 1), NEG_INF, F32)
        l = jnp.zeros((tq, 1), F32)
        acc = jnp.zeros((tq, GROUP_W), F32)
        for kt_ref, v_ref, c0, ck in chunks:
            s = _dot(qm, kt_ref[0, :, c0:c0 + ck])
            m_new = jnp.maximum(m, jnp.max(s, axis=-1, keepdims=True))
            alpha = jnp.exp2(m - m_new)
            p = jnp.exp2(s - m_new)
            l = alpha * l + jnp.sum(p, axis=-1, keepdims=True)
            acc = alpha * acc + _dot(p.astype(BF16), v_ref[0, c0:c0 + ck, :])
            m = m_new
        return acc * (1.0 / l)

    def head_out(attend, h):
        if diff:
            return (attend((lane >> (LOG2_HEAD_DIM - 1)) == 2 * h)
                    - lam * attend((lane >> (LOG2_HEAD_DIM - 1)) == 2 * h + 1))
        return attend((lane >> LOG2_HEAD_DIM) == h)

    def finish(out):
        y = _seg_rms(out, seg64_ref[...], HEAD_DIM) * vec_ref[vec_row:vec_row + 1]
        o_ref[0] = (y * out_scale).astype(BF16)

    def run_bounded():
        out = jnp.zeros((tq, GROUP_W), F32)
        for h in range(HEADS):
            out = jnp.where((lane >> LOG2_HEAD_DIM) == h, head_out(attend_bounded, h), out)
        finish(out)

    def run_online():
        def body(h, out):
            return jnp.where((lane >> LOG2_HEAD_DIM) == h, head_out(attend_online, h), out)

        finish(lax.fori_loop(0, HEADS, body, jnp.zeros((tq, GROUP_W), F32)))

    small = bound <= SAFE_EXP2_BOUND
    pl.when(small)(run_bounded)
    pl.when(jnp.logical_not(small))(run_online)


def _attention(q, srcs, lam, lam_init, seg64, vec, vec_row, out_scale, tq, chunk=768):
    b, nq, _ = q.shape
    diff = lam is not None
    in_specs = [pl.BlockSpec((1, tq, GROUP_W), lambda i, t: (i, t, 0))]
    args = [q]
    for kt, v in srcs:
        nk = v.shape[1]
        in_specs += [pl.BlockSpec((1, GROUP_W, nk), lambda i, t: (i, 0, 0)),
                     pl.BlockSpec((1, nk, GROUP_W), lambda i, t: (i, 0, 0))]
        args += [kt, v]
    if diff:
        in_specs.append(_full((4, DIFF_D)))
        args.append(lam)
    in_specs += [_full((GROUP_W, GROUP_W)), _full((VEC_ROWS, GROUP_W))]
    args += [seg64, vec]
    kern = functools.partial(_attn_kernel, n_src=len(srcs), diff=diff, lam_init=lam_init, chunk=chunk, tq=tq,
                             vec_row=vec_row, out_scale=out_scale)
    return pl.pallas_call(
        kern,
        out_shape=jax.ShapeDtypeStruct((b, nq, GROUP_W), BF16),
        grid=(b, nq // tq),
        in_specs=in_specs,
        out_specs=pl.BlockSpec((1, tq, GROUP_W), lambda i, t: (i, t, 0)),
        compiler_params=_params("arbitrary", "arbitrary"),
        name="diff_attention" if diff else "ctx_attention",
    )(*args)


def _na_kernel(q_ref, kt_ref, v_ref, ktc_ref, vc_ref, tab_ref, seg64_ref, vec_ref, o_ref, *, n_rows, pairs):
    n_steps = n_rows // NA_QROWS
    tq = NA_QROWS * GRID_W
    lane = lax.broadcasted_iota(I32, (1, GROUP_W), 1)
    head = lane >> LOG2_HEAD_DIM
    ktc = ktc_ref[0]
    vc = vc_ref[0]
    bound = (jnp.max(jnp.abs(vec_ref[2:3])) * jnp.max(jnp.abs(vec_ref[3:4])) * (math.sqrt(HEAD_DIM) * LOG2E * BOUND_SLACK)
             + jnp.max(vec_ref[8:9]))

    def run(bounded):
        for pi in range(pairs):
            t = pl.program_id(1) * pairs + pi
            ws = jnp.clip(NA_QROWS * t - NA_ROWS // 2, 0, n_rows - NA_WIN_ROWS)
            k0 = pl.multiple_of(ws * GRID_W, 128)
            tid = jnp.where(t < 2, t, jnp.where(t < n_steps - 2, 2, t - (n_steps - 5)))
            q = q_ref[0, pi * tq:(pi + 1) * tq, :]
            out = jnp.zeros((tq, GROUP_W), F32)
            for h0 in range(0, HEADS, NA_HEAD_STACK):
                hs = range(h0, h0 + NA_HEAD_STACK)
                qs = jnp.concatenate([jnp.where(head == hh, q, jnp.zeros_like(q)) for hh in hs], axis=0)
                s_loc = (_dot(qs, kt_ref[0, :, pl.ds(k0, NA_WIN)])
                         + tab_ref[0, tid, h0 * tq:(h0 + NA_HEAD_STACK) * tq, :])
                s_ctx = _dot(qs, ktc)
                if bounded:
                    p_loc = jnp.exp2(s_loc)
                    p_ctx = jnp.exp2(s_ctx)
                else:
                    m = jnp.maximum(jnp.max(s_loc, axis=-1, keepdims=True), jnp.max(s_ctx, axis=-1, keepdims=True))
                    p_loc = jnp.exp2(s_loc - m)
                    p_ctx = jnp.exp2(s_ctx - m)
                l = jnp.sum(p_loc, axis=-1, keepdims=True) + jnp.sum(p_ctx, axis=-1, keepdims=True)
                o = _dot(p_loc.astype(BF16), v_ref[0, pl.ds(k0, NA_WIN), :]) + _dot(p_ctx.astype(BF16), vc)
                o = o * (1.0 / l)
                for i, hh in enumerate(hs):
                    out = jnp.where(head == hh, o[i * tq:(i + 1) * tq], out)
            o_ref[0, pi * tq:(pi + 1) * tq, :] = (
                _seg_rms(out, seg64_ref[...], HEAD_DIM) * vec_ref[7:8]).astype(BF16)

    small = bound <= SAFE_EXP2_BOUND
    pl.when(small)(lambda: run(True))
    pl.when(jnp.logical_not(small))(lambda: run(False))


def _na_attention(q, kt, v, ktc, vc, tables, layer, seg64, vec, pairs):
    b, n, _ = q.shape
    nc = vc.shape[1]
    tq = pairs * NA_QROWS * GRID_W
    kern = functools.partial(_na_kernel, n_rows=n // GRID_W, pairs=pairs)
    return pl.pallas_call(
        kern,
        out_shape=jax.ShapeDtypeStruct((b, n, GROUP_W), BF16),
        grid=(b, n // tq),
        in_specs=[
            pl.BlockSpec((1, tq, GROUP_W), lambda i, t: (i, t, 0)),
            pl.BlockSpec((1, GROUP_W, n), lambda i, t: (i, 0, 0)),
            pl.BlockSpec((1, n, GROUP_W), lambda i, t: (i, 0, 0)),
            pl.BlockSpec((1, GROUP_W, nc), lambda i, t: (i, 0, 0)),
            pl.BlockSpec((1, nc, GROUP_W), lambda i, t: (i, 0, 0)),
            pl.BlockSpec((1,) + tables.shape[1:], lambda i, t: (layer, 0, 0, 0)),
            _full((GROUP_W, GROUP_W)),
            _full((VEC_ROWS, GROUP_W)),
        ],
        out_specs=pl.BlockSpec((1, tq, GROUP_W), lambda i, t: (i, t, 0)),
        compiler_params=_params("arbitrary", "arbitrary"),
        name="neighbourhood_attention",
    )(q, kt, v, ktc, vc, tables, seg64, vec)


def _out_kernel(ya_ref, yb_ref, yc_ref, yd_ref, w_ref, x_ref, mod_ref, g2_ref, rwt_ref, o_ref, xn_ref, lg_ref):
    y = jnp.concatenate([ya_ref[0], yb_ref[0], yc_ref[0], yd_ref[0]], axis=-1)
    mod = mod_ref[0]
    x = x_ref[0] + mod[:, 2 * D_MODEL:3 * D_MODEL] * _dot(y, w_ref[...])
    o_ref[0] = x
    sh = mod[:, 3 * D_MODEL:4 * D_MODEL]
    sc = mod[:, 4 * D_MODEL:5 * D_MODEL]
    ms = jnp.mean(x * x, axis=-1, keepdims=True)
    h = x * lax.rsqrt(ms + EPS) * g2_ref[...]
    h = (h * (1.0 + sc) + sh).astype(BF16)
    xn_ref[0] = h
    lg_ref[0] = lax.dot_general(rwt_ref[...], h, (((1,), (1,)), ((), ())), preferred_element_type=F32)


def _out_proj(ya, yb, yc, yd, w_out, x, mod, g2, rwt, tm):
    b, n, d = x.shape
    tok = pl.BlockSpec((1, tm, GROUP_W), lambda i, t: (i, t, 0))
    xs = pl.BlockSpec((1, tm, d), lambda i, t: (i, t, 0))
    return pl.pallas_call(
        _out_kernel,
        out_shape=(jax.ShapeDtypeStruct((b, n, d), F32),
                   jax.ShapeDtypeStruct((b, n, d), BF16),
                   jax.ShapeDtypeStruct((b, N_EXPERTS, n), F32)),
        grid=(b, n // tm),
        in_specs=[tok, tok, tok, tok, _full((4 * GROUP_W, d)), xs,
                  pl.BlockSpec((1, 1, 6 * d), lambda i, t: (i, 0, 0)),
                  _full((1, d)), _full((N_EXPERTS, d))],
        out_specs=(xs, xs, pl.BlockSpec((1, N_EXPERTS, tm), lambda i, t: (i, 0, t))),
        compiler_params=_params("arbitrary", "arbitrary"),
        name="out_proj",
    )(ya, yb, yc, yd, w_out, x, mod, g2, rwt)


def _cumsum_excl(m, tri):
    n = m.shape[1]
    carry = jnp.zeros((m.shape[0], 1), F32)
    outs = []
    for j in range(n // GROUP_W):
        blk = m[:, j * GROUP_W:(j + 1) * GROUP_W]
        inc = _dot(blk.astype(BF16), tri)
        outs.append(inc - blk + carry)
        carry = carry + inc[:, GROUP_W - 1:GROUP_W]
    return jnp.concatenate(outs, axis=1)


def _route_kernel(lg_ref, tri_ref, pos_ref, gate_ref, post_ref, st_ref, *, n, cap, tile, nb):
    affs = []
    for i in range(nb):
        lg = lg_ref[i]
        e = jnp.exp(lg - jnp.max(lg, axis=0, keepdims=True))
        affs.append(e / jnp.sum(e, axis=0, keepdims=True))
        gate_ref[i] = affs[i]
    aff = affs[0] if nb == 1 else jnp.concatenate(affs, axis=0)

    def unresolved(state):
        lo, hi = state
        return jnp.max(jnp.where(lo < hi, 1.0, 0.0)) > 0.0

    def bisect(state):
        lo, hi = state
        mid = 0.5 * (lo + hi)
        mid = jnp.where(mid > lo, mid, hi)
        ge = aff >= mid
        cnt = jnp.sum(jnp.where(ge, 1.0, 0.0), axis=1, keepdims=True)
        least_ge = jnp.min(jnp.where(ge, aff, jnp.inf), axis=1, keepdims=True)
        most_lt = jnp.max(jnp.where(ge, NEG_INF, aff), axis=1, keepdims=True)
        up = cnt >= cap
        return jnp.where(up, least_ge, lo), jnp.where(up, hi, most_lt)

    thr, _ = lax.while_loop(unresolved, bisect, (jnp.min(aff, axis=1, keepdims=True),
                                                 jnp.max(aff, axis=1, keepdims=True)))
    gt = aff > thr
    eq = aff == thr
    need = cap - jnp.sum(jnp.where(gt, 1.0, 0.0), axis=1, keepdims=True)
    tri = tri_ref[...]
    rank_eq = _cumsum_excl(jnp.where(eq, 1.0, 0.0), tri)
    sel = jnp.where(gt, 1.0, jnp.where(eq, jnp.where(rank_eq < need, 1.0, 0.0), 0.0))
    cum = _cumsum_excl(sel, tri)
    pos = jnp.where(sel > 0.0, cum, -1.0)
    tile_lane = lax.broadcasted_iota(I32, (nb * N_EXPERTS, LANES), 1)
    starts = jnp.zeros((nb * N_EXPERTS, LANES), F32)
    for t in range(n // tile):
        starts = jnp.where(tile_lane == t, cum[:, t * tile:t * tile + 1], starts)
    for i in range(nb):
        rows = slice(i * N_EXPERTS, (i + 1) * N_EXPERTS)
        pos_ref[i] = pos[rows].astype(I32)
        st_ref[i] = starts[rows].astype(I32)
        padded = jnp.concatenate([pos[rows], jnp.full((LANES - N_EXPERTS, n), -1.0, F32)], axis=0)
        post_ref[i] = padded.T.astype(I32)


def _route(logits, tri, cap, tile):
    b, _, n = logits.shape
    nb = next(k for k in (4, 2, 1) if b % k == 0)
    em = pl.BlockSpec((nb, N_EXPERTS, n), lambda i: (i, 0, 0))
    kern = functools.partial(_route_kernel, n=n, cap=cap, tile=tile, nb=nb)
    return pl.pallas_call(
        kern,
        out_shape=(jax.ShapeDtypeStruct((b, N_EXPERTS, n), I32),
                   jax.ShapeDtypeStruct((b, N_EXPERTS, n), F32),
                   jax.ShapeDtypeStruct((b, n, LANES), I32),
                   jax.ShapeDtypeStruct((b, N_EXPERTS, LANES), I32)),
        grid=(b // nb,),
        in_specs=[em, _full((GROUP_W, GROUP_W))],
        out_specs=(em, em, pl.BlockSpec((nb, n, LANES), lambda i: (i, 0, 0)),
                   pl.BlockSpec((nb, N_EXPERTS, LANES), lambda i: (i, 0, 0))),
        compiler_params=_params("arbitrary"),
        name="router",
    )(logits, tri)


def _ffn_kernel(st_ref, xn_ref, pos_ref, gate_ref, wg32_ref, wu32_ref, wd32_ref, o_ref, wg_ref, wu_ref, wd_ref, xg_ref,
                *, bb, cap, n, tile, stride):
    @pl.when(pl.program_id(1) == 0)
    def _():
        wg_ref[...] = wg32_ref[0, 0].astype(BF16)
        wu_ref[...] = wu32_ref[0, 0].astype(BF16)
        wd_ref[...] = wd32_ref[0, 0].astype(BF16)

    expert = pl.ds(pl.program_id(0), 1)

    def step(gather):
        slot = lax.broadcasted_iota(I32, (cap, n), 0)
        hits = [pos_ref[i, expert, :] == slot for i in range(bb)]
        gs = [jnp.sum(jnp.where(hits[i], gate_ref[i, expert, :], 0.0), axis=1, keepdims=True) for i in range(bb)]
        g = gs[0] if bb == 1 else jnp.concatenate(gs, axis=0)
        gather(hits)
        xg = xg_ref[...].astype(BF16)
        hid = (jax.nn.silu(_dot(xg, wg_ref[...])) * _dot(xg, wu_ref[...])).astype(BF16)
        o = _dot(hid, wd_ref[...]) * g
        for i in range(bb):
            o_ref[i, 0] = o[i * cap:(i + 1) * cap].astype(BF16)

    def gather_dense(hits):
        for i in range(bb):
            xg_ref[i * cap:(i + 1) * cap, :] = _dot(jnp.where(hits[i], 1.0, 0.0).astype(BF16), xn_ref[i])

    if cap <= SLOT_WINDOW:
        step(gather_dense)
        return

    nt = n // tile
    win, fits = [], True
    for i in range(bb):
        base = (pl.program_id(1) * bb + i) * nt * stride * N_EXPERTS + pl.program_id(0)
        for t in range(nt):
            start = st_ref[base + t * stride * N_EXPERTS]
            end = st_ref[base + (t + 1) * stride * N_EXPERTS] if t + 1 < nt else cap
            a = jnp.minimum((start >> 4) << 4, cap - SLOT_WINDOW)
            win.append(a)
            fits = jnp.logical_and(fits, end - a <= SLOT_WINDOW)

    def gather_windowed(hits):
        del hits
        xg_ref[...] = jnp.zeros_like(xg_ref)
        wslot = lax.broadcasted_iota(I32, (SLOT_WINDOW, tile), 0)
        for i in range(bb):
            for t in range(nt):
                a = win[i * nt + t]
                p = pos_ref[i, expert, t * tile:(t + 1) * tile]
                onehot = jnp.where(p - a == wslot, 1.0, 0.0).astype(BF16)
                rows = pl.ds(pl.multiple_of(i * cap + a, 16), SLOT_WINDOW)
                xg_ref[rows, :] = xg_ref[rows, :] + _dot(onehot, xn_ref[i, t * tile:(t + 1) * tile, :])

    pl.when(fits)(lambda: step(gather_windowed))
    pl.when(jnp.logical_not(fits))(lambda: step(gather_dense))


def _expert_ffn(xn, pos, gate, starts, wg, wu, wd, layer, cap, bb, tile, stride):
    b, n, d = xn.shape
    assert b % bb == 0 and n % tile == 0, (b, bb, n, tile)
    wspec = pl.BlockSpec((1, 1, d, d), lambda e, j, st: (layer, e, 0, 0))
    sel = pl.BlockSpec((bb, N_EXPERTS, n), lambda e, j, st: (j, 0, 0))
    kern = functools.partial(_ffn_kernel, bb=bb, cap=cap, n=n, tile=tile, stride=stride)
    return pl.pallas_call(
        kern,
        out_shape=jax.ShapeDtypeStruct((b, N_EXPERTS, cap, d), BF16),
        grid_spec=pltpu.PrefetchScalarGridSpec(
            num_scalar_prefetch=1,
            grid=(N_EXPERTS, b // bb),
            in_specs=[pl.BlockSpec((bb, n, d), lambda e, j, st: (j, 0, 0)), sel, sel, wspec, wspec, wspec],
            out_specs=pl.BlockSpec((bb, 1, cap, d), lambda e, j, st: (j, e, 0, 0)),
            scratch_shapes=[pltpu.VMEM((d, d), BF16)] * 3 + [pltpu.VMEM((bb * cap, d), F32)]),
        compiler_params=_params("arbitrary", "arbitrary"),
        name="expert_ffn",
    )(starts, xn, pos, gate, wg, wu, wd)


def _scatter_kernel(st_ref, x_ref, mod_ref, pt_ref, o_ref, out_ref, *, tn, cap, nt, sub, window):
    for s in range(sub):
        _scatter_tile(st_ref, x_ref.at[0, s * tn:(s + 1) * tn], mod_ref, pt_ref.at[0, s * tn:(s + 1) * tn], o_ref,
                      out_ref.at[0, s * tn:(s + 1) * tn], pl.program_id(1) * sub + s,
                      tn=tn, cap=cap, nt=nt, window=window)


def _scatter_tile(st_ref, x_ref, mod_ref, pt_ref, o_ref, out_ref, t, *, tn, cap, nt, window):
    pos_t = pt_ref[...]
    g = mod_ref[0][:, 5 * D_MODEL:6 * D_MODEL]

    def dense():
        if cap % LANES == 0:
            slot = lax.broadcasted_iota(I32, (tn, cap), 1)
            onehot = jnp.concatenate(
                [jnp.where(pos_t[:, e:e + 1] == slot, 1.0, 0.0).astype(BF16) for e in range(N_EXPERTS)], axis=1)
        else:
            slot = lax.broadcasted_iota(I32, (tn, N_EXPERTS * cap), 1)
            acc = jnp.zeros((tn, N_EXPERTS * cap), F32)
            for e in range(N_EXPERTS):
                pe = pos_t[:, e:e + 1]
                acc = jnp.where(jnp.where(pe >= 0, pe + e * cap, -1) == slot, 1.0, acc)
            onehot = acc.astype(BF16)
        y = _dot(onehot, o_ref[0].reshape(N_EXPERTS * cap, D_MODEL))
        out_ref[...] = x_ref[...] + g * y

    if cap <= window:
        dense()
        return

    base = (pl.program_id(0) * nt + t) * N_EXPERTS
    nxt = jnp.minimum(t + 1, nt - 1)
    nbase = (pl.program_id(0) * nt + nxt) * N_EXPERTS
    win, fits = [], True
    for e in range(N_EXPERTS):
        start = st_ref[base + e]
        end = jnp.where(t + 1 < nt, st_ref[nbase + e], cap)
        a = jnp.minimum((start >> 4) << 4, cap - window)
        win.append(a)
        fits = jnp.logical_and(fits, end - a <= window)

    @pl.when(fits)
    def _():
        per = LANES // window
        lane = lax.broadcasted_iota(I32, (tn, LANES), 1)
        groups = []
        for e0 in range(0, N_EXPERTS, per):
            target = jnp.full((tn, LANES), -1, I32)
            for i in range(per):
                rel = pos_t[:, e0 + i:e0 + i + 1] - win[e0 + i]
                here = (lane >> (window.bit_length() - 1)) == i if per > 1 else None
                shifted = jnp.where(rel >= 0, rel + i * window, -1)
                target = shifted if here is None else jnp.where(here, shifted, target)
            groups.append(jnp.where(target == lane, 1.0, 0.0).astype(BF16))
        onehot = jnp.concatenate(groups, axis=1)
        rows = jnp.concatenate(
            [o_ref[0, e, pl.ds(pl.multiple_of(win[e], 16), window), :] for e in range(N_EXPERTS)], axis=0)
        out_ref[...] = x_ref[...] + g * _dot(onehot, rows)

    pl.when(jnp.logical_not(fits))(dense)


def _scatter(x, mod, pos_t, starts, o, cap, tn, window, sub):
    b, n, d = x.shape
    nt = n // tn
    xs = pl.BlockSpec((1, sub * tn, d), lambda i, t, st: (i, t, 0))
    kern = functools.partial(_scatter_kernel, tn=tn, cap=cap, nt=nt, sub=sub, window=window)
    return pl.pallas_call(
        kern,
        out_shape=jax.ShapeDtypeStruct((b, n, d), F32),
        grid_spec=pltpu.PrefetchScalarGridSpec(
            num_scalar_prefetch=1,
            grid=(b, nt // sub),
            in_specs=[xs,
                      pl.BlockSpec((1, 1, 6 * d), lambda i, t, st: (i, 0, 0)),
                      pl.BlockSpec((1, sub * tn, LANES), lambda i, t, st: (i, t, 0)),
                      pl.BlockSpec((1, N_EXPERTS, cap, d), lambda i, t, st: (i, 0, 0, 0))],
            out_specs=xs),
        compiler_params=_params("arbitrary", "arbitrary"),
        name="scatter_add",
    )(starts, x, mod, pos_t, o)


@functools.lru_cache(maxsize=None)
def _np_consts():
    lane = np.arange(GROUP_W)
    seg32 = (lane[:, None] // DIFF_D == lane[None, :] // DIFF_D).astype(np.float32)
    seg64 = (lane[:, None] // HEAD_DIM == lane[None, :] // HEAD_DIM).astype(np.float32)
    ang = 2.0 * np.pi * ((lane[:, None] % HEAD_DIM) * (lane[None, :] % HEAD_DIM) % HEAD_DIM) / HEAD_DIM
    cc = np.cos(ang) * seg64
    ss = np.sin(ang) * seg64
    tri = (lane[:, None] <= lane[None, :]).astype(np.float32)
    return dict(seg32=seg32, seg64=seg64, cc=cc, ss=ss, tri=tri)


@functools.lru_cache(maxsize=None)
def _np_dft(n):
    idx = (np.arange(n, dtype=np.int64)[:, None] * np.arange(n, dtype=np.int64)[None, :]) % n
    ang = 2.0 * np.pi * idx.astype(np.float64) / n
    return np.cos(ang).astype(np.float32), np.sin(ang).astype(np.float32)


@functools.lru_cache(maxsize=None)
def _np_rope(n):
    half = DIFF_D // 2
    inv = 1.0 / (ROPE_BASE ** (np.arange(0, half, 2, dtype=np.float32) / half))
    t = np.arange(n)
    row = (t // GRID_W).astype(np.float32)[:, None] * inv
    col = (t % GRID_W).astype(np.float32)[:, None] * inv
    nf = inv.shape[0]
    d = np.arange(GROUP_W) % DIFF_D
    f = d % nf
    is_col = d >= half
    second = (d % half) >= nf
    ang = np.where(is_col[None, :], col[:, f], row[:, f])
    c = np.cos(ang).astype(np.float32)
    s = np.sin(ang).astype(np.float32)
    s = np.where(second[None, :], s, -s)
    return c, s


@functools.lru_cache(maxsize=None)
def _np_na_index(n_rows):
    n_steps = n_rows // NA_QROWS
    reps = [0, 1, 2, n_steps - 2, n_steps - 1]
    tq = NA_QROWS * GRID_W
    roff = np.zeros((len(reps), NA_QROWS, NA_WIN_ROWS), np.int32)
    valid = np.zeros((len(reps), tq, NA_WIN), bool)
    for ci, t in enumerate(reps):
        ws = int(np.clip(NA_QROWS * t - NA_ROWS // 2, 0, n_rows - NA_WIN_ROWS))
        qi = np.arange(tq)
        r = NA_QROWS * t + qi // GRID_W
        qcol = qi % GRID_W
        kk = np.arange(NA_WIN)
        krow = ws + kk // GRID_W
        kcol = kk % GRID_W
        rstart = np.clip(r - NA_ROWS // 2, 0, n_rows - NA_ROWS)
        wstart = np.clip(qcol - NA_COLS // 2, 0, GRID_W - NA_COLS)
        vr = (krow[None, :] >= rstart[:, None]) & (krow[None, :] < rstart[:, None] + NA_ROWS)
        vc = (kcol[None, :] >= wstart[:, None]) & (kcol[None, :] < wstart[:, None] + NA_COLS)
        valid[ci] = vr & vc
        rows_q = NA_QROWS * t + np.arange(NA_QROWS)
        rows_k = ws + np.arange(NA_WIN_ROWS)
        roff[ci] = np.clip(rows_k[None, :] - rows_q[:, None] + NA_ROWS - 1, 0, 2 * NA_ROWS - 2)
    return roff, valid


def _na_table_kernel(r_ref, mask_ref, o_ref, toep_ref, *, roff):
    n_off = 2 * NA_ROWS - 1
    for h in range(HEADS):
        for ro in range(n_off):
            row = jnp.broadcast_to(r_ref[0, h, ro:ro + 1, :], (GRID_W, LANES))
            toep_ref[h, ro] = pltpu.roll(row, 0, 1, stride=1, stride_axis=0)
    lane = lax.broadcasted_iota(I32, (GRID_W, LANES), 1)
    n_cls = roff.shape[0]
    for ci in range(n_cls):
        for h in range(HEADS):
            for rr in range(NA_QROWS):
                r0 = h * NA_QROWS * GRID_W + rr * GRID_W
                for kp in range(NA_WIN_ROWS // 2):
                    left = toep_ref[h, int(roff[ci, rr, 2 * kp])]
                    right = pltpu.roll(toep_ref[h, int(roff[ci, rr, 2 * kp + 1])], GRID_W, 1)
                    bias = jnp.where(lane < GRID_W, left, right) * LOG2E
                    o_ref[0, ci, r0:r0 + GRID_W, kp * LANES:(kp + 1) * LANES] = (
                        bias + mask_ref[ci, rr * GRID_W:(rr + 1) * GRID_W, kp * LANES:(kp + 1) * LANES])


def _na_tables(rpb_all, n_rows):
    roff, valid = _np_na_index(n_rows)
    n_cls = roff.shape[0]
    depth = rpb_all.shape[0]
    n_off = 2 * NA_ROWS - 1
    r = jnp.concatenate([rpb_all[..., NA_COLS - 1:], jnp.zeros(rpb_all.shape[:-1] + (LANES - 2 * NA_COLS + 1,), F32),
                         rpb_all[..., :NA_COLS - 1]], axis=-1).astype(F32)
    mask = jnp.asarray(np.where(valid, 0.0, NEG_INF).astype(np.float32))
    tq = NA_QROWS * GRID_W
    kern = functools.partial(_na_table_kernel, roff=roff)
    return pl.pallas_call(
        kern,
        out_shape=jax.ShapeDtypeStruct((depth, n_cls, HEADS * tq, NA_WIN), F32),
        grid=(depth,),
        in_specs=[pl.BlockSpec((1, HEADS, n_off, LANES), lambda l: (l, 0, 0, 0)),
                  _full((n_cls, tq, NA_WIN))],
        out_specs=pl.BlockSpec((1, n_cls, HEADS * tq, NA_WIN), lambda l: (l, 0, 0, 0)),
        scratch_shapes=[pltpu.VMEM((HEADS, n_off, GRID_W, LANES), F32)],
        compiler_params=_params("arbitrary"),
        name="na_bias_table",
    )(r, mask)


class _Tiles(NamedTuple):
    proj: int
    out: int
    dft: int
    attn: int
    na_pairs: int
    moe: int
    ffn_samples: int


def _latent_tiles(n):
    return _Tiles(proj=min(n, 1024), out=min(n, 2048), dft=min(n, 2048), attn=min(n, 512), na_pairs=min(8, n // (NA_QROWS * GRID_W)),
                  moe=min(n, 512), ffn_samples=2)
def _moe(x, xn, logits, mod, tri, wg, wu, wd, layer, bb, tn):
    n = x.shape[1]
    cap = CAPACITY_FACTOR * n // N_EXPERTS
    stride = 2 if cap > SLOT_WINDOW else 1
    ts = tn // stride
    pos, gate, pos_t, starts = _route(logits, tri, cap, ts)
    starts = starts[:, :, :n // ts].transpose(0, 2, 1).reshape(-1)
    o = _expert_ffn(xn, pos, gate, starts, wg, wu, wd, layer, cap, bb, tn, stride)
    return _scatter(x, mod, pos_t, starts, o, cap, ts, SLOT_WINDOW // stride, min(8, n // ts))


def kernel(x, c, ctx, c_ctx, ada_w, ada_b, norm1_g, norm2_g, w_in, w_out, head_out_g, sgu_w, sgu_b, diff_qn_g, diff_kn_g, diff_lambda, na_qn_g, na_kn_g, na_rpb, router_w, exp_w_gate, exp_w_up, exp_w_down):
    b, n, d = x.shape
    n_ctx = ctx.shape[1]
    npc = _np_consts()
    consts = {k: jnp.asarray(v, F32).astype(BF16) for k, v in npc.items()}
    seg64, tri = consts["seg64"], consts["tri"]
    cn, sn = (jnp.asarray(a, F32).astype(BF16) for a in _np_dft(n))
    cn_c, sn_c = (jnp.asarray(a, F32).astype(BF16) for a in _np_dft(n_ctx))
    rope = tuple(jnp.asarray(a, F32) for a in _np_rope(n))

    pad = (-(b + 1)) % 8
    c_rows = jnp.concatenate([c, c_ctx[None, :], jnp.zeros((pad, d), F32)], axis=0)
    mod_all = _modulation(c_rows, ada_w, ada_b)
    na_tables = _na_tables(na_rpb, n // GRID_W)

    xc = ctx
    for l in range(DEPTH):
        last = l == DEPTH - 1
        lam_init = 0.8 - 0.6 * math.exp(-0.3 * l)
        mod = mod_all[l, :b][:, None, :]
        mod_c = jnp.broadcast_to(mod_all[l, b][None, None, :], (b, 1, 6 * d))
        g1 = norm1_g[l][None, :]
        g2 = norm2_g[l][None, :]
        w_in_l = w_in[l].astype(BF16)
        w_out_l = w_out[l].astype(BF16)
        sguw = sgu_w[l].astype(BF16).transpose(1, 0, 2).reshape(CHUNK, HEADS * CHUNK)
        sgub = jnp.repeat(sgu_b[l].T, HEAD_DIM, axis=1)
        hg = head_out_g[l].reshape(4, GROUP_W)
        vec = jnp.stack([jnp.tile(diff_qn_g[l], GROUP_W // DIFF_D), jnp.tile(diff_kn_g[l], GROUP_W // DIFF_D),
                         jnp.tile(na_qn_g[l], HEADS), jnp.tile(na_kn_g[l], HEADS),
                         hg[0], hg[1], hg[2], hg[3],
                         jnp.broadcast_to(jnp.max(jnp.abs(na_rpb[l])) * LOG2E, (GROUP_W,))], axis=0).astype(F32)
        vec = jnp.pad(vec, ((0, VEC_ROWS - vec.shape[0]), (0, 0)))
        lam = diff_lambda[l].astype(F32)
        rwt = router_w[l].T.astype(BF16)
        experts = (exp_w_gate, exp_w_up, exp_w_down)

        tl = _latent_tiles(n)
        ya, zc, zs, qc, kct, vc, qd, kdt, vd = _in_proj(x, mod, g1, w_in_l, consts, sguw, sgub, vec, rope, tl.proj)
        if last:
            kct_c, vc_c, kdt_c, vd_c = _in_proj(
                xc, mod_c, g1, w_in_l, consts, sguw, sgub, vec, None, n_ctx, kv_only=True)
        else:
            ya_c, zc_c, zs_c, qc_c, kct_c, vc_c, qd_c, kdt_c, vd_c = _in_proj(
                xc, mod_c, g1, w_in_l, consts, sguw, sgub, vec, None, n_ctx)

        yb = _fourier(zc, zs, cn, sn, seg64, vec, tl.dft)
        yc = _attention(qc, [(kct, vc), (kct_c, vc_c)], lam, lam_init, seg64, vec, 6, 1.0 - lam_init,
                        tl.attn, tl.attn)
        yd = _na_attention(qd, kdt, vd, kdt_c, vd_c, na_tables, l, seg64, vec, tl.na_pairs)
        x, xn, logits = _out_proj(ya, yb, yc, yd, w_out_l, x, mod, g2, rwt, tl.out)
        x = _moe(x, xn, logits, mod, tri, *experts, l, tl.ffn_samples, tl.moe)

        if not last:
            yb_c = _fourier(zc_c, zs_c, cn_c, sn_c, seg64, vec, n_ctx)
            yc_c = _attention(qc_c, [(kct_c, vc_c)], lam, lam_init, seg64, vec, 6, 1.0 - lam_init, n_ctx)
            yd_c = _attention(qd_c, [(kdt_c, vd_c)], None, 0.0, seg64, vec, 7, 1.0, n_ctx)
            xc, xn_c, logits_c = _out_proj(ya_c, yb_c, yc_c, yd_c, w_out_l, xc, mod_c, g2, rwt, n_ctx)
            xc = _moe(xc, xn_c, logits_c, mod_c, tri, *experts, l, b, n_ctx)
    return x
```

```python
import functools
import math
from typing import NamedTuple

import numpy as np
import jax
import jax.numpy as jnp
from jax import lax
from jax.experimental import pallas as pl
from jax.experimental.pallas import tpu as pltpu

F32 = jnp.float32
BF16 = jnp.bfloat16
I32 = jnp.int32

D_MODEL = 1024
DEPTH = 2
GRID_W = 64
LANES = 128
HEAD_DIM = 64
LOG2_HEAD_DIM = 6
GROUP_W = 256
HEADS = GROUP_W // HEAD_DIM
CHUNK = 128
DIFF_D = HEAD_DIM // 2
NA_ROWS = 8
NA_COLS = 16
N_EXPERTS = 16
CAPACITY_FACTOR = 2
ROPE_BASE = 10000.0
EPS = 1e-6
IN_W = 9 * GROUP_W
LOG2E = 1.4426950408889634

VMEM_LIMIT_BYTES = 56 * 1024 * 1024
NA_QROWS = 2
NA_WIN_ROWS = NA_ROWS + 2
NA_WIN = NA_WIN_ROWS * GRID_W
NA_HEAD_STACK = 1
NEG_INF = float("-inf")
VEC_ROWS = 16
SLOT_WINDOW = 128
SAFE_EXP2_BOUND = 48.0
BOUND_SLACK = 1.02


def _dot(a, b):
    return jnp.dot(a, b, preferred_element_type=F32)


def _params(*sem):
    return pltpu.CompilerParams(dimension_semantics=sem, vmem_limit_bytes=VMEM_LIMIT_BYTES)


def _full(shape):
    nd = len(shape)
    return pl.BlockSpec(shape, lambda *_: (0,) * nd)


def _seg_rms(x, seg, width):
    ss = _dot((x * x).astype(BF16), seg)
    return x * lax.rsqrt(ss * (1.0 / width) + EPS)


def _mod_kernel(c_ref, w_ref, b_ref, o_ref):
    s = jax.nn.silu(c_ref[...]).astype(BF16)
    o_ref[0] = _dot(s, w_ref[0].astype(BF16)) + b_ref[0]


def _modulation(c_rows, ada_w, ada_b):
    depth, d, w6 = ada_w.shape
    r = c_rows.shape[0]
    tn = 1024
    return pl.pallas_call(
        _mod_kernel,
        out_shape=jax.ShapeDtypeStruct((depth, r, w6), F32),
        grid=(depth, w6 // tn),
        in_specs=[
            pl.BlockSpec((r, d), lambda l, j: (0, 0)),
            pl.BlockSpec((1, d, tn), lambda l, j: (l, 0, j)),
            pl.BlockSpec((1, 1, tn), lambda l, j: (l, 0, j)),
        ],
        out_specs=pl.BlockSpec((1, r, tn), lambda l, j: (l, 0, j)),
        compiler_params=_params("arbitrary", "arbitrary"),
        name="modulation",
    )(c_rows, ada_w, ada_b.reshape(depth, 1, w6))


def _rope(x, c, s, lane):
    fwd = pltpu.roll(x, GROUP_W - 8, 1)
    bwd = pltpu.roll(x, 8, 1)
    partner = jnp.where((lane & 8) == 0, fwd, bwd)
    return x * c + partner * s


def _in_kernel(*refs, tm, use_rope, kv_only, c_scale, d_scale):
    (x_ref, mod_ref, g1_ref, w_ref, seg32_ref, seg64_ref, cc_ref, ss_ref, sguw_ref, sgub_ref, vec_ref) = refs[:11]
    rest = refs[11:]
    if use_rope:
        ropec_ref, ropes_ref = rest[:2]
        rest = rest[2:]
    if kv_only:
        kct_ref, vc_ref, kdt_ref, vd_ref = rest
    else:
        ya_ref, zc_ref, zs_ref, qc_ref, kct_ref, vc_ref, qd_ref, kdt_ref, vd_ref = rest

    x = x_ref[0]
    mod = mod_ref[0]
    sh = mod[:, 0:D_MODEL]
    sc = mod[:, D_MODEL:2 * D_MODEL]
    ms = jnp.mean(x * x, axis=-1, keepdims=True)
    h = x * lax.rsqrt(ms + EPS) * g1_ref[...]
    h = (h * (1.0 + sc) + sh).astype(BF16)
    lane = lax.broadcasted_iota(I32, (1, GROUP_W), 1)
    head = lane >> LOG2_HEAD_DIM
    seg32 = seg32_ref[...]
    seg64 = seg64_ref[...]
    vec = vec_ref[...]

    if kv_only:
        pc = _dot(h, w_ref[:, 4 * GROUP_W:6 * GROUP_W])
        k = _seg_rms(pc[:, 0:GROUP_W], seg32, DIFF_D) * vec[1:2]
        kct_ref[0] = k.T.astype(BF16)
        vc_ref[0] = pc[:, GROUP_W:2 * GROUP_W].astype(BF16)
        pd = _dot(h, w_ref[:, 7 * GROUP_W:9 * GROUP_W])
        kd = _seg_rms(pd[:, 0:GROUP_W], seg64, HEAD_DIM) * vec[3:4]
        kdt_ref[0] = kd.T.astype(BF16)
        vd_ref[0] = pd[:, GROUP_W:2 * GROUP_W].astype(BF16)
        return

    z = jax.nn.gelu(_dot(h, w_ref[:, 0:2 * GROUP_W]))
    u = z[:, 0:GROUP_W]
    vn = _seg_rms(z[:, GROUP_W:2 * GROUP_W], seg64, HEAD_DIM).astype(BF16)
    rows = []
    for c in range(tm // CHUNK):
        vch = vn[c * CHUNK:(c + 1) * CHUNK]
        stacked = jnp.concatenate([jnp.where(head == hh, vch, jnp.zeros_like(vch)) for hh in range(HEADS)], axis=0)
        rows.append(_dot(sguw_ref[...], stacked) + sgub_ref[...])
    ya = u * jnp.concatenate(rows, axis=0)
    ya_ref[0] = (_seg_rms(ya, seg64, HEAD_DIM) * vec[4:5]).astype(BF16)

    zb = _dot(h, w_ref[:, 2 * GROUP_W:3 * GROUP_W]).astype(BF16)
    zc_ref[0] = _dot(zb, cc_ref[...]).astype(BF16)
    zs_ref[0] = _dot(zb, ss_ref[...]).astype(BF16)

    pc = _dot(h, w_ref[:, 3 * GROUP_W:6 * GROUP_W])
    q = _seg_rms(pc[:, 0:GROUP_W], seg32, DIFF_D) * vec[0:1]
    k = _seg_rms(pc[:, GROUP_W:2 * GROUP_W], seg32, DIFF_D) * vec[1:2]
    if use_rope:
        rc = ropec_ref[...]
        rs = ropes_ref[...]
        q = _rope(q, rc, rs, lane)
        k = _rope(k, rc, rs, lane)
    qc_ref[0] = (q * c_scale).astype(BF16)
    kct_ref[0] = k.T.astype(BF16)
    vc_ref[0] = pc[:, 2 * GROUP_W:3 * GROUP_W].astype(BF16)

    pd = _dot(h, w_ref[:, 6 * GROUP_W:9 * GROUP_W])
    qd = _seg_rms(pd[:, 0:GROUP_W], seg64, HEAD_DIM) * vec[2:3]
    kd = _seg_rms(pd[:, GROUP_W:2 * GROUP_W], seg64, HEAD_DIM) * vec[3:4]
    qd_ref[0] = (qd * d_scale).astype(BF16)
    kdt_ref[0] = kd.T.astype(BF16)
    vd_ref[0] = pd[:, 2 * GROUP_W:3 * GROUP_W].astype(BF16)


def _in_proj(x, mod, g1, w_in, consts, sguw, sgub, vec, rope, tm, kv_only=False):
    b, n, d = x.shape
    use_rope = rope is not None
    tok = pl.BlockSpec((1, tm, GROUP_W), lambda i, t: (i, t, 0))
    tok_t = pl.BlockSpec((1, GROUP_W, tm), lambda i, t: (i, 0, t))
    in_specs = [
        pl.BlockSpec((1, tm, d), lambda i, t: (i, t, 0)),
        pl.BlockSpec((1, 1, 6 * d), lambda i, t: (i, 0, 0)),
        _full((1, d)),
        _full((d, IN_W)),
        _full((GROUP_W, GROUP_W)), _full((GROUP_W, GROUP_W)), _full((GROUP_W, GROUP_W)), _full((GROUP_W, GROUP_W)),
        _full((CHUNK, HEADS * CHUNK)),
        _full((CHUNK, GROUP_W)),
        _full((VEC_ROWS, GROUP_W)),
    ]
    args = [x, mod, g1, w_in, consts["seg32"], consts["seg64"], consts["cc"], consts["ss"], sguw, sgub, vec]
    if use_rope:
        in_specs += [pl.BlockSpec((tm, GROUP_W), lambda i, t: (t, 0))] * 2
        args += list(rope)
    sd = jax.ShapeDtypeStruct((b, n, GROUP_W), BF16)
    sdt = jax.ShapeDtypeStruct((b, GROUP_W, n), BF16)
    kern = functools.partial(
        _in_kernel, tm=tm, use_rope=use_rope, kv_only=kv_only,
        c_scale=(DIFF_D ** -0.5) * LOG2E, d_scale=(HEAD_DIM ** -0.5) * LOG2E)
    return pl.pallas_call(
        kern,
        out_shape=(sdt, sd, sdt, sd) if kv_only else (sd, sd, sd, sd, sdt, sd, sd, sdt, sd),
        grid=(b, n // tm),
        in_specs=in_specs,
        out_specs=(tok_t, tok, tok_t, tok) if kv_only else (tok, tok, tok, tok, tok_t, tok, tok, tok_t, tok),
        compiler_params=_params("arbitrary", "arbitrary"),
        name="in_proj",
    )(*args)


def _fourier_kernel(cn_ref, sn_ref, zc_ref, zs_ref, seg64_ref, vec_ref, o_ref, *, norm):
    y = (_dot(cn_ref[...], zc_ref[0]) - _dot(sn_ref[...], zs_ref[0])) * norm
    o_ref[0] = (_seg_rms(y, seg64_ref[...], HEAD_DIM) * vec_ref[5:6]).astype(BF16)


def _fourier(zc, zs, cn, sn, seg64, vec, tn):
    b, n, _ = zc.shape
    kern = functools.partial(_fourier_kernel, norm=1.0 / math.sqrt(n * HEAD_DIM))
    return pl.pallas_call(
        kern,
        out_shape=jax.ShapeDtypeStruct((b, n, GROUP_W), BF16),
        grid=(n // tn, b),
        in_specs=[
            pl.BlockSpec((tn, n), lambda t, i: (t, 0)),
            pl.BlockSpec((tn, n), lambda t, i: (t, 0)),
            pl.BlockSpec((1, n, GROUP_W), lambda t, i: (i, 0, 0)),
            pl.BlockSpec((1, n, GROUP_W), lambda t, i: (i, 0, 0)),
            _full((GROUP_W, GROUP_W)),
            _full((VEC_ROWS, GROUP_W)),
        ],
        out_specs=pl.BlockSpec((1, tn, GROUP_W), lambda t, i: (i, t, 0)),
        compiler_params=_params("arbitrary", "arbitrary"),
        name="fourier",
    )(cn, sn, zc, zs, seg64, vec)


def _attn_kernel(*refs, n_src, diff, lam_init, chunk, tq, vec_row, out_scale):
    q_ref = refs[0]
    rest = refs[1 + 2 * n_src:]
    if diff:
        lam_ref = rest[0]
        rest = rest[1:]
    seg64_ref, vec_ref, o_ref = rest

    q = q_ref[0]
    lane = lax.broadcasted_iota(I32, (1, GROUP_W), 1)
    if diff:
        lf = lam_ref[...]
        lam = (jnp.exp(jnp.sum(lf[0:1] * lf[1:2], axis=-1, keepdims=True))
               - jnp.exp(jnp.sum(lf[2:3] * lf[3:4], axis=-1, keepdims=True)) + lam_init)

    chunks = []
    for i in range(n_src):
        kt_ref, v_ref = refs[1 + 2 * i], refs[2 + 2 * i]
        nk = kt_ref.shape[2]
        chunks += [(kt_ref, v_ref, c0, min(chunk, nk - c0)) for c0 in range(0, nk, chunk)]

    width = DIFF_D if diff else HEAD_DIM
    gq = vec_ref[0:1] if diff else vec_ref[2:3]
    gk = vec_ref[1:2] if diff else vec_ref[3:4]
    bound = jnp.max(jnp.abs(gq)) * jnp.max(jnp.abs(gk)) * (math.sqrt(width) * LOG2E * BOUND_SLACK)

    def attend_bounded(sel):
        qm = jnp.where(sel, q, jnp.zeros_like(q))
        l = jnp.zeros((tq, 1), F32)
        acc = jnp.zeros((tq, GROUP_W), F32)
        for kt_ref, v_ref, c0, ck in chunks:
            p = jnp.exp2(_dot(qm, kt_ref[0, :, c0:c0 + ck]))
            l = l + jnp.sum(p, axis=-1, keepdims=True)
            acc = acc + _dot(p.astype(BF16), v_ref[0, c0:c0 + ck, :])
        return acc * (1.0 / l)

    def attend_online(sel):
        qm = jnp.where(sel, q, jnp.zeros_like(q))
        m = jnp.full((tq, 1), NEG_INF, F32)
        l = jnp.zeros((tq, 1), F32)
        acc = jnp.zeros((tq, GROUP_W), F32)
        for kt_ref, v_ref, c0, ck in chunks:
            s = _dot(qm, kt_ref[0, :, c0:c0 + ck])
            m_new = jnp.maximum(m, jnp.max(s, axis=-1, keepdims=True))
            alpha = jnp.exp2(m - m_new)
            p = jnp.exp2(s - m_new)
            l = alpha * l + jnp.sum(p, axis=-1, keepdims=True)
            acc = alpha * acc + _dot(p.astype(BF16), v_ref[0, c0:c0 + ck, :])
            m = m_new
        return acc * (1.0 / l)

    def head_out(attend, h):
        if diff:
            return (attend((lane >> (LOG2_HEAD_DIM - 1)) == 2 * h)
                    - lam * attend((lane >> (LOG2_HEAD_DIM - 1)) == 2 * h + 1))
        return attend((lane >> LOG2_HEAD_DIM) == h)

    def finish(out):
        y = _seg_rms(out, seg64_ref[...], HEAD_DIM) * vec_ref[vec_row:vec_row + 1]
        o_ref[0] = (y * out_scale).astype(BF16)

    def run_bounded():
        out = jnp.zeros((tq, GROUP_W), F32)
        for h in range(HEADS):
            out = jnp.where((lane >> LOG2_HEAD_DIM) == h, head_out(attend_bounded, h), out)
        finish(out)

    def run_online():
        def body(h, out):
            return jnp.where((lane >> LOG2_HEAD_DIM) == h, head_out(attend_online, h), out)

        finish(lax.fori_loop(0, HEADS, body, jnp.zeros((tq, GROUP_W), F32)))

    small = bound <= SAFE_EXP2_BOUND
    pl.when(small)(run_bounded)
    pl.when(jnp.logical_not(small))(run_online)


def _attention(q, srcs, lam, lam_init, seg64, vec, vec_row, out_scale, tq, chunk=768):
    b, nq, _ = q.shape
    diff = lam is not None
    in_specs = [pl.BlockSpec((1, tq, GROUP_W), lambda i, t: (i, t, 0))]
    args = [q]
    for kt, v in srcs:
        nk = v.shape[1]
        in_specs += [pl.BlockSpec((1, GROUP_W, nk), lambda i, t: (i, 0, 0)),
                     pl.BlockSpec((1, nk, GROUP_W), lambda i, t: (i, 0, 0))]
        args += [kt, v]
    if diff:
        in_specs.append(_full((4, DIFF_D)))
        args.append(lam)
    in_specs += [_full((GROUP_W, GROUP_W)), _full((VEC_ROWS, GROUP_W))]
    args += [seg64, vec]
    kern = functools.partial(_attn_kernel, n_src=len(srcs), diff=diff, lam_init=lam_init, chunk=chunk, tq=tq,
                             vec_row=vec_row, out_scale=out_scale)
    return pl.pallas_call(
        kern,
        out_shape=jax.ShapeDtypeStruct((b, nq, GROUP_W), BF16),
        grid=(b, nq // tq),
        in_specs=in_specs,
        out_specs=pl.BlockSpec((1, tq, GROUP_W), lambda i, t: (i, t, 0)),
        compiler_params=_params("arbitrary", "arbitrary"),
        name="diff_attention" if diff else "ctx_attention",
    )(*args)


def _na_kernel(q_ref, kt_ref, v_ref, ktc_ref, vc_ref, tab_ref, seg64_ref, vec_ref, o_ref, *, n_rows, pairs):
    n_steps = n_rows // NA_QROWS
    tq = NA_QROWS * GRID_W
    lane = lax.broadcasted_iota(I32, (1, GROUP_W), 1)
    head = lane >> LOG2_HEAD_DIM
    ktc = ktc_ref[0]
    vc = vc_ref[0]
    bound = (jnp.max(jnp.abs(vec_ref[2:3])) * jnp.max(jnp.abs(vec_ref[3:4])) * (math.sqrt(HEAD_DIM) * LOG2E * BOUND_SLACK)
             + jnp.max(vec_ref[8:9]))

    def run(bounded):
        for pi in range(pairs):
            t = pl.program_id(1) * pairs + pi
            ws = jnp.clip(NA_QROWS * t - NA_ROWS // 2, 0, n_rows - NA_WIN_ROWS)
            k0 = pl.multiple_of(ws * GRID_W, 128)
            tid = jnp.where(t < 2, t, jnp.where(t < n_steps - 2, 2, t - (n_steps - 5)))
            q = q_ref[0, pi * tq:(pi + 1) * tq, :]
            out = jnp.zeros((tq, GROUP_W), F32)
            for h0 in range(0, HEADS, NA_HEAD_STACK):
                hs = range(h0, h0 + NA_HEAD_STACK)
                qs = jnp.concatenate([jnp.where(head == hh, q, jnp.zeros_like(q)) for hh in hs], axis=0)
                s_loc = (_dot(qs, kt_ref[0, :, pl.ds(k0, NA_WIN)])
                         + tab_ref[0, tid, h0 * tq:(h0 + NA_HEAD_STACK) * tq, :])
                s_ctx = _dot(qs, ktc)
                if bounded:
                    p_loc = jnp.exp2(s_loc)
                    p_ctx = jnp.exp2(s_ctx)
                else:
                    m = jnp.maximum(jnp.max(s_loc, axis=-1, keepdims=True), jnp.max(s_ctx, axis=-1, keepdims=True))
                    p_loc = jnp.exp2(s_loc - m)
                    p_ctx = jnp.exp2(s_ctx - m)
                l = jnp.sum(p_loc, axis=-1, keepdims=True) + jnp.sum(p_ctx, axis=-1, keepdims=True)
                o = _dot(p_loc.astype(BF16), v_ref[0, pl.ds(k0, NA_WIN), :]) + _dot(p_ctx.astype(BF16), vc)
                o = o * (1.0 / l)
                for i, hh in enumerate(hs):
                    out = jnp.where(head == hh, o[i * tq:(i + 1) * tq], out)
            o_ref[0, pi * tq:(pi + 1) * tq, :] = (
                _seg_rms(out, seg64_ref[...], HEAD_DIM) * vec_ref[7:8]).astype(BF16)

    small = bound <= SAFE_EXP2_BOUND
    pl.when(small)(lambda: run(True))
    pl.when(jnp.logical_not(small))(lambda: run(False))


def _na_attention(q, kt, v, ktc, vc, tables, layer, seg64, vec, pairs):
    b, n, _ = q.shape
    nc = vc.shape[1]
    tq = pairs * NA_QROWS * GRID_W
    kern = functools.partial(_na_kernel, n_rows=n // GRID_W, pairs=pairs)
    return pl.pallas_call(
        kern,
        out_shape=jax.ShapeDtypeStruct((b, n, GROUP_W), BF16),
        grid=(b, n // tq),
        in_specs=[
            pl.BlockSpec((1, tq, GROUP_W), lambda i, t: (i, t, 0)),
            pl.BlockSpec((1, GROUP_W, n), lambda i, t: (i, 0, 0)),
            pl.BlockSpec((1, n, GROUP_W), lambda i, t: (i, 0, 0)),
            pl.BlockSpec((1, GROUP_W, nc), lambda i, t: (i, 0, 0)),
            pl.BlockSpec((1, nc, GROUP_W), lambda i, t: (i, 0, 0)),
            pl.BlockSpec((1,) + tables.shape[1:], lambda i, t: (layer, 0, 0, 0)),
            _full((GROUP_W, GROUP_W)),
            _full((VEC_ROWS, GROUP_W)),
        ],
        out_specs=pl.BlockSpec((1, tq, GROUP_W), lambda i, t: (i, t, 0)),
        compiler_params=_params("arbitrary", "arbitrary"),
        name="neighbourhood_attention",
    )(q, kt, v, ktc, vc, tables, seg64, vec)


def _out_kernel(ya_ref, yb_ref, yc_ref, yd_ref, w_ref, x_ref, mod_ref, g2_ref, rwt_ref, o_ref, xn_ref, lg_ref):
    y = jnp.concatenate([ya_ref[0], yb_ref[0], yc_ref[0], yd_ref[0]], axis=-1)
    mod = mod_ref[0]
    x = x_ref[0] + mod[:, 2 * D_MODEL:3 * D_MODEL] * _dot(y, w_ref[...])
    o_ref[0] = x
    sh = mod[:, 3 * D_MODEL:4 * D_MODEL]
    sc = mod[:, 4 * D_MODEL:5 * D_MODEL]
    ms = jnp.mean(x * x, axis=-1, keepdims=True)
    h = x * lax.rsqrt(ms + EPS) * g2_ref[...]
    h = (h * (1.0 + sc) + sh).astype(BF16)
    xn_ref[0] = h
    lg_ref[0] = lax.dot_general(rwt_ref[...], h, (((1,), (1,)), ((), ())), preferred_element_type=F32)


def _out_proj(ya, yb, yc, yd, w_out, x, mod, g2, rwt, tm):
    b, n, d = x.shape
    tok = pl.BlockSpec((1, tm, GROUP_W), lambda i, t: (i, t, 0))
    xs = pl.BlockSpec((1, tm, d), lambda i, t: (i, t, 0))
    return pl.pallas_call(
        _out_kernel,
        out_shape=(jax.ShapeDtypeStruct((b, n, d), F32),
                   jax.ShapeDtypeStruct((b, n, d), BF16),
                   jax.ShapeDtypeStruct((b, N_EXPERTS, n), F32)),
        grid=(b, n // tm),
        in_specs=[tok, tok, tok, tok, _full((4 * GROUP_W, d)), xs,
                  pl.BlockSpec((1, 1, 6 * d), lambda i, t: (i, 0, 0)),
                  _full((1, d)), _full((N_EXPERTS, d))],
        out_specs=(xs, xs, pl.BlockSpec((1, N_EXPERTS, tm), lambda i, t: (i, 0, t))),
        compiler_params=_params("arbitrary", "arbitrary"),
        name="out_proj",
    )(ya, yb, yc, yd, w_out, x, mod, g2, rwt)


def _cumsum_excl(m, tri):
    n = m.shape[1]
    carry = jnp.zeros((m.shape[0], 1), F32)
    outs = []
    for j in range(n // GROUP_W):
        blk = m[:, j * GROUP_W:(j + 1) * GROUP_W]
        inc = _dot(blk.astype(BF16), tri)
        outs.append(inc - blk + carry)
        carry = carry + inc[:, GROUP_W - 1:GROUP_W]
    return jnp.concatenate(outs, axis=1)


def _route_kernel(lg_ref, tri_ref, pos_ref, gate_ref, post_ref, st_ref, *, n, cap, tile, nb):
    affs = []
    for i in range(nb):
        lg = lg_ref[i]
        e = jnp.exp(lg - jnp.max(lg, axis=0, keepdims=True))
        affs.append(e / jnp.sum(e, axis=0, keepdims=True))
        gate_ref[i] = affs[i]
    aff = affs[0] if nb == 1 else jnp.concatenate(affs, axis=0)

    def unresolved(state):
        lo, hi = state
        return jnp.max(jnp.where(lo < hi, 1.0, 0.0)) > 0.0

    def bisect(state):
        lo, hi = state
        mid = 0.5 * (lo + hi)
        mid = jnp.where(mid > lo, mid, hi)
        ge = aff >= mid
        cnt = jnp.sum(jnp.where(ge, 1.0, 0.0), axis=1, keepdims=True)
        least_ge = jnp.min(jnp.where(ge, aff, jnp.inf), axis=1, keepdims=True)
        most_lt = jnp.max(jnp.where(ge, NEG_INF, aff), axis=1, keepdims=True)
        up = cnt >= cap
        return jnp.where(up, least_ge, lo), jnp.where(up, hi, most_lt)

    thr, _ = lax.while_loop(unresolved, bisect, (jnp.min(aff, axis=1, keepdims=True),
                                                 jnp.max(aff, axis=1, keepdims=True)))
    gt = aff > thr
    eq = aff == thr
    need = cap - jnp.sum(jnp.where(gt, 1.0, 0.0), axis=1, keepdims=True)
    tri = tri_ref[...]
    rank_eq = _cumsum_excl(jnp.where(eq, 1.0, 0.0), tri)
    sel = jnp.where(gt, 1.0, jnp.where(eq, jnp.where(rank_eq < need, 1.0, 0.0), 0.0))
    cum = _cumsum_excl(sel, tri)
    pos = jnp.where(sel > 0.0, cum, -1.0)
    tile_lane = lax.broadcasted_iota(I32, (nb * N_EXPERTS, LANES), 1)
    starts = jnp.zeros((nb * N_EXPERTS, LANES), F32)
    for t in range(n // tile):
        starts = jnp.where(tile_lane == t, cum[:, t * tile:t * tile + 1], starts)
    for i in range(nb):
        rows = slice(i * N_EXPERTS, (i + 1) * N_EXPERTS)
        pos_ref[i] = pos[rows].astype(I32)
        st_ref[i] = starts[rows].astype(I32)
        padded = jnp.concatenate([pos[rows], jnp.full((LANES - N_EXPERTS, n), -1.0, F32)], axis=0)
        post_ref[i] = padded.T.astype(I32)


def _route(logits, tri, cap, tile):
    b, _, n = logits.shape
    nb = next(k for k in (4, 2, 1) if b % k == 0)
    em = pl.BlockSpec((nb, N_EXPERTS, n), lambda i: (i, 0, 0))
    kern = functools.partial(_route_kernel, n=n, cap=cap, tile=tile, nb=nb)
    return pl.pallas_call(
        kern,
        out_shape=(jax.ShapeDtypeStruct((b, N_EXPERTS, n), I32),
                   jax.ShapeDtypeStruct((b, N_EXPERTS, n), F32),
                   jax.ShapeDtypeStruct((b, n, LANES), I32),
                   jax.ShapeDtypeStruct((b, N_EXPERTS, LANES), I32)),
        grid=(b // nb,),
        in_specs=[em, _full((GROUP_W, GROUP_W))],
        out_specs=(em, em, pl.BlockSpec((nb, n, LANES), lambda i: (i, 0, 0)),
                   pl.BlockSpec((nb, N_EXPERTS, LANES), lambda i: (i, 0, 0))),
        compiler_params=_params("arbitrary"),
        name="router",
    )(logits, tri)


def _ffn_kernel(st_ref, xn_ref, pos_ref, gate_ref, wg32_ref, wu32_ref, wd32_ref, o_ref, wg_ref, wu_ref, wd_ref, xg_ref,
                *, bb, cap, n, tile, stride):
    @pl.when(pl.program_id(1) == 0)
    def _():
        wg_ref[...] = wg32_ref[0, 0].astype(BF16)
        wu_ref[...] = wu32_ref[0, 0].astype(BF16)
        wd_ref[...] = wd32_ref[0, 0].astype(BF16)

    expert = pl.ds(pl.program_id(0), 1)

    def step(gather):
        slot = lax.broadcasted_iota(I32, (cap, n), 0)
        hits = [pos_ref[i, expert, :] == slot for i in range(bb)]
        gs = [jnp.sum(jnp.where(hits[i], gate_ref[i, expert, :], 0.0), axis=1, keepdims=True) for i in range(bb)]
        g = gs[0] if bb == 1 else jnp.concatenate(gs, axis=0)
        gather(hits)
        xg = xg_ref[...].astype(BF16)
        half = wg_ref.shape[1] // 2
        acc = None
        for f0 in (0, half):
            hid = (jax.nn.silu(_dot(xg, wg_ref[:, f0:f0 + half])) * _dot(xg, wu_ref[:, f0:f0 + half])).astype(BF16)
            part = _dot(hid, wd_ref[f0:f0 + half, :])
            acc = part if acc is None else acc + part
        o = acc * g
        for i in range(bb):
            o_ref[i, 0] = o[i * cap:(i + 1) * cap].astype(BF16)

    def gather_dense(hits):
        for i in range(bb):
            xg_ref[i * cap:(i + 1) * cap, :] = _dot(jnp.where(hits[i], 1.0, 0.0).astype(BF16), xn_ref[i])

    if cap <= SLOT_WINDOW:
        step(gather_dense)
        return

    nt = n // tile
    win, fits = [], True
    for i in range(bb):
        base = (pl.program_id(1) * bb + i) * nt * stride * N_EXPERTS + pl.program_id(0)
        for t in range(nt):
            start = st_ref[base + t * stride * N_EXPERTS]
            end = st_ref[base + (t + 1) * stride * N_EXPERTS] if t + 1 < nt else cap
            a = jnp.minimum((start >> 4) << 4, cap - SLOT_WINDOW)
            win.append(a)
            fits = jnp.logical_and(fits, end - a <= SLOT_WINDOW)

    def gather_windowed(hits):
        del hits
        xg_ref[...] = jnp.zeros_like(xg_ref)
        wslot = lax.broadcasted_iota(I32, (SLOT_WINDOW, tile), 0)
        for i in range(bb):
            for t in range(nt):
                a = win[i * nt + t]
                p = pos_ref[i, expert, t * tile:(t + 1) * tile]
                onehot = jnp.where(p - a == wslot, 1.0, 0.0).astype(BF16)
                rows = pl.ds(pl.multiple_of(i * cap + a, 16), SLOT_WINDOW)
                xg_ref[rows, :] = xg_ref[rows, :] + _dot(onehot, xn_ref[i, t * tile:(t + 1) * tile, :])

    pl.when(fits)(lambda: step(gather_windowed))
    pl.when(jnp.logical_not(fits))(lambda: step(gather_dense))


def _expert_ffn(xn, pos, gate, starts, wg, wu, wd, layer, cap, bb, tile, stride):
    b, n, d = xn.shape
    assert b % bb == 0 and n % tile == 0, (b, bb, n, tile)
    wspec = pl.BlockSpec((1, 1, d, d), lambda e, j, st: (layer, e, 0, 0))
    sel = pl.BlockSpec((bb, N_EXPERTS, n), lambda e, j, st: (j, 0, 0))
    kern = functools.partial(_ffn_kernel, bb=bb, cap=cap, n=n, tile=tile, stride=stride)
    return pl.pallas_call(
        kern,
        out_shape=jax.ShapeDtypeStruct((b, N_EXPERTS, cap, d), BF16),
        grid_spec=pltpu.PrefetchScalarGridSpec(
            num_scalar_prefetch=1,
            grid=(N_EXPERTS, b // bb),
            in_specs=[pl.BlockSpec((bb, n, d), lambda e, j, st: (j, 0, 0)), sel, sel, wspec, wspec, wspec],
            out_specs=pl.BlockSpec((bb, 1, cap, d), lambda e, j, st: (j, e, 0, 0)),
            scratch_shapes=[pltpu.VMEM((d, d), BF16)] * 3 + [pltpu.VMEM((bb * cap, d), F32)]),
        compiler_params=_params("arbitrary", "arbitrary"),
        name="expert_ffn",
    )(starts, xn, pos, gate, wg, wu, wd)


def _scatter_kernel(st_ref, x_ref, mod_ref, pt_ref, o_ref, out_ref, *, tn, cap, nt, sub, window):
    for s in range(sub):
        _scatter_tile(st_ref, x_ref.at[0, s * tn:(s + 1) * tn], mod_ref, pt_ref.at[0, s * tn:(s + 1) * tn], o_ref,
                      out_ref.at[0, s * tn:(s + 1) * tn], pl.program_id(1) * sub + s,
                      tn=tn, cap=cap, nt=nt, window=window)


def _scatter_tile(st_ref, x_ref, mod_ref, pt_ref, o_ref, out_ref, t, *, tn, cap, nt, window):
    pos_t = pt_ref[...]
    g = mod_ref[0][:, 5 * D_MODEL:6 * D_MODEL]

    def dense():
        if cap % LANES == 0:
            slot = lax.broadcasted_iota(I32, (tn, cap), 1)
            onehot = jnp.concatenate(
                [jnp.where(pos_t[:, e:e + 1] == slot, 1.0, 0.0).astype(BF16) for e in range(N_EXPERTS)], axis=1)
        else:
            slot = lax.broadcasted_iota(I32, (tn, N_EXPERTS * cap), 1)
            acc = jnp.zeros((tn, N_EXPERTS * cap), F32)
            for e in range(N_EXPERTS):
                pe = pos_t[:, e:e + 1]
                acc = jnp.where(jnp.where(pe >= 0, pe + e * cap, -1) == slot, 1.0, acc)
            onehot = acc.astype(BF16)
        y = _dot(onehot, o_ref[0].reshape(N_EXPERTS * cap, D_MODEL))
        out_ref[...] = x_ref[...] + g * y

    if cap <= window:
        dense()
        return

    base = (pl.program_id(0) * nt + t) * N_EXPERTS
    nxt = jnp.minimum(t + 1, nt - 1)
    nbase = (pl.program_id(0) * nt + nxt) * N_EXPERTS
    win, fits = [], True
    for e in range(N_EXPERTS):
        start = st_ref[base + e]
        end = jnp.where(t + 1 < nt, st_ref[nbase + e], cap)
        a = jnp.minimum((start >> 4) << 4, cap - window)
        win.append(a)
        fits = jnp.logical_and(fits, end - a <= window)

    @pl.when(fits)
    def _():
        per = LANES // window
        lane = lax.broadcasted_iota(I32, (tn, LANES), 1)
        groups = []
        for e0 in range(0, N_EXPERTS, per):
            target = jnp.full((tn, LANES), -1, I32)
            for i in range(per):
                rel = pos_t[:, e0 + i:e0 + i + 1] - win[e0 + i]
                here = (lane >> (window.bit_length() - 1)) == i if per > 1 else None
                shifted = jnp.where(rel >= 0, rel + i * window, -1)
                target = shifted if here is None else jnp.where(here, shifted, target)
            groups.append(jnp.where(target == lane, 1.0, 0.0).astype(BF16))
        onehot = jnp.concatenate(groups, axis=1)
        rows = jnp.concatenate(
            [o_ref[0, e, pl.ds(pl.multiple_of(win[e], 16), window), :] for e in range(N_EXPERTS)], axis=0)
        out_ref[...] = x_ref[...] + g * _dot(onehot, rows)

    pl.when(jnp.logical_not(fits))(dense)


def _scatter(x, mod, pos_t, starts, o, cap, tn, window, sub):
    b, n, d = x.shape
    nt = n // tn
    xs = pl.BlockSpec((1, sub * tn, d), lambda i, t, st: (i, t, 0))
    kern = functools.partial(_scatter_kernel, tn=tn, cap=cap, nt=nt, sub=sub, window=window)
    return pl.pallas_call(
        kern,
        out_shape=jax.ShapeDtypeStruct((b, n, d), F32),
        grid_spec=pltpu.PrefetchScalarGridSpec(
            num_scalar_prefetch=1,
            grid=(b, nt // sub),
            in_specs=[xs,
                      pl.BlockSpec((1, 1, 6 * d), lambda i, t, st: (i, 0, 0)),
                      pl.BlockSpec((1, sub * tn, LANES), lambda i, t, st: (i, t, 0)),
                      pl.BlockSpec((1, N_EXPERTS, cap, d), lambda i, t, st: (i, 0, 0, 0))],
            out_specs=xs),
        compiler_params=_params("arbitrary", "arbitrary"),
        name="scatter_add",
    )(starts, x, mod, pos_t, o)


@functools.lru_cache(maxsize=None)
def _np_consts():
    lane = np.arange(GROUP_W)
    seg32 = (lane[:, None] // DIFF_D == lane[None, :] // DIFF_D).astype(np.float32)
    seg64 = (lane[:, None] // HEAD_DIM == lane[None, :] // HEAD_DIM).astype(np.float32)
    ang = 2.0 * np.pi * ((lane[:, None] % HEAD_DIM) * (lane[None, :] % HEAD_DIM) % HEAD_DIM) / HEAD_DIM
    cc = np.cos(ang) * seg64
    ss = np.sin(ang) * seg64
    tri = (lane[:, None] <= lane[None, :]).astype(np.float32)
    return dict(seg32=seg32, seg64=seg64, cc=cc, ss=ss, tri=tri)


@functools.lru_cache(maxsize=None)
def _np_dft(n):
    idx = (np.arange(n, dtype=np.int64)[:, None] * np.arange(n, dtype=np.int64)[None, :]) % n
    ang = 2.0 * np.pi * idx.astype(np.float64) / n
    return np.cos(ang).astype(np.float32), np.sin(ang).astype(np.float32)


@functools.lru_cache(maxsize=None)
def _np_rope(n):
    half = DIFF_D // 2
    inv = 1.0 / (ROPE_BASE ** (np.arange(0, half, 2, dtype=np.float32) / half))
    t = np.arange(n)
    row = (t // GRID_W).astype(np.float32)[:, None] * inv
    col = (t % GRID_W).astype(np.float32)[:, None] * inv
    nf = inv.shape[0]
    d = np.arange(GROUP_W) % DIFF_D
    f = d % nf
    is_col = d >= half
    second = (d % half) >= nf
    ang = np.where(is_col[None, :], col[:, f], row[:, f])
    c = np.cos(ang).astype(np.float32)
    s = np.sin(ang).astype(np.float32)
    s = np.where(second[None, :], s, -s)
    return c, s


@functools.lru_cache(maxsize=None)
def _np_na_index(n_rows):
    n_steps = n_rows // NA_QROWS
    reps = [0, 1, 2, n_steps - 2, n_steps - 1]
    tq = NA_QROWS * GRID_W
    roff = np.zeros((len(reps), NA_QROWS, NA_WIN_ROWS), np.int32)
    valid = np.zeros((len(reps), tq, NA_WIN), bool)
    for ci, t in enumerate(reps):
        ws = int(np.clip(NA_QROWS * t - NA_ROWS // 2, 0, n_rows - NA_WIN_ROWS))
        qi = np.arange(tq)
        r = NA_QROWS * t + qi // GRID_W
        qcol = qi % GRID_W
        kk = np.arange(NA_WIN)
        krow = ws + kk // GRID_W
        kcol = kk % GRID_W
        rstart = np.clip(r - NA_ROWS // 2, 0, n_rows - NA_ROWS)
        wstart = np.clip(qcol - NA_COLS // 2, 0, GRID_W - NA_COLS)
        vr = (krow[None, :] >= rstart[:, None]) & (krow[None, :] < rstart[:, None] + NA_ROWS)
        vc = (kcol[None, :] >= wstart[:, None]) & (kcol[None, :] < wstart[:, None] + NA_COLS)
        valid[ci] = vr & vc
        rows_q = NA_QROWS * t + np.arange(NA_QROWS)
        rows_k = ws + np.arange(NA_WIN_ROWS)
        roff[ci] = np.clip(rows_k[None, :] - rows_q[:, None] + NA_ROWS - 1, 0, 2 * NA_ROWS - 2)
    return roff, valid


def _na_table_kernel(r_ref, mask_ref, o_ref, toep_ref, *, roff):
    n_off = 2 * NA_ROWS - 1
    for h in range(HEADS):
        for ro in range(n_off):
            row = jnp.broadcast_to(r_ref[0, h, ro:ro + 1, :], (GRID_W, LANES))
            toep_ref[h, ro] = pltpu.roll(row, 0, 1, stride=1, stride_axis=0)
    lane = lax.broadcasted_iota(I32, (GRID_W, LANES), 1)
    n_cls = roff.shape[0]
    for ci in range(n_cls):
        for h in range(HEADS):
            for rr in range(NA_QROWS):
                r0 = h * NA_QROWS * GRID_W + rr * GRID_W
                for kp in range(NA_WIN_ROWS // 2):
                    left = toep_ref[h, int(roff[ci, rr, 2 * kp])]
                    right = pltpu.roll(toep_ref[h, int(roff[ci, rr, 2 * kp + 1])], GRID_W, 1)
                    bias = jnp.where(lane < GRID_W, left, right) * LOG2E
                    o_ref[0, ci, r0:r0 + GRID_W, kp * LANES:(kp + 1) * LANES] = (
                        bias + mask_ref[ci, rr * GRID_W:(rr + 1) * GRID_W, kp * LANES:(kp + 1) * LANES])


def _na_tables(rpb_all, n_rows):
    roff, valid = _np_na_index(n_rows)
    n_cls = roff.shape[0]
    depth = rpb_all.shape[0]
    n_off = 2 * NA_ROWS - 1
    r = jnp.concatenate([rpb_all[..., NA_COLS - 1:], jnp.zeros(rpb_all.shape[:-1] + (LANES - 2 * NA_COLS + 1,), F32),
                         rpb_all[..., :NA_COLS - 1]], axis=-1).astype(F32)
    mask = jnp.asarray(np.where(valid, 0.0, NEG_INF).astype(np.float32))
    tq = NA_QROWS * GRID_W
    kern = functools.partial(_na_table_kernel, roff=roff)
    return pl.pallas_call(
        kern,
        out_shape=jax.ShapeDtypeStruct((depth, n_cls, HEADS * tq, NA_WIN), F32),
        grid=(depth,),
        in_specs=[pl.BlockSpec((1, HEADS, n_off, LANES), lambda l: (l, 0, 0, 0)),
                  _full((n_cls, tq, NA_WIN))],
        out_specs=pl.BlockSpec((1, n_cls, HEADS * tq, NA_WIN), lambda l: (l, 0, 0, 0)),
        scratch_shapes=[pltpu.VMEM((HEADS, n_off, GRID_W, LANES), F32)],
        compiler_params=_params("arbitrary"),
        name="na_bias_table",
    )(r, mask)


class _Tiles(NamedTuple):
    proj: int
    dft: int
    attn: int
    na_pairs: int
    moe: int
    ffn_samples: int


def _latent_tiles(n):
    return _Tiles(proj=min(n, 1024), dft=min(n, 2048), attn=min(n, 512), na_pairs=min(8, n // (NA_QROWS * GRID_W)),
                  moe=min(n, 512), ffn_samples=2)
def _moe(x, xn, logits, mod, tri, wg, wu, wd, layer, bb, tn):
    n = x.shape[1]
    cap = CAPACITY_FACTOR * n // N_EXPERTS
    stride = 2 if cap > SLOT_WINDOW else 1
    ts = tn // stride
    pos, gate, pos_t, starts = _route(logits, tri, cap, ts)
    starts = starts[:, :, :n // ts].transpose(0, 2, 1).reshape(-1)
    o = _expert_ffn(xn, pos, gate, starts, wg, wu, wd, layer, cap, bb, tn, stride)
    return _scatter(x, mod, pos_t, starts, o, cap, ts, SLOT_WINDOW // stride, min(8, n // ts))


def kernel(x, c, ctx, c_ctx, ada_w, ada_b, norm1_g, norm2_g, w_in, w_out, head_out_g, sgu_w, sgu_b, diff_qn_g, diff_kn_g, diff_lambda, na_qn_g, na_kn_g, na_rpb, router_w, exp_w_gate, exp_w_up, exp_w_down):
    b, n, d = x.shape
    n_ctx = ctx.shape[1]
    npc = _np_consts()
    consts = {k: jnp.asarray(v, F32).astype(BF16) for k, v in npc.items()}
    seg64, tri = consts["seg64"], consts["tri"]
    cn, sn = (jnp.asarray(a, F32).astype(BF16) for a in _np_dft(n))
    cn_c, sn_c = (jnp.asarray(a, F32).astype(BF16) for a in _np_dft(n_ctx))
    rope = tuple(jnp.asarray(a, F32) for a in _np_rope(n))

    pad = (-(b + 1)) % 8
    c_rows = jnp.concatenate([c, c_ctx[None, :], jnp.zeros((pad, d), F32)], axis=0)
    mod_all = _modulation(c_rows, ada_w, ada_b)
    na_tables = _na_tables(na_rpb, n // GRID_W)

    xc = ctx
    for l in range(DEPTH):
        last = l == DEPTH - 1
        lam_init = 0.8 - 0.6 * math.exp(-0.3 * l)
        mod = mod_all[l, :b][:, None, :]
        mod_c = jnp.broadcast_to(mod_all[l, b][None, None, :], (b, 1, 6 * d))
        g1 = norm1_g[l][None, :]
        g2 = norm2_g[l][None, :]
        w_in_l = w_in[l].astype(BF16)
        w_out_l = w_out[l].astype(BF16)
        sguw = sgu_w[l].astype(BF16).transpose(1, 0, 2).reshape(CHUNK, HEADS * CHUNK)
        sgub = jnp.repeat(sgu_b[l].T, HEAD_DIM, axis=1)
        hg = head_out_g[l].reshape(4, GROUP_W)
        vec = jnp.stack([jnp.tile(diff_qn_g[l], GROUP_W // DIFF_D), jnp.tile(diff_kn_g[l], GROUP_W // DIFF_D),
                         jnp.tile(na_qn_g[l], HEADS), jnp.tile(na_kn_g[l], HEADS),
                         hg[0], hg[1], hg[2], hg[3],
                         jnp.broadcast_to(jnp.max(jnp.abs(na_rpb[l])) * LOG2E, (GROUP_W,))], axis=0).astype(F32)
        vec = jnp.pad(vec, ((0, VEC_ROWS - vec.shape[0]), (0, 0)))
        lam = diff_lambda[l].astype(F32)
        rwt = router_w[l].T.astype(BF16)
        experts = (exp_w_gate, exp_w_up, exp_w_down)

        tl = _latent_tiles(n)
        ya, zc, zs, qc, kct, vc, qd, kdt, vd = _in_proj(x, mod, g1, w_in_l, consts, sguw, sgub, vec, rope, tl.proj)
        if last:
            kct_c, vc_c, kdt_c, vd_c = _in_proj(
                xc, mod_c, g1, w_in_l, consts, sguw, sgub, vec, None, n_ctx, kv_only=True)
        else:
            ya_c, zc_c, zs_c, qc_c, kct_c, vc_c, qd_c, kdt_c, vd_c = _in_proj(
                xc, mod_c, g1, w_in_l, consts, sguw, sgub, vec, None, n_ctx)

        yb = _fourier(zc, zs, cn, sn, seg64, vec, tl.dft)
        yc = _attention(qc, [(kct, vc), (kct_c, vc_c)], lam, lam_init, seg64, vec, 6, 1.0 - lam_init,
                        tl.attn, tl.attn)
        yd = _na_attention(qd, kdt, vd, kdt_c, vd_c, na_tables, l, seg64, vec, tl.na_pairs)
        x, xn, logits = _out_proj(ya, yb, yc, yd, w_out_l, x, mod, g2, rwt, tl.proj)
        x = _moe(x, xn, logits, mod, tri, *experts, l, tl.ffn_samples, tl.moe)

        if not last:
            yb_c = _fourier(zc_c, zs_c, cn_c, sn_c, seg64, vec, n_ctx)
            yc_c = _attention(qc_c, [(kct_c, vc_c)], lam, lam_init, seg64, vec, 6, 1.0 - lam_init, n_ctx)
            yd_c = _attention(qd_c, [(kdt_c, vd_c)], None, 0.0, seg64, vec, 7, 1.0, n_ctx)
            xc, xn_c, logits_c = _out_proj(ya_c, yb_c, yc_c, yd_c, w_out_l, xc, mod_c, g2, rwt, n_ctx)
            xc = _moe(xc, xn_c, logits_c, mod_c, tri, *experts, l, b, n_ctx)
    return x
```
